```python
import functools
import math
import jax
import jax.numpy as jnp
from jax import lax
import numpy as np

D_MODEL = 1024
BATCH = 4
SEQ = 4096
DEPTH = 1
DEC_BATCH = 16
DEC_SEQ = 64
PAST_LEN = 4096

CHUNK = 64
N_FOX_HEADS = 8
FOX_HEAD_DIM = 64
FOX_WIDTH = N_FOX_HEADS * FOX_HEAD_DIM
Q_BLOCK = 128
N_MEM = 256
N_MEM_HEADS = 4
MEM_HEAD_DIM = 128
MEM_WIDTH = N_MEM_HEADS * MEM_HEAD_DIM
SSM_GROUP = 16
SSM_WIDTH = 512
N_SSM_GROUPS = SSM_WIDTH // SSM_GROUP
SSM_STATE = 64
N_BRANCHES = 3
IN_SPLITS = (FOX_WIDTH, FOX_WIDTH, FOX_WIDTH, N_FOX_HEADS, MEM_WIDTH, SSM_WIDTH, D_MODEL, D_MODEL, D_MODEL)
IN_WIDTH = 3 * FOX_WIDTH + N_FOX_HEADS + MEM_WIDTH + SSM_WIDTH + N_BRANCHES * D_MODEL
N_EXPERT_GROUPS = 4
EXPERTS_PER_GROUP = 8
N_EXPERTS = N_EXPERT_GROUPS * EXPERTS_PER_GROUP
TOP_K_IN_GROUP = 2
D_EXPERT = 256
RMS_EPS = 1e-6
NEG_INF = -1e30

kernel_name = 'hybrid_fox_s5_hmoe_stream_step'


def _rmsnorm(x, g):
    xf = x.astype(jnp.float32)
    y = xf * lax.rsqrt(jnp.mean(xf * xf, axis=-1, keepdims=True) + RMS_EPS)
    return (y * g.astype(jnp.float32)).astype(x.dtype)


def _split_last(z, sizes):
    offsets = []
    acc = 0
    for s in sizes[:-1]:
        acc += s
        offsets.append(acc)
    return jnp.split(z, offsets, axis=-1)


def _mixer_inputs(x, norm_mix, w_in, b_forget, qn_fox, kn_fox, qn_mem):
    bsz, s, _ = x.shape
    z = _rmsnorm(x, norm_mix) @ w_in
    q_f, k_f, v_f, f_lg, q_m, u, g_f, g_s, g_m = _split_last(z, IN_SPLITS)
    q_f = _rmsnorm(q_f.reshape(bsz, s, N_FOX_HEADS, FOX_HEAD_DIM), qn_fox)
    k_f = _rmsnorm(k_f.reshape(bsz, s, N_FOX_HEADS, FOX_HEAD_DIM), kn_fox)
    v_f = v_f.reshape(bsz, s, N_FOX_HEADS, FOX_HEAD_DIM)
    logf = jax.nn.log_sigmoid((f_lg + b_forget).astype(jnp.float32))
    q_m = _rmsnorm(q_m.reshape(bsz, s, N_MEM_HEADS, MEM_HEAD_DIM), qn_mem)
    return (q_f, k_f, v_f, logf, q_m, u,
            jax.nn.sigmoid(g_f), jax.nn.sigmoid(g_s), jax.nn.sigmoid(g_m))


def _fox_block(q, c_q, q_pos, k, v, c_k, k_pos):
    s = jnp.einsum('bqhd,bkhd->bhqk', q.astype(jnp.float32), k.astype(jnp.float32)) * (FOX_HEAD_DIM ** -0.5)
    bias = jnp.swapaxes(c_q, 1, 2)[:, :, :, None] - jnp.swapaxes(c_k, 1, 2)[:, :, None, :]
    mask = k_pos[None, :] <= q_pos[:, None]
    s = jnp.where(mask, s + bias, NEG_INF)
    p = jax.nn.softmax(s, axis=-1)
    return jnp.einsum('bhqk,bkhd->bqhd', p, v.astype(jnp.float32)).astype(v.dtype)


def _fox_prompt(q, k, v, logf):
    bsz, s, h, dh = q.shape
    nblk = s // Q_BLOCK
    c = jnp.cumsum(logf, axis=1)
    pos = jnp.arange(s)
    qb = jnp.swapaxes(q.reshape(bsz, nblk, Q_BLOCK, h, dh), 0, 1)
    cb = jnp.swapaxes(c.reshape(bsz, nblk, Q_BLOCK, h), 0, 1)
    pb = pos.reshape(nblk, Q_BLOCK)
    o = lax.map(lambda blk: _fox_block(blk[0], blk[1], blk[2], k, v, c, pos), (qb, cb, pb))
    return jnp.swapaxes(o, 0, 1).reshape(bsz, s, h * dh)


def _fox_sample(cache_k, cache_v, cache_logf, q, k_new, v_new, logf_new):
    bsz, n, h, dh = q.shape
    past = cache_k.shape[1]
    k_all = jnp.concatenate([cache_k, k_new.astype(cache_k.dtype)], axis=1)
    v_all = jnp.concatenate([cache_v, v_new.astype(cache_v.dtype)], axis=1)
    c_all = jnp.cumsum(jnp.concatenate([cache_logf.astype(jnp.float32), logf_new], axis=1), axis=1)
    k_pos = jnp.arange(past + n)
    q_pos = past + jnp.arange(n)
    o = _fox_block(q, c_all[:, past:], q_pos, k_all, v_all, c_all, k_pos)
    return o.reshape(bsz, n, h * dh)


def _complex_affine_combine(earlier, later):
    a1r, a1i, b1r, b1i = earlier
    a2r, a2i, b2r, b2i = later
    return (a2r * a1r - a2i * a1i,
            a2r * a1i + a2i * a1r,
            a2r * b1r - a2i * b1i + b2r,
            a2r * b1i + a2i * b1r + b2i)


def _ssm_branch(u, h0_re, h0_im, a_re, a_im, log_dt, b_re, b_im, c_re, c_im, d, w_glu):
    bsz, s, _ = u.shape
    f32 = jnp.float32
    a_re, a_im, b_re, b_im = a_re.astype(f32), a_im.astype(f32), b_re.astype(f32), b_im.astype(f32)
    uf = u.astype(f32)
    ug = uf.reshape(bsz, s, N_SSM_GROUPS, SSM_GROUP)
    dt = jnp.exp(log_dt.astype(f32))[:, None]
    mag = jnp.exp(dt * a_re)
    ab_re = mag * jnp.cos(dt * a_im)
    ab_im = mag * jnp.sin(dt * a_im)
    den = a_re * a_re + a_im * a_im
    nr, ni = ab_re - 1.0, ab_im
    coef_re = (nr * a_re + ni * a_im) / den
    coef_im = (ni * a_re - nr * a_im) / den
    bb_re = coef_re[..., None] * b_re - coef_im[..., None] * b_im
    bb_im = coef_re[..., None] * b_im + coef_im[..., None] * b_re
    bu_re = jnp.einsum('bsgh,gph->bsgp', ug, bb_re)
    bu_im = jnp.einsum('bsgh,gph->bsgp', ug, bb_im)
    bu_re = bu_re.at[:, 0].add(ab_re * h0_re - ab_im * h0_im)
    bu_im = bu_im.at[:, 0].add(ab_re * h0_im + ab_im * h0_re)
    a_r = jnp.broadcast_to(ab_re, bu_re.shape)
    a_i = jnp.broadcast_to(ab_im, bu_im.shape)
    _, _, h_re, h_im = lax.associative_scan(_complex_affine_combine, (a_r, a_i, bu_re, bu_im), axis=1)
    y = (jnp.einsum('bsgp,ghp->bsgh', h_re, c_re.astype(f32))
         - jnp.einsum('bsgp,ghp->bsgh', h_im, c_im.astype(f32)))
    y = y.reshape(bsz, s, SSM_WIDTH) + d.astype(f32) * uf
    z = jax.nn.gelu(y) @ w_glu.astype(f32)
    za, zg = jnp.split(z, 2, axis=-1)
    return (za * jax.nn.sigmoid(zg)).astype(u.dtype), h_re[:, -1], h_im[:, -1]


def _mem_kv(mem, norm_mem, w_mem_kv, kn_mem):
    bsz, n, _ = mem.shape
    kv = _rmsnorm(mem, norm_mem) @ w_mem_kv
    k, v = jnp.split(kv, 2, axis=-1)
    k = _rmsnorm(k.reshape(bsz, n, N_MEM_HEADS, MEM_HEAD_DIM), kn_mem)
    return k, v.reshape(bsz, n, N_MEM_HEADS, MEM_HEAD_DIM)


def _mem_attend(q, k, v):
    bsz, s = q.shape[0], q.shape[1]
    sc = jnp.einsum('bqhd,bkhd->bhqk', q.astype(jnp.float32), k.astype(jnp.float32)) * (MEM_HEAD_DIM ** -0.5)
    p = jax.nn.softmax(sc, axis=-1)
    o = jnp.einsum('bhqk,bkhd->bqhd', p, v.astype(jnp.float32))
    return o.reshape(bsz, s, MEM_WIDTH).astype(q.dtype)


def _hmoe(x, norm_ffn, w_router_group, w_router_expert, w_gate, w_up, w_down):
    h = _rmsnorm(x, norm_ffn)
    g_logits = (h @ w_router_group).astype(jnp.float32)
    g_prob = jax.nn.softmax(g_logits, axis=-1)
    grp = jnp.argmax(g_logits, axis=-1)
    g_w = jnp.max(g_prob, axis=-1, keepdims=True)
    e_logits = (h @ w_router_expert).astype(jnp.float32)
    e_logits = e_logits.reshape(x.shape[0], x.shape[1], N_EXPERT_GROUPS, EXPERTS_PER_GROUP)
    e_sel = jnp.einsum('bsge,bsg->bse', e_logits, jax.nn.one_hot(grp, N_EXPERT_GROUPS, dtype=jnp.float32))
    top_v, top_i = lax.top_k(e_sel, TOP_K_IN_GROUP)
    w_sel = jax.nn.softmax(top_v, axis=-1) * g_w
    expert_id = grp[..., None] * EXPERTS_PER_GROUP + top_i
    combine = jnp.sum(jax.nn.one_hot(expert_id, N_EXPERTS, dtype=jnp.float32) * w_sel[..., None], axis=-2)
    a = jnp.einsum('bsd,edf->bsef', h, w_gate)
    up = jnp.einsum('bsd,edf->bsef', h, w_up)
    act = jax.nn.silu(a) * up * combine[..., None].astype(h.dtype)
    return x + jnp.einsum('bsef,efd->bsd', act, w_down)


def _layer(x, attend_fox, h0_re, h0_im, mem_k, mem_v, p):
    q_f, k_f, v_f, logf, q_m, u, g_f, g_s, g_m = _mixer_inputs(
        x, p['norm_mix'], p['w_in'], p['b_forget'], p['qn_fox'], p['kn_fox'], p['qn_mem'])
    o_fox = attend_fox(q_f, k_f, v_f, logf)
    y_ssm, h_re, h_im = _ssm_branch(u, h0_re, h0_im, p['ssm_a_re'], p['ssm_a_im'], p['ssm_log_dt'],
                                    p['ssm_b_re'], p['ssm_b_im'], p['ssm_c_re'], p['ssm_c_im'],
                                    p['ssm_d'], p['w_glu'])
    o_mem = _mem_attend(q_m, mem_k, mem_v)
    merged = (g_f * (o_fox @ p['w_br_fox']) + g_s * (y_ssm @ p['w_br_ssm'])
              + g_m * (o_mem @ p['w_br_mem']))
    x = x + merged @ p['w_out']
    x = _hmoe(x, p['norm_ffn'], p['w_router_group'], p['w_router_expert'],
              p['moe_w_gate'], p['moe_w_up'], p['moe_w_down'])
    return x, k_f, v_f, logf, h_re, h_im


def setup_inputs(seed: int = 0) -> dict:
    key = jax.random.key(seed)
    it = iter(list(jax.random.split(key, 48)))
    f32 = jnp.float32

    def nrm(shape, scale=1.0):
        return scale * jax.random.normal(next(it), shape, f32)

    L = DEPTH
    G, P, H = N_SSM_GROUPS, SSM_STATE, SSM_GROUP
    return {
        'x_prompt': nrm((BATCH, SEQ, D_MODEL)),
        'x_sample': nrm((DEC_BATCH, DEC_SEQ, D_MODEL)),
        'mem_prompt': nrm((BATCH, N_MEM, D_MODEL)),
        'cache_fox_k': nrm((L, DEC_BATCH, PAST_LEN, N_FOX_HEADS, FOX_HEAD_DIM)),
        'cache_fox_v': nrm((L, DEC_BATCH, PAST_LEN, N_FOX_HEADS, FOX_HEAD_DIM)),
        'cache_fox_logf': jax.nn.log_sigmoid(4.0 + nrm((L, DEC_BATCH, PAST_LEN, N_FOX_HEADS))),
        'state_ssm_re': nrm((L, DEC_BATCH, G, P), 0.5),
        'state_ssm_im': nrm((L, DEC_BATCH, G, P), 0.5),
        'cache_mem_k': nrm((L, DEC_BATCH, N_MEM, N_MEM_HEADS, MEM_HEAD_DIM)),
        'cache_mem_v': nrm((L, DEC_BATCH, N_MEM, N_MEM_HEADS, MEM_HEAD_DIM)),
        'norm_mix': 1.0 + nrm((L, D_MODEL), 0.02),
        'w_in': nrm((L, D_MODEL, IN_WIDTH), D_MODEL ** -0.5),
        'b_forget': 4.0 + nrm((L, N_FOX_HEADS), 0.5),
        'qn_fox': 1.0 + nrm((L, FOX_HEAD_DIM), 0.02),
        'kn_fox': 1.0 + nrm((L, FOX_HEAD_DIM), 0.02),
        'qn_mem': 1.0 + nrm((L, MEM_HEAD_DIM), 0.02),
        'kn_mem': 1.0 + nrm((L, MEM_HEAD_DIM), 0.02),
        'norm_mem': 1.0 + nrm((L, D_MODEL), 0.02),
        'w_mem_kv': nrm((L, D_MODEL, 2 * MEM_WIDTH), D_MODEL ** -0.5),
        'ssm_a_re': -0.5 + nrm((L, G, P), 0.01),
        'ssm_a_im': math.pi * jnp.arange(P, dtype=f32) + nrm((L, G, P), 0.01),
        'ssm_log_dt': jax.random.uniform(next(it), (L, G), f32, math.log(1e-3), math.log(1e-1)),
        'ssm_b_re': nrm((L, G, P, H), (2 * H) ** -0.5),
        'ssm_b_im': nrm((L, G, P, H), (2 * H) ** -0.5),
        'ssm_c_re': nrm((L, G, H, P), P ** -0.5),
        'ssm_c_im': nrm((L, G, H, P), P ** -0.5),
        'ssm_d': nrm((L, SSM_WIDTH), 0.5),
        'w_glu': nrm((L, SSM_WIDTH, 2 * SSM_WIDTH), SSM_WIDTH ** -0.5),
        'w_br_fox': nrm((L, FOX_WIDTH, D_MODEL), FOX_WIDTH ** -0.5),
        'w_br_ssm': nrm((L, SSM_WIDTH, D_MODEL), SSM_WIDTH ** -0.5),
        'w_br_mem': nrm((L, MEM_WIDTH, D_MODEL), MEM_WIDTH ** -0.5),
        'w_out': nrm((L, D_MODEL, D_MODEL), D_MODEL ** -0.5),
        'norm_ffn': 1.0 + nrm((L, D_MODEL), 0.02),
        'w_router_group': nrm((L, D_MODEL, N_EXPERT_GROUPS), D_MODEL ** -0.5),
        'w_router_expert': nrm((L, D_MODEL, N_EXPERTS), D_MODEL ** -0.5),
        'moe_w_gate': nrm((L, N_EXPERTS, D_MODEL, D_EXPERT), D_MODEL ** -0.5),
        'moe_w_up': nrm((L, N_EXPERTS, D_MODEL, D_EXPERT), D_MODEL ** -0.5),
        'moe_w_down': nrm((L, N_EXPERTS, D_EXPERT, D_MODEL), D_EXPERT ** -0.5),
    }


def reference(x_prompt, x_sample, mem_prompt, cache_fox_k, cache_fox_v, cache_fox_logf,
              state_ssm_re, state_ssm_im, cache_mem_k, cache_mem_v,
              norm_mix, w_in, b_forget, qn_fox, kn_fox, qn_mem, kn_mem, norm_mem, w_mem_kv,
              ssm_a_re, ssm_a_im, ssm_log_dt, ssm_b_re, ssm_b_im, ssm_c_re, ssm_c_im, ssm_d, w_glu,
              w_br_fox, w_br_ssm, w_br_mem, w_out, norm_ffn, w_router_group, w_router_expert,
              moe_w_gate, moe_w_up, moe_w_down):
    p_k, p_v, p_lf, p_re, p_im, p_mk, p_mv = [], [], [], [], [], [], []
    s_k, s_v, s_lf, s_re, s_im = [], [], [], [], []
    xp, xs = x_prompt, x_sample
    for l in range(DEPTH):
        p = dict(norm_mix=norm_mix[l], w_in=w_in[l], b_forget=b_forget[l], qn_fox=qn_fox[l],
                 kn_fox=kn_fox[l], qn_mem=qn_mem[l], ssm_a_re=ssm_a_re[l], ssm_a_im=ssm_a_im[l],
                 ssm_log_dt=ssm_log_dt[l], ssm_b_re=ssm_b_re[l], ssm_b_im=ssm_b_im[l],
                 ssm_c_re=ssm_c_re[l], ssm_c_im=ssm_c_im[l], ssm_d=ssm_d[l], w_glu=w_glu[l],
                 w_br_fox=w_br_fox[l], w_br_ssm=w_br_ssm[l], w_br_mem=w_br_mem[l], w_out=w_out[l],
                 norm_ffn=norm_ffn[l], w_router_group=w_router_group[l],
                 w_router_expert=w_router_expert[l], moe_w_gate=moe_w_gate[l],
                 moe_w_up=moe_w_up[l], moe_w_down=moe_w_down[l])
        mk, mv = _mem_kv(mem_prompt, norm_mem[l], w_mem_kv[l], kn_mem[l])
        h0 = jnp.zeros((xp.shape[0], N_SSM_GROUPS, SSM_STATE), jnp.float32)
        xp, k_f, v_f, lf, h_re, h_im = _layer(xp, _fox_prompt, h0, h0, mk, mv, p)
        p_k.append(k_f); p_v.append(v_f); p_lf.append(lf)
        p_re.append(h_re); p_im.append(h_im); p_mk.append(mk); p_mv.append(mv)
        fox_cached = functools.partial(_fox_sample, cache_fox_k[l], cache_fox_v[l], cache_fox_logf[l])
        xs, k_f, v_f, lf, h_re, h_im = _layer(xs, fox_cached, state_ssm_re[l].astype(jnp.float32),
                                              state_ssm_im[l].astype(jnp.float32),
                                              cache_mem_k[l], cache_mem_v[l], p)
        s_k.append(k_f); s_v.append(v_f); s_lf.append(lf); s_re.append(h_re); s_im.append(h_im)
    prompt_fox_k = jnp.stack(p_k)
    prompt_fox_v = jnp.stack(p_v)
    prompt_fox_logf = jnp.stack(p_lf)
    prompt_ssm_re = jnp.stack(p_re)
    prompt_ssm_im = jnp.stack(p_im)
    prompt_mem_k = jnp.stack(p_mk)
    prompt_mem_v = jnp.stack(p_mv)
    sample_fox_k = jnp.stack(s_k)
    sample_fox_v = jnp.stack(s_v)
    sample_fox_logf = jnp.stack(s_lf)
    sample_ssm_re = jnp.stack(s_re)
    sample_ssm_im = jnp.stack(s_im)
    return (xp, xs, prompt_fox_k, prompt_fox_v, prompt_fox_logf, prompt_ssm_re, prompt_ssm_im,
            prompt_mem_k, prompt_mem_v, sample_fox_k, sample_fox_v, sample_fox_logf,
            sample_ssm_re, sample_ssm_im)
```

```python
import functools
import math

import jax
import jax.numpy as jnp
from jax import lax
from jax.experimental import pallas as pl
from jax.experimental.pallas import tpu as pltpu

F32 = jnp.float32
BF16 = jnp.bfloat16

D_MODEL = 1024
N_FOX_HEADS = 8
FOX_HEAD_DIM = 64
FOX_WIDTH = N_FOX_HEADS * FOX_HEAD_DIM
N_MEM = 256
N_MEM_HEADS = 4
MEM_HEAD_DIM = 128
MEM_WIDTH = N_MEM_HEADS * MEM_HEAD_DIM
SSM_GROUP = 16
SSM_WIDTH = 512
N_SSM_GROUPS = SSM_WIDTH // SSM_GROUP
SSM_STATE = 64
N_EXPERT_GROUPS = 4
EXPERTS_PER_GROUP = 8
N_EXPERTS = N_EXPERT_GROUPS * EXPERTS_PER_GROUP
D_EXPERT = 256
RMS_EPS = 1e-6
NEG_INF = -1e30

LANES = 128
SSM_CHUNK = 16
SSM_FEAT = SSM_CHUNK * SSM_GROUP
VMEM_LIMIT = 56 * 1024 * 1024


def _dot(a, b):
    return jnp.dot(a, b, preferred_element_type=F32)


def _dot_nt(a, b):
    return lax.dot_general(a, b, (((1,), (1,)), ((), ())), preferred_element_type=F32)


def _dot_exact(a, b):
    return jnp.dot(a, b, preferred_element_type=F32, precision=lax.Precision.HIGHEST)


def _split_bf16(x):
    hi = x.astype(BF16)
    lo = (x - hi.astype(F32)).astype(BF16)
    return hi, lo


def _params(sem):
    return pltpu.CompilerParams(dimension_semantics=sem, vmem_limit_bytes=VMEM_LIMIT)


def _full(shape):
    n = len(shape)
    return pl.BlockSpec(shape, lambda *_: (0,) * n)


def _inproj_kernel(x_ref, g_ref, wqkv_ref, wf_ref, bf_ref, wqm_ref, wu_ref, wg_ref,
                   qn_ref, kn_ref, qmn_ref, bd_ref,
                   q_ref, k_ref, v_ref, lf_ref, qm_ref, u_ref, gate_ref):
    x = x_ref[...]
    h = x * lax.rsqrt(jnp.mean(x * x, axis=-1, keepdims=True) + RMS_EPS) * g_ref[...]
    hb = h.astype(BF16)

    def head_norm(z, gain):
        hi, lo = _split_bf16(z * z)
        ss = _dot(hi, bd_ref[...]) + _dot(lo, bd_ref[...])
        return z * lax.rsqrt(ss * (1.0 / FOX_HEAD_DIM) + RMS_EPS) * gain

    zq = _dot(hb, wqkv_ref[:, 0:FOX_WIDTH])
    q_ref[...] = (head_norm(zq, qn_ref[...]) * (FOX_HEAD_DIM ** -0.5)).astype(BF16)
    zk = _dot(hb, wqkv_ref[:, FOX_WIDTH:2 * FOX_WIDTH])
    k_ref[...] = head_norm(zk, kn_ref[...])
    v_ref[...] = _dot(hb, wqkv_ref[:, 2 * FOX_WIDTH:3 * FOX_WIDTH])

    zf = _dot(hb, wf_ref[...]) + bf_ref[...]
    logf = jnp.minimum(zf, 0.0) - jnp.log1p(jnp.exp(-jnp.abs(zf)))
    lf_ref[...] = logf[:, 0:N_FOX_HEADS]

    zm = _dot(hb, wqm_ref[...])
    for hd in range(N_MEM_HEADS):
        sl = slice(hd * MEM_HEAD_DIM, (hd + 1) * MEM_HEAD_DIM)
        zh = zm[:, sl]
        ms = jnp.mean(zh * zh, axis=-1, keepdims=True)
        qm_ref[:, sl] = (zh * lax.rsqrt(ms + RMS_EPS) * qmn_ref[...] * (MEM_HEAD_DIM ** -0.5)).astype(BF16)

    u_ref[...] = _dot(hb, wu_ref[...])
    for c in range(3):
        sl = slice(c * D_MODEL, (c + 1) * D_MODEL)
        gate_ref[:, sl] = jax.nn.sigmoid(_dot(hb, wg_ref[:, sl])).astype(BF16)


def _inproj(x2d, w, tm):
    n = x2d.shape[0]
    assert n % tm == 0
    row = lambda width: pl.BlockSpec((tm, width), lambda i: (i, 0))
    ins = [x2d, w['norm_mix'], w['wqkv'], w['wf'], w['bf'], w['wqm'], w['wu'], w['wg'],
           w['qn_fox'], w['kn_fox'], w['qn_mem'], w['bd']]
    in_specs = [row(D_MODEL)] + [_full(a.shape) for a in ins[1:]]
    out_shape = (
        jax.ShapeDtypeStruct((n, FOX_WIDTH), BF16),
        jax.ShapeDtypeStruct((n, FOX_WIDTH), F32),
        jax.ShapeDtypeStruct((n, FOX_WIDTH), F32),
        jax.ShapeDtypeStruct((n, N_FOX_HEADS), F32),
        jax.ShapeDtypeStruct((n, MEM_WIDTH), BF16),
        jax.ShapeDtypeStruct((n, SSM_WIDTH), F32),
        jax.ShapeDtypeStruct((n, 3 * D_MODEL), BF16),
    )
    out_specs = (row(FOX_WIDTH), row(FOX_WIDTH), row(FOX_WIDTH), row(N_FOX_HEADS),
                 row(MEM_WIDTH), row(SSM_WIDTH), row(3 * D_MODEL))
    return pl.pallas_call(
        _inproj_kernel, out_shape=out_shape, grid=(n // tm,), in_specs=in_specs, out_specs=out_specs,
        compiler_params=_params(("parallel",)), name="inproj")(*ins)


def _memkv_kernel(x_ref, g_ref, w_ref, kn_ref, k_ref, v_ref):
    x = x_ref[...]
    h = x * lax.rsqrt(jnp.mean(x * x, axis=-1, keepdims=True) + RMS_EPS) * g_ref[...]
    hb = h.astype(BF16)
    zk = _dot(hb, w_ref[:, 0:MEM_WIDTH])
    for hd in range(N_MEM_HEADS):
        sl = slice(hd * MEM_HEAD_DIM, (hd + 1) * MEM_HEAD_DIM)
        zh = zk[:, sl]
        ms = jnp.mean(zh * zh, axis=-1, keepdims=True)
        k_ref[:, sl] = zh * lax.rsqrt(ms + RMS_EPS) * kn_ref[...]
    v_ref[...] = _dot(hb, w_ref[:, MEM_WIDTH:2 * MEM_WIDTH])


def _memkv(mem2d, norm_mem, w_kv, kn_mem, tm):
    n = mem2d.shape[0]
    row = lambda width: pl.BlockSpec((tm, width), lambda i: (i, 0))
    return pl.pallas_call(
        _memkv_kernel,
        out_shape=(jax.ShapeDtypeStruct((n, MEM_WIDTH), F32), jax.ShapeDtypeStruct((n, MEM_WIDTH), F32)),
        grid=(n // tm,),
        in_specs=[row(D_MODEL), _full(norm_mem.shape), _full(w_kv.shape), _full(kn_mem.shape)],
        out_specs=(row(MEM_WIDTH), row(MEM_WIDTH)),
        compiler_params=_params(("parallel",)), name="memkv")(mem2d, norm_mem, w_kv, kn_mem)


CUMSUM_BLOCK = 64


def _cumsum_kernel(x_ref, o_ref):
    nblk = x_ref.shape[1] // CUMSUM_BLOCK
    r = lax.broadcasted_iota(jnp.int32, (CUMSUM_BLOCK, CUMSUM_BLOCK), 0)
    c = lax.broadcasted_iota(jnp.int32, (CUMSUM_BLOCK, CUMSUM_BLOCK), 1)
    tri = (c <= r).astype(F32)

    def body(j, carry):
        s = pl.multiple_of(j * CUMSUM_BLOCK, CUMSUM_BLOCK)
        blk = x_ref[0, pl.ds(s, CUMSUM_BLOCK), :]
        cs = _dot_exact(tri, blk) + carry
        o_ref[0, pl.ds(s, CUMSUM_BLOCK), :] = cs
        return cs[CUMSUM_BLOCK - 1:CUMSUM_BLOCK, :]

    lax.fori_loop(0, nblk, body, jnp.zeros((1, x_ref.shape[2]), F32))


def _cumsum(x):
    b, s, h = x.shape
    assert s % CUMSUM_BLOCK == 0
    spec = pl.BlockSpec((1, s, h), lambda i: (i, 0, 0))
    return pl.pallas_call(
        _cumsum_kernel, out_shape=jax.ShapeDtypeStruct(x.shape, F32), grid=(b,),
        in_specs=[spec], out_specs=spec, compiler_params=_params(("parallel",)), name="cumsum")(x)


def _head_lane_masks(rows):
    lane = lax.broadcasted_iota(jnp.int32, (rows, LANES), 1)
    return lane < FOX_HEAD_DIM


def _softmax_step(qh, kb, vb, ck, cq, m, l, acc, mask):
    t = _dot_nt(qh, kb) - ck
    if mask is not None:
        t = jnp.where(mask, t, NEG_INF)
    m_new = jnp.maximum(m, jnp.max(t, axis=-1, keepdims=True) + cq)
    alpha = jnp.exp(m - m_new)
    p = jnp.exp(t + (cq - m_new))
    l_new = alpha * l + jnp.sum(p, axis=-1, keepdims=True)
    acc_new = alpha * acc + _dot(p.astype(BF16), vb)
    return m_new, l_new, acc_new


def _fox_prompt_kernel(q_ref, k_ref, v_ref, cc_ref, cr_ref, o_ref, kb_ref, vb_ref, *, tq):
    i = pl.program_id(2)

    @pl.when(i == 0)
    def _():
        kb_ref[...] = k_ref[0].astype(BF16)
        vb_ref[...] = v_ref[0].astype(BF16)

    q = q_ref[0]
    low = _head_lane_masks(tq)
    qh = (jnp.where(low, q, jnp.zeros_like(q)), jnp.where(low, jnp.zeros_like(q), q))
    cq = (cc_ref[0, 0, :, 0:1], cc_ref[0, 0, :, 1:2])

    def step(j, carry, mask):
        s = pl.multiple_of(j * tq, tq)
        kb = kb_ref[pl.ds(s, tq), :]
        vb = vb_ref[pl.ds(s, tq), :]
        out = []
        for hh in range(2):
            ck = cr_ref[0, 0, hh:hh + 1, pl.ds(s, tq)]
            m, l, acc = carry[3 * hh:3 * hh + 3]
            out.extend(_softmax_step(qh[hh], kb, vb, ck, cq[hh], m, l, acc, mask))
        return tuple(out)

    init = []
    for _ in range(2):
        init.extend([jnp.full((tq, 1), NEG_INF, F32), jnp.zeros((tq, 1), F32), jnp.zeros((tq, LANES), F32)])
    carry = lax.fori_loop(0, i, lambda j, c: step(j, c, None), tuple(init))
    r = lax.broadcasted_iota(jnp.int32, (tq, tq), 0)
    c = lax.broadcasted_iota(jnp.int32, (tq, tq), 1)
    m0, l0, a0, m1, l1, a1 = step(i, carry, c <= r)
    o_ref[0] = jnp.where(low, a0 / l0, a1 / l1).astype(o_ref.dtype)


def _fox_prompt(q, k, v, c_col, c_row, tq):
    b, s, _ = q.shape
    npair = N_FOX_HEADS // 2
    return pl.pallas_call(
        functools.partial(_fox_prompt_kernel, tq=tq),
        out_shape=jax.ShapeDtypeStruct((b, s, FOX_WIDTH), BF16),
        grid=(b, npair, s // tq),
        in_specs=[
            pl.BlockSpec((1, tq, LANES), lambda bi, hp, i: (bi, i, hp)),
            pl.BlockSpec((1, s, LANES), lambda bi, hp, i: (bi, 0, hp)),
            pl.BlockSpec((1, s, LANES), lambda bi, hp, i: (bi, 0, hp)),
            pl.BlockSpec((1, 1, tq, 2), lambda bi, hp, i: (bi, hp, i, 0)),
            pl.BlockSpec((1, 1, 2, s), lambda bi, hp, i: (bi, hp, 0, 0)),
        ],
        out_specs=pl.BlockSpec((1, tq, LANES), lambda bi, hp, i: (bi, i, hp)),
        scratch_shapes=[pltpu.VMEM((s, LANES), BF16), pltpu.VMEM((s, LANES), BF16)],
        compiler_params=_params(("parallel", "parallel", "arbitrary")), name="fox_prompt")(q, k, v, c_col, c_row)


def _fox_sample_kernel(q_ref, ck_ref, cv_ref, nk_ref, nv_ref, cq_ref, crc_ref, crn_ref, o_ref,
                       m_ref, l_ref, acc_ref, *, n):
    j = pl.program_id(1)
    nj = pl.num_programs(1)

    @pl.when(j == 0)
    def _():
        m_ref[...] = jnp.full(m_ref.shape, NEG_INF, F32)
        l_ref[...] = jnp.zeros(l_ref.shape, F32)
        acc_ref[...] = jnp.zeros(acc_ref.shape, F32)

    low = _head_lane_masks(n)

    def update(k_blk, v_blk, cr_ref_, mask):
        for hp in range(N_FOX_HEADS // 2):
            sl = slice(hp * LANES, (hp + 1) * LANES)
            q = q_ref[0, :, sl]
            kb = k_blk(sl).astype(BF16)
            vb = v_blk(sl).astype(BF16)
            acc = acc_ref[:, sl]
            res = []
            for hh in range(2):
                hd = 2 * hp + hh
                qh = jnp.where(low, q, jnp.zeros_like(q)) if hh == 0 else jnp.where(low, jnp.zeros_like(q), q)
                m_new, l_new, a_new = _softmax_step(
                    qh, kb, vb, cr_ref_[0, hd:hd + 1, :], cq_ref[0, :, hd:hd + 1],
                    m_ref[hd], l_ref[hd], acc, mask)
                m_ref[hd] = m_new
                l_ref[hd] = l_new
                res.append(a_new)
            acc_ref[:, sl] = jnp.where(low, res[0], res[1])

    update(lambda sl: ck_ref[0, :, sl], lambda sl: cv_ref[0, :, sl], crc_ref, None)

    @pl.when(j == nj - 1)
    def _():
        r = lax.broadcasted_iota(jnp.int32, (n, n), 0)
        c = lax.broadcasted_iota(jnp.int32, (n, n), 1)
        update(lambda sl: nk_ref[0, :, sl], lambda sl: nv_ref[0, :, sl], crn_ref, c <= r)
        for hp in range(N_FOX_HEADS // 2):
            sl = slice(hp * LANES, (hp + 1) * LANES)
            lsum = jnp.where(low, l_ref[2 * hp], l_ref[2 * hp + 1])
            o_ref[0, :, sl] = (acc_ref[:, sl] / lsum).astype(o_ref.dtype)


def _fox_sample(q, cache_k, cache_v, k_new, v_new, c_q, c_row_cache, c_row_new, tk):
    b, n, _ = q.shape
    past = cache_k.shape[1]
    assert past % tk == 0
    return pl.pallas_call(
        functools.partial(_fox_sample_kernel, n=n),
        out_shape=jax.ShapeDtypeStruct((b, n, FOX_WIDTH), BF16),
        grid=(b, past // tk),
        in_specs=[
            pl.BlockSpec((1, n, FOX_WIDTH), lambda bi, j: (bi, 0, 0)),
            pl.BlockSpec((1, tk, FOX_WIDTH), lambda bi, j: (bi, j, 0)),
            pl.BlockSpec((1, tk, FOX_WIDTH), lambda bi, j: (bi, j, 0)),
            pl.BlockSpec((1, n, FOX_WIDTH), lambda bi, j: (bi, 0, 0)),
            pl.BlockSpec((1, n, FOX_WIDTH), lambda bi, j: (bi, 0, 0)),
            pl.BlockSpec((1, n, N_FOX_HEADS), lambda bi, j: (bi, 0, 0)),
            pl.BlockSpec((1, N_FOX_HEADS, tk), lambda bi, j: (bi, 0, j)),
            pl.BlockSpec((1, N_FOX_HEADS, n), lambda bi, j: (bi, 0, 0)),
        ],
        out_specs=pl.BlockSpec((1, n, FOX_WIDTH), lambda bi, j: (bi, 0, 0)),
        scratch_shapes=[pltpu.VMEM((N_FOX_HEADS, n, 1), F32), pltpu.VMEM((N_FOX_HEADS, n, 1), F32),
                        pltpu.VMEM((n, FOX_WIDTH), F32)],
        compiler_params=_params(("parallel", "arbitrary")), name="fox_sample")(
            q, cache_k, cache_v, k_new, v_new, c_q, c_row_cache, c_row_new)


def _ssm_mats(p):
    f32 = F32
    a_re, a_im = p['ssm_a_re'].astype(f32), p['ssm_a_im'].astype(f32)
    b_re, b_im = p['ssm_b_re'].astype(f32), p['ssm_b_im'].astype(f32)
    c_re, c_im = p['ssm_c_re'].astype(f32), p['ssm_c_im'].astype(f32)
    dt = jnp.exp(p['ssm_log_dt'].astype(f32))[:, None]
    mag = jnp.exp(dt * a_re)
    ab_re = mag * jnp.cos(dt * a_im)
    ab_im = mag * jnp.sin(dt * a_im)
    den = a_re * a_re + a_im * a_im
    nr, ni = ab_re - 1.0, ab_im
    coef_re = (nr * a_re + ni * a_im) / den
    coef_im = (ni * a_re - nr * a_im) / den
    bb_re = coef_re[..., None] * b_re - coef_im[..., None] * b_im
    bb_im = coef_re[..., None] * b_im + coef_im[..., None] * b_re
    pr, pi = [jnp.ones_like(ab_re)], [jnp.zeros_like(ab_im)]
    for _ in range(SSM_CHUNK):
        pr.append(pr[-1] * ab_re - pi[-1] * ab_im)
        pi.append(pr[-2] * ab_im + pi[-1] * ab_re)
    pw_re, pw_im = jnp.stack(pr), jnp.stack(pi)
    T = SSM_CHUNK
    w_re = pw_re[..., None] * bb_re[None] - pw_im[..., None] * bb_im[None]
    w_im = pw_re[..., None] * bb_im[None] + pw_im[..., None] * bb_re[None]
    kk = (jnp.einsum('gop,kgpi->kgoi', c_re, w_re[:T], precision='highest')
          - jnp.einsum('gop,kgpi->kgoi', c_im, w_im[:T], precision='highest'))
    s_idx = jnp.arange(T)[:, None]
    t_idx = jnp.arange(T)[None, :]
    lag = t_idx - s_idx
    kg = kk[jnp.clip(lag, 0, T - 1)]
    kg = jnp.where((lag >= 0)[:, :, None, None, None], kg, 0.0)
    kmat = jnp.transpose(kg, (2, 0, 4, 1, 3)).reshape(N_SSM_GROUPS, SSM_FEAT, SSM_FEAT)
    m_re = jnp.transpose(w_re[T - 1 - jnp.arange(T)], (1, 0, 3, 2)).reshape(N_SSM_GROUPS, SSM_FEAT, SSM_STATE)
    m_im = jnp.transpose(w_im[T - 1 - jnp.arange(T)], (1, 0, 3, 2)).reshape(N_SSM_GROUPS, SSM_FEAT, SSM_STATE)
    ar, ai = pw_re[1:], pw_im[1:]
    n_re = (c_re[None] * ar[:, :, None, :] - c_im[None] * ai[:, :, None, :])
    n_im = -(c_re[None] * ai[:, :, None, :] + c_im[None] * ar[:, :, None, :])
    n_re = jnp.transpose(n_re, (1, 3, 0, 2)).reshape(N_SSM_GROUPS, SSM_STATE, SSM_FEAT)
    n_im = jnp.transpose(n_im, (1, 3, 0, 2)).reshape(N_SSM_GROUPS, SSM_STATE, SSM_FEAT)

    def pair_diag(m):
        g, r, c = m.shape
        m = m.reshape(g // 2, 2, r, c)
        z = jnp.zeros_like(m[:, 0])
        top = jnp.concatenate([m[:, 0], z], axis=2)
        bot = jnp.concatenate([z, m[:, 1]], axis=2)
        return jnp.concatenate([top, bot], axis=1)

    mre_hi, mre_lo = _split_bf16(pair_diag(m_re))
    mim_hi, mim_lo = _split_bf16(pair_diag(m_im))
    d_feat = jnp.broadcast_to(p['ssm_d'].astype(f32).reshape(N_SSM_GROUPS, 1, SSM_GROUP),
                              (N_SSM_GROUPS, T, SSM_GROUP)).reshape(1, N_SSM_GROUPS * SSM_FEAT)
    return dict(
        kmat=kmat.reshape(N_SSM_GROUPS // 2, 2, SSM_FEAT, SSM_FEAT).astype(BF16),
        mre_hi=mre_hi, mre_lo=mre_lo, mim_hi=mim_hi, mim_lo=mim_lo,
        nre=pair_diag(n_re).astype(BF16), nim=pair_diag(n_im).astype(BF16),
        a16_re=pw_re[T].reshape(8, 256), a16_im=pw_im[T].reshape(8, 256), d_feat=d_feat)


def _ssm_local_kernel(u_ref, mrh_ref, mrl_ref, mih_ref, mil_ref, hre_ref, him_ref):
    hi, lo = _split_bf16(u_ref[...])
    hre_ref[...] = _dot(hi, mrh_ref[0]) + _dot(hi, mrl_ref[0]) + _dot(lo, mrh_ref[0])
    him_ref[...] = _dot(hi, mih_ref[0]) + _dot(hi, mil_ref[0]) + _dot(lo, mih_ref[0])


def _ssm_local(ug, mats):
    r = ug.shape[0]
    npair = N_SSM_GROUPS // 2
    mspec = pl.BlockSpec((1, 2 * SSM_FEAT, 2 * SSM_STATE), lambda g: (g, 0, 0))
    ospec = pl.BlockSpec((r, 2 * SSM_STATE), lambda g: (0, g))
    return pl.pallas_call(
        _ssm_local_kernel,
        out_shape=(jax.ShapeDtypeStruct((r, N_SSM_GROUPS * SSM_STATE), F32),) * 2,
        grid=(npair,),
        in_specs=[pl.BlockSpec((r, 2 * SSM_FEAT), lambda g: (0, g)), mspec, mspec, mspec, mspec],
        out_specs=(ospec, ospec),
        compiler_params=_params(("parallel",)), name="ssm_local")(
            ug, mats['mre_hi'], mats['mre_lo'], mats['mim_hi'], mats['mim_lo'])


def _ssm_scan_kernel(lre_ref, lim_ref, are_ref, aim_ref, h0re_ref, h0im_ref,
                     pre_ref, pim_ref, fre_ref, fim_ref):
    nchunk = lre_ref.shape[1]
    ar, ai = are_ref[...], aim_ref[...]

    def body(c, carry):
        hr, hi = carry
        pre_ref[0, c] = hr
        pim_ref[0, c] = hi
        return (ar * hr - ai * hi + lre_ref[0, c], ar * hi + ai * hr + lim_ref[0, c])

    hr, hi = lax.fori_loop(0, nchunk, body, (h0re_ref[0], h0im_ref[0]))
    fre_ref[0] = hr
    fim_ref[0] = hi


def _ssm_scan(hloc_re, hloc_im, mats, h0_re, h0_im):
    b, nchunk = hloc_re.shape[:2]
    big = pl.BlockSpec((1, nchunk, 8, 256), lambda i: (i, 0, 0, 0))
    small = pl.BlockSpec((1, 8, 256), lambda i: (i, 0, 0))
    return pl.pallas_call(
        _ssm_scan_kernel,
        out_shape=(jax.ShapeDtypeStruct(hloc_re.shape, F32),) * 2 + (jax.ShapeDtypeStruct((b, 8, 256), F32),) * 2,
        grid=(b,),
        in_specs=[big, big, _full((8, 256)), _full((8, 256)), small, small],
        out_specs=(big, big, small, small),
        compiler_params=_params(("parallel",)), name="ssm_scan")(
            hloc_re, hloc_im, mats['a16_re'], mats['a16_im'], h0_re, h0_im)


def _gelu_tanh(y):
    return 0.5 * y * (1.0 + jnp.tanh(math.sqrt(2.0 / math.pi) * (y + 0.044715 * (y * y * y))))


def _ssm_out_kernel(u_ref, k_ref, pre_ref, pim_ref, nre_ref, nim_ref, d_ref, y_ref):
    u = u_ref[...]
    ub = u.astype(BF16)
    y = jnp.concatenate([_dot(ub[:, 0:SSM_FEAT], k_ref[0, 0]), _dot(ub[:, SSM_FEAT:2 * SSM_FEAT], k_ref[0, 1])],
                        axis=-1)
    y = y + _dot(pre_ref[...].astype(BF16), nre_ref[0]) + _dot(pim_ref[...].astype(BF16), nim_ref[0])
    y = y + d_ref[...] * u
    y_ref[...] = _gelu_tanh(y).astype(y_ref.dtype)


def _ssm_out(ug, hprev_re, hprev_im, mats):
    r = ug.shape[0]
    npair = N_SSM_GROUPS // 2
    uspec = pl.BlockSpec((r, 2 * SSM_FEAT), lambda g: (0, g))
    hspec = pl.BlockSpec((r, 2 * SSM_STATE), lambda g: (0, g))
    nspec = pl.BlockSpec((1, 2 * SSM_STATE, 2 * SSM_FEAT), lambda g: (g, 0, 0))
    return pl.pallas_call(
        _ssm_out_kernel,
        out_shape=jax.ShapeDtypeStruct(ug.shape, BF16),
        grid=(npair,),
        in_specs=[uspec, pl.BlockSpec((1, 2, SSM_FEAT, SSM_FEAT), lambda g: (g, 0, 0, 0)), hspec, hspec,
                  nspec, nspec, pl.BlockSpec((1, 2 * SSM_FEAT), lambda g: (0, g))],
        out_specs=uspec,
        compiler_params=_params(("parallel",)), name="ssm_out")(
            ug, mats['kmat'], hprev_re, hprev_im, mats['nre'], mats['nim'], mats['d_feat'])


def _ssm(u, h0_re, h0_im, mats):
    b, s, _ = u.shape
    nchunk = s // SSM_CHUNK
    r = b * nchunk
    ug = u.reshape(r, SSM_CHUNK, N_SSM_GROUPS, SSM_GROUP).transpose(0, 2, 1, 3).reshape(r, N_SSM_GROUPS * SSM_FEAT)
    hloc_re, hloc_im = _ssm_local(ug, mats)
    shp = (b, nchunk, 8, 256)
    hprev_re, hprev_im, f_re, f_im = _ssm_scan(hloc_re.reshape(shp), hloc_im.reshape(shp), mats,
                                               h0_re.reshape(b, 8, 256), h0_im.reshape(b, 8, 256))
    yg = _ssm_out(ug, hprev_re.reshape(r, -1), hprev_im.reshape(r, -1), mats)
    y = yg.reshape(r, N_SSM_GROUPS, SSM_CHUNK, SSM_GROUP).transpose(0, 2, 1, 3).reshape(b * s, SSM_WIDTH)
    return y, f_re.reshape(b, N_SSM_GROUPS, SSM_STATE), f_im.reshape(b, N_SSM_GROUPS, SSM_STATE)


def _merge_kernel(x_ref, of_ref, ys_ref, qm_ref, gate_ref, mk_ref, mv_ref,
                  wglu_ref, wbf_ref, wbs_ref, wbm_ref, wo_ref, nf_ref, wr_ref,
                  x1_ref, h2_ref, comb_ref):
    tm = x_ref.shape[0]
    om = []
    for hd in range(N_MEM_HEADS):
        sl = slice(hd * MEM_HEAD_DIM, (hd + 1) * MEM_HEAD_DIM)
        kh = mk_ref[0, :, sl].astype(BF16)
        vh = mv_ref[0, :, sl].astype(BF16)
        sc = _dot_nt(qm_ref[:, sl], kh)
        p = jnp.exp(sc - jnp.max(sc, axis=-1, keepdims=True))
        om.append(_dot(p.astype(BF16), vh) / jnp.sum(p, axis=-1, keepdims=True))
    o_mem = jnp.concatenate(om, axis=-1).astype(BF16)
    z = _dot(ys_ref[...], wglu_ref[...])
    y_ssm = (z[:, 0:SSM_WIDTH] * jax.nn.sigmoid(z[:, SSM_WIDTH:2 * SSM_WIDTH])).astype(BF16)
    g = lambda c: gate_ref[:, c * D_MODEL:(c + 1) * D_MODEL].astype(F32)
    merged = (g(0) * _dot(of_ref[...], wbf_ref[...]) + g(1) * _dot(y_ssm, wbs_ref[...])
              + g(2) * _dot(o_mem, wbm_ref[...]))
    x1 = x_ref[...] + _dot(merged.astype(BF16), wo_ref[...])
    x1_ref[...] = x1
    h2 = x1 * lax.rsqrt(jnp.mean(x1 * x1, axis=-1, keepdims=True) + RMS_EPS) * nf_ref[...]
    h2_ref[...] = h2.astype(BF16)
    logits = _dot_exact(h2, wr_ref[...])
    lane = lax.broadcasted_iota(jnp.int32, (tm, LANES), 1)
    big = jnp.int32(LANES)
    is_grp = (lane >= N_EXPERTS) & (lane < N_EXPERTS + N_EXPERT_GROUPS)
    gl = jnp.where(is_grp, logits, NEG_INF)
    gmax = jnp.max(gl, axis=-1, keepdims=True)
    grp = jnp.min(jnp.where(is_grp & (gl == gmax), lane, big), axis=-1, keepdims=True) - N_EXPERTS
    g_w = 1.0 / jnp.sum(jnp.where(is_grp, jnp.exp(gl - gmax), 0.0), axis=-1, keepdims=True)
    in_grp = (lane >= grp * EXPERTS_PER_GROUP) & (lane < (grp + 1) * EXPERTS_PER_GROUP)
    e1 = jnp.where(in_grp, logits, NEG_INF)
    m1 = jnp.max(e1, axis=-1, keepdims=True)
    i1 = jnp.min(jnp.where(in_grp & (e1 == m1), lane, big), axis=-1, keepdims=True)
    rest = in_grp & (lane != i1)
    e2 = jnp.where(rest, logits, NEG_INF)
    m2 = jnp.max(e2, axis=-1, keepdims=True)
    i2 = jnp.min(jnp.where(rest & (e2 == m2), lane, big), axis=-1, keepdims=True)
    ex = jnp.exp(m2 - m1)
    w1 = g_w / (1.0 + ex)
    w2 = g_w * ex / (1.0 + ex)
    comb_ref[...] = jnp.where(lane == i1, w1, jnp.where(lane == i2, w2, 0.0))


def _merge(x2d, o_fox, ys, q_m, gates, mem_k, mem_v, w, tm, rows_per_batch):
    n = x2d.shape[0]
    assert n % tm == 0 and rows_per_batch % tm == 0
    per = rows_per_batch // tm
    row = lambda width: pl.BlockSpec((tm, width), lambda i: (i, 0))
    memspec = pl.BlockSpec((1, N_MEM, MEM_WIDTH), lambda i: (i // per, 0, 0))
    ws = [w['w_glu'], w['w_br_fox'], w['w_br_ssm'], w['w_br_mem'], w['w_out'], w['norm_ffn'], w['w_router']]
    return pl.pallas_call(
        _merge_kernel,
        out_shape=(jax.ShapeDtypeStruct((n, D_MODEL), F32), jax.ShapeDtypeStruct((n, D_MODEL), BF16),
                   jax.ShapeDtypeStruct((n, LANES), F32)),
        grid=(n // tm,),
        in_specs=[row(D_MODEL), row(FOX_WIDTH), row(SSM_WIDTH), row(MEM_WIDTH), row(3 * D_MODEL), memspec, memspec]
                 + [_full(a.shape) for a in ws],
        out_specs=(row(D_MODEL), row(D_MODEL), row(LANES)),
        compiler_params=_params(("parallel",)), name="merge")(
            x2d, o_fox, ys, q_m, gates, mem_k, mem_v, *ws)


def _moe_kernel(h_ref, comb_ref, x1_ref, wg_ref, wu_ref, wd_ref, o_ref, acc_ref):
    e = pl.program_id(1)

    @pl.when(e == 0)
    def _():
        acc_ref[...] = jnp.zeros(acc_ref.shape, F32)

    h = h_ref[...]
    a = _dot(h, wg_ref[0])
    up = _dot(h, wu_ref[0])
    lane = lax.broadcasted_iota(jnp.int32, comb_ref.shape, 1)
    ce = jnp.sum(jnp.where(lane == e, comb_ref[...], 0.0), axis=-1, keepdims=True)
    act = a * jax.nn.sigmoid(a) * up * ce
    acc_ref[...] += _dot(act.astype(BF16), wd_ref[0])

    @pl.when(e == pl.num_programs(1) - 1)
    def _():
        o_ref[...] = x1_ref[...] + acc_ref[...]


def _moe(h2, comb, x1, wg, wu, wd, tm):
    n = h2.shape[0]
    assert n % tm == 0
    row = lambda width: pl.BlockSpec((tm, width), lambda i, e: (i, 0))
    return pl.pallas_call(
        _moe_kernel,
        out_shape=jax.ShapeDtypeStruct((n, D_MODEL), F32),
        grid=(n // tm, N_EXPERTS),
        in_specs=[row(D_MODEL), row(LANES), row(D_MODEL),
                  pl.BlockSpec((1, D_MODEL, D_EXPERT), lambda i, e: (e, 0, 0)),
                  pl.BlockSpec((1, D_MODEL, D_EXPERT), lambda i, e: (e, 0, 0)),
                  pl.BlockSpec((1, D_EXPERT, D_MODEL), lambda i, e: (e, 0, 0))],
        out_specs=row(D_MODEL),
        scratch_shapes=[pltpu.VMEM((tm, D_MODEL), F32)],
        compiler_params=_params(("parallel", "arbitrary")), name="moe")(h2, comb, x1, wg, wu, wd)


def _prep_weights(p):
    w_in = p['w_in'].astype(BF16)
    o = 0
    wqkv = w_in[:, 0:3 * FOX_WIDTH]
    o = 3 * FOX_WIDTH
    wf = jnp.pad(w_in[:, o:o + N_FOX_HEADS], ((0, 0), (0, LANES - N_FOX_HEADS)))
    o += N_FOX_HEADS
    wqm = w_in[:, o:o + MEM_WIDTH]
    o += MEM_WIDTH
    wu = w_in[:, o:o + SSM_WIDTH]
    o += SSM_WIDTH
    wg = w_in[:, o:o + 3 * D_MODEL]
    r = jnp.arange(FOX_WIDTH) // FOX_HEAD_DIM
    bd = (r[:, None] == r[None, :]).astype(BF16)
    w_router = jnp.concatenate(
        [p['w_router_expert'], p['w_router_group'],
         jnp.zeros((D_MODEL, LANES - N_EXPERTS - N_EXPERT_GROUPS), F32)], axis=1)
    return dict(
        norm_mix=p['norm_mix'].reshape(1, D_MODEL), wqkv=wqkv, wf=wf,
        bf=jnp.pad(p['b_forget'], (0, LANES - N_FOX_HEADS)).reshape(1, LANES),
        wqm=wqm, wu=wu, wg=wg,
        qn_fox=jnp.tile(p['qn_fox'], N_FOX_HEADS).reshape(1, FOX_WIDTH),
        kn_fox=jnp.tile(p['kn_fox'], N_FOX_HEADS).reshape(1, FOX_WIDTH),
        qn_mem=p['qn_mem'].reshape(1, MEM_HEAD_DIM), bd=bd,
        w_glu=p['w_glu'].astype(BF16), w_br_fox=p['w_br_fox'].astype(BF16),
        w_br_ssm=p['w_br_ssm'].astype(BF16), w_br_mem=p['w_br_mem'].astype(BF16),
        w_out=p['w_out'].astype(BF16), norm_ffn=p['norm_ffn'].reshape(1, D_MODEL), w_router=w_router,
        moe_wg=p['moe_w_gate'].astype(BF16), moe_wu=p['moe_w_up'].astype(BF16),
        moe_wd=p['moe_w_down'].astype(BF16))


def _pick_tile(n, target):
    t = min(n, target)
    while n % t:
        t //= 2
    return t


def _group(x, w, mats, mem_k, mem_v, h0_re, h0_im, cache):
    b, s, _ = x.shape
    n = b * s
    x2d = x.reshape(n, D_MODEL)
    q, k, v, logf, q_m, u, gates = _inproj(x2d, w, _pick_tile(n, 512))
    q3 = q.reshape(b, s, FOX_WIDTH)
    k3 = k.reshape(b, s, FOX_WIDTH)
    v3 = v.reshape(b, s, FOX_WIDTH)
    lf3 = logf.reshape(b, s, N_FOX_HEADS)
    if cache is None:
        c = _cumsum(lf3)
        c_col = c.reshape(b, s, N_FOX_HEADS // 2, 2).transpose(0, 2, 1, 3)
        c_row = c.reshape(b, s, N_FOX_HEADS // 2, 2).transpose(0, 2, 3, 1)
        o_fox = _fox_prompt(q3, k3, v3, c_col, c_row, _pick_tile(s, 256))
    else:
        cache_k, cache_v, cache_logf = cache
        past = cache_k.shape[1]
        c_all = _cumsum(jnp.concatenate([cache_logf.astype(F32), lf3], axis=1))
        c_row = c_all.transpose(0, 2, 1)
        o_fox = _fox_sample(q3, cache_k.reshape(b, past, FOX_WIDTH), cache_v.reshape(b, past, FOX_WIDTH),
                            k3, v3, c_all[:, past:], c_row[:, :, :past], c_row[:, :, past:],
                            _pick_tile(past, 1024))
    ys, f_re, f_im = _ssm(u.reshape(b, s, SSM_WIDTH), h0_re, h0_im, mats)
    tm = _pick_tile(s, 512)
    x1, h2, comb = _merge(x2d, o_fox.reshape(n, FOX_WIDTH), ys, q_m, gates, mem_k, mem_v, w, tm, s)
    y = _moe(h2, comb, x1, w['moe_wg'], w['moe_wu'], w['moe_wd'], _pick_tile(n, 1024))
    return (y.reshape(b, s, D_MODEL), k3.reshape(b, s, N_FOX_HEADS, FOX_HEAD_DIM),
            v3.reshape(b, s, N_FOX_HEADS, FOX_HEAD_DIM), lf3, f_re, f_im)


def kernel(x_prompt, x_sample, mem_prompt, cache_fox_k, cache_fox_v, cache_fox_logf, state_ssm_re, state_ssm_im,
           cache_mem_k, cache_mem_v, norm_mix, w_in, b_forget, qn_fox, kn_fox, qn_mem, kn_mem, norm_mem, w_mem_kv,
           ssm_a_re, ssm_a_im, ssm_log_dt, ssm_b_re, ssm_b_im, ssm_c_re, ssm_c_im, ssm_d, w_glu, w_br_fox,
           w_br_ssm, w_br_mem, w_out, norm_ffn, w_router_group, w_router_expert, moe_w_gate, moe_w_up,
           moe_w_down):
    depth = norm_mix.shape[0]
    assert depth == 1
    l = 0
    p = dict(norm_mix=norm_mix[l], w_in=w_in[l], b_forget=b_forget[l], qn_fox=qn_fox[l], kn_fox=kn_fox[l],
             qn_mem=qn_mem[l], ssm_a_re=ssm_a_re[l], ssm_a_im=ssm_a_im[l], ssm_log_dt=ssm_log_dt[l],
             ssm_b_re=ssm_b_re[l], ssm_b_im=ssm_b_im[l], ssm_c_re=ssm_c_re[l], ssm_c_im=ssm_c_im[l],
             ssm_d=ssm_d[l], w_glu=w_glu[l], w_br_fox=w_br_fox[l], w_br_ssm=w_br_ssm[l], w_br_mem=w_br_mem[l],
             w_out=w_out[l], norm_ffn=norm_ffn[l], w_router_group=w_router_group[l],
             w_router_expert=w_router_expert[l], moe_w_gate=moe_w_gate[l], moe_w_up=moe_w_up[l],
             moe_w_down=moe_w_down[l])
    w = _prep_weights(p)
    mats = _ssm_mats(p)
    bp, sp, _ = x_prompt.shape
    bs, ss, _ = x_sample.shape

    mk, mv = _memkv(mem_prompt.reshape(bp * N_MEM, D_MODEL), norm_mem[l].reshape(1, D_MODEL),
                    w_mem_kv[l].astype(BF16), kn_mem[l].reshape(1, MEM_HEAD_DIM), _pick_tile(bp * N_MEM, 512))
    mk = mk.reshape(bp, N_MEM, MEM_WIDTH)
    mv = mv.reshape(bp, N_MEM, MEM_WIDTH)
    zeros = jnp.zeros((bp, N_SSM_GROUPS, SSM_STATE), F32)
    yp, pk, pv, plf, pre, pim = _group(x_prompt, w, mats, mk, mv, zeros, zeros, None)
    cache = (cache_fox_k[l], cache_fox_v[l], cache_fox_logf[l])
    ys, sk, sv, slf, sre, sim = _group(
        x_sample, w, mats, cache_mem_k[l].reshape(bs, N_MEM, MEM_WIDTH), cache_mem_v[l].reshape(bs, N_MEM, MEM_WIDTH),
        state_ssm_re[l].astype(F32), state_ssm_im[l].astype(F32), cache)
    st = lambda a: a[None]
    return (yp, ys, st(pk), st(pv), st(plf), st(pre), st(pim),
            st(mk.reshape(bp, N_MEM, N_MEM_HEADS, MEM_HEAD_DIM)), st(mv.reshape(bp, N_MEM, N_MEM_HEADS, MEM_HEAD_DIM)),
            st(sk), st(sv), st(slf), st(sre), st(sim))
```

```python
import functools
import math

import jax
import jax.numpy as jnp
from jax import lax
from jax.experimental import pallas as pl
from jax.experimental.pallas import tpu as pltpu

F32 = jnp.float32
BF16 = jnp.bfloat16

D_MODEL = 1024
N_FOX_HEADS = 8
FOX_HEAD_DIM = 64
FOX_WIDTH = N_FOX_HEADS * FOX_HEAD_DIM
N_MEM = 256
N_MEM_HEADS = 4
MEM_HEAD_DIM = 128
MEM_WIDTH = N_MEM_HEADS * MEM_HEAD_DIM
SSM_GROUP = 16
SSM_WIDTH = 512
N_SSM_GROUPS = SSM_WIDTH // SSM_GROUP
SSM_STATE = 64
N_EXPERT_GROUPS = 4
EXPERTS_PER_GROUP = 8
N_EXPERTS = N_EXPERT_GROUPS * EXPERTS_PER_GROUP
D_EXPERT = 256
RMS_EPS = 1e-6
NEG_INF = -1e30

LANES = 128
SSM_CHUNK = 16
SSM_FEAT = SSM_CHUNK * SSM_GROUP
VMEM_LIMIT = 56 * 1024 * 1024


def _dot(a, b):
    return jnp.dot(a, b, preferred_element_type=F32)


def _dot_nt(a, b):
    return lax.dot_general(a, b, (((1,), (1,)), ((), ())), preferred_element_type=F32)


def _dot_exact(a, b):
    return jnp.dot(a, b, preferred_element_type=F32, precision=lax.Precision.HIGHEST)


def _split_bf16(x):
    hi = x.astype(BF16)
    lo = (x - hi.astype(F32)).astype(BF16)
    return hi, lo


def _params(sem):
    return pltpu.CompilerParams(dimension_semantics=sem, vmem_limit_bytes=VMEM_LIMIT)


def _full(shape):
    n = len(shape)
    return pl.BlockSpec(shape, lambda *_: (0,) * n)


def _inproj_kernel(x_ref, g_ref, wqkv_ref, wf_ref, bf_ref, wqm_ref, wu_ref, wg_ref,
                   qn_ref, kn_ref, qmn_ref, bd_ref,
                   q_ref, k_ref, v_ref, lf_ref, qm_ref, u_ref, gate_ref):
    x = x_ref[...]
    h = x * lax.rsqrt(jnp.mean(x * x, axis=-1, keepdims=True) + RMS_EPS) * g_ref[...]
    hb = h.astype(BF16)

    def head_norm(z, gain):
        hi, lo = _split_bf16(z * z)
        ss = _dot(hi, bd_ref[...]) + _dot(lo, bd_ref[...])
        return z * lax.rsqrt(ss * (1.0 / FOX_HEAD_DIM) + RMS_EPS) * gain

    zq = _dot(hb, wqkv_ref[:, 0:FOX_WIDTH])
    q_ref[...] = (head_norm(zq, qn_ref[...]) * (FOX_HEAD_DIM ** -0.5)).astype(BF16)
    zk = _dot(hb, wqkv_ref[:, FOX_WIDTH:2 * FOX_WIDTH])
    k_ref[...] = head_norm(zk, kn_ref[...])
    v_ref[...] = _dot(hb, wqkv_ref[:, 2 * FOX_WIDTH:3 * FOX_WIDTH])

    zf = _dot(hb, wf_ref[...]) + bf_ref[...]
    logf = jnp.minimum(zf, 0.0) - jnp.log1p(jnp.exp(-jnp.abs(zf)))
    lf_ref[...] = logf[:, 0:N_FOX_HEADS]

    zm = _dot(hb, wqm_ref[...])
    for hd in range(N_MEM_HEADS):
        sl = slice(hd * MEM_HEAD_DIM, (hd + 1) * MEM_HEAD_DIM)
        zh = zm[:, sl]
        ms = jnp.mean(zh * zh, axis=-1, keepdims=True)
        qm_ref[:, sl] = (zh * lax.rsqrt(ms + RMS_EPS) * qmn_ref[...] * (MEM_HEAD_DIM ** -0.5)).astype(BF16)

    u_ref[...] = _dot(hb, wu_ref[...])
    for c in range(3):
        sl = slice(c * D_MODEL, (c + 1) * D_MODEL)
        gate_ref[:, sl] = jax.nn.sigmoid(_dot(hb, wg_ref[:, sl])).astype(BF16)


def _inproj(x2d, w, tm):
    n = x2d.shape[0]
    assert n % tm == 0
    row = lambda width: pl.BlockSpec((tm, width), lambda i: (i, 0))
    ins = [x2d, w['norm_mix'], w['wqkv'], w['wf'], w['bf'], w['wqm'], w['wu'], w['wg'],
           w['qn_fox'], w['kn_fox'], w['qn_mem'], w['bd']]
    in_specs = [row(D_MODEL)] + [_full(a.shape) for a in ins[1:]]
    out_shape = (
        jax.ShapeDtypeStruct((n, FOX_WIDTH), BF16),
        jax.ShapeDtypeStruct((n, FOX_WIDTH), F32),
        jax.ShapeDtypeStruct((n, FOX_WIDTH), F32),
        jax.ShapeDtypeStruct((n, N_FOX_HEADS), F32),
        jax.ShapeDtypeStruct((n, MEM_WIDTH), BF16),
        jax.ShapeDtypeStruct((n, SSM_WIDTH), F32),
        jax.ShapeDtypeStruct((n, 3 * D_MODEL), BF16),
    )
    out_specs = (row(FOX_WIDTH), row(FOX_WIDTH), row(FOX_WIDTH), row(N_FOX_HEADS),
                 row(MEM_WIDTH), row(SSM_WIDTH), row(3 * D_MODEL))
    return pl.pallas_call(
        _inproj_kernel, out_shape=out_shape, grid=(n // tm,), in_specs=in_specs, out_specs=out_specs,
        compiler_params=_params(("parallel",)), name="inproj")(*ins)


def _memkv_kernel(x_ref, g_ref, w_ref, kn_ref, k_ref, v_ref):
    x = x_ref[...]
    h = x * lax.rsqrt(jnp.mean(x * x, axis=-1, keepdims=True) + RMS_EPS) * g_ref[...]
    hb = h.astype(BF16)
    zk = _dot(hb, w_ref[:, 0:MEM_WIDTH])
    for hd in range(N_MEM_HEADS):
        sl = slice(hd * MEM_HEAD_DIM, (hd + 1) * MEM_HEAD_DIM)
        zh = zk[:, sl]
        ms = jnp.mean(zh * zh, axis=-1, keepdims=True)
        k_ref[:, sl] = zh * lax.rsqrt(ms + RMS_EPS) * kn_ref[...]
    v_ref[...] = _dot(hb, w_ref[:, MEM_WIDTH:2 * MEM_WIDTH])


def _memkv(mem2d, norm_mem, w_kv, kn_mem, tm):
    n = mem2d.shape[0]
    row = lambda width: pl.BlockSpec((tm, width), lambda i: (i, 0))
    return pl.pallas_call(
        _memkv_kernel,
        out_shape=(jax.ShapeDtypeStruct((n, MEM_WIDTH), F32), jax.ShapeDtypeStruct((n, MEM_WIDTH), F32)),
        grid=(n // tm,),
        in_specs=[row(D_MODEL), _full(norm_mem.shape), _full(w_kv.shape), _full(kn_mem.shape)],
        out_specs=(row(MEM_WIDTH), row(MEM_WIDTH)),
        compiler_params=_params(("parallel",)), name="memkv")(mem2d, norm_mem, w_kv, kn_mem)


CUMSUM_BLOCK = 256


def _cumsum_kernel(x_ref, o_ref):
    nblk = x_ref.shape[1] // CUMSUM_BLOCK
    r = lax.broadcasted_iota(jnp.int32, (CUMSUM_BLOCK, CUMSUM_BLOCK), 0)
    c = lax.broadcasted_iota(jnp.int32, (CUMSUM_BLOCK, CUMSUM_BLOCK), 1)
    tri = (r <= c).astype(F32)
    carry = jnp.zeros((x_ref.shape[0], 1), F32)
    for j in range(nblk):
        sl = slice(j * CUMSUM_BLOCK, (j + 1) * CUMSUM_BLOCK)
        cs = _dot_exact(x_ref[:, sl], tri) + carry
        o_ref[:, sl] = cs
        carry = cs[:, CUMSUM_BLOCK - 1:CUMSUM_BLOCK]


def _cumsum_rows(x):
    rows, n = x.shape
    npad = -(-n // CUMSUM_BLOCK) * CUMSUM_BLOCK
    xp = jnp.pad(x, ((0, 0), (0, npad - n))) if npad != n else x
    out = pl.pallas_call(
        _cumsum_kernel, out_shape=jax.ShapeDtypeStruct((rows, npad), F32), grid=(1,),
        in_specs=[_full((rows, npad))], out_specs=_full((rows, npad)),
        compiler_params=_params(("arbitrary",)), name="cumsum")(xp)
    return out[:, :n] if npad != n else out


def _reduce_rows(x, op):
    rows, cols = x.shape
    if rows > 64 and rows % 64 == 0:
        x = op(x.reshape(rows // 64, 64, cols), axis=0)
        rows = 64
    if rows == 64:
        x = op(x.reshape(8, 8, cols), axis=0)
    return op(x, axis=0, keepdims=True)


def _head_lane_masks(rows):
    lane = lax.broadcasted_iota(jnp.int32, (rows, LANES), 1)
    return lane < FOX_HEAD_DIM


def _softmax_step(qh, kb, vb, ck, cq, m, l, acc, mask):
    t = _dot_nt(qh, kb) - ck
    if mask is not None:
        t = jnp.where(mask, t, NEG_INF)
    m_new = jnp.maximum(m, jnp.max(t, axis=-1, keepdims=True) + cq)
    alpha = jnp.exp(m - m_new)
    p = jnp.exp(t + (cq - m_new))
    l_new = alpha * l + jnp.sum(p, axis=-1, keepdims=True)
    acc_new = alpha * acc + _dot(p.astype(BF16), vb)
    return m_new, l_new, acc_new


def _fox_prompt_kernel(q_ref, k_ref, v_ref, cc_ref, cr_ref, o_ref,
                       kb_ref, vt_ref, ck0_ref, ck1_ref, st_ref, pt_ref, acc_ref, *, tq, tk):
    i = pl.program_id(2)
    s_len = k_ref.shape[1]

    @pl.when(i == 0)
    def _():
        kb_ref[...] = k_ref[0].astype(BF16)
        vt_ref[...] = v_ref[0].T.astype(BF16)
        ck0_ref[...] = jnp.broadcast_to(cc_ref[0, 0, :, 0:1], (s_len, LANES))
        ck1_ref[...] = jnp.broadcast_to(cc_ref[0, 0, :, 1:2], (s_len, LANES))

    qt = q_ref[0].astype(F32).T
    row = lax.broadcasted_iota(jnp.int32, (LANES, tq), 0)
    qts = (jnp.where(row < FOX_HEAD_DIM, qt, 0.0).astype(BF16), jnp.where(row < FOX_HEAD_DIM, 0.0, qt).astype(BF16))
    q0 = pl.multiple_of(i * tq, tq)
    cq = cr_ref[0, 0, :, pl.ds(q0, tq)]
    ck_refs = (ck0_ref, ck1_ref)

    def stage_a(n):
        s = pl.multiple_of(n * tk, tk)
        kb = kb_ref[pl.ds(s, tk), :]
        for hh in range(2):
            ck = ck_refs[hh][pl.ds(s, tk), :]
            st_ref[n & 1, hh] = _dot(kb, qts[hh]) - jnp.concatenate([ck] * (tq // LANES), axis=1)

    def stage_b(n, stats, masked):
        if masked:
            kpos = n * tk + lax.broadcasted_iota(jnp.int32, (tk, tq), 0)
            qpos = q0 + lax.broadcasted_iota(jnp.int32, (tk, tq), 1)
            mask = kpos <= qpos
        out = []
        for hh in range(2):
            m, l = stats[2 * hh:2 * hh + 2]
            t = st_ref[n & 1, hh]
            if masked:
                t = jnp.where(mask, t, NEG_INF)
            cqh = cq[hh:hh + 1, :]
            m_new = jnp.maximum(m, _reduce_rows(t, jnp.max) + cqh)
            alpha = jnp.exp(m - m_new)
            p = jnp.exp(t + (cqh - m_new))
            pt_ref[n & 1, hh] = p.astype(BF16)
            out.extend([m_new, alpha * l + _reduce_rows(p, jnp.sum), alpha])
        return out

    def stage_c(n, alphas):
        s = pl.multiple_of(jnp.maximum(n, 0) * tk, tk)
        for hh in range(2):
            vt = vt_ref[hh * FOX_HEAD_DIM:(hh + 1) * FOX_HEAD_DIM, pl.ds(s, tk)]
            acc_ref[hh] = alphas[hh] * acc_ref[hh] + _dot(vt, pt_ref[n & 1, hh])

    acc_ref[...] = jnp.zeros(acc_ref.shape, F32)
    pt_ref[1] = jnp.zeros(pt_ref.shape[1:], BF16)
    neg = jnp.full((1, tq), NEG_INF, F32)
    zero = jnp.zeros((1, tq), F32)
    one = jnp.ones((1, tq), F32)
    nfull = (i * tq) // tk
    stage_a(0)

    def body(n, carry):
        m0, l0, al0, m1, l1, al1 = carry
        stage_c(n - 1, (al0, al1))
        new = stage_b(n, (m0, l0, m1, l1), False)
        stage_a(n + 1)
        return tuple(new)

    m0, l0, al0, m1, l1, al1 = lax.fori_loop(0, nfull, body, (neg, zero, one, neg, zero, one))
    stage_c(nfull - 1, (al0, al1))
    _, l0, be0, _, l1, be1 = stage_b(nfull, (m0, l0, m1, l1), True)
    stage_c(nfull, (be0, be1))
    ot = jnp.concatenate([acc_ref[0] / l0, acc_ref[1] / l1], axis=0)
    o_ref[0] = ot.T.astype(o_ref.dtype)


def _fox_prompt(q, k, v, c_col, c_row, tq, tk):
    b, s, _ = q.shape
    assert s % tk == 0 and tk % tq == 0
    npair = N_FOX_HEADS // 2
    return pl.pallas_call(
        functools.partial(_fox_prompt_kernel, tq=tq, tk=tk),
        out_shape=jax.ShapeDtypeStruct((b, s, FOX_WIDTH), BF16),
        grid=(b, npair, s // tq),
        in_specs=[
            pl.BlockSpec((1, tq, LANES), lambda bi, hp, i: (bi, i, hp)),
            pl.BlockSpec((1, s, LANES), lambda bi, hp, i: (bi, 0, hp)),
            pl.BlockSpec((1, s, LANES), lambda bi, hp, i: (bi, 0, hp)),
            pl.BlockSpec((1, 1, s, 2), lambda bi, hp, i: (bi, hp, 0, 0)),
            pl.BlockSpec((1, 1, 2, s), lambda bi, hp, i: (bi, hp, 0, 0)),
        ],
        out_specs=pl.BlockSpec((1, tq, LANES), lambda bi, hp, i: (bi, i, hp)),
        scratch_shapes=[pltpu.VMEM((s, LANES), BF16), pltpu.VMEM((LANES, s), BF16),
                        pltpu.VMEM((s, LANES), F32), pltpu.VMEM((s, LANES), F32),
                        pltpu.VMEM((2, 2, tk, tq), F32), pltpu.VMEM((2, 2, tk, tq), BF16),
                        pltpu.VMEM((2, FOX_HEAD_DIM, tq), F32)],
        compiler_params=_params(("parallel", "parallel", "arbitrary")), name="fox_prompt")(q, k, v, c_col, c_row)


def _fox_sample_kernel(q_ref, ck_ref, cv_ref, nk_ref, nv_ref, cq_ref, crc_ref, crn_ref, o_ref,
                       m_ref, l_ref, acc_ref, *, n):
    j = pl.program_id(1)
    nj = pl.num_programs(1)

    @pl.when(j == 0)
    def _():
        m_ref[...] = jnp.full(m_ref.shape, NEG_INF, F32)
        l_ref[...] = jnp.zeros(l_ref.shape, F32)
        acc_ref[...] = jnp.zeros(acc_ref.shape, F32)

    low = _head_lane_masks(n)

    def update(k_blk, v_blk, cr_ref_, mask):
        for hp in range(N_FOX_HEADS // 2):
            sl = slice(hp * LANES, (hp + 1) * LANES)
            q = q_ref[0, :, sl]
            kb = k_blk(sl).astype(BF16)
            vb = v_blk(sl).astype(BF16)
            acc = acc_ref[:, sl]
            res = []
            for hh in range(2):
                hd = 2 * hp + hh
                qh = jnp.where(low, q, jnp.zeros_like(q)) if hh == 0 else jnp.where(low, jnp.zeros_like(q), q)
                m_new, l_new, a_new = _softmax_step(
                    qh, kb, vb, cr_ref_[0, hd:hd + 1, :], cq_ref[0, :, hd:hd + 1],
                    m_ref[hd], l_ref[hd], acc, mask)
                m_ref[hd] = m_new
                l_ref[hd] = l_new
                res.append(a_new)
            acc_ref[:, sl] = jnp.where(low, res[0], res[1])

    update(lambda sl: ck_ref[0, :, sl], lambda sl: cv_ref[0, :, sl], crc_ref, None)

    @pl.when(j == nj - 1)
    def _():
        r = lax.broadcasted_iota(jnp.int32, (n, n), 0)
        c = lax.broadcasted_iota(jnp.int32, (n, n), 1)
        update(lambda sl: nk_ref[0, :, sl], lambda sl: nv_ref[0, :, sl], crn_ref, c <= r)
        for hp in range(N_FOX_HEADS // 2):
            sl = slice(hp * LANES, (hp + 1) * LANES)
            lsum = jnp.where(low, l_ref[2 * hp], l_ref[2 * hp + 1])
            o_ref[0, :, sl] = (acc_ref[:, sl] / lsum).astype(o_ref.dtype)


def _fox_sample(q, cache_k, cache_v, k_new, v_new, c_q, c_row_cache, c_row_new, tk):
    b, n, _ = q.shape
    past = cache_k.shape[1]
    assert past % tk == 0
    return pl.pallas_call(
        functools.partial(_fox_sample_kernel, n=n),
        out_shape=jax.ShapeDtypeStruct((b, n, FOX_WIDTH), BF16),
        grid=(b, past // tk),
        in_specs=[
            pl.BlockSpec((1, n, FOX_WIDTH), lambda bi, j: (bi, 0, 0)),
            pl.BlockSpec((1, tk, FOX_WIDTH), lambda bi, j: (bi, j, 0)),
            pl.BlockSpec((1, tk, FOX_WIDTH), lambda bi, j: (bi, j, 0)),
            pl.BlockSpec((1, n, FOX_WIDTH), lambda bi, j: (bi, 0, 0)),
            pl.BlockSpec((1, n, FOX_WIDTH), lambda bi, j: (bi, 0, 0)),
            pl.BlockSpec((1, n, N_FOX_HEADS), lambda bi, j: (bi, 0, 0)),
            pl.BlockSpec((1, N_FOX_HEADS, tk), lambda bi, j: (bi, 0, j)),
            pl.BlockSpec((1, N_FOX_HEADS, n), lambda bi, j: (bi, 0, 0)),
        ],
        out_specs=pl.BlockSpec((1, n, FOX_WIDTH), lambda bi, j: (bi, 0, 0)),
        scratch_shapes=[pltpu.VMEM((N_FOX_HEADS, n, 1), F32), pltpu.VMEM((N_FOX_HEADS, n, 1), F32),
                        pltpu.VMEM((n, FOX_WIDTH), F32)],
        compiler_params=_params(("parallel", "arbitrary")), name="fox_sample")(
            q, cache_k, cache_v, k_new, v_new, c_q, c_row_cache, c_row_new)


def _ssm_mats(p):
    f32 = F32
    a_re, a_im = p['ssm_a_re'].astype(f32), p['ssm_a_im'].astype(f32)
    b_re, b_im = p['ssm_b_re'].astype(f32), p['ssm_b_im'].astype(f32)
    c_re, c_im = p['ssm_c_re'].astype(f32), p['ssm_c_im'].astype(f32)
    dt = jnp.exp(p['ssm_log_dt'].astype(f32))[:, None]
    mag = jnp.exp(dt * a_re)
    ab_re = mag * jnp.cos(dt * a_im)
    ab_im = mag * jnp.sin(dt * a_im)
    den = a_re * a_re + a_im * a_im
    nr, ni = ab_re - 1.0, ab_im
    coef_re = (nr * a_re + ni * a_im) / den
    coef_im = (ni * a_re - nr * a_im) / den
    bb_re = coef_re[..., None] * b_re - coef_im[..., None] * b_im
    bb_im = coef_re[..., None] * b_im + coef_im[..., None] * b_re
    pr, pi = [jnp.ones_like(ab_re)], [jnp.zeros_like(ab_im)]
    for _ in range(SSM_CHUNK):
        pr.append(pr[-1] * ab_re - pi[-1] * ab_im)
        pi.append(pr[-2] * ab_im + pi[-1] * ab_re)
    pw_re, pw_im = jnp.stack(pr), jnp.stack(pi)
    T = SSM_CHUNK
    w_re = pw_re[..., None] * bb_re[None] - pw_im[..., None] * bb_im[None]
    w_im = pw_re[..., None] * bb_im[None] + pw_im[..., None] * bb_re[None]
    kk = (jnp.einsum('gop,kgpi->kgoi', c_re, w_re[:T], precision='highest')
          - jnp.einsum('gop,kgpi->kgoi', c_im, w_im[:T], precision='highest'))
    s_idx = jnp.arange(T)[:, None]
    t_idx = jnp.arange(T)[None, :]
    lag = t_idx - s_idx
    kg = kk[jnp.clip(lag, 0, T - 1)]
    kg = jnp.where((lag >= 0)[:, :, None, None, None], kg, 0.0)
    kmat = jnp.transpose(kg, (2, 0, 4, 1, 3)).reshape(N_SSM_GROUPS, SSM_FEAT, SSM_FEAT)
    m_re = jnp.transpose(w_re[T - 1 - jnp.arange(T)], (1, 0, 3, 2)).reshape(N_SSM_GROUPS, SSM_FEAT, SSM_STATE)
    m_im = jnp.transpose(w_im[T - 1 - jnp.arange(T)], (1, 0, 3, 2)).reshape(N_SSM_GROUPS, SSM_FEAT, SSM_STATE)
    ar, ai = pw_re[1:], pw_im[1:]
    n_re = (c_re[None] * ar[:, :, None, :] - c_im[None] * ai[:, :, None, :])
    n_im = -(c_re[None] * ai[:, :, None, :] + c_im[None] * ar[:, :, None, :])
    n_re = jnp.transpose(n_re, (1, 3, 0, 2)).reshape(N_SSM_GROUPS, SSM_STATE, SSM_FEAT)
    n_im = jnp.transpose(n_im, (1, 3, 0, 2)).reshape(N_SSM_GROUPS, SSM_STATE, SSM_FEAT)

    def pair_diag(m):
        g, r, c = m.shape
        m = m.reshape(g // 2, 2, r, c)
        z = jnp.zeros_like(m[:, 0])
        top = jnp.concatenate([m[:, 0], z], axis=2)
        bot = jnp.concatenate([z, m[:, 1]], axis=2)
        return jnp.concatenate([top, bot], axis=1)

    mre_hi, mre_lo = _split_bf16(pair_diag(m_re))
    mim_hi, mim_lo = _split_bf16(pair_diag(m_im))
    d_feat = jnp.broadcast_to(p['ssm_d'].astype(f32).reshape(N_SSM_GROUPS, 1, SSM_GROUP),
                              (N_SSM_GROUPS, T, SSM_GROUP)).reshape(1, N_SSM_GROUPS * SSM_FEAT)
    return dict(
        kmat=kmat.reshape(N_SSM_GROUPS // 2, 2, SSM_FEAT, SSM_FEAT).astype(BF16),
        mre_hi=mre_hi, mre_lo=mre_lo, mim_hi=mim_hi, mim_lo=mim_lo,
        nre=pair_diag(n_re).astype(BF16), nim=pair_diag(n_im).astype(BF16),
        a16_re=pw_re[T].reshape(8, 256), a16_im=pw_im[T].reshape(8, 256), d_feat=d_feat)


def _ssm_local_kernel(u_ref, mrh_ref, mrl_ref, mih_ref, mil_ref, hre_ref, him_ref):
    hi, lo = _split_bf16(u_ref[...])
    hre_ref[...] = _dot(hi, mrh_ref[0]) + _dot(hi, mrl_ref[0]) + _dot(lo, mrh_ref[0])
    him_ref[...] = _dot(hi, mih_ref[0]) + _dot(hi, mil_ref[0]) + _dot(lo, mih_ref[0])


def _ssm_local(ug, mats):
    r = ug.shape[0]
    npair = N_SSM_GROUPS // 2
    mspec = pl.BlockSpec((1, 2 * SSM_FEAT, 2 * SSM_STATE), lambda g: (g, 0, 0))
    ospec = pl.BlockSpec((r, 2 * SSM_STATE), lambda g: (0, g))
    return pl.pallas_call(
        _ssm_local_kernel,
        out_shape=(jax.ShapeDtypeStruct((r, N_SSM_GROUPS * SSM_STATE), F32),) * 2,
        grid=(npair,),
        in_specs=[pl.BlockSpec((r, 2 * SSM_FEAT), lambda g: (0, g)), mspec, mspec, mspec, mspec],
        out_specs=(ospec, ospec),
        compiler_params=_params(("parallel",)), name="ssm_local")(
            ug, mats['mre_hi'], mats['mre_lo'], mats['mim_hi'], mats['mim_lo'])


def _ssm_scan_kernel(lre_ref, lim_ref, are_ref, aim_ref, h0re_ref, h0im_ref,
                     pre_ref, pim_ref, fre_ref, fim_ref):
    nchunk = lre_ref.shape[1]
    ar, ai = are_ref[...], aim_ref[...]

    def body(c, carry):
        hr, hi = carry
        pre_ref[0, c] = hr
        pim_ref[0, c] = hi
        return (ar * hr - ai * hi + lre_ref[0, c], ar * hi + ai * hr + lim_ref[0, c])

    hr, hi = lax.fori_loop(0, nchunk, body, (h0re_ref[0], h0im_ref[0]))
    fre_ref[0] = hr
    fim_ref[0] = hi


def _ssm_scan(hloc_re, hloc_im, mats, h0_re, h0_im):
    b, nchunk = hloc_re.shape[:2]
    big = pl.BlockSpec((1, nchunk, 8, 256), lambda i: (i, 0, 0, 0))
    small = pl.BlockSpec((1, 8, 256), lambda i: (i, 0, 0))
    return pl.pallas_call(
        _ssm_scan_kernel,
        out_shape=(jax.ShapeDtypeStruct(hloc_re.shape, F32),) * 2 + (jax.ShapeDtypeStruct((b, 8, 256), F32),) * 2,
        grid=(b,),
        in_specs=[big, big, _full((8, 256)), _full((8, 256)), small, small],
        out_specs=(big, big, small, small),
        compiler_params=_params(("parallel",)), name="ssm_scan")(
            hloc_re, hloc_im, mats['a16_re'], mats['a16_im'], h0_re, h0_im)


def _gelu_tanh(y):
    return 0.5 * y * (1.0 + jnp.tanh(math.sqrt(2.0 / math.pi) * (y + 0.044715 * (y * y * y))))


def _ssm_out_kernel(u_ref, k_ref, pre_ref, pim_ref, nre_ref, nim_ref, d_ref, y_ref):
    u = u_ref[...]
    ub = u.astype(BF16)
    y = jnp.concatenate([_dot(ub[:, 0:SSM_FEAT], k_ref[0, 0]), _dot(ub[:, SSM_FEAT:2 * SSM_FEAT], k_ref[0, 1])],
                        axis=-1)
    y = y + _dot(pre_ref[...].astype(BF16), nre_ref[0]) + _dot(pim_ref[...].astype(BF16), nim_ref[0])
    y = y + d_ref[...] * u
    y_ref[...] = _gelu_tanh(y).astype(y_ref.dtype)


def _ssm_out(ug, hprev_re, hprev_im, mats):
    r = ug.shape[0]
    npair = N_SSM_GROUPS // 2
    uspec = pl.BlockSpec((r, 2 * SSM_FEAT), lambda g: (0, g))
    hspec = pl.BlockSpec((r, 2 * SSM_STATE), lambda g: (0, g))
    nspec = pl.BlockSpec((1, 2 * SSM_STATE, 2 * SSM_FEAT), lambda g: (g, 0, 0))
    return pl.pallas_call(
        _ssm_out_kernel,
        out_shape=jax.ShapeDtypeStruct(ug.shape, BF16),
        grid=(npair,),
        in_specs=[uspec, pl.BlockSpec((1, 2, SSM_FEAT, SSM_FEAT), lambda g: (g, 0, 0, 0)), hspec, hspec,
                  nspec, nspec, pl.BlockSpec((1, 2 * SSM_FEAT), lambda g: (0, g))],
        out_specs=uspec,
        compiler_params=_params(("parallel",)), name="ssm_out")(
            ug, mats['kmat'], hprev_re, hprev_im, mats['nre'], mats['nim'], mats['d_feat'])


def _ssm(u, h0_re, h0_im, mats):
    b, s, _ = u.shape
    nchunk = s // SSM_CHUNK
    r = b * nchunk
    ug = u.reshape(r, SSM_CHUNK, N_SSM_GROUPS, SSM_GROUP).transpose(0, 2, 1, 3).reshape(r, N_SSM_GROUPS * SSM_FEAT)
    hloc_re, hloc_im = _ssm_local(ug, mats)
    shp = (b, nchunk, 8, 256)
    hprev_re, hprev_im, f_re, f_im = _ssm_scan(hloc_re.reshape(shp), hloc_im.reshape(shp), mats,
                                               h0_re.reshape(b, 8, 256), h0_im.reshape(b, 8, 256))
    yg = _ssm_out(ug, hprev_re.reshape(r, -1), hprev_im.reshape(r, -1), mats)
    y = yg.reshape(r, N_SSM_GROUPS, SSM_CHUNK, SSM_GROUP).transpose(0, 2, 1, 3).reshape(b * s, SSM_WIDTH)
    return y, f_re.reshape(b, N_SSM_GROUPS, SSM_STATE), f_im.reshape(b, N_SSM_GROUPS, SSM_STATE)


def _merge_kernel(x_ref, of_ref, ys_ref, qm_ref, gate_ref, mk_ref, mv_ref,
                  wglu_ref, wbf_ref, wbs_ref, wbm_ref, wo_ref, nf_ref, wr_ref,
                  x1_ref, h2_ref, comb_ref):
    tm = x_ref.shape[0]
    om = []
    for hd in range(N_MEM_HEADS):
        sl = slice(hd * MEM_HEAD_DIM, (hd + 1) * MEM_HEAD_DIM)
        kh = mk_ref[0, :, sl].astype(BF16)
        vh = mv_ref[0, :, sl].astype(BF16)
        sc = _dot_nt(qm_ref[:, sl], kh)
        p = jnp.exp(sc - jnp.max(sc, axis=-1, keepdims=True))
        om.append(_dot(p.astype(BF16), vh) / jnp.sum(p, axis=-1, keepdims=True))
    o_mem = jnp.concatenate(om, axis=-1).astype(BF16)
    z = _dot(ys_ref[...], wglu_ref[...])
    y_ssm = (z[:, 0:SSM_WIDTH] * jax.nn.sigmoid(z[:, SSM_WIDTH:2 * SSM_WIDTH])).astype(BF16)
    g = lambda c: gate_ref[:, c * D_MODEL:(c + 1) * D_MODEL].astype(F32)
    merged = (g(0) * _dot(of_ref[...], wbf_ref[...]) + g(1) * _dot(y_ssm, wbs_ref[...])
              + g(2) * _dot(o_mem, wbm_ref[...]))
    x1 = x_ref[...] + _dot(merged.astype(BF16), wo_ref[...])
    x1_ref[...] = x1
    h2 = x1 * lax.rsqrt(jnp.mean(x1 * x1, axis=-1, keepdims=True) + RMS_EPS) * nf_ref[...]
    h2_ref[...] = h2.astype(BF16)
    logits = _dot_exact(h2, wr_ref[...])
    lane = lax.broadcasted_iota(jnp.int32, (tm, LANES), 1)
    big = jnp.int32(LANES)
    is_grp = (lane >= N_EXPERTS) & (lane < N_EXPERTS + N_EXPERT_GROUPS)
    gl = jnp.where(is_grp, logits, NEG_INF)
    gmax = jnp.max(gl, axis=-1, keepdims=True)
    grp = jnp.min(jnp.where(is_grp & (gl == gmax), lane, big), axis=-1, keepdims=True) - N_EXPERTS
    g_w = 1.0 / jnp.sum(jnp.where(is_grp, jnp.exp(gl - gmax), 0.0), axis=-1, keepdims=True)
    in_grp = (lane >= grp * EXPERTS_PER_GROUP) & (lane < (grp + 1) * EXPERTS_PER_GROUP)
    e1 = jnp.where(in_grp, logits, NEG_INF)
    m1 = jnp.max(e1, axis=-1, keepdims=True)
    i1 = jnp.min(jnp.where(in_grp & (e1 == m1), lane, big), axis=-1, keepdims=True)
    rest = in_grp & (lane != i1)
    e2 = jnp.where(rest, logits, NEG_INF)
    m2 = jnp.max(e2, axis=-1, keepdims=True)
    i2 = jnp.min(jnp.where(rest & (e2 == m2), lane, big), axis=-1, keepdims=True)
    ex = jnp.exp(m2 - m1)
    w1 = g_w / (1.0 + ex)
    w2 = g_w * ex / (1.0 + ex)
    comb_ref[...] = jnp.where(lane == i1, w1, jnp.where(lane == i2, w2, 0.0))


def _merge(x2d, o_fox, ys, q_m, gates, mem_k, mem_v, w, tm, rows_per_batch):
    n = x2d.shape[0]
    assert n % tm == 0 and rows_per_batch % tm == 0
    per = rows_per_batch // tm
    row = lambda width: pl.BlockSpec((tm, width), lambda i: (i, 0))
    memspec = pl.BlockSpec((1, N_MEM, MEM_WIDTH), lambda i: (i // per, 0, 0))
    ws = [w['w_glu'], w['w_br_fox'], w['w_br_ssm'], w['w_br_mem'], w['w_out'], w['norm_ffn'], w['w_router']]
    return pl.pallas_call(
        _merge_kernel,
        out_shape=(jax.ShapeDtypeStruct((n, D_MODEL), F32), jax.ShapeDtypeStruct((n, D_MODEL), BF16),
                   jax.ShapeDtypeStruct((n, LANES), F32)),
        grid=(n // tm,),
        in_specs=[row(D_MODEL), row(FOX_WIDTH), row(SSM_WIDTH), row(MEM_WIDTH), row(3 * D_MODEL), memspec, memspec]
                 + [_full(a.shape) for a in ws],
        out_specs=(row(D_MODEL), row(D_MODEL), row(LANES)),
        compiler_params=_params(("parallel",)), name="merge")(
            x2d, o_fox, ys, q_m, gates, mem_k, mem_v, *ws)


def _moe_kernel(h_ref, comb_ref, x1_ref, wg_ref, wu_ref, wd_ref, o_ref, acc_ref):
    e = pl.program_id(1)

    @pl.when(e == 0)
    def _():
        acc_ref[...] = jnp.zeros(acc_ref.shape, F32)

    h = h_ref[...]
    a = _dot(h, wg_ref[0])
    up = _dot(h, wu_ref[0])
    lane = lax.broadcasted_iota(jnp.int32, comb_ref.shape, 1)
    ce = jnp.sum(jnp.where(lane == e, comb_ref[...], 0.0), axis=-1, keepdims=True)
    act = a * jax.nn.sigmoid(a) * up * ce
    acc_ref[...] += _dot(act.astype(BF16), wd_ref[0])

    @pl.when(e == pl.num_programs(1) - 1)
    def _():
        o_ref[...] = x1_ref[...] + acc_ref[...]


def _moe(h2, comb, x1, wg, wu, wd, tm):
    n = h2.shape[0]
    assert n % tm == 0
    row = lambda width: pl.BlockSpec((tm, width), lambda i, e: (i, 0))
    return pl.pallas_call(
        _moe_kernel,
        out_shape=jax.ShapeDtypeStruct((n, D_MODEL), F32),
        grid=(n // tm, N_EXPERTS),
        in_specs=[row(D_MODEL), row(LANES), row(D_MODEL),
                  pl.BlockSpec((1, D_MODEL, D_EXPERT), lambda i, e: (e, 0, 0)),
                  pl.BlockSpec((1, D_MODEL, D_EXPERT), lambda i, e: (e, 0, 0)),
                  pl.BlockSpec((1, D_EXPERT, D_MODEL), lambda i, e: (e, 0, 0))],
        out_specs=row(D_MODEL),
        scratch_shapes=[pltpu.VMEM((tm, D_MODEL), F32)],
        compiler_params=_params(("parallel", "arbitrary")), name="moe")(h2, comb, x1, wg, wu, wd)


def _prep_weights(p):
    w_in = p['w_in'].astype(BF16)
    o = 0
    wqkv = w_in[:, 0:3 * FOX_WIDTH]
    o = 3 * FOX_WIDTH
    wf = jnp.pad(w_in[:, o:o + N_FOX_HEADS], ((0, 0), (0, LANES - N_FOX_HEADS)))
    o += N_FOX_HEADS
    wqm = w_in[:, o:o + MEM_WIDTH]
    o += MEM_WIDTH
    wu = w_in[:, o:o + SSM_WIDTH]
    o += SSM_WIDTH
    wg = w_in[:, o:o + 3 * D_MODEL]
    r = jnp.arange(FOX_WIDTH) // FOX_HEAD_DIM
    bd = (r[:, None] == r[None, :]).astype(BF16)
    w_router = jnp.concatenate(
        [p['w_router_expert'], p['w_router_group'],
         jnp.zeros((D_MODEL, LANES - N_EXPERTS - N_EXPERT_GROUPS), F32)], axis=1)
    return dict(
        norm_mix=p['norm_mix'].reshape(1, D_MODEL), wqkv=wqkv, wf=wf,
        bf=jnp.pad(p['b_forget'], (0, LANES - N_FOX_HEADS)).reshape(1, LANES),
        wqm=wqm, wu=wu, wg=wg,
        qn_fox=jnp.tile(p['qn_fox'], N_FOX_HEADS).reshape(1, FOX_WIDTH),
        kn_fox=jnp.tile(p['kn_fox'], N_FOX_HEADS).reshape(1, FOX_WIDTH),
        qn_mem=p['qn_mem'].reshape(1, MEM_HEAD_DIM), bd=bd,
        w_glu=p['w_glu'].astype(BF16), w_br_fox=p['w_br_fox'].astype(BF16),
        w_br_ssm=p['w_br_ssm'].astype(BF16), w_br_mem=p['w_br_mem'].astype(BF16),
        w_out=p['w_out'].astype(BF16), norm_ffn=p['norm_ffn'].reshape(1, D_MODEL), w_router=w_router,
        moe_wg=p['moe_w_gate'].astype(BF16), moe_wu=p['moe_w_up'].astype(BF16),
        moe_wd=p['moe_w_down'].astype(BF16))


def _pick_tile(n, target):
    t = min(n, target)
    while n % t:
        t //= 2
    return t


def _group(x, w, mats, mem_k, mem_v, h0_re, h0_im, cache):
    b, s, _ = x.shape
    n = b * s
    x2d = x.reshape(n, D_MODEL)
    q, k, v, logf, q_m, u, gates = _inproj(x2d, w, _pick_tile(n, 512))
    q3 = q.reshape(b, s, FOX_WIDTH)
    k3 = k.reshape(b, s, FOX_WIDTH)
    v3 = v.reshape(b, s, FOX_WIDTH)
    lf3 = logf.reshape(b, s, N_FOX_HEADS)
    npair = N_FOX_HEADS // 2
    if cache is None:
        c_row = _cumsum_rows(lf3.transpose(0, 2, 1).reshape(b * N_FOX_HEADS, s)).reshape(b, npair, 2, s)
        c_col = c_row.transpose(0, 1, 3, 2)
        o_fox = _fox_prompt(q3, k3, v3, c_col, c_row, _pick_tile(s, 256), _pick_tile(s, 512))
    else:
        cache_k, cache_v, cache_logf = cache
        past = cache_k.shape[1]
        lf_all = jnp.concatenate([cache_logf.astype(F32).transpose(0, 2, 1), lf3.transpose(0, 2, 1)], axis=2)
        c_row = _cumsum_rows(lf_all.reshape(b * N_FOX_HEADS, past + s)).reshape(b, N_FOX_HEADS, past + s)
        o_fox = _fox_sample(q3, cache_k.reshape(b, past, FOX_WIDTH), cache_v.reshape(b, past, FOX_WIDTH),
                            k3, v3, c_row[:, :, past:].transpose(0, 2, 1), c_row[:, :, :past], c_row[:, :, past:],
                            _pick_tile(past, 1024))
    ys, f_re, f_im = _ssm(u.reshape(b, s, SSM_WIDTH), h0_re, h0_im, mats)
    tm = _pick_tile(s, 512)
    x1, h2, comb = _merge(x2d, o_fox.reshape(n, FOX_WIDTH), ys, q_m, gates, mem_k, mem_v, w, tm, s)
    y = _moe(h2, comb, x1, w['moe_wg'], w['moe_wu'], w['moe_wd'], _pick_tile(n, 1024))
    return (y.reshape(b, s, D_MODEL), k3.reshape(b, s, N_FOX_HEADS, FOX_HEAD_DIM),
            v3.reshape(b, s, N_FOX_HEADS, FOX_HEAD_DIM), lf3, f_re, f_im)


def kernel(x_prompt, x_sample, mem_prompt, cache_fox_k, cache_fox_v, cache_fox_logf, state_ssm_re, state_ssm_im,
           cache_mem_k, cache_mem_v, norm_mix, w_in, b_forget, qn_fox, kn_fox, qn_mem, kn_mem, norm_mem, w_mem_kv,
           ssm_a_re, ssm_a_im, ssm_log_dt, ssm_b_re, ssm_b_im, ssm_c_re, ssm_c_im, ssm_d, w_glu, w_br_fox,
           w_br_ssm, w_br_mem, w_out, norm_ffn, w_router_group, w_router_expert, moe_w_gate, moe_w_up,
           moe_w_down):
    depth = norm_mix.shape[0]
    assert depth == 1
    l = 0
    p = dict(norm_mix=norm_mix[l], w_in=w_in[l], b_forget=b_forget[l], qn_fox=qn_fox[l], kn_fox=kn_fox[l],
             qn_mem=qn_mem[l], ssm_a_re=ssm_a_re[l], ssm_a_im=ssm_a_im[l], ssm_log_dt=ssm_log_dt[l],
             ssm_b_re=ssm_b_re[l], ssm_b_im=ssm_b_im[l], ssm_c_re=ssm_c_re[l], ssm_c_im=ssm_c_im[l],
             ssm_d=ssm_d[l], w_glu=w_glu[l], w_br_fox=w_br_fox[l], w_br_ssm=w_br_ssm[l], w_br_mem=w_br_mem[l],
             w_out=w_out[l], norm_ffn=norm_ffn[l], w_router_group=w_router_group[l],
             w_router_expert=w_router_expert[l], moe_w_gate=moe_w_gate[l], moe_w_up=moe_w_up[l],
             moe_w_down=moe_w_down[l])
    w = _prep_weights(p)
    mats = _ssm_mats(p)
    bp, sp, _ = x_prompt.shape
    bs, ss, _ = x_sample.shape

    mk, mv = _memkv(mem_prompt.reshape(bp * N_MEM, D_MODEL), norm_mem[l].reshape(1, D_MODEL),
                    w_mem_kv[l].astype(BF16), kn_mem[l].reshape(1, MEM_HEAD_DIM), _pick_tile(bp * N_MEM, 512))
    mk = mk.reshape(bp, N_MEM, MEM_WIDTH)
    mv = mv.reshape(bp, N_MEM, MEM_WIDTH)
    zeros = jnp.zeros((bp, N_SSM_GROUPS, SSM_STATE), F32)
    yp, pk, pv, plf, pre, pim = _group(x_prompt, w, mats, mk, mv, zeros, zeros, None)
    cache = (cache_fox_k[l], cache_fox_v[l], cache_fox_logf[l])
    ys, sk, sv, slf, sre, sim = _group(
        x_sample, w, mats, cache_mem_k[l].reshape(bs, N_MEM, MEM_WIDTH), cache_mem_v[l].reshape(bs, N_MEM, MEM_WIDTH),
        state_ssm_re[l].astype(F32), state_ssm_im[l].astype(F32), cache)
    st = lambda a: a[None]
    return (yp, ys, st(pk), st(pv), st(plf), st(pre), st(pim),
            st(mk.reshape(bp, N_MEM, N_MEM_HEADS, MEM_HEAD_DIM)), st(mv.reshape(bp, N_MEM, N_MEM_HEADS, MEM_HEAD_DIM)),
            st(sk), st(sv), st(slf), st(sre), st(sim))
```

```python
import functools
import math

import jax
import jax.numpy as jnp
from jax import lax
from jax.experimental import pallas as pl
from jax.experimental.pallas import tpu as pltpu

F32 = jnp.float32
BF16 = jnp.bfloat16

D_MODEL = 1024
N_FOX_HEADS = 8
FOX_HEAD_DIM = 64
FOX_WIDTH = N_FOX_HEADS * FOX_HEAD_DIM
N_MEM = 256
N_MEM_HEADS = 4
MEM_HEAD_DIM = 128
MEM_WIDTH = N_MEM_HEADS * MEM_HEAD_DIM
SSM_GROUP = 16
SSM_WIDTH = 512
N_SSM_GROUPS = SSM_WIDTH // SSM_GROUP
SSM_STATE = 64
N_EXPERT_GROUPS = 4
EXPERTS_PER_GROUP = 8
N_EXPERTS = N_EXPERT_GROUPS * EXPERTS_PER_GROUP
D_EXPERT = 256
RMS_EPS = 1e-6
NEG_INF = -1e30

LANES = 128
SSM_CHUNK = 16
SSM_GPB = LANES // SSM_GROUP
VMEM_LIMIT = 56 * 1024 * 1024


def _dot(a, b):
    return jnp.dot(a, b, preferred_element_type=F32)


def _dot_nt(a, b):
    return lax.dot_general(a, b, (((1,), (1,)), ((), ())), preferred_element_type=F32)


def _dot_exact(a, b):
    return jnp.dot(a, b, preferred_element_type=F32, precision=lax.Precision.HIGHEST)


def _split_bf16(x):
    hi = x.astype(BF16)
    lo = (x - hi.astype(F32)).astype(BF16)
    return hi, lo


def _params(sem):
    return pltpu.CompilerParams(dimension_semantics=sem, vmem_limit_bytes=VMEM_LIMIT)


def _full(shape):
    n = len(shape)
    return pl.BlockSpec(shape, lambda *_: (0,) * n)


def _inproj_kernel(x_ref, g_ref, wqkv_ref, wf_ref, bf_ref, wqm_ref, wu_ref, wg_ref,
                   qn_ref, kn_ref, qmn_ref, bd_ref,
                   q_ref, kb_ref, vb_ref, k_ref, v_ref, lf_ref, qm_ref, u_ref, gate_ref):
    x = x_ref[...]
    h = x * lax.rsqrt(jnp.mean(x * x, axis=-1, keepdims=True) + RMS_EPS) * g_ref[...]
    hb = h.astype(BF16)

    def head_norm(z, gain):
        hi, lo = _split_bf16(z * z)
        ss = _dot(hi, bd_ref[...]) + _dot(lo, bd_ref[...])
        return z * lax.rsqrt(ss * (1.0 / FOX_HEAD_DIM) + RMS_EPS) * gain

    zq = _dot(hb, wqkv_ref[:, 0:FOX_WIDTH])
    q_ref[...] = (head_norm(zq, qn_ref[...]) * (FOX_HEAD_DIM ** -0.5)).astype(BF16)
    zk = _dot(hb, wqkv_ref[:, FOX_WIDTH:2 * FOX_WIDTH])
    kn = head_norm(zk, kn_ref[...])
    zv = _dot(hb, wqkv_ref[:, 2 * FOX_WIDTH:3 * FOX_WIDTH])
    kb_ref[...] = kn.astype(BF16)
    vb_ref[...] = zv.astype(BF16)
    tm = x_ref.shape[0]
    for hd in range(N_FOX_HEADS):
        hs = slice(hd * FOX_HEAD_DIM, (hd + 1) * FOX_HEAD_DIM)
        rows = pl.ds(hd, tm, stride=N_FOX_HEADS)
        k_ref[rows, :] = kn[:, hs]
        v_ref[rows, :] = zv[:, hs]

    zf = _dot(hb, wf_ref[...]) + bf_ref[...]
    logf = jnp.minimum(zf, 0.0) - jnp.log1p(jnp.exp(-jnp.abs(zf)))
    lf_ref[...] = logf[:, 0:N_FOX_HEADS]

    zm = _dot(hb, wqm_ref[...])
    for hd in range(N_MEM_HEADS):
        sl = slice(hd * MEM_HEAD_DIM, (hd + 1) * MEM_HEAD_DIM)
        zh = zm[:, sl]
        ms = jnp.mean(zh * zh, axis=-1, keepdims=True)
        qm_ref[:, sl] = (zh * lax.rsqrt(ms + RMS_EPS) * qmn_ref[...] * (MEM_HEAD_DIM ** -0.5)).astype(BF16)

    u_ref[...] = _dot(hb, wu_ref[...])
    for c in range(3):
        sl = slice(c * D_MODEL, (c + 1) * D_MODEL)
        gate_ref[:, sl] = jax.nn.sigmoid(_dot(hb, wg_ref[:, sl])).astype(BF16)


def _inproj(x2d, w, tm):
    n = x2d.shape[0]
    assert n % tm == 0
    row = lambda width: pl.BlockSpec((tm, width), lambda i: (i, 0))
    ins = [x2d, w['norm_mix'], w['wqkv'], w['wf'], w['bf'], w['wqm'], w['wu'], w['wg'],
           w['qn_fox'], w['kn_fox'], w['qn_mem'], w['bd']]
    in_specs = [row(D_MODEL)] + [_full(a.shape) for a in ins[1:]]
    out_shape = (
        jax.ShapeDtypeStruct((n, FOX_WIDTH), BF16),
        jax.ShapeDtypeStruct((n, FOX_WIDTH), BF16),
        jax.ShapeDtypeStruct((n, FOX_WIDTH), BF16),
        jax.ShapeDtypeStruct((n * N_FOX_HEADS, FOX_HEAD_DIM), F32),
        jax.ShapeDtypeStruct((n * N_FOX_HEADS, FOX_HEAD_DIM), F32),
        jax.ShapeDtypeStruct((n, N_FOX_HEADS), F32),
        jax.ShapeDtypeStruct((n, MEM_WIDTH), BF16),
        jax.ShapeDtypeStruct((n, SSM_WIDTH), F32),
        jax.ShapeDtypeStruct((n, 3 * D_MODEL), BF16),
    )
    heads = pl.BlockSpec((tm * N_FOX_HEADS, FOX_HEAD_DIM), lambda i: (i, 0))
    out_specs = (row(FOX_WIDTH), row(FOX_WIDTH), row(FOX_WIDTH), heads, heads, row(N_FOX_HEADS),
                 row(MEM_WIDTH), row(SSM_WIDTH), row(3 * D_MODEL))
    return pl.pallas_call(
        _inproj_kernel, out_shape=out_shape, grid=(n // tm,), in_specs=in_specs, out_specs=out_specs,
        compiler_params=_params(("parallel",)), name="inproj")(*ins)


def _memkv_kernel(x_ref, g_ref, w_ref, kn_ref, k_ref, v_ref):
    x = x_ref[...]
    h = x * lax.rsqrt(jnp.mean(x * x, axis=-1, keepdims=True) + RMS_EPS) * g_ref[...]
    hb = h.astype(BF16)
    zk = _dot(hb, w_ref[:, 0:MEM_WIDTH])
    for hd in range(N_MEM_HEADS):
        sl = slice(hd * MEM_HEAD_DIM, (hd + 1) * MEM_HEAD_DIM)
        zh = zk[:, sl]
        ms = jnp.mean(zh * zh, axis=-1, keepdims=True)
        k_ref[:, sl] = zh * lax.rsqrt(ms + RMS_EPS) * kn_ref[...]
    v_ref[...] = _dot(hb, w_ref[:, MEM_WIDTH:2 * MEM_WIDTH])


def _memkv(mem2d, norm_mem, w_kv, kn_mem, tm):
    n = mem2d.shape[0]
    row = lambda width: pl.BlockSpec((tm, width), lambda i: (i, 0))
    return pl.pallas_call(
        _memkv_kernel,
        out_shape=(jax.ShapeDtypeStruct((n, MEM_WIDTH), F32), jax.ShapeDtypeStruct((n, MEM_WIDTH), F32)),
        grid=(n // tm,),
        in_specs=[row(D_MODEL), _full(norm_mem.shape), _full(w_kv.shape), _full(kn_mem.shape)],
        out_specs=(row(MEM_WIDTH), row(MEM_WIDTH)),
        compiler_params=_params(("parallel",)), name="memkv")(mem2d, norm_mem, w_kv, kn_mem)


CUMSUM_BLOCK = 256


def _cumsum_kernel(x_ref, o_ref):
    nblk = x_ref.shape[1] // CUMSUM_BLOCK
    r = lax.broadcasted_iota(jnp.int32, (CUMSUM_BLOCK, CUMSUM_BLOCK), 0)
    c = lax.broadcasted_iota(jnp.int32, (CUMSUM_BLOCK, CUMSUM_BLOCK), 1)
    tri = (r <= c).astype(F32)
    carry = jnp.zeros((x_ref.shape[0], 1), F32)
    for j in range(nblk):
        sl = slice(j * CUMSUM_BLOCK, (j + 1) * CUMSUM_BLOCK)
        cs = _dot_exact(x_ref[:, sl], tri) + carry
        o_ref[:, sl] = cs
        carry = cs[:, CUMSUM_BLOCK - 1:CUMSUM_BLOCK]


def _cumsum_rows(x):
    rows, n = x.shape
    npad = -(-n // CUMSUM_BLOCK) * CUMSUM_BLOCK
    xp = jnp.pad(x, ((0, 0), (0, npad - n))) if npad != n else x
    out = pl.pallas_call(
        _cumsum_kernel, out_shape=jax.ShapeDtypeStruct((rows, npad), F32), grid=(1,),
        in_specs=[_full((rows, npad))], out_specs=_full((rows, npad)),
        compiler_params=_params(("arbitrary",)), name="cumsum")(xp)
    return out[:, :n] if npad != n else out


def _reduce_rows(x, op):
    rows, cols = x.shape
    if rows > 64 and rows % 64 == 0:
        x = op(x.reshape(rows // 64, 64, cols), axis=0)
        rows = 64
    if rows == 64:
        x = op(x.reshape(8, 8, cols), axis=0)
    return op(x, axis=0, keepdims=True)


def _head_lane_masks(rows):
    lane = lax.broadcasted_iota(jnp.int32, (rows, LANES), 1)
    return lane < FOX_HEAD_DIM


def _softmax_step(qh, kb, vb, ck, cq, m, l, acc, mask):
    t = _dot_nt(qh, kb) - ck
    if mask is not None:
        t = jnp.where(mask, t, NEG_INF)
    m_new = jnp.maximum(m, jnp.max(t, axis=-1, keepdims=True) + cq)
    alpha = jnp.exp(m - m_new)
    p = jnp.exp(t + (cq - m_new))
    l_new = alpha * l + jnp.sum(p, axis=-1, keepdims=True)
    acc_new = alpha * acc + _dot(p.astype(BF16), vb)
    return m_new, l_new, acc_new


def _fox_prompt_kernel(q_ref, k_ref, v_ref, cc_ref, cr_ref, o_ref,
                       vt_ref, ck0_ref, ck1_ref, st_ref, pt_ref, acc_ref, *, tq, tk):
    i = pl.program_id(2)
    s_len = k_ref.shape[1]

    @pl.when(i == 0)
    def _():
        vt_ref[...] = v_ref[0].astype(F32).T.astype(BF16)
        ck0_ref[...] = jnp.broadcast_to(cc_ref[0, 0, :, 0:1], (s_len, LANES))
        ck1_ref[...] = jnp.broadcast_to(cc_ref[0, 0, :, 1:2], (s_len, LANES))

    qt = q_ref[0].astype(F32).T
    row = lax.broadcasted_iota(jnp.int32, (LANES, tq), 0)
    qts = (jnp.where(row < FOX_HEAD_DIM, qt, 0.0).astype(BF16), jnp.where(row < FOX_HEAD_DIM, 0.0, qt).astype(BF16))
    q0 = pl.multiple_of(i * tq, tq)
    cq = cr_ref[0, 0, :, pl.ds(q0, tq)]
    ck_refs = (ck0_ref, ck1_ref)

    def stage_a(n):
        s = pl.multiple_of(n * tk, tk)
        kb = k_ref[0, pl.ds(s, tk), :]
        for hh in range(2):
            ck = ck_refs[hh][pl.ds(s, tk), :]
            st_ref[n & 1, hh] = _dot(kb, qts[hh]) - jnp.concatenate([ck] * (tq // LANES), axis=1)

    def stage_b(n, stats, masked):
        if masked:
            kpos = n * tk + lax.broadcasted_iota(jnp.int32, (tk, tq), 0)
            qpos = q0 + lax.broadcasted_iota(jnp.int32, (tk, tq), 1)
            mask = kpos <= qpos
        out = []
        for hh in range(2):
            m, l = stats[2 * hh:2 * hh + 2]
            t = st_ref[n & 1, hh]
            if masked:
                t = jnp.where(mask, t, NEG_INF)
            cqh = cq[hh:hh + 1, :]
            m_new = jnp.maximum(m, _reduce_rows(t, jnp.max) + cqh)
            alpha = jnp.exp(m - m_new)
            p = jnp.exp(t + (cqh - m_new))
            pt_ref[n & 1, hh] = p.astype(BF16)
            out.extend([m_new, alpha * l + _reduce_rows(p, jnp.sum), alpha])
        return out

    def stage_c(n, alphas):
        s = pl.multiple_of(jnp.maximum(n, 0) * tk, tk)
        for hh in range(2):
            vt = vt_ref[hh * FOX_HEAD_DIM:(hh + 1) * FOX_HEAD_DIM, pl.ds(s, tk)]
            acc_ref[hh] = alphas[hh] * acc_ref[hh] + _dot(vt, pt_ref[n & 1, hh])

    acc_ref[...] = jnp.zeros(acc_ref.shape, F32)
    pt_ref[1] = jnp.zeros(pt_ref.shape[1:], BF16)
    neg = jnp.full((1, tq), NEG_INF, F32)
    zero = jnp.zeros((1, tq), F32)
    one = jnp.ones((1, tq), F32)
    nfull = (i * tq) // tk
    stage_a(0)

    def body(n, carry):
        m0, l0, al0, m1, l1, al1 = carry
        stage_c(n - 1, (al0, al1))
        new = stage_b(n, (m0, l0, m1, l1), False)
        stage_a(n + 1)
        return tuple(new)

    m0, l0, al0, m1, l1, al1 = lax.fori_loop(0, nfull, body, (neg, zero, one, neg, zero, one))
    stage_c(nfull - 1, (al0, al1))
    _, l0, be0, _, l1, be1 = stage_b(nfull, (m0, l0, m1, l1), True)
    stage_c(nfull, (be0, be1))
    ot = jnp.concatenate([acc_ref[0] / l0, acc_ref[1] / l1], axis=0)
    o_ref[0] = ot.T.astype(o_ref.dtype)


def _fox_prompt(q, k, v, c_col, c_row, tq, tk):
    b, s, _ = q.shape
    assert s % tk == 0 and tk % tq == 0
    npair = N_FOX_HEADS // 2
    return pl.pallas_call(
        functools.partial(_fox_prompt_kernel, tq=tq, tk=tk),
        out_shape=jax.ShapeDtypeStruct((b, s, FOX_WIDTH), BF16),
        grid=(b, npair, s // tq),
        in_specs=[
            pl.BlockSpec((1, tq, LANES), lambda bi, hp, i: (bi, i, hp)),
            pl.BlockSpec((1, s, LANES), lambda bi, hp, i: (bi, 0, hp)),
            pl.BlockSpec((1, s, LANES), lambda bi, hp, i: (bi, 0, hp)),
            pl.BlockSpec((1, 1, s, 2), lambda bi, hp, i: (bi, hp, 0, 0)),
            pl.BlockSpec((1, 1, 2, s), lambda bi, hp, i: (bi, hp, 0, 0)),
        ],
        out_specs=pl.BlockSpec((1, tq, LANES), lambda bi, hp, i: (bi, i, hp)),
        scratch_shapes=[pltpu.VMEM((LANES, s), BF16),
                        pltpu.VMEM((s, LANES), F32), pltpu.VMEM((s, LANES), F32),
                        pltpu.VMEM((2, 2, tk, tq), F32), pltpu.VMEM((2, 2, tk, tq), BF16),
                        pltpu.VMEM((2, FOX_HEAD_DIM, tq), F32)],
        compiler_params=_params(("parallel", "parallel", "arbitrary")), name="fox_prompt")(q, k, v, c_col, c_row)


def _fox_sample_kernel(q_ref, ck_ref, cv_ref, nk_ref, nv_ref, cq_ref, crc_ref, crn_ref, o_ref, *state, n):
    j = pl.program_id(1)
    nj = pl.num_programs(1)
    m_refs = state[0:N_FOX_HEADS]
    l_refs = state[N_FOX_HEADS:2 * N_FOX_HEADS]
    acc_refs = state[2 * N_FOX_HEADS:3 * N_FOX_HEADS]

    @pl.when(j == 0)
    def _():
        for hd in range(N_FOX_HEADS):
            m_refs[hd][...] = jnp.full(m_refs[hd].shape, NEG_INF, F32)
            l_refs[hd][...] = jnp.zeros(l_refs[hd].shape, F32)
            acc_refs[hd][...] = jnp.zeros(acc_refs[hd].shape, F32)

    def update(k_of, v_of, cr_ref_, mask):
        ts = []
        for hd in range(N_FOX_HEADS):
            hs = slice(hd * FOX_HEAD_DIM, (hd + 1) * FOX_HEAD_DIM)
            t = _dot_nt(q_ref[0, :, hs], k_of(hd)) - cr_ref_[0, hd:hd + 1, :]
            ts.append(t if mask is None else jnp.where(mask, t, NEG_INF))
        ps = []
        for hd in range(N_FOX_HEADS):
            cq = cq_ref[0, :, hd:hd + 1]
            m = m_refs[hd][...]
            m_new = jnp.maximum(m, jnp.max(ts[hd], axis=-1, keepdims=True) + cq)
            alpha = jnp.exp(m - m_new)
            p = jnp.exp(ts[hd] + (cq - m_new))
            m_refs[hd][...] = m_new
            l_refs[hd][...] = alpha * l_refs[hd][...] + jnp.sum(p, axis=-1, keepdims=True)
            ps.append((alpha, p.astype(BF16)))
        for hd in range(N_FOX_HEADS):
            alpha, p = ps[hd]
            acc_refs[hd][...] = alpha * acc_refs[hd][...] + _dot(p, v_of(hd))

    tk = crc_ref.shape[2]
    update(lambda hd: ck_ref[0, pl.ds(hd, tk, stride=N_FOX_HEADS), :].astype(BF16),
           lambda hd: cv_ref[0, pl.ds(hd, tk, stride=N_FOX_HEADS), :].astype(BF16), crc_ref, None)

    @pl.when(j == nj - 1)
    def _():
        r = lax.broadcasted_iota(jnp.int32, (n, n), 0)
        c = lax.broadcasted_iota(jnp.int32, (n, n), 1)
        head = lambda ref: (lambda hd: ref[0, :, hd * FOX_HEAD_DIM:(hd + 1) * FOX_HEAD_DIM])
        update(head(nk_ref), head(nv_ref), crn_ref, c <= r)
        for hd in range(N_FOX_HEADS):
            hs = slice(hd * FOX_HEAD_DIM, (hd + 1) * FOX_HEAD_DIM)
            o_ref[0, :, hs] = (acc_refs[hd][...] / l_refs[hd][...]).astype(o_ref.dtype)


def _fox_sample(q, cache_k, cache_v, k_new, v_new, c_q, c_row_cache, c_row_new, tk):
    b, n, _ = q.shape
    past = cache_k.shape[1]
    assert past % tk == 0
    cache_k = cache_k.reshape(b, past * N_FOX_HEADS, FOX_HEAD_DIM)
    cache_v = cache_v.reshape(b, past * N_FOX_HEADS, FOX_HEAD_DIM)
    cache_spec = pl.BlockSpec((1, tk * N_FOX_HEADS, FOX_HEAD_DIM), lambda bi, j: (bi, j, 0))
    return pl.pallas_call(
        functools.partial(_fox_sample_kernel, n=n),
        out_shape=jax.ShapeDtypeStruct((b, n, FOX_WIDTH), BF16),
        grid=(b, past // tk),
        in_specs=[
            pl.BlockSpec((1, n, FOX_WIDTH), lambda bi, j: (bi, 0, 0)),
            cache_spec,
            cache_spec,
            pl.BlockSpec((1, n, FOX_WIDTH), lambda bi, j: (bi, 0, 0)),
            pl.BlockSpec((1, n, FOX_WIDTH), lambda bi, j: (bi, 0, 0)),
            pl.BlockSpec((1, n, N_FOX_HEADS), lambda bi, j: (bi, 0, 0)),
            pl.BlockSpec((1, N_FOX_HEADS, tk), lambda bi, j: (bi, 0, j)),
            pl.BlockSpec((1, N_FOX_HEADS, n), lambda bi, j: (bi, 0, 0)),
        ],
        out_specs=pl.BlockSpec((1, n, FOX_WIDTH), lambda bi, j: (bi, 0, 0)),
        scratch_shapes=([pltpu.VMEM((n, 1), F32)] * (2 * N_FOX_HEADS)
                        + [pltpu.VMEM((n, FOX_HEAD_DIM), F32)] * N_FOX_HEADS),
        compiler_params=_params(("parallel", "arbitrary")), name="fox_sample")(
            q, cache_k, cache_v, k_new, v_new, c_q, c_row_cache, c_row_new)


def _ssm_mats(p):
    f32 = F32
    a_re, a_im = p['ssm_a_re'].astype(f32), p['ssm_a_im'].astype(f32)
    b_re, b_im = p['ssm_b_re'].astype(f32), p['ssm_b_im'].astype(f32)
    c_re, c_im = p['ssm_c_re'].astype(f32), p['ssm_c_im'].astype(f32)
    dt = jnp.exp(p['ssm_log_dt'].astype(f32))[:, None]
    mag = jnp.exp(dt * a_re)
    ab_re = mag * jnp.cos(dt * a_im)
    ab_im = mag * jnp.sin(dt * a_im)
    den = a_re * a_re + a_im * a_im
    nr, ni = ab_re - 1.0, ab_im
    coef_re = (nr * a_re + ni * a_im) / den
    coef_im = (ni * a_re - nr * a_im) / den
    bb_re = coef_re[..., None] * b_re - coef_im[..., None] * b_im
    bb_im = coef_re[..., None] * b_im + coef_im[..., None] * b_re
    pr, pi = [jnp.ones_like(ab_re)], [jnp.zeros_like(ab_im)]
    for _ in range(SSM_CHUNK):
        pr.append(pr[-1] * ab_re - pi[-1] * ab_im)
        pi.append(pr[-2] * ab_im + pi[-1] * ab_re)
    pw_re, pw_im = jnp.stack(pr), jnp.stack(pi)
    T = SSM_CHUNK
    w_re = pw_re[..., None] * bb_re[None] - pw_im[..., None] * bb_im[None]
    w_im = pw_re[..., None] * bb_im[None] + pw_im[..., None] * bb_re[None]
    kk = (jnp.einsum('gop,kgpi->kgoi', c_re, w_re[:T], precision='highest')
          - jnp.einsum('gop,kgpi->kgoi', c_im, w_im[:T], precision='highest'))
    eye = jnp.eye(SSM_GPB, dtype=f32)
    nq = N_SSM_GROUPS // SSM_GPB

    def lane_diag(m):
        lead = m.shape[:-3]
        i, c = m.shape[-2:]
        m = m.reshape(lead + (nq, SSM_GPB, i, 1, c)) * eye[:, None, :, None]
        return m.reshape(lead + (nq, SSM_GPB * i, SSM_GPB * c))

    ktau = lane_diag(jnp.swapaxes(kk, -1, -2))
    ktau = jnp.concatenate([jnp.zeros_like(ktau[:1]), ktau], axis=0)
    units = []
    for dlag in range(T // 2 - 1, -1, -1):
        top = jnp.concatenate([ktau[2 * dlag + 1], ktau[2 * dlag + 2]], axis=-1)
        bot = jnp.concatenate([ktau[2 * dlag], ktau[2 * dlag + 1]], axis=-1)
        units.append(jnp.concatenate([top, bot], axis=-2))
    kstack = jnp.concatenate(units, axis=-2).astype(BF16)
    rev = T - 1 - jnp.arange(T)
    m_re = lane_diag(jnp.swapaxes(w_re[rev], -1, -2))
    m_im = lane_diag(jnp.swapaxes(w_im[rev], -1, -2))
    m_all = jnp.concatenate([m_re, m_im], axis=-1)
    m_all = jnp.swapaxes(m_all, 0, 1).reshape(nq, T * LANES, 2 * SSM_GPB * SSM_STATE)
    m_hi, m_lo = _split_bf16(m_all)
    ar, ai = pw_re[1:], pw_im[1:]
    n_re = (c_re[None] * ar[:, :, None, :] - c_im[None] * ai[:, :, None, :])
    n_im = -(c_re[None] * ai[:, :, None, :] + c_im[None] * ar[:, :, None, :])

    def state_rows(n):
        n = lane_diag(n)
        return jnp.transpose(n, (1, 3, 0, 2)).reshape(nq, SSM_GPB * SSM_STATE, T * LANES)

    n_all = jnp.concatenate([state_rows(n_re), state_rows(n_im)], axis=1).astype(BF16)
    return dict(kstack=kstack, m_hi=m_hi, m_lo=m_lo, n_all=n_all,
                a16_re=pw_re[T].reshape(8, 256), a16_im=pw_im[T].reshape(8, 256),
                d=p['ssm_d'].astype(f32).reshape(1, SSM_WIDTH))


def _chunk_tokens(u_ref, rows):
    return [u_ref[pl.ds(t, rows, stride=SSM_CHUNK), :] for t in range(SSM_CHUNK)]


def _ssm_local_kernel(u_ref, mh_ref, ml_ref, hre_ref, him_ref):
    rows = hre_ref.shape[0]
    parts = [_split_bf16(ut) for ut in _chunk_tokens(u_ref, rows)]
    x_hi = jnp.concatenate([h for h, _ in parts], axis=1)
    x_lo = jnp.concatenate([l for _, l in parts], axis=1)
    h = _dot(x_hi, mh_ref[0]) + _dot(x_hi, ml_ref[0]) + _dot(x_lo, mh_ref[0])
    half = SSM_GPB * SSM_STATE
    hre_ref[...] = h[:, 0:half]
    him_ref[...] = h[:, half:2 * half]


def _ssm_local(u2d, mats, rows):
    n = u2d.shape[0]
    r = n // SSM_CHUNK
    nq = N_SSM_GROUPS // SSM_GPB
    half = SSM_GPB * SSM_STATE
    mspec = pl.BlockSpec((1, SSM_CHUNK * LANES, 2 * half), lambda q, i: (q, 0, 0))
    ospec = pl.BlockSpec((rows, half), lambda q, i: (i, q))
    return pl.pallas_call(
        _ssm_local_kernel,
        out_shape=(jax.ShapeDtypeStruct((r, N_SSM_GROUPS * SSM_STATE), F32),) * 2,
        grid=(nq, r // rows),
        in_specs=[pl.BlockSpec((rows * SSM_CHUNK, LANES), lambda q, i: (i, q)), mspec, mspec],
        out_specs=(ospec, ospec),
        compiler_params=_params(("parallel", "parallel")), name="ssm_local")(u2d, mats['m_hi'], mats['m_lo'])


def _ssm_scan_kernel(lre_ref, lim_ref, are_ref, aim_ref, h0re_ref, h0im_ref,
                     pre_ref, pim_ref, fre_ref, fim_ref):
    nchunk = lre_ref.shape[1]
    ar, ai = are_ref[...], aim_ref[...]

    def body(c, carry):
        hr, hi = carry
        pre_ref[0, c] = hr
        pim_ref[0, c] = hi
        return (ar * hr - ai * hi + lre_ref[0, c], ar * hi + ai * hr + lim_ref[0, c])

    hr, hi = lax.fori_loop(0, nchunk, body, (h0re_ref[0], h0im_ref[0]))
    fre_ref[0] = hr
    fim_ref[0] = hi


def _ssm_scan(hloc_re, hloc_im, mats, h0_re, h0_im):
    b, nchunk = hloc_re.shape[:2]
    big = pl.BlockSpec((1, nchunk, 8, 256), lambda i: (i, 0, 0, 0))
    small = pl.BlockSpec((1, 8, 256), lambda i: (i, 0, 0))
    return pl.pallas_call(
        _ssm_scan_kernel,
        out_shape=(jax.ShapeDtypeStruct(hloc_re.shape, F32),) * 2 + (jax.ShapeDtypeStruct((b, 8, 256), F32),) * 2,
        grid=(b,),
        in_specs=[big, big, _full((8, 256)), _full((8, 256)), small, small],
        out_specs=(big, big, small, small),
        compiler_params=_params(("parallel",)), name="ssm_scan")(
            hloc_re, hloc_im, mats['a16_re'], mats['a16_im'], h0_re, h0_im)


def _gelu_tanh(y):
    return 0.5 * y * (1.0 + jnp.tanh(math.sqrt(2.0 / math.pi) * (y + 0.044715 * (y * y * y))))


def _ssm_out_kernel(u_ref, k_ref, pre_ref, pim_ref, n_ref, d_ref, y_ref, ysc_ref):
    rows = pre_ref.shape[0]
    us = _chunk_tokens(u_ref, rows)
    x = jnp.concatenate([ut.astype(BF16) for ut in us], axis=1)
    hp = jnp.concatenate([pre_ref[...], pim_ref[...]], axis=1).astype(BF16)
    unit = 2 * LANES
    nunit = SSM_CHUNK // 2
    for j in range(nunit):
        yj = (_dot(x[:, 0:unit * (j + 1)], k_ref[0, unit * (nunit - 1 - j):, :])
              + _dot(hp, n_ref[0, :, unit * j:unit * (j + 1)]))
        for t2 in range(2):
            t = 2 * j + t2
            y = yj[:, t2 * LANES:(t2 + 1) * LANES] + d_ref[...] * us[t]
            ysc_ref[pl.ds(t, rows, stride=SSM_CHUNK), :] = _gelu_tanh(y)
    y_ref[...] = ysc_ref[...].astype(y_ref.dtype)


def _ssm_out(u2d, hprev_re, hprev_im, mats, rows):
    n = u2d.shape[0]
    r = n // SSM_CHUNK
    nq = N_SSM_GROUPS // SSM_GPB
    half = SSM_GPB * SSM_STATE
    uspec = pl.BlockSpec((rows * SSM_CHUNK, LANES), lambda q, i: (i, q))
    hspec = pl.BlockSpec((rows, half), lambda q, i: (i, q))
    return pl.pallas_call(
        _ssm_out_kernel,
        out_shape=jax.ShapeDtypeStruct((n, SSM_WIDTH), BF16),
        grid=(nq, r // rows),
        in_specs=[uspec, pl.BlockSpec((1, SSM_CHUNK * LANES, 2 * LANES), lambda q, i: (q, 0, 0)), hspec, hspec,
                  pl.BlockSpec((1, 2 * half, SSM_CHUNK * LANES), lambda q, i: (q, 0, 0)),
                  pl.BlockSpec((1, LANES), lambda q, i: (0, q))],
        out_specs=uspec,
        scratch_shapes=[pltpu.VMEM((rows * SSM_CHUNK, LANES), F32)],
        compiler_params=_params(("parallel", "parallel")), name="ssm_out")(
            u2d, mats['kstack'], hprev_re, hprev_im, mats['n_all'], mats['d'])


def _ssm(u2d, b, h0_re, h0_im, mats):
    n = u2d.shape[0]
    nchunk = n // b // SSM_CHUNK
    r = b * nchunk
    rows = _pick_tile(r, 256)
    hloc_re, hloc_im = _ssm_local(u2d, mats, rows)
    shp = (b, nchunk, 8, 256)
    hprev_re, hprev_im, f_re, f_im = _ssm_scan(hloc_re.reshape(shp), hloc_im.reshape(shp), mats,
                                               h0_re.reshape(b, 8, 256), h0_im.reshape(b, 8, 256))
    y = _ssm_out(u2d, hprev_re.reshape(r, -1), hprev_im.reshape(r, -1), mats, rows)
    return y, f_re.reshape(b, N_SSM_GROUPS, SSM_STATE), f_im.reshape(b, N_SSM_GROUPS, SSM_STATE)


def _merge_kernel(x_ref, of_ref, ys_ref, qm_ref, gate_ref, mk_ref, mv_ref,
                  wglu_ref, wbf_ref, wbs_ref, wbm_ref, wo_ref, nf_ref, wr_ref,
                  x1_ref, h2_ref, comb_ref):
    tm = x_ref.shape[0]
    om = []
    for hd in range(N_MEM_HEADS):
        sl = slice(hd * MEM_HEAD_DIM, (hd + 1) * MEM_HEAD_DIM)
        kh = mk_ref[0, :, sl].astype(BF16)
        vh = mv_ref[0, :, sl].astype(BF16)
        sc = _dot_nt(qm_ref[:, sl], kh)
        p = jnp.exp(sc - jnp.max(sc, axis=-1, keepdims=True))
        om.append(_dot(p.astype(BF16), vh) / jnp.sum(p, axis=-1, keepdims=True))
    o_mem = jnp.concatenate(om, axis=-1).astype(BF16)
    z = _dot(ys_ref[...], wglu_ref[...])
    y_ssm = (z[:, 0:SSM_WIDTH] * jax.nn.sigmoid(z[:, SSM_WIDTH:2 * SSM_WIDTH])).astype(BF16)
    g = lambda c: gate_ref[:, c * D_MODEL:(c + 1) * D_MODEL].astype(F32)
    merged = (g(0) * _dot(of_ref[...], wbf_ref[...]) + g(1) * _dot(y_ssm, wbs_ref[...])
              + g(2) * _dot(o_mem, wbm_ref[...]))
    x1 = x_ref[...] + _dot(merged.astype(BF16), wo_ref[...])
    x1_ref[...] = x1
    h2 = x1 * lax.rsqrt(jnp.mean(x1 * x1, axis=-1, keepdims=True) + RMS_EPS) * nf_ref[...]
    h2_ref[...] = h2.astype(BF16)
    logits = _dot_exact(h2, wr_ref[...])
    lane = lax.broadcasted_iota(jnp.int32, (tm, LANES), 1)
    big = jnp.int32(LANES)
    is_grp = (lane >= N_EXPERTS) & (lane < N_EXPERTS + N_EXPERT_GROUPS)
    gl = jnp.where(is_grp, logits, NEG_INF)
    gmax = jnp.max(gl, axis=-1, keepdims=True)
    grp = jnp.min(jnp.where(is_grp & (gl == gmax), lane, big), axis=-1, keepdims=True) - N_EXPERTS
    g_w = 1.0 / jnp.sum(jnp.where(is_grp, jnp.exp(gl - gmax), 0.0), axis=-1, keepdims=True)
    in_grp = (lane >= grp * EXPERTS_PER_GROUP) & (lane < (grp + 1) * EXPERTS_PER_GROUP)
    e1 = jnp.where(in_grp, logits, NEG_INF)
    m1 = jnp.max(e1, axis=-1, keepdims=True)
    i1 = jnp.min(jnp.where(in_grp & (e1 == m1), lane, big), axis=-1, keepdims=True)
    rest = in_grp & (lane != i1)
    e2 = jnp.where(rest, logits, NEG_INF)
    m2 = jnp.max(e2, axis=-1, keepdims=True)
    i2 = jnp.min(jnp.where(rest & (e2 == m2), lane, big), axis=-1, keepdims=True)
    ex = jnp.exp(m2 - m1)
    w1 = g_w / (1.0 + ex)
    w2 = g_w * ex / (1.0 + ex)
    comb_ref[...] = jnp.where(lane == i1, w1, jnp.where(lane == i2, w2, 0.0))


def _merge(x2d, o_fox, ys, q_m, gates, mem_k, mem_v, w, tm, rows_per_batch):
    n = x2d.shape[0]
    assert n % tm == 0 and rows_per_batch % tm == 0
    per = rows_per_batch // tm
    row = lambda width: pl.BlockSpec((tm, width), lambda i: (i, 0))
    memspec = pl.BlockSpec((1, N_MEM, MEM_WIDTH), lambda i: (i // per, 0, 0))
    ws = [w['w_glu'], w['w_br_fox'], w['w_br_ssm'], w['w_br_mem'], w['w_out'], w['norm_ffn'], w['w_router']]
    return pl.pallas_call(
        _merge_kernel,
        out_shape=(jax.ShapeDtypeStruct((n, D_MODEL), F32), jax.ShapeDtypeStruct((n, D_MODEL), BF16),
                   jax.ShapeDtypeStruct((n, LANES), F32)),
        grid=(n // tm,),
        in_specs=[row(D_MODEL), row(FOX_WIDTH), row(SSM_WIDTH), row(MEM_WIDTH), row(3 * D_MODEL), memspec, memspec]
                 + [_full(a.shape) for a in ws],
        out_specs=(row(D_MODEL), row(D_MODEL), row(LANES)),
        compiler_params=_params(("parallel",)), name="merge")(
            x2d, o_fox, ys, q_m, gates, mem_k, mem_v, *ws)


def _moe_kernel(h_ref, comb_ref, x1_ref, wg_ref, wu_ref, wd_ref, o_ref, acc_ref):
    e = pl.program_id(1)

    @pl.when(e == 0)
    def _():
        acc_ref[...] = jnp.zeros(acc_ref.shape, F32)

    h = h_ref[...]
    a = _dot(h, wg_ref[0])
    up = _dot(h, wu_ref[0])
    lane = lax.broadcasted_iota(jnp.int32, comb_ref.shape, 1)
    ce = jnp.sum(jnp.where(lane == e, comb_ref[...], 0.0), axis=-1, keepdims=True)
    act = a * jax.nn.sigmoid(a) * up * ce
    acc_ref[...] += _dot(act.astype(BF16), wd_ref[0])

    @pl.when(e == pl.num_programs(1) - 1)
    def _():
        o_ref[...] = x1_ref[...] + acc_ref[...]


def _moe(h2, comb, x1, wg, wu, wd, tm):
    n = h2.shape[0]
    assert n % tm == 0
    row = lambda width: pl.BlockSpec((tm, width), lambda i, e: (i, 0))
    return pl.pallas_call(
        _moe_kernel,
        out_shape=jax.ShapeDtypeStruct((n, D_MODEL), F32),
        grid=(n // tm, N_EXPERTS),
        in_specs=[row(D_MODEL), row(LANES), row(D_MODEL),
                  pl.BlockSpec((1, D_MODEL, D_EXPERT), lambda i, e: (e, 0, 0)),
                  pl.BlockSpec((1, D_MODEL, D_EXPERT), lambda i, e: (e, 0, 0)),
                  pl.BlockSpec((1, D_EXPERT, D_MODEL), lambda i, e: (e, 0, 0))],
        out_specs=row(D_MODEL),
        scratch_shapes=[pltpu.VMEM((tm, D_MODEL), F32)],
        compiler_params=_params(("parallel", "arbitrary")), name="moe")(h2, comb, x1, wg, wu, wd)


def _prep_weights(p):
    w_in = p['w_in'].astype(BF16)
    o = 0
    wqkv = w_in[:, 0:3 * FOX_WIDTH]
    o = 3 * FOX_WIDTH
    wf = jnp.pad(w_in[:, o:o + N_FOX_HEADS], ((0, 0), (0, LANES - N_FOX_HEADS)))
    o += N_FOX_HEADS
    wqm = w_in[:, o:o + MEM_WIDTH]
    o += MEM_WIDTH
    wu = w_in[:, o:o + SSM_WIDTH]
    o += SSM_WIDTH
    wg = w_in[:, o:o + 3 * D_MODEL]
    r = jnp.arange(FOX_WIDTH) // FOX_HEAD_DIM
    bd = (r[:, None] == r[None, :]).astype(BF16)
    w_router = jnp.concatenate(
        [p['w_router_expert'], p['w_router_group'],
         jnp.zeros((D_MODEL, LANES - N_EXPERTS - N_EXPERT_GROUPS), F32)], axis=1)
    return dict(
        norm_mix=p['norm_mix'].reshape(1, D_MODEL), wqkv=wqkv, wf=wf,
        bf=jnp.pad(p['b_forget'], (0, LANES - N_FOX_HEADS)).reshape(1, LANES),
        wqm=wqm, wu=wu, wg=wg,
        qn_fox=jnp.tile(p['qn_fox'], N_FOX_HEADS).reshape(1, FOX_WIDTH),
        kn_fox=jnp.tile(p['kn_fox'], N_FOX_HEADS).reshape(1, FOX_WIDTH),
        qn_mem=p['qn_mem'].reshape(1, MEM_HEAD_DIM), bd=bd,
        w_glu=p['w_glu'].astype(BF16), w_br_fox=p['w_br_fox'].astype(BF16),
        w_br_ssm=p['w_br_ssm'].astype(BF16), w_br_mem=p['w_br_mem'].astype(BF16),
        w_out=p['w_out'].astype(BF16), norm_ffn=p['norm_ffn'].reshape(1, D_MODEL), w_router=w_router,
        moe_wg=p['moe_w_gate'].astype(BF16), moe_wu=p['moe_w_up'].astype(BF16),
        moe_wd=p['moe_w_down'].astype(BF16))


def _pick_tile(n, target):
    t = min(n, target)
    while n % t:
        t //= 2
    return t


def _group(x, w, mats, mem_k, mem_v, h0_re, h0_im, cache):
    b, s, _ = x.shape
    n = b * s
    x2d = x.reshape(n, D_MODEL)
    q, kb, vb, k4, v4, logf, q_m, u, gates = _inproj(x2d, w, _pick_tile(n, 512))
    q3 = q.reshape(b, s, FOX_WIDTH)
    k3 = kb.reshape(b, s, FOX_WIDTH)
    v3 = vb.reshape(b, s, FOX_WIDTH)
    lf3 = logf.reshape(b, s, N_FOX_HEADS)
    npair = N_FOX_HEADS // 2
    if cache is None:
        c_row = _cumsum_rows(lf3.transpose(0, 2, 1).reshape(b * N_FOX_HEADS, s)).reshape(b, npair, 2, s)
        c_col = c_row.transpose(0, 1, 3, 2)
        o_fox = _fox_prompt(q3, k3, v3, c_col, c_row, _pick_tile(s, 256), _pick_tile(s, 512))
    else:
        cache_k, cache_v, cache_logf = cache
        past = cache_k.shape[1]
        lf_all = jnp.concatenate([cache_logf.astype(F32).transpose(0, 2, 1), lf3.transpose(0, 2, 1)], axis=2)
        c_row = _cumsum_rows(lf_all.reshape(b * N_FOX_HEADS, past + s)).reshape(b, N_FOX_HEADS, past + s)
        o_fox = _fox_sample(q3, cache_k, cache_v, k3, v3, c_row[:, :, past:].transpose(0, 2, 1),
                            c_row[:, :, :past], c_row[:, :, past:], _pick_tile(past, 1024))
    ys, f_re, f_im = _ssm(u, b, h0_re, h0_im, mats)
    tm = _pick_tile(s, 512)
    x1, h2, comb = _merge(x2d, o_fox.reshape(n, FOX_WIDTH), ys, q_m, gates, mem_k, mem_v, w, tm, s)
    y = _moe(h2, comb, x1, w['moe_wg'], w['moe_wu'], w['moe_wd'], _pick_tile(n, 1024))
    return (y.reshape(b, s, D_MODEL), k4.reshape(b, s, N_FOX_HEADS, FOX_HEAD_DIM),
            v4.reshape(b, s, N_FOX_HEADS, FOX_HEAD_DIM), lf3, f_re, f_im)


def kernel(x_prompt, x_sample, mem_prompt, cache_fox_k, cache_fox_v, cache_fox_logf, state_ssm_re, state_ssm_im,
           cache_mem_k, cache_mem_v, norm_mix, w_in, b_forget, qn_fox, kn_fox, qn_mem, kn_mem, norm_mem, w_mem_kv,
           ssm_a_re, ssm_a_im, ssm_log_dt, ssm_b_re, ssm_b_im, ssm_c_re, ssm_c_im, ssm_d, w_glu, w_br_fox,
           w_br_ssm, w_br_mem, w_out, norm_ffn, w_router_group, w_router_expert, moe_w_gate, moe_w_up,
           moe_w_down):
    depth = norm_mix.shape[0]
    assert depth == 1
    l = 0
    p = dict(norm_mix=norm_mix[l], w_in=w_in[l], b_forget=b_forget[l], qn_fox=qn_fox[l], kn_fox=kn_fox[l],
             qn_mem=qn_mem[l], ssm_a_re=ssm_a_re[l], ssm_a_im=ssm_a_im[l], ssm_log_dt=ssm_log_dt[l],
             ssm_b_re=ssm_b_re[l], ssm_b_im=ssm_b_im[l], ssm_c_re=ssm_c_re[l], ssm_c_im=ssm_c_im[l],
             ssm_d=ssm_d[l], w_glu=w_glu[l], w_br_fox=w_br_fox[l], w_br_ssm=w_br_ssm[l], w_br_mem=w_br_mem[l],
             w_out=w_out[l], norm_ffn=norm_ffn[l], w_router_group=w_router_group[l],
             w_router_expert=w_router_expert[l], moe_w_gate=moe_w_gate[l], moe_w_up=moe_w_up[l],
             moe_w_down=moe_w_down[l])
    w = _prep_weights(p)
    mats = _ssm_mats(p)
    bp, sp, _ = x_prompt.shape
    bs, ss, _ = x_sample.shape

    mk, mv = _memkv(mem_prompt.reshape(bp * N_MEM, D_MODEL), norm_mem[l].reshape(1, D_MODEL),
                    w_mem_kv[l].astype(BF16), kn_mem[l].reshape(1, MEM_HEAD_DIM), _pick_tile(bp * N_MEM, 512))
    mk = mk.reshape(bp, N_MEM, MEM_WIDTH)
    mv = mv.reshape(bp, N_MEM, MEM_WIDTH)
    zeros = jnp.zeros((bp, N_SSM_GROUPS, SSM_STATE), F32)
    yp, pk, pv, plf, pre, pim = _group(x_prompt, w, mats, mk, mv, zeros, zeros, None)
    cache = (cache_fox_k[l], cache_fox_v[l], cache_fox_logf[l])
    ys, sk, sv, slf, sre, sim = _group(
        x_sample, w, mats, cache_mem_k[l].reshape(bs, N_MEM, MEM_WIDTH), cache_mem_v[l].reshape(bs, N_MEM, MEM_WIDTH),
        state_ssm_re[l].astype(F32), state_ssm_im[l].astype(F32), cache)
    st = lambda a: a[None]
    return (yp, ys, st(pk), st(pv), st(plf), st(pre), st(pim),
            st(mk.reshape(bp, N_MEM, N_MEM_HEADS, MEM_HEAD_DIM)), st(mv.reshape(bp, N_MEM, N_MEM_HEADS, MEM_HEAD_DIM)),
            st(sk), st(sv), st(slf), st(sre), st(sim))
```

```python
import functools
import math

import jax
import jax.numpy as jnp
from jax import lax
from jax.experimental import pallas as pl
from jax.experimental.pallas import tpu as pltpu

F32 = jnp.float32
BF16 = jnp.bfloat16

D_MODEL = 1024
N_FOX_HEADS = 8
FOX_HEAD_DIM = 64
FOX_WIDTH = N_FOX_HEADS * FOX_HEAD_DIM
N_MEM = 256
N_MEM_HEADS = 4
MEM_HEAD_DIM = 128
MEM_WIDTH = N_MEM_HEADS * MEM_HEAD_DIM
SSM_GROUP = 16
SSM_WIDTH = 512
N_SSM_GROUPS = SSM_WIDTH // SSM_GROUP
SSM_STATE = 64
N_EXPERT_GROUPS = 4
EXPERTS_PER_GROUP = 8
N_EXPERTS = N_EXPERT_GROUPS * EXPERTS_PER_GROUP
D_EXPERT = 256
RMS_EPS = 1e-6
NEG_INF = -1e30

LANES = 128
SSM_CHUNK = 16
SSM_GPB = LANES // SSM_GROUP
VMEM_LIMIT = 56 * 1024 * 1024


def _dot(a, b):
    return jnp.dot(a, b, preferred_element_type=F32)


def _dot_nt(a, b):
    return lax.dot_general(a, b, (((1,), (1,)), ((), ())), preferred_element_type=F32)


def _dot_exact(a, b):
    return jnp.dot(a, b, preferred_element_type=F32, precision=lax.Precision.HIGHEST)


def _split_bf16(x):
    hi = x.astype(BF16)
    lo = (x - hi.astype(F32)).astype(BF16)
    return hi, lo


def _params(sem):
    return pltpu.CompilerParams(dimension_semantics=sem, vmem_limit_bytes=VMEM_LIMIT)


def _full(shape):
    n = len(shape)
    return pl.BlockSpec(shape, lambda *_: (0,) * n)


def _inproj_kernel(x_ref, g_ref, wqkv_ref, wf_ref, bf_ref, wqm_ref, wu_ref, wg_ref,
                   qn_ref, kn_ref, qmn_ref, bd_ref,
                   q_ref, kb_ref, vb_ref, k_ref, v_ref, lf_ref, qm_ref, u_ref, gate_ref, *, kv_transposed):
    x = x_ref[...]
    h = x * lax.rsqrt(jnp.mean(x * x, axis=-1, keepdims=True) + RMS_EPS) * g_ref[...]
    hb = h.astype(BF16)

    def head_norm(z, gain):
        hi, lo = _split_bf16(z * z)
        ss = _dot(hi, bd_ref[...]) + _dot(lo, bd_ref[...])
        return z * lax.rsqrt(ss * (1.0 / FOX_HEAD_DIM) + RMS_EPS) * gain

    zq = _dot(hb, wqkv_ref[:, 0:FOX_WIDTH])
    q_ref[...] = (head_norm(zq, qn_ref[...]) * (FOX_HEAD_DIM ** -0.5)).astype(BF16)
    zk = _dot(hb, wqkv_ref[:, FOX_WIDTH:2 * FOX_WIDTH])
    kn = head_norm(zk, kn_ref[...])
    zv = _dot(hb, wqkv_ref[:, 2 * FOX_WIDTH:3 * FOX_WIDTH])
    kb_ref[...] = kn.astype(BF16)
    vb_ref[...] = zv.astype(BF16)
    tm = x_ref.shape[0]
    if kv_transposed:
        k_ref[0] = kn.T
        v_ref[0] = zv.T
    else:
        for hd in range(N_FOX_HEADS):
            hs = slice(hd * FOX_HEAD_DIM, (hd + 1) * FOX_HEAD_DIM)
            rows = pl.ds(hd, tm, stride=N_FOX_HEADS)
            k_ref[rows, :] = kn[:, hs]
            v_ref[rows, :] = zv[:, hs]

    zf = (_dot(hb, wf_ref[...]) + bf_ref[...]).T[0:N_FOX_HEADS, :]
    lf_ref[...] = jnp.minimum(zf, 0.0) - jnp.log1p(jnp.exp(-jnp.abs(zf)))

    zm = _dot(hb, wqm_ref[...])
    for hd in range(N_MEM_HEADS):
        sl = slice(hd * MEM_HEAD_DIM, (hd + 1) * MEM_HEAD_DIM)
        zh = zm[:, sl]
        ms = jnp.mean(zh * zh, axis=-1, keepdims=True)
        qm_ref[:, sl] = (zh * lax.rsqrt(ms + RMS_EPS) * qmn_ref[...] * (MEM_HEAD_DIM ** -0.5)).astype(BF16)

    u_ref[...] = _dot(hb, wu_ref[...])
    for c in range(3):
        sl = slice(c * D_MODEL, (c + 1) * D_MODEL)
        gate_ref[:, sl] = jax.nn.sigmoid(_dot(hb, wg_ref[:, sl])).astype(BF16)


def _inproj(x2d, w, tm, seq, kv_transposed):
    n = x2d.shape[0]
    assert n % tm == 0
    row = lambda width: pl.BlockSpec((tm, width), lambda i: (i, 0))
    if kv_transposed:
        assert seq % tm == 0
        per = seq // tm
        kv_shape = jax.ShapeDtypeStruct((n // seq, FOX_WIDTH, seq), F32)
        heads = pl.BlockSpec((1, FOX_WIDTH, tm), lambda i: (i // per, 0, i % per))
    else:
        kv_shape = jax.ShapeDtypeStruct((n * N_FOX_HEADS, FOX_HEAD_DIM), F32)
        heads = pl.BlockSpec((tm * N_FOX_HEADS, FOX_HEAD_DIM), lambda i: (i, 0))
    ins = [x2d, w['norm_mix'], w['wqkv'], w['wf'], w['bf'], w['wqm'], w['wu'], w['wg'],
           w['qn_fox'], w['kn_fox'], w['qn_mem'], w['bd']]
    in_specs = [row(D_MODEL)] + [_full(a.shape) for a in ins[1:]]
    out_shape = (
        jax.ShapeDtypeStruct((n, FOX_WIDTH), BF16),
        jax.ShapeDtypeStruct((n, FOX_WIDTH), BF16),
        jax.ShapeDtypeStruct((n, FOX_WIDTH), BF16),
        kv_shape,
        kv_shape,
        jax.ShapeDtypeStruct((N_FOX_HEADS, n), F32),
        jax.ShapeDtypeStruct((n, MEM_WIDTH), BF16),
        jax.ShapeDtypeStruct((n, SSM_WIDTH), F32),
        jax.ShapeDtypeStruct((n, 3 * D_MODEL), BF16),
    )
    out_specs = (row(FOX_WIDTH), row(FOX_WIDTH), row(FOX_WIDTH), heads, heads,
                 pl.BlockSpec((N_FOX_HEADS, tm), lambda i: (0, i)),
                 row(MEM_WIDTH), row(SSM_WIDTH), row(3 * D_MODEL))
    return pl.pallas_call(
        functools.partial(_inproj_kernel, kv_transposed=kv_transposed),
        out_shape=out_shape, grid=(n // tm,), in_specs=in_specs, out_specs=out_specs,
        compiler_params=_params(("parallel",)), name="inproj")(*ins)


def _memkv_kernel(x_ref, g_ref, w_ref, kn_ref, k_ref, v_ref):
    x = x_ref[...]
    h = x * lax.rsqrt(jnp.mean(x * x, axis=-1, keepdims=True) + RMS_EPS) * g_ref[...]
    hb = h.astype(BF16)
    zk = _dot(hb, w_ref[:, 0:MEM_WIDTH])
    for hd in range(N_MEM_HEADS):
        sl = slice(hd * MEM_HEAD_DIM, (hd + 1) * MEM_HEAD_DIM)
        zh = zk[:, sl]
        ms = jnp.mean(zh * zh, axis=-1, keepdims=True)
        k_ref[:, sl] = zh * lax.rsqrt(ms + RMS_EPS) * kn_ref[...]
    v_ref[...] = _dot(hb, w_ref[:, MEM_WIDTH:2 * MEM_WIDTH])


def _memkv(mem2d, norm_mem, w_kv, kn_mem, tm):
    n = mem2d.shape[0]
    row = lambda width: pl.BlockSpec((tm, width), lambda i: (i, 0))
    return pl.pallas_call(
        _memkv_kernel,
        out_shape=(jax.ShapeDtypeStruct((n, MEM_WIDTH), F32), jax.ShapeDtypeStruct((n, MEM_WIDTH), F32)),
        grid=(n // tm,),
        in_specs=[row(D_MODEL), _full(norm_mem.shape), _full(w_kv.shape), _full(kn_mem.shape)],
        out_specs=(row(MEM_WIDTH), row(MEM_WIDTH)),
        compiler_params=_params(("parallel",)), name="memkv")(mem2d, norm_mem, w_kv, kn_mem)


CUMSUM_BLOCK = 256


def _cumsum_kernel(x_ref, o_ref):
    nblk = x_ref.shape[1] // CUMSUM_BLOCK
    r = lax.broadcasted_iota(jnp.int32, (CUMSUM_BLOCK, CUMSUM_BLOCK), 0)
    c = lax.broadcasted_iota(jnp.int32, (CUMSUM_BLOCK, CUMSUM_BLOCK), 1)
    tri = (r <= c).astype(F32)
    carry = jnp.zeros((x_ref.shape[0], 1), F32)
    for j in range(nblk):
        sl = slice(j * CUMSUM_BLOCK, (j + 1) * CUMSUM_BLOCK)
        cs = _dot_exact(x_ref[:, sl], tri) + carry
        o_ref[:, sl] = cs
        carry = cs[:, CUMSUM_BLOCK - 1:CUMSUM_BLOCK]


def _cumsum_rows(x):
    rows, n = x.shape
    npad = -(-n // CUMSUM_BLOCK) * CUMSUM_BLOCK
    xp = jnp.pad(x, ((0, 0), (0, npad - n))) if npad != n else x
    out = pl.pallas_call(
        _cumsum_kernel, out_shape=jax.ShapeDtypeStruct((rows, npad), F32), grid=(1,),
        in_specs=[_full((rows, npad))], out_specs=_full((rows, npad)),
        compiler_params=_params(("arbitrary",)), name="cumsum")(xp)
    return out[:, :n] if npad != n else out


def _reduce_rows(x, op):
    rows, cols = x.shape
    if rows > 64 and rows % 64 == 0:
        x = op(x.reshape(rows // 64, 64, cols), axis=0)
        rows = 64
    if rows == 64:
        x = op(x.reshape(8, 8, cols), axis=0)
    return op(x, axis=0, keepdims=True)


def _head_lane_masks(rows):
    lane = lax.broadcasted_iota(jnp.int32, (rows, LANES), 1)
    return lane < FOX_HEAD_DIM


def _softmax_step(qh, kb, vb, ck, cq, m, l, acc, mask):
    t = _dot_nt(qh, kb) - ck
    if mask is not None:
        t = jnp.where(mask, t, NEG_INF)
    m_new = jnp.maximum(m, jnp.max(t, axis=-1, keepdims=True) + cq)
    alpha = jnp.exp(m - m_new)
    p = jnp.exp(t + (cq - m_new))
    l_new = alpha * l + jnp.sum(p, axis=-1, keepdims=True)
    acc_new = alpha * acc + _dot(p.astype(BF16), vb)
    return m_new, l_new, acc_new


def _fox_prompt_kernel(q_ref, k_ref, v_ref, cc_ref, cr_ref, o_ref,
                       vt_ref, ck0_ref, ck1_ref, st_ref, pt_ref, acc_ref, *, tq, tk):
    i = pl.program_id(2)
    s_len = k_ref.shape[1]

    @pl.when(i == 0)
    def _():
        vt_ref[...] = v_ref[0].astype(F32).T.astype(BF16)
        ck0_ref[...] = jnp.broadcast_to(cc_ref[0, 0, :, 0:1], (s_len, LANES))
        ck1_ref[...] = jnp.broadcast_to(cc_ref[0, 0, :, 1:2], (s_len, LANES))

    qt = q_ref[0].astype(F32).T
    row = lax.broadcasted_iota(jnp.int32, (LANES, tq), 0)
    qts = (jnp.where(row < FOX_HEAD_DIM, qt, 0.0).astype(BF16), jnp.where(row < FOX_HEAD_DIM, 0.0, qt).astype(BF16))
    q0 = pl.multiple_of(i * tq, tq)
    cq = cr_ref[0, 0, :, pl.ds(q0, tq)]
    ck_refs = (ck0_ref, ck1_ref)

    def stage_a(n):
        s = pl.multiple_of(n * tk, tk)
        kb = k_ref[0, pl.ds(s, tk), :]
        for hh in range(2):
            ck = ck_refs[hh][pl.ds(s, tk), :]
            st_ref[n & 1, hh] = _dot(kb, qts[hh]) - jnp.concatenate([ck] * (tq // LANES), axis=1)

    def stage_b(n, stats, masked):
        if masked:
            kpos = n * tk + lax.broadcasted_iota(jnp.int32, (tk, tq), 0)
            qpos = q0 + lax.broadcasted_iota(jnp.int32, (tk, tq), 1)
            mask = kpos <= qpos
        out = []
        for hh in range(2):
            m, l = stats[2 * hh:2 * hh + 2]
            t = st_ref[n & 1, hh]
            if masked:
                t = jnp.where(mask, t, NEG_INF)
            cqh = cq[hh:hh + 1, :]
            m_new = jnp.maximum(m, _reduce_rows(t, jnp.max) + cqh)
            alpha = jnp.exp(m - m_new)
            p = jnp.exp(t + (cqh - m_new))
            pt_ref[n & 1, hh] = p.astype(BF16)
            out.extend([m_new, alpha * l + _reduce_rows(p, jnp.sum), alpha])
        return out

    def stage_c(n, alphas):
        s = pl.multiple_of(jnp.maximum(n, 0) * tk, tk)
        for hh in range(2):
            vt = vt_ref[hh * FOX_HEAD_DIM:(hh + 1) * FOX_HEAD_DIM, pl.ds(s, tk)]
            acc_ref[hh] = alphas[hh] * acc_ref[hh] + _dot(vt, pt_ref[n & 1, hh])

    acc_ref[...] = jnp.zeros(acc_ref.shape, F32)
    pt_ref[1] = jnp.zeros(pt_ref.shape[1:], BF16)
    neg = jnp.full((1, tq), NEG_INF, F32)
    zero = jnp.zeros((1, tq), F32)
    one = jnp.ones((1, tq), F32)
    nfull = (i * tq) // tk
    stage_a(0)

    def body(n, carry):
        m0, l0, al0, m1, l1, al1 = carry
        stage_c(n - 1, (al0, al1))
        new = stage_b(n, (m0, l0, m1, l1), False)
        stage_a(n + 1)
        return tuple(new)

    m0, l0, al0, m1, l1, al1 = lax.fori_loop(0, nfull, body, (neg, zero, one, neg, zero, one))
    stage_c(nfull - 1, (al0, al1))
    _, l0, be0, _, l1, be1 = stage_b(nfull, (m0, l0, m1, l1), True)
    stage_c(nfull, (be0, be1))
    ot = jnp.concatenate([acc_ref[0] / l0, acc_ref[1] / l1], axis=0)
    o_ref[0] = ot.T.astype(o_ref.dtype)


def _fox_prompt(q, k, v, c_col, c_row, tq, tk):
    b, s, _ = q.shape
    assert s % tk == 0 and tk % tq == 0
    npair = N_FOX_HEADS // 2
    return pl.pallas_call(
        functools.partial(_fox_prompt_kernel, tq=tq, tk=tk),
        out_shape=jax.ShapeDtypeStruct((b, s, FOX_WIDTH), BF16),
        grid=(b, npair, s // tq),
        in_specs=[
            pl.BlockSpec((1, tq, LANES), lambda bi, hp, i: (bi, i, hp)),
            pl.BlockSpec((1, s, LANES), lambda bi, hp, i: (bi, 0, hp)),
            pl.BlockSpec((1, s, LANES), lambda bi, hp, i: (bi, 0, hp)),
            pl.BlockSpec((1, 1, s, 2), lambda bi, hp, i: (bi, hp, 0, 0)),
            pl.BlockSpec((1, 1, 2, s), lambda bi, hp, i: (bi, hp, 0, 0)),
        ],
        out_specs=pl.BlockSpec((1, tq, LANES), lambda bi, hp, i: (bi, i, hp)),
        scratch_shapes=[pltpu.VMEM((LANES, s), BF16),
                        pltpu.VMEM((s, LANES), F32), pltpu.VMEM((s, LANES), F32),
                        pltpu.VMEM((2, 2, tk, tq), F32), pltpu.VMEM((2, 2, tk, tq), BF16),
                        pltpu.VMEM((2, FOX_HEAD_DIM, tq), F32)],
        compiler_params=_params(("parallel", "parallel", "arbitrary")), name="fox_prompt")(q, k, v, c_col, c_row)


def _fox_sample_kernel(q_ref, ck_ref, cv_ref, nk_ref, nv_ref, cq_ref, crc_ref, crn_ref, o_ref, *state, n):
    j = pl.program_id(1)
    nj = pl.num_programs(1)
    m_refs = state[0:N_FOX_HEADS]
    l_refs = state[N_FOX_HEADS:2 * N_FOX_HEADS]
    acc_refs = state[2 * N_FOX_HEADS:3 * N_FOX_HEADS]

    @pl.when(j == 0)
    def _():
        for hd in range(N_FOX_HEADS):
            m_refs[hd][...] = jnp.full(m_refs[hd].shape, NEG_INF, F32)
            l_refs[hd][...] = jnp.zeros(l_refs[hd].shape, F32)
            acc_refs[hd][...] = jnp.zeros(acc_refs[hd].shape, F32)

    def update(k_of, v_of, cr_ref_, mask, transposed):
        qk = _dot if transposed else _dot_nt
        pv = _dot_nt if transposed else _dot
        ts = []
        for hd in range(N_FOX_HEADS):
            hs = slice(hd * FOX_HEAD_DIM, (hd + 1) * FOX_HEAD_DIM)
            t = qk(q_ref[0, :, hs], k_of(hd)) - cr_ref_[0, hd:hd + 1, :]
            ts.append(t if mask is None else jnp.where(mask, t, NEG_INF))
        ps = []
        for hd in range(N_FOX_HEADS):
            cq = cq_ref[0, :, hd:hd + 1]
            m = m_refs[hd][...]
            m_new = jnp.maximum(m, jnp.max(ts[hd], axis=-1, keepdims=True) + cq)
            alpha = jnp.exp(m - m_new)
            p = jnp.exp(ts[hd] + (cq - m_new))
            m_refs[hd][...] = m_new
            l_refs[hd][...] = alpha * l_refs[hd][...] + jnp.sum(p, axis=-1, keepdims=True)
            ps.append((alpha, p.astype(BF16)))
        for hd in range(N_FOX_HEADS):
            alpha, p = ps[hd]
            acc_refs[hd][...] = alpha * acc_refs[hd][...] + pv(p, v_of(hd))

    update(lambda hd: ck_ref[0, hd].astype(BF16), lambda hd: cv_ref[0, hd].astype(BF16), crc_ref, None, True)

    @pl.when(j == nj - 1)
    def _():
        r = lax.broadcasted_iota(jnp.int32, (n, n), 0)
        c = lax.broadcasted_iota(jnp.int32, (n, n), 1)
        head = lambda ref: (lambda hd: ref[0, :, hd * FOX_HEAD_DIM:(hd + 1) * FOX_HEAD_DIM])
        update(head(nk_ref), head(nv_ref), crn_ref, c <= r, False)
        for hd in range(N_FOX_HEADS):
            hs = slice(hd * FOX_HEAD_DIM, (hd + 1) * FOX_HEAD_DIM)
            o_ref[0, :, hs] = (acc_refs[hd][...] / l_refs[hd][...]).astype(o_ref.dtype)


def _fox_sample(q, cache_k, cache_v, k_new, v_new, c_q, c_row_cache, c_row_new, tk):
    b, n, _ = q.shape
    past = cache_k.shape[3]
    assert past % tk == 0
    cache_spec = pl.BlockSpec((1, N_FOX_HEADS, FOX_HEAD_DIM, tk), lambda bi, j: (bi, 0, 0, j))
    return pl.pallas_call(
        functools.partial(_fox_sample_kernel, n=n),
        out_shape=jax.ShapeDtypeStruct((b, n, FOX_WIDTH), BF16),
        grid=(b, past // tk),
        in_specs=[
            pl.BlockSpec((1, n, FOX_WIDTH), lambda bi, j: (bi, 0, 0)),
            cache_spec,
            cache_spec,
            pl.BlockSpec((1, n, FOX_WIDTH), lambda bi, j: (bi, 0, 0)),
            pl.BlockSpec((1, n, FOX_WIDTH), lambda bi, j: (bi, 0, 0)),
            pl.BlockSpec((1, n, N_FOX_HEADS), lambda bi, j: (bi, 0, 0)),
            pl.BlockSpec((1, N_FOX_HEADS, tk), lambda bi, j: (bi, 0, j)),
            pl.BlockSpec((1, N_FOX_HEADS, n), lambda bi, j: (bi, 0, 0)),
        ],
        out_specs=pl.BlockSpec((1, n, FOX_WIDTH), lambda bi, j: (bi, 0, 0)),
        scratch_shapes=([pltpu.VMEM((n, 1), F32)] * (2 * N_FOX_HEADS)
                        + [pltpu.VMEM((n, FOX_HEAD_DIM), F32)] * N_FOX_HEADS),
        compiler_params=_params(("parallel", "arbitrary")), name="fox_sample")(
            q, cache_k, cache_v, k_new, v_new, c_q, c_row_cache, c_row_new)


def _ssm_mats(p):
    f32 = F32
    a_re, a_im = p['ssm_a_re'].astype(f32), p['ssm_a_im'].astype(f32)
    b_re, b_im = p['ssm_b_re'].astype(f32), p['ssm_b_im'].astype(f32)
    c_re, c_im = p['ssm_c_re'].astype(f32), p['ssm_c_im'].astype(f32)
    dt = jnp.exp(p['ssm_log_dt'].astype(f32))[:, None]
    mag = jnp.exp(dt * a_re)
    ab_re = mag * jnp.cos(dt * a_im)
    ab_im = mag * jnp.sin(dt * a_im)
    den = a_re * a_re + a_im * a_im
    nr, ni = ab_re - 1.0, ab_im
    coef_re = (nr * a_re + ni * a_im) / den
    coef_im = (ni * a_re - nr * a_im) / den
    bb_re = coef_re[..., None] * b_re - coef_im[..., None] * b_im
    bb_im = coef_re[..., None] * b_im + coef_im[..., None] * b_re
    pr, pi = [jnp.ones_like(ab_re)], [jnp.zeros_like(ab_im)]
    for _ in range(SSM_CHUNK):
        pr.append(pr[-1] * ab_re - pi[-1] * ab_im)
        pi.append(pr[-2] * ab_im + pi[-1] * ab_re)
    pw_re, pw_im = jnp.stack(pr), jnp.stack(pi)
    T = SSM_CHUNK
    w_re = pw_re[..., None] * bb_re[None] - pw_im[..., None] * bb_im[None]
    w_im = pw_re[..., None] * bb_im[None] + pw_im[..., None] * bb_re[None]
    kk = (jnp.einsum('gop,kgpi->kgoi', c_re, w_re[:T], precision='highest')
          - jnp.einsum('gop,kgpi->kgoi', c_im, w_im[:T], precision='highest'))
    eye = jnp.eye(SSM_GPB, dtype=f32)
    nq = N_SSM_GROUPS // SSM_GPB

    def lane_diag(m):
        lead = m.shape[:-3]
        i, c = m.shape[-2:]
        m = m.reshape(lead + (nq, SSM_GPB, i, 1, c)) * eye[:, None, :, None]
        return m.reshape(lead + (nq, SSM_GPB * i, SSM_GPB * c))

    ktau = lane_diag(jnp.swapaxes(kk, -1, -2))
    ktau = jnp.concatenate([jnp.zeros_like(ktau[:1]), ktau], axis=0)
    units = []
    for dlag in range(T // 2 - 1, -1, -1):
        top = jnp.concatenate([ktau[2 * dlag + 1], ktau[2 * dlag + 2]], axis=-1)
        bot = jnp.concatenate([ktau[2 * dlag], ktau[2 * dlag + 1]], axis=-1)
        units.append(jnp.concatenate([top, bot], axis=-2))
    kstack = jnp.concatenate(units, axis=-2).astype(BF16)
    rev = T - 1 - jnp.arange(T)
    m_re = lane_diag(jnp.swapaxes(w_re[rev], -1, -2))
    m_im = lane_diag(jnp.swapaxes(w_im[rev], -1, -2))
    m_all = jnp.concatenate([m_re, m_im], axis=-1)
    m_all = jnp.swapaxes(m_all, 0, 1).reshape(nq, T * LANES, 2 * SSM_GPB * SSM_STATE)
    m_hi, m_lo = _split_bf16(m_all)
    ar, ai = pw_re[1:], pw_im[1:]
    n_re = (c_re[None] * ar[:, :, None, :] - c_im[None] * ai[:, :, None, :])
    n_im = -(c_re[None] * ai[:, :, None, :] + c_im[None] * ar[:, :, None, :])

    def state_rows(n):
        n = lane_diag(n)
        return jnp.transpose(n, (1, 3, 0, 2)).reshape(nq, SSM_GPB * SSM_STATE, T * LANES)

    n_all = jnp.concatenate([state_rows(n_re), state_rows(n_im)], axis=1).astype(BF16)
    return dict(kstack=kstack, m_hi=m_hi, m_lo=m_lo, n_all=n_all,
                a16_re=pw_re[T].reshape(8, 256), a16_im=pw_im[T].reshape(8, 256),
                d=p['ssm_d'].astype(f32).reshape(1, SSM_WIDTH))


def _chunk_tokens(u_ref, rows):
    return [u_ref[pl.ds(t, rows, stride=SSM_CHUNK), :] for t in range(SSM_CHUNK)]


def _ssm_local_kernel(u_ref, mh_ref, ml_ref, hre_ref, him_ref):
    rows = hre_ref.shape[0]
    parts = [_split_bf16(ut) for ut in _chunk_tokens(u_ref, rows)]
    x_hi = jnp.concatenate([h for h, _ in parts], axis=1)
    x_lo = jnp.concatenate([l for _, l in parts], axis=1)
    h = _dot(x_hi, mh_ref[0]) + _dot(x_hi, ml_ref[0]) + _dot(x_lo, mh_ref[0])
    half = SSM_GPB * SSM_STATE
    hre_ref[...] = h[:, 0:half]
    him_ref[...] = h[:, half:2 * half]


def _ssm_local(u2d, mats, rows):
    n = u2d.shape[0]
    r = n // SSM_CHUNK
    nq = N_SSM_GROUPS // SSM_GPB
    half = SSM_GPB * SSM_STATE
    mspec = pl.BlockSpec((1, SSM_CHUNK * LANES, 2 * half), lambda q, i: (q, 0, 0))
    ospec = pl.BlockSpec((rows, half), lambda q, i: (i, q))
    return pl.pallas_call(
        _ssm_local_kernel,
        out_shape=(jax.ShapeDtypeStruct((r, N_SSM_GROUPS * SSM_STATE), F32),) * 2,
        grid=(nq, r // rows),
        in_specs=[pl.BlockSpec((rows * SSM_CHUNK, LANES), lambda q, i: (i, q)), mspec, mspec],
        out_specs=(ospec, ospec),
        compiler_params=_params(("parallel", "parallel")), name="ssm_local")(u2d, mats['m_hi'], mats['m_lo'])


def _ssm_scan_kernel(lre_ref, lim_ref, are_ref, aim_ref, h0re_ref, h0im_ref,
                     pre_ref, pim_ref, fre_ref, fim_ref):
    nchunk = lre_ref.shape[1]
    ar, ai = are_ref[...], aim_ref[...]

    def body(c, carry):
        hr, hi = carry
        pre_ref[0, c] = hr
        pim_ref[0, c] = hi
        return (ar * hr - ai * hi + lre_ref[0, c], ar * hi + ai * hr + lim_ref[0, c])

    hr, hi = lax.fori_loop(0, nchunk, body, (h0re_ref[0], h0im_ref[0]))
    fre_ref[0] = hr
    fim_ref[0] = hi


def _ssm_scan(hloc_re, hloc_im, mats, h0_re, h0_im):
    b, nchunk = hloc_re.shape[:2]
    big = pl.BlockSpec((1, nchunk, 8, 256), lambda i: (i, 0, 0, 0))
    small = pl.BlockSpec((1, 8, 256), lambda i: (i, 0, 0))
    return pl.pallas_call(
        _ssm_scan_kernel,
        out_shape=(jax.ShapeDtypeStruct(hloc_re.shape, F32),) * 2 + (jax.ShapeDtypeStruct((b, 8, 256), F32),) * 2,
        grid=(b,),
        in_specs=[big, big, _full((8, 256)), _full((8, 256)), small, small],
        out_specs=(big, big, small, small),
        compiler_params=_params(("parallel",)), name="ssm_scan")(
            hloc_re, hloc_im, mats['a16_re'], mats['a16_im'], h0_re, h0_im)


def _gelu_tanh(y):
    return 0.5 * y * (1.0 + jnp.tanh(math.sqrt(2.0 / math.pi) * (y + 0.044715 * (y * y * y))))


def _ssm_out_kernel(u_ref, k_ref, pre_ref, pim_ref, n_ref, d_ref, y_ref, ysc_ref):
    rows = pre_ref.shape[0]
    us = _chunk_tokens(u_ref, rows)
    x = jnp.concatenate([ut.astype(BF16) for ut in us], axis=1)
    hp = jnp.concatenate([pre_ref[...], pim_ref[...]], axis=1).astype(BF16)
    unit = 2 * LANES
    nunit = SSM_CHUNK // 2
    for j in range(nunit):
        yj = (_dot(x[:, 0:unit * (j + 1)], k_ref[0, unit * (nunit - 1 - j):, :])
              + _dot(hp, n_ref[0, :, unit * j:unit * (j + 1)]))
        for t2 in range(2):
            t = 2 * j + t2
            y = yj[:, t2 * LANES:(t2 + 1) * LANES] + d_ref[...] * us[t]
            ysc_ref[pl.ds(t, rows, stride=SSM_CHUNK), :] = _gelu_tanh(y)
    y_ref[...] = ysc_ref[...].astype(y_ref.dtype)


def _ssm_out(u2d, hprev_re, hprev_im, mats, rows):
    n = u2d.shape[0]
    r = n // SSM_CHUNK
    nq = N_SSM_GROUPS // SSM_GPB
    half = SSM_GPB * SSM_STATE
    uspec = pl.BlockSpec((rows * SSM_CHUNK, LANES), lambda q, i: (i, q))
    hspec = pl.BlockSpec((rows, half), lambda q, i: (i, q))
    return pl.pallas_call(
        _ssm_out_kernel,
        out_shape=jax.ShapeDtypeStruct((n, SSM_WIDTH), BF16),
        grid=(nq, r // rows),
        in_specs=[uspec, pl.BlockSpec((1, SSM_CHUNK * LANES, 2 * LANES), lambda q, i: (q, 0, 0)), hspec, hspec,
                  pl.BlockSpec((1, 2 * half, SSM_CHUNK * LANES), lambda q, i: (q, 0, 0)),
                  pl.BlockSpec((1, LANES), lambda q, i: (0, q))],
        out_specs=uspec,
        scratch_shapes=[pltpu.VMEM((rows * SSM_CHUNK, LANES), F32)],
        compiler_params=_params(("parallel", "parallel")), name="ssm_out")(
            u2d, mats['kstack'], hprev_re, hprev_im, mats['n_all'], mats['d'])


def _ssm(u2d, b, h0_re, h0_im, mats):
    n = u2d.shape[0]
    nchunk = n // b // SSM_CHUNK
    r = b * nchunk
    rows = _pick_tile(r, 256)
    hloc_re, hloc_im = _ssm_local(u2d, mats, rows)
    shp = (b, nchunk, 8, 256)
    hprev_re, hprev_im, f_re, f_im = _ssm_scan(hloc_re.reshape(shp), hloc_im.reshape(shp), mats,
                                               h0_re.reshape(b, 8, 256), h0_im.reshape(b, 8, 256))
    y = _ssm_out(u2d, hprev_re.reshape(r, -1), hprev_im.reshape(r, -1), mats, rows)
    return y, f_re.reshape(b, N_SSM_GROUPS, SSM_STATE), f_im.reshape(b, N_SSM_GROUPS, SSM_STATE)


def _merge_kernel(x_ref, of_ref, ys_ref, qm_ref, gate_ref, mk_ref, mv_ref,
                  wglu_ref, wbf_ref, wbs_ref, wbm_ref, wo_ref, nf_ref, wr_ref,
                  x1_ref, h2_ref, comb_ref):
    tm = x_ref.shape[0]
    om = []
    for hd in range(N_MEM_HEADS):
        sl = slice(hd * MEM_HEAD_DIM, (hd + 1) * MEM_HEAD_DIM)
        kh = mk_ref[0, :, sl].astype(BF16)
        vh = mv_ref[0, :, sl].astype(BF16)
        sc = _dot_nt(qm_ref[:, sl], kh)
        p = jnp.exp(sc - jnp.max(sc, axis=-1, keepdims=True))
        om.append(_dot(p.astype(BF16), vh) / jnp.sum(p, axis=-1, keepdims=True))
    o_mem = jnp.concatenate(om, axis=-1).astype(BF16)
    z = _dot(ys_ref[...], wglu_ref[...])
    y_ssm = (z[:, 0:SSM_WIDTH] * jax.nn.sigmoid(z[:, SSM_WIDTH:2 * SSM_WIDTH])).astype(BF16)
    g = lambda c: gate_ref[:, c * D_MODEL:(c + 1) * D_MODEL].astype(F32)
    merged = (g(0) * _dot(of_ref[...], wbf_ref[...]) + g(1) * _dot(y_ssm, wbs_ref[...])
              + g(2) * _dot(o_mem, wbm_ref[...]))
    x1 = x_ref[...] + _dot(merged.astype(BF16), wo_ref[...])
    x1_ref[...] = x1
    h2 = x1 * lax.rsqrt(jnp.mean(x1 * x1, axis=-1, keepdims=True) + RMS_EPS) * nf_ref[...]
    h2_ref[...] = h2.astype(BF16)
    logits = _dot_exact(h2, wr_ref[...])
    lane = lax.broadcasted_iota(jnp.int32, (tm, LANES), 1)
    big = jnp.int32(LANES)
    is_grp = (lane >= N_EXPERTS) & (lane < N_EXPERTS + N_EXPERT_GROUPS)
    gl = jnp.where(is_grp, logits, NEG_INF)
    gmax = jnp.max(gl, axis=-1, keepdims=True)
    grp = jnp.min(jnp.where(is_grp & (gl == gmax), lane, big), axis=-1, keepdims=True) - N_EXPERTS
    g_w = 1.0 / jnp.sum(jnp.where(is_grp, jnp.exp(gl - gmax), 0.0), axis=-1, keepdims=True)
    in_grp = (lane >= grp * EXPERTS_PER_GROUP) & (lane < (grp + 1) * EXPERTS_PER_GROUP)
    e1 = jnp.where(in_grp, logits, NEG_INF)
    m1 = jnp.max(e1, axis=-1, keepdims=True)
    i1 = jnp.min(jnp.where(in_grp & (e1 == m1), lane, big), axis=-1, keepdims=True)
    rest = in_grp & (lane != i1)
    e2 = jnp.where(rest, logits, NEG_INF)
    m2 = jnp.max(e2, axis=-1, keepdims=True)
    i2 = jnp.min(jnp.where(rest & (e2 == m2), lane, big), axis=-1, keepdims=True)
    ex = jnp.exp(m2 - m1)
    w1 = g_w / (1.0 + ex)
    w2 = g_w * ex / (1.0 + ex)
    comb_ref[...] = jnp.where(lane == i1, w1, jnp.where(lane == i2, w2, 0.0))


def _merge(x2d, o_fox, ys, q_m, gates, mem_k, mem_v, w, tm, rows_per_batch):
    n = x2d.shape[0]
    assert n % tm == 0 and rows_per_batch % tm == 0
    per = rows_per_batch // tm
    row = lambda width: pl.BlockSpec((tm, width), lambda i: (i, 0))
    memspec = pl.BlockSpec((1, N_MEM, MEM_WIDTH), lambda i: (i // per, 0, 0))
    ws = [w['w_glu'], w['w_br_fox'], w['w_br_ssm'], w['w_br_mem'], w['w_out'], w['norm_ffn'], w['w_router']]
    return pl.pallas_call(
        _merge_kernel,
        out_shape=(jax.ShapeDtypeStruct((n, D_MODEL), F32), jax.ShapeDtypeStruct((n, D_MODEL), BF16),
                   jax.ShapeDtypeStruct((n, LANES), F32)),
        grid=(n // tm,),
        in_specs=[row(D_MODEL), row(FOX_WIDTH), row(SSM_WIDTH), row(MEM_WIDTH), row(3 * D_MODEL), memspec, memspec]
                 + [_full(a.shape) for a in ws],
        out_specs=(row(D_MODEL), row(D_MODEL), row(LANES)),
        compiler_params=_params(("parallel",)), name="merge")(
            x2d, o_fox, ys, q_m, gates, mem_k, mem_v, *ws)


def _moe_kernel(h_ref, comb_ref, x1_ref, wg_ref, wu_ref, wd_ref, o_ref, acc_ref):
    e = pl.program_id(1)

    @pl.when(e == 0)
    def _():
        acc_ref[...] = jnp.zeros(acc_ref.shape, F32)

    h = h_ref[...]
    a = _dot(h, wg_ref[0])
    up = _dot(h, wu_ref[0])
    lane = lax.broadcasted_iota(jnp.int32, comb_ref.shape, 1)
    ce = jnp.sum(jnp.where(lane == e, comb_ref[...], 0.0), axis=-1, keepdims=True)
    act = a * jax.nn.sigmoid(a) * up * ce
    acc_ref[...] += _dot(act.astype(BF16), wd_ref[0])

    @pl.when(e == pl.num_programs(1) - 1)
    def _():
        o_ref[...] = x1_ref[...] + acc_ref[...]


def _moe(h2, comb, x1, wg, wu, wd, tm):
    n = h2.shape[0]
    assert n % tm == 0
    row = lambda width: pl.BlockSpec((tm, width), lambda i, e: (i, 0))
    return pl.pallas_call(
        _moe_kernel,
        out_shape=jax.ShapeDtypeStruct((n, D_MODEL), F32),
        grid=(n // tm, N_EXPERTS),
        in_specs=[row(D_MODEL), row(LANES), row(D_MODEL),
                  pl.BlockSpec((1, D_MODEL, D_EXPERT), lambda i, e: (e, 0, 0)),
                  pl.BlockSpec((1, D_MODEL, D_EXPERT), lambda i, e: (e, 0, 0)),
                  pl.BlockSpec((1, D_EXPERT, D_MODEL), lambda i, e: (e, 0, 0))],
        out_specs=row(D_MODEL),
        scratch_shapes=[pltpu.VMEM((tm, D_MODEL), F32)],
        compiler_params=_params(("parallel", "arbitrary")), name="moe")(h2, comb, x1, wg, wu, wd)


def _prep_weights(p):
    w_in = p['w_in'].astype(BF16)
    o = 0
    wqkv = w_in[:, 0:3 * FOX_WIDTH]
    o = 3 * FOX_WIDTH
    wf = jnp.pad(w_in[:, o:o + N_FOX_HEADS], ((0, 0), (0, LANES - N_FOX_HEADS)))
    o += N_FOX_HEADS
    wqm = w_in[:, o:o + MEM_WIDTH]
    o += MEM_WIDTH
    wu = w_in[:, o:o + SSM_WIDTH]
    o += SSM_WIDTH
    wg = w_in[:, o:o + 3 * D_MODEL]
    r = jnp.arange(FOX_WIDTH) // FOX_HEAD_DIM
    bd = (r[:, None] == r[None, :]).astype(BF16)
    w_router = jnp.concatenate(
        [p['w_router_expert'], p['w_router_group'],
         jnp.zeros((D_MODEL, LANES - N_EXPERTS - N_EXPERT_GROUPS), F32)], axis=1)
    return dict(
        norm_mix=p['norm_mix'].reshape(1, D_MODEL), wqkv=wqkv, wf=wf,
        bf=jnp.pad(p['b_forget'], (0, LANES - N_FOX_HEADS)).reshape(1, LANES),
        wqm=wqm, wu=wu, wg=wg,
        qn_fox=jnp.tile(p['qn_fox'], N_FOX_HEADS).reshape(1, FOX_WIDTH),
        kn_fox=jnp.tile(p['kn_fox'], N_FOX_HEADS).reshape(1, FOX_WIDTH),
        qn_mem=p['qn_mem'].reshape(1, MEM_HEAD_DIM), bd=bd,
        w_glu=p['w_glu'].astype(BF16), w_br_fox=p['w_br_fox'].astype(BF16),
        w_br_ssm=p['w_br_ssm'].astype(BF16), w_br_mem=p['w_br_mem'].astype(BF16),
        w_out=p['w_out'].astype(BF16), norm_ffn=p['norm_ffn'].reshape(1, D_MODEL), w_router=w_router,
        moe_wg=p['moe_w_gate'].astype(BF16), moe_wu=p['moe_w_up'].astype(BF16),
        moe_wd=p['moe_w_down'].astype(BF16))


def _pick_tile(n, target):
    t = min(n, target)
    while n % t:
        t //= 2
    return t


def _group(x, w, mats, mem_k, mem_v, h0_re, h0_im, cache):
    b, s, _ = x.shape
    n = b * s
    x2d = x.reshape(n, D_MODEL)
    prompt = cache is None
    q, kb, vb, k_out, v_out, lf_t, q_m, u, gates = _inproj(x2d, w, _pick_tile(s if prompt else n, 512), s, prompt)
    q3 = q.reshape(b, s, FOX_WIDTH)
    k3 = kb.reshape(b, s, FOX_WIDTH)
    v3 = vb.reshape(b, s, FOX_WIDTH)
    lf_rows = lf_t.reshape(N_FOX_HEADS, b, s).transpose(1, 0, 2)
    lf3 = lf_rows.transpose(0, 2, 1)
    npair = N_FOX_HEADS // 2
    if prompt:
        c_row = _cumsum_rows(lf_rows.reshape(b * N_FOX_HEADS, s)).reshape(b, npair, 2, s)
        c_col = c_row.transpose(0, 1, 3, 2)
        o_fox = _fox_prompt(q3, k3, v3, c_col, c_row, _pick_tile(s, 256), _pick_tile(s, 512))
        unt = lambda a: a.reshape(b, N_FOX_HEADS, FOX_HEAD_DIM, s).transpose(0, 3, 1, 2)
        k4, v4 = unt(k_out), unt(v_out)
    else:
        cache_k, cache_v, cache_logf = cache
        past = cache_k.shape[1]
        lf_all = jnp.concatenate([cache_logf.astype(F32).transpose(0, 2, 1), lf_rows], axis=2)
        c_row = _cumsum_rows(lf_all.reshape(b * N_FOX_HEADS, past + s)).reshape(b, N_FOX_HEADS, past + s)
        o_fox = _fox_sample(q3, cache_k.transpose(0, 2, 3, 1), cache_v.transpose(0, 2, 3, 1), k3, v3,
                            c_row[:, :, past:].transpose(0, 2, 1), c_row[:, :, :past], c_row[:, :, past:],
                            _pick_tile(past, 1024))
        k4 = k_out.reshape(b, s, N_FOX_HEADS, FOX_HEAD_DIM)
        v4 = v_out.reshape(b, s, N_FOX_HEADS, FOX_HEAD_DIM)
    ys, f_re, f_im = _ssm(u, b, h0_re, h0_im, mats)
    tm = _pick_tile(s, 512)
    x1, h2, comb = _merge(x2d, o_fox.reshape(n, FOX_WIDTH), ys, q_m, gates, mem_k, mem_v, w, tm, s)
    y = _moe(h2, comb, x1, w['moe_wg'], w['moe_wu'], w['moe_wd'], _pick_tile(n, 1024))
    return y.reshape(b, s, D_MODEL), k4, v4, lf3, f_re, f_im


def kernel(x_prompt, x_sample, mem_prompt, cache_fox_k, cache_fox_v, cache_fox_logf, state_ssm_re, state_ssm_im,
           cache_mem_k, cache_mem_v, norm_mix, w_in, b_forget, qn_fox, kn_fox, qn_mem, kn_mem, norm_mem, w_mem_kv,
           ssm_a_re, ssm_a_im, ssm_log_dt, ssm_b_re, ssm_b_im, ssm_c_re, ssm_c_im, ssm_d, w_glu, w_br_fox,
           w_br_ssm, w_br_mem, w_out, norm_ffn, w_router_group, w_router_expert, moe_w_gate, moe_w_up,
           moe_w_down):
    depth = norm_mix.shape[0]
    assert depth == 1
    l = 0
    p = dict(norm_mix=norm_mix[l], w_in=w_in[l], b_forget=b_forget[l], qn_fox=qn_fox[l], kn_fox=kn_fox[l],
             qn_mem=qn_mem[l], ssm_a_re=ssm_a_re[l], ssm_a_im=ssm_a_im[l], ssm_log_dt=ssm_log_dt[l],
             ssm_b_re=ssm_b_re[l], ssm_b_im=ssm_b_im[l], ssm_c_re=ssm_c_re[l], ssm_c_im=ssm_c_im[l],
             ssm_d=ssm_d[l], w_glu=w_glu[l], w_br_fox=w_br_fox[l], w_br_ssm=w_br_ssm[l], w_br_mem=w_br_mem[l],
             w_out=w_out[l], norm_ffn=norm_ffn[l], w_router_group=w_router_group[l],
             w_router_expert=w_router_expert[l], moe_w_gate=moe_w_gate[l], moe_w_up=moe_w_up[l],
             moe_w_down=moe_w_down[l])
    w = _prep_weights(p)
    mats = _ssm_mats(p)
    bp, sp, _ = x_prompt.shape
    bs, ss, _ = x_sample.shape

    mk, mv = _memkv(mem_prompt.reshape(bp * N_MEM, D_MODEL), norm_mem[l].reshape(1, D_MODEL),
                    w_mem_kv[l].astype(BF16), kn_mem[l].reshape(1, MEM_HEAD_DIM), _pick_tile(bp * N_MEM, 512))
    mk = mk.reshape(bp, N_MEM, MEM_WIDTH)
    mv = mv.reshape(bp, N_MEM, MEM_WIDTH)
    zeros = jnp.zeros((bp, N_SSM_GROUPS, SSM_STATE), F32)
    yp, pk, pv, plf, pre, pim = _group(x_prompt, w, mats, mk, mv, zeros, zeros, None)
    cache = (cache_fox_k[l], cache_fox_v[l], cache_fox_logf[l])
    ys, sk, sv, slf, sre, sim = _group(
        x_sample, w, mats, cache_mem_k[l].reshape(bs, N_MEM, MEM_WIDTH), cache_mem_v[l].reshape(bs, N_MEM, MEM_WIDTH),
        state_ssm_re[l].astype(F32), state_ssm_im[l].astype(F32), cache)
    st = lambda a: a[None]
    return (yp, ys, st(pk), st(pv), st(plf), st(pre), st(pim),
            st(mk.reshape(bp, N_MEM, N_MEM_HEADS, MEM_HEAD_DIM)), st(mv.reshape(bp, N_MEM, N_MEM_HEADS, MEM_HEAD_DIM)),
            st(sk), st(sv), st(slf), st(sre), st(sim))
```

```python
import functools
import math

import jax
import jax.numpy as jnp
from jax import lax
from jax.experimental import pallas as pl
from jax.experimental.pallas import tpu as pltpu

F32 = jnp.float32
BF16 = jnp.bfloat16

D_MODEL = 1024
N_FOX_HEADS = 8
FOX_HEAD_DIM = 64
FOX_WIDTH = N_FOX_HEADS * FOX_HEAD_DIM
N_MEM = 256
N_MEM_HEADS = 4
MEM_HEAD_DIM = 128
MEM_WIDTH = N_MEM_HEADS * MEM_HEAD_DIM
SSM_GROUP = 16
SSM_WIDTH = 512
N_SSM_GROUPS = SSM_WIDTH // SSM_GROUP
SSM_STATE = 64
N_EXPERT_GROUPS = 4
EXPERTS_PER_GROUP = 8
N_EXPERTS = N_EXPERT_GROUPS * EXPERTS_PER_GROUP
D_EXPERT = 256
RMS_EPS = 1e-6
NEG_INF = -1e30

LANES = 128
SSM_CHUNK = 16
SSM_GPB = LANES // SSM_GROUP
SLAB_ROWS = D_MODEL // LANES
PAIRS_PER_GROUP = EXPERTS_PER_GROUP * (EXPERTS_PER_GROUP - 1) // 2
N_BUCKETS = N_EXPERT_GROUPS * PAIRS_PER_GROUP
MOE_TILE = 128
SPARSE_MIN_TOKENS = N_BUCKETS * MOE_TILE
VMEM_LIMIT = 56 * 1024 * 1024


def _dot(a, b):
    return jnp.dot(a, b, preferred_element_type=F32)


def _dot_nt(a, b):
    return lax.dot_general(a, b, (((1,), (1,)), ((), ())), preferred_element_type=F32)


def _dot_exact(a, b):
    return jnp.dot(a, b, preferred_element_type=F32, precision=lax.Precision.HIGHEST)


def _split_bf16(x):
    hi = x.astype(BF16)
    lo = (x - hi.astype(F32)).astype(BF16)
    return hi, lo


def _params(sem):
    return pltpu.CompilerParams(dimension_semantics=sem, vmem_limit_bytes=VMEM_LIMIT)


def _full(shape):
    n = len(shape)
    return pl.BlockSpec(shape, lambda *_: (0,) * n)


def _inproj_kernel(x_ref, g_ref, wqkv_ref, wf_ref, bf_ref, wqm_ref, wu_ref, wg_ref,
                   qn_ref, kn_ref, qmn_ref, bd_ref,
                   q_ref, kb_ref, vb_ref, k_ref, v_ref, lf_ref, qm_ref, u_ref, gate_ref, *, kv_transposed):
    x = x_ref[...]
    h = x * lax.rsqrt(jnp.mean(x * x, axis=-1, keepdims=True) + RMS_EPS) * g_ref[...]
    hb = h.astype(BF16)

    def head_norm(z, gain):
        hi, lo = _split_bf16(z * z)
        ss = _dot(hi, bd_ref[...]) + _dot(lo, bd_ref[...])
        return z * lax.rsqrt(ss * (1.0 / FOX_HEAD_DIM) + RMS_EPS) * gain

    zq = _dot(hb, wqkv_ref[:, 0:FOX_WIDTH])
    q_ref[...] = (head_norm(zq, qn_ref[...]) * (FOX_HEAD_DIM ** -0.5)).astype(BF16)
    zk = _dot(hb, wqkv_ref[:, FOX_WIDTH:2 * FOX_WIDTH])
    kn = head_norm(zk, kn_ref[...])
    zv = _dot(hb, wqkv_ref[:, 2 * FOX_WIDTH:3 * FOX_WIDTH])
    kb_ref[...] = kn.astype(BF16)
    vb_ref[...] = zv.astype(BF16)
    tm = x_ref.shape[0]
    if kv_transposed:
        k_ref[0] = kn.T
        v_ref[0] = zv.T
    else:
        for hd in range(N_FOX_HEADS):
            hs = slice(hd * FOX_HEAD_DIM, (hd + 1) * FOX_HEAD_DIM)
            rows = pl.ds(hd, tm, stride=N_FOX_HEADS)
            k_ref[rows, :] = kn[:, hs]
            v_ref[rows, :] = zv[:, hs]

    zf = (_dot(hb, wf_ref[...]) + bf_ref[...]).T[0:N_FOX_HEADS, :]
    lf_ref[...] = jnp.minimum(zf, 0.0) - jnp.log1p(jnp.exp(-jnp.abs(zf)))

    zm = _dot(hb, wqm_ref[...])
    for hd in range(N_MEM_HEADS):
        sl = slice(hd * MEM_HEAD_DIM, (hd + 1) * MEM_HEAD_DIM)
        zh = zm[:, sl]
        ms = jnp.mean(zh * zh, axis=-1, keepdims=True)
        qm_ref[:, sl] = (zh * lax.rsqrt(ms + RMS_EPS) * qmn_ref[...] * (MEM_HEAD_DIM ** -0.5)).astype(BF16)

    u_ref[...] = _dot(hb, wu_ref[...])
    for c in range(3):
        sl = slice(c * D_MODEL, (c + 1) * D_MODEL)
        gate_ref[:, sl] = jax.nn.sigmoid(_dot(hb, wg_ref[:, sl])).astype(BF16)


def _inproj(x2d, w, tm, seq, kv_transposed):
    n = x2d.shape[0]
    assert n % tm == 0
    row = lambda width: pl.BlockSpec((tm, width), lambda i: (i, 0))
    if kv_transposed:
        assert seq % tm == 0
        per = seq // tm
        kv_shape = jax.ShapeDtypeStruct((n // seq, FOX_WIDTH, seq), F32)
        heads = pl.BlockSpec((1, FOX_WIDTH, tm), lambda i: (i // per, 0, i % per))
    else:
        kv_shape = jax.ShapeDtypeStruct((n * N_FOX_HEADS, FOX_HEAD_DIM), F32)
        heads = pl.BlockSpec((tm * N_FOX_HEADS, FOX_HEAD_DIM), lambda i: (i, 0))
    ins = [x2d, w['norm_mix'], w['wqkv'], w['wf'], w['bf'], w['wqm'], w['wu'], w['wg'],
           w['qn_fox'], w['kn_fox'], w['qn_mem'], w['bd']]
    in_specs = [row(D_MODEL)] + [_full(a.shape) for a in ins[1:]]
    out_shape = (
        jax.ShapeDtypeStruct((n, FOX_WIDTH), BF16),
        jax.ShapeDtypeStruct((n, FOX_WIDTH), BF16),
        jax.ShapeDtypeStruct((n, FOX_WIDTH), BF16),
        kv_shape,
        kv_shape,
        jax.ShapeDtypeStruct((N_FOX_HEADS, n), F32),
        jax.ShapeDtypeStruct((n, MEM_WIDTH), BF16),
        jax.ShapeDtypeStruct((n, SSM_WIDTH), F32),
        jax.ShapeDtypeStruct((n, 3 * D_MODEL), BF16),
    )
    out_specs = (row(FOX_WIDTH), row(FOX_WIDTH), row(FOX_WIDTH), heads, heads,
                 pl.BlockSpec((N_FOX_HEADS, tm), lambda i: (0, i)),
                 row(MEM_WIDTH), row(SSM_WIDTH), row(3 * D_MODEL))
    return pl.pallas_call(
        functools.partial(_inproj_kernel, kv_transposed=kv_transposed),
        out_shape=out_shape, grid=(n // tm,), in_specs=in_specs, out_specs=out_specs,
        compiler_params=_params(("parallel",)), name="inproj")(*ins)


def _memkv_kernel(x_ref, g_ref, w_ref, kn_ref, k_ref, v_ref):
    x = x_ref[...]
    h = x * lax.rsqrt(jnp.mean(x * x, axis=-1, keepdims=True) + RMS_EPS) * g_ref[...]
    hb = h.astype(BF16)
    zk = _dot(hb, w_ref[:, 0:MEM_WIDTH])
    for hd in range(N_MEM_HEADS):
        sl = slice(hd * MEM_HEAD_DIM, (hd + 1) * MEM_HEAD_DIM)
        zh = zk[:, sl]
        ms = jnp.mean(zh * zh, axis=-1, keepdims=True)
        k_ref[:, sl] = zh * lax.rsqrt(ms + RMS_EPS) * kn_ref[...]
    v_ref[...] = _dot(hb, w_ref[:, MEM_WIDTH:2 * MEM_WIDTH])


def _memkv(mem2d, norm_mem, w_kv, kn_mem, tm):
    n = mem2d.shape[0]
    row = lambda width: pl.BlockSpec((tm, width), lambda i: (i, 0))
    return pl.pallas_call(
        _memkv_kernel,
        out_shape=(jax.ShapeDtypeStruct((n, MEM_WIDTH), F32), jax.ShapeDtypeStruct((n, MEM_WIDTH), F32)),
        grid=(n // tm,),
        in_specs=[row(D_MODEL), _full(norm_mem.shape), _full(w_kv.shape), _full(kn_mem.shape)],
        out_specs=(row(MEM_WIDTH), row(MEM_WIDTH)),
        compiler_params=_params(("parallel",)), name="memkv")(mem2d, norm_mem, w_kv, kn_mem)


CUMSUM_BLOCK = 256


def _cumsum_kernel(x_ref, o_ref):
    nblk = x_ref.shape[1] // CUMSUM_BLOCK
    r = lax.broadcasted_iota(jnp.int32, (CUMSUM_BLOCK, CUMSUM_BLOCK), 0)
    c = lax.broadcasted_iota(jnp.int32, (CUMSUM_BLOCK, CUMSUM_BLOCK), 1)
    tri = (r <= c).astype(F32)
    carry = jnp.zeros((x_ref.shape[0], 1), F32)
    for j in range(nblk):
        sl = slice(j * CUMSUM_BLOCK, (j + 1) * CUMSUM_BLOCK)
        cs = _dot_exact(x_ref[:, sl], tri) + carry
        o_ref[:, sl] = cs
        carry = cs[:, CUMSUM_BLOCK - 1:CUMSUM_BLOCK]


def _cumsum_rows(x):
    rows, n = x.shape
    npad = -(-n // CUMSUM_BLOCK) * CUMSUM_BLOCK
    xp = jnp.pad(x, ((0, 0), (0, npad - n))) if npad != n else x
    out = pl.pallas_call(
        _cumsum_kernel, out_shape=jax.ShapeDtypeStruct((rows, npad), F32), grid=(1,),
        in_specs=[_full((rows, npad))], out_specs=_full((rows, npad)),
        compiler_params=_params(("arbitrary",)), name="cumsum")(xp)
    return out[:, :n] if npad != n else out


def _reduce_rows(x, op):
    rows, cols = x.shape
    if rows > 64 and rows % 64 == 0:
        x = op(x.reshape(rows // 64, 64, cols), axis=0)
        rows = 64
    if rows == 64:
        x = op(x.reshape(8, 8, cols), axis=0)
    return op(x, axis=0, keepdims=True)


def _head_lane_masks(rows):
    lane = lax.broadcasted_iota(jnp.int32, (rows, LANES), 1)
    return lane < FOX_HEAD_DIM


def _softmax_step(qh, kb, vb, ck, cq, m, l, acc, mask):
    t = _dot_nt(qh, kb) - ck
    if mask is not None:
        t = jnp.where(mask, t, NEG_INF)
    m_new = jnp.maximum(m, jnp.max(t, axis=-1, keepdims=True) + cq)
    alpha = jnp.exp(m - m_new)
    p = jnp.exp(t + (cq - m_new))
    l_new = alpha * l + jnp.sum(p, axis=-1, keepdims=True)
    acc_new = alpha * acc + _dot(p.astype(BF16), vb)
    return m_new, l_new, acc_new


def _fox_prompt_kernel(q_ref, k_ref, v_ref, cc_ref, cr_ref, o_ref,
                       vt_ref, ck0_ref, ck1_ref, st_ref, pt_ref, acc_ref, *, tq, tk):
    i = pl.program_id(2)
    s_len = k_ref.shape[1]

    @pl.when(i == 0)
    def _():
        vt_ref[...] = v_ref[0].astype(F32).T.astype(BF16)
        ck0_ref[...] = jnp.broadcast_to(cc_ref[0, 0, :, 0:1], (s_len, LANES))
        ck1_ref[...] = jnp.broadcast_to(cc_ref[0, 0, :, 1:2], (s_len, LANES))

    qt = q_ref[0].astype(F32).T
    row = lax.broadcasted_iota(jnp.int32, (LANES, tq), 0)
    qts = (jnp.where(row < FOX_HEAD_DIM, qt, 0.0).astype(BF16), jnp.where(row < FOX_HEAD_DIM, 0.0, qt).astype(BF16))
    q0 = pl.multiple_of(i * tq, tq)
    cq = cr_ref[0, 0, :, pl.ds(q0, tq)]
    ck_refs = (ck0_ref, ck1_ref)

    def stage_a(n):
        s = pl.multiple_of(n * tk, tk)
        kb = k_ref[0, pl.ds(s, tk), :]
        for hh in range(2):
            ck = ck_refs[hh][pl.ds(s, tk), :]
            st_ref[n & 1, hh] = _dot(kb, qts[hh]) - jnp.concatenate([ck] * (tq // LANES), axis=1)

    def stage_b(n, stats, masked):
        if masked:
            kpos = n * tk + lax.broadcasted_iota(jnp.int32, (tk, tq), 0)
            qpos = q0 + lax.broadcasted_iota(jnp.int32, (tk, tq), 1)
            mask = kpos <= qpos
        out = []
        for hh in range(2):
            m, l = stats[2 * hh:2 * hh + 2]
            t = st_ref[n & 1, hh]
            if masked:
                t = jnp.where(mask, t, NEG_INF)
            cqh = cq[hh:hh + 1, :]
            m_new = jnp.maximum(m, _reduce_rows(t, jnp.max) + cqh)
            alpha = jnp.exp(m - m_new)
            p = jnp.exp(t + (cqh - m_new))
            pt_ref[n & 1, hh] = p.astype(BF16)
            out.extend([m_new, alpha * l + _reduce_rows(p, jnp.sum), alpha])
        return out

    def stage_c(n, alphas):
        s = pl.multiple_of(jnp.maximum(n, 0) * tk, tk)
        for hh in range(2):
            vt = vt_ref[hh * FOX_HEAD_DIM:(hh + 1) * FOX_HEAD_DIM, pl.ds(s, tk)]
            acc_ref[hh] = alphas[hh] * acc_ref[hh] + _dot(vt, pt_ref[n & 1, hh])

    acc_ref[...] = jnp.zeros(acc_ref.shape, F32)
    pt_ref[1] = jnp.zeros(pt_ref.shape[1:], BF16)
    neg = jnp.full((1, tq), NEG_INF, F32)
    zero = jnp.zeros((1, tq), F32)
    one = jnp.ones((1, tq), F32)
    nfull = (i * tq) // tk
    stage_a(0)

    def body(n, carry):
        m0, l0, al0, m1, l1, al1 = carry
        stage_c(n - 1, (al0, al1))
        new = stage_b(n, (m0, l0, m1, l1), False)
        stage_a(n + 1)
        return tuple(new)

    m0, l0, al0, m1, l1, al1 = lax.fori_loop(0, nfull, body, (neg, zero, one, neg, zero, one))
    stage_c(nfull - 1, (al0, al1))
    _, l0, be0, _, l1, be1 = stage_b(nfull, (m0, l0, m1, l1), True)
    stage_c(nfull, (be0, be1))
    ot = jnp.concatenate([acc_ref[0] / l0, acc_ref[1] / l1], axis=0)
    o_ref[0] = ot.T.astype(o_ref.dtype)


def _fox_prompt(q, k, v, c_col, c_row, tq, tk):
    b, s, _ = q.shape
    assert s % tk == 0 and tk % tq == 0
    npair = N_FOX_HEADS // 2
    return pl.pallas_call(
        functools.partial(_fox_prompt_kernel, tq=tq, tk=tk),
        out_shape=jax.ShapeDtypeStruct((b, s, FOX_WIDTH), BF16),
        grid=(b, npair, s // tq),
        in_specs=[
            pl.BlockSpec((1, tq, LANES), lambda bi, hp, i: (bi, i, hp)),
            pl.BlockSpec((1, s, LANES), lambda bi, hp, i: (bi, 0, hp)),
            pl.BlockSpec((1, s, LANES), lambda bi, hp, i: (bi, 0, hp)),
            pl.BlockSpec((1, 1, s, 2), lambda bi, hp, i: (bi, hp, 0, 0)),
            pl.BlockSpec((1, 1, 2, s), lambda bi, hp, i: (bi, hp, 0, 0)),
        ],
        out_specs=pl.BlockSpec((1, tq, LANES), lambda bi, hp, i: (bi, i, hp)),
        scratch_shapes=[pltpu.VMEM((LANES, s), BF16),
                        pltpu.VMEM((s, LANES), F32), pltpu.VMEM((s, LANES), F32),
                        pltpu.VMEM((2, 2, tk, tq), F32), pltpu.VMEM((2, 2, tk, tq), BF16),
                        pltpu.VMEM((2, FOX_HEAD_DIM, tq), F32)],
        compiler_params=_params(("parallel", "parallel", "arbitrary")), name="fox_prompt")(q, k, v, c_col, c_row)


def _fox_sample_kernel(q_ref, ck_ref, cv_ref, nk_ref, nv_ref, cq_ref, crc_ref, crn_ref, o_ref, *state, n):
    j = pl.program_id(1)
    nj = pl.num_programs(1)
    m_refs = state[0:N_FOX_HEADS]
    l_refs = state[N_FOX_HEADS:2 * N_FOX_HEADS]
    acc_refs = state[2 * N_FOX_HEADS:3 * N_FOX_HEADS]

    @pl.when(j == 0)
    def _():
        for hd in range(N_FOX_HEADS):
            m_refs[hd][...] = jnp.full(m_refs[hd].shape, NEG_INF, F32)
            l_refs[hd][...] = jnp.zeros(l_refs[hd].shape, F32)
            acc_refs[hd][...] = jnp.zeros(acc_refs[hd].shape, F32)

    def update(k_of, v_of, cr_ref_, mask, transposed):
        qk = _dot if transposed else _dot_nt
        pv = _dot_nt if transposed else _dot
        ts = []
        for hd in range(N_FOX_HEADS):
            hs = slice(hd * FOX_HEAD_DIM, (hd + 1) * FOX_HEAD_DIM)
            t = qk(q_ref[0, :, hs], k_of(hd)) - cr_ref_[0, hd:hd + 1, :]
            ts.append(t if mask is None else jnp.where(mask, t, NEG_INF))
        ps = []
        for hd in range(N_FOX_HEADS):
            cq = cq_ref[0, :, hd:hd + 1]
            m = m_refs[hd][...]
            m_new = jnp.maximum(m, jnp.max(ts[hd], axis=-1, keepdims=True) + cq)
            alpha = jnp.exp(m - m_new)
            p = jnp.exp(ts[hd] + (cq - m_new))
            m_refs[hd][...] = m_new
            l_refs[hd][...] = alpha * l_refs[hd][...] + jnp.sum(p, axis=-1, keepdims=True)
            ps.append((alpha, p.astype(BF16)))
        for hd in range(N_FOX_HEADS):
            alpha, p = ps[hd]
            acc_refs[hd][...] = alpha * acc_refs[hd][...] + pv(p, v_of(hd))

    update(lambda hd: ck_ref[0, hd].astype(BF16), lambda hd: cv_ref[0, hd].astype(BF16), crc_ref, None, True)

    @pl.when(j == nj - 1)
    def _():
        r = lax.broadcasted_iota(jnp.int32, (n, n), 0)
        c = lax.broadcasted_iota(jnp.int32, (n, n), 1)
        head = lambda ref: (lambda hd: ref[0, :, hd * FOX_HEAD_DIM:(hd + 1) * FOX_HEAD_DIM])
        update(head(nk_ref), head(nv_ref), crn_ref, c <= r, False)
        for hd in range(N_FOX_HEADS):
            hs = slice(hd * FOX_HEAD_DIM, (hd + 1) * FOX_HEAD_DIM)
            o_ref[0, :, hs] = (acc_refs[hd][...] / l_refs[hd][...]).astype(o_ref.dtype)


def _fox_sample(q, cache_k, cache_v, k_new, v_new, c_q, c_row_cache, c_row_new, tk):
    b, n, _ = q.shape
    past = cache_k.shape[3]
    assert past % tk == 0
    cache_spec = pl.BlockSpec((1, N_FOX_HEADS, FOX_HEAD_DIM, tk), lambda bi, j: (bi, 0, 0, j))
    return pl.pallas_call(
        functools.partial(_fox_sample_kernel, n=n),
        out_shape=jax.ShapeDtypeStruct((b, n, FOX_WIDTH), BF16),
        grid=(b, past // tk),
        in_specs=[
            pl.BlockSpec((1, n, FOX_WIDTH), lambda bi, j: (bi, 0, 0)),
            cache_spec,
            cache_spec,
            pl.BlockSpec((1, n, FOX_WIDTH), lambda bi, j: (bi, 0, 0)),
            pl.BlockSpec((1, n, FOX_WIDTH), lambda bi, j: (bi, 0, 0)),
            pl.BlockSpec((1, n, N_FOX_HEADS), lambda bi, j: (bi, 0, 0)),
            pl.BlockSpec((1, N_FOX_HEADS, tk), lambda bi, j: (bi, 0, j)),
            pl.BlockSpec((1, N_FOX_HEADS, n), lambda bi, j: (bi, 0, 0)),
        ],
        out_specs=pl.BlockSpec((1, n, FOX_WIDTH), lambda bi, j: (bi, 0, 0)),
        scratch_shapes=([pltpu.VMEM((n, 1), F32)] * (2 * N_FOX_HEADS)
                        + [pltpu.VMEM((n, FOX_HEAD_DIM), F32)] * N_FOX_HEADS),
        compiler_params=_params(("parallel", "arbitrary")), name="fox_sample")(
            q, cache_k, cache_v, k_new, v_new, c_q, c_row_cache, c_row_new)


def _ssm_mats(p):
    f32 = F32
    a_re, a_im = p['ssm_a_re'].astype(f32), p['ssm_a_im'].astype(f32)
    b_re, b_im = p['ssm_b_re'].astype(f32), p['ssm_b_im'].astype(f32)
    c_re, c_im = p['ssm_c_re'].astype(f32), p['ssm_c_im'].astype(f32)
    dt = jnp.exp(p['ssm_log_dt'].astype(f32))[:, None]
    mag = jnp.exp(dt * a_re)
    ab_re = mag * jnp.cos(dt * a_im)
    ab_im = mag * jnp.sin(dt * a_im)
    den = a_re * a_re + a_im * a_im
    nr, ni = ab_re - 1.0, ab_im
    coef_re = (nr * a_re + ni * a_im) / den
    coef_im = (ni * a_re - nr * a_im) / den
    bb_re = coef_re[..., None] * b_re - coef_im[..., None] * b_im
    bb_im = coef_re[..., None] * b_im + coef_im[..., None] * b_re
    pr, pi = [jnp.ones_like(ab_re)], [jnp.zeros_like(ab_im)]
    for _ in range(SSM_CHUNK):
        pr.append(pr[-1] * ab_re - pi[-1] * ab_im)
        pi.append(pr[-2] * ab_im + pi[-1] * ab_re)
    pw_re, pw_im = jnp.stack(pr), jnp.stack(pi)
    T = SSM_CHUNK
    w_re = pw_re[..., None] * bb_re[None] - pw_im[..., None] * bb_im[None]
    w_im = pw_re[..., None] * bb_im[None] + pw_im[..., None] * bb_re[None]
    kk = (jnp.einsum('gop,kgpi->kgoi', c_re, w_re[:T], precision='highest')
          - jnp.einsum('gop,kgpi->kgoi', c_im, w_im[:T], precision='highest'))
    eye = jnp.eye(SSM_GPB, dtype=f32)
    nq = N_SSM_GROUPS // SSM_GPB

    def lane_diag(m):
        lead = m.shape[:-3]
        i, c = m.shape[-2:]
        m = m.reshape(lead + (nq, SSM_GPB, i, 1, c)) * eye[:, None, :, None]
        return m.reshape(lead + (nq, SSM_GPB * i, SSM_GPB * c))

    ktau = lane_diag(jnp.swapaxes(kk, -1, -2))
    ktau = jnp.concatenate([jnp.zeros_like(ktau[:1]), ktau], axis=0)
    units = []
    for dlag in range(T // 2 - 1, -1, -1):
        top = jnp.concatenate([ktau[2 * dlag + 1], ktau[2 * dlag + 2]], axis=-1)
        bot = jnp.concatenate([ktau[2 * dlag], ktau[2 * dlag + 1]], axis=-1)
        units.append(jnp.concatenate([top, bot], axis=-2))
    kstack = jnp.concatenate(units, axis=-2).astype(BF16)
    rev = T - 1 - jnp.arange(T)
    m_re = lane_diag(jnp.swapaxes(w_re[rev], -1, -2))
    m_im = lane_diag(jnp.swapaxes(w_im[rev], -1, -2))
    m_all = jnp.concatenate([m_re, m_im], axis=-1)
    m_all = jnp.swapaxes(m_all, 0, 1).reshape(nq, T * LANES, 2 * SSM_GPB * SSM_STATE)
    m_hi, m_lo = _split_bf16(m_all)
    ar, ai = pw_re[1:], pw_im[1:]
    n_re = (c_re[None] * ar[:, :, None, :] - c_im[None] * ai[:, :, None, :])
    n_im = -(c_re[None] * ai[:, :, None, :] + c_im[None] * ar[:, :, None, :])

    def state_rows(n):
        n = lane_diag(n)
        return jnp.transpose(n, (1, 3, 0, 2)).reshape(nq, SSM_GPB * SSM_STATE, T * LANES)

    n_all = jnp.concatenate([state_rows(n_re), state_rows(n_im)], axis=1).astype(BF16)
    return dict(kstack=kstack, m_hi=m_hi, m_lo=m_lo, n_all=n_all,
                a16_re=pw_re[T].reshape(8, 256), a16_im=pw_im[T].reshape(8, 256),
                d=p['ssm_d'].astype(f32).reshape(1, SSM_WIDTH))


def _chunk_tokens(u_ref, rows):
    return [u_ref[pl.ds(t, rows, stride=SSM_CHUNK), :] for t in range(SSM_CHUNK)]


def _ssm_local_kernel(u_ref, mh_ref, ml_ref, hre_ref, him_ref):
    rows = hre_ref.shape[0]
    parts = [_split_bf16(ut) for ut in _chunk_tokens(u_ref, rows)]
    x_hi = jnp.concatenate([h for h, _ in parts], axis=1)
    x_lo = jnp.concatenate([l for _, l in parts], axis=1)
    h = _dot(x_hi, mh_ref[0]) + _dot(x_hi, ml_ref[0]) + _dot(x_lo, mh_ref[0])
    half = SSM_GPB * SSM_STATE
    hre_ref[...] = h[:, 0:half]
    him_ref[...] = h[:, half:2 * half]


def _ssm_local(u2d, mats, rows):
    n = u2d.shape[0]
    r = n // SSM_CHUNK
    nq = N_SSM_GROUPS // SSM_GPB
    half = SSM_GPB * SSM_STATE
    mspec = pl.BlockSpec((1, SSM_CHUNK * LANES, 2 * half), lambda q, i: (q, 0, 0))
    ospec = pl.BlockSpec((rows, half), lambda q, i: (i, q))
    return pl.pallas_call(
        _ssm_local_kernel,
        out_shape=(jax.ShapeDtypeStruct((r, N_SSM_GROUPS * SSM_STATE), F32),) * 2,
        grid=(nq, r // rows),
        in_specs=[pl.BlockSpec((rows * SSM_CHUNK, LANES), lambda q, i: (i, q)), mspec, mspec],
        out_specs=(ospec, ospec),
        compiler_params=_params(("parallel", "parallel")), name="ssm_local")(u2d, mats['m_hi'], mats['m_lo'])


def _ssm_scan_kernel(lre_ref, lim_ref, are_ref, aim_ref, h0re_ref, h0im_ref,
                     pre_ref, pim_ref, fre_ref, fim_ref):
    nchunk = lre_ref.shape[1]
    ar, ai = are_ref[...], aim_ref[...]

    def body(c, carry):
        hr, hi = carry
        pre_ref[0, c] = hr
        pim_ref[0, c] = hi
        return (ar * hr - ai * hi + lre_ref[0, c], ar * hi + ai * hr + lim_ref[0, c])

    hr, hi = lax.fori_loop(0, nchunk, body, (h0re_ref[0], h0im_ref[0]))
    fre_ref[0] = hr
    fim_ref[0] = hi


def _ssm_scan(hloc_re, hloc_im, mats, h0_re, h0_im):
    b, nchunk = hloc_re.shape[:2]
    big = pl.BlockSpec((1, nchunk, 8, 256), lambda i: (i, 0, 0, 0))
    small = pl.BlockSpec((1, 8, 256), lambda i: (i, 0, 0))
    return pl.pallas_call(
        _ssm_scan_kernel,
        out_shape=(jax.ShapeDtypeStruct(hloc_re.shape, F32),) * 2 + (jax.ShapeDtypeStruct((b, 8, 256), F32),) * 2,
        grid=(b,),
        in_specs=[big, big, _full((8, 256)), _full((8, 256)), small, small],
        out_specs=(big, big, small, small),
        compiler_params=_params(("parallel",)), name="ssm_scan")(
            hloc_re, hloc_im, mats['a16_re'], mats['a16_im'], h0_re, h0_im)


def _gelu_tanh(y):
    return 0.5 * y * (1.0 + jnp.tanh(math.sqrt(2.0 / math.pi) * (y + 0.044715 * (y * y * y))))


def _ssm_out_kernel(u_ref, k_ref, pre_ref, pim_ref, n_ref, d_ref, y_ref, ysc_ref):
    rows = pre_ref.shape[0]
    us = _chunk_tokens(u_ref, rows)
    x = jnp.concatenate([ut.astype(BF16) for ut in us], axis=1)
    hp = jnp.concatenate([pre_ref[...], pim_ref[...]], axis=1).astype(BF16)
    unit = 2 * LANES
    nunit = SSM_CHUNK // 2
    for j in range(nunit):
        yj = (_dot(x[:, 0:unit * (j + 1)], k_ref[0, unit * (nunit - 1 - j):, :])
              + _dot(hp, n_ref[0, :, unit * j:unit * (j + 1)]))
        for t2 in range(2):
            t = 2 * j + t2
            y = yj[:, t2 * LANES:(t2 + 1) * LANES] + d_ref[...] * us[t]
            ysc_ref[pl.ds(t, rows, stride=SSM_CHUNK), :] = _gelu_tanh(y)
    y_ref[...] = ysc_ref[...].astype(y_ref.dtype)


def _ssm_out(u2d, hprev_re, hprev_im, mats, rows):
    n = u2d.shape[0]
    r = n // SSM_CHUNK
    nq = N_SSM_GROUPS // SSM_GPB
    half = SSM_GPB * SSM_STATE
    uspec = pl.BlockSpec((rows * SSM_CHUNK, LANES), lambda q, i: (i, q))
    hspec = pl.BlockSpec((rows, half), lambda q, i: (i, q))
    return pl.pallas_call(
        _ssm_out_kernel,
        out_shape=jax.ShapeDtypeStruct((n, SSM_WIDTH), BF16),
        grid=(nq, r // rows),
        in_specs=[uspec, pl.BlockSpec((1, SSM_CHUNK * LANES, 2 * LANES), lambda q, i: (q, 0, 0)), hspec, hspec,
                  pl.BlockSpec((1, 2 * half, SSM_CHUNK * LANES), lambda q, i: (q, 0, 0)),
                  pl.BlockSpec((1, LANES), lambda q, i: (0, q))],
        out_specs=uspec,
        scratch_shapes=[pltpu.VMEM((rows * SSM_CHUNK, LANES), F32)],
        compiler_params=_params(("parallel", "parallel")), name="ssm_out")(
            u2d, mats['kstack'], hprev_re, hprev_im, mats['n_all'], mats['d'])


def _ssm(u2d, b, h0_re, h0_im, mats):
    n = u2d.shape[0]
    nchunk = n // b // SSM_CHUNK
    r = b * nchunk
    rows = _pick_tile(r, 256)
    hloc_re, hloc_im = _ssm_local(u2d, mats, rows)
    shp = (b, nchunk, 8, 256)
    hprev_re, hprev_im, f_re, f_im = _ssm_scan(hloc_re.reshape(shp), hloc_im.reshape(shp), mats,
                                               h0_re.reshape(b, 8, 256), h0_im.reshape(b, 8, 256))
    y = _ssm_out(u2d, hprev_re.reshape(r, -1), hprev_im.reshape(r, -1), mats, rows)
    return y, f_re.reshape(b, N_SSM_GROUPS, SSM_STATE), f_im.reshape(b, N_SSM_GROUPS, SSM_STATE)


def _merge_kernel(x_ref, of_ref, ys_ref, qm_ref, gate_ref, mk_ref, mv_ref,
                  wglu_ref, wbf_ref, wbs_ref, wbm_ref, wo_ref, nf_ref, wr_ref,
                  x1_ref, h2_ref, r_ref, *sparse_refs, sparse):
    tm = x_ref.shape[0]
    om = []
    for hd in range(N_MEM_HEADS):
        sl = slice(hd * MEM_HEAD_DIM, (hd + 1) * MEM_HEAD_DIM)
        kh = mk_ref[0, :, sl].astype(BF16)
        vh = mv_ref[0, :, sl].astype(BF16)
        sc = _dot_nt(qm_ref[:, sl], kh)
        p = jnp.exp(sc - jnp.max(sc, axis=-1, keepdims=True))
        om.append(_dot(p.astype(BF16), vh) / jnp.sum(p, axis=-1, keepdims=True))
    o_mem = jnp.concatenate(om, axis=-1).astype(BF16)
    z = _dot(ys_ref[...], wglu_ref[...])
    y_ssm = (z[:, 0:SSM_WIDTH] * jax.nn.sigmoid(z[:, SSM_WIDTH:2 * SSM_WIDTH])).astype(BF16)
    g = lambda c: gate_ref[:, c * D_MODEL:(c + 1) * D_MODEL].astype(F32)
    merged = (g(0) * _dot(of_ref[...], wbf_ref[...]) + g(1) * _dot(y_ssm, wbs_ref[...])
              + g(2) * _dot(o_mem, wbm_ref[...]))
    x1 = x_ref[...] + _dot(merged.astype(BF16), wo_ref[...])
    x1_ref[...] = x1
    h2 = x1 * lax.rsqrt(jnp.mean(x1 * x1, axis=-1, keepdims=True) + RMS_EPS) * nf_ref[...]
    if sparse:
        _store_slabs(h2_ref, h2, 0, SLAB_ROWS)
    else:
        h2_ref[...] = h2.astype(BF16)
    logits = _dot_exact(h2, wr_ref[...])
    lane = lax.broadcasted_iota(jnp.int32, (tm, LANES), 1)
    big = jnp.int32(LANES)
    is_grp = (lane >= N_EXPERTS) & (lane < N_EXPERTS + N_EXPERT_GROUPS)
    gl = jnp.where(is_grp, logits, NEG_INF)
    gmax = jnp.max(gl, axis=-1, keepdims=True)
    grp = jnp.min(jnp.where(is_grp & (gl == gmax), lane, big), axis=-1, keepdims=True) - N_EXPERTS
    g_w = 1.0 / jnp.sum(jnp.where(is_grp, jnp.exp(gl - gmax), 0.0), axis=-1, keepdims=True)
    in_grp = (lane >= grp * EXPERTS_PER_GROUP) & (lane < (grp + 1) * EXPERTS_PER_GROUP)
    e1 = jnp.where(in_grp, logits, NEG_INF)
    m1 = jnp.max(e1, axis=-1, keepdims=True)
    i1 = jnp.min(jnp.where(in_grp & (e1 == m1), lane, big), axis=-1, keepdims=True)
    rest = in_grp & (lane != i1)
    e2 = jnp.where(rest, logits, NEG_INF)
    m2 = jnp.max(e2, axis=-1, keepdims=True)
    i2 = jnp.min(jnp.where(rest & (e2 == m2), lane, big), axis=-1, keepdims=True)
    ex = jnp.exp(m2 - m1)
    w1 = g_w / (1.0 + ex)
    w2 = g_w * ex / (1.0 + ex)
    if not sparse:
        r_ref[...] = jnp.where(lane == i1, w1, jnp.where(lane == i2, w2, 0.0))
        return
    cnt_ref, carry_ref = sparse_refs

    @pl.when(pl.program_id(0) == 0)
    def _():
        carry_ref[...] = jnp.zeros(carry_ref.shape, F32)

    first = i1 < i2
    lo = jnp.where(first, i1, i2) - grp * EXPERTS_PER_GROUP
    hi = jnp.where(first, i2, i1) - grp * EXPERTS_PER_GROUP
    bucket = grp * PAIRS_PER_GROUP + ((lo * (2 * EXPERTS_PER_GROUP - 1 - lo)) >> 1) + (hi - lo - 1)
    onehot = lane == bucket
    ind = jnp.where(onehot, 1.0, 0.0)
    r = lax.broadcasted_iota(jnp.int32, (tm, tm), 0)
    c = lax.broadcasted_iota(jnp.int32, (tm, tm), 1)
    before = _dot(jnp.where(c < r, 1.0, 0.0).astype(BF16), ind.astype(BF16)) + carry_ref[...]
    rank = jnp.sum(jnp.where(onehot, before, 0.0), axis=-1, keepdims=True)
    total = carry_ref[...] + jnp.sum(ind, axis=0, keepdims=True)
    carry_ref[...] = total
    cnt_ref[...] = jnp.broadcast_to(total, cnt_ref.shape)
    r_ref[...] = jnp.where(lane == 0, bucket.astype(F32), jnp.where(lane == 1, rank, jnp.where(
        lane == 2, jnp.where(first, w1, w2), jnp.where(lane == 3, jnp.where(first, w2, w1), 0.0))))


def _merge(x2d, o_fox, ys, q_m, gates, mem_k, mem_v, w, tm, rows_per_batch, sparse):
    n = x2d.shape[0]
    assert n % tm == 0 and rows_per_batch % tm == 0
    per = rows_per_batch // tm
    row = lambda width: pl.BlockSpec((tm, width), lambda i: (i, 0))
    memspec = pl.BlockSpec((1, N_MEM, MEM_WIDTH), lambda i: (i // per, 0, 0))
    ws = [w['w_glu'], w['w_br_fox'], w['w_br_ssm'], w['w_br_mem'], w['w_out'], w['norm_ffn'], w['w_router']]
    out_shape = [jax.ShapeDtypeStruct((n, D_MODEL), F32)]
    out_specs = [row(D_MODEL)]
    if sparse:
        out_shape += [jax.ShapeDtypeStruct((n * SLAB_ROWS, LANES), F32), jax.ShapeDtypeStruct((n, LANES), F32),
                      jax.ShapeDtypeStruct((8, LANES), F32)]
        out_specs += [pl.BlockSpec((tm * SLAB_ROWS, LANES), lambda i: (i, 0)), row(LANES), _full((8, LANES))]
        scratch = [pltpu.VMEM((1, LANES), F32)]
    else:
        out_shape += [jax.ShapeDtypeStruct((n, D_MODEL), BF16), jax.ShapeDtypeStruct((n, LANES), F32)]
        out_specs += [row(D_MODEL), row(LANES)]
        scratch = []
    return pl.pallas_call(
        functools.partial(_merge_kernel, sparse=sparse),
        out_shape=tuple(out_shape),
        grid=(n // tm,),
        in_specs=[row(D_MODEL), row(FOX_WIDTH), row(SSM_WIDTH), row(MEM_WIDTH), row(3 * D_MODEL), memspec, memspec]
                 + [_full(a.shape) for a in ws],
        out_specs=tuple(out_specs), scratch_shapes=scratch,
        compiler_params=_params(("arbitrary" if sparse else "parallel",)), name="merge")(
            x2d, o_fox, ys, q_m, gates, mem_k, mem_v, *ws)


def _moe_kernel(h_ref, comb_ref, x1_ref, wg_ref, wu_ref, wd_ref, o_ref, acc_ref):
    e = pl.program_id(1)

    @pl.when(e == 0)
    def _():
        acc_ref[...] = jnp.zeros(acc_ref.shape, F32)

    h = h_ref[...]
    a = _dot(h, wg_ref[0].astype(BF16))
    up = _dot(h, wu_ref[0].astype(BF16))
    lane = lax.broadcasted_iota(jnp.int32, comb_ref.shape, 1)
    ce = jnp.sum(jnp.where(lane == e, comb_ref[...], 0.0), axis=-1, keepdims=True)
    act = a * jax.nn.sigmoid(a) * up * ce
    acc_ref[...] += _dot(act.astype(BF16), wd_ref[0].astype(BF16))

    @pl.when(e == pl.num_programs(1) - 1)
    def _():
        o_ref[...] = x1_ref[...] + acc_ref[...]


def _moe(h2, comb, x1, wg, wu, wd, tm):
    n = h2.shape[0]
    assert n % tm == 0
    row = lambda width: pl.BlockSpec((tm, width), lambda i, e: (i, 0))
    return pl.pallas_call(
        _moe_kernel,
        out_shape=jax.ShapeDtypeStruct((n, D_MODEL), F32),
        grid=(n // tm, N_EXPERTS),
        in_specs=[row(D_MODEL), row(LANES), row(D_MODEL),
                  pl.BlockSpec((1, D_MODEL, D_EXPERT), lambda i, e: (e, 0, 0)),
                  pl.BlockSpec((1, D_MODEL, D_EXPERT), lambda i, e: (e, 0, 0)),
                  pl.BlockSpec((1, D_EXPERT, D_MODEL), lambda i, e: (e, 0, 0))],
        out_specs=row(D_MODEL),
        scratch_shapes=[pltpu.VMEM((tm, D_MODEL), F32)],
        compiler_params=_params(("parallel", "arbitrary")), name="moe")(h2, comb, x1, wg, wu, wd)


def _store_slabs(ref, x, first_row, rows_per_token):
    m = x.shape[0]
    for s in range(SLAB_ROWS):
        ref[pl.ds(first_row + s, m, stride=rows_per_token), :] = x[:, s * LANES:(s + 1) * LANES]


def _row_copies(idx_ref, base, count, src_hbm, dst_ref, slot, sem, rows, start):
    def body(r, carry):
        src_row = pl.multiple_of(idx_ref[base + r] * rows, rows)
        dst_row = pl.multiple_of(r * rows, rows)
        copy = pltpu.make_async_copy(src_hbm.at[pl.ds(src_row, rows)], dst_ref.at[slot, pl.ds(dst_row, rows)],
                                     sem.at[slot])
        if start:
            copy.start()
        else:
            copy.wait()
        return carry

    lax.fori_loop(0, count, body, 0)


def _moe_sparse_kernel(ea_ref, eb_ref, nt_ref, src_ref, h2_hbm, wga_ref, wua_ref, wda_ref, wgb_ref, wub_ref, wdb_ref,
                       os_ref, gbuf_ref, sem):
    i = pl.program_id(0)
    nt = nt_ref[0]
    tmr = MOE_TILE

    @pl.when((i == 0) & (nt > 0))
    def _():
        _row_copies(src_ref, 0, tmr, h2_hbm, gbuf_ref, 0, sem, SLAB_ROWS, True)

    @pl.when(i + 1 < nt)
    def _():
        _row_copies(src_ref, (i + 1) * tmr, tmr, h2_hbm, gbuf_ref, (i + 1) % 2, sem, SLAB_ROWS, True)

    @pl.when(i < nt)
    def _():
        slot = i % 2
        _row_copies(src_ref, i * tmr, tmr, h2_hbm, gbuf_ref, slot, sem, SLAB_ROWS, False)
        x = jnp.concatenate([gbuf_ref[slot, pl.ds(s, tmr, stride=SLAB_ROWS), :] for s in range(SLAB_ROWS)],
                            axis=1).astype(BF16)
        for k, (wg, wu, wd) in enumerate(((wga_ref, wua_ref, wda_ref), (wgb_ref, wub_ref, wdb_ref))):
            a = _dot(x, wg[0].astype(BF16))
            up = _dot(x, wu[0].astype(BF16))
            act = (a * jax.nn.sigmoid(a) * up).astype(BF16)
            _store_slabs(os_ref, _dot(act, wd[0].astype(BF16)), k * SLAB_ROWS, 2 * SLAB_ROWS)

    @pl.when(i >= nt)
    def _():
        os_ref[...] = jnp.zeros(os_ref.shape, F32)


def _moe_sparse(h2_slabs, plan, wg, wu, wd):
    ntile_max = plan['ea'].shape[0]
    wspec = lambda shape, which: pl.BlockSpec(shape, lambda i, ea, eb, nt, src: ((ea, eb)[which][i], 0, 0))
    up_shape, down_shape = (1, D_MODEL, D_EXPERT), (1, D_EXPERT, D_MODEL)
    grid_spec = pltpu.PrefetchScalarGridSpec(
        num_scalar_prefetch=4, grid=(ntile_max,),
        in_specs=[pl.BlockSpec(memory_space=pl.ANY),
                  wspec(up_shape, 0), wspec(up_shape, 0), wspec(down_shape, 0),
                  wspec(up_shape, 1), wspec(up_shape, 1), wspec(down_shape, 1)],
        out_specs=pl.BlockSpec((MOE_TILE * 2 * SLAB_ROWS, LANES), lambda i, ea, eb, nt, src: (i, 0)),
        scratch_shapes=[pltpu.VMEM((2, MOE_TILE * SLAB_ROWS, LANES), F32), pltpu.SemaphoreType.DMA((2,))])
    return pl.pallas_call(
        _moe_sparse_kernel, grid_spec=grid_spec,
        out_shape=jax.ShapeDtypeStruct((ntile_max * MOE_TILE * 2 * SLAB_ROWS, LANES), F32),
        compiler_params=_params(("arbitrary",)), name="moe_sparse")(
            plan['ea'], plan['eb'], plan['ntiles'], plan['src'], h2_slabs, wg, wu, wd, wg, wu, wd)


def _moe_combine_kernel(pos_ref, x1_ref, r_ref, os_hbm, y_ref, cbuf_ref, sem):
    i = pl.program_id(0)
    n = pl.num_programs(0)
    tm = x1_ref.shape[0]
    rows = 2 * SLAB_ROWS

    @pl.when(i == 0)
    def _():
        _row_copies(pos_ref, 0, tm, os_hbm, cbuf_ref, 0, sem, rows, True)

    @pl.when(i + 1 < n)
    def _():
        _row_copies(pos_ref, (i + 1) * tm, tm, os_hbm, cbuf_ref, (i + 1) % 2, sem, rows, True)

    slot = i % 2
    _row_copies(pos_ref, i * tm, tm, os_hbm, cbuf_ref, slot, sem, rows, False)
    y = x1_ref[...]
    for k in range(2):
        o = jnp.concatenate([cbuf_ref[slot, pl.ds(k * SLAB_ROWS + s, tm, stride=rows), :] for s in range(SLAB_ROWS)],
                            axis=1)
        y = y + r_ref[:, 2 + k:3 + k] * o
    y_ref[...] = y


def _moe_combine(x1, route, os_slabs, pos, tm):
    n = x1.shape[0]
    assert n % tm == 0
    grid_spec = pltpu.PrefetchScalarGridSpec(
        num_scalar_prefetch=1, grid=(n // tm,),
        in_specs=[pl.BlockSpec((tm, D_MODEL), lambda i, pos: (i, 0)), pl.BlockSpec((tm, LANES), lambda i, pos: (i, 0)),
                  pl.BlockSpec(memory_space=pl.ANY)],
        out_specs=pl.BlockSpec((tm, D_MODEL), lambda i, pos: (i, 0)),
        scratch_shapes=[pltpu.VMEM((2, tm * 2 * SLAB_ROWS, LANES), F32), pltpu.SemaphoreType.DMA((2,))])
    return pl.pallas_call(
        _moe_combine_kernel, grid_spec=grid_spec, out_shape=jax.ShapeDtypeStruct((n, D_MODEL), F32),
        compiler_params=_params(("arbitrary",)), name="moe_combine")(pos, x1, route, os_slabs)


def _bucket_experts():
    ea, eb = [], []
    for g in range(N_EXPERT_GROUPS):
        for lo in range(EXPERTS_PER_GROUP):
            for hi in range(lo + 1, EXPERTS_PER_GROUP):
                ea.append(g * EXPERTS_PER_GROUP + lo)
                eb.append(g * EXPERTS_PER_GROUP + hi)
    return jnp.asarray(ea, jnp.int32), jnp.asarray(eb, jnp.int32)


def _moe_plan(route, cnt, n):
    ntile_max = n // MOE_TILE + N_BUCKETS
    counts = cnt[0, 0:N_BUCKETS].astype(jnp.int32)
    tiles = (counts + MOE_TILE - 1) // MOE_TILE
    ends = jnp.cumsum(tiles)
    starts = ends - tiles
    bucket = route[:, 0].astype(jnp.int32)
    rank = route[:, 1].astype(jnp.int32)
    pos = starts[bucket] * MOE_TILE + rank
    src = jnp.zeros((ntile_max * MOE_TILE,), jnp.int32).at[pos].set(jnp.arange(n, dtype=jnp.int32))
    tile_bucket = jnp.minimum(jnp.searchsorted(ends, jnp.arange(ntile_max, dtype=jnp.int32), side='right'),
                              N_BUCKETS - 1).astype(jnp.int32)
    ea, eb = _bucket_experts()
    return dict(pos=pos, src=src, ea=ea[tile_bucket], eb=eb[tile_bucket], ntiles=ends[-1:].astype(jnp.int32))


def _prep_weights(p):
    w_in = p['w_in'].astype(BF16)
    o = 0
    wqkv = w_in[:, 0:3 * FOX_WIDTH]
    o = 3 * FOX_WIDTH
    wf = jnp.pad(w_in[:, o:o + N_FOX_HEADS], ((0, 0), (0, LANES - N_FOX_HEADS)))
    o += N_FOX_HEADS
    wqm = w_in[:, o:o + MEM_WIDTH]
    o += MEM_WIDTH
    wu = w_in[:, o:o + SSM_WIDTH]
    o += SSM_WIDTH
    wg = w_in[:, o:o + 3 * D_MODEL]
    r = jnp.arange(FOX_WIDTH) // FOX_HEAD_DIM
    bd = (r[:, None] == r[None, :]).astype(BF16)
    w_router = jnp.concatenate(
        [p['w_router_expert'], p['w_router_group'],
         jnp.zeros((D_MODEL, LANES - N_EXPERTS - N_EXPERT_GROUPS), F32)], axis=1)
    return dict(
        norm_mix=p['norm_mix'].reshape(1, D_MODEL), wqkv=wqkv, wf=wf,
        bf=jnp.pad(p['b_forget'], (0, LANES - N_FOX_HEADS)).reshape(1, LANES),
        wqm=wqm, wu=wu, wg=wg,
        qn_fox=jnp.tile(p['qn_fox'], N_FOX_HEADS).reshape(1, FOX_WIDTH),
        kn_fox=jnp.tile(p['kn_fox'], N_FOX_HEADS).reshape(1, FOX_WIDTH),
        qn_mem=p['qn_mem'].reshape(1, MEM_HEAD_DIM), bd=bd,
        w_glu=p['w_glu'].astype(BF16), w_br_fox=p['w_br_fox'].astype(BF16),
        w_br_ssm=p['w_br_ssm'].astype(BF16), w_br_mem=p['w_br_mem'].astype(BF16),
        w_out=p['w_out'].astype(BF16), norm_ffn=p['norm_ffn'].reshape(1, D_MODEL), w_router=w_router,
        moe_wg=p['moe_w_gate'], moe_wu=p['moe_w_up'], moe_wd=p['moe_w_down'])


def _pick_tile(n, target):
    t = min(n, target)
    while n % t:
        t //= 2
    return t


def _group(x, w, mats, mem_k, mem_v, h0_re, h0_im, cache):
    b, s, _ = x.shape
    n = b * s
    x2d = x.reshape(n, D_MODEL)
    prompt = cache is None
    q, kb, vb, k_out, v_out, lf_t, q_m, u, gates = _inproj(x2d, w, _pick_tile(s if prompt else n, 512), s, prompt)
    q3 = q.reshape(b, s, FOX_WIDTH)
    k3 = kb.reshape(b, s, FOX_WIDTH)
    v3 = vb.reshape(b, s, FOX_WIDTH)
    lf_rows = lf_t.reshape(N_FOX_HEADS, b, s).transpose(1, 0, 2)
    lf3 = lf_rows.transpose(0, 2, 1)
    npair = N_FOX_HEADS // 2
    if prompt:
        c_row = _cumsum_rows(lf_rows.reshape(b * N_FOX_HEADS, s)).reshape(b, npair, 2, s)
        c_col = c_row.transpose(0, 1, 3, 2)
        o_fox = _fox_prompt(q3, k3, v3, c_col, c_row, _pick_tile(s, 256), _pick_tile(s, 512))
        unt = lambda a: a.reshape(b, N_FOX_HEADS, FOX_HEAD_DIM, s).transpose(0, 3, 1, 2)
        k4, v4 = unt(k_out), unt(v_out)
    else:
        cache_k, cache_v, cache_logf = cache
        past = cache_k.shape[1]
        lf_all = jnp.concatenate([cache_logf.astype(F32).transpose(0, 2, 1), lf_rows], axis=2)
        c_row = _cumsum_rows(lf_all.reshape(b * N_FOX_HEADS, past + s)).reshape(b, N_FOX_HEADS, past + s)
        o_fox = _fox_sample(q3, cache_k.transpose(0, 2, 3, 1), cache_v.transpose(0, 2, 3, 1), k3, v3,
                            c_row[:, :, past:].transpose(0, 2, 1), c_row[:, :, :past], c_row[:, :, past:],
                            _pick_tile(past, 1024))
        k4 = k_out.reshape(b, s, N_FOX_HEADS, FOX_HEAD_DIM)
        v4 = v_out.reshape(b, s, N_FOX_HEADS, FOX_HEAD_DIM)
    ys, f_re, f_im = _ssm(u, b, h0_re, h0_im, mats)
    tm = _pick_tile(s, 512)
    sparse = n >= SPARSE_MIN_TOKENS and n % MOE_TILE == 0
    merged = _merge(x2d, o_fox.reshape(n, FOX_WIDTH), ys, q_m, gates, mem_k, mem_v, w, tm, s, sparse)
    if sparse:
        x1, h2_slabs, route, cnt = merged
        plan = _moe_plan(route, cnt, n)
        os_slabs = _moe_sparse(h2_slabs, plan, w['moe_wg'], w['moe_wu'], w['moe_wd'])
        y = _moe_combine(x1, route, os_slabs, plan['pos'], _pick_tile(n, 256))
    else:
        x1, h2, comb = merged
        y = _moe(h2, comb, x1, w['moe_wg'], w['moe_wu'], w['moe_wd'], _pick_tile(n, 1024))
    return y.reshape(b, s, D_MODEL), k4, v4, lf3, f_re, f_im


def kernel(x_prompt, x_sample, mem_prompt, cache_fox_k, cache_fox_v, cache_fox_logf, state_ssm_re, state_ssm_im,
           cache_mem_k, cache_mem_v, norm_mix, w_in, b_forget, qn_fox, kn_fox, qn_mem, kn_mem, norm_mem, w_mem_kv,
           ssm_a_re, ssm_a_im, ssm_log_dt, ssm_b_re, ssm_b_im, ssm_c_re, ssm_c_im, ssm_d, w_glu, w_br_fox,
           w_br_ssm, w_br_mem, w_out, norm_ffn, w_router_group, w_router_expert, moe_w_gate, moe_w_up,
           moe_w_down):
    depth = norm_mix.shape[0]
    assert depth == 1
    l = 0
    p = dict(norm_mix=norm_mix[l], w_in=w_in[l], b_forget=b_forget[l], qn_fox=qn_fox[l], kn_fox=kn_fox[l],
             qn_mem=qn_mem[l], ssm_a_re=ssm_a_re[l], ssm_a_im=ssm_a_im[l], ssm_log_dt=ssm_log_dt[l],
             ssm_b_re=ssm_b_re[l], ssm_b_im=ssm_b_im[l], ssm_c_re=ssm_c_re[l], ssm_c_im=ssm_c_im[l],
             ssm_d=ssm_d[l], w_glu=w_glu[l], w_br_fox=w_br_fox[l], w_br_ssm=w_br_ssm[l], w_br_mem=w_br_mem[l],
             w_out=w_out[l], norm_ffn=norm_ffn[l], w_router_group=w_router_group[l],
             w_router_expert=w_router_expert[l], moe_w_gate=moe_w_gate[l], moe_w_up=moe_w_up[l],
             moe_w_down=moe_w_down[l])
    w = _prep_weights(p)
    mats = _ssm_mats(p)
    bp, sp, _ = x_prompt.shape
    bs, ss, _ = x_sample.shape

    mk, mv = _memkv(mem_prompt.reshape(bp * N_MEM, D_MODEL), norm_mem[l].reshape(1, D_MODEL),
                    w_mem_kv[l].astype(BF16), kn_mem[l].reshape(1, MEM_HEAD_DIM), _pick_tile(bp * N_MEM, 512))
    mk = mk.reshape(bp, N_MEM, MEM_WIDTH)
    mv = mv.reshape(bp, N_MEM, MEM_WIDTH)
    zeros = jnp.zeros((bp, N_SSM_GROUPS, SSM_STATE), F32)
    yp, pk, pv, plf, pre, pim = _group(x_prompt, w, mats, mk, mv, zeros, zeros, None)
    cache = (cache_fox_k[l], cache_fox_v[l], cache_fox_logf[l])
    ys, sk, sv, slf, sre, sim = _group(
        x_sample, w, mats, cache_mem_k[l].reshape(bs, N_MEM, MEM_WIDTH), cache_mem_v[l].reshape(bs, N_MEM, MEM_WIDTH),
        state_ssm_re[l].astype(F32), state_ssm_im[l].astype(F32), cache)
    st = lambda a: a[None]
    return (yp, ys, st(pk), st(pv), st(plf), st(pre), st(pim),
            st(mk.reshape(bp, N_MEM, N_MEM_HEADS, MEM_HEAD_DIM)), st(mv.reshape(bp, N_MEM, N_MEM_HEADS, MEM_HEAD_DIM)),
            st(sk), st(sv), st(slf), st(sre), st(sim))
```

```python
import functools
import math

import jax
import jax.numpy as jnp
from jax import lax
from jax.experimental import pallas as pl
from jax.experimental.pallas import tpu as pltpu

F32 = jnp.float32
BF16 = jnp.bfloat16

D_MODEL = 1024
N_FOX_HEADS = 8
FOX_HEAD_DIM = 64
FOX_WIDTH = N_FOX_HEADS * FOX_HEAD_DIM
N_MEM = 256
N_MEM_HEADS = 4
MEM_HEAD_DIM = 128
MEM_WIDTH = N_MEM_HEADS * MEM_HEAD_DIM
SSM_GROUP = 16
SSM_WIDTH = 512
N_SSM_GROUPS = SSM_WIDTH // SSM_GROUP
SSM_STATE = 64
N_EXPERT_GROUPS = 4
EXPERTS_PER_GROUP = 8
N_EXPERTS = N_EXPERT_GROUPS * EXPERTS_PER_GROUP
D_EXPERT = 256
RMS_EPS = 1e-6
NEG_INF = -1e30

LANES = 128
SSM_CHUNK = 16
SSM_GPB = LANES // SSM_GROUP
GROUP_LANE = N_EXPERTS
MOE_SUB = 128
VMEM_LIMIT = 56 * 1024 * 1024


def _dot(a, b):
    return jnp.dot(a, b, preferred_element_type=F32)


def _dot_nt(a, b):
    return lax.dot_general(a, b, (((1,), (1,)), ((), ())), preferred_element_type=F32)


def _dot_exact(a, b):
    return jnp.dot(a, b, preferred_element_type=F32, precision=lax.Precision.HIGHEST)


def _split_bf16(x):
    hi = x.astype(BF16)
    lo = (x - hi.astype(F32)).astype(BF16)
    return hi, lo


def _params(sem):
    return pltpu.CompilerParams(dimension_semantics=sem, vmem_limit_bytes=VMEM_LIMIT)


def _full(shape):
    n = len(shape)
    return pl.BlockSpec(shape, lambda *_: (0,) * n)


def _inproj_kernel(x_ref, g_ref, wqkv_ref, wf_ref, bf_ref, wqm_ref, wu_ref, wg_ref,
                   qn_ref, kn_ref, qmn_ref, bd_ref,
                   q_ref, kb_ref, vb_ref, k_ref, v_ref, lf_ref, qm_ref, u_ref, gate_ref, *, kv_transposed):
    x = x_ref[...]
    h = x * lax.rsqrt(jnp.mean(x * x, axis=-1, keepdims=True) + RMS_EPS) * g_ref[...]
    hb = h.astype(BF16)

    def head_norm(z, gain):
        hi, lo = _split_bf16(z * z)
        ss = _dot(hi, bd_ref[...]) + _dot(lo, bd_ref[...])
        return z * lax.rsqrt(ss * (1.0 / FOX_HEAD_DIM) + RMS_EPS) * gain

    zq = _dot(hb, wqkv_ref[:, 0:FOX_WIDTH])
    q_ref[...] = (head_norm(zq, qn_ref[...]) * (FOX_HEAD_DIM ** -0.5)).astype(BF16)
    zk = _dot(hb, wqkv_ref[:, FOX_WIDTH:2 * FOX_WIDTH])
    kn = head_norm(zk, kn_ref[...])
    zv = _dot(hb, wqkv_ref[:, 2 * FOX_WIDTH:3 * FOX_WIDTH])
    kb_ref[...] = kn.astype(BF16)
    vb_ref[...] = zv.astype(BF16)
    tm = x_ref.shape[0]
    if kv_transposed:
        k_ref[0] = kn.T
        v_ref[0] = zv.T
    else:
        for hd in range(N_FOX_HEADS):
            hs = slice(hd * FOX_HEAD_DIM, (hd + 1) * FOX_HEAD_DIM)
            rows = pl.ds(hd, tm, stride=N_FOX_HEADS)
            k_ref[rows, :] = kn[:, hs]
            v_ref[rows, :] = zv[:, hs]

    zf = (_dot(hb, wf_ref[...]) + bf_ref[...]).T[0:N_FOX_HEADS, :]
    lf_ref[...] = jnp.minimum(zf, 0.0) - jnp.log1p(jnp.exp(-jnp.abs(zf)))

    zm = _dot(hb, wqm_ref[...])
    for hd in range(N_MEM_HEADS):
        sl = slice(hd * MEM_HEAD_DIM, (hd + 1) * MEM_HEAD_DIM)
        zh = zm[:, sl]
        ms = jnp.mean(zh * zh, axis=-1, keepdims=True)
        qm_ref[:, sl] = (zh * lax.rsqrt(ms + RMS_EPS) * qmn_ref[...] * (MEM_HEAD_DIM ** -0.5)).astype(BF16)

    u_ref[...] = _dot(hb, wu_ref[...])
    for c in range(3):
        sl = slice(c * D_MODEL, (c + 1) * D_MODEL)
        gate_ref[:, sl] = jax.nn.sigmoid(_dot(hb, wg_ref[:, sl])).astype(BF16)


def _inproj(x2d, w, tm, seq, kv_transposed):
    n = x2d.shape[0]
    assert n % tm == 0
    row = lambda width: pl.BlockSpec((tm, width), lambda i: (i, 0))
    if kv_transposed:
        assert seq % tm == 0
        per = seq // tm
        kv_shape = jax.ShapeDtypeStruct((n // seq, FOX_WIDTH, seq), F32)
        heads = pl.BlockSpec((1, FOX_WIDTH, tm), lambda i: (i // per, 0, i % per))
    else:
        kv_shape = jax.ShapeDtypeStruct((n * N_FOX_HEADS, FOX_HEAD_DIM), F32)
        heads = pl.BlockSpec((tm * N_FOX_HEADS, FOX_HEAD_DIM), lambda i: (i, 0))
    ins = [x2d, w['norm_mix'], w['wqkv'], w['wf'], w['bf'], w['wqm'], w['wu'], w['wg'],
           w['qn_fox'], w['kn_fox'], w['qn_mem'], w['bd']]
    in_specs = [row(D_MODEL)] + [_full(a.shape) for a in ins[1:]]
    out_shape = (
        jax.ShapeDtypeStruct((n, FOX_WIDTH), BF16),
        jax.ShapeDtypeStruct((n, FOX_WIDTH), BF16),
        jax.ShapeDtypeStruct((n, FOX_WIDTH), BF16),
        kv_shape,
        kv_shape,
        jax.ShapeDtypeStruct((N_FOX_HEADS, n), F32),
        jax.ShapeDtypeStruct((n, MEM_WIDTH), BF16),
        jax.ShapeDtypeStruct((n, SSM_WIDTH), F32),
        jax.ShapeDtypeStruct((n, 3 * D_MODEL), BF16),
    )
    out_specs = (row(FOX_WIDTH), row(FOX_WIDTH), row(FOX_WIDTH), heads, heads,
                 pl.BlockSpec((N_FOX_HEADS, tm), lambda i: (0, i)),
                 row(MEM_WIDTH), row(SSM_WIDTH), row(3 * D_MODEL))
    return pl.pallas_call(
        functools.partial(_inproj_kernel, kv_transposed=kv_transposed),
        out_shape=out_shape, grid=(n // tm,), in_specs=in_specs, out_specs=out_specs,
        compiler_params=_params(("parallel",)), name="inproj")(*ins)


def _memkv_kernel(x_ref, g_ref, w_ref, kn_ref, k_ref, v_ref):
    x = x_ref[...]
    h = x * lax.rsqrt(jnp.mean(x * x, axis=-1, keepdims=True) + RMS_EPS) * g_ref[...]
    hb = h.astype(BF16)
    zk = _dot(hb, w_ref[:, 0:MEM_WIDTH])
    for hd in range(N_MEM_HEADS):
        sl = slice(hd * MEM_HEAD_DIM, (hd + 1) * MEM_HEAD_DIM)
        zh = zk[:, sl]
        ms = jnp.mean(zh * zh, axis=-1, keepdims=True)
        k_ref[:, sl] = zh * lax.rsqrt(ms + RMS_EPS) * kn_ref[...]
    v_ref[...] = _dot(hb, w_ref[:, MEM_WIDTH:2 * MEM_WIDTH])


def _memkv(mem2d, norm_mem, w_kv, kn_mem, tm):
    n = mem2d.shape[0]
    row = lambda width: pl.BlockSpec((tm, width), lambda i: (i, 0))
    return pl.pallas_call(
        _memkv_kernel,
        out_shape=(jax.ShapeDtypeStruct((n, MEM_WIDTH), F32), jax.ShapeDtypeStruct((n, MEM_WIDTH), F32)),
        grid=(n // tm,),
        in_specs=[row(D_MODEL), _full(norm_mem.shape), _full(w_kv.shape), _full(kn_mem.shape)],
        out_specs=(row(MEM_WIDTH), row(MEM_WIDTH)),
        compiler_params=_params(("parallel",)), name="memkv")(mem2d, norm_mem, w_kv, kn_mem)


CUMSUM_BLOCK = 256


def _cumsum_kernel(x_ref, o_ref):
    nblk = x_ref.shape[1] // CUMSUM_BLOCK
    r = lax.broadcasted_iota(jnp.int32, (CUMSUM_BLOCK, CUMSUM_BLOCK), 0)
    c = lax.broadcasted_iota(jnp.int32, (CUMSUM_BLOCK, CUMSUM_BLOCK), 1)
    tri = (r <= c).astype(F32)
    carry = jnp.zeros((x_ref.shape[0], 1), F32)
    for j in range(nblk):
        sl = slice(j * CUMSUM_BLOCK, (j + 1) * CUMSUM_BLOCK)
        cs = _dot_exact(x_ref[:, sl], tri) + carry
        o_ref[:, sl] = cs
        carry = cs[:, CUMSUM_BLOCK - 1:CUMSUM_BLOCK]


def _cumsum_rows(x):
    rows, n = x.shape
    npad = -(-n // CUMSUM_BLOCK) * CUMSUM_BLOCK
    xp = jnp.pad(x, ((0, 0), (0, npad - n))) if npad != n else x
    out = pl.pallas_call(
        _cumsum_kernel, out_shape=jax.ShapeDtypeStruct((rows, npad), F32), grid=(1,),
        in_specs=[_full((rows, npad))], out_specs=_full((rows, npad)),
        compiler_params=_params(("arbitrary",)), name="cumsum")(xp)
    return out[:, :n] if npad != n else out


def _reduce_rows(x, op):
    rows, cols = x.shape
    if rows > 64 and rows % 64 == 0:
        x = op(x.reshape(rows // 64, 64, cols), axis=0)
        rows = 64
    if rows == 64:
        x = op(x.reshape(8, 8, cols), axis=0)
    return op(x, axis=0, keepdims=True)


def _head_lane_masks(rows):
    lane = lax.broadcasted_iota(jnp.int32, (rows, LANES), 1)
    return lane < FOX_HEAD_DIM


def _softmax_step(qh, kb, vb, ck, cq, m, l, acc, mask):
    t = _dot_nt(qh, kb) - ck
    if mask is not None:
        t = jnp.where(mask, t, NEG_INF)
    m_new = jnp.maximum(m, jnp.max(t, axis=-1, keepdims=True) + cq)
    alpha = jnp.exp(m - m_new)
    p = jnp.exp(t + (cq - m_new))
    l_new = alpha * l + jnp.sum(p, axis=-1, keepdims=True)
    acc_new = alpha * acc + _dot(p.astype(BF16), vb)
    return m_new, l_new, acc_new


def _fox_prompt_kernel(q_ref, k_ref, v_ref, cc_ref, cr_ref, o_ref,
                       vt_ref, ck0_ref, ck1_ref, st_ref, pt_ref, acc_ref, *, tq, tk):
    i = pl.program_id(2)
    s_len = k_ref.shape[1]

    @pl.when(i == 0)
    def _():
        vt_ref[...] = v_ref[0].astype(F32).T.astype(BF16)
        ck0_ref[...] = jnp.broadcast_to(cc_ref[0, 0, :, 0:1], (s_len, LANES))
        ck1_ref[...] = jnp.broadcast_to(cc_ref[0, 0, :, 1:2], (s_len, LANES))

    qt = q_ref[0].astype(F32).T
    row = lax.broadcasted_iota(jnp.int32, (LANES, tq), 0)
    qts = (jnp.where(row < FOX_HEAD_DIM, qt, 0.0).astype(BF16), jnp.where(row < FOX_HEAD_DIM, 0.0, qt).astype(BF16))
    q0 = pl.multiple_of(i * tq, tq)
    cq = cr_ref[0, 0, :, pl.ds(q0, tq)]
    ck_refs = (ck0_ref, ck1_ref)

    def stage_a(n):
        s = pl.multiple_of(n * tk, tk)
        kb = k_ref[0, pl.ds(s, tk), :]
        for hh in range(2):
            ck = ck_refs[hh][pl.ds(s, tk), :]
            st_ref[n & 1, hh] = _dot(kb, qts[hh]) - jnp.concatenate([ck] * (tq // LANES), axis=1)

    def stage_b(n, stats, masked):
        if masked:
            kpos = n * tk + lax.broadcasted_iota(jnp.int32, (tk, tq), 0)
            qpos = q0 + lax.broadcasted_iota(jnp.int32, (tk, tq), 1)
            mask = kpos <= qpos
        out = []
        for hh in range(2):
            m, l = stats[2 * hh:2 * hh + 2]
            t = st_ref[n & 1, hh]
            if masked:
                t = jnp.where(mask, t, NEG_INF)
            cqh = cq[hh:hh + 1, :]
            m_new = jnp.maximum(m, _reduce_rows(t, jnp.max) + cqh)
            alpha = jnp.exp(m - m_new)
            p = jnp.exp(t + (cqh - m_new))
            pt_ref[n & 1, hh] = p.astype(BF16)
            out.extend([m_new, alpha * l + _reduce_rows(p, jnp.sum), alpha])
        return out

    def stage_c(n, alphas):
        s = pl.multiple_of(jnp.maximum(n, 0) * tk, tk)
        for hh in range(2):
            vt = vt_ref[hh * FOX_HEAD_DIM:(hh + 1) * FOX_HEAD_DIM, pl.ds(s, tk)]
            acc_ref[hh] = alphas[hh] * acc_ref[hh] + _dot(vt, pt_ref[n & 1, hh])

    acc_ref[...] = jnp.zeros(acc_ref.shape, F32)
    pt_ref[1] = jnp.zeros(pt_ref.shape[1:], BF16)
    neg = jnp.full((1, tq), NEG_INF, F32)
    zero = jnp.zeros((1, tq), F32)
    one = jnp.ones((1, tq), F32)
    nfull = (i * tq) // tk
    stage_a(0)

    def body(n, carry):
        m0, l0, al0, m1, l1, al1 = carry
        stage_c(n - 1, (al0, al1))
        new = stage_b(n, (m0, l0, m1, l1), False)
        stage_a(n + 1)
        return tuple(new)

    m0, l0, al0, m1, l1, al1 = lax.fori_loop(0, nfull, body, (neg, zero, one, neg, zero, one))
    stage_c(nfull - 1, (al0, al1))
    _, l0, be0, _, l1, be1 = stage_b(nfull, (m0, l0, m1, l1), True)
    stage_c(nfull, (be0, be1))
    ot = jnp.concatenate([acc_ref[0] / l0, acc_ref[1] / l1], axis=0)
    o_ref[0] = ot.T.astype(o_ref.dtype)


def _fox_prompt(q, k, v, c_col, c_row, tq, tk):
    b, s, _ = q.shape
    assert s % tk == 0 and tk % tq == 0
    npair = N_FOX_HEADS // 2
    return pl.pallas_call(
        functools.partial(_fox_prompt_kernel, tq=tq, tk=tk),
        out_shape=jax.ShapeDtypeStruct((b, s, FOX_WIDTH), BF16),
        grid=(b, npair, s // tq),
        in_specs=[
            pl.BlockSpec((1, tq, LANES), lambda bi, hp, i: (bi, i, hp)),
            pl.BlockSpec((1, s, LANES), lambda bi, hp, i: (bi, 0, hp)),
            pl.BlockSpec((1, s, LANES), lambda bi, hp, i: (bi, 0, hp)),
            pl.BlockSpec((1, 1, s, 2), lambda bi, hp, i: (bi, hp, 0, 0)),
            pl.BlockSpec((1, 1, 2, s), lambda bi, hp, i: (bi, hp, 0, 0)),
        ],
        out_specs=pl.BlockSpec((1, tq, LANES), lambda bi, hp, i: (bi, i, hp)),
        scratch_shapes=[pltpu.VMEM((LANES, s), BF16),
                        pltpu.VMEM((s, LANES), F32), pltpu.VMEM((s, LANES), F32),
                        pltpu.VMEM((2, 2, tk, tq), F32), pltpu.VMEM((2, 2, tk, tq), BF16),
                        pltpu.VMEM((2, FOX_HEAD_DIM, tq), F32)],
        compiler_params=_params(("parallel", "parallel", "arbitrary")), name="fox_prompt")(q, k, v, c_col, c_row)


def _fox_sample_kernel(q_ref, ck_ref, cv_ref, nk_ref, nv_ref, cq_ref, crc_ref, crn_ref, o_ref, *state, n):
    j = pl.program_id(1)
    nj = pl.num_programs(1)
    m_refs = state[0:N_FOX_HEADS]
    l_refs = state[N_FOX_HEADS:2 * N_FOX_HEADS]
    acc_refs = state[2 * N_FOX_HEADS:3 * N_FOX_HEADS]

    @pl.when(j == 0)
    def _():
        for hd in range(N_FOX_HEADS):
            m_refs[hd][...] = jnp.full(m_refs[hd].shape, NEG_INF, F32)
            l_refs[hd][...] = jnp.zeros(l_refs[hd].shape, F32)
            acc_refs[hd][...] = jnp.zeros(acc_refs[hd].shape, F32)

    def update(k_of, v_of, cr_ref_, mask, transposed):
        qk = _dot if transposed else _dot_nt
        pv = _dot_nt if transposed else _dot
        ts = []
        for hd in range(N_FOX_HEADS):
            hs = slice(hd * FOX_HEAD_DIM, (hd + 1) * FOX_HEAD_DIM)
            t = qk(q_ref[0, :, hs], k_of(hd)) - cr_ref_[0, hd:hd + 1, :]
            ts.append(t if mask is None else jnp.where(mask, t, NEG_INF))
        ps = []
        for hd in range(N_FOX_HEADS):
            cq = cq_ref[0, :, hd:hd + 1]
            m = m_refs[hd][...]
            m_new = jnp.maximum(m, jnp.max(ts[hd], axis=-1, keepdims=True) + cq)
            alpha = jnp.exp(m - m_new)
            p = jnp.exp(ts[hd] + (cq - m_new))
            m_refs[hd][...] = m_new
            l_refs[hd][...] = alpha * l_refs[hd][...] + jnp.sum(p, axis=-1, keepdims=True)
            ps.append((alpha, p.astype(BF16)))
        for hd in range(N_FOX_HEADS):
            alpha, p = ps[hd]
            acc_refs[hd][...] = alpha * acc_refs[hd][...] + pv(p, v_of(hd))

    update(lambda hd: ck_ref[0, hd].astype(BF16), lambda hd: cv_ref[0, hd].astype(BF16), crc_ref, None, True)

    @pl.when(j == nj - 1)
    def _():
        r = lax.broadcasted_iota(jnp.int32, (n, n), 0)
        c = lax.broadcasted_iota(jnp.int32, (n, n), 1)
        head = lambda ref: (lambda hd: ref[0, :, hd * FOX_HEAD_DIM:(hd + 1) * FOX_HEAD_DIM])
        update(head(nk_ref), head(nv_ref), crn_ref, c <= r, False)
        for hd in range(N_FOX_HEADS):
            hs = slice(hd * FOX_HEAD_DIM, (hd + 1) * FOX_HEAD_DIM)
            o_ref[0, :, hs] = (acc_refs[hd][...] / l_refs[hd][...]).astype(o_ref.dtype)


def _fox_sample(q, cache_k, cache_v, k_new, v_new, c_q, c_row_cache, c_row_new, tk):
    b, n, _ = q.shape
    past = cache_k.shape[3]
    assert past % tk == 0
    cache_spec = pl.BlockSpec((1, N_FOX_HEADS, FOX_HEAD_DIM, tk), lambda bi, j: (bi, 0, 0, j))
    return pl.pallas_call(
        functools.partial(_fox_sample_kernel, n=n),
        out_shape=jax.ShapeDtypeStruct((b, n, FOX_WIDTH), BF16),
        grid=(b, past // tk),
        in_specs=[
            pl.BlockSpec((1, n, FOX_WIDTH), lambda bi, j: (bi, 0, 0)),
            cache_spec,
            cache_spec,
            pl.BlockSpec((1, n, FOX_WIDTH), lambda bi, j: (bi, 0, 0)),
            pl.BlockSpec((1, n, FOX_WIDTH), lambda bi, j: (bi, 0, 0)),
            pl.BlockSpec((1, n, N_FOX_HEADS), lambda bi, j: (bi, 0, 0)),
            pl.BlockSpec((1, N_FOX_HEADS, tk), lambda bi, j: (bi, 0, j)),
            pl.BlockSpec((1, N_FOX_HEADS, n), lambda bi, j: (bi, 0, 0)),
        ],
        out_specs=pl.BlockSpec((1, n, FOX_WIDTH), lambda bi, j: (bi, 0, 0)),
        scratch_shapes=([pltpu.VMEM((n, 1), F32)] * (2 * N_FOX_HEADS)
                        + [pltpu.VMEM((n, FOX_HEAD_DIM), F32)] * N_FOX_HEADS),
        compiler_params=_params(("parallel", "arbitrary")), name="fox_sample")(
            q, cache_k, cache_v, k_new, v_new, c_q, c_row_cache, c_row_new)


def _ssm_mats(p):
    f32 = F32
    a_re, a_im = p['ssm_a_re'].astype(f32), p['ssm_a_im'].astype(f32)
    b_re, b_im = p['ssm_b_re'].astype(f32), p['ssm_b_im'].astype(f32)
    c_re, c_im = p['ssm_c_re'].astype(f32), p['ssm_c_im'].astype(f32)
    dt = jnp.exp(p['ssm_log_dt'].astype(f32))[:, None]
    mag = jnp.exp(dt * a_re)
    ab_re = mag * jnp.cos(dt * a_im)
    ab_im = mag * jnp.sin(dt * a_im)
    den = a_re * a_re + a_im * a_im
    nr, ni = ab_re - 1.0, ab_im
    coef_re = (nr * a_re + ni * a_im) / den
    coef_im = (ni * a_re - nr * a_im) / den
    bb_re = coef_re[..., None] * b_re - coef_im[..., None] * b_im
    bb_im = coef_re[..., None] * b_im + coef_im[..., None] * b_re
    pr, pi = [jnp.ones_like(ab_re)], [jnp.zeros_like(ab_im)]
    for _ in range(SSM_CHUNK):
        pr.append(pr[-1] * ab_re - pi[-1] * ab_im)
        pi.append(pr[-2] * ab_im + pi[-1] * ab_re)
    pw_re, pw_im = jnp.stack(pr), jnp.stack(pi)
    T = SSM_CHUNK
    w_re = pw_re[..., None] * bb_re[None] - pw_im[..., None] * bb_im[None]
    w_im = pw_re[..., None] * bb_im[None] + pw_im[..., None] * bb_re[None]
    kk = (jnp.einsum('gop,kgpi->kgoi', c_re, w_re[:T], precision='highest')
          - jnp.einsum('gop,kgpi->kgoi', c_im, w_im[:T], precision='highest'))
    eye = jnp.eye(SSM_GPB, dtype=f32)
    nq = N_SSM_GROUPS // SSM_GPB

    def lane_diag(m):
        lead = m.shape[:-3]
        i, c = m.shape[-2:]
        m = m.reshape(lead + (nq, SSM_GPB, i, 1, c)) * eye[:, None, :, None]
        return m.reshape(lead + (nq, SSM_GPB * i, SSM_GPB * c))

    ktau = lane_diag(jnp.swapaxes(kk, -1, -2))
    ktau = jnp.concatenate([jnp.zeros_like(ktau[:1]), ktau], axis=0)
    units = []
    for dlag in range(T // 2 - 1, -1, -1):
        top = jnp.concatenate([ktau[2 * dlag + 1], ktau[2 * dlag + 2]], axis=-1)
        bot = jnp.concatenate([ktau[2 * dlag], ktau[2 * dlag + 1]], axis=-1)
        units.append(jnp.concatenate([top, bot], axis=-2))
    kstack = jnp.concatenate(units, axis=-2).astype(BF16)
    rev = T - 1 - jnp.arange(T)
    m_re = lane_diag(jnp.swapaxes(w_re[rev], -1, -2))
    m_im = lane_diag(jnp.swapaxes(w_im[rev], -1, -2))
    m_all = jnp.concatenate([m_re, m_im], axis=-1)
    m_all = jnp.swapaxes(m_all, 0, 1).reshape(nq, T * LANES, 2 * SSM_GPB * SSM_STATE)
    m_hi, m_lo = _split_bf16(m_all)
    ar, ai = pw_re[1:], pw_im[1:]
    n_re = (c_re[None] * ar[:, :, None, :] - c_im[None] * ai[:, :, None, :])
    n_im = -(c_re[None] * ai[:, :, None, :] + c_im[None] * ar[:, :, None, :])

    def state_rows(n):
        n = lane_diag(n)
        return jnp.transpose(n, (1, 3, 0, 2)).reshape(nq, SSM_GPB * SSM_STATE, T * LANES)

    n_all = jnp.concatenate([state_rows(n_re), state_rows(n_im)], axis=1).astype(BF16)
    return dict(kstack=kstack, m_hi=m_hi, m_lo=m_lo, n_all=n_all,
                a16_re=pw_re[T].reshape(8, 256), a16_im=pw_im[T].reshape(8, 256),
                d=p['ssm_d'].astype(f32).reshape(1, SSM_WIDTH))


def _chunk_tokens(u_ref, rows):
    return [u_ref[pl.ds(t, rows, stride=SSM_CHUNK), :] for t in range(SSM_CHUNK)]


def _ssm_local_kernel(u_ref, mh_ref, ml_ref, hre_ref, him_ref):
    rows = hre_ref.shape[0]
    parts = [_split_bf16(ut) for ut in _chunk_tokens(u_ref, rows)]
    x_hi = jnp.concatenate([h for h, _ in parts], axis=1)
    x_lo = jnp.concatenate([l for _, l in parts], axis=1)
    h = _dot(x_hi, mh_ref[0]) + _dot(x_hi, ml_ref[0]) + _dot(x_lo, mh_ref[0])
    half = SSM_GPB * SSM_STATE
    hre_ref[...] = h[:, 0:half]
    him_ref[...] = h[:, half:2 * half]


def _ssm_local(u2d, mats, rows):
    n = u2d.shape[0]
    r = n // SSM_CHUNK
    nq = N_SSM_GROUPS // SSM_GPB
    half = SSM_GPB * SSM_STATE
    mspec = pl.BlockSpec((1, SSM_CHUNK * LANES, 2 * half), lambda q, i: (q, 0, 0))
    ospec = pl.BlockSpec((rows, half), lambda q, i: (i, q))
    return pl.pallas_call(
        _ssm_local_kernel,
        out_shape=(jax.ShapeDtypeStruct((r, N_SSM_GROUPS * SSM_STATE), F32),) * 2,
        grid=(nq, r // rows),
        in_specs=[pl.BlockSpec((rows * SSM_CHUNK, LANES), lambda q, i: (i, q)), mspec, mspec],
        out_specs=(ospec, ospec),
        compiler_params=_params(("parallel", "parallel")), name="ssm_local")(u2d, mats['m_hi'], mats['m_lo'])


def _ssm_scan_kernel(lre_ref, lim_ref, are_ref, aim_ref, h0re_ref, h0im_ref,
                     pre_ref, pim_ref, fre_ref, fim_ref):
    nchunk = lre_ref.shape[1]
    ar, ai = are_ref[...], aim_ref[...]

    def body(c, carry):
        hr, hi = carry
        pre_ref[0, c] = hr
        pim_ref[0, c] = hi
        return (ar * hr - ai * hi + lre_ref[0, c], ar * hi + ai * hr + lim_ref[0, c])

    hr, hi = lax.fori_loop(0, nchunk, body, (h0re_ref[0], h0im_ref[0]))
    fre_ref[0] = hr
    fim_ref[0] = hi


def _ssm_scan(hloc_re, hloc_im, mats, h0_re, h0_im):
    b, nchunk = hloc_re.shape[:2]
    big = pl.BlockSpec((1, nchunk, 8, 256), lambda i: (i, 0, 0, 0))
    small = pl.BlockSpec((1, 8, 256), lambda i: (i, 0, 0))
    return pl.pallas_call(
        _ssm_scan_kernel,
        out_shape=(jax.ShapeDtypeStruct(hloc_re.shape, F32),) * 2 + (jax.ShapeDtypeStruct((b, 8, 256), F32),) * 2,
        grid=(b,),
        in_specs=[big, big, _full((8, 256)), _full((8, 256)), small, small],
        out_specs=(big, big, small, small),
        compiler_params=_params(("parallel",)), name="ssm_scan")(
            hloc_re, hloc_im, mats['a16_re'], mats['a16_im'], h0_re, h0_im)


def _gelu_tanh(y):
    return 0.5 * y * (1.0 + jnp.tanh(math.sqrt(2.0 / math.pi) * (y + 0.044715 * (y * y * y))))


def _ssm_out_kernel(u_ref, k_ref, pre_ref, pim_ref, n_ref, d_ref, y_ref, ysc_ref):
    rows = pre_ref.shape[0]
    us = _chunk_tokens(u_ref, rows)
    x = jnp.concatenate([ut.astype(BF16) for ut in us], axis=1)
    hp = jnp.concatenate([pre_ref[...], pim_ref[...]], axis=1).astype(BF16)
    unit = 2 * LANES
    nunit = SSM_CHUNK // 2
    for j in range(nunit):
        yj = (_dot(x[:, 0:unit * (j + 1)], k_ref[0, unit * (nunit - 1 - j):, :])
              + _dot(hp, n_ref[0, :, unit * j:unit * (j + 1)]))
        for t2 in range(2):
            t = 2 * j + t2
            y = yj[:, t2 * LANES:(t2 + 1) * LANES] + d_ref[...] * us[t]
            ysc_ref[pl.ds(t, rows, stride=SSM_CHUNK), :] = _gelu_tanh(y)
    y_ref[...] = ysc_ref[...].astype(y_ref.dtype)


def _ssm_out(u2d, hprev_re, hprev_im, mats, rows):
    n = u2d.shape[0]
    r = n // SSM_CHUNK
    nq = N_SSM_GROUPS // SSM_GPB
    half = SSM_GPB * SSM_STATE
    uspec = pl.BlockSpec((rows * SSM_CHUNK, LANES), lambda q, i: (i, q))
    hspec = pl.BlockSpec((rows, half), lambda q, i: (i, q))
    return pl.pallas_call(
        _ssm_out_kernel,
        out_shape=jax.ShapeDtypeStruct((n, SSM_WIDTH), BF16),
        grid=(nq, r // rows),
        in_specs=[uspec, pl.BlockSpec((1, SSM_CHUNK * LANES, 2 * LANES), lambda q, i: (q, 0, 0)), hspec, hspec,
                  pl.BlockSpec((1, 2 * half, SSM_CHUNK * LANES), lambda q, i: (q, 0, 0)),
                  pl.BlockSpec((1, LANES), lambda q, i: (0, q))],
        out_specs=uspec,
        scratch_shapes=[pltpu.VMEM((rows * SSM_CHUNK, LANES), F32)],
        compiler_params=_params(("parallel", "parallel")), name="ssm_out")(
            u2d, mats['kstack'], hprev_re, hprev_im, mats['n_all'], mats['d'])


def _ssm(u2d, b, h0_re, h0_im, mats):
    n = u2d.shape[0]
    nchunk = n // b // SSM_CHUNK
    r = b * nchunk
    rows = _pick_tile(r, 256)
    hloc_re, hloc_im = _ssm_local(u2d, mats, rows)
    shp = (b, nchunk, 8, 256)
    hprev_re, hprev_im, f_re, f_im = _ssm_scan(hloc_re.reshape(shp), hloc_im.reshape(shp), mats,
                                               h0_re.reshape(b, 8, 256), h0_im.reshape(b, 8, 256))
    y = _ssm_out(u2d, hprev_re.reshape(r, -1), hprev_im.reshape(r, -1), mats, rows)
    return y, f_re.reshape(b, N_SSM_GROUPS, SSM_STATE), f_im.reshape(b, N_SSM_GROUPS, SSM_STATE)


def _merge_kernel(x_ref, of_ref, ys_ref, qm_ref, gate_ref, mk_ref, mv_ref,
                  wglu_ref, wbf_ref, wbs_ref, wbm_ref, wo_ref, nf_ref, wr_ref,
                  x1_ref, h2_ref, r_ref):
    tm = x_ref.shape[0]
    om = []
    for hd in range(N_MEM_HEADS):
        sl = slice(hd * MEM_HEAD_DIM, (hd + 1) * MEM_HEAD_DIM)
        kh = mk_ref[0, :, sl].astype(BF16)
        vh = mv_ref[0, :, sl].astype(BF16)
        sc = _dot_nt(qm_ref[:, sl], kh)
        p = jnp.exp(sc - jnp.max(sc, axis=-1, keepdims=True))
        om.append(_dot(p.astype(BF16), vh) / jnp.sum(p, axis=-1, keepdims=True))
    o_mem = jnp.concatenate(om, axis=-1).astype(BF16)
    z = _dot(ys_ref[...], wglu_ref[...])
    y_ssm = (z[:, 0:SSM_WIDTH] * jax.nn.sigmoid(z[:, SSM_WIDTH:2 * SSM_WIDTH])).astype(BF16)
    g = lambda c: gate_ref[:, c * D_MODEL:(c + 1) * D_MODEL].astype(F32)
    merged = (g(0) * _dot(of_ref[...], wbf_ref[...]) + g(1) * _dot(y_ssm, wbs_ref[...])
              + g(2) * _dot(o_mem, wbm_ref[...]))
    x1 = x_ref[...] + _dot(merged.astype(BF16), wo_ref[...])
    x1_ref[...] = x1
    h2 = x1 * lax.rsqrt(jnp.mean(x1 * x1, axis=-1, keepdims=True) + RMS_EPS) * nf_ref[...]
    h2_ref[...] = h2.astype(BF16)
    logits = _dot_exact(h2, wr_ref[...])
    lane = lax.broadcasted_iota(jnp.int32, (tm, LANES), 1)
    big = jnp.int32(LANES)
    is_grp = (lane >= N_EXPERTS) & (lane < N_EXPERTS + N_EXPERT_GROUPS)
    gl = jnp.where(is_grp, logits, NEG_INF)
    gmax = jnp.max(gl, axis=-1, keepdims=True)
    grp = jnp.min(jnp.where(is_grp & (gl == gmax), lane, big), axis=-1, keepdims=True) - N_EXPERTS
    g_w = 1.0 / jnp.sum(jnp.where(is_grp, jnp.exp(gl - gmax), 0.0), axis=-1, keepdims=True)
    in_grp = (lane >= grp * EXPERTS_PER_GROUP) & (lane < (grp + 1) * EXPERTS_PER_GROUP)
    e1 = jnp.where(in_grp, logits, NEG_INF)
    m1 = jnp.max(e1, axis=-1, keepdims=True)
    i1 = jnp.min(jnp.where(in_grp & (e1 == m1), lane, big), axis=-1, keepdims=True)
    rest = in_grp & (lane != i1)
    e2 = jnp.where(rest, logits, NEG_INF)
    m2 = jnp.max(e2, axis=-1, keepdims=True)
    i2 = jnp.min(jnp.where(rest & (e2 == m2), lane, big), axis=-1, keepdims=True)
    ex = jnp.exp(m2 - m1)
    w1 = g_w / (1.0 + ex)
    w2 = g_w * ex / (1.0 + ex)
    r_ref[...] = jnp.where(lane == i1, w1, jnp.where(lane == i2, w2, jnp.where(lane == GROUP_LANE, grp.astype(F32), 0.0)))


def _merge(x2d, o_fox, ys, q_m, gates, mem_k, mem_v, w, tm, rows_per_batch):
    n = x2d.shape[0]
    assert n % tm == 0 and rows_per_batch % tm == 0
    per = rows_per_batch // tm
    row = lambda width: pl.BlockSpec((tm, width), lambda i: (i, 0))
    memspec = pl.BlockSpec((1, N_MEM, MEM_WIDTH), lambda i: (i // per, 0, 0))
    ws = [w['w_glu'], w['w_br_fox'], w['w_br_ssm'], w['w_br_mem'], w['w_out'], w['norm_ffn'], w['w_router']]
    return pl.pallas_call(
        _merge_kernel,
        out_shape=(jax.ShapeDtypeStruct((n, D_MODEL), F32), jax.ShapeDtypeStruct((n, D_MODEL), BF16),
                   jax.ShapeDtypeStruct((n, LANES), F32)),
        grid=(n // tm,),
        in_specs=[row(D_MODEL), row(FOX_WIDTH), row(SSM_WIDTH), row(MEM_WIDTH), row(3 * D_MODEL), memspec, memspec]
                 + [_full(a.shape) for a in ws],
        out_specs=(row(D_MODEL), row(D_MODEL), row(LANES)),
        compiler_params=_params(("parallel",)), name="merge")(
            x2d, o_fox, ys, q_m, gates, mem_k, mem_v, *ws)


def _moe_kernel(h_ref, r_ref, x1_ref, tri_ref, wg_ref, wu_ref, wd_ref, o_ref,
                xs_ref, cw_ref, og_ref, acc_ref, rank_ref, nblk_ref):
    e = pl.program_id(1)
    g = e // EXPERTS_PER_GROUP
    el = e % EXPERTS_PER_GROUP
    tm = h_ref.shape[0]
    nsub = tm // MOE_SUB
    gf = g.astype(F32)

    @pl.when(e == 0)
    def _():
        acc_ref[...] = jnp.zeros(acc_ref.shape, F32)

    @pl.when(el == 0)
    def _():
        rt = r_ref[...]
        rtt = rt.T
        mrow = rtt[GROUP_LANE:GROUP_LANE + 1, :] == gf
        m8 = jnp.broadcast_to(jnp.where(mrow, 1.0, 0.0), (8, tm))
        rank8 = _dot(m8.astype(BF16), tri_ref[...])
        rank_row = jnp.where(mrow, rank8[0:1, :], -1.0)
        rank_ref[...] = jnp.broadcast_to(jnp.where(mrow, rank8, -1.0).T[:, 0:1], rank_ref.shape)
        count = jnp.sum(jnp.where(mrow, 1, 0))
        nblk_ref[0] = (count + MOE_SUB - 1) // MOE_SUB
        hi = rt.astype(BF16)
        rem = rt - hi.astype(F32)
        mid = rem.astype(BF16)
        lo = (rem - mid.astype(F32)).astype(BF16)
        for j in range(nsub):
            @pl.when(j < nblk_ref[0] + (nblk_ref[0] % 2))
            def _():
                rows = slice(j * MOE_SUB, (j + 1) * MOE_SUB)
                slot = j * MOE_SUB + lax.broadcasted_iota(jnp.int32, (MOE_SUB, tm), 0)
                perm = jnp.where(rank_row == slot.astype(F32), 1.0, 0.0).astype(BF16)
                xs_ref[rows, :] = _dot(perm, h_ref[...]).astype(BF16)
                cw_ref[rows, :] = _dot(perm, hi) + _dot(perm, mid) + _dot(perm, lo)
                og_ref[rows, :] = jnp.zeros((MOE_SUB, D_MODEL), F32)

    wg = wg_ref[0]
    wu = wu_ref[0]
    wd = wd_ref[0]
    lane = lax.broadcasted_iota(jnp.int32, (MOE_SUB, LANES), 1)
    for j in range(nsub):
        @pl.when(j < nblk_ref[0])
        def _():
            rows = slice(j * MOE_SUB, (j + 1) * MOE_SUB)
            x = xs_ref[rows, :]
            a = _dot(x, wg)
            up = _dot(x, wu)
            ce = jnp.sum(jnp.where(lane == e, cw_ref[rows, :], 0.0), axis=-1, keepdims=True)
            act = a * jax.nn.sigmoid(a) * up * ce
            og_ref[rows, :] += _dot(act.astype(BF16), wd)

    @pl.when(el == EXPERTS_PER_GROUP - 1)
    def _():
        pair = 2 * MOE_SUB
        for jj in range(nsub // 2):
            @pl.when(2 * jj < nblk_ref[0])
            def _():
                rows = slice(jj * pair, (jj + 1) * pair)
                slot = jj * pair + lax.broadcasted_iota(jnp.int32, (tm, pair), 1)
                back = jnp.where(rank_ref[:, 0:1] == slot.astype(F32), 1.0, 0.0).astype(BF16)
                acc_ref[...] += _dot(back, og_ref[rows, :].astype(BF16))

    @pl.when(e == pl.num_programs(1) - 1)
    def _():
        o_ref[...] = x1_ref[...] + acc_ref[...]


def _moe(h2, route, x1, wg, wu, wd, tm):
    n = h2.shape[0]
    assert n % tm == 0 and tm % (2 * MOE_SUB) == 0
    row = lambda width: pl.BlockSpec((tm, width), lambda i, e: (i, 0))
    r = jnp.arange(tm)
    tri = (r[:, None] < r[None, :]).astype(BF16)
    return pl.pallas_call(
        _moe_kernel,
        out_shape=jax.ShapeDtypeStruct((n, D_MODEL), F32),
        grid=(n // tm, N_EXPERTS),
        in_specs=[row(D_MODEL), row(LANES), row(D_MODEL), pl.BlockSpec((tm, tm), lambda i, e: (0, 0)),
                  pl.BlockSpec((1, D_MODEL, D_EXPERT), lambda i, e: (e, 0, 0)),
                  pl.BlockSpec((1, D_MODEL, D_EXPERT), lambda i, e: (e, 0, 0)),
                  pl.BlockSpec((1, D_EXPERT, D_MODEL), lambda i, e: (e, 0, 0))],
        out_specs=row(D_MODEL),
        scratch_shapes=[pltpu.VMEM((tm, D_MODEL), BF16), pltpu.VMEM((tm, LANES), F32), pltpu.VMEM((tm, D_MODEL), F32),
                        pltpu.VMEM((tm, D_MODEL), F32), pltpu.VMEM((tm, LANES), F32), pltpu.SMEM((1,), jnp.int32)],
        compiler_params=_params(("parallel", "arbitrary")), name="moe")(h2, route, x1, tri, wg, wu, wd)


def _prep_weights(p):
    w_in = p['w_in'].astype(BF16)
    o = 0
    wqkv = w_in[:, 0:3 * FOX_WIDTH]
    o = 3 * FOX_WIDTH
    wf = jnp.pad(w_in[:, o:o + N_FOX_HEADS], ((0, 0), (0, LANES - N_FOX_HEADS)))
    o += N_FOX_HEADS
    wqm = w_in[:, o:o + MEM_WIDTH]
    o += MEM_WIDTH
    wu = w_in[:, o:o + SSM_WIDTH]
    o += SSM_WIDTH
    wg = w_in[:, o:o + 3 * D_MODEL]
    r = jnp.arange(FOX_WIDTH) // FOX_HEAD_DIM
    bd = (r[:, None] == r[None, :]).astype(BF16)
    w_router = jnp.concatenate(
        [p['w_router_expert'], p['w_router_group'],
         jnp.zeros((D_MODEL, LANES - N_EXPERTS - N_EXPERT_GROUPS), F32)], axis=1)
    return dict(
        norm_mix=p['norm_mix'].reshape(1, D_MODEL), wqkv=wqkv, wf=wf,
        bf=jnp.pad(p['b_forget'], (0, LANES - N_FOX_HEADS)).reshape(1, LANES),
        wqm=wqm, wu=wu, wg=wg,
        qn_fox=jnp.tile(p['qn_fox'], N_FOX_HEADS).reshape(1, FOX_WIDTH),
        kn_fox=jnp.tile(p['kn_fox'], N_FOX_HEADS).reshape(1, FOX_WIDTH),
        qn_mem=p['qn_mem'].reshape(1, MEM_HEAD_DIM), bd=bd,
        w_glu=p['w_glu'].astype(BF16), w_br_fox=p['w_br_fox'].astype(BF16),
        w_br_ssm=p['w_br_ssm'].astype(BF16), w_br_mem=p['w_br_mem'].astype(BF16),
        w_out=p['w_out'].astype(BF16), norm_ffn=p['norm_ffn'].reshape(1, D_MODEL), w_router=w_router,
        moe_wg=p['moe_w_gate'].astype(BF16), moe_wu=p['moe_w_up'].astype(BF16),
        moe_wd=p['moe_w_down'].astype(BF16))


def _pick_tile(n, target):
    t = min(n, target)
    while n % t:
        t //= 2
    return t


def _group(x, w, mats, mem_k, mem_v, h0_re, h0_im, cache):
    b, s, _ = x.shape
    n = b * s
    x2d = x.reshape(n, D_MODEL)
    prompt = cache is None
    q, kb, vb, k_out, v_out, lf_t, q_m, u, gates = _inproj(x2d, w, _pick_tile(s if prompt else n, 512), s, prompt)
    q3 = q.reshape(b, s, FOX_WIDTH)
    k3 = kb.reshape(b, s, FOX_WIDTH)
    v3 = vb.reshape(b, s, FOX_WIDTH)
    lf_rows = lf_t.reshape(N_FOX_HEADS, b, s).transpose(1, 0, 2)
    lf3 = lf_rows.transpose(0, 2, 1)
    npair = N_FOX_HEADS // 2
    if prompt:
        c_row = _cumsum_rows(lf_rows.reshape(b * N_FOX_HEADS, s)).reshape(b, npair, 2, s)
        c_col = c_row.transpose(0, 1, 3, 2)
        o_fox = _fox_prompt(q3, k3, v3, c_col, c_row, _pick_tile(s, 256), _pick_tile(s, 512))
        unt = lambda a: a.reshape(b, N_FOX_HEADS, FOX_HEAD_DIM, s).transpose(0, 3, 1, 2)
        k4, v4 = unt(k_out), unt(v_out)
    else:
        cache_k, cache_v, cache_logf = cache
        past = cache_k.shape[1]
        lf_all = jnp.concatenate([cache_logf.astype(F32).transpose(0, 2, 1), lf_rows], axis=2)
        c_row = _cumsum_rows(lf_all.reshape(b * N_FOX_HEADS, past + s)).reshape(b, N_FOX_HEADS, past + s)
        o_fox = _fox_sample(q3, cache_k.transpose(0, 2, 3, 1), cache_v.transpose(0, 2, 3, 1), k3, v3,
                            c_row[:, :, past:].transpose(0, 2, 1), c_row[:, :, :past], c_row[:, :, past:],
                            _pick_tile(past, 1024))
        k4 = k_out.reshape(b, s, N_FOX_HEADS, FOX_HEAD_DIM)
        v4 = v_out.reshape(b, s, N_FOX_HEADS, FOX_HEAD_DIM)
    ys, f_re, f_im = _ssm(u, b, h0_re, h0_im, mats)
    tm = _pick_tile(s, 512)
    x1, h2, route = _merge(x2d, o_fox.reshape(n, FOX_WIDTH), ys, q_m, gates, mem_k, mem_v, w, tm, s)
    y = _moe(h2, route, x1, w['moe_wg'], w['moe_wu'], w['moe_wd'], _pick_tile(n, 1024))
    return y.reshape(b, s, D_MODEL), k4, v4, lf3, f_re, f_im


def kernel(x_prompt, x_sample, mem_prompt, cache_fox_k, cache_fox_v, cache_fox_logf, state_ssm_re, state_ssm_im,
           cache_mem_k, cache_mem_v, norm_mix, w_in, b_forget, qn_fox, kn_fox, qn_mem, kn_mem, norm_mem, w_mem_kv,
           ssm_a_re, ssm_a_im, ssm_log_dt, ssm_b_re, ssm_b_im, ssm_c_re, ssm_c_im, ssm_d, w_glu, w_br_fox,
           w_br_ssm, w_br_mem, w_out, norm_ffn, w_router_group, w_router_expert, moe_w_gate, moe_w_up,
           moe_w_down):
    depth = norm_mix.shape[0]
    assert depth == 1
    l = 0
    p = dict(norm_mix=norm_mix[l], w_in=w_in[l], b_forget=b_forget[l], qn_fox=qn_fox[l], kn_fox=kn_fox[l],
             qn_mem=qn_mem[l], ssm_a_re=ssm_a_re[l], ssm_a_im=ssm_a_im[l], ssm_log_dt=ssm_log_dt[l],
             ssm_b_re=ssm_b_re[l], ssm_b_im=ssm_b_im[l], ssm_c_re=ssm_c_re[l], ssm_c_im=ssm_c_im[l],
             ssm_d=ssm_d[l], w_glu=w_glu[l], w_br_fox=w_br_fox[l], w_br_ssm=w_br_ssm[l], w_br_mem=w_br_mem[l],
             w_out=w_out[l], norm_ffn=norm_ffn[l], w_router_group=w_router_group[l],
             w_router_expert=w_router_expert[l], moe_w_gate=moe_w_gate[l], moe_w_up=moe_w_up[l],
             moe_w_down=moe_w_down[l])
    w = _prep_weights(p)
    mats = _ssm_mats(p)
    bp, sp, _ = x_prompt.shape
    bs, ss, _ = x_sample.shape

    mk, mv = _memkv(mem_prompt.reshape(bp * N_MEM, D_MODEL), norm_mem[l].reshape(1, D_MODEL),
                    w_mem_kv[l].astype(BF16), kn_mem[l].reshape(1, MEM_HEAD_DIM), _pick_tile(bp * N_MEM, 512))
    mk = mk.reshape(bp, N_MEM, MEM_WIDTH)
    mv = mv.reshape(bp, N_MEM, MEM_WIDTH)
    zeros = jnp.zeros((bp, N_SSM_GROUPS, SSM_STATE), F32)
    yp, pk, pv, plf, pre, pim = _group(x_prompt, w, mats, mk, mv, zeros, zeros, None)
    cache = (cache_fox_k[l], cache_fox_v[l], cache_fox_logf[l])
    ys, sk, sv, slf, sre, sim = _group(
        x_sample, w, mats, cache_mem_k[l].reshape(bs, N_MEM, MEM_WIDTH), cache_mem_v[l].reshape(bs, N_MEM, MEM_WIDTH),
        state_ssm_re[l].astype(F32), state_ssm_im[l].astype(F32), cache)
    st = lambda a: a[None]
    return (yp, ys, st(pk), st(pv), st(plf), st(pre), st(pim),
            st(mk.reshape(bp, N_MEM, N_MEM_HEADS, MEM_HEAD_DIM)), st(mv.reshape(bp, N_MEM, N_MEM_HEADS, MEM_HEAD_DIM)),
            st(sk), st(sv), st(slf), st(sre), st(sim))
```

```python
import functools
import math

import jax
import jax.numpy as jnp
from jax import lax
from jax.experimental import pallas as pl
from jax.experimental.pallas import tpu as pltpu

F32 = jnp.float32
BF16 = jnp.bfloat16

D_MODEL = 1024
N_FOX_HEADS = 8
FOX_HEAD_DIM = 64
FOX_WIDTH = N_FOX_HEADS * FOX_HEAD_DIM
N_MEM = 256
N_MEM_HEADS = 4
MEM_HEAD_DIM = 128
MEM_WIDTH = N_MEM_HEADS * MEM_HEAD_DIM
SSM_GROUP = 16
SSM_WIDTH = 512
N_SSM_GROUPS = SSM_WIDTH // SSM_GROUP
SSM_STATE = 64
N_EXPERT_GROUPS = 4
EXPERTS_PER_GROUP = 8
N_EXPERTS = N_EXPERT_GROUPS * EXPERTS_PER_GROUP
D_EXPERT = 256
RMS_EPS = 1e-6
NEG_INF = -1e30

LANES = 128
SSM_CHUNK = 16
SSM_GPB = LANES // SSM_GROUP
GROUP_LANE = N_EXPERTS
MOE_SUB = 128
MOE_EPS = 4
VMEM_LIMIT = 56 * 1024 * 1024


def _dot(a, b):
    return jnp.dot(a, b, preferred_element_type=F32)


def _dot_nt(a, b):
    return lax.dot_general(a, b, (((1,), (1,)), ((), ())), preferred_element_type=F32)


def _dot_exact(a, b):
    return jnp.dot(a, b, preferred_element_type=F32, precision=lax.Precision.HIGHEST)


def _split_bf16(x):
    hi = x.astype(BF16)
    lo = (x - hi.astype(F32)).astype(BF16)
    return hi, lo


def _params(sem):
    return pltpu.CompilerParams(dimension_semantics=sem, vmem_limit_bytes=VMEM_LIMIT)


def _full(shape):
    n = len(shape)
    return pl.BlockSpec(shape, lambda *_: (0,) * n)


def _inproj_kernel(x_ref, g_ref, wqkv_ref, wf_ref, bf_ref, wqm_ref, wu_ref, wg_ref,
                   qn_ref, kn_ref, qmn_ref, bd_ref,
                   q_ref, kb_ref, vb_ref, k_ref, v_ref, lf_ref, qm_ref, u_ref, gate_ref, *, kv_transposed):
    x = x_ref[...]
    h = x * lax.rsqrt(jnp.mean(x * x, axis=-1, keepdims=True) + RMS_EPS) * g_ref[...]
    hb = h.astype(BF16)

    def head_norm(z, gain):
        hi, lo = _split_bf16(z * z)
        ss = _dot(hi, bd_ref[...]) + _dot(lo, bd_ref[...])
        return z * lax.rsqrt(ss * (1.0 / FOX_HEAD_DIM) + RMS_EPS) * gain

    zq = _dot(hb, wqkv_ref[:, 0:FOX_WIDTH])
    q_ref[...] = (head_norm(zq, qn_ref[...]) * (FOX_HEAD_DIM ** -0.5)).astype(BF16)
    zk = _dot(hb, wqkv_ref[:, FOX_WIDTH:2 * FOX_WIDTH])
    kn = head_norm(zk, kn_ref[...])
    zv = _dot(hb, wqkv_ref[:, 2 * FOX_WIDTH:3 * FOX_WIDTH])
    kb_ref[...] = kn.astype(BF16)
    vb_ref[...] = zv.astype(BF16)
    tm = x_ref.shape[0]
    if kv_transposed:
        k_ref[0] = kn.T
        v_ref[0] = zv.T
    else:
        for hd in range(N_FOX_HEADS):
            hs = slice(hd * FOX_HEAD_DIM, (hd + 1) * FOX_HEAD_DIM)
            rows = pl.ds(hd, tm, stride=N_FOX_HEADS)
            k_ref[rows, :] = kn[:, hs]
            v_ref[rows, :] = zv[:, hs]

    zf = (_dot(hb, wf_ref[...]) + bf_ref[...]).T[0:N_FOX_HEADS, :]
    lf_ref[...] = jnp.minimum(zf, 0.0) - jnp.log1p(jnp.exp(-jnp.abs(zf)))

    zm = _dot(hb, wqm_ref[...])
    for hd in range(N_MEM_HEADS):
        sl = slice(hd * MEM_HEAD_DIM, (hd + 1) * MEM_HEAD_DIM)
        zh = zm[:, sl]
        ms = jnp.mean(zh * zh, axis=-1, keepdims=True)
        qm_ref[:, sl] = (zh * lax.rsqrt(ms + RMS_EPS) * qmn_ref[...] * (MEM_HEAD_DIM ** -0.5)).astype(BF16)

    u_ref[...] = _dot(hb, wu_ref[...])
    for c in range(3):
        sl = slice(c * D_MODEL, (c + 1) * D_MODEL)
        gate_ref[:, sl] = jax.nn.sigmoid(_dot(hb, wg_ref[:, sl])).astype(BF16)


def _inproj(x2d, w, tm, seq, kv_transposed):
    n = x2d.shape[0]
    assert n % tm == 0
    row = lambda width: pl.BlockSpec((tm, width), lambda i: (i, 0))
    if kv_transposed:
        assert seq % tm == 0
        per = seq // tm
        kv_shape = jax.ShapeDtypeStruct((n // seq, FOX_WIDTH, seq), F32)
        heads = pl.BlockSpec((1, FOX_WIDTH, tm), lambda i: (i // per, 0, i % per))
    else:
        kv_shape = jax.ShapeDtypeStruct((n * N_FOX_HEADS, FOX_HEAD_DIM), F32)
        heads = pl.BlockSpec((tm * N_FOX_HEADS, FOX_HEAD_DIM), lambda i: (i, 0))
    ins = [x2d, w['norm_mix'], w['wqkv'], w['wf'], w['bf'], w['wqm'], w['wu'], w['wg'],
           w['qn_fox'], w['kn_fox'], w['qn_mem'], w['bd']]
    in_specs = [row(D_MODEL)] + [_full(a.shape) for a in ins[1:]]
    out_shape = (
        jax.ShapeDtypeStruct((n, FOX_WIDTH), BF16),
        jax.ShapeDtypeStruct((n, FOX_WIDTH), BF16),
        jax.ShapeDtypeStruct((n, FOX_WIDTH), BF16),
        kv_shape,
        kv_shape,
        jax.ShapeDtypeStruct((N_FOX_HEADS, n), F32),
        jax.ShapeDtypeStruct((n, MEM_WIDTH), BF16),
        jax.ShapeDtypeStruct((n, SSM_WIDTH), F32),
        jax.ShapeDtypeStruct((n, 3 * D_MODEL), BF16),
    )
    out_specs = (row(FOX_WIDTH), row(FOX_WIDTH), row(FOX_WIDTH), heads, heads,
                 pl.BlockSpec((N_FOX_HEADS, tm), lambda i: (0, i)),
                 row(MEM_WIDTH), row(SSM_WIDTH), row(3 * D_MODEL))
    return pl.pallas_call(
        functools.partial(_inproj_kernel, kv_transposed=kv_transposed),
        out_shape=out_shape, grid=(n // tm,), in_specs=in_specs, out_specs=out_specs,
        compiler_params=_params(("parallel",)), name="inproj")(*ins)


def _memkv_kernel(x_ref, g_ref, w_ref, kn_ref, k_ref, v_ref):
    x = x_ref[...]
    h = x * lax.rsqrt(jnp.mean(x * x, axis=-1, keepdims=True) + RMS_EPS) * g_ref[...]
    hb = h.astype(BF16)
    zk = _dot(hb, w_ref[:, 0:MEM_WIDTH])
    for hd in range(N_MEM_HEADS):
        sl = slice(hd * MEM_HEAD_DIM, (hd + 1) * MEM_HEAD_DIM)
        zh = zk[:, sl]
        ms = jnp.mean(zh * zh, axis=-1, keepdims=True)
        k_ref[:, sl] = zh * lax.rsqrt(ms + RMS_EPS) * kn_ref[...]
    v_ref[...] = _dot(hb, w_ref[:, MEM_WIDTH:2 * MEM_WIDTH])


def _memkv(mem2d, norm_mem, w_kv, kn_mem, tm):
    n = mem2d.shape[0]
    row = lambda width: pl.BlockSpec((tm, width), lambda i: (i, 0))
    return pl.pallas_call(
        _memkv_kernel,
        out_shape=(jax.ShapeDtypeStruct((n, MEM_WIDTH), F32), jax.ShapeDtypeStruct((n, MEM_WIDTH), F32)),
        grid=(n // tm,),
        in_specs=[row(D_MODEL), _full(norm_mem.shape), _full(w_kv.shape), _full(kn_mem.shape)],
        out_specs=(row(MEM_WIDTH), row(MEM_WIDTH)),
        compiler_params=_params(("parallel",)), name="memkv")(mem2d, norm_mem, w_kv, kn_mem)


CUMSUM_BLOCK = 256


def _cumsum_kernel(x_ref, o_ref):
    nblk = x_ref.shape[1] // CUMSUM_BLOCK
    r = lax.broadcasted_iota(jnp.int32, (CUMSUM_BLOCK, CUMSUM_BLOCK), 0)
    c = lax.broadcasted_iota(jnp.int32, (CUMSUM_BLOCK, CUMSUM_BLOCK), 1)
    tri = (r <= c).astype(F32)
    carry = jnp.zeros((x_ref.shape[0], 1), F32)
    for j in range(nblk):
        sl = slice(j * CUMSUM_BLOCK, (j + 1) * CUMSUM_BLOCK)
        cs = _dot_exact(x_ref[:, sl], tri) + carry
        o_ref[:, sl] = cs
        carry = cs[:, CUMSUM_BLOCK - 1:CUMSUM_BLOCK]


def _cumsum_rows(x):
    rows, n = x.shape
    npad = -(-n // CUMSUM_BLOCK) * CUMSUM_BLOCK
    xp = jnp.pad(x, ((0, 0), (0, npad - n))) if npad != n else x
    out = pl.pallas_call(
        _cumsum_kernel, out_shape=jax.ShapeDtypeStruct((rows, npad), F32), grid=(1,),
        in_specs=[_full((rows, npad))], out_specs=_full((rows, npad)),
        compiler_params=_params(("arbitrary",)), name="cumsum")(xp)
    return out[:, :n] if npad != n else out


def _reduce_rows(x, op):
    rows, cols = x.shape
    if rows > 64 and rows % 64 == 0:
        x = op(x.reshape(rows // 64, 64, cols), axis=0)
        rows = 64
    if rows == 64:
        x = op(x.reshape(8, 8, cols), axis=0)
    return op(x, axis=0, keepdims=True)


def _head_lane_masks(rows):
    lane = lax.broadcasted_iota(jnp.int32, (rows, LANES), 1)
    return lane < FOX_HEAD_DIM


def _softmax_step(qh, kb, vb, ck, cq, m, l, acc, mask):
    t = _dot_nt(qh, kb) - ck
    if mask is not None:
        t = jnp.where(mask, t, NEG_INF)
    m_new = jnp.maximum(m, jnp.max(t, axis=-1, keepdims=True) + cq)
    alpha = jnp.exp(m - m_new)
    p = jnp.exp(t + (cq - m_new))
    l_new = alpha * l + jnp.sum(p, axis=-1, keepdims=True)
    acc_new = alpha * acc + _dot(p.astype(BF16), vb)
    return m_new, l_new, acc_new


def _fox_prompt_kernel(q_ref, k_ref, v_ref, cc_ref, cr_ref, o_ref,
                       vt_ref, ck0_ref, ck1_ref, st_ref, pt_ref, acc_ref, *, tq, tk):
    i = pl.program_id(2)
    s_len = k_ref.shape[1]

    @pl.when(i == 0)
    def _():
        vt_ref[...] = v_ref[0].astype(F32).T.astype(BF16)
        ck0_ref[...] = jnp.broadcast_to(cc_ref[0, 0, :, 0:1], (s_len, LANES))
        ck1_ref[...] = jnp.broadcast_to(cc_ref[0, 0, :, 1:2], (s_len, LANES))

    qt = q_ref[0].astype(F32).T
    row = lax.broadcasted_iota(jnp.int32, (LANES, tq), 0)
    qts = (jnp.where(row < FOX_HEAD_DIM, qt, 0.0).astype(BF16), jnp.where(row < FOX_HEAD_DIM, 0.0, qt).astype(BF16))
    q0 = pl.multiple_of(i * tq, tq)
    cq = cr_ref[0, 0, :, pl.ds(q0, tq)]
    ck_refs = (ck0_ref, ck1_ref)

    def stage_a(n):
        s = pl.multiple_of(n * tk, tk)
        kb = k_ref[0, pl.ds(s, tk), :]
        for hh in range(2):
            ck = ck_refs[hh][pl.ds(s, tk), :]
            st_ref[n & 1, hh] = _dot(kb, qts[hh]) - jnp.concatenate([ck] * (tq // LANES), axis=1)

    def stage_b(n, stats, masked):
        if masked:
            kpos = n * tk + lax.broadcasted_iota(jnp.int32, (tk, tq), 0)
            qpos = q0 + lax.broadcasted_iota(jnp.int32, (tk, tq), 1)
            mask = kpos <= qpos
        out = []
        for hh in range(2):
            m, l = stats[2 * hh:2 * hh + 2]
            t = st_ref[n & 1, hh]
            if masked:
                t = jnp.where(mask, t, NEG_INF)
            cqh = cq[hh:hh + 1, :]
            m_new = jnp.maximum(m, _reduce_rows(t, jnp.max) + cqh)
            alpha = jnp.exp(m - m_new)
            p = jnp.exp(t + (cqh - m_new))
            pt_ref[n & 1, hh] = p.astype(BF16)
            out.extend([m_new, alpha * l + _reduce_rows(p, jnp.sum), alpha])
        return out

    def stage_c(n, alphas):
        s = pl.multiple_of(jnp.maximum(n, 0) * tk, tk)
        for hh in range(2):
            vt = vt_ref[hh * FOX_HEAD_DIM:(hh + 1) * FOX_HEAD_DIM, pl.ds(s, tk)]
            acc_ref[hh] = alphas[hh] * acc_ref[hh] + _dot(vt, pt_ref[n & 1, hh])

    acc_ref[...] = jnp.zeros(acc_ref.shape, F32)
    pt_ref[1] = jnp.zeros(pt_ref.shape[1:], BF16)
    neg = jnp.full((1, tq), NEG_INF, F32)
    zero = jnp.zeros((1, tq), F32)
    one = jnp.ones((1, tq), F32)
    nfull = (i * tq) // tk
    stage_a(0)

    def body(n, carry):
        m0, l0, al0, m1, l1, al1 = carry
        stage_c(n - 1, (al0, al1))
        new = stage_b(n, (m0, l0, m1, l1), False)
        stage_a(n + 1)
        return tuple(new)

    m0, l0, al0, m1, l1, al1 = lax.fori_loop(0, nfull, body, (neg, zero, one, neg, zero, one))
    stage_c(nfull - 1, (al0, al1))
    _, l0, be0, _, l1, be1 = stage_b(nfull, (m0, l0, m1, l1), True)
    stage_c(nfull, (be0, be1))
    ot = jnp.concatenate([acc_ref[0] / l0, acc_ref[1] / l1], axis=0)
    o_ref[0] = ot.T.astype(o_ref.dtype)


def _fox_prompt(q, k, v, c_col, c_row, tq, tk):
    b, s, _ = q.shape
    assert s % tk == 0 and tk % tq == 0
    npair = N_FOX_HEADS // 2
    return pl.pallas_call(
        functools.partial(_fox_prompt_kernel, tq=tq, tk=tk),
        out_shape=jax.ShapeDtypeStruct((b, s, FOX_WIDTH), BF16),
        grid=(b, npair, s // tq),
        in_specs=[
            pl.BlockSpec((1, tq, LANES), lambda bi, hp, i: (bi, i, hp)),
            pl.BlockSpec((1, s, LANES), lambda bi, hp, i: (bi, 0, hp)),
            pl.BlockSpec((1, s, LANES), lambda bi, hp, i: (bi, 0, hp)),
            pl.BlockSpec((1, 1, s, 2), lambda bi, hp, i: (bi, hp, 0, 0)),
            pl.BlockSpec((1, 1, 2, s), lambda bi, hp, i: (bi, hp, 0, 0)),
        ],
        out_specs=pl.BlockSpec((1, tq, LANES), lambda bi, hp, i: (bi, i, hp)),
        scratch_shapes=[pltpu.VMEM((LANES, s), BF16),
                        pltpu.VMEM((s, LANES), F32), pltpu.VMEM((s, LANES), F32),
                        pltpu.VMEM((2, 2, tk, tq), F32), pltpu.VMEM((2, 2, tk, tq), BF16),
                        pltpu.VMEM((2, FOX_HEAD_DIM, tq), F32)],
        compiler_params=_params(("parallel", "parallel", "arbitrary")), name="fox_prompt")(q, k, v, c_col, c_row)


def _fox_sample_kernel(q_ref, ck_ref, cv_ref, nk_ref, nv_ref, cq_ref, crc_ref, crn_ref, o_ref, *state, n):
    j = pl.program_id(1)
    nj = pl.num_programs(1)
    m_refs = state[0:N_FOX_HEADS]
    l_refs = state[N_FOX_HEADS:2 * N_FOX_HEADS]
    acc_refs = state[2 * N_FOX_HEADS:3 * N_FOX_HEADS]

    @pl.when(j == 0)
    def _():
        for hd in range(N_FOX_HEADS):
            m_refs[hd][...] = jnp.full(m_refs[hd].shape, NEG_INF, F32)
            l_refs[hd][...] = jnp.zeros(l_refs[hd].shape, F32)
            acc_refs[hd][...] = jnp.zeros(acc_refs[hd].shape, F32)

    def update(k_of, v_of, cr_ref_, mask, transposed):
        qk = _dot if transposed else _dot_nt
        pv = _dot_nt if transposed else _dot
        ts = []
        for hd in range(N_FOX_HEADS):
            hs = slice(hd * FOX_HEAD_DIM, (hd + 1) * FOX_HEAD_DIM)
            t = qk(q_ref[0, :, hs], k_of(hd)) - cr_ref_[0, hd:hd + 1, :]
            ts.append(t if mask is None else jnp.where(mask, t, NEG_INF))
        ps = []
        for hd in range(N_FOX_HEADS):
            cq = cq_ref[0, :, hd:hd + 1]
            m = m_refs[hd][...]
            m_new = jnp.maximum(m, jnp.max(ts[hd], axis=-1, keepdims=True) + cq)
            alpha = jnp.exp(m - m_new)
            p = jnp.exp(ts[hd] + (cq - m_new))
            m_refs[hd][...] = m_new
            l_refs[hd][...] = alpha * l_refs[hd][...] + jnp.sum(p, axis=-1, keepdims=True)
            ps.append((alpha, p.astype(BF16)))
        for hd in range(N_FOX_HEADS):
            alpha, p = ps[hd]
            acc_refs[hd][...] = alpha * acc_refs[hd][...] + pv(p, v_of(hd))

    update(lambda hd: ck_ref[0, hd].astype(BF16), lambda hd: cv_ref[0, hd].astype(BF16), crc_ref, None, True)

    @pl.when(j == nj - 1)
    def _():
        r = lax.broadcasted_iota(jnp.int32, (n, n), 0)
        c = lax.broadcasted_iota(jnp.int32, (n, n), 1)
        head = lambda ref: (lambda hd: ref[0, :, hd * FOX_HEAD_DIM:(hd + 1) * FOX_HEAD_DIM])
        update(head(nk_ref), head(nv_ref), crn_ref, c <= r, False)
        for hd in range(N_FOX_HEADS):
            hs = slice(hd * FOX_HEAD_DIM, (hd + 1) * FOX_HEAD_DIM)
            o_ref[0, :, hs] = (acc_refs[hd][...] / l_refs[hd][...]).astype(o_ref.dtype)


def _fox_sample(q, cache_k, cache_v, k_new, v_new, c_q, c_row_cache, c_row_new, tk):
    b, n, _ = q.shape
    past = cache_k.shape[3]
    assert past % tk == 0
    cache_spec = pl.BlockSpec((1, N_FOX_HEADS, FOX_HEAD_DIM, tk), lambda bi, j: (bi, 0, 0, j))
    return pl.pallas_call(
        functools.partial(_fox_sample_kernel, n=n),
        out_shape=jax.ShapeDtypeStruct((b, n, FOX_WIDTH), BF16),
        grid=(b, past // tk),
        in_specs=[
            pl.BlockSpec((1, n, FOX_WIDTH), lambda bi, j: (bi, 0, 0)),
            cache_spec,
            cache_spec,
            pl.BlockSpec((1, n, FOX_WIDTH), lambda bi, j: (bi, 0, 0)),
            pl.BlockSpec((1, n, FOX_WIDTH), lambda bi, j: (bi, 0, 0)),
            pl.BlockSpec((1, n, N_FOX_HEADS), lambda bi, j: (bi, 0, 0)),
            pl.BlockSpec((1, N_FOX_HEADS, tk), lambda bi, j: (bi, 0, j)),
            pl.BlockSpec((1, N_FOX_HEADS, n), lambda bi, j: (bi, 0, 0)),
        ],
        out_specs=pl.BlockSpec((1, n, FOX_WIDTH), lambda bi, j: (bi, 0, 0)),
        scratch_shapes=([pltpu.VMEM((n, 1), F32)] * (2 * N_FOX_HEADS)
                        + [pltpu.VMEM((n, FOX_HEAD_DIM), F32)] * N_FOX_HEADS),
        compiler_params=_params(("parallel", "arbitrary")), name="fox_sample")(
            q, cache_k, cache_v, k_new, v_new, c_q, c_row_cache, c_row_new)


def _ssm_mats(p):
    f32 = F32
    a_re, a_im = p['ssm_a_re'].astype(f32), p['ssm_a_im'].astype(f32)
    b_re, b_im = p['ssm_b_re'].astype(f32), p['ssm_b_im'].astype(f32)
    c_re, c_im = p['ssm_c_re'].astype(f32), p['ssm_c_im'].astype(f32)
    dt = jnp.exp(p['ssm_log_dt'].astype(f32))[:, None]
    mag = jnp.exp(dt * a_re)
    ab_re = mag * jnp.cos(dt * a_im)
    ab_im = mag * jnp.sin(dt * a_im)
    den = a_re * a_re + a_im * a_im
    nr, ni = ab_re - 1.0, ab_im
    coef_re = (nr * a_re + ni * a_im) / den
    coef_im = (ni * a_re - nr * a_im) / den
    bb_re = coef_re[..., None] * b_re - coef_im[..., None] * b_im
    bb_im = coef_re[..., None] * b_im + coef_im[..., None] * b_re
    pr, pi = [jnp.ones_like(ab_re)], [jnp.zeros_like(ab_im)]
    for _ in range(SSM_CHUNK):
        pr.append(pr[-1] * ab_re - pi[-1] * ab_im)
        pi.append(pr[-2] * ab_im + pi[-1] * ab_re)
    pw_re, pw_im = jnp.stack(pr), jnp.stack(pi)
    T = SSM_CHUNK
    w_re = pw_re[..., None] * bb_re[None] - pw_im[..., None] * bb_im[None]
    w_im = pw_re[..., None] * bb_im[None] + pw_im[..., None] * bb_re[None]
    kk = (jnp.einsum('gop,kgpi->kgoi', c_re, w_re[:T], precision='highest')
          - jnp.einsum('gop,kgpi->kgoi', c_im, w_im[:T], precision='highest'))
    eye = jnp.eye(SSM_GPB, dtype=f32)
    nq = N_SSM_GROUPS // SSM_GPB

    def lane_diag(m):
        lead = m.shape[:-3]
        i, c = m.shape[-2:]
        m = m.reshape(lead + (nq, SSM_GPB, i, 1, c)) * eye[:, None, :, None]
        return m.reshape(lead + (nq, SSM_GPB * i, SSM_GPB * c))

    ktau = lane_diag(jnp.swapaxes(kk, -1, -2))
    ktau = jnp.concatenate([jnp.zeros_like(ktau[:1]), ktau], axis=0)
    units = []
    for dlag in range(T // 2 - 1, -1, -1):
        top = jnp.concatenate([ktau[2 * dlag + 1], ktau[2 * dlag + 2]], axis=-1)
        bot = jnp.concatenate([ktau[2 * dlag], ktau[2 * dlag + 1]], axis=-1)
        units.append(jnp.concatenate([top, bot], axis=-2))
    kstack = jnp.concatenate(units, axis=-2).astype(BF16)
    rev = T - 1 - jnp.arange(T)
    m_re = lane_diag(jnp.swapaxes(w_re[rev], -1, -2))
    m_im = lane_diag(jnp.swapaxes(w_im[rev], -1, -2))
    m_all = jnp.concatenate([m_re, m_im], axis=-1)
    m_all = jnp.swapaxes(m_all, 0, 1).reshape(nq, T * LANES, 2 * SSM_GPB * SSM_STATE)
    m_hi, m_lo = _split_bf16(m_all)
    ar, ai = pw_re[1:], pw_im[1:]
    n_re = (c_re[None] * ar[:, :, None, :] - c_im[None] * ai[:, :, None, :])
    n_im = -(c_re[None] * ai[:, :, None, :] + c_im[None] * ar[:, :, None, :])

    def state_rows(n):
        n = lane_diag(n)
        return jnp.transpose(n, (1, 3, 0, 2)).reshape(nq, SSM_GPB * SSM_STATE, T * LANES)

    n_all = jnp.concatenate([state_rows(n_re), state_rows(n_im)], axis=1).astype(BF16)
    return dict(kstack=kstack, m_hi=m_hi, m_lo=m_lo, n_all=n_all,
                a16_re=pw_re[T].reshape(8, 256), a16_im=pw_im[T].reshape(8, 256),
                d=p['ssm_d'].astype(f32).reshape(1, SSM_WIDTH))


def _chunk_tokens(u_ref, rows):
    return [u_ref[pl.ds(t, rows, stride=SSM_CHUNK), :] for t in range(SSM_CHUNK)]


def _ssm_local_kernel(u_ref, mh_ref, ml_ref, hre_ref, him_ref):
    rows = hre_ref.shape[0]
    parts = [_split_bf16(ut) for ut in _chunk_tokens(u_ref, rows)]
    x_hi = jnp.concatenate([h for h, _ in parts], axis=1)
    x_lo = jnp.concatenate([l for _, l in parts], axis=1)
    h = _dot(x_hi, mh_ref[0]) + _dot(x_hi, ml_ref[0]) + _dot(x_lo, mh_ref[0])
    half = SSM_GPB * SSM_STATE
    hre_ref[...] = h[:, 0:half]
    him_ref[...] = h[:, half:2 * half]


def _ssm_local(u2d, mats, rows):
    n = u2d.shape[0]
    r = n // SSM_CHUNK
    nq = N_SSM_GROUPS // SSM_GPB
    half = SSM_GPB * SSM_STATE
    mspec = pl.BlockSpec((1, SSM_CHUNK * LANES, 2 * half), lambda q, i: (q, 0, 0))
    ospec = pl.BlockSpec((rows, half), lambda q, i: (i, q))
    return pl.pallas_call(
        _ssm_local_kernel,
        out_shape=(jax.ShapeDtypeStruct((r, N_SSM_GROUPS * SSM_STATE), F32),) * 2,
        grid=(nq, r // rows),
        in_specs=[pl.BlockSpec((rows * SSM_CHUNK, LANES), lambda q, i: (i, q)), mspec, mspec],
        out_specs=(ospec, ospec),
        compiler_params=_params(("parallel", "parallel")), name="ssm_local")(u2d, mats['m_hi'], mats['m_lo'])


def _ssm_scan_kernel(lre_ref, lim_ref, are_ref, aim_ref, h0re_ref, h0im_ref,
                     pre_ref, pim_ref, fre_ref, fim_ref):
    nchunk = lre_ref.shape[1]
    ar, ai = are_ref[...], aim_ref[...]

    def body(c, carry):
        hr, hi = carry
        pre_ref[0, c] = hr
        pim_ref[0, c] = hi
        return (ar * hr - ai * hi + lre_ref[0, c], ar * hi + ai * hr + lim_ref[0, c])

    hr, hi = lax.fori_loop(0, nchunk, body, (h0re_ref[0], h0im_ref[0]))
    fre_ref[0] = hr
    fim_ref[0] = hi


def _ssm_scan(hloc_re, hloc_im, mats, h0_re, h0_im):
    b, nchunk = hloc_re.shape[:2]
    big = pl.BlockSpec((1, nchunk, 8, 256), lambda i: (i, 0, 0, 0))
    small = pl.BlockSpec((1, 8, 256), lambda i: (i, 0, 0))
    return pl.pallas_call(
        _ssm_scan_kernel,
        out_shape=(jax.ShapeDtypeStruct(hloc_re.shape, F32),) * 2 + (jax.ShapeDtypeStruct((b, 8, 256), F32),) * 2,
        grid=(b,),
        in_specs=[big, big, _full((8, 256)), _full((8, 256)), small, small],
        out_specs=(big, big, small, small),
        compiler_params=_params(("parallel",)), name="ssm_scan")(
            hloc_re, hloc_im, mats['a16_re'], mats['a16_im'], h0_re, h0_im)


def _gelu_tanh(y):
    return 0.5 * y * (1.0 + jnp.tanh(math.sqrt(2.0 / math.pi) * (y + 0.044715 * (y * y * y))))


def _ssm_out_kernel(u_ref, k_ref, pre_ref, pim_ref, n_ref, d_ref, y_ref, ysc_ref):
    rows = pre_ref.shape[0]
    us = _chunk_tokens(u_ref, rows)
    x = jnp.concatenate([ut.astype(BF16) for ut in us], axis=1)
    hp = jnp.concatenate([pre_ref[...], pim_ref[...]], axis=1).astype(BF16)
    unit = 2 * LANES
    nunit = SSM_CHUNK // 2
    for j in range(nunit):
        yj = (_dot(x[:, 0:unit * (j + 1)], k_ref[0, unit * (nunit - 1 - j):, :])
              + _dot(hp, n_ref[0, :, unit * j:unit * (j + 1)]))
        for t2 in range(2):
            t = 2 * j + t2
            y = yj[:, t2 * LANES:(t2 + 1) * LANES] + d_ref[...] * us[t]
            ysc_ref[pl.ds(t, rows, stride=SSM_CHUNK), :] = _gelu_tanh(y)
    y_ref[...] = ysc_ref[...].astype(y_ref.dtype)


def _ssm_out(u2d, hprev_re, hprev_im, mats, rows):
    n = u2d.shape[0]
    r = n // SSM_CHUNK
    nq = N_SSM_GROUPS // SSM_GPB
    half = SSM_GPB * SSM_STATE
    uspec = pl.BlockSpec((rows * SSM_CHUNK, LANES), lambda q, i: (i, q))
    hspec = pl.BlockSpec((rows, half), lambda q, i: (i, q))
    return pl.pallas_call(
        _ssm_out_kernel,
        out_shape=jax.ShapeDtypeStruct((n, SSM_WIDTH), BF16),
        grid=(nq, r // rows),
        in_specs=[uspec, pl.BlockSpec((1, SSM_CHUNK * LANES, 2 * LANES), lambda q, i: (q, 0, 0)), hspec, hspec,
                  pl.BlockSpec((1, 2 * half, SSM_CHUNK * LANES), lambda q, i: (q, 0, 0)),
                  pl.BlockSpec((1, LANES), lambda q, i: (0, q))],
        out_specs=uspec,
        scratch_shapes=[pltpu.VMEM((rows * SSM_CHUNK, LANES), F32)],
        compiler_params=_params(("parallel", "parallel")), name="ssm_out")(
            u2d, mats['kstack'], hprev_re, hprev_im, mats['n_all'], mats['d'])


def _ssm(u2d, b, h0_re, h0_im, mats):
    n = u2d.shape[0]
    nchunk = n // b // SSM_CHUNK
    r = b * nchunk
    rows = _pick_tile(r, 256)
    hloc_re, hloc_im = _ssm_local(u2d, mats, rows)
    shp = (b, nchunk, 8, 256)
    hprev_re, hprev_im, f_re, f_im = _ssm_scan(hloc_re.reshape(shp), hloc_im.reshape(shp), mats,
                                               h0_re.reshape(b, 8, 256), h0_im.reshape(b, 8, 256))
    y = _ssm_out(u2d, hprev_re.reshape(r, -1), hprev_im.reshape(r, -1), mats, rows)
    return y, f_re.reshape(b, N_SSM_GROUPS, SSM_STATE), f_im.reshape(b, N_SSM_GROUPS, SSM_STATE)


def _merge_kernel(x_ref, of_ref, ys_ref, qm_ref, gate_ref, mk_ref, mv_ref,
                  wglu_ref, wbf_ref, wbs_ref, wbm_ref, wo_ref, nf_ref, wr_ref,
                  x1_ref, h2_ref, r_ref):
    tm = x_ref.shape[0]
    om = []
    for hd in range(N_MEM_HEADS):
        sl = slice(hd * MEM_HEAD_DIM, (hd + 1) * MEM_HEAD_DIM)
        kh = mk_ref[0, :, sl].astype(BF16)
        vh = mv_ref[0, :, sl].astype(BF16)
        sc = _dot_nt(qm_ref[:, sl], kh)
        p = jnp.exp(sc - jnp.max(sc, axis=-1, keepdims=True))
        om.append(_dot(p.astype(BF16), vh) / jnp.sum(p, axis=-1, keepdims=True))
    o_mem = jnp.concatenate(om, axis=-1).astype(BF16)
    z = _dot(ys_ref[...], wglu_ref[...])
    y_ssm = (z[:, 0:SSM_WIDTH] * jax.nn.sigmoid(z[:, SSM_WIDTH:2 * SSM_WIDTH])).astype(BF16)
    g = lambda c: gate_ref[:, c * D_MODEL:(c + 1) * D_MODEL].astype(F32)
    merged = (g(0) * _dot(of_ref[...], wbf_ref[...]) + g(1) * _dot(y_ssm, wbs_ref[...])
              + g(2) * _dot(o_mem, wbm_ref[...]))
    x1 = x_ref[...] + _dot(merged.astype(BF16), wo_ref[...])
    x1_ref[...] = x1
    h2 = x1 * lax.rsqrt(jnp.mean(x1 * x1, axis=-1, keepdims=True) + RMS_EPS) * nf_ref[...]
    h2_ref[...] = h2.astype(BF16)
    logits = _dot_exact(h2, wr_ref[...])
    lane = lax.broadcasted_iota(jnp.int32, (tm, LANES), 1)
    big = jnp.int32(LANES)
    is_grp = (lane >= N_EXPERTS) & (lane < N_EXPERTS + N_EXPERT_GROUPS)
    gl = jnp.where(is_grp, logits, NEG_INF)
    gmax = jnp.max(gl, axis=-1, keepdims=True)
    grp = jnp.min(jnp.where(is_grp & (gl == gmax), lane, big), axis=-1, keepdims=True) - N_EXPERTS
    g_w = 1.0 / jnp.sum(jnp.where(is_grp, jnp.exp(gl - gmax), 0.0), axis=-1, keepdims=True)
    in_grp = (lane >= grp * EXPERTS_PER_GROUP) & (lane < (grp + 1) * EXPERTS_PER_GROUP)
    e1 = jnp.where(in_grp, logits, NEG_INF)
    m1 = jnp.max(e1, axis=-1, keepdims=True)
    i1 = jnp.min(jnp.where(in_grp & (e1 == m1), lane, big), axis=-1, keepdims=True)
    rest = in_grp & (lane != i1)
    e2 = jnp.where(rest, logits, NEG_INF)
    m2 = jnp.max(e2, axis=-1, keepdims=True)
    i2 = jnp.min(jnp.where(rest & (e2 == m2), lane, big), axis=-1, keepdims=True)
    ex = jnp.exp(m2 - m1)
    w1 = g_w / (1.0 + ex)
    w2 = g_w * ex / (1.0 + ex)
    r_ref[...] = jnp.where(lane == i1, w1, jnp.where(lane == i2, w2, jnp.where(lane == GROUP_LANE, grp.astype(F32), 0.0)))


def _merge(x2d, o_fox, ys, q_m, gates, mem_k, mem_v, w, tm, rows_per_batch):
    n = x2d.shape[0]
    assert n % tm == 0 and rows_per_batch % tm == 0
    per = rows_per_batch // tm
    row = lambda width: pl.BlockSpec((tm, width), lambda i: (i, 0))
    memspec = pl.BlockSpec((1, N_MEM, MEM_WIDTH), lambda i: (i // per, 0, 0))
    ws = [w['w_glu'], w['w_br_fox'], w['w_br_ssm'], w['w_br_mem'], w['w_out'], w['norm_ffn'], w['w_router']]
    return pl.pallas_call(
        _merge_kernel,
        out_shape=(jax.ShapeDtypeStruct((n, D_MODEL), F32), jax.ShapeDtypeStruct((n, D_MODEL), BF16),
                   jax.ShapeDtypeStruct((n, LANES), F32)),
        grid=(n // tm,),
        in_specs=[row(D_MODEL), row(FOX_WIDTH), row(SSM_WIDTH), row(MEM_WIDTH), row(3 * D_MODEL), memspec, memspec]
                 + [_full(a.shape) for a in ws],
        out_specs=(row(D_MODEL), row(D_MODEL), row(LANES)),
        compiler_params=_params(("parallel",)), name="merge")(
            x2d, o_fox, ys, q_m, gates, mem_k, mem_v, *ws)


def _moe_kernel(h_ref, r_ref, x1_ref, tri_ref, wg_ref, wu_ref, wd_ref, o_ref,
                xs_ref, cw_ref, og_ref, acc_ref, rank_ref, nblk_ref, *, main):
    step = pl.program_id(1)
    steps_per_group = EXPERTS_PER_GROUP // MOE_EPS
    g = step // steps_per_group
    tm = h_ref.shape[0]
    gf = g.astype(F32)
    blocks = [(0, main, None)] + [(r0, MOE_SUB, r0 // MOE_SUB) for r0 in range(main, tm, MOE_SUB)]

    def guarded(need, fn):
        if need is None:
            fn()
        else:
            pl.when(need < nblk_ref[0])(fn)

    @pl.when(step == 0)
    def _():
        acc_ref[...] = jnp.zeros(acc_ref.shape, F32)

    @pl.when(step % steps_per_group == 0)
    def _():
        rt = r_ref[...]
        rtt = rt.T
        mrow = rtt[GROUP_LANE:GROUP_LANE + 1, :] == gf
        m8 = jnp.broadcast_to(jnp.where(mrow, 1.0, 0.0), (8, tm))
        rank8 = _dot(m8.astype(BF16), tri_ref[...])
        rank_row = jnp.where(mrow, rank8[0:1, :], -1.0)
        rank_ref[...] = jnp.broadcast_to(jnp.where(mrow, rank8, -1.0).T[:, 0:1], rank_ref.shape)
        count = jnp.sum(jnp.where(mrow, 1, 0))
        nblk_ref[0] = (count + MOE_SUB - 1) // MOE_SUB
        hi = rt.astype(BF16)
        rem = rt - hi.astype(F32)
        mid = rem.astype(BF16)
        lo = (rem - mid.astype(F32)).astype(BF16)
        for r0, nrows, need in blocks:
            def compact(r0=r0, nrows=nrows):
                rows = slice(r0, r0 + nrows)
                slot = r0 + lax.broadcasted_iota(jnp.int32, (nrows, tm), 0)
                perm = jnp.where(rank_row == slot.astype(F32), 1.0, 0.0).astype(BF16)
                xs_ref[rows, :] = _dot(perm, h_ref[...]).astype(BF16)
                cw_ref[rows, :] = _dot(perm, hi) + _dot(perm, mid) + _dot(perm, lo)
                og_ref[rows, :] = jnp.zeros((nrows, D_MODEL), F32)
            guarded(need, compact)

    for k in range(MOE_EPS):
        e = step * MOE_EPS + k
        for r0, nrows, need in blocks:
            def expert(r0=r0, nrows=nrows, k=k, e=e):
                rows = slice(r0, r0 + nrows)
                x = xs_ref[rows, :]
                a = _dot(x, wg_ref[k])
                up = _dot(x, wu_ref[k])
                lane = lax.broadcasted_iota(jnp.int32, (nrows, LANES), 1)
                ce = jnp.sum(jnp.where(lane == e, cw_ref[rows, :], 0.0), axis=-1, keepdims=True)
                act = a * jax.nn.sigmoid(a) * up * ce
                og_ref[rows, :] += _dot(act.astype(BF16), wd_ref[k])
            guarded(need, expert)

    @pl.when(step % steps_per_group == steps_per_group - 1)
    def _():
        for r0, nrows, need in blocks:
            def scatter_back(r0=r0, nrows=nrows):
                rows = slice(r0, r0 + nrows)
                slot = r0 + lax.broadcasted_iota(jnp.int32, (tm, nrows), 1)
                back = jnp.where(rank_ref[:, 0:1] == slot.astype(F32), 1.0, 0.0).astype(BF16)
                acc_ref[...] += _dot(back, og_ref[rows, :].astype(BF16))
            guarded(need, scatter_back)

    @pl.when(step == pl.num_programs(1) - 1)
    def _():
        o_ref[...] = x1_ref[...] + acc_ref[...]


def _moe(h2, route, x1, wg, wu, wd, tm):
    n = h2.shape[0]
    assert n % tm == 0 and tm % MOE_SUB == 0
    main = max(MOE_SUB, (3 * tm // 8) // MOE_SUB * MOE_SUB)
    row = lambda width: pl.BlockSpec((tm, width), lambda i, s: (i, 0))
    r = jnp.arange(tm)
    tri = (r[:, None] < r[None, :]).astype(BF16)
    return pl.pallas_call(
        functools.partial(_moe_kernel, main=main),
        out_shape=jax.ShapeDtypeStruct((n, D_MODEL), F32),
        grid=(n // tm, N_EXPERTS // MOE_EPS),
        in_specs=[row(D_MODEL), row(LANES), row(D_MODEL), pl.BlockSpec((tm, tm), lambda i, s: (0, 0)),
                  pl.BlockSpec((MOE_EPS, D_MODEL, D_EXPERT), lambda i, s: (s, 0, 0)),
                  pl.BlockSpec((MOE_EPS, D_MODEL, D_EXPERT), lambda i, s: (s, 0, 0)),
                  pl.BlockSpec((MOE_EPS, D_EXPERT, D_MODEL), lambda i, s: (s, 0, 0))],
        out_specs=row(D_MODEL),
        scratch_shapes=[pltpu.VMEM((tm, D_MODEL), BF16), pltpu.VMEM((tm, LANES), F32), pltpu.VMEM((tm, D_MODEL), F32),
                        pltpu.VMEM((tm, D_MODEL), F32), pltpu.VMEM((tm, LANES), F32), pltpu.SMEM((1,), jnp.int32)],
        compiler_params=_params(("parallel", "arbitrary")), name="moe")(h2, route, x1, tri, wg, wu, wd)


def _prep_weights(p):
    w_in = p['w_in'].astype(BF16)
    o = 0
    wqkv = w_in[:, 0:3 * FOX_WIDTH]
    o = 3 * FOX_WIDTH
    wf = jnp.pad(w_in[:, o:o + N_FOX_HEADS], ((0, 0), (0, LANES - N_FOX_HEADS)))
    o += N_FOX_HEADS
    wqm = w_in[:, o:o + MEM_WIDTH]
    o += MEM_WIDTH
    wu = w_in[:, o:o + SSM_WIDTH]
    o += SSM_WIDTH
    wg = w_in[:, o:o + 3 * D_MODEL]
    r = jnp.arange(FOX_WIDTH) // FOX_HEAD_DIM
    bd = (r[:, None] == r[None, :]).astype(BF16)
    w_router = jnp.concatenate(
        [p['w_router_expert'], p['w_router_group'],
         jnp.zeros((D_MODEL, LANES - N_EXPERTS - N_EXPERT_GROUPS), F32)], axis=1)
    return dict(
        norm_mix=p['norm_mix'].reshape(1, D_MODEL), wqkv=wqkv, wf=wf,
        bf=jnp.pad(p['b_forget'], (0, LANES - N_FOX_HEADS)).reshape(1, LANES),
        wqm=wqm, wu=wu, wg=wg,
        qn_fox=jnp.tile(p['qn_fox'], N_FOX_HEADS).reshape(1, FOX_WIDTH),
        kn_fox=jnp.tile(p['kn_fox'], N_FOX_HEADS).reshape(1, FOX_WIDTH),
        qn_mem=p['qn_mem'].reshape(1, MEM_HEAD_DIM), bd=bd,
        w_glu=p['w_glu'].astype(BF16), w_br_fox=p['w_br_fox'].astype(BF16),
        w_br_ssm=p['w_br_ssm'].astype(BF16), w_br_mem=p['w_br_mem'].astype(BF16),
        w_out=p['w_out'].astype(BF16), norm_ffn=p['norm_ffn'].reshape(1, D_MODEL), w_router=w_router,
        moe_wg=p['moe_w_gate'].astype(BF16), moe_wu=p['moe_w_up'].astype(BF16),
        moe_wd=p['moe_w_down'].astype(BF16))


def _pick_tile(n, target):
    t = min(n, target)
    while n % t:
        t //= 2
    return t


def _group(x, w, mats, mem_k, mem_v, h0_re, h0_im, cache):
    b, s, _ = x.shape
    n = b * s
    x2d = x.reshape(n, D_MODEL)
    prompt = cache is None
    q, kb, vb, k_out, v_out, lf_t, q_m, u, gates = _inproj(x2d, w, _pick_tile(s if prompt else n, 512), s, prompt)
    q3 = q.reshape(b, s, FOX_WIDTH)
    k3 = kb.reshape(b, s, FOX_WIDTH)
    v3 = vb.reshape(b, s, FOX_WIDTH)
    lf_rows = lf_t.reshape(N_FOX_HEADS, b, s).transpose(1, 0, 2)
    lf3 = lf_rows.transpose(0, 2, 1)
    npair = N_FOX_HEADS // 2
    if prompt:
        c_row = _cumsum_rows(lf_rows.reshape(b * N_FOX_HEADS, s)).reshape(b, npair, 2, s)
        c_col = c_row.transpose(0, 1, 3, 2)
        o_fox = _fox_prompt(q3, k3, v3, c_col, c_row, _pick_tile(s, 256), _pick_tile(s, 512))
        unt = lambda a: a.reshape(b, N_FOX_HEADS, FOX_HEAD_DIM, s).transpose(0, 3, 1, 2)
        k4, v4 = unt(k_out), unt(v_out)
    else:
        cache_k, cache_v, cache_logf = cache
        past = cache_k.shape[1]
        lf_all = jnp.concatenate([cache_logf.astype(F32).transpose(0, 2, 1), lf_rows], axis=2)
        c_row = _cumsum_rows(lf_all.reshape(b * N_FOX_HEADS, past + s)).reshape(b, N_FOX_HEADS, past + s)
        o_fox = _fox_sample(q3, cache_k.transpose(0, 2, 3, 1), cache_v.transpose(0, 2, 3, 1), k3, v3,
                            c_row[:, :, past:].transpose(0, 2, 1), c_row[:, :, :past], c_row[:, :, past:],
                            _pick_tile(past, 1024))
        k4 = k_out.reshape(b, s, N_FOX_HEADS, FOX_HEAD_DIM)
        v4 = v_out.reshape(b, s, N_FOX_HEADS, FOX_HEAD_DIM)
    ys, f_re, f_im = _ssm(u, b, h0_re, h0_im, mats)
    tm = _pick_tile(s, 512)
    x1, h2, route = _merge(x2d, o_fox.reshape(n, FOX_WIDTH), ys, q_m, gates, mem_k, mem_v, w, tm, s)
    y = _moe(h2, route, x1, w['moe_wg'], w['moe_wu'], w['moe_wd'], _pick_tile(n, 1024))
    return y.reshape(b, s, D_MODEL), k4, v4, lf3, f_re, f_im


def kernel(x_prompt, x_sample, mem_prompt, cache_fox_k, cache_fox_v, cache_fox_logf, state_ssm_re, state_ssm_im,
           cache_mem_k, cache_mem_v, norm_mix, w_in, b_forget, qn_fox, kn_fox, qn_mem, kn_mem, norm_mem, w_mem_kv,
           ssm_a_re, ssm_a_im, ssm_log_dt, ssm_b_re, ssm_b_im, ssm_c_re, ssm_c_im, ssm_d, w_glu, w_br_fox,
           w_br_ssm, w_br_mem, w_out, norm_ffn, w_router_group, w_router_expert, moe_w_gate, moe_w_up,
           moe_w_down):
    depth = norm_mix.shape[0]
    assert depth == 1
    l = 0
    p = dict(norm_mix=norm_mix[l], w_in=w_in[l], b_forget=b_forget[l], qn_fox=qn_fox[l], kn_fox=kn_fox[l],
             qn_mem=qn_mem[l], ssm_a_re=ssm_a_re[l], ssm_a_im=ssm_a_im[l], ssm_log_dt=ssm_log_dt[l],
             ssm_b_re=ssm_b_re[l], ssm_b_im=ssm_b_im[l], ssm_c_re=ssm_c_re[l], ssm_c_im=ssm_c_im[l],
             ssm_d=ssm_d[l], w_glu=w_glu[l], w_br_fox=w_br_fox[l], w_br_ssm=w_br_ssm[l], w_br_mem=w_br_mem[l],
             w_out=w_out[l], norm_ffn=norm_ffn[l], w_router_group=w_router_group[l],
             w_router_expert=w_router_expert[l], moe_w_gate=moe_w_gate[l], moe_w_up=moe_w_up[l],
             moe_w_down=moe_w_down[l])
    w = _prep_weights(p)
    mats = _ssm_mats(p)
    bp, sp, _ = x_prompt.shape
    bs, ss, _ = x_sample.shape

    mk, mv = _memkv(mem_prompt.reshape(bp * N_MEM, D_MODEL), norm_mem[l].reshape(1, D_MODEL),
                    w_mem_kv[l].astype(BF16), kn_mem[l].reshape(1, MEM_HEAD_DIM), _pick_tile(bp * N_MEM, 512))
    mk = mk.reshape(bp, N_MEM, MEM_WIDTH)
    mv = mv.reshape(bp, N_MEM, MEM_WIDTH)
    zeros = jnp.zeros((bp, N_SSM_GROUPS, SSM_STATE), F32)
    yp, pk, pv, plf, pre, pim = _group(x_prompt, w, mats, mk, mv, zeros, zeros, None)
    cache = (cache_fox_k[l], cache_fox_v[l], cache_fox_logf[l])
    ys, sk, sv, slf, sre, sim = _group(
        x_sample, w, mats, cache_mem_k[l].reshape(bs, N_MEM, MEM_WIDTH), cache_mem_v[l].reshape(bs, N_MEM, MEM_WIDTH),
        state_ssm_re[l].astype(F32), state_ssm_im[l].astype(F32), cache)
    st = lambda a: a[None]
    return (yp, ys, st(pk), st(pv), st(plf), st(pre), st(pim),
            st(mk.reshape(bp, N_MEM, N_MEM_HEADS, MEM_HEAD_DIM)), st(mv.reshape(bp, N_MEM, N_MEM_HEADS, MEM_HEAD_DIM)),
            st(sk), st(sv), st(slf), st(sre), st(sim))
```

```python
import functools
import math

import jax
import jax.numpy as jnp
from jax import lax
from jax.experimental import pallas as pl
from jax.experimental.pallas import tpu as pltpu

F32 = jnp.float32
BF16 = jnp.bfloat16

D_MODEL = 1024
N_FOX_HEADS = 8
FOX_HEAD_DIM = 64
FOX_WIDTH = N_FOX_HEADS * FOX_HEAD_DIM
N_MEM = 256
N_MEM_HEADS = 4
MEM_HEAD_DIM = 128
MEM_WIDTH = N_MEM_HEADS * MEM_HEAD_DIM
SSM_GROUP = 16
SSM_WIDTH = 512
N_SSM_GROUPS = SSM_WIDTH // SSM_GROUP
SSM_STATE = 64
N_EXPERT_GROUPS = 4
EXPERTS_PER_GROUP = 8
N_EXPERTS = N_EXPERT_GROUPS * EXPERTS_PER_GROUP
D_EXPERT = 256
RMS_EPS = 1e-6
NEG_INF = -1e30
LOG2E = 1.4426950408889634

LANES = 128
SSM_CHUNK = 16
SSM_GPB = LANES // SSM_GROUP
GROUP_LANE = N_EXPERTS
MOE_SUB = 128
MOE_EPS = 4
VMEM_LIMIT = 56 * 1024 * 1024


def _dot(a, b):
    return jnp.dot(a, b, preferred_element_type=F32)


def _dot_nt(a, b):
    return lax.dot_general(a, b, (((1,), (1,)), ((), ())), preferred_element_type=F32)


def _dot_exact(a, b):
    return jnp.dot(a, b, preferred_element_type=F32, precision=lax.Precision.HIGHEST)


def _split_bf16(x):
    hi = x.astype(BF16)
    lo = (x - hi.astype(F32)).astype(BF16)
    return hi, lo


def _params(sem):
    return pltpu.CompilerParams(dimension_semantics=sem, vmem_limit_bytes=VMEM_LIMIT)


def _full(shape):
    n = len(shape)
    return pl.BlockSpec(shape, lambda *_: (0,) * n)


def _inproj_kernel(x_ref, g_ref, wqkv_ref, wf_ref, bf_ref, wqm_ref, wu_ref, wg_ref,
                   qn_ref, kn_ref, qmn_ref, bd_ref,
                   q_ref, kb_ref, vb_ref, k_ref, v_ref, lf_ref, qm_ref, u_ref, gate_ref, *, kv_transposed):
    x = x_ref[...]
    h = x * lax.rsqrt(jnp.mean(x * x, axis=-1, keepdims=True) + RMS_EPS) * g_ref[...]
    hb = h.astype(BF16)

    def head_norm(z, gain):
        hi, lo = _split_bf16(z * z)
        ss = _dot(hi, bd_ref[...]) + _dot(lo, bd_ref[...])
        return z * lax.rsqrt(ss * (1.0 / FOX_HEAD_DIM) + RMS_EPS) * gain

    zq = _dot(hb, wqkv_ref[:, 0:FOX_WIDTH])
    q_ref[...] = (head_norm(zq, qn_ref[...]) * (LOG2E * FOX_HEAD_DIM ** -0.5)).astype(BF16)
    zk = _dot(hb, wqkv_ref[:, FOX_WIDTH:2 * FOX_WIDTH])
    kn = head_norm(zk, kn_ref[...])
    zv = _dot(hb, wqkv_ref[:, 2 * FOX_WIDTH:3 * FOX_WIDTH])
    kb_ref[...] = kn.astype(BF16)
    vb_ref[...] = zv.astype(BF16)
    tm = x_ref.shape[0]
    if kv_transposed:
        k_ref[0] = kn.T
        v_ref[0] = zv.T
    else:
        for hd in range(N_FOX_HEADS):
            hs = slice(hd * FOX_HEAD_DIM, (hd + 1) * FOX_HEAD_DIM)
            rows = pl.ds(hd, tm, stride=N_FOX_HEADS)
            k_ref[rows, :] = kn[:, hs]
            v_ref[rows, :] = zv[:, hs]

    zf = (_dot(hb, wf_ref[...]) + bf_ref[...]).T[0:N_FOX_HEADS, :]
    lf_ref[...] = jnp.minimum(zf, 0.0) - jnp.log1p(jnp.exp(-jnp.abs(zf)))

    zm = _dot(hb, wqm_ref[...])
    for hd in range(N_MEM_HEADS):
        sl = slice(hd * MEM_HEAD_DIM, (hd + 1) * MEM_HEAD_DIM)
        zh = zm[:, sl]
        ms = jnp.mean(zh * zh, axis=-1, keepdims=True)
        qm_ref[:, sl] = (zh * lax.rsqrt(ms + RMS_EPS) * qmn_ref[...] * (MEM_HEAD_DIM ** -0.5)).astype(BF16)

    u_ref[...] = _dot(hb, wu_ref[...])
    for c in range(3):
        sl = slice(c * D_MODEL, (c + 1) * D_MODEL)
        gate_ref[:, sl] = jax.nn.sigmoid(_dot(hb, wg_ref[:, sl])).astype(BF16)


def _inproj(x2d, w, tm, seq, kv_transposed):
    n = x2d.shape[0]
    assert n % tm == 0
    row = lambda width: pl.BlockSpec((tm, width), lambda i: (i, 0))
    if kv_transposed:
        assert seq % tm == 0
        per = seq // tm
        kv_shape = jax.ShapeDtypeStruct((n // seq, FOX_WIDTH, seq), F32)
        heads = pl.BlockSpec((1, FOX_WIDTH, tm), lambda i: (i // per, 0, i % per))
    else:
        kv_shape = jax.ShapeDtypeStruct((n * N_FOX_HEADS, FOX_HEAD_DIM), F32)
        heads = pl.BlockSpec((tm * N_FOX_HEADS, FOX_HEAD_DIM), lambda i: (i, 0))
    ins = [x2d, w['norm_mix'], w['wqkv'], w['wf'], w['bf'], w['wqm'], w['wu'], w['wg'],
           w['qn_fox'], w['kn_fox'], w['qn_mem'], w['bd']]
    in_specs = [row(D_MODEL)] + [_full(a.shape) for a in ins[1:]]
    out_shape = (
        jax.ShapeDtypeStruct((n, FOX_WIDTH), BF16),
        jax.ShapeDtypeStruct((n, FOX_WIDTH), BF16),
        jax.ShapeDtypeStruct((n, FOX_WIDTH), BF16),
        kv_shape,
        kv_shape,
        jax.ShapeDtypeStruct((N_FOX_HEADS, n), F32),
        jax.ShapeDtypeStruct((n, MEM_WIDTH), BF16),
        jax.ShapeDtypeStruct((n, SSM_WIDTH), F32),
        jax.ShapeDtypeStruct((n, 3 * D_MODEL), BF16),
    )
    out_specs = (row(FOX_WIDTH), row(FOX_WIDTH), row(FOX_WIDTH), heads, heads,
                 pl.BlockSpec((N_FOX_HEADS, tm), lambda i: (0, i)),
                 row(MEM_WIDTH), row(SSM_WIDTH), row(3 * D_MODEL))
    return pl.pallas_call(
        functools.partial(_inproj_kernel, kv_transposed=kv_transposed),
        out_shape=out_shape, grid=(n // tm,), in_specs=in_specs, out_specs=out_specs,
        compiler_params=_params(("parallel",)), name="inproj")(*ins)


def _memkv_kernel(x_ref, g_ref, w_ref, kn_ref, k_ref, v_ref):
    x = x_ref[...]
    h = x * lax.rsqrt(jnp.mean(x * x, axis=-1, keepdims=True) + RMS_EPS) * g_ref[...]
    hb = h.astype(BF16)
    zk = _dot(hb, w_ref[:, 0:MEM_WIDTH])
    for hd in range(N_MEM_HEADS):
        sl = slice(hd * MEM_HEAD_DIM, (hd + 1) * MEM_HEAD_DIM)
        zh = zk[:, sl]
        ms = jnp.mean(zh * zh, axis=-1, keepdims=True)
        k_ref[:, sl] = zh * lax.rsqrt(ms + RMS_EPS) * kn_ref[...]
    v_ref[...] = _dot(hb, w_ref[:, MEM_WIDTH:2 * MEM_WIDTH])


def _memkv(mem2d, norm_mem, w_kv, kn_mem, tm):
    n = mem2d.shape[0]
    row = lambda width: pl.BlockSpec((tm, width), lambda i: (i, 0))
    return pl.pallas_call(
        _memkv_kernel,
        out_shape=(jax.ShapeDtypeStruct((n, MEM_WIDTH), F32), jax.ShapeDtypeStruct((n, MEM_WIDTH), F32)),
        grid=(n // tm,),
        in_specs=[row(D_MODEL), _full(norm_mem.shape), _full(w_kv.shape), _full(kn_mem.shape)],
        out_specs=(row(MEM_WIDTH), row(MEM_WIDTH)),
        compiler_params=_params(("parallel",)), name="memkv")(mem2d, norm_mem, w_kv, kn_mem)


CUMSUM_BLOCK = 256


def _cumsum_kernel(x_ref, o_ref):
    nblk = x_ref.shape[1] // CUMSUM_BLOCK
    r = lax.broadcasted_iota(jnp.int32, (CUMSUM_BLOCK, CUMSUM_BLOCK), 0)
    c = lax.broadcasted_iota(jnp.int32, (CUMSUM_BLOCK, CUMSUM_BLOCK), 1)
    tri = (r <= c).astype(F32)
    carry = jnp.zeros((x_ref.shape[0], 1), F32)
    for j in range(nblk):
        sl = slice(j * CUMSUM_BLOCK, (j + 1) * CUMSUM_BLOCK)
        cs = _dot_exact(x_ref[:, sl], tri) + carry
        o_ref[:, sl] = cs
        carry = cs[:, CUMSUM_BLOCK - 1:CUMSUM_BLOCK]


def _cumsum_rows(x):
    rows, n = x.shape
    npad = -(-n // CUMSUM_BLOCK) * CUMSUM_BLOCK
    xp = jnp.pad(x, ((0, 0), (0, npad - n))) if npad != n else x
    out = pl.pallas_call(
        _cumsum_kernel, out_shape=jax.ShapeDtypeStruct((rows, npad), F32), grid=(1,),
        in_specs=[_full((rows, npad))], out_specs=_full((rows, npad)),
        compiler_params=_params(("arbitrary",)), name="cumsum")(xp)
    return out[:, :n] if npad != n else out


def _reduce_rows(x, op):
    rows, cols = x.shape
    if rows > 64 and rows % 64 == 0:
        x = op(x.reshape(rows // 64, 64, cols), axis=0)
        rows = 64
    if rows == 64:
        x = op(x.reshape(8, 8, cols), axis=0)
    return op(x, axis=0, keepdims=True)


def _fox_prompt_kernel(q_ref, k_ref, v_ref, cc_ref, cr_ref, o_ref,
                       vt_ref, ck0_ref, ck1_ref, st0_ref, st1_ref, pt0_ref, pt1_ref, acc_ref, *, tq, tk):
    st_refs = (st0_ref, st1_ref)
    pt_refs = (pt0_ref, pt1_ref)
    i = pl.program_id(2)
    s_len = k_ref.shape[1]

    @pl.when(i == 0)
    def _():
        vt_ref[...] = v_ref[0].astype(F32).T.astype(BF16)
        ck0_ref[...] = jnp.broadcast_to(cc_ref[0, 0, :, 0:1], (s_len, LANES))
        ck1_ref[...] = jnp.broadcast_to(cc_ref[0, 0, :, 1:2], (s_len, LANES))

    qt = q_ref[0].astype(F32).T
    row = lax.broadcasted_iota(jnp.int32, (LANES, tq), 0)
    qts = (jnp.where(row < FOX_HEAD_DIM, qt, 0.0).astype(BF16), jnp.where(row < FOX_HEAD_DIM, 0.0, qt).astype(BF16))
    q0 = pl.multiple_of(i * tq, tq)
    cq = cr_ref[0, 0, :, pl.ds(q0, tq)]
    ck_refs = (ck0_ref, ck1_ref)

    def stage_a(n, par):
        s = pl.multiple_of(n * tk, tk)
        kb = k_ref[0, pl.ds(s, tk), :]
        for hh in range(2):
            ck = ck_refs[hh][pl.ds(s, tk), :]
            st_refs[par][hh] = _dot(kb, qts[hh]) - jnp.concatenate([ck] * (tq // LANES), axis=1)

    def stage_b(n, par, stats, masked):
        if masked:
            kpos = n * tk + lax.broadcasted_iota(jnp.int32, (tk, tq), 0)
            qpos = q0 + lax.broadcasted_iota(jnp.int32, (tk, tq), 1)
            mask = kpos <= qpos
        out = []
        for hh in range(2):
            m, l = stats[2 * hh:2 * hh + 2]
            t = st_refs[par][hh]
            if masked:
                t = jnp.where(mask, t, NEG_INF)
            cqh = cq[hh:hh + 1, :]
            m_new = jnp.maximum(m, _reduce_rows(t, jnp.max) + cqh)
            alpha = jnp.exp2(m - m_new)
            p = jnp.exp2(t + (cqh - m_new))
            pt_refs[par][hh] = p.astype(BF16)
            out.extend([m_new, alpha * l + _reduce_rows(p, jnp.sum), alpha])
        return tuple(out)

    def stage_c(n, par, alphas):
        s = pl.multiple_of(jnp.maximum(n, 0) * tk, tk)
        for hh in range(2):
            vt = vt_ref[hh * FOX_HEAD_DIM:(hh + 1) * FOX_HEAD_DIM, pl.ds(s, tk)]
            acc_ref[hh] = alphas[hh] * acc_ref[hh] + _dot(vt, pt_refs[par][hh])

    def iteration(n, par, carry):
        m0, l0, al0, m1, l1, al1 = carry
        stage_c(n - 1, 1 - par, (al0, al1))
        new = stage_b(n, par, (m0, l0, m1, l1), False)
        stage_a(n + 1, 1 - par)
        return new

    def finish(par, carry):
        m0, l0, al0, m1, l1, al1 = carry
        stage_c(nfull - 1, 1 - par, (al0, al1))
        _, l0, be0, _, l1, be1 = stage_b(nfull, par, (m0, l0, m1, l1), True)
        stage_c(nfull, par, (be0, be1))
        ot = jnp.concatenate([acc_ref[0] / l0, acc_ref[1] / l1], axis=0)
        o_ref[0] = ot.T.astype(o_ref.dtype)

    acc_ref[...] = jnp.zeros(acc_ref.shape, F32)
    pt1_ref[...] = jnp.zeros(pt1_ref.shape, BF16)
    neg = jnp.full((1, tq), NEG_INF, F32)
    zero = jnp.zeros((1, tq), F32)
    one = jnp.ones((1, tq), F32)
    nfull = (i * tq) // tk
    stage_a(0, 0)
    carry = lax.fori_loop(0, nfull // 2, lambda k, c: iteration(2 * k + 1, 1, iteration(2 * k, 0, c)),
                          (neg, zero, one, neg, zero, one))
    odd = nfull % 2 == 1
    carry = lax.cond(odd, lambda c: iteration(nfull - 1, 0, c), lambda c: c, carry)
    pl.when(odd)(lambda: finish(1, carry))
    pl.when(jnp.logical_not(odd))(lambda: finish(0, carry))


def _fox_prompt(q, k, v, c_col, c_row, tq, tk):
    b, s, _ = q.shape
    assert s % tk == 0 and tk % tq == 0
    npair = N_FOX_HEADS // 2
    return pl.pallas_call(
        functools.partial(_fox_prompt_kernel, tq=tq, tk=tk),
        out_shape=jax.ShapeDtypeStruct((b, s, FOX_WIDTH), BF16),
        grid=(b, npair, s // tq),
        in_specs=[
            pl.BlockSpec((1, tq, LANES), lambda bi, hp, i: (bi, i, hp)),
            pl.BlockSpec((1, s, LANES), lambda bi, hp, i: (bi, 0, hp)),
            pl.BlockSpec((1, s, LANES), lambda bi, hp, i: (bi, 0, hp)),
            pl.BlockSpec((1, 1, s, 2), lambda bi, hp, i: (bi, hp, 0, 0)),
            pl.BlockSpec((1, 1, 2, s), lambda bi, hp, i: (bi, hp, 0, 0)),
        ],
        out_specs=pl.BlockSpec((1, tq, LANES), lambda bi, hp, i: (bi, i, hp)),
        scratch_shapes=[pltpu.VMEM((LANES, s), BF16),
                        pltpu.VMEM((s, LANES), F32), pltpu.VMEM((s, LANES), F32),
                        pltpu.VMEM((2, tk, tq), F32), pltpu.VMEM((2, tk, tq), F32),
                        pltpu.VMEM((2, tk, tq), BF16), pltpu.VMEM((2, tk, tq), BF16),
                        pltpu.VMEM((2, FOX_HEAD_DIM, tq), F32)],
        compiler_params=_params(("parallel", "parallel", "arbitrary")), name="fox_prompt")(q, k, v, c_col, c_row)


def _fox_sample_kernel(q_ref, ck_ref, cv_ref, nk_ref, nv_ref, cq_ref, crc_ref, crn_ref, o_ref, *state, n):
    j = pl.program_id(1)
    nj = pl.num_programs(1)
    m_refs = state[0:N_FOX_HEADS]
    l_refs = state[N_FOX_HEADS:2 * N_FOX_HEADS]
    acc_refs = state[2 * N_FOX_HEADS:3 * N_FOX_HEADS]

    @pl.when(j == 0)
    def _():
        for hd in range(N_FOX_HEADS):
            m_refs[hd][...] = jnp.full(m_refs[hd].shape, NEG_INF, F32)
            l_refs[hd][...] = jnp.zeros(l_refs[hd].shape, F32)
            acc_refs[hd][...] = jnp.zeros(acc_refs[hd].shape, F32)

    def update(k_of, v_of, cr_ref_, mask, transposed):
        qk = _dot if transposed else _dot_nt
        pv = _dot_nt if transposed else _dot
        ts = []
        for hd in range(N_FOX_HEADS):
            hs = slice(hd * FOX_HEAD_DIM, (hd + 1) * FOX_HEAD_DIM)
            t = qk(q_ref[0, :, hs], k_of(hd)) - cr_ref_[0, hd:hd + 1, :]
            ts.append(t if mask is None else jnp.where(mask, t, NEG_INF))
        ps = []
        for hd in range(N_FOX_HEADS):
            cq = cq_ref[0, :, hd:hd + 1]
            m = m_refs[hd][...]
            m_new = jnp.maximum(m, jnp.max(ts[hd], axis=-1, keepdims=True) + cq)
            alpha = jnp.exp2(m - m_new)
            p = jnp.exp2(ts[hd] + (cq - m_new))
            m_refs[hd][...] = m_new
            l_refs[hd][...] = alpha * l_refs[hd][...] + jnp.sum(p, axis=-1, keepdims=True)
            ps.append((alpha, p.astype(BF16)))
        for hd in range(N_FOX_HEADS):
            alpha, p = ps[hd]
            acc_refs[hd][...] = alpha * acc_refs[hd][...] + pv(p, v_of(hd))

    update(lambda hd: ck_ref[0, hd].astype(BF16), lambda hd: cv_ref[0, hd].astype(BF16), crc_ref, None, True)

    @pl.when(j == nj - 1)
    def _():
        r = lax.broadcasted_iota(jnp.int32, (n, n), 0)
        c = lax.broadcasted_iota(jnp.int32, (n, n), 1)
        head = lambda ref: (lambda hd: ref[0, :, hd * FOX_HEAD_DIM:(hd + 1) * FOX_HEAD_DIM])
        update(head(nk_ref), head(nv_ref), crn_ref, c <= r, False)
        for hd in range(N_FOX_HEADS):
            hs = slice(hd * FOX_HEAD_DIM, (hd + 1) * FOX_HEAD_DIM)
            o_ref[0, :, hs] = (acc_refs[hd][...] / l_refs[hd][...]).astype(o_ref.dtype)


def _fox_sample(q, cache_k, cache_v, k_new, v_new, c_q, c_row_cache, c_row_new, tk):
    b, n, _ = q.shape
    past = cache_k.shape[3]
    assert past % tk == 0
    cache_spec = pl.BlockSpec((1, N_FOX_HEADS, FOX_HEAD_DIM, tk), lambda bi, j: (bi, 0, 0, j))
    return pl.pallas_call(
        functools.partial(_fox_sample_kernel, n=n),
        out_shape=jax.ShapeDtypeStruct((b, n, FOX_WIDTH), BF16),
        grid=(b, past // tk),
        in_specs=[
            pl.BlockSpec((1, n, FOX_WIDTH), lambda bi, j: (bi, 0, 0)),
            cache_spec,
            cache_spec,
            pl.BlockSpec((1, n, FOX_WIDTH), lambda bi, j: (bi, 0, 0)),
            pl.BlockSpec((1, n, FOX_WIDTH), lambda bi, j: (bi, 0, 0)),
            pl.BlockSpec((1, n, N_FOX_HEADS), lambda bi, j: (bi, 0, 0)),
            pl.BlockSpec((1, N_FOX_HEADS, tk), lambda bi, j: (bi, 0, j)),
            pl.BlockSpec((1, N_FOX_HEADS, n), lambda bi, j: (bi, 0, 0)),
        ],
        out_specs=pl.BlockSpec((1, n, FOX_WIDTH), lambda bi, j: (bi, 0, 0)),
        scratch_shapes=([pltpu.VMEM((n, 1), F32)] * (2 * N_FOX_HEADS)
                        + [pltpu.VMEM((n, FOX_HEAD_DIM), F32)] * N_FOX_HEADS),
        compiler_params=_params(("parallel", "arbitrary")), name="fox_sample")(
            q, cache_k, cache_v, k_new, v_new, c_q, c_row_cache, c_row_new)


def _ssm_mats(p):
    f32 = F32
    a_re, a_im = p['ssm_a_re'].astype(f32), p['ssm_a_im'].astype(f32)
    b_re, b_im = p['ssm_b_re'].astype(f32), p['ssm_b_im'].astype(f32)
    c_re, c_im = p['ssm_c_re'].astype(f32), p['ssm_c_im'].astype(f32)
    dt = jnp.exp(p['ssm_log_dt'].astype(f32))[:, None]
    mag = jnp.exp(dt * a_re)
    ab_re = mag * jnp.cos(dt * a_im)
    ab_im = mag * jnp.sin(dt * a_im)
    den = a_re * a_re + a_im * a_im
    nr, ni = ab_re - 1.0, ab_im
    coef_re = (nr * a_re + ni * a_im) / den
    coef_im = (ni * a_re - nr * a_im) / den
    bb_re = coef_re[..., None] * b_re - coef_im[..., None] * b_im
    bb_im = coef_re[..., None] * b_im + coef_im[..., None] * b_re
    pr, pi = [jnp.ones_like(ab_re)], [jnp.zeros_like(ab_im)]
    for _ in range(SSM_CHUNK):
        pr.append(pr[-1] * ab_re - pi[-1] * ab_im)
        pi.append(pr[-2] * ab_im + pi[-1] * ab_re)
    pw_re, pw_im = jnp.stack(pr), jnp.stack(pi)
    T = SSM_CHUNK
    w_re = pw_re[..., None] * bb_re[None] - pw_im[..., None] * bb_im[None]
    w_im = pw_re[..., None] * bb_im[None] + pw_im[..., None] * bb_re[None]
    kk = (jnp.einsum('gop,kgpi->kgoi', c_re, w_re[:T], precision='highest')
          - jnp.einsum('gop,kgpi->kgoi', c_im, w_im[:T], precision='highest'))
    nq = N_SSM_GROUPS // SSM_GPB

    def lane_diag(m):
        lead = m.shape[:-3]
        i, c = m.shape[-2:]
        m = jnp.tile(m.reshape(lead + (nq, SSM_GPB * i, c)), (1,) * (len(lead) + 2) + (SSM_GPB,))
        same = (jnp.arange(SSM_GPB * i) // i)[:, None] == (jnp.arange(SSM_GPB * c) // c)[None, :]
        return jnp.where(same, m, 0.0)

    ktau = lane_diag(jnp.swapaxes(kk, -1, -2))
    ktau = jnp.concatenate([jnp.zeros_like(ktau[:1]), ktau], axis=0)
    units = []
    for dlag in range(T // 2 - 1, -1, -1):
        top = jnp.concatenate([ktau[2 * dlag + 1], ktau[2 * dlag + 2]], axis=-1)
        bot = jnp.concatenate([ktau[2 * dlag], ktau[2 * dlag + 1]], axis=-1)
        units.append(jnp.concatenate([top, bot], axis=-2))
    kstack = jnp.concatenate(units, axis=-2).astype(BF16)
    rev = T - 1 - jnp.arange(T)
    m_re = lane_diag(jnp.swapaxes(w_re[rev], -1, -2))
    m_im = lane_diag(jnp.swapaxes(w_im[rev], -1, -2))
    m_all = jnp.concatenate([m_re, m_im], axis=-1)
    m_all = jnp.swapaxes(m_all, 0, 1).reshape(nq, T * LANES, 2 * SSM_GPB * SSM_STATE)
    m_hi, m_lo = _split_bf16(m_all)
    ar, ai = pw_re[1:], pw_im[1:]
    n_re = (c_re[None] * ar[:, :, None, :] - c_im[None] * ai[:, :, None, :])
    n_im = -(c_re[None] * ai[:, :, None, :] + c_im[None] * ar[:, :, None, :])

    def state_rows(n):
        n = jnp.transpose(n, (1, 3, 0, 2)).reshape(nq, SSM_GPB * SSM_STATE, T, SSM_GROUP)
        n = jnp.tile(n, (1, 1, 1, SSM_GPB))
        same = (jnp.arange(SSM_GPB * SSM_STATE) // SSM_STATE)[:, None, None] == (jnp.arange(LANES) // SSM_GROUP)
        return jnp.where(same, n, 0.0).reshape(nq, SSM_GPB * SSM_STATE, T * LANES)

    n_all = jnp.concatenate([state_rows(n_re), state_rows(n_im)], axis=1).astype(BF16)
    return dict(kstack=kstack, m_hi=m_hi, m_lo=m_lo, n_all=n_all,
                a16_re=pw_re[T].reshape(8, 256), a16_im=pw_im[T].reshape(8, 256),
                d=p['ssm_d'].astype(f32).reshape(1, SSM_WIDTH))


def _chunk_tokens(u_ref, rows):
    return [u_ref[pl.ds(t, rows, stride=SSM_CHUNK), :] for t in range(SSM_CHUNK)]


def _ssm_local_kernel(u_ref, mh_ref, ml_ref, hre_ref, him_ref):
    rows = hre_ref.shape[0]
    parts = [_split_bf16(ut) for ut in _chunk_tokens(u_ref, rows)]
    x_hi = jnp.concatenate([h for h, _ in parts], axis=1)
    x_lo = jnp.concatenate([l for _, l in parts], axis=1)
    h = _dot(x_hi, mh_ref[0]) + _dot(x_hi, ml_ref[0]) + _dot(x_lo, mh_ref[0])
    half = SSM_GPB * SSM_STATE
    hre_ref[...] = h[:, 0:half]
    him_ref[...] = h[:, half:2 * half]


def _ssm_local(u2d, mats, rows):
    n = u2d.shape[0]
    r = n // SSM_CHUNK
    nq = N_SSM_GROUPS // SSM_GPB
    half = SSM_GPB * SSM_STATE
    mspec = pl.BlockSpec((1, SSM_CHUNK * LANES, 2 * half), lambda q, i: (q, 0, 0))
    ospec = pl.BlockSpec((rows, half), lambda q, i: (i, q))
    return pl.pallas_call(
        _ssm_local_kernel,
        out_shape=(jax.ShapeDtypeStruct((r, N_SSM_GROUPS * SSM_STATE), F32),) * 2,
        grid=(nq, r // rows),
        in_specs=[pl.BlockSpec((rows * SSM_CHUNK, LANES), lambda q, i: (i, q)), mspec, mspec],
        out_specs=(ospec, ospec),
        compiler_params=_params(("parallel", "parallel")), name="ssm_local")(u2d, mats['m_hi'], mats['m_lo'])


def _ssm_scan_kernel(lre_ref, lim_ref, are_ref, aim_ref, h0re_ref, h0im_ref,
                     pre_ref, pim_ref, fre_ref, fim_ref):
    nchunk = lre_ref.shape[1]
    ar, ai = are_ref[...], aim_ref[...]

    def body(c, carry):
        hr, hi = carry
        pre_ref[0, c] = hr
        pim_ref[0, c] = hi
        return (ar * hr - ai * hi + lre_ref[0, c], ar * hi + ai * hr + lim_ref[0, c])

    hr, hi = lax.fori_loop(0, nchunk, body, (h0re_ref[0], h0im_ref[0]))
    fre_ref[0] = hr
    fim_ref[0] = hi


def _ssm_scan(hloc_re, hloc_im, mats, h0_re, h0_im):
    b, nchunk = hloc_re.shape[:2]
    big = pl.BlockSpec((1, nchunk, 8, 256), lambda i: (i, 0, 0, 0))
    small = pl.BlockSpec((1, 8, 256), lambda i: (i, 0, 0))
    return pl.pallas_call(
        _ssm_scan_kernel,
        out_shape=(jax.ShapeDtypeStruct(hloc_re.shape, F32),) * 2 + (jax.ShapeDtypeStruct((b, 8, 256), F32),) * 2,
        grid=(b,),
        in_specs=[big, big, _full((8, 256)), _full((8, 256)), small, small],
        out_specs=(big, big, small, small),
        compiler_params=_params(("parallel",)), name="ssm_scan")(
            hloc_re, hloc_im, mats['a16_re'], mats['a16_im'], h0_re, h0_im)


def _gelu_tanh(y):
    return 0.5 * y * (1.0 + jnp.tanh(math.sqrt(2.0 / math.pi) * (y + 0.044715 * (y * y * y))))


def _ssm_out_kernel(u_ref, k_ref, pre_ref, pim_ref, n_ref, d_ref, y_ref, ysc_ref):
    rows = pre_ref.shape[0]
    us = _chunk_tokens(u_ref, rows)
    x = jnp.concatenate([ut.astype(BF16) for ut in us], axis=1)
    hp = jnp.concatenate([pre_ref[...], pim_ref[...]], axis=1).astype(BF16)
    unit = 2 * LANES
    nunit = SSM_CHUNK // 2
    for j in range(nunit):
        yj = (_dot(x[:, 0:unit * (j + 1)], k_ref[0, unit * (nunit - 1 - j):, :])
              + _dot(hp, n_ref[0, :, unit * j:unit * (j + 1)]))
        for t2 in range(2):
            t = 2 * j + t2
            y = yj[:, t2 * LANES:(t2 + 1) * LANES] + d_ref[...] * us[t]
            ysc_ref[pl.ds(t, rows, stride=SSM_CHUNK), :] = _gelu_tanh(y)
    y_ref[...] = ysc_ref[...].astype(y_ref.dtype)


def _ssm_out(u2d, hprev_re, hprev_im, mats, rows):
    n = u2d.shape[0]
    r = n // SSM_CHUNK
    nq = N_SSM_GROUPS // SSM_GPB
    half = SSM_GPB * SSM_STATE
    uspec = pl.BlockSpec((rows * SSM_CHUNK, LANES), lambda q, i: (i, q))
    hspec = pl.BlockSpec((rows, half), lambda q, i: (i, q))
    return pl.pallas_call(
        _ssm_out_kernel,
        out_shape=jax.ShapeDtypeStruct((n, SSM_WIDTH), BF16),
        grid=(nq, r // rows),
        in_specs=[uspec, pl.BlockSpec((1, SSM_CHUNK * LANES, 2 * LANES), lambda q, i: (q, 0, 0)), hspec, hspec,
                  pl.BlockSpec((1, 2 * half, SSM_CHUNK * LANES), lambda q, i: (q, 0, 0)),
                  pl.BlockSpec((1, LANES), lambda q, i: (0, q))],
        out_specs=uspec,
        scratch_shapes=[pltpu.VMEM((rows * SSM_CHUNK, LANES), F32)],
        compiler_params=_params(("parallel", "parallel")), name="ssm_out")(
            u2d, mats['kstack'], hprev_re, hprev_im, mats['n_all'], mats['d'])


def _ssm(u2d, b, h0_re, h0_im, mats):
    n = u2d.shape[0]
    nchunk = n // b // SSM_CHUNK
    r = b * nchunk
    rows = _pick_tile(r, 256)
    hloc_re, hloc_im = _ssm_local(u2d, mats, rows)
    shp = (b, nchunk, 8, 256)
    hprev_re, hprev_im, f_re, f_im = _ssm_scan(hloc_re.reshape(shp), hloc_im.reshape(shp), mats,
                                               h0_re.reshape(b, 8, 256), h0_im.reshape(b, 8, 256))
    y = _ssm_out(u2d, hprev_re.reshape(r, -1), hprev_im.reshape(r, -1), mats, rows)
    return y, f_re.reshape(b, N_SSM_GROUPS, SSM_STATE), f_im.reshape(b, N_SSM_GROUPS, SSM_STATE)


def _merge_kernel(x_ref, of_ref, ys_ref, qm_ref, gate_ref, mk_ref, mv_ref,
                  wglu_ref, wbf_ref, wbs_ref, wbm_ref, wo_ref, nf_ref, wr_ref,
                  x1_ref, h2_ref, r_ref):
    tm = x_ref.shape[0]
    om = []
    for hd in range(N_MEM_HEADS):
        sl = slice(hd * MEM_HEAD_DIM, (hd + 1) * MEM_HEAD_DIM)
        kh = mk_ref[0, :, sl].astype(BF16)
        vh = mv_ref[0, :, sl].astype(BF16)
        sc = _dot_nt(qm_ref[:, sl], kh)
        p = jnp.exp(sc - jnp.max(sc, axis=-1, keepdims=True))
        om.append(_dot(p.astype(BF16), vh) / jnp.sum(p, axis=-1, keepdims=True))
    o_mem = jnp.concatenate(om, axis=-1).astype(BF16)
    z = _dot(ys_ref[...], wglu_ref[...])
    y_ssm = (z[:, 0:SSM_WIDTH] * jax.nn.sigmoid(z[:, SSM_WIDTH:2 * SSM_WIDTH])).astype(BF16)
    g = lambda c: gate_ref[:, c * D_MODEL:(c + 1) * D_MODEL].astype(F32)
    merged = (g(0) * _dot(of_ref[...], wbf_ref[...]) + g(1) * _dot(y_ssm, wbs_ref[...])
              + g(2) * _dot(o_mem, wbm_ref[...]))
    x1 = x_ref[...] + _dot(merged.astype(BF16), wo_ref[...])
    x1_ref[...] = x1
    h2 = x1 * lax.rsqrt(jnp.mean(x1 * x1, axis=-1, keepdims=True) + RMS_EPS) * nf_ref[...]
    h2_ref[...] = h2.astype(BF16)
    logits = _dot_exact(h2, wr_ref[...])
    lane = lax.broadcasted_iota(jnp.int32, (tm, LANES), 1)
    big = jnp.int32(LANES)
    is_grp = (lane >= N_EXPERTS) & (lane < N_EXPERTS + N_EXPERT_GROUPS)
    gl = jnp.where(is_grp, logits, NEG_INF)
    gmax = jnp.max(gl, axis=-1, keepdims=True)
    grp = jnp.min(jnp.where(is_grp & (gl == gmax), lane, big), axis=-1, keepdims=True) - N_EXPERTS
    g_w = 1.0 / jnp.sum(jnp.where(is_grp, jnp.exp(gl - gmax), 0.0), axis=-1, keepdims=True)
    in_grp = (lane >= grp * EXPERTS_PER_GROUP) & (lane < (grp + 1) * EXPERTS_PER_GROUP)
    e1 = jnp.where(in_grp, logits, NEG_INF)
    m1 = jnp.max(e1, axis=-1, keepdims=True)
    i1 = jnp.min(jnp.where(in_grp & (e1 == m1), lane, big), axis=-1, keepdims=True)
    rest = in_grp & (lane != i1)
    e2 = jnp.where(rest, logits, NEG_INF)
    m2 = jnp.max(e2, axis=-1, keepdims=True)
    i2 = jnp.min(jnp.where(rest & (e2 == m2), lane, big), axis=-1, keepdims=True)
    ex = jnp.exp(m2 - m1)
    w1 = g_w / (1.0 + ex)
    w2 = g_w * ex / (1.0 + ex)
    r_ref[...] = jnp.where(lane == i1, w1, jnp.where(lane == i2, w2, jnp.where(lane == GROUP_LANE, grp.astype(F32), 0.0)))


def _merge(x2d, o_fox, ys, q_m, gates, mem_k, mem_v, w, tm, rows_per_batch):
    n = x2d.shape[0]
    assert n % tm == 0 and rows_per_batch % tm == 0
    per = rows_per_batch // tm
    row = lambda width: pl.BlockSpec((tm, width), lambda i: (i, 0))
    memspec = pl.BlockSpec((1, N_MEM, MEM_WIDTH), lambda i: (i // per, 0, 0))
    ws = [w['w_glu'], w['w_br_fox'], w['w_br_ssm'], w['w_br_mem'], w['w_out'], w['norm_ffn'], w['w_router']]
    return pl.pallas_call(
        _merge_kernel,
        out_shape=(jax.ShapeDtypeStruct((n, D_MODEL), F32), jax.ShapeDtypeStruct((n, D_MODEL), BF16),
                   jax.ShapeDtypeStruct((n, LANES), F32)),
        grid=(n // tm,),
        in_specs=[row(D_MODEL), row(FOX_WIDTH), row(SSM_WIDTH), row(MEM_WIDTH), row(3 * D_MODEL), memspec, memspec]
                 + [_full(a.shape) for a in ws],
        out_specs=(row(D_MODEL), row(D_MODEL), row(LANES)),
        compiler_params=_params(("parallel",)), name="merge")(
            x2d, o_fox, ys, q_m, gates, mem_k, mem_v, *ws)


def _moe_kernel(h_ref, r_ref, x1_ref, tri_ref, wg_ref, wu_ref, wd_ref, o_ref,
                xs_ref, cw_ref, og_ref, acc_ref, rank_ref, nblk_ref, *, main):
    step = pl.program_id(1)
    steps_per_group = EXPERTS_PER_GROUP // MOE_EPS
    g = step // steps_per_group
    tm = h_ref.shape[0]
    gf = g.astype(F32)
    blocks = [(0, main, None)] + [(r0, MOE_SUB, r0 // MOE_SUB) for r0 in range(main, tm, MOE_SUB)]

    def guarded(need, fn):
        if need is None:
            fn()
        else:
            pl.when(need < nblk_ref[0])(fn)

    @pl.when(step == 0)
    def _():
        acc_ref[...] = jnp.zeros(acc_ref.shape, F32)

    @pl.when(step % steps_per_group == 0)
    def _():
        rt = r_ref[...]
        rtt = rt.T
        mrow = rtt[GROUP_LANE:GROUP_LANE + 1, :] == gf
        m8 = jnp.broadcast_to(jnp.where(mrow, 1.0, 0.0), (8, tm))
        rank8 = _dot(m8.astype(BF16), tri_ref[...])
        rank_row = jnp.where(mrow, rank8[0:1, :], -1.0)
        rank_ref[...] = jnp.broadcast_to(jnp.where(mrow, rank8, -1.0).T[:, 0:1], rank_ref.shape)
        count = jnp.sum(jnp.where(mrow, 1, 0))
        nblk_ref[0] = (count + MOE_SUB - 1) // MOE_SUB
        hi = rt.astype(BF16)
        rem = rt - hi.astype(F32)
        mid = rem.astype(BF16)
        lo = (rem - mid.astype(F32)).astype(BF16)
        for r0, nrows, need in blocks:
            def compact(r0=r0, nrows=nrows):
                rows = slice(r0, r0 + nrows)
                slot = r0 + lax.broadcasted_iota(jnp.int32, (nrows, tm), 0)
                perm = jnp.where(rank_row == slot.astype(F32), 1.0, 0.0).astype(BF16)
                xs_ref[rows, :] = _dot(perm, h_ref[...]).astype(BF16)
                cw_ref[rows, :] = _dot(perm, hi) + _dot(perm, mid) + _dot(perm, lo)
                og_ref[rows, :] = jnp.zeros((nrows, D_MODEL), F32)
            guarded(need, compact)

    for k in range(MOE_EPS):
        e = step * MOE_EPS + k
        for r0, nrows, need in blocks:
            def expert(r0=r0, nrows=nrows, k=k, e=e):
                rows = slice(r0, r0 + nrows)
                x = xs_ref[rows, :]
                a = _dot(x, wg_ref[k])
                up = _dot(x, wu_ref[k])
                lane = lax.broadcasted_iota(jnp.int32, (nrows, LANES), 1)
                ce = jnp.sum(jnp.where(lane == e, cw_ref[rows, :], 0.0), axis=-1, keepdims=True)
                act = a * jax.nn.sigmoid(a) * up * ce
                og_ref[rows, :] += _dot(act.astype(BF16), wd_ref[k])
            guarded(need, expert)

    @pl.when(step % steps_per_group == steps_per_group - 1)
    def _():
        for r0, nrows, need in blocks:
            def scatter_back(r0=r0, nrows=nrows):
                rows = slice(r0, r0 + nrows)
                slot = r0 + lax.broadcasted_iota(jnp.int32, (tm, nrows), 1)
                back = jnp.where(rank_ref[:, 0:1] == slot.astype(F32), 1.0, 0.0).astype(BF16)
                acc_ref[...] += _dot(back, og_ref[rows, :].astype(BF16))
            guarded(need, scatter_back)

    @pl.when(step == pl.num_programs(1) - 1)
    def _():
        o_ref[...] = x1_ref[...] + acc_ref[...]


def _moe(h2, route, x1, wg, wu, wd, tm):
    n = h2.shape[0]
    assert n % tm == 0 and tm % MOE_SUB == 0
    main = max(MOE_SUB, (3 * tm // 8) // MOE_SUB * MOE_SUB)
    row = lambda width: pl.BlockSpec((tm, width), lambda i, s: (i, 0))
    r = jnp.arange(tm)
    tri = (r[:, None] < r[None, :]).astype(BF16)
    return pl.pallas_call(
        functools.partial(_moe_kernel, main=main),
        out_shape=jax.ShapeDtypeStruct((n, D_MODEL), F32),
        grid=(n // tm, N_EXPERTS // MOE_EPS),
        in_specs=[row(D_MODEL), row(LANES), row(D_MODEL), pl.BlockSpec((tm, tm), lambda i, s: (0, 0)),
                  pl.BlockSpec((MOE_EPS, D_MODEL, D_EXPERT), lambda i, s: (s, 0, 0)),
                  pl.BlockSpec((MOE_EPS, D_MODEL, D_EXPERT), lambda i, s: (s, 0, 0)),
                  pl.BlockSpec((MOE_EPS, D_EXPERT, D_MODEL), lambda i, s: (s, 0, 0))],
        out_specs=row(D_MODEL),
        scratch_shapes=[pltpu.VMEM((tm, D_MODEL), BF16), pltpu.VMEM((tm, LANES), F32), pltpu.VMEM((tm, D_MODEL), F32),
                        pltpu.VMEM((tm, D_MODEL), F32), pltpu.VMEM((tm, LANES), F32), pltpu.SMEM((1,), jnp.int32)],
        compiler_params=_params(("parallel", "arbitrary")), name="moe")(h2, route, x1, tri, wg, wu, wd)


def _prep_weights(p):
    w_in = p['w_in'].astype(BF16)
    o = 0
    wqkv = w_in[:, 0:3 * FOX_WIDTH]
    o = 3 * FOX_WIDTH
    wf = jnp.pad(w_in[:, o:o + N_FOX_HEADS], ((0, 0), (0, LANES - N_FOX_HEADS)))
    o += N_FOX_HEADS
    wqm = w_in[:, o:o + MEM_WIDTH]
    o += MEM_WIDTH
    wu = w_in[:, o:o + SSM_WIDTH]
    o += SSM_WIDTH
    wg = w_in[:, o:o + 3 * D_MODEL]
    r = jnp.arange(FOX_WIDTH) // FOX_HEAD_DIM
    bd = (r[:, None] == r[None, :]).astype(BF16)
    w_router = jnp.concatenate(
        [p['w_router_expert'], p['w_router_group'],
         jnp.zeros((D_MODEL, LANES - N_EXPERTS - N_EXPERT_GROUPS), F32)], axis=1)
    return dict(
        norm_mix=p['norm_mix'].reshape(1, D_MODEL), wqkv=wqkv, wf=wf,
        bf=jnp.pad(p['b_forget'], (0, LANES - N_FOX_HEADS)).reshape(1, LANES),
        wqm=wqm, wu=wu, wg=wg,
        qn_fox=jnp.tile(p['qn_fox'], N_FOX_HEADS).reshape(1, FOX_WIDTH),
        kn_fox=jnp.tile(p['kn_fox'], N_FOX_HEADS).reshape(1, FOX_WIDTH),
        qn_mem=p['qn_mem'].reshape(1, MEM_HEAD_DIM), bd=bd,
        w_glu=p['w_glu'].astype(BF16), w_br_fox=p['w_br_fox'].astype(BF16),
        w_br_ssm=p['w_br_ssm'].astype(BF16), w_br_mem=p['w_br_mem'].astype(BF16),
        w_out=p['w_out'].astype(BF16), norm_ffn=p['norm_ffn'].reshape(1, D_MODEL), w_router=w_router,
        moe_wg=p['moe_w_gate'].astype(BF16), moe_wu=p['moe_w_up'].astype(BF16),
        moe_wd=p['moe_w_down'].astype(BF16))


def _pick_tile(n, target):
    t = min(n, target)
    while n % t:
        t //= 2
    return t


def _group(x, w, mats, mem_k, mem_v, h0_re, h0_im, cache):
    b, s, _ = x.shape
    n = b * s
    x2d = x.reshape(n, D_MODEL)
    prompt = cache is None
    q, kb, vb, k_out, v_out, lf_t, q_m, u, gates = _inproj(x2d, w, _pick_tile(s if prompt else n, 512), s, prompt)
    q3 = q.reshape(b, s, FOX_WIDTH)
    k3 = kb.reshape(b, s, FOX_WIDTH)
    v3 = vb.reshape(b, s, FOX_WIDTH)
    lf_rows = lf_t.reshape(N_FOX_HEADS, b, s).transpose(1, 0, 2)
    lf3 = lf_rows.transpose(0, 2, 1)
    npair = N_FOX_HEADS // 2
    if prompt:
        c_row = LOG2E * _cumsum_rows(lf_rows.reshape(b * N_FOX_HEADS, s)).reshape(b, npair, 2, s)
        c_col = c_row.transpose(0, 1, 3, 2)
        o_fox = _fox_prompt(q3, k3, v3, c_col, c_row, _pick_tile(s, 512), _pick_tile(s, 512))
        unt = lambda a: a.reshape(b, N_FOX_HEADS, FOX_HEAD_DIM, s).transpose(0, 3, 1, 2)
        k4, v4 = unt(k_out), unt(v_out)
    else:
        cache_k, cache_v, cache_logf = cache
        past = cache_k.shape[1]
        lf_all = jnp.concatenate([cache_logf.astype(F32).transpose(0, 2, 1), lf_rows], axis=2)
        c_row = LOG2E * _cumsum_rows(lf_all.reshape(b * N_FOX_HEADS, past + s)).reshape(b, N_FOX_HEADS, past + s)
        o_fox = _fox_sample(q3, cache_k.transpose(0, 2, 3, 1), cache_v.transpose(0, 2, 3, 1), k3, v3,
                            c_row[:, :, past:].transpose(0, 2, 1), c_row[:, :, :past], c_row[:, :, past:],
                            _pick_tile(past, 1024))
        k4 = k_out.reshape(b, s, N_FOX_HEADS, FOX_HEAD_DIM)
        v4 = v_out.reshape(b, s, N_FOX_HEADS, FOX_HEAD_DIM)
    ys, f_re, f_im = _ssm(u, b, h0_re, h0_im, mats)
    tm = _pick_tile(s, 512)
    x1, h2, route = _merge(x2d, o_fox.reshape(n, FOX_WIDTH), ys, q_m, gates, mem_k, mem_v, w, tm, s)
    y = _moe(h2, route, x1, w['moe_wg'], w['moe_wu'], w['moe_wd'], _pick_tile(n, 1024))
    return y.reshape(b, s, D_MODEL), k4, v4, lf3, f_re, f_im


def kernel(x_prompt, x_sample, mem_prompt, cache_fox_k, cache_fox_v, cache_fox_logf, state_ssm_re, state_ssm_im,
           cache_mem_k, cache_mem_v, norm_mix, w_in, b_forget, qn_fox, kn_fox, qn_mem, kn_mem, norm_mem, w_mem_kv,
           ssm_a_re, ssm_a_im, ssm_log_dt, ssm_b_re, ssm_b_im, ssm_c_re, ssm_c_im, ssm_d, w_glu, w_br_fox,
           w_br_ssm, w_br_mem, w_out, norm_ffn, w_router_group, w_router_expert, moe_w_gate, moe_w_up,
           moe_w_down):
    depth = norm_mix.shape[0]
    assert depth == 1
    l = 0
    p = dict(norm_mix=norm_mix[l], w_in=w_in[l], b_forget=b_forget[l], qn_fox=qn_fox[l], kn_fox=kn_fox[l],
             qn_mem=qn_mem[l], ssm_a_re=ssm_a_re[l], ssm_a_im=ssm_a_im[l], ssm_log_dt=ssm_log_dt[l],
             ssm_b_re=ssm_b_re[l], ssm_b_im=ssm_b_im[l], ssm_c_re=ssm_c_re[l], ssm_c_im=ssm_c_im[l],
             ssm_d=ssm_d[l], w_glu=w_glu[l], w_br_fox=w_br_fox[l], w_br_ssm=w_br_ssm[l], w_br_mem=w_br_mem[l],
             w_out=w_out[l], norm_ffn=norm_ffn[l], w_router_group=w_router_group[l],
             w_router_expert=w_router_expert[l], moe_w_gate=moe_w_gate[l], moe_w_up=moe_w_up[l],
             moe_w_down=moe_w_down[l])
    w = _prep_weights(p)
    mats = _ssm_mats(p)
    bp, sp, _ = x_prompt.shape
    bs, ss, _ = x_sample.shape

    mk, mv = _memkv(mem_prompt.reshape(bp * N_MEM, D_MODEL), norm_mem[l].reshape(1, D_MODEL),
                    w_mem_kv[l].astype(BF16), kn_mem[l].reshape(1, MEM_HEAD_DIM), _pick_tile(bp * N_MEM, 512))
    mk = mk.reshape(bp, N_MEM, MEM_WIDTH)
    mv = mv.reshape(bp, N_MEM, MEM_WIDTH)
    zeros = jnp.zeros((bp, N_SSM_GROUPS, SSM_STATE), F32)
    yp, pk, pv, plf, pre, pim = _group(x_prompt, w, mats, mk, mv, zeros, zeros, None)
    cache = (cache_fox_k[l], cache_fox_v[l], cache_fox_logf[l])
    ys, sk, sv, slf, sre, sim = _group(
        x_sample, w, mats, cache_mem_k[l].reshape(bs, N_MEM, MEM_WIDTH), cache_mem_v[l].reshape(bs, N_MEM, MEM_WIDTH),
        state_ssm_re[l].astype(F32), state_ssm_im[l].astype(F32), cache)
    st = lambda a: a[None]
    return (yp, ys, st(pk), st(pv), st(plf), st(pre), st(pim),
            st(mk.reshape(bp, N_MEM, N_MEM_HEADS, MEM_HEAD_DIM)), st(mv.reshape(bp, N_MEM, N_MEM_HEADS, MEM_HEAD_DIM)),
            st(sk), st(sv), st(slf), st(sre), st(sim))
```

```python
import functools
import math

import jax
import jax.numpy as jnp
from jax import lax
from jax.experimental import pallas as pl
from jax.experimental.pallas import tpu as pltpu

F32 = jnp.float32
BF16 = jnp.bfloat16

D_MODEL = 1024
N_FOX_HEADS = 8
FOX_HEAD_DIM = 64
FOX_WIDTH = N_FOX_HEADS * FOX_HEAD_DIM
N_MEM = 256
N_MEM_HEADS = 4
MEM_HEAD_DIM = 128
MEM_WIDTH = N_MEM_HEADS * MEM_HEAD_DIM
SSM_GROUP = 16
SSM_WIDTH = 512
N_SSM_GROUPS = SSM_WIDTH // SSM_GROUP
SSM_STATE = 64
N_EXPERT_GROUPS = 4
EXPERTS_PER_GROUP = 8
N_EXPERTS = N_EXPERT_GROUPS * EXPERTS_PER_GROUP
D_EXPERT = 256
RMS_EPS = 1e-6
NEG_INF = -1e30
LOG2E = 1.4426950408889634

LANES = 128
SSM_CHUNK = 16
SSM_GPB = LANES // SSM_GROUP
GROUP_LANE = N_EXPERTS
MOE_SUB = 128
MOE_EPS = 4
VMEM_LIMIT = 56 * 1024 * 1024


def _dot(a, b):
    return jnp.dot(a, b, preferred_element_type=F32)


def _dot_nt(a, b):
    return lax.dot_general(a, b, (((1,), (1,)), ((), ())), preferred_element_type=F32)


def _dot_exact(a, b):
    return jnp.dot(a, b, preferred_element_type=F32, precision=lax.Precision.HIGHEST)


def _split_bf16(x):
    hi = x.astype(BF16)
    lo = (x - hi.astype(F32)).astype(BF16)
    return hi, lo


def _params(sem):
    return pltpu.CompilerParams(dimension_semantics=sem, vmem_limit_bytes=VMEM_LIMIT)


def _full(shape):
    n = len(shape)
    return pl.BlockSpec(shape, lambda *_: (0,) * n)


def _inproj_kernel(x_ref, g_ref, wqkv_ref, wf_ref, bf_ref, wqm_ref, wu_ref, wg_ref,
                   qn_ref, kn_ref, qmn_ref, bd_ref,
                   q_ref, kb_ref, vb_ref, k_ref, v_ref, lf_ref, qm_ref, u_ref, gate_ref, *, kv_transposed):
    x = x_ref[...]
    h = x * lax.rsqrt(jnp.mean(x * x, axis=-1, keepdims=True) + RMS_EPS) * g_ref[...]
    hb = h.astype(BF16)

    def head_norm(z, gain):
        ss = _dot((z * z).astype(BF16), bd_ref[...])
        return z * lax.rsqrt(ss * (1.0 / FOX_HEAD_DIM) + RMS_EPS) * gain

    zq = _dot(hb, wqkv_ref[:, 0:FOX_WIDTH])
    q_ref[...] = (head_norm(zq, qn_ref[...]) * (LOG2E * FOX_HEAD_DIM ** -0.5)).astype(BF16)
    zk = _dot(hb, wqkv_ref[:, FOX_WIDTH:2 * FOX_WIDTH])
    kn = head_norm(zk, kn_ref[...])
    zv = _dot(hb, wqkv_ref[:, 2 * FOX_WIDTH:3 * FOX_WIDTH])
    kb_ref[...] = kn.astype(BF16)
    vb_ref[...] = zv.astype(BF16)
    tm = x_ref.shape[0]
    if kv_transposed:
        k_ref[0] = kn.T
        v_ref[0] = zv.T
    else:
        for hd in range(N_FOX_HEADS):
            hs = slice(hd * FOX_HEAD_DIM, (hd + 1) * FOX_HEAD_DIM)
            rows = pl.ds(hd, tm, stride=N_FOX_HEADS)
            k_ref[rows, :] = kn[:, hs]
            v_ref[rows, :] = zv[:, hs]

    zf = (_dot(hb, wf_ref[...]) + bf_ref[...]).T[0:N_FOX_HEADS, :]
    lf_ref[...] = jnp.minimum(zf, 0.0) - jnp.log1p(jnp.exp(-jnp.abs(zf)))

    zm = _dot(hb, wqm_ref[...])
    for hd in range(N_MEM_HEADS):
        sl = slice(hd * MEM_HEAD_DIM, (hd + 1) * MEM_HEAD_DIM)
        zh = zm[:, sl]
        ms = jnp.mean(zh * zh, axis=-1, keepdims=True)
        qm_ref[:, sl] = (zh * lax.rsqrt(ms + RMS_EPS) * qmn_ref[...] * (MEM_HEAD_DIM ** -0.5)).astype(BF16)

    u_ref[...] = _dot(hb, wu_ref[...])
    for c in range(3):
        sl = slice(c * D_MODEL, (c + 1) * D_MODEL)
        gate_ref[:, sl] = (0.5 * jnp.tanh(0.5 * _dot(hb, wg_ref[:, sl])) + 0.5).astype(BF16)


def _inproj(x2d, w, tm, seq, kv_transposed):
    n = x2d.shape[0]
    assert n % tm == 0
    row = lambda width: pl.BlockSpec((tm, width), lambda i: (i, 0))
    if kv_transposed:
        assert seq % tm == 0
        per = seq // tm
        kv_shape = jax.ShapeDtypeStruct((n // seq, FOX_WIDTH, seq), F32)
        heads = pl.BlockSpec((1, FOX_WIDTH, tm), lambda i: (i // per, 0, i % per))
    else:
        kv_shape = jax.ShapeDtypeStruct((n * N_FOX_HEADS, FOX_HEAD_DIM), F32)
        heads = pl.BlockSpec((tm * N_FOX_HEADS, FOX_HEAD_DIM), lambda i: (i, 0))
    ins = [x2d, w['norm_mix'], w['wqkv'], w['wf'], w['bf'], w['wqm'], w['wu'], w['wg'],
           w['qn_fox'], w['kn_fox'], w['qn_mem'], w['bd']]
    in_specs = [row(D_MODEL)] + [_full(a.shape) for a in ins[1:]]
    out_shape = (
        jax.ShapeDtypeStruct((n, FOX_WIDTH), BF16),
        jax.ShapeDtypeStruct((n, FOX_WIDTH), BF16),
        jax.ShapeDtypeStruct((n, FOX_WIDTH), BF16),
        kv_shape,
        kv_shape,
        jax.ShapeDtypeStruct((N_FOX_HEADS, n), F32),
        jax.ShapeDtypeStruct((n, MEM_WIDTH), BF16),
        jax.ShapeDtypeStruct((n, SSM_WIDTH), F32),
        jax.ShapeDtypeStruct((n, 3 * D_MODEL), BF16),
    )
    out_specs = (row(FOX_WIDTH), row(FOX_WIDTH), row(FOX_WIDTH), heads, heads,
                 pl.BlockSpec((N_FOX_HEADS, tm), lambda i: (0, i)),
                 row(MEM_WIDTH), row(SSM_WIDTH), row(3 * D_MODEL))
    return pl.pallas_call(
        functools.partial(_inproj_kernel, kv_transposed=kv_transposed),
        out_shape=out_shape, grid=(n // tm,), in_specs=in_specs, out_specs=out_specs,
        compiler_params=_params(("parallel",)), name="inproj")(*ins)


def _memkv_kernel(x_ref, g_ref, w_ref, kn_ref, k_ref, v_ref):
    x = x_ref[...]
    h = x * lax.rsqrt(jnp.mean(x * x, axis=-1, keepdims=True) + RMS_EPS) * g_ref[...]
    hb = h.astype(BF16)
    zk = _dot(hb, w_ref[:, 0:MEM_WIDTH])
    for hd in range(N_MEM_HEADS):
        sl = slice(hd * MEM_HEAD_DIM, (hd + 1) * MEM_HEAD_DIM)
        zh = zk[:, sl]
        ms = jnp.mean(zh * zh, axis=-1, keepdims=True)
        k_ref[:, sl] = zh * lax.rsqrt(ms + RMS_EPS) * kn_ref[...]
    v_ref[...] = _dot(hb, w_ref[:, MEM_WIDTH:2 * MEM_WIDTH])


def _memkv(mem2d, norm_mem, w_kv, kn_mem, tm):
    n = mem2d.shape[0]
    row = lambda width: pl.BlockSpec((tm, width), lambda i: (i, 0))
    return pl.pallas_call(
        _memkv_kernel,
        out_shape=(jax.ShapeDtypeStruct((n, MEM_WIDTH), F32), jax.ShapeDtypeStruct((n, MEM_WIDTH), F32)),
        grid=(n // tm,),
        in_specs=[row(D_MODEL), _full(norm_mem.shape), _full(w_kv.shape), _full(kn_mem.shape)],
        out_specs=(row(MEM_WIDTH), row(MEM_WIDTH)),
        compiler_params=_params(("parallel",)), name="memkv")(mem2d, norm_mem, w_kv, kn_mem)


CUMSUM_BLOCK = 256


def _cumsum_kernel(x_ref, o_ref):
    nblk = x_ref.shape[1] // CUMSUM_BLOCK
    r = lax.broadcasted_iota(jnp.int32, (CUMSUM_BLOCK, CUMSUM_BLOCK), 0)
    c = lax.broadcasted_iota(jnp.int32, (CUMSUM_BLOCK, CUMSUM_BLOCK), 1)
    tri = (r <= c).astype(F32)
    carry = jnp.zeros((x_ref.shape[0], 1), F32)
    for j in range(nblk):
        sl = slice(j * CUMSUM_BLOCK, (j + 1) * CUMSUM_BLOCK)
        cs = _dot_exact(x_ref[:, sl], tri) + carry
        o_ref[:, sl] = cs
        carry = cs[:, CUMSUM_BLOCK - 1:CUMSUM_BLOCK]


def _cumsum_rows(x):
    rows, n = x.shape
    npad = -(-n // CUMSUM_BLOCK) * CUMSUM_BLOCK
    xp = jnp.pad(x, ((0, 0), (0, npad - n))) if npad != n else x
    out = pl.pallas_call(
        _cumsum_kernel, out_shape=jax.ShapeDtypeStruct((rows, npad), F32), grid=(1,),
        in_specs=[_full((rows, npad))], out_specs=_full((rows, npad)),
        compiler_params=_params(("arbitrary",)), name="cumsum")(xp)
    return out[:, :n] if npad != n else out


def _reduce_rows(x, op):
    rows, cols = x.shape
    if rows > 64 and rows % 64 == 0:
        x = op(x.reshape(rows // 64, 64, cols), axis=0)
        rows = 64
    if rows == 64:
        x = op(x.reshape(8, 8, cols), axis=0)
    return op(x, axis=0, keepdims=True)


def _fox_prompt_kernel(q_ref, k_ref, v_ref, cc_ref, cr_ref, o_ref,
                       vt_ref, ck0_ref, ck1_ref, st0_ref, st1_ref, pt0_ref, pt1_ref, acc_ref, *, tq, tk):
    st_refs = (st0_ref, st1_ref)
    pt_refs = (pt0_ref, pt1_ref)
    i = pl.program_id(2)
    s_len = k_ref.shape[1]

    @pl.when(i == 0)
    def _():
        vt_ref[...] = v_ref[0].astype(F32).T.astype(BF16)
        ck0_ref[...] = jnp.broadcast_to(cc_ref[0, 0, :, 0:1], (s_len, LANES))
        ck1_ref[...] = jnp.broadcast_to(cc_ref[0, 0, :, 1:2], (s_len, LANES))

    qt = q_ref[0].astype(F32).T
    row = lax.broadcasted_iota(jnp.int32, (LANES, tq), 0)
    qts = (jnp.where(row < FOX_HEAD_DIM, qt, 0.0).astype(BF16), jnp.where(row < FOX_HEAD_DIM, 0.0, qt).astype(BF16))
    q0 = pl.multiple_of(i * tq, tq)
    cq = cr_ref[0, 0, :, pl.ds(q0, tq)]
    ck_refs = (ck0_ref, ck1_ref)

    def stage_a(n, par):
        s = pl.multiple_of(n * tk, tk)
        kb = k_ref[0, pl.ds(s, tk), :]
        for hh in range(2):
            ck = ck_refs[hh][pl.ds(s, tk), :]
            st_refs[par][hh] = _dot(kb, qts[hh]) - jnp.concatenate([ck] * (tq // LANES), axis=1)

    def stage_b(n, par, stats, masked):
        if masked:
            kpos = n * tk + lax.broadcasted_iota(jnp.int32, (tk, tq), 0)
            qpos = q0 + lax.broadcasted_iota(jnp.int32, (tk, tq), 1)
            mask = kpos <= qpos
        out = []
        for hh in range(2):
            m, l = stats[2 * hh:2 * hh + 2]
            t = st_refs[par][hh]
            if masked:
                t = jnp.where(mask, t, NEG_INF)
            cqh = cq[hh:hh + 1, :]
            m_new = jnp.maximum(m, _reduce_rows(t, jnp.max) + cqh)
            alpha = jnp.exp2(m - m_new)
            p = jnp.exp2(t + (cqh - m_new))
            pt_refs[par][hh] = p.astype(BF16)
            out.extend([m_new, alpha * l + _reduce_rows(p, jnp.sum), alpha])
        return tuple(out)

    def stage_c(n, par, alphas):
        s = pl.multiple_of(jnp.maximum(n, 0) * tk, tk)
        for hh in range(2):
            vt = vt_ref[hh * FOX_HEAD_DIM:(hh + 1) * FOX_HEAD_DIM, pl.ds(s, tk)]
            acc_ref[hh] = alphas[hh] * acc_ref[hh] + _dot(vt, pt_refs[par][hh])

    def iteration(n, par, carry):
        m0, l0, al0, m1, l1, al1 = carry
        stage_c(n - 1, 1 - par, (al0, al1))
        new = stage_b(n, par, (m0, l0, m1, l1), False)
        stage_a(n + 1, 1 - par)
        return new

    def finish(par, carry):
        m0, l0, al0, m1, l1, al1 = carry
        stage_c(nfull - 1, 1 - par, (al0, al1))
        _, l0, be0, _, l1, be1 = stage_b(nfull, par, (m0, l0, m1, l1), True)
        stage_c(nfull, par, (be0, be1))
        ot = jnp.concatenate([acc_ref[0] / l0, acc_ref[1] / l1], axis=0)
        o_ref[0] = ot.T.astype(o_ref.dtype)

    acc_ref[...] = jnp.zeros(acc_ref.shape, F32)
    pt1_ref[...] = jnp.zeros(pt1_ref.shape, BF16)
    neg = jnp.full((1, tq), NEG_INF, F32)
    zero = jnp.zeros((1, tq), F32)
    one = jnp.ones((1, tq), F32)
    nfull = (i * tq) // tk
    stage_a(0, 0)
    carry = lax.fori_loop(0, nfull // 2, lambda k, c: iteration(2 * k + 1, 1, iteration(2 * k, 0, c)),
                          (neg, zero, one, neg, zero, one))
    odd = nfull % 2 == 1
    carry = lax.cond(odd, lambda c: iteration(nfull - 1, 0, c), lambda c: c, carry)
    pl.when(odd)(lambda: finish(1, carry))
    pl.when(jnp.logical_not(odd))(lambda: finish(0, carry))


def _fox_prompt(q, k, v, c_col, c_row, tq, tk):
    b, s, _ = q.shape
    assert s % tk == 0 and tk % tq == 0
    npair = N_FOX_HEADS // 2
    return pl.pallas_call(
        functools.partial(_fox_prompt_kernel, tq=tq, tk=tk),
        out_shape=jax.ShapeDtypeStruct((b, s, FOX_WIDTH), BF16),
        grid=(b, npair, s // tq),
        in_specs=[
            pl.BlockSpec((1, tq, LANES), lambda bi, hp, i: (bi, i, hp)),
            pl.BlockSpec((1, s, LANES), lambda bi, hp, i: (bi, 0, hp)),
            pl.BlockSpec((1, s, LANES), lambda bi, hp, i: (bi, 0, hp)),
            pl.BlockSpec((1, 1, s, 2), lambda bi, hp, i: (bi, hp, 0, 0)),
            pl.BlockSpec((1, 1, 2, s), lambda bi, hp, i: (bi, hp, 0, 0)),
        ],
        out_specs=pl.BlockSpec((1, tq, LANES), lambda bi, hp, i: (bi, i, hp)),
        scratch_shapes=[pltpu.VMEM((LANES, s), BF16),
                        pltpu.VMEM((s, LANES), F32), pltpu.VMEM((s, LANES), F32),
                        pltpu.VMEM((2, tk, tq), F32), pltpu.VMEM((2, tk, tq), F32),
                        pltpu.VMEM((2, tk, tq), BF16), pltpu.VMEM((2, tk, tq), BF16),
                        pltpu.VMEM((2, FOX_HEAD_DIM, tq), F32)],
        compiler_params=_params(("parallel", "parallel", "arbitrary")), name="fox_prompt")(q, k, v, c_col, c_row)


def _fox_sample_kernel(q_ref, ck_ref, cv_ref, nk_ref, nv_ref, cq_ref, crc_ref, crn_ref, o_ref, *state, n):
    j = pl.program_id(1)
    nj = pl.num_programs(1)
    m_refs = state[0:N_FOX_HEADS]
    l_refs = state[N_FOX_HEADS:2 * N_FOX_HEADS]
    acc_refs = state[2 * N_FOX_HEADS:3 * N_FOX_HEADS]

    @pl.when(j == 0)
    def _():
        for hd in range(N_FOX_HEADS):
            m_refs[hd][...] = jnp.full(m_refs[hd].shape, NEG_INF, F32)
            l_refs[hd][...] = jnp.zeros(l_refs[hd].shape, F32)
            acc_refs[hd][...] = jnp.zeros(acc_refs[hd].shape, F32)

    def update(k_of, v_of, cr_ref_, mask, transposed):
        qk = _dot if transposed else _dot_nt
        pv = _dot_nt if transposed else _dot
        ts = []
        for hd in range(N_FOX_HEADS):
            hs = slice(hd * FOX_HEAD_DIM, (hd + 1) * FOX_HEAD_DIM)
            t = qk(q_ref[0, :, hs], k_of(hd)) - cr_ref_[0, hd:hd + 1, :]
            ts.append(t if mask is None else jnp.where(mask, t, NEG_INF))
        ps = []
        for hd in range(N_FOX_HEADS):
            cq = cq_ref[0, :, hd:hd + 1]
            m = m_refs[hd][...]
            m_new = jnp.maximum(m, jnp.max(ts[hd], axis=-1, keepdims=True) + cq)
            alpha = jnp.exp2(m - m_new)
            p = jnp.exp2(ts[hd] + (cq - m_new))
            m_refs[hd][...] = m_new
            l_refs[hd][...] = alpha * l_refs[hd][...] + jnp.sum(p, axis=-1, keepdims=True)
            ps.append((alpha, p.astype(BF16)))
        for hd in range(N_FOX_HEADS):
            alpha, p = ps[hd]
            acc_refs[hd][...] = alpha * acc_refs[hd][...] + pv(p, v_of(hd))

    update(lambda hd: ck_ref[0, hd].astype(BF16), lambda hd: cv_ref[0, hd].astype(BF16), crc_ref, None, True)

    @pl.when(j == nj - 1)
    def _():
        r = lax.broadcasted_iota(jnp.int32, (n, n), 0)
        c = lax.broadcasted_iota(jnp.int32, (n, n), 1)
        head = lambda ref: (lambda hd: ref[0, :, hd * FOX_HEAD_DIM:(hd + 1) * FOX_HEAD_DIM])
        update(head(nk_ref), head(nv_ref), crn_ref, c <= r, False)
        for hd in range(N_FOX_HEADS):
            hs = slice(hd * FOX_HEAD_DIM, (hd + 1) * FOX_HEAD_DIM)
            o_ref[0, :, hs] = (acc_refs[hd][...] / l_refs[hd][...]).astype(o_ref.dtype)


def _fox_sample(q, cache_k, cache_v, k_new, v_new, c_q, c_row_cache, c_row_new, tk):
    b, n, _ = q.shape
    past = cache_k.shape[3]
    assert past % tk == 0
    cache_spec = pl.BlockSpec((1, N_FOX_HEADS, FOX_HEAD_DIM, tk), lambda bi, j: (bi, 0, 0, j))
    return pl.pallas_call(
        functools.partial(_fox_sample_kernel, n=n),
        out_shape=jax.ShapeDtypeStruct((b, n, FOX_WIDTH), BF16),
        grid=(b, past // tk),
        in_specs=[
            pl.BlockSpec((1, n, FOX_WIDTH), lambda bi, j: (bi, 0, 0)),
            cache_spec,
            cache_spec,
            pl.BlockSpec((1, n, FOX_WIDTH), lambda bi, j: (bi, 0, 0)),
            pl.BlockSpec((1, n, FOX_WIDTH), lambda bi, j: (bi, 0, 0)),
            pl.BlockSpec((1, n, N_FOX_HEADS), lambda bi, j: (bi, 0, 0)),
            pl.BlockSpec((1, N_FOX_HEADS, tk), lambda bi, j: (bi, 0, j)),
            pl.BlockSpec((1, N_FOX_HEADS, n), lambda bi, j: (bi, 0, 0)),
        ],
        out_specs=pl.BlockSpec((1, n, FOX_WIDTH), lambda bi, j: (bi, 0, 0)),
        scratch_shapes=([pltpu.VMEM((n, 1), F32)] * (2 * N_FOX_HEADS)
                        + [pltpu.VMEM((n, FOX_HEAD_DIM), F32)] * N_FOX_HEADS),
        compiler_params=_params(("parallel", "arbitrary")), name="fox_sample")(
            q, cache_k, cache_v, k_new, v_new, c_q, c_row_cache, c_row_new)


def _ssm_mats(p):
    f32 = F32
    a_re, a_im = p['ssm_a_re'].astype(f32), p['ssm_a_im'].astype(f32)
    b_re, b_im = p['ssm_b_re'].astype(f32), p['ssm_b_im'].astype(f32)
    c_re, c_im = p['ssm_c_re'].astype(f32), p['ssm_c_im'].astype(f32)
    dt = jnp.exp(p['ssm_log_dt'].astype(f32))[:, None]
    mag = jnp.exp(dt * a_re)
    ab_re = mag * jnp.cos(dt * a_im)
    ab_im = mag * jnp.sin(dt * a_im)
    den = a_re * a_re + a_im * a_im
    nr, ni = ab_re - 1.0, ab_im
    coef_re = (nr * a_re + ni * a_im) / den
    coef_im = (ni * a_re - nr * a_im) / den
    bb_re = coef_re[..., None] * b_re - coef_im[..., None] * b_im
    bb_im = coef_re[..., None] * b_im + coef_im[..., None] * b_re
    pr, pi = [jnp.ones_like(ab_re)], [jnp.zeros_like(ab_im)]
    for _ in range(SSM_CHUNK):
        pr.append(pr[-1] * ab_re - pi[-1] * ab_im)
        pi.append(pr[-2] * ab_im + pi[-1] * ab_re)
    pw_re, pw_im = jnp.stack(pr), jnp.stack(pi)
    T = SSM_CHUNK
    w_re = pw_re[..., None] * bb_re[None] - pw_im[..., None] * bb_im[None]
    w_im = pw_re[..., None] * bb_im[None] + pw_im[..., None] * bb_re[None]
    kk = (jnp.einsum('gop,kgpi->kgoi', c_re, w_re[:T], precision='highest')
          - jnp.einsum('gop,kgpi->kgoi', c_im, w_im[:T], precision='highest'))
    nq = N_SSM_GROUPS // SSM_GPB

    def lane_diag(m):
        lead = m.shape[:-3]
        i, c = m.shape[-2:]
        m = jnp.tile(m.reshape(lead + (nq, SSM_GPB * i, c)), (1,) * (len(lead) + 2) + (SSM_GPB,))
        same = (jnp.arange(SSM_GPB * i) // i)[:, None] == (jnp.arange(SSM_GPB * c) // c)[None, :]
        return jnp.where(same, m, 0.0)

    ktau = lane_diag(jnp.swapaxes(kk, -1, -2))
    ktau = jnp.concatenate([jnp.zeros_like(ktau[:1]), ktau], axis=0)
    units = []
    for dlag in range(T // 2 - 1, -1, -1):
        top = jnp.concatenate([ktau[2 * dlag + 1], ktau[2 * dlag + 2]], axis=-1)
        bot = jnp.concatenate([ktau[2 * dlag], ktau[2 * dlag + 1]], axis=-1)
        units.append(jnp.concatenate([top, bot], axis=-2))
    kstack = jnp.concatenate(units, axis=-2).astype(BF16)
    rev = T - 1 - jnp.arange(T)
    m_re = lane_diag(jnp.swapaxes(w_re[rev], -1, -2))
    m_im = lane_diag(jnp.swapaxes(w_im[rev], -1, -2))
    m_all = jnp.concatenate([m_re, m_im], axis=-1)
    m_all = jnp.swapaxes(m_all, 0, 1).reshape(nq, T * LANES, 2 * SSM_GPB * SSM_STATE)
    m_hi, m_lo = _split_bf16(m_all)
    ar, ai = pw_re[1:], pw_im[1:]
    n_re = (c_re[None] * ar[:, :, None, :] - c_im[None] * ai[:, :, None, :])
    n_im = -(c_re[None] * ai[:, :, None, :] + c_im[None] * ar[:, :, None, :])

    def state_rows(n):
        n = jnp.transpose(n, (1, 3, 0, 2)).reshape(nq, SSM_GPB * SSM_STATE, T, SSM_GROUP)
        n = jnp.tile(n, (1, 1, 1, SSM_GPB))
        same = (jnp.arange(SSM_GPB * SSM_STATE) // SSM_STATE)[:, None, None] == (jnp.arange(LANES) // SSM_GROUP)
        return jnp.where(same, n, 0.0).reshape(nq, SSM_GPB * SSM_STATE, T * LANES)

    n_all = jnp.concatenate([state_rows(n_re), state_rows(n_im)], axis=1).astype(BF16)
    return dict(kstack=kstack, m_hi=m_hi, m_lo=m_lo, n_all=n_all,
                a16_re=pw_re[T].reshape(8, 256), a16_im=pw_im[T].reshape(8, 256),
                d=p['ssm_d'].astype(f32).reshape(1, SSM_WIDTH))


def _chunk_tokens(u_ref, rows):
    return [u_ref[pl.ds(t, rows, stride=SSM_CHUNK), :] for t in range(SSM_CHUNK)]


def _ssm_local_kernel(u_ref, mh_ref, ml_ref, hre_ref, him_ref):
    rows = hre_ref.shape[0]
    parts = [_split_bf16(ut) for ut in _chunk_tokens(u_ref, rows)]
    x_hi = jnp.concatenate([h for h, _ in parts], axis=1)
    x_lo = jnp.concatenate([l for _, l in parts], axis=1)
    h = _dot(x_hi, mh_ref[0]) + _dot(x_hi, ml_ref[0]) + _dot(x_lo, mh_ref[0])
    half = SSM_GPB * SSM_STATE
    hre_ref[...] = h[:, 0:half]
    him_ref[...] = h[:, half:2 * half]


def _ssm_local(u2d, mats, rows):
    n = u2d.shape[0]
    r = n // SSM_CHUNK
    nq = N_SSM_GROUPS // SSM_GPB
    half = SSM_GPB * SSM_STATE
    mspec = pl.BlockSpec((1, SSM_CHUNK * LANES, 2 * half), lambda q, i: (q, 0, 0))
    ospec = pl.BlockSpec((rows, half), lambda q, i: (i, q))
    return pl.pallas_call(
        _ssm_local_kernel,
        out_shape=(jax.ShapeDtypeStruct((r, N_SSM_GROUPS * SSM_STATE), F32),) * 2,
        grid=(nq, r // rows),
        in_specs=[pl.BlockSpec((rows * SSM_CHUNK, LANES), lambda q, i: (i, q)), mspec, mspec],
        out_specs=(ospec, ospec),
        compiler_params=_params(("parallel", "parallel")), name="ssm_local")(u2d, mats['m_hi'], mats['m_lo'])


def _ssm_scan_kernel(lre_ref, lim_ref, are_ref, aim_ref, h0re_ref, h0im_ref,
                     pre_ref, pim_ref, fre_ref, fim_ref):
    nchunk = lre_ref.shape[1]
    ar, ai = are_ref[...], aim_ref[...]

    def body(c, carry):
        hr, hi = carry
        pre_ref[0, c] = hr
        pim_ref[0, c] = hi
        return (ar * hr - ai * hi + lre_ref[0, c], ar * hi + ai * hr + lim_ref[0, c])

    hr, hi = lax.fori_loop(0, nchunk, body, (h0re_ref[0], h0im_ref[0]))
    fre_ref[0] = hr
    fim_ref[0] = hi


def _ssm_scan(hloc_re, hloc_im, mats, h0_re, h0_im):
    b, nchunk = hloc_re.shape[:2]
    big = pl.BlockSpec((1, nchunk, 8, 256), lambda i: (i, 0, 0, 0))
    small = pl.BlockSpec((1, 8, 256), lambda i: (i, 0, 0))
    return pl.pallas_call(
        _ssm_scan_kernel,
        out_shape=(jax.ShapeDtypeStruct(hloc_re.shape, F32),) * 2 + (jax.ShapeDtypeStruct((b, 8, 256), F32),) * 2,
        grid=(b,),
        in_specs=[big, big, _full((8, 256)), _full((8, 256)), small, small],
        out_specs=(big, big, small, small),
        compiler_params=_params(("parallel",)), name="ssm_scan")(
            hloc_re, hloc_im, mats['a16_re'], mats['a16_im'], h0_re, h0_im)


def _gelu_tanh(y):
    return 0.5 * y * (1.0 + jnp.tanh(math.sqrt(2.0 / math.pi) * (y + 0.044715 * (y * y * y))))


def _ssm_out_kernel(u_ref, k_ref, pre_ref, pim_ref, n_ref, d_ref, y_ref, ysc_ref):
    rows = pre_ref.shape[0]
    us = _chunk_tokens(u_ref, rows)
    x = jnp.concatenate([ut.astype(BF16) for ut in us], axis=1)
    hp = jnp.concatenate([pre_ref[...], pim_ref[...]], axis=1).astype(BF16)
    unit = 2 * LANES
    nunit = SSM_CHUNK // 2
    for j in range(nunit):
        yj = (_dot(x[:, 0:unit * (j + 1)], k_ref[0, unit * (nunit - 1 - j):, :])
              + _dot(hp, n_ref[0, :, unit * j:unit * (j + 1)]))
        for t2 in range(2):
            t = 2 * j + t2
            y = yj[:, t2 * LANES:(t2 + 1) * LANES] + d_ref[...] * us[t]
            ysc_ref[pl.ds(t, rows, stride=SSM_CHUNK), :] = _gelu_tanh(y)
    y_ref[...] = ysc_ref[...].astype(y_ref.dtype)


def _ssm_out(u2d, hprev_re, hprev_im, mats, rows):
    n = u2d.shape[0]
    r = n // SSM_CHUNK
    nq = N_SSM_GROUPS // SSM_GPB
    half = SSM_GPB * SSM_STATE
    uspec = pl.BlockSpec((rows * SSM_CHUNK, LANES), lambda q, i: (i, q))
    hspec = pl.BlockSpec((rows, half), lambda q, i: (i, q))
    return pl.pallas_call(
        _ssm_out_kernel,
        out_shape=jax.ShapeDtypeStruct((n, SSM_WIDTH), BF16),
        grid=(nq, r // rows),
        in_specs=[uspec, pl.BlockSpec((1, SSM_CHUNK * LANES, 2 * LANES), lambda q, i: (q, 0, 0)), hspec, hspec,
                  pl.BlockSpec((1, 2 * half, SSM_CHUNK * LANES), lambda q, i: (q, 0, 0)),
                  pl.BlockSpec((1, LANES), lambda q, i: (0, q))],
        out_specs=uspec,
        scratch_shapes=[pltpu.VMEM((rows * SSM_CHUNK, LANES), F32)],
        compiler_params=_params(("parallel", "parallel")), name="ssm_out")(
            u2d, mats['kstack'], hprev_re, hprev_im, mats['n_all'], mats['d'])


def _ssm(u2d, b, h0_re, h0_im, mats):
    n = u2d.shape[0]
    nchunk = n // b // SSM_CHUNK
    r = b * nchunk
    rows = _pick_tile(r, 256)
    hloc_re, hloc_im = _ssm_local(u2d, mats, rows)
    shp = (b, nchunk, 8, 256)
    hprev_re, hprev_im, f_re, f_im = _ssm_scan(hloc_re.reshape(shp), hloc_im.reshape(shp), mats,
                                               h0_re.reshape(b, 8, 256), h0_im.reshape(b, 8, 256))
    y = _ssm_out(u2d, hprev_re.reshape(r, -1), hprev_im.reshape(r, -1), mats, rows)
    return y, f_re.reshape(b, N_SSM_GROUPS, SSM_STATE), f_im.reshape(b, N_SSM_GROUPS, SSM_STATE)


def _merge_kernel(x_ref, of_ref, ys_ref, qm_ref, gate_ref, mk_ref, mv_ref,
                  wglu_ref, wbf_ref, wbs_ref, wbm_ref, wo_ref, nf_ref, wr_ref,
                  x1_ref, h2_ref, r_ref):
    tm = x_ref.shape[0]
    om = []
    for hd in range(N_MEM_HEADS):
        sl = slice(hd * MEM_HEAD_DIM, (hd + 1) * MEM_HEAD_DIM)
        kh = mk_ref[0, :, sl].astype(BF16)
        vh = mv_ref[0, :, sl].astype(BF16)
        sc = _dot_nt(qm_ref[:, sl], kh)
        p = jnp.exp(sc - jnp.max(sc, axis=-1, keepdims=True))
        om.append(_dot(p.astype(BF16), vh) / jnp.sum(p, axis=-1, keepdims=True))
    o_mem = jnp.concatenate(om, axis=-1).astype(BF16)
    z = _dot(ys_ref[...], wglu_ref[...])
    y_ssm = (z[:, 0:SSM_WIDTH] * jax.nn.sigmoid(z[:, SSM_WIDTH:2 * SSM_WIDTH])).astype(BF16)
    g = lambda c: gate_ref[:, c * D_MODEL:(c + 1) * D_MODEL].astype(F32)
    merged = (g(0) * _dot(of_ref[...], wbf_ref[...]) + g(1) * _dot(y_ssm, wbs_ref[...])
              + g(2) * _dot(o_mem, wbm_ref[...]))
    x1 = x_ref[...] + _dot(merged.astype(BF16), wo_ref[...])
    x1_ref[...] = x1
    h2 = x1 * lax.rsqrt(jnp.mean(x1 * x1, axis=-1, keepdims=True) + RMS_EPS) * nf_ref[...]
    h2_ref[...] = h2.astype(BF16)
    h2_hi, h2_lo = _split_bf16(h2)
    hw = _dot(h2_hi, wr_ref[...])
    logits = hw[:, 0:LANES] + hw[:, LANES:2 * LANES] + _dot(h2_lo, wr_ref[:, 0:LANES])
    lane = lax.broadcasted_iota(jnp.int32, (tm, LANES), 1)
    big = jnp.int32(LANES)
    is_grp = (lane >= N_EXPERTS) & (lane < N_EXPERTS + N_EXPERT_GROUPS)
    gl = jnp.where(is_grp, logits, NEG_INF)
    gmax = jnp.max(gl, axis=-1, keepdims=True)
    grp = jnp.min(jnp.where(is_grp & (gl == gmax), lane, big), axis=-1, keepdims=True) - N_EXPERTS
    g_w = 1.0 / jnp.sum(jnp.where(is_grp, jnp.exp(gl - gmax), 0.0), axis=-1, keepdims=True)
    in_grp = (lane >= grp * EXPERTS_PER_GROUP) & (lane < (grp + 1) * EXPERTS_PER_GROUP)
    e1 = jnp.where(in_grp, logits, NEG_INF)
    m1 = jnp.max(e1, axis=-1, keepdims=True)
    i1 = jnp.min(jnp.where(in_grp & (e1 == m1), lane, big), axis=-1, keepdims=True)
    rest = in_grp & (lane != i1)
    e2 = jnp.where(rest, logits, NEG_INF)
    m2 = jnp.max(e2, axis=-1, keepdims=True)
    i2 = jnp.min(jnp.where(rest & (e2 == m2), lane, big), axis=-1, keepdims=True)
    ex = jnp.exp(m2 - m1)
    w1 = g_w / (1.0 + ex)
    w2 = g_w * ex / (1.0 + ex)
    r_ref[...] = jnp.where(lane == i1, w1, jnp.where(lane == i2, w2, jnp.where(lane == GROUP_LANE, grp.astype(F32), 0.0)))


def _merge(x2d, o_fox, ys, q_m, gates, mem_k, mem_v, w, tm, rows_per_batch):
    n = x2d.shape[0]
    assert n % tm == 0 and rows_per_batch % tm == 0
    per = rows_per_batch // tm
    row = lambda width: pl.BlockSpec((tm, width), lambda i: (i, 0))
    memspec = pl.BlockSpec((1, N_MEM, MEM_WIDTH), lambda i: (i // per, 0, 0))
    ws = [w['w_glu'], w['w_br_fox'], w['w_br_ssm'], w['w_br_mem'], w['w_out'], w['norm_ffn'], w['w_router']]
    return pl.pallas_call(
        _merge_kernel,
        out_shape=(jax.ShapeDtypeStruct((n, D_MODEL), F32), jax.ShapeDtypeStruct((n, D_MODEL), BF16),
                   jax.ShapeDtypeStruct((n, LANES), F32)),
        grid=(n // tm,),
        in_specs=[row(D_MODEL), row(FOX_WIDTH), row(SSM_WIDTH), row(MEM_WIDTH), row(3 * D_MODEL), memspec, memspec]
                 + [_full(a.shape) for a in ws],
        out_specs=(row(D_MODEL), row(D_MODEL), row(LANES)),
        compiler_params=_params(("parallel",)), name="merge")(
            x2d, o_fox, ys, q_m, gates, mem_k, mem_v, *ws)


def _moe_kernel(h_ref, r_ref, x1_ref, tri_ref, wg_ref, wu_ref, wd_ref, o_ref,
                xs_ref, cw_ref, og_ref, acc_ref, rank_ref, nblk_ref, *, main):
    step = pl.program_id(1)
    steps_per_group = EXPERTS_PER_GROUP // MOE_EPS
    g = step // steps_per_group
    tm = h_ref.shape[0]
    gf = g.astype(F32)
    blocks = [(0, main, None)] + [(r0, MOE_SUB, r0 // MOE_SUB) for r0 in range(main, tm, MOE_SUB)]

    def guarded(need, fn):
        if need is None:
            fn()
        else:
            pl.when(need < nblk_ref[0])(fn)

    @pl.when(step == 0)
    def _():
        acc_ref[...] = jnp.zeros(acc_ref.shape, F32)

    @pl.when(step % steps_per_group == 0)
    def _():
        rt = r_ref[...]
        rtt = rt.T
        mrow = rtt[GROUP_LANE:GROUP_LANE + 1, :] == gf
        m8 = jnp.broadcast_to(jnp.where(mrow, 1.0, 0.0), (8, tm))
        rank8 = _dot(m8.astype(BF16), tri_ref[...])
        rank_row = jnp.where(mrow, rank8[0:1, :], -1.0)
        rank_ref[...] = jnp.broadcast_to(jnp.where(mrow, rank8, -1.0).T[:, 0:1], rank_ref.shape)
        count = jnp.sum(jnp.where(mrow, 1, 0))
        nblk_ref[0] = (count + MOE_SUB - 1) // MOE_SUB
        hi = rt.astype(BF16)
        rem = rt - hi.astype(F32)
        mid = rem.astype(BF16)
        lo = (rem - mid.astype(F32)).astype(BF16)
        for r0, nrows, need in blocks:
            def compact(r0=r0, nrows=nrows):
                rows = slice(r0, r0 + nrows)
                slot = r0 + lax.broadcasted_iota(jnp.int32, (nrows, tm), 0)
                perm = jnp.where(rank_row == slot.astype(F32), 1.0, 0.0).astype(BF16)
                xs_ref[rows, :] = _dot(perm, h_ref[...]).astype(BF16)
                cw_ref[rows, :] = _dot(perm, hi) + _dot(perm, mid) + _dot(perm, lo)
                og_ref[rows, :] = jnp.zeros((nrows, D_MODEL), F32)
            guarded(need, compact)

    for k in range(MOE_EPS):
        e = step * MOE_EPS + k
        for r0, nrows, need in blocks:
            def expert(r0=r0, nrows=nrows, k=k, e=e):
                rows = slice(r0, r0 + nrows)
                x = xs_ref[rows, :]
                a = _dot(x, wg_ref[k])
                up = _dot(x, wu_ref[k])
                lane = lax.broadcasted_iota(jnp.int32, (nrows, LANES), 1)
                ce = jnp.sum(jnp.where(lane == e, cw_ref[rows, :], 0.0), axis=-1, keepdims=True)
                act = a * jax.nn.sigmoid(a) * up * ce
                og_ref[rows, :] += _dot(act.astype(BF16), wd_ref[k])
            guarded(need, expert)

    @pl.when(step % steps_per_group == steps_per_group - 1)
    def _():
        for r0, nrows, need in blocks:
            def scatter_back(r0=r0, nrows=nrows):
                rows = slice(r0, r0 + nrows)
                slot = r0 + lax.broadcasted_iota(jnp.int32, (tm, nrows), 1)
                back = jnp.where(rank_ref[:, 0:1] == slot.astype(F32), 1.0, 0.0).astype(BF16)
                acc_ref[...] += _dot(back, og_ref[rows, :].astype(BF16))
            guarded(need, scatter_back)

    @pl.when(step == pl.num_programs(1) - 1)
    def _():
        o_ref[...] = x1_ref[...] + acc_ref[...]


def _moe(h2, route, x1, wg, wu, wd, tm):
    n = h2.shape[0]
    assert n % tm == 0 and tm % MOE_SUB == 0
    main = max(MOE_SUB, (3 * tm // 8) // MOE_SUB * MOE_SUB)
    row = lambda width: pl.BlockSpec((tm, width), lambda i, s: (i, 0))
    r = jnp.arange(tm)
    tri = (r[:, None] < r[None, :]).astype(BF16)
    return pl.pallas_call(
        functools.partial(_moe_kernel, main=main),
        out_shape=jax.ShapeDtypeStruct((n, D_MODEL), F32),
        grid=(n // tm, N_EXPERTS // MOE_EPS),
        in_specs=[row(D_MODEL), row(LANES), row(D_MODEL), pl.BlockSpec((tm, tm), lambda i, s: (0, 0)),
                  pl.BlockSpec((MOE_EPS, D_MODEL, D_EXPERT), lambda i, s: (s, 0, 0)),
                  pl.BlockSpec((MOE_EPS, D_MODEL, D_EXPERT), lambda i, s: (s, 0, 0)),
                  pl.BlockSpec((MOE_EPS, D_EXPERT, D_MODEL), lambda i, s: (s, 0, 0))],
        out_specs=row(D_MODEL),
        scratch_shapes=[pltpu.VMEM((tm, D_MODEL), BF16), pltpu.VMEM((tm, LANES), F32), pltpu.VMEM((tm, D_MODEL), F32),
                        pltpu.VMEM((tm, D_MODEL), F32), pltpu.VMEM((tm, LANES), F32), pltpu.SMEM((1,), jnp.int32)],
        compiler_params=_params(("parallel", "arbitrary")), name="moe")(h2, route, x1, tri, wg, wu, wd)


def _prep_weights(p):
    w_in = p['w_in'].astype(BF16)
    o = 0
    wqkv = w_in[:, 0:3 * FOX_WIDTH]
    o = 3 * FOX_WIDTH
    wf = jnp.pad(w_in[:, o:o + N_FOX_HEADS], ((0, 0), (0, LANES - N_FOX_HEADS)))
    o += N_FOX_HEADS
    wqm = w_in[:, o:o + MEM_WIDTH]
    o += MEM_WIDTH
    wu = w_in[:, o:o + SSM_WIDTH]
    o += SSM_WIDTH
    wg = w_in[:, o:o + 3 * D_MODEL]
    r = jnp.arange(FOX_WIDTH) // FOX_HEAD_DIM
    bd = (r[:, None] == r[None, :]).astype(BF16)
    w_router = jnp.concatenate(
        [p['w_router_expert'], p['w_router_group'],
         jnp.zeros((D_MODEL, LANES - N_EXPERTS - N_EXPERT_GROUPS), F32)], axis=1)
    w_router = jnp.concatenate(_split_bf16(w_router), axis=1)
    return dict(
        norm_mix=p['norm_mix'].reshape(1, D_MODEL), wqkv=wqkv, wf=wf,
        bf=jnp.pad(p['b_forget'], (0, LANES - N_FOX_HEADS)).reshape(1, LANES),
        wqm=wqm, wu=wu, wg=wg,
        qn_fox=jnp.tile(p['qn_fox'], N_FOX_HEADS).reshape(1, FOX_WIDTH),
        kn_fox=jnp.tile(p['kn_fox'], N_FOX_HEADS).reshape(1, FOX_WIDTH),
        qn_mem=p['qn_mem'].reshape(1, MEM_HEAD_DIM), bd=bd,
        w_glu=p['w_glu'].astype(BF16), w_br_fox=p['w_br_fox'].astype(BF16),
        w_br_ssm=p['w_br_ssm'].astype(BF16), w_br_mem=p['w_br_mem'].astype(BF16),
        w_out=p['w_out'].astype(BF16), norm_ffn=p['norm_ffn'].reshape(1, D_MODEL), w_router=w_router,
        moe_wg=p['moe_w_gate'].astype(BF16), moe_wu=p['moe_w_up'].astype(BF16),
        moe_wd=p['moe_w_down'].astype(BF16))


def _pick_tile(n, target):
    t = min(n, target)
    while n % t:
        t //= 2
    return t


def _group(x, w, mats, mem_k, mem_v, h0_re, h0_im, cache):
    b, s, _ = x.shape
    n = b * s
    x2d = x.reshape(n, D_MODEL)
    prompt = cache is None
    q, kb, vb, k_out, v_out, lf_t, q_m, u, gates = _inproj(x2d, w, _pick_tile(s if prompt else n, 512), s, prompt)
    q3 = q.reshape(b, s, FOX_WIDTH)
    k3 = kb.reshape(b, s, FOX_WIDTH)
    v3 = vb.reshape(b, s, FOX_WIDTH)
    lf_rows = lf_t.reshape(N_FOX_HEADS, b, s).transpose(1, 0, 2)
    lf3 = lf_rows.transpose(0, 2, 1)
    npair = N_FOX_HEADS // 2
    if prompt:
        c_row = LOG2E * _cumsum_rows(lf_rows.reshape(b * N_FOX_HEADS, s)).reshape(b, npair, 2, s)
        c_col = c_row.transpose(0, 1, 3, 2)
        o_fox = _fox_prompt(q3, k3, v3, c_col, c_row, _pick_tile(s, 512), _pick_tile(s, 512))
        unt = lambda a: a.reshape(b, N_FOX_HEADS, FOX_HEAD_DIM, s).transpose(0, 3, 1, 2)
        k4, v4 = unt(k_out), unt(v_out)
    else:
        cache_k, cache_v, cache_logf = cache
        past = cache_k.shape[1]
        lf_all = jnp.concatenate([cache_logf.astype(F32).transpose(0, 2, 1), lf_rows], axis=2)
        c_row = LOG2E * _cumsum_rows(lf_all.reshape(b * N_FOX_HEADS, past + s)).reshape(b, N_FOX_HEADS, past + s)
        o_fox = _fox_sample(q3, cache_k.transpose(0, 2, 3, 1), cache_v.transpose(0, 2, 3, 1), k3, v3,
                            c_row[:, :, past:].transpose(0, 2, 1), c_row[:, :, :past], c_row[:, :, past:],
                            _pick_tile(past, 1024))
        k4 = k_out.reshape(b, s, N_FOX_HEADS, FOX_HEAD_DIM)
        v4 = v_out.reshape(b, s, N_FOX_HEADS, FOX_HEAD_DIM)
    ys, f_re, f_im = _ssm(u, b, h0_re, h0_im, mats)
    tm = _pick_tile(s, 512)
    x1, h2, route = _merge(x2d, o_fox.reshape(n, FOX_WIDTH), ys, q_m, gates, mem_k, mem_v, w, tm, s)
    y = _moe(h2, route, x1, w['moe_wg'], w['moe_wu'], w['moe_wd'], _pick_tile(n, 1024))
    return y.reshape(b, s, D_MODEL), k4, v4, lf3, f_re, f_im


def kernel(x_prompt, x_sample, mem_prompt, cache_fox_k, cache_fox_v, cache_fox_logf, state_ssm_re, state_ssm_im,
           cache_mem_k, cache_mem_v, norm_mix, w_in, b_forget, qn_fox, kn_fox, qn_mem, kn_mem, norm_mem, w_mem_kv,
           ssm_a_re, ssm_a_im, ssm_log_dt, ssm_b_re, ssm_b_im, ssm_c_re, ssm_c_im, ssm_d, w_glu, w_br_fox,
           w_br_ssm, w_br_mem, w_out, norm_ffn, w_router_group, w_router_expert, moe_w_gate, moe_w_up,
           moe_w_down):
    depth = norm_mix.shape[0]
    assert depth == 1
    l = 0
    p = dict(norm_mix=norm_mix[l], w_in=w_in[l], b_forget=b_forget[l], qn_fox=qn_fox[l], kn_fox=kn_fox[l],
             qn_mem=qn_mem[l], ssm_a_re=ssm_a_re[l], ssm_a_im=ssm_a_im[l], ssm_log_dt=ssm_log_dt[l],
             ssm_b_re=ssm_b_re[l], ssm_b_im=ssm_b_im[l], ssm_c_re=ssm_c_re[l], ssm_c_im=ssm_c_im[l],
             ssm_d=ssm_d[l], w_glu=w_glu[l], w_br_fox=w_br_fox[l], w_br_ssm=w_br_ssm[l], w_br_mem=w_br_mem[l],
             w_out=w_out[l], norm_ffn=norm_ffn[l], w_router_group=w_router_group[l],
             w_router_expert=w_router_expert[l], moe_w_gate=moe_w_gate[l], moe_w_up=moe_w_up[l],
             moe_w_down=moe_w_down[l])
    w = _prep_weights(p)
    mats = _ssm_mats(p)
    bp, sp, _ = x_prompt.shape
    bs, ss, _ = x_sample.shape

    mk, mv = _memkv(mem_prompt.reshape(bp * N_MEM, D_MODEL), norm_mem[l].reshape(1, D_MODEL),
                    w_mem_kv[l].astype(BF16), kn_mem[l].reshape(1, MEM_HEAD_DIM), _pick_tile(bp * N_MEM, 512))
    mk = mk.reshape(bp, N_MEM, MEM_WIDTH)
    mv = mv.reshape(bp, N_MEM, MEM_WIDTH)
    zeros = jnp.zeros((bp, N_SSM_GROUPS, SSM_STATE), F32)
    yp, pk, pv, plf, pre, pim = _group(x_prompt, w, mats, mk, mv, zeros, zeros, None)
    cache = (cache_fox_k[l], cache_fox_v[l], cache_fox_logf[l])
    ys, sk, sv, slf, sre, sim = _group(
        x_sample, w, mats, cache_mem_k[l].reshape(bs, N_MEM, MEM_WIDTH), cache_mem_v[l].reshape(bs, N_MEM, MEM_WIDTH),
        state_ssm_re[l].astype(F32), state_ssm_im[l].astype(F32), cache)
    st = lambda a: a[None]
    return (yp, ys, st(pk), st(pv), st(plf), st(pre), st(pim),
            st(mk.reshape(bp, N_MEM, N_MEM_HEADS, MEM_HEAD_DIM)), st(mv.reshape(bp, N_MEM, N_MEM_HEADS, MEM_HEAD_DIM)),
            st(sk), st(sv), st(slf), st(sre), st(sim))
```

```python
import functools
import math

import jax
import jax.numpy as jnp
from jax import lax
from jax.experimental import pallas as pl
from jax.experimental.pallas import tpu as pltpu

F32 = jnp.float32
BF16 = jnp.bfloat16

D_MODEL = 1024
N_FOX_HEADS = 8
FOX_HEAD_DIM = 64
FOX_WIDTH = N_FOX_HEADS * FOX_HEAD_DIM
N_MEM = 256
N_MEM_HEADS = 4
MEM_HEAD_DIM = 128
MEM_WIDTH = N_MEM_HEADS * MEM_HEAD_DIM
SSM_GROUP = 16
SSM_WIDTH = 512
N_SSM_GROUPS = SSM_WIDTH // SSM_GROUP
SSM_STATE = 64
N_EXPERT_GROUPS = 4
EXPERTS_PER_GROUP = 8
N_EXPERTS = N_EXPERT_GROUPS * EXPERTS_PER_GROUP
D_EXPERT = 256
RMS_EPS = 1e-6
NEG_INF = -1e30
LOG2E = 1.4426950408889634

LANES = 128
SSM_CHUNK = 16
SSM_GPB = LANES // SSM_GROUP
GROUP_LANE = N_EXPERTS
MOE_SUB = 128
MOE_EPS = 4
VMEM_LIMIT = 56 * 1024 * 1024


def _dot(a, b):
    return jnp.dot(a, b, preferred_element_type=F32)


def _dot_nt(a, b):
    return lax.dot_general(a, b, (((1,), (1,)), ((), ())), preferred_element_type=F32)


def _dot_exact(a, b):
    return jnp.dot(a, b, preferred_element_type=F32, precision=lax.Precision.HIGHEST)


def _split_bf16(x):
    hi = x.astype(BF16)
    lo = (x - hi.astype(F32)).astype(BF16)
    return hi, lo


def _params(sem):
    return pltpu.CompilerParams(dimension_semantics=sem, vmem_limit_bytes=VMEM_LIMIT)


def _full(shape):
    n = len(shape)
    return pl.BlockSpec(shape, lambda *_: (0,) * n)


def _inproj_kernel(x_ref, g_ref, wqkv_ref, wf_ref, bf_ref, wqm_ref, wu_ref, wg_ref,
                   qn_ref, kn_ref, qmn_ref, bd_ref,
                   q_ref, kb_ref, vb_ref, k_ref, v_ref, lf_ref, qm_ref, u_ref, gate_ref, *, kv_transposed):
    x = x_ref[...]
    h = x * lax.rsqrt(jnp.mean(x * x, axis=-1, keepdims=True) + RMS_EPS) * g_ref[...]
    hb = h.astype(BF16)

    def head_norm(z, gain):
        ss = _dot((z * z).astype(BF16), bd_ref[...])
        return z * lax.rsqrt(ss * (1.0 / FOX_HEAD_DIM) + RMS_EPS) * gain

    zq = _dot(hb, wqkv_ref[:, 0:FOX_WIDTH])
    q_ref[...] = (head_norm(zq, qn_ref[...]) * (LOG2E * FOX_HEAD_DIM ** -0.5)).astype(BF16)
    zk = _dot(hb, wqkv_ref[:, FOX_WIDTH:2 * FOX_WIDTH])
    kn = head_norm(zk, kn_ref[...])
    zv = _dot(hb, wqkv_ref[:, 2 * FOX_WIDTH:3 * FOX_WIDTH])
    kb_ref[...] = kn.astype(BF16)
    vb_ref[...] = zv.astype(BF16)
    tm = x_ref.shape[0]
    if kv_transposed:
        k_ref[0] = kn.T
        v_ref[0] = zv.T
    else:
        for hd in range(N_FOX_HEADS):
            hs = slice(hd * FOX_HEAD_DIM, (hd + 1) * FOX_HEAD_DIM)
            rows = pl.ds(hd, tm, stride=N_FOX_HEADS)
            k_ref[rows, :] = kn[:, hs]
            v_ref[rows, :] = zv[:, hs]

    zf = (_dot(hb, wf_ref[...]) + bf_ref[...]).T[0:N_FOX_HEADS, :]
    lf_ref[...] = jnp.minimum(zf, 0.0) - jnp.log1p(jnp.exp(-jnp.abs(zf)))

    zm = _dot(hb, wqm_ref[...])
    for hd in range(N_MEM_HEADS):
        sl = slice(hd * MEM_HEAD_DIM, (hd + 1) * MEM_HEAD_DIM)
        zh = zm[:, sl]
        ms = jnp.mean(zh * zh, axis=-1, keepdims=True)
        qm_ref[:, sl] = (zh * lax.rsqrt(ms + RMS_EPS) * qmn_ref[...] * (MEM_HEAD_DIM ** -0.5)).astype(BF16)

    u_ref[...] = _dot(hb, wu_ref[...])
    for c in range(3):
        sl = slice(c * D_MODEL, (c + 1) * D_MODEL)
        gate_ref[:, sl] = (0.5 * jnp.tanh(0.5 * _dot(hb, wg_ref[:, sl])) + 0.5).astype(BF16)


def _inproj(x2d, w, tm, seq, kv_transposed):
    n = x2d.shape[0]
    assert n % tm == 0
    row = lambda width: pl.BlockSpec((tm, width), lambda i: (i, 0))
    if kv_transposed:
        assert seq % tm == 0
        per = seq // tm
        kv_shape = jax.ShapeDtypeStruct((n // seq, FOX_WIDTH, seq), F32)
        heads = pl.BlockSpec((1, FOX_WIDTH, tm), lambda i: (i // per, 0, i % per))
    else:
        kv_shape = jax.ShapeDtypeStruct((n * N_FOX_HEADS, FOX_HEAD_DIM), F32)
        heads = pl.BlockSpec((tm * N_FOX_HEADS, FOX_HEAD_DIM), lambda i: (i, 0))
    ins = [x2d, w['norm_mix'], w['wqkv'], w['wf'], w['bf'], w['wqm'], w['wu'], w['wg'],
           w['qn_fox'], w['kn_fox'], w['qn_mem'], w['bd']]
    in_specs = [row(D_MODEL)] + [_full(a.shape) for a in ins[1:]]
    out_shape = (
        jax.ShapeDtypeStruct((n, FOX_WIDTH), BF16),
        jax.ShapeDtypeStruct((n, FOX_WIDTH), BF16),
        jax.ShapeDtypeStruct((n, FOX_WIDTH), BF16),
        kv_shape,
        kv_shape,
        jax.ShapeDtypeStruct((N_FOX_HEADS, n), F32),
        jax.ShapeDtypeStruct((n, MEM_WIDTH), BF16),
        jax.ShapeDtypeStruct((n, SSM_WIDTH), F32),
        jax.ShapeDtypeStruct((n, 3 * D_MODEL), BF16),
    )
    out_specs = (row(FOX_WIDTH), row(FOX_WIDTH), row(FOX_WIDTH), heads, heads,
                 pl.BlockSpec((N_FOX_HEADS, tm), lambda i: (0, i)),
                 row(MEM_WIDTH), row(SSM_WIDTH), row(3 * D_MODEL))
    return pl.pallas_call(
        functools.partial(_inproj_kernel, kv_transposed=kv_transposed),
        out_shape=out_shape, grid=(n // tm,), in_specs=in_specs, out_specs=out_specs,
        compiler_params=_params(("parallel",)), name="inproj")(*ins)


def _memkv_kernel(x_ref, g_ref, w_ref, kn_ref, k_ref, v_ref):
    x = x_ref[...]
    h = x * lax.rsqrt(jnp.mean(x * x, axis=-1, keepdims=True) + RMS_EPS) * g_ref[...]
    hb = h.astype(BF16)
    zk = _dot(hb, w_ref[:, 0:MEM_WIDTH])
    for hd in range(N_MEM_HEADS):
        sl = slice(hd * MEM_HEAD_DIM, (hd + 1) * MEM_HEAD_DIM)
        zh = zk[:, sl]
        ms = jnp.mean(zh * zh, axis=-1, keepdims=True)
        k_ref[:, sl] = zh * lax.rsqrt(ms + RMS_EPS) * kn_ref[...]
    v_ref[...] = _dot(hb, w_ref[:, MEM_WIDTH:2 * MEM_WIDTH])


def _memkv(mem2d, norm_mem, w_kv, kn_mem, tm):
    n = mem2d.shape[0]
    row = lambda width: pl.BlockSpec((tm, width), lambda i: (i, 0))
    return pl.pallas_call(
        _memkv_kernel,
        out_shape=(jax.ShapeDtypeStruct((n, MEM_WIDTH), F32), jax.ShapeDtypeStruct((n, MEM_WIDTH), F32)),
        grid=(n // tm,),
        in_specs=[row(D_MODEL), _full(norm_mem.shape), _full(w_kv.shape), _full(kn_mem.shape)],
        out_specs=(row(MEM_WIDTH), row(MEM_WIDTH)),
        compiler_params=_params(("parallel",)), name="memkv")(mem2d, norm_mem, w_kv, kn_mem)


CUMSUM_BLOCK = 256


def _cumsum_kernel(x_ref, o_ref):
    nblk = x_ref.shape[1] // CUMSUM_BLOCK
    r = lax.broadcasted_iota(jnp.int32, (CUMSUM_BLOCK, CUMSUM_BLOCK), 0)
    c = lax.broadcasted_iota(jnp.int32, (CUMSUM_BLOCK, CUMSUM_BLOCK), 1)
    tri = (r <= c).astype(F32)
    carry = jnp.zeros((x_ref.shape[0], 1), F32)
    for j in range(nblk):
        sl = slice(j * CUMSUM_BLOCK, (j + 1) * CUMSUM_BLOCK)
        cs = _dot_exact(x_ref[:, sl], tri) + carry
        o_ref[:, sl] = cs
        carry = cs[:, CUMSUM_BLOCK - 1:CUMSUM_BLOCK]


def _cumsum_rows(x):
    rows, n = x.shape
    npad = -(-n // CUMSUM_BLOCK) * CUMSUM_BLOCK
    xp = jnp.pad(x, ((0, 0), (0, npad - n))) if npad != n else x
    out = pl.pallas_call(
        _cumsum_kernel, out_shape=jax.ShapeDtypeStruct((rows, npad), F32), grid=(1,),
        in_specs=[_full((rows, npad))], out_specs=_full((rows, npad)),
        compiler_params=_params(("arbitrary",)), name="cumsum")(xp)
    return out[:, :n] if npad != n else out


def _reduce_rows(x, op):
    rows, cols = x.shape
    if rows > 64 and rows % 64 == 0:
        x = op(x.reshape(rows // 64, 64, cols), axis=0)
        rows = 64
    if rows == 64:
        x = op(x.reshape(8, 8, cols), axis=0)
    return op(x, axis=0, keepdims=True)


def _fox_prompt_kernel(q_ref, k_ref, v_ref, cr_ref, o_ref,
                       vt_ref, ck0_ref, ck1_ref, st0_ref, st1_ref, pt0_ref, pt1_ref, acc_ref, *, tq, tk):
    st_refs = (st0_ref, st1_ref)
    pt_refs = (pt0_ref, pt1_ref)
    i = pl.program_id(2)
    s_len = k_ref.shape[1]

    @pl.when(i == 0)
    def _():
        vt_ref[...] = v_ref[0].astype(F32).T.astype(BF16)
        ck0_ref[...] = jnp.broadcast_to(cr_ref[0, 0, 0:1, :], (LANES, s_len)).T
        ck1_ref[...] = jnp.broadcast_to(cr_ref[0, 0, 1:2, :], (LANES, s_len)).T

    qt = q_ref[0].astype(F32).T
    row = lax.broadcasted_iota(jnp.int32, (LANES, tq), 0)
    qts = (jnp.where(row < FOX_HEAD_DIM, qt, 0.0).astype(BF16), jnp.where(row < FOX_HEAD_DIM, 0.0, qt).astype(BF16))
    q0 = pl.multiple_of(i * tq, tq)
    cq = cr_ref[0, 0, :, pl.ds(q0, tq)]
    ck_refs = (ck0_ref, ck1_ref)

    def stage_a(n, par):
        s = pl.multiple_of(n * tk, tk)
        kb = k_ref[0, pl.ds(s, tk), :]
        for hh in range(2):
            ck = ck_refs[hh][pl.ds(s, tk), :]
            st_refs[par][hh] = _dot(kb, qts[hh]) - jnp.concatenate([ck] * (tq // LANES), axis=1)

    def stage_b(n, par, stats, masked):
        if masked:
            kpos = n * tk + lax.broadcasted_iota(jnp.int32, (tk, tq), 0)
            qpos = q0 + lax.broadcasted_iota(jnp.int32, (tk, tq), 1)
            mask = kpos <= qpos
        out = []
        for hh in range(2):
            m, l = stats[2 * hh:2 * hh + 2]
            t = st_refs[par][hh]
            if masked:
                t = jnp.where(mask, t, NEG_INF)
            cqh = cq[hh:hh + 1, :]
            m_new = jnp.maximum(m, _reduce_rows(t, jnp.max) + cqh)
            alpha = jnp.exp2(m - m_new)
            p = jnp.exp2(t + (cqh - m_new))
            pt_refs[par][hh] = p.astype(BF16)
            out.extend([m_new, alpha * l + _reduce_rows(p, jnp.sum), alpha])
        return tuple(out)

    def stage_c(n, par, alphas):
        s = pl.multiple_of(jnp.maximum(n, 0) * tk, tk)
        for hh in range(2):
            vt = vt_ref[hh * FOX_HEAD_DIM:(hh + 1) * FOX_HEAD_DIM, pl.ds(s, tk)]
            acc_ref[hh] = alphas[hh] * acc_ref[hh] + _dot(vt, pt_refs[par][hh])

    def iteration(n, par, carry):
        m0, l0, al0, m1, l1, al1 = carry
        stage_c(n - 1, 1 - par, (al0, al1))
        new = stage_b(n, par, (m0, l0, m1, l1), False)
        stage_a(n + 1, 1 - par)
        return new

    def finish(par, carry):
        m0, l0, al0, m1, l1, al1 = carry
        stage_c(nfull - 1, 1 - par, (al0, al1))
        _, l0, be0, _, l1, be1 = stage_b(nfull, par, (m0, l0, m1, l1), True)
        stage_c(nfull, par, (be0, be1))
        ot = jnp.concatenate([acc_ref[0] / l0, acc_ref[1] / l1], axis=0)
        o_ref[0] = ot.T.astype(o_ref.dtype)

    acc_ref[...] = jnp.zeros(acc_ref.shape, F32)
    pt1_ref[...] = jnp.zeros(pt1_ref.shape, BF16)
    neg = jnp.full((1, tq), NEG_INF, F32)
    zero = jnp.zeros((1, tq), F32)
    one = jnp.ones((1, tq), F32)
    nfull = (i * tq) // tk
    stage_a(0, 0)
    carry = lax.fori_loop(0, nfull // 2, lambda k, c: iteration(2 * k + 1, 1, iteration(2 * k, 0, c)),
                          (neg, zero, one, neg, zero, one))
    odd = nfull % 2 == 1
    carry = lax.cond(odd, lambda c: iteration(nfull - 1, 0, c), lambda c: c, carry)
    pl.when(odd)(lambda: finish(1, carry))
    pl.when(jnp.logical_not(odd))(lambda: finish(0, carry))


def _fox_prompt(q, k, v, c_row, tq, tk):
    b, s, _ = q.shape
    assert s % tk == 0 and tk % tq == 0
    npair = N_FOX_HEADS // 2
    return pl.pallas_call(
        functools.partial(_fox_prompt_kernel, tq=tq, tk=tk),
        out_shape=jax.ShapeDtypeStruct((b, s, FOX_WIDTH), BF16),
        grid=(b, npair, s // tq),
        in_specs=[
            pl.BlockSpec((1, tq, LANES), lambda bi, hp, i: (bi, i, hp)),
            pl.BlockSpec((1, s, LANES), lambda bi, hp, i: (bi, 0, hp)),
            pl.BlockSpec((1, s, LANES), lambda bi, hp, i: (bi, 0, hp)),
            pl.BlockSpec((1, 1, 2, s), lambda bi, hp, i: (bi, hp, 0, 0)),
        ],
        out_specs=pl.BlockSpec((1, tq, LANES), lambda bi, hp, i: (bi, i, hp)),
        scratch_shapes=[pltpu.VMEM((LANES, s), BF16),
                        pltpu.VMEM((s, LANES), F32), pltpu.VMEM((s, LANES), F32),
                        pltpu.VMEM((2, tk, tq), F32), pltpu.VMEM((2, tk, tq), F32),
                        pltpu.VMEM((2, tk, tq), BF16), pltpu.VMEM((2, tk, tq), BF16),
                        pltpu.VMEM((2, FOX_HEAD_DIM, tq), F32)],
        compiler_params=_params(("parallel", "parallel", "arbitrary")), name="fox_prompt")(q, k, v, c_row)


def _fox_sample_kernel(q_ref, ck_ref, cv_ref, nk_ref, nv_ref, cq_ref, crc_ref, crn_ref, o_ref, *state, n):
    j = pl.program_id(1)
    nj = pl.num_programs(1)
    m_refs = state[0:N_FOX_HEADS]
    l_refs = state[N_FOX_HEADS:2 * N_FOX_HEADS]
    acc_refs = state[2 * N_FOX_HEADS:3 * N_FOX_HEADS]

    @pl.when(j == 0)
    def _():
        for hd in range(N_FOX_HEADS):
            m_refs[hd][...] = jnp.full(m_refs[hd].shape, NEG_INF, F32)
            l_refs[hd][...] = jnp.zeros(l_refs[hd].shape, F32)
            acc_refs[hd][...] = jnp.zeros(acc_refs[hd].shape, F32)

    def update(k_of, v_of, cr_ref_, mask, transposed):
        qk = _dot if transposed else _dot_nt
        pv = _dot_nt if transposed else _dot
        ts = []
        for hd in range(N_FOX_HEADS):
            hs = slice(hd * FOX_HEAD_DIM, (hd + 1) * FOX_HEAD_DIM)
            t = qk(q_ref[0, :, hs], k_of(hd)) - cr_ref_[0, hd:hd + 1, :]
            ts.append(t if mask is None else jnp.where(mask, t, NEG_INF))
        ps = []
        for hd in range(N_FOX_HEADS):
            cq = cq_ref[0, :, hd:hd + 1]
            m = m_refs[hd][...]
            m_new = jnp.maximum(m, jnp.max(ts[hd], axis=-1, keepdims=True) + cq)
            alpha = jnp.exp2(m - m_new)
            p = jnp.exp2(ts[hd] + (cq - m_new))
            m_refs[hd][...] = m_new
            l_refs[hd][...] = alpha * l_refs[hd][...] + jnp.sum(p, axis=-1, keepdims=True)
            ps.append((alpha, p.astype(BF16)))
        for hd in range(N_FOX_HEADS):
            alpha, p = ps[hd]
            acc_refs[hd][...] = alpha * acc_refs[hd][...] + pv(p, v_of(hd))

    update(lambda hd: ck_ref[0, hd].astype(BF16), lambda hd: cv_ref[0, hd].astype(BF16), crc_ref, None, True)

    @pl.when(j == nj - 1)
    def _():
        r = lax.broadcasted_iota(jnp.int32, (n, n), 0)
        c = lax.broadcasted_iota(jnp.int32, (n, n), 1)
        head = lambda ref: (lambda hd: ref[0, :, hd * FOX_HEAD_DIM:(hd + 1) * FOX_HEAD_DIM])
        update(head(nk_ref), head(nv_ref), crn_ref, c <= r, False)
        for hd in range(N_FOX_HEADS):
            hs = slice(hd * FOX_HEAD_DIM, (hd + 1) * FOX_HEAD_DIM)
            o_ref[0, :, hs] = (acc_refs[hd][...] / l_refs[hd][...]).astype(o_ref.dtype)


def _fox_sample(q, cache_k, cache_v, k_new, v_new, c_q, c_row_cache, c_row_new, tk):
    b, n, _ = q.shape
    past = cache_k.shape[3]
    assert past % tk == 0
    cache_spec = pl.BlockSpec((1, N_FOX_HEADS, FOX_HEAD_DIM, tk), lambda bi, j: (bi, 0, 0, j))
    return pl.pallas_call(
        functools.partial(_fox_sample_kernel, n=n),
        out_shape=jax.ShapeDtypeStruct((b, n, FOX_WIDTH), BF16),
        grid=(b, past // tk),
        in_specs=[
            pl.BlockSpec((1, n, FOX_WIDTH), lambda bi, j: (bi, 0, 0)),
            cache_spec,
            cache_spec,
            pl.BlockSpec((1, n, FOX_WIDTH), lambda bi, j: (bi, 0, 0)),
            pl.BlockSpec((1, n, FOX_WIDTH), lambda bi, j: (bi, 0, 0)),
            pl.BlockSpec((1, n, N_FOX_HEADS), lambda bi, j: (bi, 0, 0)),
            pl.BlockSpec((1, N_FOX_HEADS, tk), lambda bi, j: (bi, 0, j)),
            pl.BlockSpec((1, N_FOX_HEADS, n), lambda bi, j: (bi, 0, 0)),
        ],
        out_specs=pl.BlockSpec((1, n, FOX_WIDTH), lambda bi, j: (bi, 0, 0)),
        scratch_shapes=([pltpu.VMEM((n, 1), F32)] * (2 * N_FOX_HEADS)
                        + [pltpu.VMEM((n, FOX_HEAD_DIM), F32)] * N_FOX_HEADS),
        compiler_params=_params(("parallel", "arbitrary")), name="fox_sample")(
            q, cache_k, cache_v, k_new, v_new, c_q, c_row_cache, c_row_new)


def _ssm_mats(p):
    f32 = F32
    a_re, a_im = p['ssm_a_re'].astype(f32), p['ssm_a_im'].astype(f32)
    b_re, b_im = p['ssm_b_re'].astype(f32), p['ssm_b_im'].astype(f32)
    c_re, c_im = p['ssm_c_re'].astype(f32), p['ssm_c_im'].astype(f32)
    dt = jnp.exp(p['ssm_log_dt'].astype(f32))[:, None]
    mag = jnp.exp(dt * a_re)
    ab_re = mag * jnp.cos(dt * a_im)
    ab_im = mag * jnp.sin(dt * a_im)
    den = a_re * a_re + a_im * a_im
    nr, ni = ab_re - 1.0, ab_im
    coef_re = (nr * a_re + ni * a_im) / den
    coef_im = (ni * a_re - nr * a_im) / den
    bb_re = coef_re[..., None] * b_re - coef_im[..., None] * b_im
    bb_im = coef_re[..., None] * b_im + coef_im[..., None] * b_re
    pr, pi = [jnp.ones_like(ab_re)], [jnp.zeros_like(ab_im)]
    for _ in range(SSM_CHUNK):
        pr.append(pr[-1] * ab_re - pi[-1] * ab_im)
        pi.append(pr[-2] * ab_im + pi[-1] * ab_re)
    pw_re, pw_im = jnp.stack(pr), jnp.stack(pi)
    T = SSM_CHUNK
    w_re = pw_re[..., None] * bb_re[None] - pw_im[..., None] * bb_im[None]
    w_im = pw_re[..., None] * bb_im[None] + pw_im[..., None] * bb_re[None]
    kk = (jnp.einsum('gop,kgpi->kgoi', c_re, w_re[:T], precision='highest')
          - jnp.einsum('gop,kgpi->kgoi', c_im, w_im[:T], precision='highest'))
    nq = N_SSM_GROUPS // SSM_GPB

    def group_diag(m):
        rows, c = m.shape[-2:]
        m = jnp.tile(m, (1,) * (m.ndim - 1) + (SSM_GPB,))
        same = (jnp.arange(rows) // (rows // SSM_GPB))[:, None] == (jnp.arange(SSM_GPB * c) // c)[None, :]
        return jnp.where(same, m, 0.0)

    def lane_diag(m):
        lead = m.shape[:-3]
        i, c = m.shape[-2:]
        return group_diag(m.reshape(lead + (nq, SSM_GPB * i, c)))

    ktau = lane_diag(jnp.swapaxes(kk, -1, -2))
    ktau = jnp.concatenate([jnp.zeros_like(ktau[:1]), ktau], axis=0)
    units = []
    for dlag in range(T // 2 - 1, -1, -1):
        top = jnp.concatenate([ktau[2 * dlag + 1], ktau[2 * dlag + 2]], axis=-1)
        bot = jnp.concatenate([ktau[2 * dlag], ktau[2 * dlag + 1]], axis=-1)
        units.append(jnp.concatenate([top, bot], axis=-2))
    kstack = jnp.concatenate(units, axis=-2).astype(BF16)
    rev = T - 1 - jnp.arange(T)
    def local_rows(w):
        w = jnp.transpose(w[rev], (1, 0, 3, 2)).reshape(nq, SSM_GPB, T, SSM_GROUP, SSM_STATE)
        return group_diag(jnp.transpose(w, (0, 2, 1, 3, 4)).reshape(nq, T, LANES, SSM_STATE))

    m_all = jnp.concatenate([local_rows(w_re), local_rows(w_im)], axis=-1)
    m_all = m_all.reshape(nq, T * LANES, 2 * SSM_GPB * SSM_STATE)
    m_hi, m_lo = _split_bf16(m_all)
    ar, ai = pw_re[1:], pw_im[1:]
    n_re = (c_re[None] * ar[:, :, None, :] - c_im[None] * ai[:, :, None, :])
    n_im = -(c_re[None] * ai[:, :, None, :] + c_im[None] * ar[:, :, None, :])

    def state_rows(n):
        n = jnp.transpose(n, (1, 3, 0, 2)).reshape(nq, SSM_GPB * SSM_STATE, T, SSM_GROUP)
        n = jnp.tile(n, (1, 1, 1, SSM_GPB))
        same = (jnp.arange(SSM_GPB * SSM_STATE) // SSM_STATE)[:, None, None] == (jnp.arange(LANES) // SSM_GROUP)
        return jnp.where(same, n, 0.0).reshape(nq, SSM_GPB * SSM_STATE, T * LANES)

    n_all = jnp.concatenate([state_rows(n_re), state_rows(n_im)], axis=1).astype(BF16)
    return dict(kstack=kstack, m_hi=m_hi, m_lo=m_lo, n_all=n_all,
                a16_re=pw_re[T].reshape(8, 256), a16_im=pw_im[T].reshape(8, 256),
                d=p['ssm_d'].astype(f32).reshape(1, SSM_WIDTH))


def _chunk_tokens(u_ref, rows):
    return [u_ref[pl.ds(t, rows, stride=SSM_CHUNK), :] for t in range(SSM_CHUNK)]


def _ssm_local_kernel(u_ref, mh_ref, ml_ref, hre_ref, him_ref):
    rows = hre_ref.shape[0]
    parts = [_split_bf16(ut) for ut in _chunk_tokens(u_ref, rows)]
    x_hi = jnp.concatenate([h for h, _ in parts], axis=1)
    x_lo = jnp.concatenate([l for _, l in parts], axis=1)
    h = _dot(x_hi, mh_ref[0]) + _dot(x_hi, ml_ref[0]) + _dot(x_lo, mh_ref[0])
    half = SSM_GPB * SSM_STATE
    hre_ref[...] = h[:, 0:half]
    him_ref[...] = h[:, half:2 * half]


def _ssm_local(u2d, mats, rows):
    n = u2d.shape[0]
    r = n // SSM_CHUNK
    nq = N_SSM_GROUPS // SSM_GPB
    half = SSM_GPB * SSM_STATE
    mspec = pl.BlockSpec((1, SSM_CHUNK * LANES, 2 * half), lambda q, i: (q, 0, 0))
    ospec = pl.BlockSpec((rows, half), lambda q, i: (i, q))
    return pl.pallas_call(
        _ssm_local_kernel,
        out_shape=(jax.ShapeDtypeStruct((r, N_SSM_GROUPS * SSM_STATE), F32),) * 2,
        grid=(nq, r // rows),
        in_specs=[pl.BlockSpec((rows * SSM_CHUNK, LANES), lambda q, i: (i, q)), mspec, mspec],
        out_specs=(ospec, ospec),
        compiler_params=_params(("parallel", "parallel")), name="ssm_local")(u2d, mats['m_hi'], mats['m_lo'])


def _ssm_scan_kernel(lre_ref, lim_ref, are_ref, aim_ref, h0re_ref, h0im_ref,
                     pre_ref, pim_ref, fre_ref, fim_ref):
    nchunk = lre_ref.shape[1]
    ar, ai = are_ref[...], aim_ref[...]

    def body(c, carry):
        hr, hi = carry
        pre_ref[0, c] = hr
        pim_ref[0, c] = hi
        return (ar * hr - ai * hi + lre_ref[0, c], ar * hi + ai * hr + lim_ref[0, c])

    hr, hi = lax.fori_loop(0, nchunk, body, (h0re_ref[0], h0im_ref[0]))
    fre_ref[0] = hr
    fim_ref[0] = hi


def _ssm_scan(hloc_re, hloc_im, mats, h0_re, h0_im):
    b, nchunk = hloc_re.shape[:2]
    big = pl.BlockSpec((1, nchunk, 8, 256), lambda i: (i, 0, 0, 0))
    small = pl.BlockSpec((1, 8, 256), lambda i: (i, 0, 0))
    return pl.pallas_call(
        _ssm_scan_kernel,
        out_shape=(jax.ShapeDtypeStruct(hloc_re.shape, F32),) * 2 + (jax.ShapeDtypeStruct((b, 8, 256), F32),) * 2,
        grid=(b,),
        in_specs=[big, big, _full((8, 256)), _full((8, 256)), small, small],
        out_specs=(big, big, small, small),
        compiler_params=_params(("parallel",)), name="ssm_scan")(
            hloc_re, hloc_im, mats['a16_re'], mats['a16_im'], h0_re, h0_im)


def _gelu_tanh(y):
    return 0.5 * y * (1.0 + jnp.tanh(math.sqrt(2.0 / math.pi) * (y + 0.044715 * (y * y * y))))


def _ssm_out_kernel(u_ref, k_ref, pre_ref, pim_ref, n_ref, d_ref, y_ref, ysc_ref):
    rows = pre_ref.shape[0]
    us = _chunk_tokens(u_ref, rows)
    x = jnp.concatenate([ut.astype(BF16) for ut in us], axis=1)
    hp = jnp.concatenate([pre_ref[...], pim_ref[...]], axis=1).astype(BF16)
    unit = 2 * LANES
    nunit = SSM_CHUNK // 2
    for j in range(nunit):
        yj = (_dot(x[:, 0:unit * (j + 1)], k_ref[0, unit * (nunit - 1 - j):, :])
              + _dot(hp, n_ref[0, :, unit * j:unit * (j + 1)]))
        for t2 in range(2):
            t = 2 * j + t2
            y = yj[:, t2 * LANES:(t2 + 1) * LANES] + d_ref[...] * us[t]
            ysc_ref[pl.ds(t, rows, stride=SSM_CHUNK), :] = _gelu_tanh(y)
    y_ref[...] = ysc_ref[...].astype(y_ref.dtype)


def _ssm_out(u2d, hprev_re, hprev_im, mats, rows):
    n = u2d.shape[0]
    r = n // SSM_CHUNK
    nq = N_SSM_GROUPS // SSM_GPB
    half = SSM_GPB * SSM_STATE
    uspec = pl.BlockSpec((rows * SSM_CHUNK, LANES), lambda q, i: (i, q))
    hspec = pl.BlockSpec((rows, half), lambda q, i: (i, q))
    return pl.pallas_call(
        _ssm_out_kernel,
        out_shape=jax.ShapeDtypeStruct((n, SSM_WIDTH), BF16),
        grid=(nq, r // rows),
        in_specs=[uspec, pl.BlockSpec((1, SSM_CHUNK * LANES, 2 * LANES), lambda q, i: (q, 0, 0)), hspec, hspec,
                  pl.BlockSpec((1, 2 * half, SSM_CHUNK * LANES), lambda q, i: (q, 0, 0)),
                  pl.BlockSpec((1, LANES), lambda q, i: (0, q))],
        out_specs=uspec,
        scratch_shapes=[pltpu.VMEM((rows * SSM_CHUNK, LANES), F32)],
        compiler_params=_params(("parallel", "parallel")), name="ssm_out")(
            u2d, mats['kstack'], hprev_re, hprev_im, mats['n_all'], mats['d'])


def _ssm(u2d, b, h0_re, h0_im, mats):
    n = u2d.shape[0]
    nchunk = n // b // SSM_CHUNK
    r = b * nchunk
    rows = _pick_tile(r, 256)
    hloc_re, hloc_im = _ssm_local(u2d, mats, rows)
    shp = (b, nchunk, 8, 256)
    hprev_re, hprev_im, f_re, f_im = _ssm_scan(hloc_re.reshape(shp), hloc_im.reshape(shp), mats,
                                               h0_re.reshape(b, 8, 256), h0_im.reshape(b, 8, 256))
    y = _ssm_out(u2d, hprev_re.reshape(r, -1), hprev_im.reshape(r, -1), mats, rows)
    return y, f_re.reshape(b, N_SSM_GROUPS, SSM_STATE), f_im.reshape(b, N_SSM_GROUPS, SSM_STATE)


def _merge_kernel(x_ref, of_ref, ys_ref, qm_ref, gate_ref, mk_ref, mv_ref,
                  wglu_ref, wbf_ref, wbs_ref, wbm_ref, wo_ref, nf_ref, wr_ref,
                  x1_ref, h2_ref, r_ref):
    tm = x_ref.shape[0]
    om = []
    for hd in range(N_MEM_HEADS):
        sl = slice(hd * MEM_HEAD_DIM, (hd + 1) * MEM_HEAD_DIM)
        kh = mk_ref[0, :, sl].astype(BF16)
        vh = mv_ref[0, :, sl].astype(BF16)
        sc = _dot_nt(qm_ref[:, sl], kh)
        p = jnp.exp(sc - jnp.max(sc, axis=-1, keepdims=True))
        om.append(_dot(p.astype(BF16), vh) / jnp.sum(p, axis=-1, keepdims=True))
    o_mem = jnp.concatenate(om, axis=-1).astype(BF16)
    z = _dot(ys_ref[...], wglu_ref[...])
    y_ssm = (z[:, 0:SSM_WIDTH] * jax.nn.sigmoid(z[:, SSM_WIDTH:2 * SSM_WIDTH])).astype(BF16)
    g = lambda c: gate_ref[:, c * D_MODEL:(c + 1) * D_MODEL].astype(F32)
    merged = (g(0) * _dot(of_ref[...], wbf_ref[...]) + g(1) * _dot(y_ssm, wbs_ref[...])
              + g(2) * _dot(o_mem, wbm_ref[...]))
    x1 = x_ref[...] + _dot(merged.astype(BF16), wo_ref[...])
    x1_ref[...] = x1
    h2 = x1 * lax.rsqrt(jnp.mean(x1 * x1, axis=-1, keepdims=True) + RMS_EPS) * nf_ref[...]
    h2_ref[...] = h2.astype(BF16)
    h2_hi, h2_lo = _split_bf16(h2)
    hw = _dot(h2_hi, wr_ref[...])
    logits = hw[:, 0:LANES] + hw[:, LANES:2 * LANES] + _dot(h2_lo, wr_ref[:, 0:LANES])
    lane = lax.broadcasted_iota(jnp.int32, (tm, LANES), 1)
    big = jnp.int32(LANES)
    is_grp = (lane >= N_EXPERTS) & (lane < N_EXPERTS + N_EXPERT_GROUPS)
    gl = jnp.where(is_grp, logits, NEG_INF)
    gmax = jnp.max(gl, axis=-1, keepdims=True)
    grp = jnp.min(jnp.where(is_grp & (gl == gmax), lane, big), axis=-1, keepdims=True) - N_EXPERTS
    g_w = 1.0 / jnp.sum(jnp.where(is_grp, jnp.exp(gl - gmax), 0.0), axis=-1, keepdims=True)
    in_grp = (lane >= grp * EXPERTS_PER_GROUP) & (lane < (grp + 1) * EXPERTS_PER_GROUP)
    e1 = jnp.where(in_grp, logits, NEG_INF)
    m1 = jnp.max(e1, axis=-1, keepdims=True)
    i1 = jnp.min(jnp.where(in_grp & (e1 == m1), lane, big), axis=-1, keepdims=True)
    rest = in_grp & (lane != i1)
    e2 = jnp.where(rest, logits, NEG_INF)
    m2 = jnp.max(e2, axis=-1, keepdims=True)
    i2 = jnp.min(jnp.where(rest & (e2 == m2), lane, big), axis=-1, keepdims=True)
    ex = jnp.exp(m2 - m1)
    w1 = g_w / (1.0 + ex)
    w2 = g_w * ex / (1.0 + ex)
    r_ref[...] = jnp.where(lane == i1, w1, jnp.where(lane == i2, w2, jnp.where(lane == GROUP_LANE, grp.astype(F32), 0.0)))


def _merge(x2d, o_fox, ys, q_m, gates, mem_k, mem_v, w, tm, rows_per_batch):
    n = x2d.shape[0]
    assert n % tm == 0 and rows_per_batch % tm == 0
    per = rows_per_batch // tm
    row = lambda width: pl.BlockSpec((tm, width), lambda i: (i, 0))
    memspec = pl.BlockSpec((1, N_MEM, MEM_WIDTH), lambda i: (i // per, 0, 0))
    ws = [w['w_glu'], w['w_br_fox'], w['w_br_ssm'], w['w_br_mem'], w['w_out'], w['norm_ffn'], w['w_router']]
    return pl.pallas_call(
        _merge_kernel,
        out_shape=(jax.ShapeDtypeStruct((n, D_MODEL), F32), jax.ShapeDtypeStruct((n, D_MODEL), BF16),
                   jax.ShapeDtypeStruct((n, LANES), F32)),
        grid=(n // tm,),
        in_specs=[row(D_MODEL), row(FOX_WIDTH), row(SSM_WIDTH), row(MEM_WIDTH), row(3 * D_MODEL), memspec, memspec]
                 + [_full(a.shape) for a in ws],
        out_specs=(row(D_MODEL), row(D_MODEL), row(LANES)),
        compiler_params=_params(("parallel",)), name="merge")(
            x2d, o_fox, ys, q_m, gates, mem_k, mem_v, *ws)


def _moe_kernel(h_ref, r_ref, x1_ref, tri_ref, wg_ref, wu_ref, wd_ref, o_ref,
                xs_ref, cw_ref, og_ref, acc_ref, rank_ref, nblk_ref, *, main):
    step = pl.program_id(1)
    steps_per_group = EXPERTS_PER_GROUP // MOE_EPS
    g = step // steps_per_group
    tm = h_ref.shape[0]
    gf = g.astype(F32)
    bounds = [0, main] + list(range(-(-main // MOE_SUB) * MOE_SUB, tm, MOE_SUB)) + [tm]
    bounds = sorted(set(bounds))
    blocks = [(lo, hi - lo, lo > 0) for lo, hi in zip(bounds[:-1], bounds[1:])]

    def guarded(r0, fn):
        if r0 == 0:
            fn()
        else:
            pl.when(r0 < nblk_ref[0])(fn)

    @pl.when(step == 0)
    def _():
        acc_ref[...] = jnp.zeros(acc_ref.shape, F32)

    @pl.when(step % steps_per_group == 0)
    def _():
        rt = r_ref[...]
        rtt = rt.T
        mrow = rtt[GROUP_LANE:GROUP_LANE + 1, :] == gf
        m8 = jnp.broadcast_to(jnp.where(mrow, 1.0, 0.0), (8, tm))
        rank8 = _dot(m8.astype(BF16), tri_ref[...])
        rank_row = jnp.where(mrow, rank8[0:1, :], -1.0)
        rank_ref[...] = jnp.broadcast_to(jnp.where(mrow, rank8, -1.0).T[:, 0:1], rank_ref.shape)
        nblk_ref[0] = jnp.sum(jnp.where(mrow, 1, 0))
        hilo = jnp.concatenate(_split_bf16(rt), axis=1)
        for r0, nrows, _ in blocks:
            def compact(r0=r0, nrows=nrows):
                rows = slice(r0, r0 + nrows)
                slot = r0 + lax.broadcasted_iota(jnp.int32, (nrows, tm), 0)
                perm = jnp.where(rank_row == slot.astype(F32), 1.0, 0.0).astype(BF16)
                xs_ref[rows, :] = _dot(perm, h_ref[...]).astype(BF16)
                cw = _dot(perm, hilo)
                cw_ref[rows, :] = cw[:, 0:LANES] + cw[:, LANES:2 * LANES]
                og_ref[rows, :] = jnp.zeros((nrows, D_MODEL), F32)
            guarded(r0, compact)

    for k in range(MOE_EPS):
        e = step * MOE_EPS + k
        for r0, nrows, _ in blocks:
            def expert(r0=r0, nrows=nrows, k=k, e=e):
                rows = slice(r0, r0 + nrows)
                x = xs_ref[rows, :]
                a = _dot(x, wg_ref[k])
                up = _dot(x, wu_ref[k])
                lane = lax.broadcasted_iota(jnp.int32, (nrows, LANES), 1)
                ce = jnp.sum(jnp.where(lane == e, cw_ref[rows, :], 0.0), axis=-1, keepdims=True)
                act = a * jax.nn.sigmoid(a) * up * ce
                og_ref[rows, :] += _dot(act.astype(BF16), wd_ref[k])
            guarded(r0, expert)

    @pl.when(step % steps_per_group == steps_per_group - 1)
    def _():
        for r0, nrows, _ in blocks:
            def scatter_back(r0=r0, nrows=nrows):
                rows = slice(r0, r0 + nrows)
                slot = r0 + lax.broadcasted_iota(jnp.int32, (tm, nrows), 1)
                back = jnp.where(rank_ref[:, 0:1] == slot.astype(F32), 1.0, 0.0).astype(BF16)
                acc_ref[...] += _dot(back, og_ref[rows, :].astype(BF16))
            guarded(r0, scatter_back)

    @pl.when(step == pl.num_programs(1) - 1)
    def _():
        o_ref[...] = x1_ref[...] + acc_ref[...]


def _moe(h2, route, x1, wg, wu, wd, tm):
    n = h2.shape[0]
    assert n % tm == 0 and tm % MOE_SUB == 0
    main = max(MOE_SUB // 2, (5 * tm // 16) // 64 * 64)
    row = lambda width: pl.BlockSpec((tm, width), lambda i, s: (i, 0))
    r = jnp.arange(tm)
    tri = (r[:, None] < r[None, :]).astype(BF16)
    return pl.pallas_call(
        functools.partial(_moe_kernel, main=main),
        out_shape=jax.ShapeDtypeStruct((n, D_MODEL), F32),
        grid=(n // tm, N_EXPERTS // MOE_EPS),
        in_specs=[row(D_MODEL), row(LANES), row(D_MODEL), pl.BlockSpec((tm, tm), lambda i, s: (0, 0)),
                  pl.BlockSpec((MOE_EPS, D_MODEL, D_EXPERT), lambda i, s: (s, 0, 0)),
                  pl.BlockSpec((MOE_EPS, D_MODEL, D_EXPERT), lambda i, s: (s, 0, 0)),
                  pl.BlockSpec((MOE_EPS, D_EXPERT, D_MODEL), lambda i, s: (s, 0, 0))],
        out_specs=row(D_MODEL),
        scratch_shapes=[pltpu.VMEM((tm, D_MODEL), BF16), pltpu.VMEM((tm, LANES), F32), pltpu.VMEM((tm, D_MODEL), F32),
                        pltpu.VMEM((tm, D_MODEL), F32), pltpu.VMEM((tm, LANES), F32), pltpu.SMEM((1,), jnp.int32)],
        compiler_params=_params(("parallel", "arbitrary")), name="moe")(h2, route, x1, tri, wg, wu, wd)


def _prep_weights(p):
    w_in = p['w_in'].astype(BF16)
    o = 0
    wqkv = w_in[:, 0:3 * FOX_WIDTH]
    o = 3 * FOX_WIDTH
    wf = jnp.pad(w_in[:, o:o + N_FOX_HEADS], ((0, 0), (0, LANES - N_FOX_HEADS)))
    o += N_FOX_HEADS
    wqm = w_in[:, o:o + MEM_WIDTH]
    o += MEM_WIDTH
    wu = w_in[:, o:o + SSM_WIDTH]
    o += SSM_WIDTH
    wg = w_in[:, o:o + 3 * D_MODEL]
    r = jnp.arange(FOX_WIDTH) // FOX_HEAD_DIM
    bd = (r[:, None] == r[None, :]).astype(BF16)
    w_router = jnp.concatenate(
        [p['w_router_expert'], p['w_router_group'],
         jnp.zeros((D_MODEL, LANES - N_EXPERTS - N_EXPERT_GROUPS), F32)], axis=1)
    w_router = jnp.concatenate(_split_bf16(w_router), axis=1)
    return dict(
        norm_mix=p['norm_mix'].reshape(1, D_MODEL), wqkv=wqkv, wf=wf,
        bf=jnp.pad(p['b_forget'], (0, LANES - N_FOX_HEADS)).reshape(1, LANES),
        wqm=wqm, wu=wu, wg=wg,
        qn_fox=jnp.tile(p['qn_fox'], N_FOX_HEADS).reshape(1, FOX_WIDTH),
        kn_fox=jnp.tile(p['kn_fox'], N_FOX_HEADS).reshape(1, FOX_WIDTH),
        qn_mem=p['qn_mem'].reshape(1, MEM_HEAD_DIM), bd=bd,
        w_glu=p['w_glu'].astype(BF16), w_br_fox=p['w_br_fox'].astype(BF16),
        w_br_ssm=p['w_br_ssm'].astype(BF16), w_br_mem=p['w_br_mem'].astype(BF16),
        w_out=p['w_out'].astype(BF16), norm_ffn=p['norm_ffn'].reshape(1, D_MODEL), w_router=w_router,
        moe_wg=p['moe_w_gate'].astype(BF16), moe_wu=p['moe_w_up'].astype(BF16),
        moe_wd=p['moe_w_down'].astype(BF16))


def _pick_tile(n, target):
    t = min(n, target)
    while n % t:
        t //= 2
    return t


def _group(x, w, mats, mem_k, mem_v, h0_re, h0_im, cache):
    b, s, _ = x.shape
    n = b * s
    x2d = x.reshape(n, D_MODEL)
    prompt = cache is None
    q, kb, vb, k_out, v_out, lf_t, q_m, u, gates = _inproj(x2d, w, _pick_tile(s if prompt else n, 512), s, prompt)
    q3 = q.reshape(b, s, FOX_WIDTH)
    k3 = kb.reshape(b, s, FOX_WIDTH)
    v3 = vb.reshape(b, s, FOX_WIDTH)
    lf_rows = lf_t.reshape(N_FOX_HEADS, b, s).transpose(1, 0, 2)
    lf3 = lf_rows.transpose(0, 2, 1)
    npair = N_FOX_HEADS // 2
    if prompt:
        c_row = LOG2E * _cumsum_rows(lf_rows.reshape(b * N_FOX_HEADS, s)).reshape(b, npair, 2, s)
        o_fox = _fox_prompt(q3, k3, v3, c_row, _pick_tile(s, 512), _pick_tile(s, 512))
        unt = lambda a: a.reshape(b, N_FOX_HEADS, FOX_HEAD_DIM, s).transpose(0, 3, 1, 2)
        k4, v4 = unt(k_out), unt(v_out)
    else:
        cache_k, cache_v, cache_logf = cache
        past = cache_k.shape[1]
        lf_all = jnp.concatenate([cache_logf.astype(F32).transpose(0, 2, 1), lf_rows], axis=2)
        c_row = LOG2E * _cumsum_rows(lf_all.reshape(b * N_FOX_HEADS, past + s)).reshape(b, N_FOX_HEADS, past + s)
        o_fox = _fox_sample(q3, cache_k.transpose(0, 2, 3, 1), cache_v.transpose(0, 2, 3, 1), k3, v3,
                            c_row[:, :, past:].transpose(0, 2, 1), c_row[:, :, :past], c_row[:, :, past:],
                            _pick_tile(past, 1024))
        k4 = k_out.reshape(b, s, N_FOX_HEADS, FOX_HEAD_DIM)
        v4 = v_out.reshape(b, s, N_FOX_HEADS, FOX_HEAD_DIM)
    ys, f_re, f_im = _ssm(u, b, h0_re, h0_im, mats)
    tm = _pick_tile(s, 512)
    x1, h2, route = _merge(x2d, o_fox.reshape(n, FOX_WIDTH), ys, q_m, gates, mem_k, mem_v, w, tm, s)
    y = _moe(h2, route, x1, w['moe_wg'], w['moe_wu'], w['moe_wd'], _pick_tile(n, 1024))
    return y.reshape(b, s, D_MODEL), k4, v4, lf3, f_re, f_im


def kernel(x_prompt, x_sample, mem_prompt, cache_fox_k, cache_fox_v, cache_fox_logf, state_ssm_re, state_ssm_im,
           cache_mem_k, cache_mem_v, norm_mix, w_in, b_forget, qn_fox, kn_fox, qn_mem, kn_mem, norm_mem, w_mem_kv,
           ssm_a_re, ssm_a_im, ssm_log_dt, ssm_b_re, ssm_b_im, ssm_c_re, ssm_c_im, ssm_d, w_glu, w_br_fox,
           w_br_ssm, w_br_mem, w_out, norm_ffn, w_router_group, w_router_expert, moe_w_gate, moe_w_up,
           moe_w_down):
    depth = norm_mix.shape[0]
    assert depth == 1
    l = 0
    p = dict(norm_mix=norm_mix[l], w_in=w_in[l], b_forget=b_forget[l], qn_fox=qn_fox[l], kn_fox=kn_fox[l],
             qn_mem=qn_mem[l], ssm_a_re=ssm_a_re[l], ssm_a_im=ssm_a_im[l], ssm_log_dt=ssm_log_dt[l],
             ssm_b_re=ssm_b_re[l], ssm_b_im=ssm_b_im[l], ssm_c_re=ssm_c_re[l], ssm_c_im=ssm_c_im[l],
             ssm_d=ssm_d[l], w_glu=w_glu[l], w_br_fox=w_br_fox[l], w_br_ssm=w_br_ssm[l], w_br_mem=w_br_mem[l],
             w_out=w_out[l], norm_ffn=norm_ffn[l], w_router_group=w_router_group[l],
             w_router_expert=w_router_expert[l], moe_w_gate=moe_w_gate[l], moe_w_up=moe_w_up[l],
             moe_w_down=moe_w_down[l])
    w = _prep_weights(p)
    mats = _ssm_mats(p)
    bp, sp, _ = x_prompt.shape
    bs, ss, _ = x_sample.shape

    mk, mv = _memkv(mem_prompt.reshape(bp * N_MEM, D_MODEL), norm_mem[l].reshape(1, D_MODEL),
                    w_mem_kv[l].astype(BF16), kn_mem[l].reshape(1, MEM_HEAD_DIM), _pick_tile(bp * N_MEM, 512))
    mk = mk.reshape(bp, N_MEM, MEM_WIDTH)
    mv = mv.reshape(bp, N_MEM, MEM_WIDTH)
    zeros = jnp.zeros((bp, N_SSM_GROUPS, SSM_STATE), F32)
    yp, pk, pv, plf, pre, pim = _group(x_prompt, w, mats, mk, mv, zeros, zeros, None)
    cache = (cache_fox_k[l], cache_fox_v[l], cache_fox_logf[l])
    ys, sk, sv, slf, sre, sim = _group(
        x_sample, w, mats, cache_mem_k[l].reshape(bs, N_MEM, MEM_WIDTH), cache_mem_v[l].reshape(bs, N_MEM, MEM_WIDTH),
        state_ssm_re[l].astype(F32), state_ssm_im[l].astype(F32), cache)
    st = lambda a: a[None]
    return (yp, ys, st(pk), st(pv), st(plf), st(pre), st(pim),
            st(mk.reshape(bp, N_MEM, N_MEM_HEADS, MEM_HEAD_DIM)), st(mv.reshape(bp, N_MEM, N_MEM_HEADS, MEM_HEAD_DIM)),
            st(sk), st(sv), st(slf), st(sre), st(sim))
```

```python
import functools
import math

import jax
import jax.numpy as jnp
from jax import lax
from jax.experimental import pallas as pl
from jax.experimental.pallas import tpu as pltpu

F32 = jnp.float32
BF16 = jnp.bfloat16

D_MODEL = 1024
N_FOX_HEADS = 8
FOX_HEAD_DIM = 64
FOX_WIDTH = N_FOX_HEADS * FOX_HEAD_DIM
N_MEM = 256
N_MEM_HEADS = 4
MEM_HEAD_DIM = 128
MEM_WIDTH = N_MEM_HEADS * MEM_HEAD_DIM
SSM_GROUP = 16
SSM_WIDTH = 512
N_SSM_GROUPS = SSM_WIDTH // SSM_GROUP
SSM_STATE = 64
N_EXPERT_GROUPS = 4
EXPERTS_PER_GROUP = 8
N_EXPERTS = N_EXPERT_GROUPS * EXPERTS_PER_GROUP
D_EXPERT = 256
RMS_EPS = 1e-6
NEG_INF = -1e30
LOG2E = 1.4426950408889634

LANES = 128
SSM_CHUNK = 16
SSM_GPB = LANES // SSM_GROUP
GROUP_LANE = N_EXPERTS
MOE_SUB = 128
MOE_EPS = 4
VMEM_LIMIT = 56 * 1024 * 1024


def _dot(a, b):
    return jnp.dot(a, b, preferred_element_type=F32)


def _dot_nt(a, b):
    return lax.dot_general(a, b, (((1,), (1,)), ((), ())), preferred_element_type=F32)


def _dot_exact(a, b):
    return jnp.dot(a, b, preferred_element_type=F32, precision=lax.Precision.HIGHEST)


def _split_bf16(x):
    hi = x.astype(BF16)
    lo = (x - hi.astype(F32)).astype(BF16)
    return hi, lo


def _params(sem):
    return pltpu.CompilerParams(dimension_semantics=sem, vmem_limit_bytes=VMEM_LIMIT)


def _full(shape):
    n = len(shape)
    return pl.BlockSpec(shape, lambda *_: (0,) * n)


def _inproj_kernel(x_ref, g_ref, wqkv_ref, wf_ref, bf_ref, wqm_ref, wu_ref, wg_ref,
                   qn_ref, kn_ref, qmn_ref, bd_ref,
                   q_ref, kb_ref, vb_ref, k_ref, v_ref, lf_ref, qm_ref, u_ref, gate_ref, *, kv_transposed):
    x = x_ref[...]
    h = x * lax.rsqrt(jnp.mean(x * x, axis=-1, keepdims=True) + RMS_EPS) * g_ref[...]
    hb = h.astype(BF16)

    def head_norm(z, gain):
        ss = _dot((z * z).astype(BF16), bd_ref[...])
        return z * lax.rsqrt(ss * (1.0 / FOX_HEAD_DIM) + RMS_EPS) * gain

    zq = _dot(hb, wqkv_ref[:, 0:FOX_WIDTH])
    q_ref[...] = (head_norm(zq, qn_ref[...]) * (LOG2E * FOX_HEAD_DIM ** -0.5)).astype(BF16)
    zk = _dot(hb, wqkv_ref[:, FOX_WIDTH:2 * FOX_WIDTH])
    kn = head_norm(zk, kn_ref[...])
    zv = _dot(hb, wqkv_ref[:, 2 * FOX_WIDTH:3 * FOX_WIDTH])
    kb_ref[...] = kn.astype(BF16)
    vb_ref[...] = zv.astype(BF16)
    tm = x_ref.shape[0]
    if kv_transposed:
        k_ref[0] = kn.T
        v_ref[0] = zv.T
    else:
        for hd in range(N_FOX_HEADS):
            hs = slice(hd * FOX_HEAD_DIM, (hd + 1) * FOX_HEAD_DIM)
            rows = pl.ds(hd, tm, stride=N_FOX_HEADS)
            k_ref[rows, :] = kn[:, hs]
            v_ref[rows, :] = zv[:, hs]

    zf = (_dot(hb, wf_ref[...]) + bf_ref[...]).T[0:N_FOX_HEADS, :]
    lf_ref[...] = jnp.minimum(zf, 0.0) - jnp.log1p(jnp.exp(-jnp.abs(zf)))

    zm = _dot(hb, wqm_ref[...])
    for hd in range(N_MEM_HEADS):
        sl = slice(hd * MEM_HEAD_DIM, (hd + 1) * MEM_HEAD_DIM)
        zh = zm[:, sl]
        ms = jnp.mean(zh * zh, axis=-1, keepdims=True)
        qm_ref[:, sl] = (zh * lax.rsqrt(ms + RMS_EPS) * qmn_ref[...] * (MEM_HEAD_DIM ** -0.5)).astype(BF16)

    u_ref[...] = _dot(hb, wu_ref[...])
    for c in range(3):
        sl = slice(c * D_MODEL, (c + 1) * D_MODEL)
        gate_ref[:, sl] = (0.5 * jnp.tanh(0.5 * _dot(hb, wg_ref[:, sl])) + 0.5).astype(BF16)


def _inproj(x2d, w, tm, seq, kv_transposed):
    n = x2d.shape[0]
    assert n % tm == 0
    row = lambda width: pl.BlockSpec((tm, width), lambda i: (i, 0))
    if kv_transposed:
        assert seq % tm == 0
        per = seq // tm
        kv_shape = jax.ShapeDtypeStruct((n // seq, FOX_WIDTH, seq), F32)
        heads = pl.BlockSpec((1, FOX_WIDTH, tm), lambda i: (i // per, 0, i % per))
    else:
        kv_shape = jax.ShapeDtypeStruct((n * N_FOX_HEADS, FOX_HEAD_DIM), F32)
        heads = pl.BlockSpec((tm * N_FOX_HEADS, FOX_HEAD_DIM), lambda i: (i, 0))
    ins = [x2d, w['norm_mix'], w['wqkv'], w['wf'], w['bf'], w['wqm'], w['wu'], w['wg'],
           w['qn_fox'], w['kn_fox'], w['qn_mem'], w['bd']]
    in_specs = [row(D_MODEL)] + [_full(a.shape) for a in ins[1:]]
    out_shape = (
        jax.ShapeDtypeStruct((n, FOX_WIDTH), BF16),
        jax.ShapeDtypeStruct((n, FOX_WIDTH), BF16),
        jax.ShapeDtypeStruct((n, FOX_WIDTH), BF16),
        kv_shape,
        kv_shape,
        jax.ShapeDtypeStruct((N_FOX_HEADS, n), F32),
        jax.ShapeDtypeStruct((n, MEM_WIDTH), BF16),
        jax.ShapeDtypeStruct((n, SSM_WIDTH), F32),
        jax.ShapeDtypeStruct((n, 3 * D_MODEL), BF16),
    )
    out_specs = (row(FOX_WIDTH), row(FOX_WIDTH), row(FOX_WIDTH), heads, heads,
                 pl.BlockSpec((N_FOX_HEADS, tm), lambda i: (0, i)),
                 row(MEM_WIDTH), row(SSM_WIDTH), row(3 * D_MODEL))
    return pl.pallas_call(
        functools.partial(_inproj_kernel, kv_transposed=kv_transposed),
        out_shape=out_shape, grid=(n // tm,), in_specs=in_specs, out_specs=out_specs,
        compiler_params=_params(("parallel",)), name="inproj")(*ins)


def _memkv_kernel(x_ref, g_ref, w_ref, kn_ref, k_ref, v_ref):
    x = x_ref[...]
    h = x * lax.rsqrt(jnp.mean(x * x, axis=-1, keepdims=True) + RMS_EPS) * g_ref[...]
    hb = h.astype(BF16)
    tm = x_ref.shape[0]
    zk = _dot(hb, w_ref[:, 0:MEM_WIDTH])
    zv = _dot(hb, w_ref[:, MEM_WIDTH:2 * MEM_WIDTH])
    for hd in range(N_MEM_HEADS):
        sl = slice(hd * MEM_HEAD_DIM, (hd + 1) * MEM_HEAD_DIM)
        rows = pl.ds(hd, tm, stride=N_MEM_HEADS)
        zh = zk[:, sl]
        ms = jnp.mean(zh * zh, axis=-1, keepdims=True)
        k_ref[rows, :] = zh * lax.rsqrt(ms + RMS_EPS) * kn_ref[...]
        v_ref[rows, :] = zv[:, sl]


def _memkv(mem2d, norm_mem, w_kv, kn_mem, tm):
    n = mem2d.shape[0]
    out = jax.ShapeDtypeStruct((n * N_MEM_HEADS, MEM_HEAD_DIM), F32)
    ospec = pl.BlockSpec((tm * N_MEM_HEADS, MEM_HEAD_DIM), lambda i: (i, 0))
    return pl.pallas_call(
        _memkv_kernel,
        out_shape=(out, out),
        grid=(n // tm,),
        in_specs=[pl.BlockSpec((tm, D_MODEL), lambda i: (i, 0)), _full(norm_mem.shape), _full(w_kv.shape),
                  _full(kn_mem.shape)],
        out_specs=(ospec, ospec),
        compiler_params=_params(("parallel",)), name="memkv")(mem2d, norm_mem, w_kv, kn_mem)


CUMSUM_BLOCK = 256


def _cumsum_kernel(x_ref, o_ref):
    nblk = x_ref.shape[1] // CUMSUM_BLOCK
    r = lax.broadcasted_iota(jnp.int32, (CUMSUM_BLOCK, CUMSUM_BLOCK), 0)
    c = lax.broadcasted_iota(jnp.int32, (CUMSUM_BLOCK, CUMSUM_BLOCK), 1)
    tri = (r <= c).astype(F32)
    carry = jnp.zeros((x_ref.shape[0], 1), F32)
    for j in range(nblk):
        sl = slice(j * CUMSUM_BLOCK, (j + 1) * CUMSUM_BLOCK)
        cs = _dot_exact(x_ref[:, sl], tri) + carry
        o_ref[:, sl] = cs
        carry = cs[:, CUMSUM_BLOCK - 1:CUMSUM_BLOCK]


def _cumsum_rows(x):
    rows, n = x.shape
    npad = -(-n // CUMSUM_BLOCK) * CUMSUM_BLOCK
    xp = jnp.pad(x, ((0, 0), (0, npad - n))) if npad != n else x
    out = pl.pallas_call(
        _cumsum_kernel, out_shape=jax.ShapeDtypeStruct((rows, npad), F32), grid=(1,),
        in_specs=[_full((rows, npad))], out_specs=_full((rows, npad)),
        compiler_params=_params(("arbitrary",)), name="cumsum")(xp)
    return out[:, :n] if npad != n else out


def _reduce_rows(x, op):
    rows, cols = x.shape
    if rows > 64 and rows % 64 == 0:
        x = op(x.reshape(rows // 64, 64, cols), axis=0)
        rows = 64
    if rows == 64:
        x = op(x.reshape(8, 8, cols), axis=0)
    return op(x, axis=0, keepdims=True)


def _fox_prompt_kernel(q_ref, k_ref, v_ref, cr_ref, o_ref,
                       vt_ref, ck0_ref, ck1_ref, st0_ref, st1_ref, pt0_ref, pt1_ref, acc_ref, *, tq, tk):
    st_refs = (st0_ref, st1_ref)
    pt_refs = (pt0_ref, pt1_ref)
    i = pl.program_id(2)
    s_len = k_ref.shape[1]

    @pl.when(i == 0)
    def _():
        vt_ref[...] = v_ref[0].astype(F32).T.astype(BF16)
        ck0_ref[...] = jnp.broadcast_to(cr_ref[0, 0, 0:1, :], (LANES, s_len)).T
        ck1_ref[...] = jnp.broadcast_to(cr_ref[0, 0, 1:2, :], (LANES, s_len)).T

    qt = q_ref[0].astype(F32).T
    row = lax.broadcasted_iota(jnp.int32, (LANES, tq), 0)
    qts = (jnp.where(row < FOX_HEAD_DIM, qt, 0.0).astype(BF16), jnp.where(row < FOX_HEAD_DIM, 0.0, qt).astype(BF16))
    q0 = pl.multiple_of(i * tq, tq)
    cq = cr_ref[0, 0, :, pl.ds(q0, tq)]
    ck_refs = (ck0_ref, ck1_ref)

    def stage_a(n, par):
        s = pl.multiple_of(n * tk, tk)
        kb = k_ref[0, pl.ds(s, tk), :]
        for hh in range(2):
            ck = ck_refs[hh][pl.ds(s, tk), :]
            st_refs[par][hh] = _dot(kb, qts[hh]) - jnp.concatenate([ck] * (tq // LANES), axis=1)

    def stage_b(n, par, stats, masked):
        if masked:
            kpos = n * tk + lax.broadcasted_iota(jnp.int32, (tk, tq), 0)
            qpos = q0 + lax.broadcasted_iota(jnp.int32, (tk, tq), 1)
            mask = kpos <= qpos
        out = []
        for hh in range(2):
            m, l = stats[2 * hh:2 * hh + 2]
            t = st_refs[par][hh]
            if masked:
                t = jnp.where(mask, t, NEG_INF)
            cqh = cq[hh:hh + 1, :]
            m_new = jnp.maximum(m, _reduce_rows(t, jnp.max) + cqh)
            alpha = jnp.exp2(m - m_new)
            p = jnp.exp2(t + (cqh - m_new))
            pt_refs[par][hh] = p.astype(BF16)
            out.extend([m_new, alpha * l + _reduce_rows(p, jnp.sum), alpha])
        return tuple(out)

    def stage_c(n, par, alphas):
        s = pl.multiple_of(jnp.maximum(n, 0) * tk, tk)
        for hh in range(2):
            vt = vt_ref[hh * FOX_HEAD_DIM:(hh + 1) * FOX_HEAD_DIM, pl.ds(s, tk)]
            acc_ref[hh] = alphas[hh] * acc_ref[hh] + _dot(vt, pt_refs[par][hh])

    def iteration(n, par, carry):
        m0, l0, al0, m1, l1, al1 = carry
        stage_c(n - 1, 1 - par, (al0, al1))
        new = stage_b(n, par, (m0, l0, m1, l1), False)
        stage_a(n + 1, 1 - par)
        return new

    def finish(par, carry):
        m0, l0, al0, m1, l1, al1 = carry
        stage_c(nfull - 1, 1 - par, (al0, al1))
        _, l0, be0, _, l1, be1 = stage_b(nfull, par, (m0, l0, m1, l1), True)
        stage_c(nfull, par, (be0, be1))
        ot = jnp.concatenate([acc_ref[0] / l0, acc_ref[1] / l1], axis=0)
        o_ref[0] = ot.T.astype(o_ref.dtype)

    acc_ref[...] = jnp.zeros(acc_ref.shape, F32)
    pt1_ref[...] = jnp.zeros(pt1_ref.shape, BF16)
    neg = jnp.full((1, tq), NEG_INF, F32)
    zero = jnp.zeros((1, tq), F32)
    one = jnp.ones((1, tq), F32)
    nfull = (i * tq) // tk
    stage_a(0, 0)
    carry = lax.fori_loop(0, nfull // 2, lambda k, c: iteration(2 * k + 1, 1, iteration(2 * k, 0, c)),
                          (neg, zero, one, neg, zero, one))
    odd = nfull % 2 == 1
    carry = lax.cond(odd, lambda c: iteration(nfull - 1, 0, c), lambda c: c, carry)
    pl.when(odd)(lambda: finish(1, carry))
    pl.when(jnp.logical_not(odd))(lambda: finish(0, carry))


def _fox_prompt(q, k, v, c_row, tq, tk):
    b, s, _ = q.shape
    assert s % tk == 0 and tk % tq == 0
    npair = N_FOX_HEADS // 2
    return pl.pallas_call(
        functools.partial(_fox_prompt_kernel, tq=tq, tk=tk),
        out_shape=jax.ShapeDtypeStruct((b, s, FOX_WIDTH), BF16),
        grid=(b, npair, s // tq),
        in_specs=[
            pl.BlockSpec((1, tq, LANES), lambda bi, hp, i: (bi, i, hp)),
            pl.BlockSpec((1, s, LANES), lambda bi, hp, i: (bi, 0, hp)),
            pl.BlockSpec((1, s, LANES), lambda bi, hp, i: (bi, 0, hp)),
            pl.BlockSpec((1, 1, 2, s), lambda bi, hp, i: (bi, hp, 0, 0)),
        ],
        out_specs=pl.BlockSpec((1, tq, LANES), lambda bi, hp, i: (bi, i, hp)),
        scratch_shapes=[pltpu.VMEM((LANES, s), BF16),
                        pltpu.VMEM((s, LANES), F32), pltpu.VMEM((s, LANES), F32),
                        pltpu.VMEM((2, tk, tq), F32), pltpu.VMEM((2, tk, tq), F32),
                        pltpu.VMEM((2, tk, tq), BF16), pltpu.VMEM((2, tk, tq), BF16),
                        pltpu.VMEM((2, FOX_HEAD_DIM, tq), F32)],
        compiler_params=_params(("parallel", "parallel", "arbitrary")), name="fox_prompt")(q, k, v, c_row)


def _fox_sample_kernel(q_ref, ck_ref, cv_ref, nk_ref, nv_ref, cq_ref, crc_ref, crn_ref, o_ref, *state, n):
    j = pl.program_id(1)
    nj = pl.num_programs(1)
    m_refs = state[0:N_FOX_HEADS]
    l_refs = state[N_FOX_HEADS:2 * N_FOX_HEADS]
    acc_refs = state[2 * N_FOX_HEADS:3 * N_FOX_HEADS]

    @pl.when(j == 0)
    def _():
        for hd in range(N_FOX_HEADS):
            m_refs[hd][...] = jnp.full(m_refs[hd].shape, NEG_INF, F32)
            l_refs[hd][...] = jnp.zeros(l_refs[hd].shape, F32)
            acc_refs[hd][...] = jnp.zeros(acc_refs[hd].shape, F32)

    def update(k_of, v_of, cr_ref_, mask, transposed):
        qk = _dot if transposed else _dot_nt
        pv = _dot_nt if transposed else _dot
        ts = []
        for hd in range(N_FOX_HEADS):
            hs = slice(hd * FOX_HEAD_DIM, (hd + 1) * FOX_HEAD_DIM)
            t = qk(q_ref[0, :, hs], k_of(hd)) - cr_ref_[0, hd:hd + 1, :]
            ts.append(t if mask is None else jnp.where(mask, t, NEG_INF))
        ps = []
        for hd in range(N_FOX_HEADS):
            cq = cq_ref[0, :, hd:hd + 1]
            m = m_refs[hd][...]
            m_new = jnp.maximum(m, jnp.max(ts[hd], axis=-1, keepdims=True) + cq)
            alpha = jnp.exp2(m - m_new)
            p = jnp.exp2(ts[hd] + (cq - m_new))
            m_refs[hd][...] = m_new
            l_refs[hd][...] = alpha * l_refs[hd][...] + jnp.sum(p, axis=-1, keepdims=True)
            ps.append((alpha, p.astype(BF16)))
        for hd in range(N_FOX_HEADS):
            alpha, p = ps[hd]
            acc_refs[hd][...] = alpha * acc_refs[hd][...] + pv(p, v_of(hd))

    update(lambda hd: ck_ref[0, hd].astype(BF16), lambda hd: cv_ref[0, hd].astype(BF16), crc_ref, None, True)

    @pl.when(j == nj - 1)
    def _():
        r = lax.broadcasted_iota(jnp.int32, (n, n), 0)
        c = lax.broadcasted_iota(jnp.int32, (n, n), 1)
        head = lambda ref: (lambda hd: ref[0, :, hd * FOX_HEAD_DIM:(hd + 1) * FOX_HEAD_DIM])
        update(head(nk_ref), head(nv_ref), crn_ref, c <= r, False)
        for hd in range(N_FOX_HEADS):
            hs = slice(hd * FOX_HEAD_DIM, (hd + 1) * FOX_HEAD_DIM)
            o_ref[0, :, hs] = (acc_refs[hd][...] / l_refs[hd][...]).astype(o_ref.dtype)


def _fox_sample(q, cache_k, cache_v, k_new, v_new, c_q, c_row_cache, c_row_new, tk):
    b, n, _ = q.shape
    past = cache_k.shape[3]
    assert past % tk == 0
    cache_spec = pl.BlockSpec((1, N_FOX_HEADS, FOX_HEAD_DIM, tk), lambda bi, j: (bi, 0, 0, j))
    return pl.pallas_call(
        functools.partial(_fox_sample_kernel, n=n),
        out_shape=jax.ShapeDtypeStruct((b, n, FOX_WIDTH), BF16),
        grid=(b, past // tk),
        in_specs=[
            pl.BlockSpec((1, n, FOX_WIDTH), lambda bi, j: (bi, 0, 0)),
            cache_spec,
            cache_spec,
            pl.BlockSpec((1, n, FOX_WIDTH), lambda bi, j: (bi, 0, 0)),
            pl.BlockSpec((1, n, FOX_WIDTH), lambda bi, j: (bi, 0, 0)),
            pl.BlockSpec((1, n, N_FOX_HEADS), lambda bi, j: (bi, 0, 0)),
            pl.BlockSpec((1, N_FOX_HEADS, tk), lambda bi, j: (bi, 0, j)),
            pl.BlockSpec((1, N_FOX_HEADS, n), lambda bi, j: (bi, 0, 0)),
        ],
        out_specs=pl.BlockSpec((1, n, FOX_WIDTH), lambda bi, j: (bi, 0, 0)),
        scratch_shapes=([pltpu.VMEM((n, 1), F32)] * (2 * N_FOX_HEADS)
                        + [pltpu.VMEM((n, FOX_HEAD_DIM), F32)] * N_FOX_HEADS),
        compiler_params=_params(("parallel", "arbitrary")), name="fox_sample")(
            q, cache_k, cache_v, k_new, v_new, c_q, c_row_cache, c_row_new)


def _ssm_mats(p):
    f32 = F32
    a_re, a_im = p['ssm_a_re'].astype(f32), p['ssm_a_im'].astype(f32)
    b_re, b_im = p['ssm_b_re'].astype(f32), p['ssm_b_im'].astype(f32)
    c_re, c_im = p['ssm_c_re'].astype(f32), p['ssm_c_im'].astype(f32)
    dt = jnp.exp(p['ssm_log_dt'].astype(f32))[:, None]
    mag = jnp.exp(dt * a_re)
    ab_re = mag * jnp.cos(dt * a_im)
    ab_im = mag * jnp.sin(dt * a_im)
    den = a_re * a_re + a_im * a_im
    nr, ni = ab_re - 1.0, ab_im
    coef_re = (nr * a_re + ni * a_im) / den
    coef_im = (ni * a_re - nr * a_im) / den
    bb_re = coef_re[..., None] * b_re - coef_im[..., None] * b_im
    bb_im = coef_re[..., None] * b_im + coef_im[..., None] * b_re
    pr, pi = [jnp.ones_like(ab_re)], [jnp.zeros_like(ab_im)]
    for _ in range(SSM_CHUNK):
        pr.append(pr[-1] * ab_re - pi[-1] * ab_im)
        pi.append(pr[-2] * ab_im + pi[-1] * ab_re)
    pw_re, pw_im = jnp.stack(pr), jnp.stack(pi)
    T = SSM_CHUNK
    w_re = pw_re[..., None] * bb_re[None] - pw_im[..., None] * bb_im[None]
    w_im = pw_re[..., None] * bb_im[None] + pw_im[..., None] * bb_re[None]
    kk = (jnp.einsum('gop,kgpi->kgoi', c_re, w_re[:T], precision='highest')
          - jnp.einsum('gop,kgpi->kgoi', c_im, w_im[:T], precision='highest'))
    nq = N_SSM_GROUPS // SSM_GPB

    def group_diag(m):
        rows, c = m.shape[-2:]
        m = jnp.tile(m, (1,) * (m.ndim - 1) + (SSM_GPB,))
        same = (jnp.arange(rows) // (rows // SSM_GPB))[:, None] == (jnp.arange(SSM_GPB * c) // c)[None, :]
        return jnp.where(same, m, 0.0)

    def lane_diag(m):
        lead = m.shape[:-3]
        i, c = m.shape[-2:]
        return group_diag(m.reshape(lead + (nq, SSM_GPB * i, c)))

    ktau = lane_diag(jnp.swapaxes(kk, -1, -2))
    ktau = jnp.concatenate([jnp.zeros_like(ktau[:1]), ktau], axis=0)
    units = []
    for dlag in range(T // 2 - 1, -1, -1):
        top = jnp.concatenate([ktau[2 * dlag + 1], ktau[2 * dlag + 2]], axis=-1)
        bot = jnp.concatenate([ktau[2 * dlag], ktau[2 * dlag + 1]], axis=-1)
        units.append(jnp.concatenate([top, bot], axis=-2))
    kstack = jnp.concatenate(units, axis=-2).astype(BF16)
    rev = T - 1 - jnp.arange(T)
    def local_rows(w):
        w = jnp.transpose(w[rev], (1, 0, 3, 2)).reshape(nq, SSM_GPB, T, SSM_GROUP, SSM_STATE)
        return group_diag(jnp.transpose(w, (0, 2, 1, 3, 4)).reshape(nq, T, LANES, SSM_STATE))

    m_all = jnp.concatenate([local_rows(w_re), local_rows(w_im)], axis=-1)
    m_all = m_all.reshape(nq, T * LANES, 2 * SSM_GPB * SSM_STATE)
    m_hi, m_lo = _split_bf16(m_all)
    ar, ai = pw_re[1:], pw_im[1:]
    n_re = (c_re[None] * ar[:, :, None, :] - c_im[None] * ai[:, :, None, :])
    n_im = -(c_re[None] * ai[:, :, None, :] + c_im[None] * ar[:, :, None, :])

    def state_rows(n):
        n = jnp.transpose(n, (1, 3, 0, 2)).reshape(nq, SSM_GPB * SSM_STATE, T, SSM_GROUP)
        n = jnp.tile(n, (1, 1, 1, SSM_GPB))
        same = (jnp.arange(SSM_GPB * SSM_STATE) // SSM_STATE)[:, None, None] == (jnp.arange(LANES) // SSM_GROUP)
        return jnp.where(same, n, 0.0).reshape(nq, SSM_GPB * SSM_STATE, T * LANES)

    n_all = jnp.concatenate([state_rows(n_re), state_rows(n_im)], axis=1).astype(BF16)
    return dict(kstack=kstack, m_hi=m_hi, m_lo=m_lo, n_all=n_all,
                a16_re=pw_re[T].reshape(8, 256), a16_im=pw_im[T].reshape(8, 256),
                d=p['ssm_d'].astype(f32).reshape(1, SSM_WIDTH))


def _chunk_tokens(u_ref, rows):
    return [u_ref[pl.ds(t, rows, stride=SSM_CHUNK), :] for t in range(SSM_CHUNK)]


def _ssm_local_kernel(u_ref, mh_ref, ml_ref, hre_ref, him_ref):
    rows = hre_ref.shape[0]
    parts = [_split_bf16(ut) for ut in _chunk_tokens(u_ref, rows)]
    x_hi = jnp.concatenate([h for h, _ in parts], axis=1)
    x_lo = jnp.concatenate([l for _, l in parts], axis=1)
    h = _dot(x_hi, mh_ref[0]) + _dot(x_hi, ml_ref[0]) + _dot(x_lo, mh_ref[0])
    half = SSM_GPB * SSM_STATE
    hre_ref[...] = h[:, 0:half]
    him_ref[...] = h[:, half:2 * half]


def _ssm_local(u2d, mats, rows):
    n = u2d.shape[0]
    r = n // SSM_CHUNK
    nq = N_SSM_GROUPS // SSM_GPB
    half = SSM_GPB * SSM_STATE
    mspec = pl.BlockSpec((1, SSM_CHUNK * LANES, 2 * half), lambda q, i: (q, 0, 0))
    ospec = pl.BlockSpec((rows, half), lambda q, i: (i, q))
    return pl.pallas_call(
        _ssm_local_kernel,
        out_shape=(jax.ShapeDtypeStruct((r, N_SSM_GROUPS * SSM_STATE), F32),) * 2,
        grid=(nq, r // rows),
        in_specs=[pl.BlockSpec((rows * SSM_CHUNK, LANES), lambda q, i: (i, q)), mspec, mspec],
        out_specs=(ospec, ospec),
        compiler_params=_params(("parallel", "parallel")), name="ssm_local")(u2d, mats['m_hi'], mats['m_lo'])


def _ssm_scan_kernel(lre_ref, lim_ref, are_ref, aim_ref, h0re_ref, h0im_ref,
                     pre_ref, pim_ref, fre_ref, fim_ref):
    nchunk = lre_ref.shape[1]
    ar, ai = are_ref[...], aim_ref[...]

    def body(c, carry):
        hr, hi = carry
        pre_ref[0, c] = hr
        pim_ref[0, c] = hi
        return (ar * hr - ai * hi + lre_ref[0, c], ar * hi + ai * hr + lim_ref[0, c])

    hr, hi = lax.fori_loop(0, nchunk, body, (h0re_ref[0], h0im_ref[0]))
    fre_ref[0] = hr
    fim_ref[0] = hi


def _ssm_scan(hloc_re, hloc_im, mats, h0_re, h0_im):
    b, nchunk = hloc_re.shape[:2]
    big = pl.BlockSpec((1, nchunk, 8, 256), lambda i: (i, 0, 0, 0))
    small = pl.BlockSpec((1, 8, 256), lambda i: (i, 0, 0))
    return pl.pallas_call(
        _ssm_scan_kernel,
        out_shape=(jax.ShapeDtypeStruct(hloc_re.shape, F32),) * 2 + (jax.ShapeDtypeStruct((b, 8, 256), F32),) * 2,
        grid=(b,),
        in_specs=[big, big, _full((8, 256)), _full((8, 256)), small, small],
        out_specs=(big, big, small, small),
        compiler_params=_params(("parallel",)), name="ssm_scan")(
            hloc_re, hloc_im, mats['a16_re'], mats['a16_im'], h0_re, h0_im)


def _gelu_tanh(y):
    return 0.5 * y * (1.0 + jnp.tanh(math.sqrt(2.0 / math.pi) * (y + 0.044715 * (y * y * y))))


def _ssm_out_kernel(u_ref, k_ref, pre_ref, pim_ref, n_ref, d_ref, y_ref, ysc_ref):
    rows = pre_ref.shape[0]
    us = _chunk_tokens(u_ref, rows)
    x = jnp.concatenate([ut.astype(BF16) for ut in us], axis=1)
    hp = jnp.concatenate([pre_ref[...], pim_ref[...]], axis=1).astype(BF16)
    unit = 2 * LANES
    nunit = SSM_CHUNK // 2
    for j in range(nunit):
        yj = (_dot(x[:, 0:unit * (j + 1)], k_ref[0, unit * (nunit - 1 - j):, :])
              + _dot(hp, n_ref[0, :, unit * j:unit * (j + 1)]))
        for t2 in range(2):
            t = 2 * j + t2
            y = yj[:, t2 * LANES:(t2 + 1) * LANES] + d_ref[...] * us[t]
            ysc_ref[pl.ds(t, rows, stride=SSM_CHUNK), :] = _gelu_tanh(y)
    y_ref[...] = ysc_ref[...].astype(y_ref.dtype)


def _ssm_out(u2d, hprev_re, hprev_im, mats, rows):
    n = u2d.shape[0]
    r = n // SSM_CHUNK
    nq = N_SSM_GROUPS // SSM_GPB
    half = SSM_GPB * SSM_STATE
    uspec = pl.BlockSpec((rows * SSM_CHUNK, LANES), lambda q, i: (i, q))
    hspec = pl.BlockSpec((rows, half), lambda q, i: (i, q))
    return pl.pallas_call(
        _ssm_out_kernel,
        out_shape=jax.ShapeDtypeStruct((n, SSM_WIDTH), BF16),
        grid=(nq, r // rows),
        in_specs=[uspec, pl.BlockSpec((1, SSM_CHUNK * LANES, 2 * LANES), lambda q, i: (q, 0, 0)), hspec, hspec,
                  pl.BlockSpec((1, 2 * half, SSM_CHUNK * LANES), lambda q, i: (q, 0, 0)),
                  pl.BlockSpec((1, LANES), lambda q, i: (0, q))],
        out_specs=uspec,
        scratch_shapes=[pltpu.VMEM((rows * SSM_CHUNK, LANES), F32)],
        compiler_params=_params(("parallel", "parallel")), name="ssm_out")(
            u2d, mats['kstack'], hprev_re, hprev_im, mats['n_all'], mats['d'])


def _ssm(u2d, b, h0_re, h0_im, mats):
    n = u2d.shape[0]
    nchunk = n // b // SSM_CHUNK
    r = b * nchunk
    rows = _pick_tile(r, 256)
    hloc_re, hloc_im = _ssm_local(u2d, mats, rows)
    shp = (b, nchunk, 8, 256)
    hprev_re, hprev_im, f_re, f_im = _ssm_scan(hloc_re.reshape(shp), hloc_im.reshape(shp), mats,
                                               h0_re.reshape(b, 8, 256), h0_im.reshape(b, 8, 256))
    y = _ssm_out(u2d, hprev_re.reshape(r, -1), hprev_im.reshape(r, -1), mats, rows)
    return y, f_re.reshape(b, N_SSM_GROUPS, SSM_STATE), f_im.reshape(b, N_SSM_GROUPS, SSM_STATE)


def _merge_kernel(x_ref, of_ref, ys_ref, qm_ref, gate_ref, mk_ref, mv_ref,
                  wglu_ref, wbf_ref, wbs_ref, wbm_ref, wo_ref, nf_ref, wr_ref,
                  x1_ref, h2_ref, r_ref):
    tm = x_ref.shape[0]
    om = []
    for hd in range(N_MEM_HEADS):
        sl = slice(hd * MEM_HEAD_DIM, (hd + 1) * MEM_HEAD_DIM)
        head_rows = pl.ds(hd, N_MEM, stride=N_MEM_HEADS)
        kh = mk_ref[0, head_rows, :].astype(BF16)
        vh = mv_ref[0, head_rows, :].astype(BF16)
        sc = _dot_nt(qm_ref[:, sl], kh)
        p = jnp.exp(sc - jnp.max(sc, axis=-1, keepdims=True))
        om.append(_dot(p.astype(BF16), vh) / jnp.sum(p, axis=-1, keepdims=True))
    o_mem = jnp.concatenate(om, axis=-1).astype(BF16)
    z = _dot(ys_ref[...], wglu_ref[...])
    y_ssm = (z[:, 0:SSM_WIDTH] * jax.nn.sigmoid(z[:, SSM_WIDTH:2 * SSM_WIDTH])).astype(BF16)
    g = lambda c: gate_ref[:, c * D_MODEL:(c + 1) * D_MODEL].astype(F32)
    merged = (g(0) * _dot(of_ref[...], wbf_ref[...]) + g(1) * _dot(y_ssm, wbs_ref[...])
              + g(2) * _dot(o_mem, wbm_ref[...]))
    x1 = x_ref[...] + _dot(merged.astype(BF16), wo_ref[...])
    x1_ref[...] = x1
    h2 = x1 * lax.rsqrt(jnp.mean(x1 * x1, axis=-1, keepdims=True) + RMS_EPS) * nf_ref[...]
    h2_ref[...] = h2.astype(BF16)
    h2_hi, h2_lo = _split_bf16(h2)
    hw = _dot(h2_hi, wr_ref[...])
    logits = hw[:, 0:LANES] + hw[:, LANES:2 * LANES] + _dot(h2_lo, wr_ref[:, 0:LANES])
    lane = lax.broadcasted_iota(jnp.int32, (tm, LANES), 1)
    big = jnp.int32(LANES)
    is_grp = (lane >= N_EXPERTS) & (lane < N_EXPERTS + N_EXPERT_GROUPS)
    gl = jnp.where(is_grp, logits, NEG_INF)
    gmax = jnp.max(gl, axis=-1, keepdims=True)
    grp = jnp.min(jnp.where(is_grp & (gl == gmax), lane, big), axis=-1, keepdims=True) - N_EXPERTS
    g_w = 1.0 / jnp.sum(jnp.where(is_grp, jnp.exp(gl - gmax), 0.0), axis=-1, keepdims=True)
    in_grp = (lane >= grp * EXPERTS_PER_GROUP) & (lane < (grp + 1) * EXPERTS_PER_GROUP)
    e1 = jnp.where(in_grp, logits, NEG_INF)
    m1 = jnp.max(e1, axis=-1, keepdims=True)
    i1 = jnp.min(jnp.where(in_grp & (e1 == m1), lane, big), axis=-1, keepdims=True)
    rest = in_grp & (lane != i1)
    e2 = jnp.where(rest, logits, NEG_INF)
    m2 = jnp.max(e2, axis=-1, keepdims=True)
    i2 = jnp.min(jnp.where(rest & (e2 == m2), lane, big), axis=-1, keepdims=True)
    ex = jnp.exp(m2 - m1)
    w1 = g_w / (1.0 + ex)
    w2 = g_w * ex / (1.0 + ex)
    r_ref[...] = jnp.where(lane == i1, w1, jnp.where(lane == i2, w2, jnp.where(lane == GROUP_LANE, grp.astype(F32), 0.0)))


def _merge(x2d, o_fox, ys, q_m, gates, mem_k, mem_v, w, tm, rows_per_batch):
    n = x2d.shape[0]
    assert n % tm == 0 and rows_per_batch % tm == 0
    per = rows_per_batch // tm
    row = lambda width: pl.BlockSpec((tm, width), lambda i: (i, 0))
    memspec = pl.BlockSpec((1, N_MEM * N_MEM_HEADS, MEM_HEAD_DIM), lambda i: (i // per, 0, 0))
    ws = [w['w_glu'], w['w_br_fox'], w['w_br_ssm'], w['w_br_mem'], w['w_out'], w['norm_ffn'], w['w_router']]
    return pl.pallas_call(
        _merge_kernel,
        out_shape=(jax.ShapeDtypeStruct((n, D_MODEL), F32), jax.ShapeDtypeStruct((n, D_MODEL), BF16),
                   jax.ShapeDtypeStruct((n, LANES), F32)),
        grid=(n // tm,),
        in_specs=[row(D_MODEL), row(FOX_WIDTH), row(SSM_WIDTH), row(MEM_WIDTH), row(3 * D_MODEL), memspec, memspec]
                 + [_full(a.shape) for a in ws],
        out_specs=(row(D_MODEL), row(D_MODEL), row(LANES)),
        compiler_params=_params(("parallel",)), name="merge")(
            x2d, o_fox, ys, q_m, gates, mem_k, mem_v, *ws)


def _moe_kernel(h_ref, r_ref, x1_ref, tri_ref, wg_ref, wu_ref, wd_ref, o_ref,
                xs_ref, cw_ref, og_ref, acc_ref, rank_ref, nblk_ref, *, main):
    step = pl.program_id(1)
    steps_per_group = EXPERTS_PER_GROUP // MOE_EPS
    g = step // steps_per_group
    tm = h_ref.shape[0]
    gf = g.astype(F32)
    bounds = [0, main] + list(range(-(-main // MOE_SUB) * MOE_SUB, tm, MOE_SUB)) + [tm]
    bounds = sorted(set(bounds))
    blocks = [(lo, hi - lo, lo > 0) for lo, hi in zip(bounds[:-1], bounds[1:])]

    def guarded(r0, fn):
        if r0 == 0:
            fn()
        else:
            pl.when(r0 < nblk_ref[0])(fn)

    @pl.when(step == 0)
    def _():
        acc_ref[...] = jnp.zeros(acc_ref.shape, F32)

    @pl.when(step % steps_per_group == 0)
    def _():
        rt = r_ref[...]
        rtt = rt.T
        mrow = rtt[GROUP_LANE:GROUP_LANE + 1, :] == gf
        m8 = jnp.broadcast_to(jnp.where(mrow, 1.0, 0.0), (8, tm))
        rank8 = _dot(m8.astype(BF16), tri_ref[...])
        rank_row = jnp.where(mrow, rank8[0:1, :], -1.0)
        rank_ref[...] = jnp.broadcast_to(jnp.where(mrow, rank8, -1.0).T[:, 0:1], rank_ref.shape)
        nblk_ref[0] = jnp.sum(jnp.where(mrow, 1, 0))
        hilo = jnp.concatenate(_split_bf16(rt), axis=1)
        for r0, nrows, _ in blocks:
            def compact(r0=r0, nrows=nrows):
                rows = slice(r0, r0 + nrows)
                slot = r0 + lax.broadcasted_iota(jnp.int32, (nrows, tm), 0)
                perm = jnp.where(rank_row == slot.astype(F32), 1.0, 0.0).astype(BF16)
                xs_ref[rows, :] = _dot(perm, h_ref[...]).astype(BF16)
                cw = _dot(perm, hilo)
                cw_ref[rows, :] = cw[:, 0:LANES] + cw[:, LANES:2 * LANES]
                og_ref[rows, :] = jnp.zeros((nrows, D_MODEL), F32)
            guarded(r0, compact)

    for k in range(MOE_EPS):
        e = step * MOE_EPS + k
        for r0, nrows, _ in blocks:
            def expert(r0=r0, nrows=nrows, k=k, e=e):
                rows = slice(r0, r0 + nrows)
                x = xs_ref[rows, :]
                a = _dot(x, wg_ref[k])
                up = _dot(x, wu_ref[k])
                lane = lax.broadcasted_iota(jnp.int32, (nrows, LANES), 1)
                ce = jnp.sum(jnp.where(lane == e, cw_ref[rows, :], 0.0), axis=-1, keepdims=True)
                act = a * jax.nn.sigmoid(a) * up * ce
                og_ref[rows, :] += _dot(act.astype(BF16), wd_ref[k])
            guarded(r0, expert)

    @pl.when(step % steps_per_group == steps_per_group - 1)
    def _():
        for r0, nrows, _ in blocks:
            def scatter_back(r0=r0, nrows=nrows):
                rows = slice(r0, r0 + nrows)
                slot = r0 + lax.broadcasted_iota(jnp.int32, (tm, nrows), 1)
                back = jnp.where(rank_ref[:, 0:1] == slot.astype(F32), 1.0, 0.0).astype(BF16)
                acc_ref[...] += _dot(back, og_ref[rows, :].astype(BF16))
            guarded(r0, scatter_back)

    @pl.when(step == pl.num_programs(1) - 1)
    def _():
        o_ref[...] = x1_ref[...] + acc_ref[...]


def _moe(h2, route, x1, wg, wu, wd, tm):
    n = h2.shape[0]
    assert n % tm == 0 and tm % MOE_SUB == 0
    main = max(MOE_SUB // 2, (5 * tm // 16) // 64 * 64)
    row = lambda width: pl.BlockSpec((tm, width), lambda i, s: (i, 0))
    r = jnp.arange(tm)
    tri = (r[:, None] < r[None, :]).astype(BF16)
    return pl.pallas_call(
        functools.partial(_moe_kernel, main=main),
        out_shape=jax.ShapeDtypeStruct((n, D_MODEL), F32),
        grid=(n // tm, N_EXPERTS // MOE_EPS),
        in_specs=[row(D_MODEL), row(LANES), row(D_MODEL), pl.BlockSpec((tm, tm), lambda i, s: (0, 0)),
                  pl.BlockSpec((MOE_EPS, D_MODEL, D_EXPERT), lambda i, s: (s, 0, 0)),
                  pl.BlockSpec((MOE_EPS, D_MODEL, D_EXPERT), lambda i, s: (s, 0, 0)),
                  pl.BlockSpec((MOE_EPS, D_EXPERT, D_MODEL), lambda i, s: (s, 0, 0))],
        out_specs=row(D_MODEL),
        scratch_shapes=[pltpu.VMEM((tm, D_MODEL), BF16), pltpu.VMEM((tm, LANES), F32), pltpu.VMEM((tm, D_MODEL), F32),
                        pltpu.VMEM((tm, D_MODEL), F32), pltpu.VMEM((tm, LANES), F32), pltpu.SMEM((1,), jnp.int32)],
        compiler_params=_params(("parallel", "arbitrary")), name="moe")(h2, route, x1, tri, wg, wu, wd)


def _prep_weights(p):
    w_in = p['w_in'].astype(BF16)
    o = 0
    wqkv = w_in[:, 0:3 * FOX_WIDTH]
    o = 3 * FOX_WIDTH
    wf = jnp.pad(w_in[:, o:o + N_FOX_HEADS], ((0, 0), (0, LANES - N_FOX_HEADS)))
    o += N_FOX_HEADS
    wqm = w_in[:, o:o + MEM_WIDTH]
    o += MEM_WIDTH
    wu = w_in[:, o:o + SSM_WIDTH]
    o += SSM_WIDTH
    wg = w_in[:, o:o + 3 * D_MODEL]
    r = jnp.arange(FOX_WIDTH) // FOX_HEAD_DIM
    bd = (r[:, None] == r[None, :]).astype(BF16)
    w_router = jnp.concatenate(
        [p['w_router_expert'], p['w_router_group'],
         jnp.zeros((D_MODEL, LANES - N_EXPERTS - N_EXPERT_GROUPS), F32)], axis=1)
    w_router = jnp.concatenate(_split_bf16(w_router), axis=1)
    return dict(
        norm_mix=p['norm_mix'].reshape(1, D_MODEL), wqkv=wqkv, wf=wf,
        bf=jnp.pad(p['b_forget'], (0, LANES - N_FOX_HEADS)).reshape(1, LANES),
        wqm=wqm, wu=wu, wg=wg,
        qn_fox=jnp.tile(p['qn_fox'], N_FOX_HEADS).reshape(1, FOX_WIDTH),
        kn_fox=jnp.tile(p['kn_fox'], N_FOX_HEADS).reshape(1, FOX_WIDTH),
        qn_mem=p['qn_mem'].reshape(1, MEM_HEAD_DIM), bd=bd,
        w_glu=p['w_glu'].astype(BF16), w_br_fox=p['w_br_fox'].astype(BF16),
        w_br_ssm=p['w_br_ssm'].astype(BF16), w_br_mem=p['w_br_mem'].astype(BF16),
        w_out=p['w_out'].astype(BF16), norm_ffn=p['norm_ffn'].reshape(1, D_MODEL), w_router=w_router,
        moe_wg=p['moe_w_gate'].astype(BF16), moe_wu=p['moe_w_up'].astype(BF16),
        moe_wd=p['moe_w_down'].astype(BF16))


def _pick_tile(n, target):
    t = min(n, target)
    while n % t:
        t //= 2
    return t


def _group(x, w, mats, mem_k, mem_v, h0_re, h0_im, cache):
    b, s, _ = x.shape
    n = b * s
    x2d = x.reshape(n, D_MODEL)
    prompt = cache is None
    q, kb, vb, k_out, v_out, lf_t, q_m, u, gates = _inproj(x2d, w, _pick_tile(s if prompt else n, 512), s, prompt)
    q3 = q.reshape(b, s, FOX_WIDTH)
    k3 = kb.reshape(b, s, FOX_WIDTH)
    v3 = vb.reshape(b, s, FOX_WIDTH)
    lf_rows = lf_t.reshape(N_FOX_HEADS, b, s).transpose(1, 0, 2)
    lf3 = lf_rows.transpose(0, 2, 1)
    npair = N_FOX_HEADS // 2
    if prompt:
        c_row = LOG2E * _cumsum_rows(lf_rows.reshape(b * N_FOX_HEADS, s)).reshape(b, npair, 2, s)
        o_fox = _fox_prompt(q3, k3, v3, c_row, _pick_tile(s, 512), _pick_tile(s, 512))
        unt = lambda a: a.reshape(b, N_FOX_HEADS, FOX_HEAD_DIM, s).transpose(0, 3, 1, 2)
        k4, v4 = unt(k_out), unt(v_out)
    else:
        cache_k, cache_v, cache_logf = cache
        past = cache_k.shape[1]
        lf_all = jnp.concatenate([cache_logf.astype(F32).transpose(0, 2, 1), lf_rows], axis=2)
        c_row = LOG2E * _cumsum_rows(lf_all.reshape(b * N_FOX_HEADS, past + s)).reshape(b, N_FOX_HEADS, past + s)
        o_fox = _fox_sample(q3, cache_k.transpose(0, 2, 3, 1), cache_v.transpose(0, 2, 3, 1), k3, v3,
                            c_row[:, :, past:].transpose(0, 2, 1), c_row[:, :, :past], c_row[:, :, past:],
                            _pick_tile(past, 4096))
        k4 = k_out.reshape(b, s, N_FOX_HEADS, FOX_HEAD_DIM)
        v4 = v_out.reshape(b, s, N_FOX_HEADS, FOX_HEAD_DIM)
    ys, f_re, f_im = _ssm(u, b, h0_re, h0_im, mats)
    tm = _pick_tile(s, 512)
    x1, h2, route = _merge(x2d, o_fox.reshape(n, FOX_WIDTH), ys, q_m, gates, mem_k, mem_v, w, tm, s)
    y = _moe(h2, route, x1, w['moe_wg'], w['moe_wu'], w['moe_wd'], _pick_tile(n, 1024))
    return y.reshape(b, s, D_MODEL), k4, v4, lf3, f_re, f_im


def kernel(x_prompt, x_sample, mem_prompt, cache_fox_k, cache_fox_v, cache_fox_logf, state_ssm_re, state_ssm_im,
           cache_mem_k, cache_mem_v, norm_mix, w_in, b_forget, qn_fox, kn_fox, qn_mem, kn_mem, norm_mem, w_mem_kv,
           ssm_a_re, ssm_a_im, ssm_log_dt, ssm_b_re, ssm_b_im, ssm_c_re, ssm_c_im, ssm_d, w_glu, w_br_fox,
           w_br_ssm, w_br_mem, w_out, norm_ffn, w_router_group, w_router_expert, moe_w_gate, moe_w_up,
           moe_w_down):
    depth = norm_mix.shape[0]
    assert depth == 1
    l = 0
    p = dict(norm_mix=norm_mix[l], w_in=w_in[l], b_forget=b_forget[l], qn_fox=qn_fox[l], kn_fox=kn_fox[l],
             qn_mem=qn_mem[l], ssm_a_re=ssm_a_re[l], ssm_a_im=ssm_a_im[l], ssm_log_dt=ssm_log_dt[l],
             ssm_b_re=ssm_b_re[l], ssm_b_im=ssm_b_im[l], ssm_c_re=ssm_c_re[l], ssm_c_im=ssm_c_im[l],
             ssm_d=ssm_d[l], w_glu=w_glu[l], w_br_fox=w_br_fox[l], w_br_ssm=w_br_ssm[l], w_br_mem=w_br_mem[l],
             w_out=w_out[l], norm_ffn=norm_ffn[l], w_router_group=w_router_group[l],
             w_router_expert=w_router_expert[l], moe_w_gate=moe_w_gate[l], moe_w_up=moe_w_up[l],
             moe_w_down=moe_w_down[l])
    w = _prep_weights(p)
    mats = _ssm_mats(p)
    bp, sp, _ = x_prompt.shape
    bs, ss, _ = x_sample.shape

    mk, mv = _memkv(mem_prompt.reshape(bp * N_MEM, D_MODEL), norm_mem[l].reshape(1, D_MODEL),
                    w_mem_kv[l].astype(BF16), kn_mem[l].reshape(1, MEM_HEAD_DIM), _pick_tile(bp * N_MEM, 512))
    mem_rows = lambda a, b: a.reshape(b, N_MEM * N_MEM_HEADS, MEM_HEAD_DIM)
    mk = mem_rows(mk, bp)
    mv = mem_rows(mv, bp)
    zeros = jnp.zeros((bp, N_SSM_GROUPS, SSM_STATE), F32)
    yp, pk, pv, plf, pre, pim = _group(x_prompt, w, mats, mk, mv, zeros, zeros, None)
    cache = (cache_fox_k[l], cache_fox_v[l], cache_fox_logf[l])
    ys, sk, sv, slf, sre, sim = _group(
        x_sample, w, mats, mem_rows(cache_mem_k[l], bs), mem_rows(cache_mem_v[l], bs),
        state_ssm_re[l].astype(F32), state_ssm_im[l].astype(F32), cache)
    st = lambda a: a[None]
    return (yp, ys, st(pk), st(pv), st(plf), st(pre), st(pim),
            st(mk.reshape(bp, N_MEM, N_MEM_HEADS, MEM_HEAD_DIM)), st(mv.reshape(bp, N_MEM, N_MEM_HEADS, MEM_HEAD_DIM)),
            st(sk), st(sv), st(slf), st(sre), st(sim))
```

```python
import functools
import math

import jax
import jax.numpy as jnp
from jax import lax
from jax.experimental import pallas as pl
from jax.experimental.pallas import tpu as pltpu

F32 = jnp.float32
BF16 = jnp.bfloat16

D_MODEL = 1024
N_FOX_HEADS = 8
FOX_HEAD_DIM = 64
FOX_WIDTH = N_FOX_HEADS * FOX_HEAD_DIM
N_MEM = 256
N_MEM_HEADS = 4
MEM_HEAD_DIM = 128
MEM_WIDTH = N_MEM_HEADS * MEM_HEAD_DIM
SSM_GROUP = 16
SSM_WIDTH = 512
N_SSM_GROUPS = SSM_WIDTH // SSM_GROUP
SSM_STATE = 64
N_EXPERT_GROUPS = 4
EXPERTS_PER_GROUP = 8
N_EXPERTS = N_EXPERT_GROUPS * EXPERTS_PER_GROUP
D_EXPERT = 256
RMS_EPS = 1e-6
NEG_INF = -1e30
LOG2E = 1.4426950408889634

LANES = 128
SSM_CHUNK = 16
SSM_GPB = LANES // SSM_GROUP
GROUP_LANE = N_EXPERTS
MOE_SUB = 128
MOE_EPS = 4
VMEM_LIMIT = 56 * 1024 * 1024


def _dot(a, b):
    return jnp.dot(a, b, preferred_element_type=F32)


def _dot_nt(a, b):
    return lax.dot_general(a, b, (((1,), (1,)), ((), ())), preferred_element_type=F32)


def _dot_exact(a, b):
    return jnp.dot(a, b, preferred_element_type=F32, precision=lax.Precision.HIGHEST)


def _split_bf16(x):
    hi = x.astype(BF16)
    lo = (x - hi.astype(F32)).astype(BF16)
    return hi, lo


def _params(sem):
    return pltpu.CompilerParams(dimension_semantics=sem, vmem_limit_bytes=VMEM_LIMIT)


def _full(shape):
    n = len(shape)
    return pl.BlockSpec(shape, lambda *_: (0,) * n)


def _inproj_kernel(x_ref, g_ref, wqkv_ref, wf_ref, bf_ref, wqm_ref, wu_ref, wg_ref,
                   qn_ref, kn_ref, qmn_ref, bd_ref,
                   q_ref, kb_ref, vb_ref, k_ref, v_ref, lf_ref, qm_ref, u_ref, gate_ref, *, kv_transposed):
    x = x_ref[...]
    h = x * lax.rsqrt(jnp.mean(x * x, axis=-1, keepdims=True) + RMS_EPS) * g_ref[...]
    hb = h.astype(BF16)

    def head_norm(z, gain):
        ss = _dot((z * z).astype(BF16), bd_ref[...])
        return z * lax.rsqrt(ss * (1.0 / FOX_HEAD_DIM) + RMS_EPS) * gain

    zq = _dot(hb, wqkv_ref[:, 0:FOX_WIDTH])
    q_ref[...] = (head_norm(zq, qn_ref[...]) * (LOG2E * FOX_HEAD_DIM ** -0.5)).astype(BF16)
    zk = _dot(hb, wqkv_ref[:, FOX_WIDTH:2 * FOX_WIDTH])
    kn = head_norm(zk, kn_ref[...])
    zv = _dot(hb, wqkv_ref[:, 2 * FOX_WIDTH:3 * FOX_WIDTH])
    kb_ref[...] = kn.astype(BF16)
    vb_ref[...] = zv.astype(BF16)
    tm = x_ref.shape[0]
    if kv_transposed:
        k_ref[0] = kn.T
        v_ref[0] = zv.T
    else:
        for hd in range(N_FOX_HEADS):
            hs = slice(hd * FOX_HEAD_DIM, (hd + 1) * FOX_HEAD_DIM)
            rows = pl.ds(hd, tm, stride=N_FOX_HEADS)
            k_ref[rows, :] = kn[:, hs]
            v_ref[rows, :] = zv[:, hs]

    zf = (_dot(hb, wf_ref[...]) + bf_ref[...]).T[0:N_FOX_HEADS, :]
    lf_ref[...] = jnp.minimum(zf, 0.0) - jnp.log1p(jnp.exp(-jnp.abs(zf)))

    zm = _dot(hb, wqm_ref[...])
    for hd in range(N_MEM_HEADS):
        sl = slice(hd * MEM_HEAD_DIM, (hd + 1) * MEM_HEAD_DIM)
        zh = zm[:, sl]
        ms = jnp.mean(zh * zh, axis=-1, keepdims=True)
        qm_ref[:, sl] = (zh * lax.rsqrt(ms + RMS_EPS) * qmn_ref[...] * (MEM_HEAD_DIM ** -0.5)).astype(BF16)

    u_ref[...] = _dot(hb, wu_ref[...])
    for c in range(3):
        sl = slice(c * D_MODEL, (c + 1) * D_MODEL)
        gate_ref[:, sl] = (0.5 * jnp.tanh(0.5 * _dot(hb, wg_ref[:, sl])) + 0.5).astype(BF16)


def _inproj(x2d, w, tm, seq, kv_transposed):
    n = x2d.shape[0]
    assert n % tm == 0
    row = lambda width: pl.BlockSpec((tm, width), lambda i: (i, 0))
    if kv_transposed:
        assert seq % tm == 0
        per = seq // tm
        kv_shape = jax.ShapeDtypeStruct((n // seq, FOX_WIDTH, seq), F32)
        heads = pl.BlockSpec((1, FOX_WIDTH, tm), lambda i: (i // per, 0, i % per))
    else:
        kv_shape = jax.ShapeDtypeStruct((n * N_FOX_HEADS, FOX_HEAD_DIM), F32)
        heads = pl.BlockSpec((tm * N_FOX_HEADS, FOX_HEAD_DIM), lambda i: (i, 0))
    ins = [x2d, w['norm_mix'], w['wqkv'], w['wf'], w['bf'], w['wqm'], w['wu'], w['wg'],
           w['qn_fox'], w['kn_fox'], w['qn_mem'], w['bd']]
    in_specs = [row(D_MODEL)] + [_full(a.shape) for a in ins[1:]]
    out_shape = (
        jax.ShapeDtypeStruct((n, FOX_WIDTH), BF16),
        jax.ShapeDtypeStruct((n, FOX_WIDTH), BF16),
        jax.ShapeDtypeStruct((n, FOX_WIDTH), BF16),
        kv_shape,
        kv_shape,
        jax.ShapeDtypeStruct((N_FOX_HEADS, n), F32),
        jax.ShapeDtypeStruct((n, MEM_WIDTH), BF16),
        jax.ShapeDtypeStruct((n, SSM_WIDTH), F32),
        jax.ShapeDtypeStruct((n, 3 * D_MODEL), BF16),
    )
    out_specs = (row(FOX_WIDTH), row(FOX_WIDTH), row(FOX_WIDTH), heads, heads,
                 pl.BlockSpec((N_FOX_HEADS, tm), lambda i: (0, i)),
                 row(MEM_WIDTH), row(SSM_WIDTH), row(3 * D_MODEL))
    return pl.pallas_call(
        functools.partial(_inproj_kernel, kv_transposed=kv_transposed),
        out_shape=out_shape, grid=(n // tm,), in_specs=in_specs, out_specs=out_specs,
        compiler_params=_params(("parallel",)), name="inproj")(*ins)


def _memkv_kernel(x_ref, g_ref, w_ref, kn_ref, k_ref, v_ref):
    x = x_ref[...]
    h = x * lax.rsqrt(jnp.mean(x * x, axis=-1, keepdims=True) + RMS_EPS) * g_ref[...]
    hb = h.astype(BF16)
    tm = x_ref.shape[0]
    zk = _dot(hb, w_ref[:, 0:MEM_WIDTH])
    zv = _dot(hb, w_ref[:, MEM_WIDTH:2 * MEM_WIDTH])
    for hd in range(N_MEM_HEADS):
        sl = slice(hd * MEM_HEAD_DIM, (hd + 1) * MEM_HEAD_DIM)
        rows = pl.ds(hd, tm, stride=N_MEM_HEADS)
        zh = zk[:, sl]
        ms = jnp.mean(zh * zh, axis=-1, keepdims=True)
        k_ref[rows, :] = zh * lax.rsqrt(ms + RMS_EPS) * kn_ref[...]
        v_ref[rows, :] = zv[:, sl]


def _memkv(mem2d, norm_mem, w_kv, kn_mem, tm):
    n = mem2d.shape[0]
    out = jax.ShapeDtypeStruct((n * N_MEM_HEADS, MEM_HEAD_DIM), F32)
    ospec = pl.BlockSpec((tm * N_MEM_HEADS, MEM_HEAD_DIM), lambda i: (i, 0))
    return pl.pallas_call(
        _memkv_kernel,
        out_shape=(out, out),
        grid=(n // tm,),
        in_specs=[pl.BlockSpec((tm, D_MODEL), lambda i: (i, 0)), _full(norm_mem.shape), _full(w_kv.shape),
                  _full(kn_mem.shape)],
        out_specs=(ospec, ospec),
        compiler_params=_params(("parallel",)), name="memkv")(mem2d, norm_mem, w_kv, kn_mem)


CUMSUM_BLOCK = 256


def _cumsum_kernel(x_ref, o_ref):
    nblk = x_ref.shape[1] // CUMSUM_BLOCK
    r = lax.broadcasted_iota(jnp.int32, (CUMSUM_BLOCK, CUMSUM_BLOCK), 0)
    c = lax.broadcasted_iota(jnp.int32, (CUMSUM_BLOCK, CUMSUM_BLOCK), 1)
    tri = (r <= c).astype(F32)
    carry = jnp.zeros((x_ref.shape[0], 1), F32)
    for j in range(nblk):
        sl = slice(j * CUMSUM_BLOCK, (j + 1) * CUMSUM_BLOCK)
        cs = _dot_exact(x_ref[:, sl], tri) + carry
        o_ref[:, sl] = cs
        carry = cs[:, CUMSUM_BLOCK - 1:CUMSUM_BLOCK]


def _cumsum_rows(x):
    rows, n = x.shape
    npad = -(-n // CUMSUM_BLOCK) * CUMSUM_BLOCK
    xp = jnp.pad(x, ((0, 0), (0, npad - n))) if npad != n else x
    out = pl.pallas_call(
        _cumsum_kernel, out_shape=jax.ShapeDtypeStruct((rows, npad), F32), grid=(1,),
        in_specs=[_full((rows, npad))], out_specs=_full((rows, npad)),
        compiler_params=_params(("arbitrary",)), name="cumsum")(xp)
    return out[:, :n] if npad != n else out


def _reduce_rows(x, op):
    rows, cols = x.shape
    if rows > 64 and rows % 64 == 0:
        x = op(x.reshape(rows // 64, 64, cols), axis=0)
        rows = 64
    if rows == 64:
        x = op(x.reshape(8, 8, cols), axis=0)
    return op(x, axis=0, keepdims=True)


def _fox_prompt_kernel(q_ref, k_ref, v_ref, cr_ref, o_ref,
                       vt_ref, ck0_ref, ck1_ref, st0_ref, st1_ref, pt0_ref, pt1_ref, acc_ref, *, tq, tk):
    st_refs = (st0_ref, st1_ref)
    pt_refs = (pt0_ref, pt1_ref)
    i = pl.program_id(2)
    s_len = k_ref.shape[1]

    @pl.when(i == 0)
    def _():
        vt_ref[...] = v_ref[0].astype(F32).T.astype(BF16)
        ck0_ref[...] = jnp.broadcast_to(cr_ref[0, 0, 0:1, :], (LANES, s_len)).T
        ck1_ref[...] = jnp.broadcast_to(cr_ref[0, 0, 1:2, :], (LANES, s_len)).T

    qt = q_ref[0].astype(F32).T
    row = lax.broadcasted_iota(jnp.int32, (LANES, tq), 0)
    qts = (jnp.where(row < FOX_HEAD_DIM, qt, 0.0).astype(BF16), jnp.where(row < FOX_HEAD_DIM, 0.0, qt).astype(BF16))
    q0 = pl.multiple_of(i * tq, tq)
    cq = cr_ref[0, 0, :, pl.ds(q0, tq)]
    ck_refs = (ck0_ref, ck1_ref)

    def stage_a(n, par):
        s = pl.multiple_of(n * tk, tk)
        kb = k_ref[0, pl.ds(s, tk), :]
        for hh in range(2):
            ck = ck_refs[hh][pl.ds(s, tk), :]
            st_refs[par][hh] = _dot(kb, qts[hh]) - jnp.concatenate([ck] * (tq // LANES), axis=1)

    def stage_b(n, par, stats, masked):
        if masked:
            kpos = n * tk + lax.broadcasted_iota(jnp.int32, (tk, tq), 0)
            qpos = q0 + lax.broadcasted_iota(jnp.int32, (tk, tq), 1)
            mask = kpos <= qpos
        out = []
        for hh in range(2):
            m, l = stats[2 * hh:2 * hh + 2]
            t = st_refs[par][hh]
            if masked:
                t = jnp.where(mask, t, NEG_INF)
            cqh = cq[hh:hh + 1, :]
            m_new = jnp.maximum(m, _reduce_rows(t, jnp.max) + cqh)
            alpha = jnp.exp2(m - m_new)
            p = jnp.exp2(t + (cqh - m_new))
            pt_refs[par][hh] = p.astype(BF16)
            out.extend([m_new, alpha * l + _reduce_rows(p, jnp.sum), alpha])
        return tuple(out)

    def stage_c(n, par, alphas):
        s = pl.multiple_of(jnp.maximum(n, 0) * tk, tk)
        for hh in range(2):
            vt = vt_ref[hh * FOX_HEAD_DIM:(hh + 1) * FOX_HEAD_DIM, pl.ds(s, tk)]
            acc_ref[hh] = alphas[hh] * acc_ref[hh] + _dot(vt, pt_refs[par][hh])

    def iteration(n, par, carry):
        m0, l0, al0, m1, l1, al1 = carry
        stage_c(n - 1, 1 - par, (al0, al1))
        new = stage_b(n, par, (m0, l0, m1, l1), False)
        stage_a(n + 1, 1 - par)
        return new

    def finish(par, carry):
        m0, l0, al0, m1, l1, al1 = carry
        stage_c(nfull - 1, 1 - par, (al0, al1))
        _, l0, be0, _, l1, be1 = stage_b(nfull, par, (m0, l0, m1, l1), True)
        stage_c(nfull, par, (be0, be1))
        ot = jnp.concatenate([acc_ref[0] / l0, acc_ref[1] / l1], axis=0)
        o_ref[0] = ot.T.astype(o_ref.dtype)

    acc_ref[...] = jnp.zeros(acc_ref.shape, F32)
    pt1_ref[...] = jnp.zeros(pt1_ref.shape, BF16)
    neg = jnp.full((1, tq), NEG_INF, F32)
    zero = jnp.zeros((1, tq), F32)
    one = jnp.ones((1, tq), F32)
    nfull = (i * tq) // tk
    stage_a(0, 0)
    carry = lax.fori_loop(0, nfull // 2, lambda k, c: iteration(2 * k + 1, 1, iteration(2 * k, 0, c)),
                          (neg, zero, one, neg, zero, one))
    odd = nfull % 2 == 1
    carry = lax.cond(odd, lambda c: iteration(nfull - 1, 0, c), lambda c: c, carry)
    pl.when(odd)(lambda: finish(1, carry))
    pl.when(jnp.logical_not(odd))(lambda: finish(0, carry))


def _fox_prompt(q, k, v, c_row, tq, tk):
    b, s, _ = q.shape
    assert s % tk == 0 and tk % tq == 0
    npair = N_FOX_HEADS // 2
    return pl.pallas_call(
        functools.partial(_fox_prompt_kernel, tq=tq, tk=tk),
        out_shape=jax.ShapeDtypeStruct((b, s, FOX_WIDTH), BF16),
        grid=(b, npair, s // tq),
        in_specs=[
            pl.BlockSpec((1, tq, LANES), lambda bi, hp, i: (bi, i, hp)),
            pl.BlockSpec((1, s, LANES), lambda bi, hp, i: (bi, 0, hp)),
            pl.BlockSpec((1, s, LANES), lambda bi, hp, i: (bi, 0, hp)),
            pl.BlockSpec((1, 1, 2, s), lambda bi, hp, i: (bi, hp, 0, 0)),
        ],
        out_specs=pl.BlockSpec((1, tq, LANES), lambda bi, hp, i: (bi, i, hp)),
        scratch_shapes=[pltpu.VMEM((LANES, s), BF16),
                        pltpu.VMEM((s, LANES), F32), pltpu.VMEM((s, LANES), F32),
                        pltpu.VMEM((2, tk, tq), F32), pltpu.VMEM((2, tk, tq), F32),
                        pltpu.VMEM((2, tk, tq), BF16), pltpu.VMEM((2, tk, tq), BF16),
                        pltpu.VMEM((2, FOX_HEAD_DIM, tq), F32)],
        compiler_params=_params(("parallel", "parallel", "arbitrary")), name="fox_prompt")(q, k, v, c_row)


def _fox_sample_kernel(q_ref, ck_ref, cv_ref, nk_ref, nv_ref, cq_ref, crc_ref, crn_ref, o_ref, *state, n):
    j = pl.program_id(1)
    nj = pl.num_programs(1)
    m_refs = state[0:N_FOX_HEADS]
    l_refs = state[N_FOX_HEADS:2 * N_FOX_HEADS]
    acc_refs = state[2 * N_FOX_HEADS:3 * N_FOX_HEADS]

    @pl.when(j == 0)
    def _():
        for hd in range(N_FOX_HEADS):
            m_refs[hd][...] = jnp.full(m_refs[hd].shape, NEG_INF, F32)
            l_refs[hd][...] = jnp.zeros(l_refs[hd].shape, F32)
            acc_refs[hd][...] = jnp.zeros(acc_refs[hd].shape, F32)

    def update(k_of, v_of, cr_ref_, mask, transposed):
        qk = _dot if transposed else _dot_nt
        pv = _dot_nt if transposed else _dot
        ts = []
        for hd in range(N_FOX_HEADS):
            hs = slice(hd * FOX_HEAD_DIM, (hd + 1) * FOX_HEAD_DIM)
            t = qk(q_ref[0, :, hs], k_of(hd)) - cr_ref_[0, hd:hd + 1, :]
            ts.append(t if mask is None else jnp.where(mask, t, NEG_INF))
        ps = []
        for hd in range(N_FOX_HEADS):
            cq = cq_ref[0, :, hd:hd + 1]
            m = m_refs[hd][...]
            m_new = jnp.maximum(m, jnp.max(ts[hd], axis=-1, keepdims=True) + cq)
            alpha = jnp.exp2(m - m_new)
            p = jnp.exp2(ts[hd] + (cq - m_new))
            m_refs[hd][...] = m_new
            l_refs[hd][...] = alpha * l_refs[hd][...] + jnp.sum(p, axis=-1, keepdims=True)
            ps.append((alpha, p.astype(BF16)))
        for hd in range(N_FOX_HEADS):
            alpha, p = ps[hd]
            acc_refs[hd][...] = alpha * acc_refs[hd][...] + pv(p, v_of(hd))

    update(lambda hd: ck_ref[0, hd].astype(BF16), lambda hd: cv_ref[0, hd].astype(BF16), crc_ref, None, True)

    @pl.when(j == nj - 1)
    def _():
        r = lax.broadcasted_iota(jnp.int32, (n, n), 0)
        c = lax.broadcasted_iota(jnp.int32, (n, n), 1)
        head = lambda ref: (lambda hd: ref[0, :, hd * FOX_HEAD_DIM:(hd + 1) * FOX_HEAD_DIM])
        update(head(nk_ref), head(nv_ref), crn_ref, c <= r, False)
        for hd in range(N_FOX_HEADS):
            hs = slice(hd * FOX_HEAD_DIM, (hd + 1) * FOX_HEAD_DIM)
            o_ref[0, :, hs] = (acc_refs[hd][...] / l_refs[hd][...]).astype(o_ref.dtype)


def _fox_sample(q, cache_k, cache_v, k_new, v_new, c_q, c_row_cache, c_row_new, tk):
    b, n, _ = q.shape
    past = cache_k.shape[3]
    assert past % tk == 0
    cache_spec = pl.BlockSpec((1, N_FOX_HEADS, FOX_HEAD_DIM, tk), lambda bi, j: (bi, 0, 0, j))
    return pl.pallas_call(
        functools.partial(_fox_sample_kernel, n=n),
        out_shape=jax.ShapeDtypeStruct((b, n, FOX_WIDTH), BF16),
        grid=(b, past // tk),
        in_specs=[
            pl.BlockSpec((1, n, FOX_WIDTH), lambda bi, j: (bi, 0, 0)),
            cache_spec,
            cache_spec,
            pl.BlockSpec((1, n, FOX_WIDTH), lambda bi, j: (bi, 0, 0)),
            pl.BlockSpec((1, n, FOX_WIDTH), lambda bi, j: (bi, 0, 0)),
            pl.BlockSpec((1, n, N_FOX_HEADS), lambda bi, j: (bi, 0, 0)),
            pl.BlockSpec((1, N_FOX_HEADS, tk), lambda bi, j: (bi, 0, j)),
            pl.BlockSpec((1, N_FOX_HEADS, n), lambda bi, j: (bi, 0, 0)),
        ],
        out_specs=pl.BlockSpec((1, n, FOX_WIDTH), lambda bi, j: (bi, 0, 0)),
        scratch_shapes=([pltpu.VMEM((n, 1), F32)] * (2 * N_FOX_HEADS)
                        + [pltpu.VMEM((n, FOX_HEAD_DIM), F32)] * N_FOX_HEADS),
        compiler_params=_params(("parallel", "arbitrary")), name="fox_sample")(
            q, cache_k, cache_v, k_new, v_new, c_q, c_row_cache, c_row_new)


def _ssm_mats(p):
    f32 = F32
    a_re, a_im = p['ssm_a_re'].astype(f32), p['ssm_a_im'].astype(f32)
    b_re, b_im = p['ssm_b_re'].astype(f32), p['ssm_b_im'].astype(f32)
    c_re, c_im = p['ssm_c_re'].astype(f32), p['ssm_c_im'].astype(f32)
    dt = jnp.exp(p['ssm_log_dt'].astype(f32))[:, None]
    mag = jnp.exp(dt * a_re)
    ab_re = mag * jnp.cos(dt * a_im)
    ab_im = mag * jnp.sin(dt * a_im)
    den = a_re * a_re + a_im * a_im
    nr, ni = ab_re - 1.0, ab_im
    coef_re = (nr * a_re + ni * a_im) / den
    coef_im = (ni * a_re - nr * a_im) / den
    bb_re = coef_re[..., None] * b_re - coef_im[..., None] * b_im
    bb_im = coef_re[..., None] * b_im + coef_im[..., None] * b_re
    pr, pi = [jnp.ones_like(ab_re)], [jnp.zeros_like(ab_im)]
    for _ in range(SSM_CHUNK):
        pr.append(pr[-1] * ab_re - pi[-1] * ab_im)
        pi.append(pr[-2] * ab_im + pi[-1] * ab_re)
    pw_re, pw_im = jnp.stack(pr), jnp.stack(pi)
    T = SSM_CHUNK
    w_re = pw_re[..., None] * bb_re[None] - pw_im[..., None] * bb_im[None]
    w_im = pw_re[..., None] * bb_im[None] + pw_im[..., None] * bb_re[None]
    kk = (jnp.einsum('gop,kgpi->kgoi', c_re, w_re[:T], precision='highest')
          - jnp.einsum('gop,kgpi->kgoi', c_im, w_im[:T], precision='highest'))
    nq = N_SSM_GROUPS // SSM_GPB

    def group_diag(m):
        rows, c = m.shape[-2:]
        m = jnp.tile(m, (1,) * (m.ndim - 1) + (SSM_GPB,))
        same = (jnp.arange(rows) // (rows // SSM_GPB))[:, None] == (jnp.arange(SSM_GPB * c) // c)[None, :]
        return jnp.where(same, m, 0.0)

    def lane_diag(m):
        lead = m.shape[:-3]
        i, c = m.shape[-2:]
        return group_diag(m.reshape(lead + (nq, SSM_GPB * i, c)))

    ktau = lane_diag(jnp.swapaxes(kk, -1, -2))
    ktau = jnp.concatenate([jnp.zeros_like(ktau[:1]), ktau], axis=0)
    units = []
    for dlag in range(T // 2 - 1, -1, -1):
        top = jnp.concatenate([ktau[2 * dlag + 1], ktau[2 * dlag + 2]], axis=-1)
        bot = jnp.concatenate([ktau[2 * dlag], ktau[2 * dlag + 1]], axis=-1)
        units.append(jnp.concatenate([top, bot], axis=-2))
    kstack = jnp.concatenate(units, axis=-2).astype(BF16)
    rev = T - 1 - jnp.arange(T)
    def local_rows(w):
        w = jnp.transpose(w[rev], (1, 0, 3, 2)).reshape(nq, SSM_GPB, T, SSM_GROUP, SSM_STATE)
        return group_diag(jnp.transpose(w, (0, 2, 1, 3, 4)).reshape(nq, T, LANES, SSM_STATE))

    m_all = jnp.concatenate([local_rows(w_re), local_rows(w_im)], axis=-1)
    m_all = m_all.reshape(nq, T * LANES, 2 * SSM_GPB * SSM_STATE)
    m_hi, m_lo = _split_bf16(m_all)
    ar, ai = pw_re[1:], pw_im[1:]
    n_re = (c_re[None] * ar[:, :, None, :] - c_im[None] * ai[:, :, None, :])
    n_im = -(c_re[None] * ai[:, :, None, :] + c_im[None] * ar[:, :, None, :])

    def state_rows(n):
        n = jnp.transpose(n, (1, 3, 0, 2)).reshape(nq, SSM_GPB * SSM_STATE, T, SSM_GROUP)
        n = jnp.tile(jnp.transpose(n, (0, 2, 1, 3)), (1, 1, 1, SSM_GPB))
        same = (jnp.arange(SSM_GPB * SSM_STATE) // SSM_STATE)[:, None] == (jnp.arange(LANES) // SSM_GROUP)[None, :]
        return jnp.where(same, n, 0.0)

    n_all = jnp.concatenate([state_rows(n_re), state_rows(n_im)], axis=2).astype(BF16)
    return dict(kstack=kstack, m_hi=m_hi, m_lo=m_lo, n_all=n_all,
                a16_re=pw_re[T].reshape(8, 256), a16_im=pw_im[T].reshape(8, 256),
                d=p['ssm_d'].astype(f32).reshape(1, SSM_WIDTH))


def _chunk_tokens(u_ref, rows):
    return [u_ref[pl.ds(t, rows, stride=SSM_CHUNK), :] for t in range(SSM_CHUNK)]


def _ssm_local_kernel(u_ref, mh_ref, ml_ref, hre_ref, him_ref):
    rows = hre_ref.shape[0]
    parts = [_split_bf16(ut) for ut in _chunk_tokens(u_ref, rows)]
    x_hi = jnp.concatenate([h for h, _ in parts], axis=1)
    x_lo = jnp.concatenate([l for _, l in parts], axis=1)
    h = _dot(x_hi, mh_ref[0]) + _dot(x_hi, ml_ref[0]) + _dot(x_lo, mh_ref[0])
    half = SSM_GPB * SSM_STATE
    hre_ref[...] = h[:, 0:half]
    him_ref[...] = h[:, half:2 * half]


def _ssm_local(u2d, mats, rows):
    n = u2d.shape[0]
    r = n // SSM_CHUNK
    nq = N_SSM_GROUPS // SSM_GPB
    half = SSM_GPB * SSM_STATE
    mspec = pl.BlockSpec((1, SSM_CHUNK * LANES, 2 * half), lambda q, i: (q, 0, 0))
    ospec = pl.BlockSpec((rows, half), lambda q, i: (i, q))
    return pl.pallas_call(
        _ssm_local_kernel,
        out_shape=(jax.ShapeDtypeStruct((r, N_SSM_GROUPS * SSM_STATE), F32),) * 2,
        grid=(nq, r // rows),
        in_specs=[pl.BlockSpec((rows * SSM_CHUNK, LANES), lambda q, i: (i, q)), mspec, mspec],
        out_specs=(ospec, ospec),
        compiler_params=_params(("parallel", "parallel")), name="ssm_local")(u2d, mats['m_hi'], mats['m_lo'])


def _ssm_scan_kernel(lre_ref, lim_ref, are_ref, aim_ref, h0re_ref, h0im_ref,
                     pre_ref, pim_ref, fre_ref, fim_ref):
    nchunk = lre_ref.shape[1]
    ar, ai = are_ref[...], aim_ref[...]

    def body(c, carry):
        hr, hi = carry
        pre_ref[0, c] = hr
        pim_ref[0, c] = hi
        return (ar * hr - ai * hi + lre_ref[0, c], ar * hi + ai * hr + lim_ref[0, c])

    hr, hi = lax.fori_loop(0, nchunk, body, (h0re_ref[0], h0im_ref[0]))
    fre_ref[0] = hr
    fim_ref[0] = hi


def _ssm_scan(hloc_re, hloc_im, mats, h0_re, h0_im):
    b, nchunk = hloc_re.shape[:2]
    big = pl.BlockSpec((1, nchunk, 8, 256), lambda i: (i, 0, 0, 0))
    small = pl.BlockSpec((1, 8, 256), lambda i: (i, 0, 0))
    return pl.pallas_call(
        _ssm_scan_kernel,
        out_shape=(jax.ShapeDtypeStruct(hloc_re.shape, F32),) * 2 + (jax.ShapeDtypeStruct((b, 8, 256), F32),) * 2,
        grid=(b,),
        in_specs=[big, big, _full((8, 256)), _full((8, 256)), small, small],
        out_specs=(big, big, small, small),
        compiler_params=_params(("parallel",)), name="ssm_scan")(
            hloc_re, hloc_im, mats['a16_re'], mats['a16_im'], h0_re, h0_im)


def _gelu_tanh(y):
    return 0.5 * y * (1.0 + jnp.tanh(math.sqrt(2.0 / math.pi) * (y + 0.044715 * (y * y * y))))


def _ssm_out_kernel(u_ref, k_ref, pre_ref, pim_ref, n_ref, d_ref, y_ref, ysc_ref):
    rows = pre_ref.shape[0]
    us = _chunk_tokens(u_ref, rows)
    x = jnp.concatenate([ut.astype(BF16) for ut in us], axis=1)
    hp = jnp.concatenate([pre_ref[...], pim_ref[...]], axis=1).astype(BF16)
    unit = 2 * LANES
    nunit = SSM_CHUNK // 2
    for j in range(nunit):
        n_unit = jnp.concatenate([n_ref[0, 2 * j], n_ref[0, 2 * j + 1]], axis=1)
        yj = _dot(x[:, 0:unit * (j + 1)], k_ref[0, unit * (nunit - 1 - j):, :]) + _dot(hp, n_unit)
        for t2 in range(2):
            t = 2 * j + t2
            y = yj[:, t2 * LANES:(t2 + 1) * LANES] + d_ref[...] * us[t]
            ysc_ref[pl.ds(t, rows, stride=SSM_CHUNK), :] = _gelu_tanh(y)
    y_ref[...] = ysc_ref[...].astype(y_ref.dtype)


def _ssm_out(u2d, hprev_re, hprev_im, mats, rows):
    n = u2d.shape[0]
    r = n // SSM_CHUNK
    nq = N_SSM_GROUPS // SSM_GPB
    half = SSM_GPB * SSM_STATE
    uspec = pl.BlockSpec((rows * SSM_CHUNK, LANES), lambda q, i: (i, q))
    hspec = pl.BlockSpec((rows, half), lambda q, i: (i, q))
    return pl.pallas_call(
        _ssm_out_kernel,
        out_shape=jax.ShapeDtypeStruct((n, SSM_WIDTH), BF16),
        grid=(nq, r // rows),
        in_specs=[uspec, pl.BlockSpec((1, SSM_CHUNK * LANES, 2 * LANES), lambda q, i: (q, 0, 0)), hspec, hspec,
                  pl.BlockSpec((1, SSM_CHUNK, 2 * half, LANES), lambda q, i: (q, 0, 0, 0)),
                  pl.BlockSpec((1, LANES), lambda q, i: (0, q))],
        out_specs=uspec,
        scratch_shapes=[pltpu.VMEM((rows * SSM_CHUNK, LANES), F32)],
        compiler_params=_params(("parallel", "parallel")), name="ssm_out")(
            u2d, mats['kstack'], hprev_re, hprev_im, mats['n_all'], mats['d'])


def _ssm(u2d, b, h0_re, h0_im, mats):
    n = u2d.shape[0]
    nchunk = n // b // SSM_CHUNK
    r = b * nchunk
    rows = _pick_tile(r, 256)
    hloc_re, hloc_im = _ssm_local(u2d, mats, rows)
    shp = (b, nchunk, 8, 256)
    hprev_re, hprev_im, f_re, f_im = _ssm_scan(hloc_re.reshape(shp), hloc_im.reshape(shp), mats,
                                               h0_re.reshape(b, 8, 256), h0_im.reshape(b, 8, 256))
    y = _ssm_out(u2d, hprev_re.reshape(r, -1), hprev_im.reshape(r, -1), mats, rows)
    return y, f_re.reshape(b, N_SSM_GROUPS, SSM_STATE), f_im.reshape(b, N_SSM_GROUPS, SSM_STATE)


def _merge_kernel(x_ref, of_ref, ys_ref, qm_ref, gate_ref, mk_ref, mv_ref,
                  wglu_ref, wbf_ref, wbs_ref, wbm_ref, wo_ref, nf_ref, wr_ref,
                  x1_ref, h2_ref, r_ref):
    tm = x_ref.shape[0]
    om = []
    for hd in range(N_MEM_HEADS):
        sl = slice(hd * MEM_HEAD_DIM, (hd + 1) * MEM_HEAD_DIM)
        head_rows = pl.ds(hd, N_MEM, stride=N_MEM_HEADS)
        kh = mk_ref[0, head_rows, :].astype(BF16)
        vh = mv_ref[0, head_rows, :].astype(BF16)
        sc = _dot_nt(qm_ref[:, sl], kh)
        p = jnp.exp(sc - jnp.max(sc, axis=-1, keepdims=True))
        om.append(_dot(p.astype(BF16), vh) / jnp.sum(p, axis=-1, keepdims=True))
    o_mem = jnp.concatenate(om, axis=-1).astype(BF16)
    z = _dot(ys_ref[...], wglu_ref[...])
    y_ssm = (z[:, 0:SSM_WIDTH] * jax.nn.sigmoid(z[:, SSM_WIDTH:2 * SSM_WIDTH])).astype(BF16)
    g = lambda c: gate_ref[:, c * D_MODEL:(c + 1) * D_MODEL].astype(F32)
    merged = (g(0) * _dot(of_ref[...], wbf_ref[...]) + g(1) * _dot(y_ssm, wbs_ref[...])
              + g(2) * _dot(o_mem, wbm_ref[...]))
    x1 = x_ref[...] + _dot(merged.astype(BF16), wo_ref[...])
    x1_ref[...] = x1
    h2 = x1 * lax.rsqrt(jnp.mean(x1 * x1, axis=-1, keepdims=True) + RMS_EPS) * nf_ref[...]
    h2_ref[...] = h2.astype(BF16)
    h2_hi, h2_lo = _split_bf16(h2)
    hw = _dot(h2_hi, wr_ref[...])
    logits = hw[:, 0:LANES] + hw[:, LANES:2 * LANES] + _dot(h2_lo, wr_ref[:, 0:LANES])
    lane = lax.broadcasted_iota(jnp.int32, (tm, LANES), 1)
    big = jnp.int32(LANES)
    is_grp = (lane >= N_EXPERTS) & (lane < N_EXPERTS + N_EXPERT_GROUPS)
    gl = jnp.where(is_grp, logits, NEG_INF)
    gmax = jnp.max(gl, axis=-1, keepdims=True)
    grp = jnp.min(jnp.where(is_grp & (gl == gmax), lane, big), axis=-1, keepdims=True) - N_EXPERTS
    g_w = 1.0 / jnp.sum(jnp.where(is_grp, jnp.exp(gl - gmax), 0.0), axis=-1, keepdims=True)
    in_grp = (lane >= grp * EXPERTS_PER_GROUP) & (lane < (grp + 1) * EXPERTS_PER_GROUP)
    e1 = jnp.where(in_grp, logits, NEG_INF)
    m1 = jnp.max(e1, axis=-1, keepdims=True)
    i1 = jnp.min(jnp.where(in_grp & (e1 == m1), lane, big), axis=-1, keepdims=True)
    rest = in_grp & (lane != i1)
    e2 = jnp.where(rest, logits, NEG_INF)
    m2 = jnp.max(e2, axis=-1, keepdims=True)
    i2 = jnp.min(jnp.where(rest & (e2 == m2), lane, big), axis=-1, keepdims=True)
    ex = jnp.exp(m2 - m1)
    w1 = g_w / (1.0 + ex)
    w2 = g_w * ex / (1.0 + ex)
    r_ref[...] = jnp.where(lane == i1, w1, jnp.where(lane == i2, w2, jnp.where(lane == GROUP_LANE, grp.astype(F32), 0.0)))


def _merge(x2d, o_fox, ys, q_m, gates, mem_k, mem_v, w, tm, rows_per_batch):
    n = x2d.shape[0]
    assert n % tm == 0 and rows_per_batch % tm == 0
    per = rows_per_batch // tm
    row = lambda width: pl.BlockSpec((tm, width), lambda i: (i, 0))
    memspec = pl.BlockSpec((1, N_MEM * N_MEM_HEADS, MEM_HEAD_DIM), lambda i: (i // per, 0, 0))
    ws = [w['w_glu'], w['w_br_fox'], w['w_br_ssm'], w['w_br_mem'], w['w_out'], w['norm_ffn'], w['w_router']]
    return pl.pallas_call(
        _merge_kernel,
        out_shape=(jax.ShapeDtypeStruct((n, D_MODEL), F32), jax.ShapeDtypeStruct((n, D_MODEL), BF16),
                   jax.ShapeDtypeStruct((n, LANES), F32)),
        grid=(n // tm,),
        in_specs=[row(D_MODEL), row(FOX_WIDTH), row(SSM_WIDTH), row(MEM_WIDTH), row(3 * D_MODEL), memspec, memspec]
                 + [_full(a.shape) for a in ws],
        out_specs=(row(D_MODEL), row(D_MODEL), row(LANES)),
        compiler_params=_params(("parallel",)), name="merge")(
            x2d, o_fox, ys, q_m, gates, mem_k, mem_v, *ws)


def _moe_kernel(h_ref, r_ref, x1_ref, tri_ref, wg_ref, wu_ref, wd_ref, o_ref,
                xs_ref, cw_ref, og_ref, acc_ref, rank_ref, nblk_ref, *, main):
    step = pl.program_id(1)
    steps_per_group = EXPERTS_PER_GROUP // MOE_EPS
    g = step // steps_per_group
    tm = h_ref.shape[0]
    gf = g.astype(F32)
    bounds = [0, main] + list(range(-(-main // MOE_SUB) * MOE_SUB, tm, MOE_SUB)) + [tm]
    bounds = sorted(set(bounds))
    blocks = [(lo, hi - lo, lo > 0) for lo, hi in zip(bounds[:-1], bounds[1:])]

    def guarded(r0, fn):
        if r0 == 0:
            fn()
        else:
            pl.when(r0 < nblk_ref[0])(fn)

    @pl.when(step == 0)
    def _():
        acc_ref[...] = jnp.zeros(acc_ref.shape, F32)

    @pl.when(step % steps_per_group == 0)
    def _():
        rt = r_ref[...]
        rtt = rt.T
        mrow = rtt[GROUP_LANE:GROUP_LANE + 1, :] == gf
        m8 = jnp.broadcast_to(jnp.where(mrow, 1.0, 0.0), (8, tm))
        rank8 = _dot(m8.astype(BF16), tri_ref[...])
        rank_row = jnp.where(mrow, rank8[0:1, :], -1.0)
        rank_ref[...] = jnp.broadcast_to(jnp.where(mrow, rank8, -1.0).T[:, 0:1], rank_ref.shape)
        nblk_ref[0] = jnp.sum(jnp.where(mrow, 1, 0))
        hilo = jnp.concatenate(_split_bf16(rt), axis=1)
        for r0, nrows, _ in blocks:
            def compact(r0=r0, nrows=nrows):
                rows = slice(r0, r0 + nrows)
                slot = r0 + lax.broadcasted_iota(jnp.int32, (nrows, tm), 0)
                perm = jnp.where(rank_row == slot.astype(F32), 1.0, 0.0).astype(BF16)
                xs_ref[rows, :] = _dot(perm, h_ref[...]).astype(BF16)
                cw = _dot(perm, hilo)
                cw_ref[rows, :] = cw[:, 0:LANES] + cw[:, LANES:2 * LANES]
                og_ref[rows, :] = jnp.zeros((nrows, D_MODEL), F32)
            guarded(r0, compact)

    for k in range(MOE_EPS):
        e = step * MOE_EPS + k
        for r0, nrows, _ in blocks:
            def expert(r0=r0, nrows=nrows, k=k, e=e):
                rows = slice(r0, r0 + nrows)
                x = xs_ref[rows, :]
                a = _dot(x, wg_ref[k])
                up = _dot(x, wu_ref[k])
                lane = lax.broadcasted_iota(jnp.int32, (nrows, LANES), 1)
                ce = jnp.sum(jnp.where(lane == e, cw_ref[rows, :], 0.0), axis=-1, keepdims=True)
                act = a * jax.nn.sigmoid(a) * up * ce
                og_ref[rows, :] += _dot(act.astype(BF16), wd_ref[k])
            guarded(r0, expert)

    @pl.when(step % steps_per_group == steps_per_group - 1)
    def _():
        for r0, nrows, _ in blocks:
            def scatter_back(r0=r0, nrows=nrows):
                rows = slice(r0, r0 + nrows)
                slot = r0 + lax.broadcasted_iota(jnp.int32, (tm, nrows), 1)
                back = jnp.where(rank_ref[:, 0:1] == slot.astype(F32), 1.0, 0.0).astype(BF16)
                acc_ref[...] += _dot(back, og_ref[rows, :].astype(BF16))
            guarded(r0, scatter_back)

    @pl.when(step == pl.num_programs(1) - 1)
    def _():
        o_ref[...] = x1_ref[...] + acc_ref[...]


def _moe(h2, route, x1, wg, wu, wd, tm):
    n = h2.shape[0]
    assert n % tm == 0 and tm % MOE_SUB == 0
    main = max(MOE_SUB // 2, (5 * tm // 16) // 64 * 64)
    row = lambda width: pl.BlockSpec((tm, width), lambda i, s: (i, 0))
    r = jnp.arange(tm)
    tri = (r[:, None] < r[None, :]).astype(BF16)
    return pl.pallas_call(
        functools.partial(_moe_kernel, main=main),
        out_shape=jax.ShapeDtypeStruct((n, D_MODEL), F32),
        grid=(n // tm, N_EXPERTS // MOE_EPS),
        in_specs=[row(D_MODEL), row(LANES), row(D_MODEL), pl.BlockSpec((tm, tm), lambda i, s: (0, 0)),
                  pl.BlockSpec((MOE_EPS, D_MODEL, D_EXPERT), lambda i, s: (s, 0, 0)),
                  pl.BlockSpec((MOE_EPS, D_MODEL, D_EXPERT), lambda i, s: (s, 0, 0)),
                  pl.BlockSpec((MOE_EPS, D_EXPERT, D_MODEL), lambda i, s: (s, 0, 0))],
        out_specs=row(D_MODEL),
        scratch_shapes=[pltpu.VMEM((tm, D_MODEL), BF16), pltpu.VMEM((tm, LANES), F32), pltpu.VMEM((tm, D_MODEL), F32),
                        pltpu.VMEM((tm, D_MODEL), F32), pltpu.VMEM((tm, LANES), F32), pltpu.SMEM((1,), jnp.int32)],
        compiler_params=_params(("parallel", "arbitrary")), name="moe")(h2, route, x1, tri, wg, wu, wd)


def _prep_weights(p):
    w_in = p['w_in'].astype(BF16)
    o = 0
    wqkv = w_in[:, 0:3 * FOX_WIDTH]
    o = 3 * FOX_WIDTH
    wf = jnp.pad(w_in[:, o:o + N_FOX_HEADS], ((0, 0), (0, LANES - N_FOX_HEADS)))
    o += N_FOX_HEADS
    wqm = w_in[:, o:o + MEM_WIDTH]
    o += MEM_WIDTH
    wu = w_in[:, o:o + SSM_WIDTH]
    o += SSM_WIDTH
    wg = w_in[:, o:o + 3 * D_MODEL]
    r = jnp.arange(FOX_WIDTH) // FOX_HEAD_DIM
    bd = (r[:, None] == r[None, :]).astype(BF16)
    w_router = jnp.concatenate(
        [p['w_router_expert'], p['w_router_group'],
         jnp.zeros((D_MODEL, LANES - N_EXPERTS - N_EXPERT_GROUPS), F32)], axis=1)
    w_router = jnp.concatenate(_split_bf16(w_router), axis=1)
    return dict(
        norm_mix=p['norm_mix'].reshape(1, D_MODEL), wqkv=wqkv, wf=wf,
        bf=jnp.pad(p['b_forget'], (0, LANES - N_FOX_HEADS)).reshape(1, LANES),
        wqm=wqm, wu=wu, wg=wg,
        qn_fox=jnp.tile(p['qn_fox'], N_FOX_HEADS).reshape(1, FOX_WIDTH),
        kn_fox=jnp.tile(p['kn_fox'], N_FOX_HEADS).reshape(1, FOX_WIDTH),
        qn_mem=p['qn_mem'].reshape(1, MEM_HEAD_DIM), bd=bd,
        w_glu=p['w_glu'].astype(BF16), w_br_fox=p['w_br_fox'].astype(BF16),
        w_br_ssm=p['w_br_ssm'].astype(BF16), w_br_mem=p['w_br_mem'].astype(BF16),
        w_out=p['w_out'].astype(BF16), norm_ffn=p['norm_ffn'].reshape(1, D_MODEL), w_router=w_router,
        moe_wg=p['moe_w_gate'].astype(BF16), moe_wu=p['moe_w_up'].astype(BF16),
        moe_wd=p['moe_w_down'].astype(BF16))


def _pick_tile(n, target):
    t = min(n, target)
    while n % t:
        t //= 2
    return t


def _group(x, w, mats, mem_k, mem_v, h0_re, h0_im, cache):
    b, s, _ = x.shape
    n = b * s
    x2d = x.reshape(n, D_MODEL)
    prompt = cache is None
    q, kb, vb, k_out, v_out, lf_t, q_m, u, gates = _inproj(x2d, w, _pick_tile(s if prompt else n, 512), s, prompt)
    q3 = q.reshape(b, s, FOX_WIDTH)
    k3 = kb.reshape(b, s, FOX_WIDTH)
    v3 = vb.reshape(b, s, FOX_WIDTH)
    lf_rows = lf_t.reshape(N_FOX_HEADS, b, s).transpose(1, 0, 2)
    lf3 = lf_rows.transpose(0, 2, 1)
    npair = N_FOX_HEADS // 2
    if prompt:
        c_row = LOG2E * _cumsum_rows(lf_rows.reshape(b * N_FOX_HEADS, s)).reshape(b, npair, 2, s)
        o_fox = _fox_prompt(q3, k3, v3, c_row, _pick_tile(s, 512), _pick_tile(s, 512))
        unt = lambda a: a.reshape(b, N_FOX_HEADS, FOX_HEAD_DIM, s).transpose(0, 3, 1, 2)
        k4, v4 = unt(k_out), unt(v_out)
    else:
        cache_k, cache_v, cache_logf = cache
        past = cache_k.shape[1]
        lf_all = jnp.concatenate([cache_logf.astype(F32).transpose(0, 2, 1), lf_rows], axis=2)
        c_row = LOG2E * _cumsum_rows(lf_all.reshape(b * N_FOX_HEADS, past + s)).reshape(b, N_FOX_HEADS, past + s)
        o_fox = _fox_sample(q3, cache_k.transpose(0, 2, 3, 1), cache_v.transpose(0, 2, 3, 1), k3, v3,
                            c_row[:, :, past:].transpose(0, 2, 1), c_row[:, :, :past], c_row[:, :, past:],
                            _pick_tile(past, 4096))
        k4 = k_out.reshape(b, s, N_FOX_HEADS, FOX_HEAD_DIM)
        v4 = v_out.reshape(b, s, N_FOX_HEADS, FOX_HEAD_DIM)
    ys, f_re, f_im = _ssm(u, b, h0_re, h0_im, mats)
    tm = _pick_tile(s, 512)
    x1, h2, route = _merge(x2d, o_fox.reshape(n, FOX_WIDTH), ys, q_m, gates, mem_k, mem_v, w, tm, s)
    y = _moe(h2, route, x1, w['moe_wg'], w['moe_wu'], w['moe_wd'], _pick_tile(n, 1024))
    return y.reshape(b, s, D_MODEL), k4, v4, lf3, f_re, f_im


def kernel(x_prompt, x_sample, mem_prompt, cache_fox_k, cache_fox_v, cache_fox_logf, state_ssm_re, state_ssm_im,
           cache_mem_k, cache_mem_v, norm_mix, w_in, b_forget, qn_fox, kn_fox, qn_mem, kn_mem, norm_mem, w_mem_kv,
           ssm_a_re, ssm_a_im, ssm_log_dt, ssm_b_re, ssm_b_im, ssm_c_re, ssm_c_im, ssm_d, w_glu, w_br_fox,
           w_br_ssm, w_br_mem, w_out, norm_ffn, w_router_group, w_router_expert, moe_w_gate, moe_w_up,
           moe_w_down):
    depth = norm_mix.shape[0]
    assert depth == 1
    l = 0
    p = dict(norm_mix=norm_mix[l], w_in=w_in[l], b_forget=b_forget[l], qn_fox=qn_fox[l], kn_fox=kn_fox[l],
             qn_mem=qn_mem[l], ssm_a_re=ssm_a_re[l], ssm_a_im=ssm_a_im[l], ssm_log_dt=ssm_log_dt[l],
             ssm_b_re=ssm_b_re[l], ssm_b_im=ssm_b_im[l], ssm_c_re=ssm_c_re[l], ssm_c_im=ssm_c_im[l],
             ssm_d=ssm_d[l], w_glu=w_glu[l], w_br_fox=w_br_fox[l], w_br_ssm=w_br_ssm[l], w_br_mem=w_br_mem[l],
             w_out=w_out[l], norm_ffn=norm_ffn[l], w_router_group=w_router_group[l],
             w_router_expert=w_router_expert[l], moe_w_gate=moe_w_gate[l], moe_w_up=moe_w_up[l],
             moe_w_down=moe_w_down[l])
    w = _prep_weights(p)
    mats = _ssm_mats(p)
    bp, sp, _ = x_prompt.shape
    bs, ss, _ = x_sample.shape

    mk, mv = _memkv(mem_prompt.reshape(bp * N_MEM, D_MODEL), norm_mem[l].reshape(1, D_MODEL),
                    w_mem_kv[l].astype(BF16), kn_mem[l].reshape(1, MEM_HEAD_DIM), _pick_tile(bp * N_MEM, 512))
    mem_rows = lambda a, b: a.reshape(b, N_MEM * N_MEM_HEADS, MEM_HEAD_DIM)
    mk = mem_rows(mk, bp)
    mv = mem_rows(mv, bp)
    zeros = jnp.zeros((bp, N_SSM_GROUPS, SSM_STATE), F32)
    yp, pk, pv, plf, pre, pim = _group(x_prompt, w, mats, mk, mv, zeros, zeros, None)
    cache = (cache_fox_k[l], cache_fox_v[l], cache_fox_logf[l])
    ys, sk, sv, slf, sre, sim = _group(
        x_sample, w, mats, mem_rows(cache_mem_k[l], bs), mem_rows(cache_mem_v[l], bs),
        state_ssm_re[l].astype(F32), state_ssm_im[l].astype(F32), cache)
    st = lambda a: a[None]
    return (yp, ys, st(pk), st(pv), st(plf), st(pre), st(pim),
            st(mk.reshape(bp, N_MEM, N_MEM_HEADS, MEM_HEAD_DIM)), st(mv.reshape(bp, N_MEM, N_MEM_HEADS, MEM_HEAD_DIM)),
            st(sk), st(sv), st(slf), st(sre), st(sim))
```

```python
import functools
import math

import jax
import jax.numpy as jnp
from jax import lax
from jax.experimental import pallas as pl
from jax.experimental.pallas import tpu as pltpu

F32 = jnp.float32
BF16 = jnp.bfloat16

D_MODEL = 1024
N_FOX_HEADS = 8
FOX_HEAD_DIM = 64
FOX_WIDTH = N_FOX_HEADS * FOX_HEAD_DIM
N_MEM = 256
N_MEM_HEADS = 4
MEM_HEAD_DIM = 128
MEM_WIDTH = N_MEM_HEADS * MEM_HEAD_DIM
SSM_GROUP = 16
SSM_WIDTH = 512
N_SSM_GROUPS = SSM_WIDTH // SSM_GROUP
SSM_STATE = 64
N_EXPERT_GROUPS = 4
EXPERTS_PER_GROUP = 8
N_EXPERTS = N_EXPERT_GROUPS * EXPERTS_PER_GROUP
D_EXPERT = 256
RMS_EPS = 1e-6
NEG_INF = -1e30
LOG2E = 1.4426950408889634

LANES = 128
SSM_CHUNK = 16
SSM_GPB = LANES // SSM_GROUP
GROUP_LANE = N_EXPERTS
MOE_SUB = 128
MOE_EPS = 4
VMEM_LIMIT = 56 * 1024 * 1024
STATE_TILE = (8, N_SSM_GROUPS * SSM_STATE // 8)

TILE_INPROJ = 512
TILE_MEMKV = 512
TILE_FOX_Q = 512
TILE_FOX_K = 512
TILE_SAMPLE_K = 4096
TILE_SSM_ROWS = 256
TILE_MERGE = 512
TILE_MOE = 1024


def _dot(a, b):
    return jnp.dot(a, b, preferred_element_type=F32)


def _dot_nt(a, b):
    return lax.dot_general(a, b, (((1,), (1,)), ((), ())), preferred_element_type=F32)


def _dot_exact(a, b):
    return jnp.dot(a, b, preferred_element_type=F32, precision=lax.Precision.HIGHEST)


def _split_bf16(x):
    hi = x.astype(BF16)
    lo = (x - hi.astype(F32)).astype(BF16)
    return hi, lo


def _params(sem):
    return pltpu.CompilerParams(dimension_semantics=sem, vmem_limit_bytes=VMEM_LIMIT)


def _full(shape):
    n = len(shape)
    return pl.BlockSpec(shape, lambda *_: (0,) * n)


def _inproj_kernel(x_ref, g_ref, wqkv_ref, wf_ref, bf_ref, wqm_ref, wu_ref, wg_ref,
                   qn_ref, kn_ref, qmn_ref, bd_ref,
                   q_ref, kb_ref, vb_ref, k_ref, v_ref, lf_ref, qm_ref, u_ref, gate_ref, *, kv_transposed):
    x = x_ref[...]
    h = x * lax.rsqrt(jnp.mean(x * x, axis=-1, keepdims=True) + RMS_EPS) * g_ref[...]
    hb = h.astype(BF16)

    def head_norm(z, gain):
        ss = _dot((z * z).astype(BF16), bd_ref[...])
        return z * lax.rsqrt(ss * (1.0 / FOX_HEAD_DIM) + RMS_EPS) * gain

    zq = _dot(hb, wqkv_ref[:, 0:FOX_WIDTH])
    q_ref[...] = (head_norm(zq, qn_ref[...]) * (LOG2E * FOX_HEAD_DIM ** -0.5)).astype(BF16)
    zk = _dot(hb, wqkv_ref[:, FOX_WIDTH:2 * FOX_WIDTH])
    kn = head_norm(zk, kn_ref[...])
    zv = _dot(hb, wqkv_ref[:, 2 * FOX_WIDTH:3 * FOX_WIDTH])
    kb_ref[...] = kn.astype(BF16)
    vb_ref[...] = zv.astype(BF16)
    tm = x_ref.shape[0]
    if kv_transposed:
        k_ref[0] = kn.T
        v_ref[0] = zv.T
    else:
        for hd in range(N_FOX_HEADS):
            hs = slice(hd * FOX_HEAD_DIM, (hd + 1) * FOX_HEAD_DIM)
            rows = pl.ds(hd, tm, stride=N_FOX_HEADS)
            k_ref[rows, :] = kn[:, hs]
            v_ref[rows, :] = zv[:, hs]

    zf = (_dot(hb, wf_ref[...]) + bf_ref[...]).T[0:N_FOX_HEADS, :]
    lf_ref[...] = jnp.minimum(zf, 0.0) - jnp.log1p(jnp.exp(-jnp.abs(zf)))

    zm = _dot(hb, wqm_ref[...])
    for hd in range(N_MEM_HEADS):
        sl = slice(hd * MEM_HEAD_DIM, (hd + 1) * MEM_HEAD_DIM)
        zh = zm[:, sl]
        ms = jnp.mean(zh * zh, axis=-1, keepdims=True)
        qm_ref[:, sl] = (zh * lax.rsqrt(ms + RMS_EPS) * qmn_ref[...] * (MEM_HEAD_DIM ** -0.5)).astype(BF16)

    u_ref[...] = _dot(hb, wu_ref[...])
    for c in range(3):
        sl = slice(c * D_MODEL, (c + 1) * D_MODEL)
        gate_ref[:, sl] = (0.5 * jnp.tanh(0.5 * _dot(hb, wg_ref[:, sl])) + 0.5).astype(BF16)


def _inproj(x2d, w, tm, seq, kv_transposed):
    n = x2d.shape[0]
    assert n % tm == 0
    row = lambda width: pl.BlockSpec((tm, width), lambda i: (i, 0))
    if kv_transposed:
        assert seq % tm == 0
        per = seq // tm
        kv_shape = jax.ShapeDtypeStruct((n // seq, FOX_WIDTH, seq), F32)
        heads = pl.BlockSpec((1, FOX_WIDTH, tm), lambda i: (i // per, 0, i % per))
    else:
        kv_shape = jax.ShapeDtypeStruct((n * N_FOX_HEADS, FOX_HEAD_DIM), F32)
        heads = pl.BlockSpec((tm * N_FOX_HEADS, FOX_HEAD_DIM), lambda i: (i, 0))
    ins = [x2d, w['norm_mix'], w['wqkv'], w['wf'], w['bf'], w['wqm'], w['wu'], w['wg'],
           w['qn_fox'], w['kn_fox'], w['qn_mem'], w['bd']]
    in_specs = [row(D_MODEL)] + [_full(a.shape) for a in ins[1:]]
    out_shape = (
        jax.ShapeDtypeStruct((n, FOX_WIDTH), BF16),
        jax.ShapeDtypeStruct((n, FOX_WIDTH), BF16),
        jax.ShapeDtypeStruct((n, FOX_WIDTH), BF16),
        kv_shape,
        kv_shape,
        jax.ShapeDtypeStruct((N_FOX_HEADS, n), F32),
        jax.ShapeDtypeStruct((n, MEM_WIDTH), BF16),
        jax.ShapeDtypeStruct((n, SSM_WIDTH), F32),
        jax.ShapeDtypeStruct((n, 3 * D_MODEL), BF16),
    )
    out_specs = (row(FOX_WIDTH), row(FOX_WIDTH), row(FOX_WIDTH), heads, heads,
                 pl.BlockSpec((N_FOX_HEADS, tm), lambda i: (0, i)),
                 row(MEM_WIDTH), row(SSM_WIDTH), row(3 * D_MODEL))
    return pl.pallas_call(
        functools.partial(_inproj_kernel, kv_transposed=kv_transposed),
        out_shape=out_shape, grid=(n // tm,), in_specs=in_specs, out_specs=out_specs,
        compiler_params=_params(("parallel",)), name="inproj")(*ins)


def _memkv_kernel(x_ref, g_ref, w_ref, kn_ref, k_ref, v_ref):
    x = x_ref[...]
    h = x * lax.rsqrt(jnp.mean(x * x, axis=-1, keepdims=True) + RMS_EPS) * g_ref[...]
    hb = h.astype(BF16)
    tm = x_ref.shape[0]
    zk = _dot(hb, w_ref[:, 0:MEM_WIDTH])
    zv = _dot(hb, w_ref[:, MEM_WIDTH:2 * MEM_WIDTH])
    for hd in range(N_MEM_HEADS):
        sl = slice(hd * MEM_HEAD_DIM, (hd + 1) * MEM_HEAD_DIM)
        rows = pl.ds(hd, tm, stride=N_MEM_HEADS)
        zh = zk[:, sl]
        ms = jnp.mean(zh * zh, axis=-1, keepdims=True)
        k_ref[rows, :] = zh * lax.rsqrt(ms + RMS_EPS) * kn_ref[...]
        v_ref[rows, :] = zv[:, sl]


def _memkv(mem2d, norm_mem, w_kv, kn_mem, tm):
    n = mem2d.shape[0]
    out = jax.ShapeDtypeStruct((n * N_MEM_HEADS, MEM_HEAD_DIM), F32)
    ospec = pl.BlockSpec((tm * N_MEM_HEADS, MEM_HEAD_DIM), lambda i: (i, 0))
    return pl.pallas_call(
        _memkv_kernel,
        out_shape=(out, out),
        grid=(n // tm,),
        in_specs=[pl.BlockSpec((tm, D_MODEL), lambda i: (i, 0)), _full(norm_mem.shape), _full(w_kv.shape),
                  _full(kn_mem.shape)],
        out_specs=(ospec, ospec),
        compiler_params=_params(("parallel",)), name="memkv")(mem2d, norm_mem, w_kv, kn_mem)


CUMSUM_BLOCK = 256


def _cumsum_kernel(x_ref, o_ref):
    nblk = x_ref.shape[1] // CUMSUM_BLOCK
    r = lax.broadcasted_iota(jnp.int32, (CUMSUM_BLOCK, CUMSUM_BLOCK), 0)
    c = lax.broadcasted_iota(jnp.int32, (CUMSUM_BLOCK, CUMSUM_BLOCK), 1)
    tri = (r <= c).astype(F32)
    carry = jnp.zeros((x_ref.shape[0], 1), F32)
    for j in range(nblk):
        sl = slice(j * CUMSUM_BLOCK, (j + 1) * CUMSUM_BLOCK)
        cs = _dot_exact(x_ref[:, sl], tri) + carry
        o_ref[:, sl] = cs
        carry = cs[:, CUMSUM_BLOCK - 1:CUMSUM_BLOCK]


def _cumsum_rows(x):
    rows, n = x.shape
    npad = -(-n // CUMSUM_BLOCK) * CUMSUM_BLOCK
    xp = jnp.pad(x, ((0, 0), (0, npad - n))) if npad != n else x
    out = pl.pallas_call(
        _cumsum_kernel, out_shape=jax.ShapeDtypeStruct((rows, npad), F32), grid=(1,),
        in_specs=[_full((rows, npad))], out_specs=_full((rows, npad)),
        compiler_params=_params(("arbitrary",)), name="cumsum")(xp)
    return out[:, :n] if npad != n else out


def _reduce_rows(x, op):
    rows, cols = x.shape
    if rows > 64 and rows % 64 == 0:
        x = op(x.reshape(rows // 64, 64, cols), axis=0)
        rows = 64
    if rows == 64:
        x = op(x.reshape(8, 8, cols), axis=0)
    return op(x, axis=0, keepdims=True)


def _fox_prompt_kernel(q_ref, k_ref, v_ref, cr_ref, o_ref,
                       vt_ref, ck0_ref, ck1_ref, st0_ref, st1_ref, pt0_ref, pt1_ref, acc_ref, *, tq, tk):
    st_refs = (st0_ref, st1_ref)
    pt_refs = (pt0_ref, pt1_ref)
    i = pl.program_id(2)
    s_len = k_ref.shape[1]

    @pl.when(i == 0)
    def _():
        vt_ref[...] = v_ref[0].astype(F32).T.astype(BF16)
        ck0_ref[...] = jnp.broadcast_to(cr_ref[0, 0, 0:1, :], (LANES, s_len)).T
        ck1_ref[...] = jnp.broadcast_to(cr_ref[0, 0, 1:2, :], (LANES, s_len)).T

    qt = q_ref[0].astype(F32).T
    row = lax.broadcasted_iota(jnp.int32, (LANES, tq), 0)
    qts = (jnp.where(row < FOX_HEAD_DIM, qt, 0.0).astype(BF16), jnp.where(row < FOX_HEAD_DIM, 0.0, qt).astype(BF16))
    q0 = pl.multiple_of(i * tq, tq)
    cq = cr_ref[0, 0, :, pl.ds(q0, tq)]
    ck_refs = (ck0_ref, ck1_ref)

    def stage_a(n, par):
        s = pl.multiple_of(n * tk, tk)
        kb = k_ref[0, pl.ds(s, tk), :]
        for hh in range(2):
            ck = ck_refs[hh][pl.ds(s, tk), :]
            st_refs[par][hh] = _dot(kb, qts[hh]) - jnp.concatenate([ck] * (tq // LANES), axis=1)

    def stage_b(n, par, stats, masked):
        if masked:
            kpos = n * tk + lax.broadcasted_iota(jnp.int32, (tk, tq), 0)
            qpos = q0 + lax.broadcasted_iota(jnp.int32, (tk, tq), 1)
            mask = kpos <= qpos
        out = []
        for hh in range(2):
            m, l = stats[2 * hh:2 * hh + 2]
            t = st_refs[par][hh]
            if masked:
                t = jnp.where(mask, t, NEG_INF)
            cqh = cq[hh:hh + 1, :]
            m_new = jnp.maximum(m, _reduce_rows(t, jnp.max) + cqh)
            alpha = jnp.exp2(m - m_new)
            p = jnp.exp2(t + (cqh - m_new))
            pt_refs[par][hh] = p.astype(BF16)
            out.extend([m_new, alpha * l + _reduce_rows(p, jnp.sum), alpha])
        return tuple(out)

    def stage_c(n, par, alphas):
        s = pl.multiple_of(jnp.maximum(n, 0) * tk, tk)
        for hh in range(2):
            vt = vt_ref[hh * FOX_HEAD_DIM:(hh + 1) * FOX_HEAD_DIM, pl.ds(s, tk)]
            acc_ref[hh] = alphas[hh] * acc_ref[hh] + _dot(vt, pt_refs[par][hh])

    def iteration(n, par, carry):
        m0, l0, al0, m1, l1, al1 = carry
        stage_c(n - 1, 1 - par, (al0, al1))
        new = stage_b(n, par, (m0, l0, m1, l1), False)
        stage_a(n + 1, 1 - par)
        return new

    def finish(par, carry):
        m0, l0, al0, m1, l1, al1 = carry
        stage_c(nfull - 1, 1 - par, (al0, al1))
        _, l0, be0, _, l1, be1 = stage_b(nfull, par, (m0, l0, m1, l1), True)
        stage_c(nfull, par, (be0, be1))
        ot = jnp.concatenate([acc_ref[0] / l0, acc_ref[1] / l1], axis=0)
        o_ref[0] = ot.T.astype(o_ref.dtype)

    acc_ref[...] = jnp.zeros(acc_ref.shape, F32)
    pt1_ref[...] = jnp.zeros(pt1_ref.shape, BF16)
    neg = jnp.full((1, tq), NEG_INF, F32)
    zero = jnp.zeros((1, tq), F32)
    one = jnp.ones((1, tq), F32)
    nfull = (i * tq) // tk
    stage_a(0, 0)
    carry = lax.fori_loop(0, nfull // 2, lambda k, c: iteration(2 * k + 1, 1, iteration(2 * k, 0, c)),
                          (neg, zero, one, neg, zero, one))
    odd = nfull % 2 == 1
    carry = lax.cond(odd, lambda c: iteration(nfull - 1, 0, c), lambda c: c, carry)
    pl.when(odd)(lambda: finish(1, carry))
    pl.when(jnp.logical_not(odd))(lambda: finish(0, carry))


def _fox_prompt(q, k, v, c_row, tq, tk):
    b, s, _ = q.shape
    assert s % tk == 0 and tk % tq == 0
    npair = N_FOX_HEADS // 2
    return pl.pallas_call(
        functools.partial(_fox_prompt_kernel, tq=tq, tk=tk),
        out_shape=jax.ShapeDtypeStruct((b, s, FOX_WIDTH), BF16),
        grid=(b, npair, s // tq),
        in_specs=[
            pl.BlockSpec((1, tq, LANES), lambda bi, hp, i: (bi, i, hp)),
            pl.BlockSpec((1, s, LANES), lambda bi, hp, i: (bi, 0, hp)),
            pl.BlockSpec((1, s, LANES), lambda bi, hp, i: (bi, 0, hp)),
            pl.BlockSpec((1, 1, 2, s), lambda bi, hp, i: (bi, hp, 0, 0)),
        ],
        out_specs=pl.BlockSpec((1, tq, LANES), lambda bi, hp, i: (bi, i, hp)),
        scratch_shapes=[pltpu.VMEM((LANES, s), BF16),
                        pltpu.VMEM((s, LANES), F32), pltpu.VMEM((s, LANES), F32),
                        pltpu.VMEM((2, tk, tq), F32), pltpu.VMEM((2, tk, tq), F32),
                        pltpu.VMEM((2, tk, tq), BF16), pltpu.VMEM((2, tk, tq), BF16),
                        pltpu.VMEM((2, FOX_HEAD_DIM, tq), F32)],
        compiler_params=_params(("parallel", "parallel", "arbitrary")), name="fox_prompt")(q, k, v, c_row)


def _fox_sample_kernel(q_ref, ck_ref, cv_ref, nk_ref, nv_ref, cq_ref, crc_ref, crn_ref, o_ref, *state, n):
    j = pl.program_id(1)
    nj = pl.num_programs(1)
    m_refs = state[0:N_FOX_HEADS]
    l_refs = state[N_FOX_HEADS:2 * N_FOX_HEADS]
    acc_refs = state[2 * N_FOX_HEADS:3 * N_FOX_HEADS]

    @pl.when(j == 0)
    def _():
        for hd in range(N_FOX_HEADS):
            m_refs[hd][...] = jnp.full(m_refs[hd].shape, NEG_INF, F32)
            l_refs[hd][...] = jnp.zeros(l_refs[hd].shape, F32)
            acc_refs[hd][...] = jnp.zeros(acc_refs[hd].shape, F32)

    def update(k_of, v_of, cr_ref_, mask, transposed):
        qk = _dot if transposed else _dot_nt
        pv = _dot_nt if transposed else _dot
        ts = []
        for hd in range(N_FOX_HEADS):
            hs = slice(hd * FOX_HEAD_DIM, (hd + 1) * FOX_HEAD_DIM)
            t = qk(q_ref[0, :, hs], k_of(hd)) - cr_ref_[0, hd:hd + 1, :]
            ts.append(t if mask is None else jnp.where(mask, t, NEG_INF))
        ps = []
        for hd in range(N_FOX_HEADS):
            cq = cq_ref[0, :, hd:hd + 1]
            m = m_refs[hd][...]
            m_new = jnp.maximum(m, jnp.max(ts[hd], axis=-1, keepdims=True) + cq)
            alpha = jnp.exp2(m - m_new)
            p = jnp.exp2(ts[hd] + (cq - m_new))
            m_refs[hd][...] = m_new
            l_refs[hd][...] = alpha * l_refs[hd][...] + jnp.sum(p, axis=-1, keepdims=True)
            ps.append((alpha, p.astype(BF16)))
        for hd in range(N_FOX_HEADS):
            alpha, p = ps[hd]
            acc_refs[hd][...] = alpha * acc_refs[hd][...] + pv(p, v_of(hd))

    update(lambda hd: ck_ref[0, hd].astype(BF16), lambda hd: cv_ref[0, hd].astype(BF16), crc_ref, None, True)

    @pl.when(j == nj - 1)
    def _():
        r = lax.broadcasted_iota(jnp.int32, (n, n), 0)
        c = lax.broadcasted_iota(jnp.int32, (n, n), 1)
        head = lambda ref: (lambda hd: ref[0, :, hd * FOX_HEAD_DIM:(hd + 1) * FOX_HEAD_DIM])
        update(head(nk_ref), head(nv_ref), crn_ref, c <= r, False)
        for hd in range(N_FOX_HEADS):
            hs = slice(hd * FOX_HEAD_DIM, (hd + 1) * FOX_HEAD_DIM)
            o_ref[0, :, hs] = (acc_refs[hd][...] / l_refs[hd][...]).astype(o_ref.dtype)


def _fox_sample(q, cache_k, cache_v, k_new, v_new, c_q, c_row_cache, c_row_new, tk):
    b, n, _ = q.shape
    past = cache_k.shape[3]
    assert past % tk == 0
    cache_spec = pl.BlockSpec((1, N_FOX_HEADS, FOX_HEAD_DIM, tk), lambda bi, j: (bi, 0, 0, j))
    return pl.pallas_call(
        functools.partial(_fox_sample_kernel, n=n),
        out_shape=jax.ShapeDtypeStruct((b, n, FOX_WIDTH), BF16),
        grid=(b, past // tk),
        in_specs=[
            pl.BlockSpec((1, n, FOX_WIDTH), lambda bi, j: (bi, 0, 0)),
            cache_spec,
            cache_spec,
            pl.BlockSpec((1, n, FOX_WIDTH), lambda bi, j: (bi, 0, 0)),
            pl.BlockSpec((1, n, FOX_WIDTH), lambda bi, j: (bi, 0, 0)),
            pl.BlockSpec((1, n, N_FOX_HEADS), lambda bi, j: (bi, 0, 0)),
            pl.BlockSpec((1, N_FOX_HEADS, tk), lambda bi, j: (bi, 0, j)),
            pl.BlockSpec((1, N_FOX_HEADS, n), lambda bi, j: (bi, 0, 0)),
        ],
        out_specs=pl.BlockSpec((1, n, FOX_WIDTH), lambda bi, j: (bi, 0, 0)),
        scratch_shapes=([pltpu.VMEM((n, 1), F32)] * (2 * N_FOX_HEADS)
                        + [pltpu.VMEM((n, FOX_HEAD_DIM), F32)] * N_FOX_HEADS),
        compiler_params=_params(("parallel", "arbitrary")), name="fox_sample")(
            q, cache_k, cache_v, k_new, v_new, c_q, c_row_cache, c_row_new)


def _ssm_mats(p):
    f32 = F32
    a_re, a_im = p['ssm_a_re'].astype(f32), p['ssm_a_im'].astype(f32)
    b_re, b_im = p['ssm_b_re'].astype(f32), p['ssm_b_im'].astype(f32)
    c_re, c_im = p['ssm_c_re'].astype(f32), p['ssm_c_im'].astype(f32)
    dt = jnp.exp(p['ssm_log_dt'].astype(f32))[:, None]
    mag = jnp.exp(dt * a_re)
    ab_re = mag * jnp.cos(dt * a_im)
    ab_im = mag * jnp.sin(dt * a_im)
    den = a_re * a_re + a_im * a_im
    nr, ni = ab_re - 1.0, ab_im
    coef_re = (nr * a_re + ni * a_im) / den
    coef_im = (ni * a_re - nr * a_im) / den
    bb_re = coef_re[..., None] * b_re - coef_im[..., None] * b_im
    bb_im = coef_re[..., None] * b_im + coef_im[..., None] * b_re
    pr, pi = [jnp.ones_like(ab_re)], [jnp.zeros_like(ab_im)]
    for _ in range(SSM_CHUNK):
        pr.append(pr[-1] * ab_re - pi[-1] * ab_im)
        pi.append(pr[-2] * ab_im + pi[-1] * ab_re)
    pw_re, pw_im = jnp.stack(pr), jnp.stack(pi)
    T = SSM_CHUNK
    w_re = pw_re[..., None] * bb_re[None] - pw_im[..., None] * bb_im[None]
    w_im = pw_re[..., None] * bb_im[None] + pw_im[..., None] * bb_re[None]
    kk = (jnp.einsum('gop,kgpi->kgoi', c_re, w_re[:T], precision='highest')
          - jnp.einsum('gop,kgpi->kgoi', c_im, w_im[:T], precision='highest'))
    nq = N_SSM_GROUPS // SSM_GPB

    def group_diag(m):
        rows, c = m.shape[-2:]
        m = jnp.tile(m, (1,) * (m.ndim - 1) + (SSM_GPB,))
        same = (jnp.arange(rows) // (rows // SSM_GPB))[:, None] == (jnp.arange(SSM_GPB * c) // c)[None, :]
        return jnp.where(same, m, 0.0)

    def lane_diag(m):
        lead = m.shape[:-3]
        i, c = m.shape[-2:]
        return group_diag(m.reshape(lead + (nq, SSM_GPB * i, c)))

    ktau = lane_diag(jnp.swapaxes(kk, -1, -2))
    ktau = jnp.concatenate([jnp.zeros_like(ktau[:1]), ktau], axis=0)
    units = []
    for dlag in range(T // 2 - 1, -1, -1):
        top = jnp.concatenate([ktau[2 * dlag + 1], ktau[2 * dlag + 2]], axis=-1)
        bot = jnp.concatenate([ktau[2 * dlag], ktau[2 * dlag + 1]], axis=-1)
        units.append(jnp.concatenate([top, bot], axis=-2))
    kstack = jnp.concatenate(units, axis=-2).astype(BF16)
    rev = T - 1 - jnp.arange(T)
    def local_rows(w):
        w = jnp.transpose(w[rev], (1, 0, 3, 2)).reshape(nq, SSM_GPB, T, SSM_GROUP, SSM_STATE)
        return group_diag(jnp.transpose(w, (0, 2, 1, 3, 4)).reshape(nq, T, LANES, SSM_STATE))

    m_all = jnp.concatenate([local_rows(w_re), local_rows(w_im)], axis=-1)
    m_all = m_all.reshape(nq, T * LANES, 2 * SSM_GPB * SSM_STATE)
    m_hi, m_lo = _split_bf16(m_all)
    ar, ai = pw_re[1:], pw_im[1:]
    n_re = (c_re[None] * ar[:, :, None, :] - c_im[None] * ai[:, :, None, :])
    n_im = -(c_re[None] * ai[:, :, None, :] + c_im[None] * ar[:, :, None, :])

    def state_rows(n):
        n = jnp.transpose(n, (1, 3, 0, 2)).reshape(nq, SSM_GPB * SSM_STATE, T, SSM_GROUP)
        n = jnp.tile(jnp.transpose(n, (0, 2, 1, 3)), (1, 1, 1, SSM_GPB))
        same = (jnp.arange(SSM_GPB * SSM_STATE) // SSM_STATE)[:, None] == (jnp.arange(LANES) // SSM_GROUP)[None, :]
        return jnp.where(same, n, 0.0)

    n_all = jnp.concatenate([state_rows(n_re), state_rows(n_im)], axis=2).astype(BF16)
    return dict(kstack=kstack, m_hi=m_hi, m_lo=m_lo, n_all=n_all,
                a16_re=pw_re[T].reshape(STATE_TILE), a16_im=pw_im[T].reshape(STATE_TILE),
                d=p['ssm_d'].astype(f32).reshape(1, SSM_WIDTH))


def _chunk_tokens(u_ref, rows):
    return [u_ref[pl.ds(t, rows, stride=SSM_CHUNK), :] for t in range(SSM_CHUNK)]


def _ssm_local_kernel(u_ref, mh_ref, ml_ref, hre_ref, him_ref):
    rows = hre_ref.shape[0]
    parts = [_split_bf16(ut) for ut in _chunk_tokens(u_ref, rows)]
    x_hi = jnp.concatenate([h for h, _ in parts], axis=1)
    x_lo = jnp.concatenate([l for _, l in parts], axis=1)
    h = _dot(x_hi, mh_ref[0]) + _dot(x_hi, ml_ref[0]) + _dot(x_lo, mh_ref[0])
    half = SSM_GPB * SSM_STATE
    hre_ref[...] = h[:, 0:half]
    him_ref[...] = h[:, half:2 * half]


def _ssm_local(u2d, mats, rows):
    n = u2d.shape[0]
    r = n // SSM_CHUNK
    nq = N_SSM_GROUPS // SSM_GPB
    half = SSM_GPB * SSM_STATE
    mspec = pl.BlockSpec((1, SSM_CHUNK * LANES, 2 * half), lambda q, i: (q, 0, 0))
    ospec = pl.BlockSpec((rows, half), lambda q, i: (i, q))
    return pl.pallas_call(
        _ssm_local_kernel,
        out_shape=(jax.ShapeDtypeStruct((r, N_SSM_GROUPS * SSM_STATE), F32),) * 2,
        grid=(nq, r // rows),
        in_specs=[pl.BlockSpec((rows * SSM_CHUNK, LANES), lambda q, i: (i, q)), mspec, mspec],
        out_specs=(ospec, ospec),
        compiler_params=_params(("parallel", "parallel")), name="ssm_local")(u2d, mats['m_hi'], mats['m_lo'])


def _ssm_scan_kernel(lre_ref, lim_ref, are_ref, aim_ref, h0re_ref, h0im_ref,
                     pre_ref, pim_ref, fre_ref, fim_ref):
    nchunk = lre_ref.shape[1]
    ar, ai = are_ref[...], aim_ref[...]

    def body(c, carry):
        hr, hi = carry
        pre_ref[0, c] = hr
        pim_ref[0, c] = hi
        return (ar * hr - ai * hi + lre_ref[0, c], ar * hi + ai * hr + lim_ref[0, c])

    hr, hi = lax.fori_loop(0, nchunk, body, (h0re_ref[0], h0im_ref[0]))
    fre_ref[0] = hr
    fim_ref[0] = hi


def _ssm_scan(hloc_re, hloc_im, mats, h0_re, h0_im):
    b, nchunk = hloc_re.shape[:2]
    big = pl.BlockSpec((1, nchunk) + STATE_TILE, lambda i: (i, 0, 0, 0))
    small = pl.BlockSpec((1,) + STATE_TILE, lambda i: (i, 0, 0))
    return pl.pallas_call(
        _ssm_scan_kernel,
        out_shape=(jax.ShapeDtypeStruct(hloc_re.shape, F32),) * 2 + (jax.ShapeDtypeStruct((b,) + STATE_TILE, F32),) * 2,
        grid=(b,),
        in_specs=[big, big, _full(STATE_TILE), _full(STATE_TILE), small, small],
        out_specs=(big, big, small, small),
        compiler_params=_params(("parallel",)), name="ssm_scan")(
            hloc_re, hloc_im, mats['a16_re'], mats['a16_im'], h0_re, h0_im)


def _gelu_tanh(y):
    return 0.5 * y * (1.0 + jnp.tanh(math.sqrt(2.0 / math.pi) * (y + 0.044715 * (y * y * y))))


def _ssm_out_kernel(u_ref, k_ref, pre_ref, pim_ref, n_ref, d_ref, y_ref, ysc_ref):
    rows = pre_ref.shape[0]
    us = _chunk_tokens(u_ref, rows)
    x = jnp.concatenate([ut.astype(BF16) for ut in us], axis=1)
    hp = jnp.concatenate([pre_ref[...], pim_ref[...]], axis=1).astype(BF16)
    unit = 2 * LANES
    nunit = SSM_CHUNK // 2
    for j in range(nunit):
        n_unit = jnp.concatenate([n_ref[0, 2 * j], n_ref[0, 2 * j + 1]], axis=1)
        yj = _dot(x[:, 0:unit * (j + 1)], k_ref[0, unit * (nunit - 1 - j):, :]) + _dot(hp, n_unit)
        for t2 in range(2):
            t = 2 * j + t2
            y = yj[:, t2 * LANES:(t2 + 1) * LANES] + d_ref[...] * us[t]
            ysc_ref[pl.ds(t, rows, stride=SSM_CHUNK), :] = _gelu_tanh(y)
    y_ref[...] = ysc_ref[...].astype(y_ref.dtype)


def _ssm_out(u2d, hprev_re, hprev_im, mats, rows):
    n = u2d.shape[0]
    r = n // SSM_CHUNK
    nq = N_SSM_GROUPS // SSM_GPB
    half = SSM_GPB * SSM_STATE
    uspec = pl.BlockSpec((rows * SSM_CHUNK, LANES), lambda q, i: (i, q))
    hspec = pl.BlockSpec((rows, half), lambda q, i: (i, q))
    return pl.pallas_call(
        _ssm_out_kernel,
        out_shape=jax.ShapeDtypeStruct((n, SSM_WIDTH), BF16),
        grid=(nq, r // rows),
        in_specs=[uspec, pl.BlockSpec((1, SSM_CHUNK * LANES, 2 * LANES), lambda q, i: (q, 0, 0)), hspec, hspec,
                  pl.BlockSpec((1, SSM_CHUNK, 2 * half, LANES), lambda q, i: (q, 0, 0, 0)),
                  pl.BlockSpec((1, LANES), lambda q, i: (0, q))],
        out_specs=uspec,
        scratch_shapes=[pltpu.VMEM((rows * SSM_CHUNK, LANES), F32)],
        compiler_params=_params(("parallel", "parallel")), name="ssm_out")(
            u2d, mats['kstack'], hprev_re, hprev_im, mats['n_all'], mats['d'])


def _ssm(u2d, b, h0_re, h0_im, mats):
    n = u2d.shape[0]
    nchunk = n // b // SSM_CHUNK
    r = b * nchunk
    rows = _pick_tile(r, TILE_SSM_ROWS)
    hloc_re, hloc_im = _ssm_local(u2d, mats, rows)
    shp = (b, nchunk) + STATE_TILE
    hprev_re, hprev_im, f_re, f_im = _ssm_scan(hloc_re.reshape(shp), hloc_im.reshape(shp), mats,
                                               h0_re.reshape((b,) + STATE_TILE), h0_im.reshape((b,) + STATE_TILE))
    y = _ssm_out(u2d, hprev_re.reshape(r, -1), hprev_im.reshape(r, -1), mats, rows)
    return y, f_re.reshape(b, N_SSM_GROUPS, SSM_STATE), f_im.reshape(b, N_SSM_GROUPS, SSM_STATE)


def _merge_kernel(x_ref, of_ref, ys_ref, qm_ref, gate_ref, mk_ref, mv_ref,
                  wglu_ref, wbf_ref, wbs_ref, wbm_ref, wo_ref, nf_ref, wr_ref,
                  x1_ref, h2_ref, r_ref):
    tm = x_ref.shape[0]
    om = []
    for hd in range(N_MEM_HEADS):
        sl = slice(hd * MEM_HEAD_DIM, (hd + 1) * MEM_HEAD_DIM)
        head_rows = pl.ds(hd, N_MEM, stride=N_MEM_HEADS)
        kh = mk_ref[0, head_rows, :].astype(BF16)
        vh = mv_ref[0, head_rows, :].astype(BF16)
        sc = _dot_nt(qm_ref[:, sl], kh)
        p = jnp.exp(sc - jnp.max(sc, axis=-1, keepdims=True))
        om.append(_dot(p.astype(BF16), vh) / jnp.sum(p, axis=-1, keepdims=True))
    o_mem = jnp.concatenate(om, axis=-1).astype(BF16)
    z = _dot(ys_ref[...], wglu_ref[...])
    y_ssm = (z[:, 0:SSM_WIDTH] * jax.nn.sigmoid(z[:, SSM_WIDTH:2 * SSM_WIDTH])).astype(BF16)
    g = lambda c: gate_ref[:, c * D_MODEL:(c + 1) * D_MODEL].astype(F32)
    merged = (g(0) * _dot(of_ref[...], wbf_ref[...]) + g(1) * _dot(y_ssm, wbs_ref[...])
              + g(2) * _dot(o_mem, wbm_ref[...]))
    x1 = x_ref[...] + _dot(merged.astype(BF16), wo_ref[...])
    x1_ref[...] = x1
    h2 = x1 * lax.rsqrt(jnp.mean(x1 * x1, axis=-1, keepdims=True) + RMS_EPS) * nf_ref[...]
    h2_ref[...] = h2.astype(BF16)
    h2_hi, h2_lo = _split_bf16(h2)
    hw = _dot(h2_hi, wr_ref[...])
    logits = hw[:, 0:LANES] + hw[:, LANES:2 * LANES] + _dot(h2_lo, wr_ref[:, 0:LANES])
    lane = lax.broadcasted_iota(jnp.int32, (tm, LANES), 1)
    big = jnp.int32(LANES)
    is_grp = (lane >= N_EXPERTS) & (lane < N_EXPERTS + N_EXPERT_GROUPS)
    gl = jnp.where(is_grp, logits, NEG_INF)
    gmax = jnp.max(gl, axis=-1, keepdims=True)
    grp = jnp.min(jnp.where(is_grp & (gl == gmax), lane, big), axis=-1, keepdims=True) - N_EXPERTS
    g_w = 1.0 / jnp.sum(jnp.where(is_grp, jnp.exp(gl - gmax), 0.0), axis=-1, keepdims=True)
    in_grp = (lane >= grp * EXPERTS_PER_GROUP) & (lane < (grp + 1) * EXPERTS_PER_GROUP)
    e1 = jnp.where(in_grp, logits, NEG_INF)
    m1 = jnp.max(e1, axis=-1, keepdims=True)
    i1 = jnp.min(jnp.where(in_grp & (e1 == m1), lane, big), axis=-1, keepdims=True)
    rest = in_grp & (lane != i1)
    e2 = jnp.where(rest, logits, NEG_INF)
    m2 = jnp.max(e2, axis=-1, keepdims=True)
    i2 = jnp.min(jnp.where(rest & (e2 == m2), lane, big), axis=-1, keepdims=True)
    ex = jnp.exp(m2 - m1)
    w1 = g_w / (1.0 + ex)
    w2 = g_w * ex / (1.0 + ex)
    r_ref[...] = jnp.where(lane == i1, w1, jnp.where(lane == i2, w2, jnp.where(lane == GROUP_LANE, grp.astype(F32), 0.0)))


def _merge(x2d, o_fox, ys, q_m, gates, mem_k, mem_v, w, tm, rows_per_batch):
    n = x2d.shape[0]
    assert n % tm == 0 and rows_per_batch % tm == 0
    per = rows_per_batch // tm
    row = lambda width: pl.BlockSpec((tm, width), lambda i: (i, 0))
    memspec = pl.BlockSpec((1, N_MEM * N_MEM_HEADS, MEM_HEAD_DIM), lambda i: (i // per, 0, 0))
    ws = [w['w_glu'], w['w_br_fox'], w['w_br_ssm'], w['w_br_mem'], w['w_out'], w['norm_ffn'], w['w_router']]
    return pl.pallas_call(
        _merge_kernel,
        out_shape=(jax.ShapeDtypeStruct((n, D_MODEL), F32), jax.ShapeDtypeStruct((n, D_MODEL), BF16),
                   jax.ShapeDtypeStruct((n, LANES), F32)),
        grid=(n // tm,),
        in_specs=[row(D_MODEL), row(FOX_WIDTH), row(SSM_WIDTH), row(MEM_WIDTH), row(3 * D_MODEL), memspec, memspec]
                 + [_full(a.shape) for a in ws],
        out_specs=(row(D_MODEL), row(D_MODEL), row(LANES)),
        compiler_params=_params(("parallel",)), name="merge")(
            x2d, o_fox, ys, q_m, gates, mem_k, mem_v, *ws)


def _moe_kernel(h_ref, r_ref, x1_ref, tri_ref, wg_ref, wu_ref, wd_ref, o_ref,
                xs_ref, cw_ref, og_ref, acc_ref, rank_ref, count_ref, *, main):
    step = pl.program_id(1)
    steps_per_group = EXPERTS_PER_GROUP // MOE_EPS
    g = step // steps_per_group
    tm = h_ref.shape[0]
    gf = g.astype(F32)
    bounds = [0, main] + list(range(-(-main // MOE_SUB) * MOE_SUB, tm, MOE_SUB)) + [tm]
    bounds = sorted(set(bounds))
    blocks = [(lo, hi - lo, lo > 0) for lo, hi in zip(bounds[:-1], bounds[1:])]

    def guarded(r0, fn):
        if r0 == 0:
            fn()
        else:
            pl.when(r0 < count_ref[0])(fn)

    @pl.when(step == 0)
    def _():
        acc_ref[...] = jnp.zeros(acc_ref.shape, F32)

    @pl.when(step % steps_per_group == 0)
    def _():
        rt = r_ref[...]
        rtt = rt.T
        mrow = rtt[GROUP_LANE:GROUP_LANE + 1, :] == gf
        m8 = jnp.broadcast_to(jnp.where(mrow, 1.0, 0.0), (8, tm))
        rank8 = _dot(m8.astype(BF16), tri_ref[...])
        rank_row = jnp.where(mrow, rank8[0:1, :], -1.0)
        rank_ref[...] = jnp.broadcast_to(jnp.where(mrow, rank8, -1.0).T[:, 0:1], rank_ref.shape)
        count_ref[0] = jnp.sum(jnp.where(mrow, 1, 0))
        hilo = jnp.concatenate(_split_bf16(rt), axis=1)
        for r0, nrows, _ in blocks:
            def compact(r0=r0, nrows=nrows):
                rows = slice(r0, r0 + nrows)
                slot = r0 + lax.broadcasted_iota(jnp.int32, (nrows, tm), 0)
                perm = jnp.where(rank_row == slot.astype(F32), 1.0, 0.0).astype(BF16)
                xs_ref[rows, :] = _dot(perm, h_ref[...]).astype(BF16)
                cw = _dot(perm, hilo)
                cw_ref[rows, :] = cw[:, 0:LANES] + cw[:, LANES:2 * LANES]
                og_ref[rows, :] = jnp.zeros((nrows, D_MODEL), F32)
            guarded(r0, compact)

    for k in range(MOE_EPS):
        e = step * MOE_EPS + k
        for r0, nrows, _ in blocks:
            def expert(r0=r0, nrows=nrows, k=k, e=e):
                rows = slice(r0, r0 + nrows)
                x = xs_ref[rows, :]
                a = _dot(x, wg_ref[k])
                up = _dot(x, wu_ref[k])
                lane = lax.broadcasted_iota(jnp.int32, (nrows, LANES), 1)
                ce = jnp.sum(jnp.where(lane == e, cw_ref[rows, :], 0.0), axis=-1, keepdims=True)
                act = a * jax.nn.sigmoid(a) * up * ce
                og_ref[rows, :] += _dot(act.astype(BF16), wd_ref[k])
            guarded(r0, expert)

    @pl.when(step % steps_per_group == steps_per_group - 1)
    def _():
        for r0, nrows, _ in blocks:
            def scatter_back(r0=r0, nrows=nrows):
                rows = slice(r0, r0 + nrows)
                slot = r0 + lax.broadcasted_iota(jnp.int32, (tm, nrows), 1)
                back = jnp.where(rank_ref[:, 0:1] == slot.astype(F32), 1.0, 0.0).astype(BF16)
                acc_ref[...] += _dot(back, og_ref[rows, :].astype(BF16))
            guarded(r0, scatter_back)

    @pl.when(step == pl.num_programs(1) - 1)
    def _():
        o_ref[...] = x1_ref[...] + acc_ref[...]


def _moe(h2, route, x1, wg, wu, wd, tm):
    n = h2.shape[0]
    assert n % tm == 0 and tm % MOE_SUB == 0
    main = max(MOE_SUB // 2, (5 * tm // 16) // 64 * 64)
    row = lambda width: pl.BlockSpec((tm, width), lambda i, s: (i, 0))
    r = jnp.arange(tm)
    tri = (r[:, None] < r[None, :]).astype(BF16)
    return pl.pallas_call(
        functools.partial(_moe_kernel, main=main),
        out_shape=jax.ShapeDtypeStruct((n, D_MODEL), F32),
        grid=(n // tm, N_EXPERTS // MOE_EPS),
        in_specs=[row(D_MODEL), row(LANES), row(D_MODEL), pl.BlockSpec((tm, tm), lambda i, s: (0, 0)),
                  pl.BlockSpec((MOE_EPS, D_MODEL, D_EXPERT), lambda i, s: (s, 0, 0)),
                  pl.BlockSpec((MOE_EPS, D_MODEL, D_EXPERT), lambda i, s: (s, 0, 0)),
                  pl.BlockSpec((MOE_EPS, D_EXPERT, D_MODEL), lambda i, s: (s, 0, 0))],
        out_specs=row(D_MODEL),
        scratch_shapes=[pltpu.VMEM((tm, D_MODEL), BF16), pltpu.VMEM((tm, LANES), F32), pltpu.VMEM((tm, D_MODEL), F32),
                        pltpu.VMEM((tm, D_MODEL), F32), pltpu.VMEM((tm, LANES), F32), pltpu.SMEM((1,), jnp.int32)],
        compiler_params=_params(("parallel", "arbitrary")), name="moe")(h2, route, x1, tri, wg, wu, wd)


def _prep_weights(p):
    w_in = p['w_in'].astype(BF16)
    o = 0
    wqkv = w_in[:, 0:3 * FOX_WIDTH]
    o = 3 * FOX_WIDTH
    wf = jnp.pad(w_in[:, o:o + N_FOX_HEADS], ((0, 0), (0, LANES - N_FOX_HEADS)))
    o += N_FOX_HEADS
    wqm = w_in[:, o:o + MEM_WIDTH]
    o += MEM_WIDTH
    wu = w_in[:, o:o + SSM_WIDTH]
    o += SSM_WIDTH
    wg = w_in[:, o:o + 3 * D_MODEL]
    r = jnp.arange(FOX_WIDTH) // FOX_HEAD_DIM
    bd = (r[:, None] == r[None, :]).astype(BF16)
    w_router = jnp.concatenate(
        [p['w_router_expert'], p['w_router_group'],
         jnp.zeros((D_MODEL, LANES - N_EXPERTS - N_EXPERT_GROUPS), F32)], axis=1)
    w_router = jnp.concatenate(_split_bf16(w_router), axis=1)
    return dict(
        norm_mix=p['norm_mix'].reshape(1, D_MODEL), wqkv=wqkv, wf=wf,
        bf=jnp.pad(p['b_forget'], (0, LANES - N_FOX_HEADS)).reshape(1, LANES),
        wqm=wqm, wu=wu, wg=wg,
        qn_fox=jnp.tile(p['qn_fox'], N_FOX_HEADS).reshape(1, FOX_WIDTH),
        kn_fox=jnp.tile(p['kn_fox'], N_FOX_HEADS).reshape(1, FOX_WIDTH),
        qn_mem=p['qn_mem'].reshape(1, MEM_HEAD_DIM), bd=bd,
        w_glu=p['w_glu'].astype(BF16), w_br_fox=p['w_br_fox'].astype(BF16),
        w_br_ssm=p['w_br_ssm'].astype(BF16), w_br_mem=p['w_br_mem'].astype(BF16),
        w_out=p['w_out'].astype(BF16), norm_ffn=p['norm_ffn'].reshape(1, D_MODEL), w_router=w_router,
        moe_wg=p['moe_w_gate'].astype(BF16), moe_wu=p['moe_w_up'].astype(BF16),
        moe_wd=p['moe_w_down'].astype(BF16))


def _pick_tile(n, target):
    t = min(n, target)
    while n % t:
        t //= 2
    return t


def _group(x, w, mats, mem_k, mem_v, h0_re, h0_im, cache):
    b, s, _ = x.shape
    n = b * s
    x2d = x.reshape(n, D_MODEL)
    prompt = cache is None
    q, kb, vb, k_out, v_out, lf_t, q_m, u, gates = _inproj(x2d, w, _pick_tile(s if prompt else n, TILE_INPROJ), s, prompt)
    q3 = q.reshape(b, s, FOX_WIDTH)
    k3 = kb.reshape(b, s, FOX_WIDTH)
    v3 = vb.reshape(b, s, FOX_WIDTH)
    lf_rows = lf_t.reshape(N_FOX_HEADS, b, s).transpose(1, 0, 2)
    lf3 = lf_rows.transpose(0, 2, 1)
    npair = N_FOX_HEADS // 2
    if prompt:
        c_row = LOG2E * _cumsum_rows(lf_rows.reshape(b * N_FOX_HEADS, s)).reshape(b, npair, 2, s)
        o_fox = _fox_prompt(q3, k3, v3, c_row, _pick_tile(s, TILE_FOX_Q), _pick_tile(s, TILE_FOX_K))
        unt = lambda a: a.reshape(b, N_FOX_HEADS, FOX_HEAD_DIM, s).transpose(0, 3, 1, 2)
        k4, v4 = unt(k_out), unt(v_out)
    else:
        cache_k, cache_v, cache_logf = cache
        past = cache_k.shape[1]
        lf_all = jnp.concatenate([cache_logf.astype(F32).transpose(0, 2, 1), lf_rows], axis=2)
        c_row = LOG2E * _cumsum_rows(lf_all.reshape(b * N_FOX_HEADS, past + s)).reshape(b, N_FOX_HEADS, past + s)
        o_fox = _fox_sample(q3, cache_k.transpose(0, 2, 3, 1), cache_v.transpose(0, 2, 3, 1), k3, v3,
                            c_row[:, :, past:].transpose(0, 2, 1), c_row[:, :, :past], c_row[:, :, past:],
                            _pick_tile(past, TILE_SAMPLE_K))
        k4 = k_out.reshape(b, s, N_FOX_HEADS, FOX_HEAD_DIM)
        v4 = v_out.reshape(b, s, N_FOX_HEADS, FOX_HEAD_DIM)
    ys, f_re, f_im = _ssm(u, b, h0_re, h0_im, mats)
    tm = _pick_tile(s, TILE_MERGE)
    x1, h2, route = _merge(x2d, o_fox.reshape(n, FOX_WIDTH), ys, q_m, gates, mem_k, mem_v, w, tm, s)
    y = _moe(h2, route, x1, w['moe_wg'], w['moe_wu'], w['moe_wd'], _pick_tile(n, TILE_MOE))
    return y.reshape(b, s, D_MODEL), k4, v4, lf3, f_re, f_im


def kernel(x_prompt, x_sample, mem_prompt, cache_fox_k, cache_fox_v, cache_fox_logf, state_ssm_re, state_ssm_im,
           cache_mem_k, cache_mem_v, norm_mix, w_in, b_forget, qn_fox, kn_fox, qn_mem, kn_mem, norm_mem, w_mem_kv,
           ssm_a_re, ssm_a_im, ssm_log_dt, ssm_b_re, ssm_b_im, ssm_c_re, ssm_c_im, ssm_d, w_glu, w_br_fox,
           w_br_ssm, w_br_mem, w_out, norm_ffn, w_router_group, w_router_expert, moe_w_gate, moe_w_up,
           moe_w_down):
    depth = norm_mix.shape[0]
    assert depth == 1
    l = 0
    p = dict(norm_mix=norm_mix[l], w_in=w_in[l], b_forget=b_forget[l], qn_fox=qn_fox[l], kn_fox=kn_fox[l],
             qn_mem=qn_mem[l], ssm_a_re=ssm_a_re[l], ssm_a_im=ssm_a_im[l], ssm_log_dt=ssm_log_dt[l],
             ssm_b_re=ssm_b_re[l], ssm_b_im=ssm_b_im[l], ssm_c_re=ssm_c_re[l], ssm_c_im=ssm_c_im[l],
             ssm_d=ssm_d[l], w_glu=w_glu[l], w_br_fox=w_br_fox[l], w_br_ssm=w_br_ssm[l], w_br_mem=w_br_mem[l],
             w_out=w_out[l], norm_ffn=norm_ffn[l], w_router_group=w_router_group[l],
             w_router_expert=w_router_expert[l], moe_w_gate=moe_w_gate[l], moe_w_up=moe_w_up[l],
             moe_w_down=moe_w_down[l])
    w = _prep_weights(p)
    mats = _ssm_mats(p)
    bp, sp, _ = x_prompt.shape
    bs, ss, _ = x_sample.shape

    mk, mv = _memkv(mem_prompt.reshape(bp * N_MEM, D_MODEL), norm_mem[l].reshape(1, D_MODEL),
                    w_mem_kv[l].astype(BF16), kn_mem[l].reshape(1, MEM_HEAD_DIM), _pick_tile(bp * N_MEM, TILE_MEMKV))
    mem_rows = lambda a, b: a.reshape(b, N_MEM * N_MEM_HEADS, MEM_HEAD_DIM)
    mk = mem_rows(mk, bp)
    mv = mem_rows(mv, bp)
    zeros = jnp.zeros((bp, N_SSM_GROUPS, SSM_STATE), F32)
    yp, pk, pv, plf, pre, pim = _group(x_prompt, w, mats, mk, mv, zeros, zeros, None)
    cache = (cache_fox_k[l], cache_fox_v[l], cache_fox_logf[l])
    ys, sk, sv, slf, sre, sim = _group(
        x_sample, w, mats, mem_rows(cache_mem_k[l], bs), mem_rows(cache_mem_v[l], bs),
        state_ssm_re[l].astype(F32), state_ssm_im[l].astype(F32), cache)
    st = lambda a: a[None]
    return (yp, ys, st(pk), st(pv), st(plf), st(pre), st(pim),
            st(mk.reshape(bp, N_MEM, N_MEM_HEADS, MEM_HEAD_DIM)), st(mv.reshape(bp, N_MEM, N_MEM_HEADS, MEM_HEAD_DIM)),
            st(sk), st(sv), st(slf), st(sre), st(sim))
```

```python
import functools
import math

import jax
import jax.numpy as jnp
from jax import lax
from jax.experimental import pallas as pl
from jax.experimental.pallas import tpu as pltpu

F32 = jnp.float32
BF16 = jnp.bfloat16

D_MODEL = 1024
N_FOX_HEADS = 8
FOX_HEAD_DIM = 64
FOX_WIDTH = N_FOX_HEADS * FOX_HEAD_DIM
N_MEM = 256
N_MEM_HEADS = 4
MEM_HEAD_DIM = 128
MEM_WIDTH = N_MEM_HEADS * MEM_HEAD_DIM
SSM_GROUP = 16
SSM_WIDTH = 512
N_SSM_GROUPS = SSM_WIDTH // SSM_GROUP
SSM_STATE = 64
N_EXPERT_GROUPS = 4
EXPERTS_PER_GROUP = 8
N_EXPERTS = N_EXPERT_GROUPS * EXPERTS_PER_GROUP
D_EXPERT = 256
RMS_EPS = 1e-6
NEG_INF = -1e30
LOG2E = 1.4426950408889634

LANES = 128
SSM_CHUNK = 16
SSM_GPB = LANES // SSM_GROUP
GROUP_LANE = N_EXPERTS
MOE_SUB = 128
MOE_EPS = 4
VMEM_LIMIT = 56 * 1024 * 1024
STATE_TILE = (8, N_SSM_GROUPS * SSM_STATE // 8)

TILE_INPROJ = 512
TILE_MEMKV = 512
TILE_FOX_Q = 512
TILE_FOX_K = 512
TILE_SAMPLE_K = 4096
TILE_SSM_ROWS = 256
TILE_MERGE = 512
TILE_MOE = 1024


def _dot(a, b):
    return jnp.dot(a, b, preferred_element_type=F32)


def _dot_nt(a, b):
    return lax.dot_general(a, b, (((1,), (1,)), ((), ())), preferred_element_type=F32)


def _dot_exact(a, b):
    return jnp.dot(a, b, preferred_element_type=F32, precision=lax.Precision.HIGHEST)


def _split_bf16(x):
    hi = x.astype(BF16)
    lo = (x - hi.astype(F32)).astype(BF16)
    return hi, lo


def _params(sem):
    return pltpu.CompilerParams(dimension_semantics=sem, vmem_limit_bytes=VMEM_LIMIT)


def _full(shape):
    n = len(shape)
    return pl.BlockSpec(shape, lambda *_: (0,) * n)


def _inproj_kernel(x_ref, g_ref, wqkv_ref, wf_ref, bf_ref, wqm_ref, wu_ref, wg_ref,
                   qn_ref, kn_ref, qmn_ref, bd_ref,
                   q_ref, kb_ref, vb_ref, k_ref, v_ref, lf_ref, qm_ref, u_ref, gate_ref, *, kv_transposed):
    x = x_ref[...]
    h = x * lax.rsqrt(jnp.mean(x * x, axis=-1, keepdims=True) + RMS_EPS) * g_ref[...]
    hb = h.astype(BF16)

    def head_norm(z, gain):
        ss = _dot((z * z).astype(BF16), bd_ref[...])
        return z * lax.rsqrt(ss * (1.0 / FOX_HEAD_DIM) + RMS_EPS) * gain

    zq = _dot(hb, wqkv_ref[:, 0:FOX_WIDTH])
    q_ref[...] = (head_norm(zq, qn_ref[...]) * (LOG2E * FOX_HEAD_DIM ** -0.5)).astype(BF16)
    zk = _dot(hb, wqkv_ref[:, FOX_WIDTH:2 * FOX_WIDTH])
    kn = head_norm(zk, kn_ref[...])
    zv = _dot(hb, wqkv_ref[:, 2 * FOX_WIDTH:3 * FOX_WIDTH])
    kb_ref[...] = kn.astype(BF16)
    vb_ref[...] = zv.astype(BF16)
    tm = x_ref.shape[0]
    if kv_transposed:
        k_ref[0] = kn.T
        v_ref[0] = zv.T
    else:
        for hd in range(N_FOX_HEADS):
            hs = slice(hd * FOX_HEAD_DIM, (hd + 1) * FOX_HEAD_DIM)
            rows = pl.ds(hd, tm, stride=N_FOX_HEADS)
            k_ref[rows, :] = kn[:, hs]
            v_ref[rows, :] = zv[:, hs]

    zf = (_dot(hb, wf_ref[...]) + bf_ref[...]).T[0:N_FOX_HEADS, :]
    lf_ref[...] = jnp.minimum(zf, 0.0) - jnp.log1p(jnp.exp(-jnp.abs(zf)))

    zm = _dot(hb, wqm_ref[...])
    for hd in range(N_MEM_HEADS):
        sl = slice(hd * MEM_HEAD_DIM, (hd + 1) * MEM_HEAD_DIM)
        zh = zm[:, sl]
        ms = jnp.mean(zh * zh, axis=-1, keepdims=True)
        qm_ref[:, sl] = (zh * lax.rsqrt(ms + RMS_EPS) * qmn_ref[...] * (MEM_HEAD_DIM ** -0.5)).astype(BF16)

    u_ref[...] = _dot(hb, wu_ref[...])
    for c in range(3):
        sl = slice(c * D_MODEL, (c + 1) * D_MODEL)
        gate_ref[:, sl] = (0.5 * jnp.tanh(0.5 * _dot(hb, wg_ref[:, sl])) + 0.5).astype(BF16)


def _inproj(x2d, w, tm, seq, kv_transposed):
    n = x2d.shape[0]
    assert n % tm == 0
    row = lambda width: pl.BlockSpec((tm, width), lambda i: (i, 0))
    if kv_transposed:
        assert seq % tm == 0
        per = seq // tm
        kv_shape = jax.ShapeDtypeStruct((n // seq, FOX_WIDTH, seq), F32)
        heads = pl.BlockSpec((1, FOX_WIDTH, tm), lambda i: (i // per, 0, i % per))
    else:
        kv_shape = jax.ShapeDtypeStruct((n * N_FOX_HEADS, FOX_HEAD_DIM), F32)
        heads = pl.BlockSpec((tm * N_FOX_HEADS, FOX_HEAD_DIM), lambda i: (i, 0))
    ins = [x2d, w['norm_mix'], w['wqkv'], w['wf'], w['bf'], w['wqm'], w['wu'], w['wg'],
           w['qn_fox'], w['kn_fox'], w['qn_mem'], w['bd']]
    in_specs = [row(D_MODEL)] + [_full(a.shape) for a in ins[1:]]
    out_shape = (
        jax.ShapeDtypeStruct((n, FOX_WIDTH), BF16),
        jax.ShapeDtypeStruct((n, FOX_WIDTH), BF16),
        jax.ShapeDtypeStruct((n, FOX_WIDTH), BF16),
        kv_shape,
        kv_shape,
        jax.ShapeDtypeStruct((N_FOX_HEADS, n), F32),
        jax.ShapeDtypeStruct((n, MEM_WIDTH), BF16),
        jax.ShapeDtypeStruct((n, SSM_WIDTH), F32),
        jax.ShapeDtypeStruct((n, 3 * D_MODEL), BF16),
    )
    out_specs = (row(FOX_WIDTH), row(FOX_WIDTH), row(FOX_WIDTH), heads, heads,
                 pl.BlockSpec((N_FOX_HEADS, tm), lambda i: (0, i)),
                 row(MEM_WIDTH), row(SSM_WIDTH), row(3 * D_MODEL))
    return pl.pallas_call(
        functools.partial(_inproj_kernel, kv_transposed=kv_transposed),
        out_shape=out_shape, grid=(n // tm,), in_specs=in_specs, out_specs=out_specs,
        compiler_params=_params(("parallel",)), name="inproj")(*ins)


def _memkv_kernel(x_ref, g_ref, w_ref, kn_ref, k_ref, v_ref):
    x = x_ref[...]
    h = x * lax.rsqrt(jnp.mean(x * x, axis=-1, keepdims=True) + RMS_EPS) * g_ref[...]
    hb = h.astype(BF16)
    tm = x_ref.shape[0]
    zk = _dot(hb, w_ref[:, 0:MEM_WIDTH])
    zv = _dot(hb, w_ref[:, MEM_WIDTH:2 * MEM_WIDTH])
    for hd in range(N_MEM_HEADS):
        sl = slice(hd * MEM_HEAD_DIM, (hd + 1) * MEM_HEAD_DIM)
        rows = pl.ds(hd, tm, stride=N_MEM_HEADS)
        zh = zk[:, sl]
        ms = jnp.mean(zh * zh, axis=-1, keepdims=True)
        k_ref[rows, :] = zh * lax.rsqrt(ms + RMS_EPS) * kn_ref[...]
        v_ref[rows, :] = zv[:, sl]


def _memkv(mem2d, norm_mem, w_kv, kn_mem, tm):
    n = mem2d.shape[0]
    out = jax.ShapeDtypeStruct((n * N_MEM_HEADS, MEM_HEAD_DIM), F32)
    ospec = pl.BlockSpec((tm * N_MEM_HEADS, MEM_HEAD_DIM), lambda i: (i, 0))
    return pl.pallas_call(
        _memkv_kernel,
        out_shape=(out, out),
        grid=(n // tm,),
        in_specs=[pl.BlockSpec((tm, D_MODEL), lambda i: (i, 0)), _full(norm_mem.shape), _full(w_kv.shape),
                  _full(kn_mem.shape)],
        out_specs=(ospec, ospec),
        compiler_params=_params(("parallel",)), name="memkv")(mem2d, norm_mem, w_kv, kn_mem)


CUMSUM_BLOCK = 256


def _cumsum_kernel(x_ref, o_ref):
    nblk = x_ref.shape[1] // CUMSUM_BLOCK
    r = lax.broadcasted_iota(jnp.int32, (CUMSUM_BLOCK, CUMSUM_BLOCK), 0)
    c = lax.broadcasted_iota(jnp.int32, (CUMSUM_BLOCK, CUMSUM_BLOCK), 1)
    tri = (r <= c).astype(F32)
    carry = jnp.zeros((x_ref.shape[0], 1), F32)
    for j in range(nblk):
        sl = slice(j * CUMSUM_BLOCK, (j + 1) * CUMSUM_BLOCK)
        cs = _dot_exact(x_ref[:, sl], tri) + carry
        o_ref[:, sl] = cs
        carry = cs[:, CUMSUM_BLOCK - 1:CUMSUM_BLOCK]


def _cumsum_rows(x):
    rows, n = x.shape
    npad = -(-n // CUMSUM_BLOCK) * CUMSUM_BLOCK
    xp = jnp.pad(x, ((0, 0), (0, npad - n))) if npad != n else x
    out = pl.pallas_call(
        _cumsum_kernel, out_shape=jax.ShapeDtypeStruct((rows, npad), F32), grid=(1,),
        in_specs=[_full((rows, npad))], out_specs=_full((rows, npad)),
        compiler_params=_params(("arbitrary",)), name="cumsum")(xp)
    return out[:, :n] if npad != n else out


def _reduce_rows(x, op):
    rows, cols = x.shape
    if rows > 64 and rows % 64 == 0:
        x = op(x.reshape(rows // 64, 64, cols), axis=0)
        rows = 64
    if rows == 64:
        x = op(x.reshape(8, 8, cols), axis=0)
    return op(x, axis=0, keepdims=True)


def _fox_prompt_kernel(q_ref, k_ref, v_ref, cr_ref, o_ref,
                       vt_ref, ck0_ref, ck1_ref, st0_ref, st1_ref, pt0_ref, pt1_ref, acc_ref, *, tq, tk):
    st_refs = (st0_ref, st1_ref)
    pt_refs = (pt0_ref, pt1_ref)
    i = pl.program_id(2)
    s_len = k_ref.shape[1]

    @pl.when(i == 0)
    def _():
        vt_ref[...] = v_ref[0].astype(F32).T.astype(BF16)
        ck0_ref[...] = jnp.broadcast_to(cr_ref[0, 0, 0:1, :], (LANES, s_len)).T
        ck1_ref[...] = jnp.broadcast_to(cr_ref[0, 0, 1:2, :], (LANES, s_len)).T

    def q_heads_t(blk):
        qt = q_ref[0, pl.ds(pl.multiple_of(blk * tq, tq), tq), :].astype(F32).T
        row = lax.broadcasted_iota(jnp.int32, (LANES, tq), 0)
        return (jnp.where(row < FOX_HEAD_DIM, qt, 0.0).astype(BF16),
                jnp.where(row < FOX_HEAD_DIM, 0.0, qt).astype(BF16))

    qts = q_heads_t(i)
    q0 = pl.multiple_of(i * tq, tq)
    cq = cr_ref[0, 0, :, pl.ds(q0, tq)]
    ck_refs = (ck0_ref, ck1_ref)

    def stage_a(n, par, qts=qts):
        s = pl.multiple_of(n * tk, tk)
        kb = k_ref[0, pl.ds(s, tk), :]
        for hh in range(2):
            ck = ck_refs[hh][pl.ds(s, tk), :]
            st_refs[par][hh] = _dot(kb, qts[hh]) - jnp.concatenate([ck] * (tq // LANES), axis=1)

    def stage_b(n, par, stats, masked):
        if masked:
            kpos = n * tk + lax.broadcasted_iota(jnp.int32, (tk, tq), 0)
            qpos = q0 + lax.broadcasted_iota(jnp.int32, (tk, tq), 1)
            mask = kpos <= qpos
        out = []
        for hh in range(2):
            m, l = stats[2 * hh:2 * hh + 2]
            t = st_refs[par][hh]
            if masked:
                t = jnp.where(mask, t, NEG_INF)
            cqh = cq[hh:hh + 1, :]
            m_new = jnp.maximum(m, _reduce_rows(t, jnp.max) + cqh)
            alpha = jnp.exp2(m - m_new)
            p = jnp.exp2(t + (cqh - m_new))
            pt_refs[par][hh] = p.astype(BF16)
            out.extend([m_new, alpha * l + _reduce_rows(p, jnp.sum), alpha])
        return tuple(out)

    def stage_c(n, par, alphas):
        s = pl.multiple_of(jnp.maximum(n, 0) * tk, tk)
        for hh in range(2):
            vt = vt_ref[hh * FOX_HEAD_DIM:(hh + 1) * FOX_HEAD_DIM, pl.ds(s, tk)]
            acc_ref[hh] = alphas[hh] * acc_ref[hh] + _dot(vt, pt_refs[par][hh])

    def iteration(n, par, carry):
        m0, l0, al0, m1, l1, al1 = carry
        stage_c(n - 1, 1 - par, (al0, al1))
        new = stage_b(n, par, (m0, l0, m1, l1), False)
        stage_a(n + 1, 1 - par)
        return new

    def finish(par, carry):
        m0, l0, al0, m1, l1, al1 = carry
        stage_c(nfull - 1, 1 - par, (al0, al1))
        _, l0, be0, _, l1, be1 = stage_b(nfull, par, (m0, l0, m1, l1), True)
        stage_a(0, 0, q_heads_t(jnp.minimum(i + 1, pl.num_programs(2) - 1)))
        stage_c(nfull, par, (be0, be1))
        ot = jnp.concatenate([acc_ref[0] / l0, acc_ref[1] / l1], axis=0)
        o_ref[0] = ot.T.astype(o_ref.dtype)

    acc_ref[...] = jnp.zeros(acc_ref.shape, F32)
    pt1_ref[...] = jnp.zeros(pt1_ref.shape, BF16)
    neg = jnp.full((1, tq), NEG_INF, F32)
    zero = jnp.zeros((1, tq), F32)
    one = jnp.ones((1, tq), F32)
    nfull = (i * tq) // tk
    pl.when(i == 0)(lambda: stage_a(0, 0))
    carry = lax.fori_loop(0, nfull // 2, lambda k, c: iteration(2 * k + 1, 1, iteration(2 * k, 0, c)),
                          (neg, zero, one, neg, zero, one))
    odd = nfull % 2 == 1
    carry = lax.cond(odd, lambda c: iteration(nfull - 1, 0, c), lambda c: c, carry)
    pl.when(odd)(lambda: finish(1, carry))
    pl.when(jnp.logical_not(odd))(lambda: finish(0, carry))


def _fox_prompt(q, k, v, c_row, tq, tk):
    b, s, _ = q.shape
    assert s % tk == 0 and tk % tq == 0
    npair = N_FOX_HEADS // 2
    return pl.pallas_call(
        functools.partial(_fox_prompt_kernel, tq=tq, tk=tk),
        out_shape=jax.ShapeDtypeStruct((b, s, FOX_WIDTH), BF16),
        grid=(b, npair, s // tq),
        in_specs=[
            pl.BlockSpec((1, s, LANES), lambda bi, hp, i: (bi, 0, hp)),
            pl.BlockSpec((1, s, LANES), lambda bi, hp, i: (bi, 0, hp)),
            pl.BlockSpec((1, s, LANES), lambda bi, hp, i: (bi, 0, hp)),
            pl.BlockSpec((1, 1, 2, s), lambda bi, hp, i: (bi, hp, 0, 0)),
        ],
        out_specs=pl.BlockSpec((1, tq, LANES), lambda bi, hp, i: (bi, i, hp)),
        scratch_shapes=[pltpu.VMEM((LANES, s), BF16),
                        pltpu.VMEM((s, LANES), F32), pltpu.VMEM((s, LANES), F32),
                        pltpu.VMEM((2, tk, tq), F32), pltpu.VMEM((2, tk, tq), F32),
                        pltpu.VMEM((2, tk, tq), BF16), pltpu.VMEM((2, tk, tq), BF16),
                        pltpu.VMEM((2, FOX_HEAD_DIM, tq), F32)],
        compiler_params=_params(("parallel", "parallel", "arbitrary")), name="fox_prompt")(q, k, v, c_row)


def _fox_sample_kernel(q_ref, ck_ref, cv_ref, nk_ref, nv_ref, cq_ref, crc_ref, crn_ref, o_ref, *state, n):
    j = pl.program_id(1)
    nj = pl.num_programs(1)
    m_refs = state[0:N_FOX_HEADS]
    l_refs = state[N_FOX_HEADS:2 * N_FOX_HEADS]
    acc_refs = state[2 * N_FOX_HEADS:3 * N_FOX_HEADS]

    @pl.when(j == 0)
    def _():
        for hd in range(N_FOX_HEADS):
            m_refs[hd][...] = jnp.full(m_refs[hd].shape, NEG_INF, F32)
            l_refs[hd][...] = jnp.zeros(l_refs[hd].shape, F32)
            acc_refs[hd][...] = jnp.zeros(acc_refs[hd].shape, F32)

    def update(k_of, v_of, cr_ref_, mask, transposed):
        qk = _dot if transposed else _dot_nt
        pv = _dot_nt if transposed else _dot
        ts = []
        for hd in range(N_FOX_HEADS):
            hs = slice(hd * FOX_HEAD_DIM, (hd + 1) * FOX_HEAD_DIM)
            t = qk(q_ref[0, :, hs], k_of(hd)) - cr_ref_[0, hd:hd + 1, :]
            ts.append(t if mask is None else jnp.where(mask, t, NEG_INF))
        ps = []
        for hd in range(N_FOX_HEADS):
            cq = cq_ref[0, :, hd:hd + 1]
            m = m_refs[hd][...]
            m_new = jnp.maximum(m, jnp.max(ts[hd], axis=-1, keepdims=True) + cq)
            alpha = jnp.exp2(m - m_new)
            p = jnp.exp2(ts[hd] + (cq - m_new))
            m_refs[hd][...] = m_new
            l_refs[hd][...] = alpha * l_refs[hd][...] + jnp.sum(p, axis=-1, keepdims=True)
            ps.append((alpha, p.astype(BF16)))
        for hd in range(N_FOX_HEADS):
            alpha, p = ps[hd]
            acc_refs[hd][...] = alpha * acc_refs[hd][...] + pv(p, v_of(hd))

    update(lambda hd: ck_ref[0, hd].astype(BF16), lambda hd: cv_ref[0, hd].astype(BF16), crc_ref, None, True)

    @pl.when(j == nj - 1)
    def _():
        r = lax.broadcasted_iota(jnp.int32, (n, n), 0)
        c = lax.broadcasted_iota(jnp.int32, (n, n), 1)
        head = lambda ref: (lambda hd: ref[0, :, hd * FOX_HEAD_DIM:(hd + 1) * FOX_HEAD_DIM])
        update(head(nk_ref), head(nv_ref), crn_ref, c <= r, False)
        for hd in range(N_FOX_HEADS):
            hs = slice(hd * FOX_HEAD_DIM, (hd + 1) * FOX_HEAD_DIM)
            o_ref[0, :, hs] = (acc_refs[hd][...] / l_refs[hd][...]).astype(o_ref.dtype)


def _fox_sample(q, cache_k, cache_v, k_new, v_new, c_q, c_row_cache, c_row_new, tk):
    b, n, _ = q.shape
    past = cache_k.shape[3]
    assert past % tk == 0
    cache_spec = pl.BlockSpec((1, N_FOX_HEADS, FOX_HEAD_DIM, tk), lambda bi, j: (bi, 0, 0, j))
    return pl.pallas_call(
        functools.partial(_fox_sample_kernel, n=n),
        out_shape=jax.ShapeDtypeStruct((b, n, FOX_WIDTH), BF16),
        grid=(b, past // tk),
        in_specs=[
            pl.BlockSpec((1, n, FOX_WIDTH), lambda bi, j: (bi, 0, 0)),
            cache_spec,
            cache_spec,
            pl.BlockSpec((1, n, FOX_WIDTH), lambda bi, j: (bi, 0, 0)),
            pl.BlockSpec((1, n, FOX_WIDTH), lambda bi, j: (bi, 0, 0)),
            pl.BlockSpec((1, n, N_FOX_HEADS), lambda bi, j: (bi, 0, 0)),
            pl.BlockSpec((1, N_FOX_HEADS, tk), lambda bi, j: (bi, 0, j)),
            pl.BlockSpec((1, N_FOX_HEADS, n), lambda bi, j: (bi, 0, 0)),
        ],
        out_specs=pl.BlockSpec((1, n, FOX_WIDTH), lambda bi, j: (bi, 0, 0)),
        scratch_shapes=([pltpu.VMEM((n, 1), F32)] * (2 * N_FOX_HEADS)
                        + [pltpu.VMEM((n, FOX_HEAD_DIM), F32)] * N_FOX_HEADS),
        compiler_params=_params(("parallel", "arbitrary")), name="fox_sample")(
            q, cache_k, cache_v, k_new, v_new, c_q, c_row_cache, c_row_new)


def _ssm_mats(p):
    f32 = F32
    a_re, a_im = p['ssm_a_re'].astype(f32), p['ssm_a_im'].astype(f32)
    b_re, b_im = p['ssm_b_re'].astype(f32), p['ssm_b_im'].astype(f32)
    c_re, c_im = p['ssm_c_re'].astype(f32), p['ssm_c_im'].astype(f32)
    dt = jnp.exp(p['ssm_log_dt'].astype(f32))[:, None]
    mag = jnp.exp(dt * a_re)
    ab_re = mag * jnp.cos(dt * a_im)
    ab_im = mag * jnp.sin(dt * a_im)
    den = a_re * a_re + a_im * a_im
    nr, ni = ab_re - 1.0, ab_im
    coef_re = (nr * a_re + ni * a_im) / den
    coef_im = (ni * a_re - nr * a_im) / den
    bb_re = coef_re[..., None] * b_re - coef_im[..., None] * b_im
    bb_im = coef_re[..., None] * b_im + coef_im[..., None] * b_re
    pr, pi = [jnp.ones_like(ab_re)], [jnp.zeros_like(ab_im)]
    for _ in range(SSM_CHUNK):
        pr.append(pr[-1] * ab_re - pi[-1] * ab_im)
        pi.append(pr[-2] * ab_im + pi[-1] * ab_re)
    pw_re, pw_im = jnp.stack(pr), jnp.stack(pi)
    T = SSM_CHUNK
    w_re = pw_re[..., None] * bb_re[None] - pw_im[..., None] * bb_im[None]
    w_im = pw_re[..., None] * bb_im[None] + pw_im[..., None] * bb_re[None]
    kk = (jnp.einsum('gop,kgpi->kgoi', c_re, w_re[:T], precision='highest')
          - jnp.einsum('gop,kgpi->kgoi', c_im, w_im[:T], precision='highest'))
    nq = N_SSM_GROUPS // SSM_GPB

    def group_diag(m):
        rows, c = m.shape[-2:]
        m = jnp.tile(m, (1,) * (m.ndim - 1) + (SSM_GPB,))
        same = (jnp.arange(rows) // (rows // SSM_GPB))[:, None] == (jnp.arange(SSM_GPB * c) // c)[None, :]
        return jnp.where(same, m, 0.0)

    def lane_diag(m):
        lead = m.shape[:-3]
        i, c = m.shape[-2:]
        return group_diag(m.reshape(lead + (nq, SSM_GPB * i, c)))

    ktau = lane_diag(jnp.swapaxes(kk, -1, -2))
    ktau = jnp.concatenate([jnp.zeros_like(ktau[:1]), ktau], axis=0)
    units = []
    for dlag in range(T // 2 - 1, -1, -1):
        top = jnp.concatenate([ktau[2 * dlag + 1], ktau[2 * dlag + 2]], axis=-1)
        bot = jnp.concatenate([ktau[2 * dlag], ktau[2 * dlag + 1]], axis=-1)
        units.append(jnp.concatenate([top, bot], axis=-2))
    kstack = jnp.concatenate(units, axis=-2).astype(BF16)
    rev = T - 1 - jnp.arange(T)
    def local_rows(w):
        w = jnp.transpose(w[rev], (1, 0, 3, 2)).reshape(nq, SSM_GPB, T, SSM_GROUP, SSM_STATE)
        return group_diag(jnp.transpose(w, (0, 2, 1, 3, 4)).reshape(nq, T, LANES, SSM_STATE))

    m_all = jnp.concatenate([local_rows(w_re), local_rows(w_im)], axis=-1)
    m_all = m_all.reshape(nq, T * LANES, 2 * SSM_GPB * SSM_STATE)
    m_hi, m_lo = _split_bf16(m_all)
    ar, ai = pw_re[1:], pw_im[1:]
    n_re = (c_re[None] * ar[:, :, None, :] - c_im[None] * ai[:, :, None, :])
    n_im = -(c_re[None] * ai[:, :, None, :] + c_im[None] * ar[:, :, None, :])

    def state_rows(n):
        n = jnp.transpose(n, (1, 3, 0, 2)).reshape(nq, SSM_GPB * SSM_STATE, T, SSM_GROUP)
        n = jnp.tile(jnp.transpose(n, (0, 2, 1, 3)), (1, 1, 1, SSM_GPB))
        same = (jnp.arange(SSM_GPB * SSM_STATE) // SSM_STATE)[:, None] == (jnp.arange(LANES) // SSM_GROUP)[None, :]
        return jnp.where(same, n, 0.0)

    n_all = jnp.concatenate([state_rows(n_re), state_rows(n_im)], axis=2).astype(BF16)
    return dict(kstack=kstack, m_hi=m_hi, m_lo=m_lo, n_all=n_all,
                a16_re=pw_re[T].reshape(STATE_TILE), a16_im=pw_im[T].reshape(STATE_TILE),
                d=p['ssm_d'].astype(f32).reshape(1, SSM_WIDTH))


def _chunk_tokens(u_ref, rows):
    return [u_ref[pl.ds(t, rows, stride=SSM_CHUNK), :] for t in range(SSM_CHUNK)]


def _ssm_local_kernel(u_ref, mh_ref, ml_ref, hre_ref, him_ref):
    rows = hre_ref.shape[0]
    parts = [_split_bf16(ut) for ut in _chunk_tokens(u_ref, rows)]
    x_hi = jnp.concatenate([h for h, _ in parts], axis=1)
    x_lo = jnp.concatenate([l for _, l in parts], axis=1)
    h = _dot(x_hi, mh_ref[0]) + _dot(x_hi, ml_ref[0]) + _dot(x_lo, mh_ref[0])
    half = SSM_GPB * SSM_STATE
    hre_ref[...] = h[:, 0:half]
    him_ref[...] = h[:, half:2 * half]


def _ssm_local(u2d, mats, rows):
    n = u2d.shape[0]
    r = n // SSM_CHUNK
    nq = N_SSM_GROUPS // SSM_GPB
    half = SSM_GPB * SSM_STATE
    mspec = pl.BlockSpec((1, SSM_CHUNK * LANES, 2 * half), lambda q, i: (q, 0, 0))
    ospec = pl.BlockSpec((rows, half), lambda q, i: (i, q))
    return pl.pallas_call(
        _ssm_local_kernel,
        out_shape=(jax.ShapeDtypeStruct((r, N_SSM_GROUPS * SSM_STATE), F32),) * 2,
        grid=(nq, r // rows),
        in_specs=[pl.BlockSpec((rows * SSM_CHUNK, LANES), lambda q, i: (i, q)), mspec, mspec],
        out_specs=(ospec, ospec),
        compiler_params=_params(("parallel", "parallel")), name="ssm_local")(u2d, mats['m_hi'], mats['m_lo'])


def _ssm_scan_kernel(lre_ref, lim_ref, are_ref, aim_ref, h0re_ref, h0im_ref,
                     pre_ref, pim_ref, fre_ref, fim_ref):
    nchunk = lre_ref.shape[1]
    ar, ai = are_ref[...], aim_ref[...]

    def body(c, carry):
        hr, hi = carry
        pre_ref[0, c] = hr
        pim_ref[0, c] = hi
        return (ar * hr - ai * hi + lre_ref[0, c], ar * hi + ai * hr + lim_ref[0, c])

    hr, hi = lax.fori_loop(0, nchunk, body, (h0re_ref[0], h0im_ref[0]))
    fre_ref[0] = hr
    fim_ref[0] = hi


def _ssm_scan(hloc_re, hloc_im, mats, h0_re, h0_im):
    b, nchunk = hloc_re.shape[:2]
    big = pl.BlockSpec((1, nchunk) + STATE_TILE, lambda i: (i, 0, 0, 0))
    small = pl.BlockSpec((1,) + STATE_TILE, lambda i: (i, 0, 0))
    return pl.pallas_call(
        _ssm_scan_kernel,
        out_shape=(jax.ShapeDtypeStruct(hloc_re.shape, F32),) * 2 + (jax.ShapeDtypeStruct((b,) + STATE_TILE, F32),) * 2,
        grid=(b,),
        in_specs=[big, big, _full(STATE_TILE), _full(STATE_TILE), small, small],
        out_specs=(big, big, small, small),
        compiler_params=_params(("parallel",)), name="ssm_scan")(
            hloc_re, hloc_im, mats['a16_re'], mats['a16_im'], h0_re, h0_im)


def _gelu_tanh(y):
    return 0.5 * y * (1.0 + jnp.tanh(math.sqrt(2.0 / math.pi) * (y + 0.044715 * (y * y * y))))


def _ssm_out_kernel(u_ref, k_ref, pre_ref, pim_ref, n_ref, d_ref, y_ref, ysc_ref):
    rows = pre_ref.shape[0]
    us = _chunk_tokens(u_ref, rows)
    x = jnp.concatenate([ut.astype(BF16) for ut in us], axis=1)
    hp = jnp.concatenate([pre_ref[...], pim_ref[...]], axis=1).astype(BF16)
    unit = 2 * LANES
    nunit = SSM_CHUNK // 2
    for j in range(nunit):
        n_unit = jnp.concatenate([n_ref[0, 2 * j], n_ref[0, 2 * j + 1]], axis=1)
        yj = _dot(x[:, 0:unit * (j + 1)], k_ref[0, unit * (nunit - 1 - j):, :]) + _dot(hp, n_unit)
        for t2 in range(2):
            t = 2 * j + t2
            y = yj[:, t2 * LANES:(t2 + 1) * LANES] + d_ref[...] * us[t]
            ysc_ref[pl.ds(t, rows, stride=SSM_CHUNK), :] = _gelu_tanh(y)
    y_ref[...] = ysc_ref[...].astype(y_ref.dtype)


def _ssm_out(u2d, hprev_re, hprev_im, mats, rows):
    n = u2d.shape[0]
    r = n // SSM_CHUNK
    nq = N_SSM_GROUPS // SSM_GPB
    half = SSM_GPB * SSM_STATE
    uspec = pl.BlockSpec((rows * SSM_CHUNK, LANES), lambda q, i: (i, q))
    hspec = pl.BlockSpec((rows, half), lambda q, i: (i, q))
    return pl.pallas_call(
        _ssm_out_kernel,
        out_shape=jax.ShapeDtypeStruct((n, SSM_WIDTH), BF16),
        grid=(nq, r // rows),
        in_specs=[uspec, pl.BlockSpec((1, SSM_CHUNK * LANES, 2 * LANES), lambda q, i: (q, 0, 0)), hspec, hspec,
                  pl.BlockSpec((1, SSM_CHUNK, 2 * half, LANES), lambda q, i: (q, 0, 0, 0)),
                  pl.BlockSpec((1, LANES), lambda q, i: (0, q))],
        out_specs=uspec,
        scratch_shapes=[pltpu.VMEM((rows * SSM_CHUNK, LANES), F32)],
        compiler_params=_params(("parallel", "parallel")), name="ssm_out")(
            u2d, mats['kstack'], hprev_re, hprev_im, mats['n_all'], mats['d'])


def _ssm(u2d, b, h0_re, h0_im, mats):
    n = u2d.shape[0]
    nchunk = n // b // SSM_CHUNK
    r = b * nchunk
    rows = _pick_tile(r, TILE_SSM_ROWS)
    hloc_re, hloc_im = _ssm_local(u2d, mats, rows)
    shp = (b, nchunk) + STATE_TILE
    hprev_re, hprev_im, f_re, f_im = _ssm_scan(hloc_re.reshape(shp), hloc_im.reshape(shp), mats,
                                               h0_re.reshape((b,) + STATE_TILE), h0_im.reshape((b,) + STATE_TILE))
    y = _ssm_out(u2d, hprev_re.reshape(r, -1), hprev_im.reshape(r, -1), mats, rows)
    return y, f_re.reshape(b, N_SSM_GROUPS, SSM_STATE), f_im.reshape(b, N_SSM_GROUPS, SSM_STATE)


def _merge_kernel(x_ref, of_ref, ys_ref, qm_ref, gate_ref, mk_ref, mv_ref,
                  wglu_ref, wbf_ref, wbs_ref, wbm_ref, wo_ref, nf_ref, wr_ref,
                  x1_ref, h2_ref, r_ref):
    tm = x_ref.shape[0]
    om = []
    for hd in range(N_MEM_HEADS):
        sl = slice(hd * MEM_HEAD_DIM, (hd + 1) * MEM_HEAD_DIM)
        head_rows = pl.ds(hd, N_MEM, stride=N_MEM_HEADS)
        kh = mk_ref[0, head_rows, :].astype(BF16)
        vh = mv_ref[0, head_rows, :].astype(BF16)
        sc = _dot_nt(qm_ref[:, sl], kh)
        p = jnp.exp(sc - jnp.max(sc, axis=-1, keepdims=True))
        om.append(_dot(p.astype(BF16), vh) / jnp.sum(p, axis=-1, keepdims=True))
    o_mem = jnp.concatenate(om, axis=-1).astype(BF16)
    z = _dot(ys_ref[...], wglu_ref[...])
    y_ssm = (z[:, 0:SSM_WIDTH] * jax.nn.sigmoid(z[:, SSM_WIDTH:2 * SSM_WIDTH])).astype(BF16)
    g = lambda c: gate_ref[:, c * D_MODEL:(c + 1) * D_MODEL].astype(F32)
    merged = (g(0) * _dot(of_ref[...], wbf_ref[...]) + g(1) * _dot(y_ssm, wbs_ref[...])
              + g(2) * _dot(o_mem, wbm_ref[...]))
    x1 = x_ref[...] + _dot(merged.astype(BF16), wo_ref[...])
    x1_ref[...] = x1
    h2 = x1 * lax.rsqrt(jnp.mean(x1 * x1, axis=-1, keepdims=True) + RMS_EPS) * nf_ref[...]
    h2_ref[...] = h2.astype(BF16)
    h2_hi, h2_lo = _split_bf16(h2)
    hw = _dot(h2_hi, wr_ref[...])
    logits = hw[:, 0:LANES] + hw[:, LANES:2 * LANES] + _dot(h2_lo, wr_ref[:, 0:LANES])
    lane = lax.broadcasted_iota(jnp.int32, (tm, LANES), 1)
    big = jnp.int32(LANES)
    is_grp = (lane >= N_EXPERTS) & (lane < N_EXPERTS + N_EXPERT_GROUPS)
    gl = jnp.where(is_grp, logits, NEG_INF)
    gmax = jnp.max(gl, axis=-1, keepdims=True)
    grp = jnp.min(jnp.where(is_grp & (gl == gmax), lane, big), axis=-1, keepdims=True) - N_EXPERTS
    g_w = 1.0 / jnp.sum(jnp.where(is_grp, jnp.exp(gl - gmax), 0.0), axis=-1, keepdims=True)
    in_grp = (lane >= grp * EXPERTS_PER_GROUP) & (lane < (grp + 1) * EXPERTS_PER_GROUP)
    e1 = jnp.where(in_grp, logits, NEG_INF)
    m1 = jnp.max(e1, axis=-1, keepdims=True)
    i1 = jnp.min(jnp.where(in_grp & (e1 == m1), lane, big), axis=-1, keepdims=True)
    rest = in_grp & (lane != i1)
    e2 = jnp.where(rest, logits, NEG_INF)
    m2 = jnp.max(e2, axis=-1, keepdims=True)
    i2 = jnp.min(jnp.where(rest & (e2 == m2), lane, big), axis=-1, keepdims=True)
    ex = jnp.exp(m2 - m1)
    w1 = g_w / (1.0 + ex)
    w2 = g_w * ex / (1.0 + ex)
    r_ref[...] = jnp.where(lane == i1, w1, jnp.where(lane == i2, w2, jnp.where(lane == GROUP_LANE, grp.astype(F32), 0.0)))


def _merge(x2d, o_fox, ys, q_m, gates, mem_k, mem_v, w, tm, rows_per_batch):
    n = x2d.shape[0]
    assert n % tm == 0 and rows_per_batch % tm == 0
    per = rows_per_batch // tm
    row = lambda width: pl.BlockSpec((tm, width), lambda i: (i, 0))
    memspec = pl.BlockSpec((1, N_MEM * N_MEM_HEADS, MEM_HEAD_DIM), lambda i: (i // per, 0, 0))
    ws = [w['w_glu'], w['w_br_fox'], w['w_br_ssm'], w['w_br_mem'], w['w_out'], w['norm_ffn'], w['w_router']]
    return pl.pallas_call(
        _merge_kernel,
        out_shape=(jax.ShapeDtypeStruct((n, D_MODEL), F32), jax.ShapeDtypeStruct((n, D_MODEL), BF16),
                   jax.ShapeDtypeStruct((n, LANES), F32)),
        grid=(n // tm,),
        in_specs=[row(D_MODEL), row(FOX_WIDTH), row(SSM_WIDTH), row(MEM_WIDTH), row(3 * D_MODEL), memspec, memspec]
                 + [_full(a.shape) for a in ws],
        out_specs=(row(D_MODEL), row(D_MODEL), row(LANES)),
        compiler_params=_params(("parallel",)), name="merge")(
            x2d, o_fox, ys, q_m, gates, mem_k, mem_v, *ws)


def _moe_kernel(h_ref, r_ref, x1_ref, tri_ref, wg_ref, wu_ref, wd_ref, o_ref,
                xs_ref, cw_ref, og_ref, acc_ref, rank_ref, count_ref, *, main):
    step = pl.program_id(1)
    steps_per_group = EXPERTS_PER_GROUP // MOE_EPS
    g = step // steps_per_group
    tm = h_ref.shape[0]
    gf = g.astype(F32)
    bounds = [0, main] + list(range(-(-main // MOE_SUB) * MOE_SUB, tm, MOE_SUB)) + [tm]
    bounds = sorted(set(bounds))
    blocks = [(lo, hi - lo, lo > 0) for lo, hi in zip(bounds[:-1], bounds[1:])]

    def guarded(r0, fn):
        if r0 == 0:
            fn()
        else:
            pl.when(r0 < count_ref[0])(fn)

    @pl.when(step == 0)
    def _():
        acc_ref[...] = jnp.zeros(acc_ref.shape, F32)

    @pl.when(step % steps_per_group == 0)
    def _():
        rt = r_ref[...]
        rtt = rt.T
        mrow = rtt[GROUP_LANE:GROUP_LANE + 1, :] == gf
        m8 = jnp.broadcast_to(jnp.where(mrow, 1.0, 0.0), (8, tm))
        rank8 = _dot(m8.astype(BF16), tri_ref[...])
        rank_row = jnp.where(mrow, rank8[0:1, :], -1.0)
        rank_ref[...] = jnp.broadcast_to(jnp.where(mrow, rank8, -1.0).T[:, 0:1], rank_ref.shape)
        count_ref[0] = jnp.sum(jnp.where(mrow, 1, 0))
        hilo = jnp.concatenate(_split_bf16(rt), axis=1)
        for r0, nrows, _ in blocks:
            def compact(r0=r0, nrows=nrows):
                rows = slice(r0, r0 + nrows)
                slot = r0 + lax.broadcasted_iota(jnp.int32, (nrows, tm), 0)
                perm = jnp.where(rank_row == slot.astype(F32), 1.0, 0.0).astype(BF16)
                xs_ref[rows, :] = _dot(perm, h_ref[...]).astype(BF16)
                cw = _dot(perm, hilo)
                cw_ref[rows, :] = cw[:, 0:LANES] + cw[:, LANES:2 * LANES]
                og_ref[rows, :] = jnp.zeros((nrows, D_MODEL), F32)
            guarded(r0, compact)

    for k in range(MOE_EPS):
        e = step * MOE_EPS + k
        for r0, nrows, _ in blocks:
            def expert(r0=r0, nrows=nrows, k=k, e=e):
                rows = slice(r0, r0 + nrows)
                x = xs_ref[rows, :]
                a = _dot(x, wg_ref[k])
                up = _dot(x, wu_ref[k])
                lane = lax.broadcasted_iota(jnp.int32, (nrows, LANES), 1)
                ce = jnp.sum(jnp.where(lane == e, cw_ref[rows, :], 0.0), axis=-1, keepdims=True)
                act = a * jax.nn.sigmoid(a) * up * ce
                og_ref[rows, :] += _dot(act.astype(BF16), wd_ref[k])
            guarded(r0, expert)

    @pl.when(step % steps_per_group == steps_per_group - 1)
    def _():
        for r0, nrows, _ in blocks:
            def scatter_back(r0=r0, nrows=nrows):
                rows = slice(r0, r0 + nrows)
                slot = r0 + lax.broadcasted_iota(jnp.int32, (tm, nrows), 1)
                back = jnp.where(rank_ref[:, 0:1] == slot.astype(F32), 1.0, 0.0).astype(BF16)
                acc_ref[...] += _dot(back, og_ref[rows, :].astype(BF16))
            guarded(r0, scatter_back)

    @pl.when(step == pl.num_programs(1) - 1)
    def _():
        o_ref[...] = x1_ref[...] + acc_ref[...]


def _moe(h2, route, x1, wg, wu, wd, tm):
    n = h2.shape[0]
    assert n % tm == 0 and tm % MOE_SUB == 0
    main = max(MOE_SUB // 2, (5 * tm // 16) // 64 * 64)
    row = lambda width: pl.BlockSpec((tm, width), lambda i, s: (i, 0))
    r = jnp.arange(tm)
    tri = (r[:, None] < r[None, :]).astype(BF16)
    return pl.pallas_call(
        functools.partial(_moe_kernel, main=main),
        out_shape=jax.ShapeDtypeStruct((n, D_MODEL), F32),
        grid=(n // tm, N_EXPERTS // MOE_EPS),
        in_specs=[row(D_MODEL), row(LANES), row(D_MODEL), pl.BlockSpec((tm, tm), lambda i, s: (0, 0)),
                  pl.BlockSpec((MOE_EPS, D_MODEL, D_EXPERT), lambda i, s: (s, 0, 0)),
                  pl.BlockSpec((MOE_EPS, D_MODEL, D_EXPERT), lambda i, s: (s, 0, 0)),
                  pl.BlockSpec((MOE_EPS, D_EXPERT, D_MODEL), lambda i, s: (s, 0, 0))],
        out_specs=row(D_MODEL),
        scratch_shapes=[pltpu.VMEM((tm, D_MODEL), BF16), pltpu.VMEM((tm, LANES), F32), pltpu.VMEM((tm, D_MODEL), F32),
                        pltpu.VMEM((tm, D_MODEL), F32), pltpu.VMEM((tm, LANES), F32), pltpu.SMEM((1,), jnp.int32)],
        compiler_params=_params(("parallel", "arbitrary")), name="moe")(h2, route, x1, tri, wg, wu, wd)


def _prep_weights(p):
    w_in = p['w_in'].astype(BF16)
    o = 0
    wqkv = w_in[:, 0:3 * FOX_WIDTH]
    o = 3 * FOX_WIDTH
    wf = jnp.pad(w_in[:, o:o + N_FOX_HEADS], ((0, 0), (0, LANES - N_FOX_HEADS)))
    o += N_FOX_HEADS
    wqm = w_in[:, o:o + MEM_WIDTH]
    o += MEM_WIDTH
    wu = w_in[:, o:o + SSM_WIDTH]
    o += SSM_WIDTH
    wg = w_in[:, o:o + 3 * D_MODEL]
    r = jnp.arange(FOX_WIDTH) // FOX_HEAD_DIM
    bd = (r[:, None] == r[None, :]).astype(BF16)
    w_router = jnp.concatenate(
        [p['w_router_expert'], p['w_router_group'],
         jnp.zeros((D_MODEL, LANES - N_EXPERTS - N_EXPERT_GROUPS), F32)], axis=1)
    w_router = jnp.concatenate(_split_bf16(w_router), axis=1)
    return dict(
        norm_mix=p['norm_mix'].reshape(1, D_MODEL), wqkv=wqkv, wf=wf,
        bf=jnp.pad(p['b_forget'], (0, LANES - N_FOX_HEADS)).reshape(1, LANES),
        wqm=wqm, wu=wu, wg=wg,
        qn_fox=jnp.tile(p['qn_fox'], N_FOX_HEADS).reshape(1, FOX_WIDTH),
        kn_fox=jnp.tile(p['kn_fox'], N_FOX_HEADS).reshape(1, FOX_WIDTH),
        qn_mem=p['qn_mem'].reshape(1, MEM_HEAD_DIM), bd=bd,
        w_glu=p['w_glu'].astype(BF16), w_br_fox=p['w_br_fox'].astype(BF16),
        w_br_ssm=p['w_br_ssm'].astype(BF16), w_br_mem=p['w_br_mem'].astype(BF16),
        w_out=p['w_out'].astype(BF16), norm_ffn=p['norm_ffn'].reshape(1, D_MODEL), w_router=w_router,
        moe_wg=p['moe_w_gate'].astype(BF16), moe_wu=p['moe_w_up'].astype(BF16),
        moe_wd=p['moe_w_down'].astype(BF16))


def _pick_tile(n, target):
    t = min(n, target)
    while n % t:
        t //= 2
    return t


def _group(x, w, mats, mem_k, mem_v, h0_re, h0_im, cache):
    b, s, _ = x.shape
    n = b * s
    x2d = x.reshape(n, D_MODEL)
    prompt = cache is None
    q, kb, vb, k_out, v_out, lf_t, q_m, u, gates = _inproj(x2d, w, _pick_tile(s if prompt else n, TILE_INPROJ), s, prompt)
    q3 = q.reshape(b, s, FOX_WIDTH)
    k3 = kb.reshape(b, s, FOX_WIDTH)
    v3 = vb.reshape(b, s, FOX_WIDTH)
    lf_rows = lf_t.reshape(N_FOX_HEADS, b, s).transpose(1, 0, 2)
    lf3 = lf_rows.transpose(0, 2, 1)
    npair = N_FOX_HEADS // 2
    if prompt:
        c_row = LOG2E * _cumsum_rows(lf_rows.reshape(b * N_FOX_HEADS, s)).reshape(b, npair, 2, s)
        o_fox = _fox_prompt(q3, k3, v3, c_row, _pick_tile(s, TILE_FOX_Q), _pick_tile(s, TILE_FOX_K))
        unt = lambda a: a.reshape(b, N_FOX_HEADS, FOX_HEAD_DIM, s).transpose(0, 3, 1, 2)
        k4, v4 = unt(k_out), unt(v_out)
    else:
        cache_k, cache_v, cache_logf = cache
        past = cache_k.shape[1]
        lf_all = jnp.concatenate([cache_logf.astype(F32).transpose(0, 2, 1), lf_rows], axis=2)
        c_row = LOG2E * _cumsum_rows(lf_all.reshape(b * N_FOX_HEADS, past + s)).reshape(b, N_FOX_HEADS, past + s)
        o_fox = _fox_sample(q3, cache_k.transpose(0, 2, 3, 1), cache_v.transpose(0, 2, 3, 1), k3, v3,
                            c_row[:, :, past:].transpose(0, 2, 1), c_row[:, :, :past], c_row[:, :, past:],
                            _pick_tile(past, TILE_SAMPLE_K))
        k4 = k_out.reshape(b, s, N_FOX_HEADS, FOX_HEAD_DIM)
        v4 = v_out.reshape(b, s, N_FOX_HEADS, FOX_HEAD_DIM)
    ys, f_re, f_im = _ssm(u, b, h0_re, h0_im, mats)
    tm = _pick_tile(s, TILE_MERGE)
    x1, h2, route = _merge(x2d, o_fox.reshape(n, FOX_WIDTH), ys, q_m, gates, mem_k, mem_v, w, tm, s)
    y = _moe(h2, route, x1, w['moe_wg'], w['moe_wu'], w['moe_wd'], _pick_tile(n, TILE_MOE))
    return y.reshape(b, s, D_MODEL), k4, v4, lf3, f_re, f_im


def kernel(x_prompt, x_sample, mem_prompt, cache_fox_k, cache_fox_v, cache_fox_logf, state_ssm_re, state_ssm_im,
           cache_mem_k, cache_mem_v, norm_mix, w_in, b_forget, qn_fox, kn_fox, qn_mem, kn_mem, norm_mem, w_mem_kv,
           ssm_a_re, ssm_a_im, ssm_log_dt, ssm_b_re, ssm_b_im, ssm_c_re, ssm_c_im, ssm_d, w_glu, w_br_fox,
           w_br_ssm, w_br_mem, w_out, norm_ffn, w_router_group, w_router_expert, moe_w_gate, moe_w_up,
           moe_w_down):
    depth = norm_mix.shape[0]
    assert depth == 1
    l = 0
    p = dict(norm_mix=norm_mix[l], w_in=w_in[l], b_forget=b_forget[l], qn_fox=qn_fox[l], kn_fox=kn_fox[l],
             qn_mem=qn_mem[l], ssm_a_re=ssm_a_re[l], ssm_a_im=ssm_a_im[l], ssm_log_dt=ssm_log_dt[l],
             ssm_b_re=ssm_b_re[l], ssm_b_im=ssm_b_im[l], ssm_c_re=ssm_c_re[l], ssm_c_im=ssm_c_im[l],
             ssm_d=ssm_d[l], w_glu=w_glu[l], w_br_fox=w_br_fox[l], w_br_ssm=w_br_ssm[l], w_br_mem=w_br_mem[l],
             w_out=w_out[l], norm_ffn=norm_ffn[l], w_router_group=w_router_group[l],
             w_router_expert=w_router_expert[l], moe_w_gate=moe_w_gate[l], moe_w_up=moe_w_up[l],
             moe_w_down=moe_w_down[l])
    w = _prep_weights(p)
    mats = _ssm_mats(p)
    bp, sp, _ = x_prompt.shape
    bs, ss, _ = x_sample.shape

    mk, mv = _memkv(mem_prompt.reshape(bp * N_MEM, D_MODEL), norm_mem[l].reshape(1, D_MODEL),
                    w_mem_kv[l].astype(BF16), kn_mem[l].reshape(1, MEM_HEAD_DIM), _pick_tile(bp * N_MEM, TILE_MEMKV))
    mem_rows = lambda a, b: a.reshape(b, N_MEM * N_MEM_HEADS, MEM_HEAD_DIM)
    mk = mem_rows(mk, bp)
    mv = mem_rows(mv, bp)
    zeros = jnp.zeros((bp, N_SSM_GROUPS, SSM_STATE), F32)
    yp, pk, pv, plf, pre, pim = _group(x_prompt, w, mats, mk, mv, zeros, zeros, None)
    cache = (cache_fox_k[l], cache_fox_v[l], cache_fox_logf[l])
    ys, sk, sv, slf, sre, sim = _group(
        x_sample, w, mats, mem_rows(cache_mem_k[l], bs), mem_rows(cache_mem_v[l], bs),
        state_ssm_re[l].astype(F32), state_ssm_im[l].astype(F32), cache)
    st = lambda a: a[None]
    return (yp, ys, st(pk), st(pv), st(plf), st(pre), st(pim),
            st(mk.reshape(bp, N_MEM, N_MEM_HEADS, MEM_HEAD_DIM)), st(mv.reshape(bp, N_MEM, N_MEM_HEADS, MEM_HEAD_DIM)),
            st(sk), st(sv), st(slf), st(sre), st(sim))
```

```python
import functools
import math

import jax
import jax.numpy as jnp
from jax import lax
from jax.experimental import pallas as pl
from jax.experimental.pallas import tpu as pltpu

F32 = jnp.float32
BF16 = jnp.bfloat16

D_MODEL = 1024
N_FOX_HEADS = 8
FOX_HEAD_DIM = 64
FOX_WIDTH = N_FOX_HEADS * FOX_HEAD_DIM
N_MEM = 256
N_MEM_HEADS = 4
MEM_HEAD_DIM = 128
MEM_WIDTH = N_MEM_HEADS * MEM_HEAD_DIM
SSM_GROUP = 16
SSM_WIDTH = 512
N_SSM_GROUPS = SSM_WIDTH // SSM_GROUP
SSM_STATE = 64
N_EXPERT_GROUPS = 4
EXPERTS_PER_GROUP = 8
N_EXPERTS = N_EXPERT_GROUPS * EXPERTS_PER_GROUP
D_EXPERT = 256
RMS_EPS = 1e-6
NEG_INF = -1e30
LOG2E = 1.4426950408889634

LANES = 128
SSM_CHUNK = 16
SSM_GPB = LANES // SSM_GROUP
GROUP_LANE = N_EXPERTS
MOE_SUB = 128
MOE_EPS = 4
VMEM_LIMIT = 56 * 1024 * 1024
STATE_TILE = (8, N_SSM_GROUPS * SSM_STATE // 8)

TILE_INPROJ = 512
TILE_MEMKV = 512
TILE_FOX_Q = 512
TILE_FOX_K = 512
TILE_SAMPLE_K = 4096
TILE_SSM_ROWS = 256
TILE_MERGE = 512
TILE_MOE = 1024


def _dot(a, b):
    return jnp.dot(a, b, preferred_element_type=F32)


def _dot_nt(a, b):
    return lax.dot_general(a, b, (((1,), (1,)), ((), ())), preferred_element_type=F32)


def _dot_exact(a, b):
    return jnp.dot(a, b, preferred_element_type=F32, precision=lax.Precision.HIGHEST)


def _split_bf16(x):
    hi = x.astype(BF16)
    lo = (x - hi.astype(F32)).astype(BF16)
    return hi, lo


def _params(sem):
    return pltpu.CompilerParams(dimension_semantics=sem, vmem_limit_bytes=VMEM_LIMIT)


def _full(shape):
    n = len(shape)
    return pl.BlockSpec(shape, lambda *_: (0,) * n)


def _inproj_kernel(x_ref, g_ref, wqkv_ref, wf_ref, bf_ref, wqm_ref, wu_ref, wg_ref,
                   qn_ref, kn_ref, qmn_ref, bd_ref,
                   q_ref, kb_ref, vb_ref, k_ref, v_ref, lf_ref, qm_ref, u_ref, gate_ref, *, kv_transposed):
    x = x_ref[...]
    h = x * lax.rsqrt(jnp.mean(x * x, axis=-1, keepdims=True) + RMS_EPS) * g_ref[...]
    hb = h.astype(BF16)

    def head_norm(z, gain):
        ss = _dot((z * z).astype(BF16), bd_ref[...])
        return z * lax.rsqrt(ss * (1.0 / FOX_HEAD_DIM) + RMS_EPS) * gain

    zq = _dot(hb, wqkv_ref[:, 0:FOX_WIDTH])
    q_ref[...] = (head_norm(zq, qn_ref[...]) * (LOG2E * FOX_HEAD_DIM ** -0.5)).astype(BF16)
    zk = _dot(hb, wqkv_ref[:, FOX_WIDTH:2 * FOX_WIDTH])
    kn = head_norm(zk, kn_ref[...])
    zv = _dot(hb, wqkv_ref[:, 2 * FOX_WIDTH:3 * FOX_WIDTH])
    kb_ref[...] = kn.astype(BF16)
    vb_ref[...] = zv.astype(BF16)
    tm = x_ref.shape[0]
    if kv_transposed:
        k_ref[0] = kn.T
        v_ref[0] = zv.T
    else:
        for hd in range(N_FOX_HEADS):
            hs = slice(hd * FOX_HEAD_DIM, (hd + 1) * FOX_HEAD_DIM)
            rows = pl.ds(hd, tm, stride=N_FOX_HEADS)
            k_ref[rows, :] = kn[:, hs]
            v_ref[rows, :] = zv[:, hs]

    zf = (_dot(hb, wf_ref[...]) + bf_ref[...]).T[0:N_FOX_HEADS, :]
    lf_ref[...] = jnp.minimum(zf, 0.0) - jnp.log1p(jnp.exp(-jnp.abs(zf)))

    zm = _dot(hb, wqm_ref[...])
    for hd in range(N_MEM_HEADS):
        sl = slice(hd * MEM_HEAD_DIM, (hd + 1) * MEM_HEAD_DIM)
        zh = zm[:, sl]
        ms = jnp.mean(zh * zh, axis=-1, keepdims=True)
        qm_ref[:, sl] = (zh * lax.rsqrt(ms + RMS_EPS) * qmn_ref[...] * (MEM_HEAD_DIM ** -0.5)).astype(BF16)

    u_ref[...] = _dot(hb, wu_ref[...])
    for c in range(3):
        sl = slice(c * D_MODEL, (c + 1) * D_MODEL)
        gate_ref[:, sl] = (0.5 * jnp.tanh(0.5 * _dot(hb, wg_ref[:, sl])) + 0.5).astype(BF16)


def _inproj(x2d, w, tm, seq, kv_transposed):
    n = x2d.shape[0]
    assert n % tm == 0
    row = lambda width: pl.BlockSpec((tm, width), lambda i: (i, 0))
    if kv_transposed:
        assert seq % tm == 0
        per = seq // tm
        kv_shape = jax.ShapeDtypeStruct((n // seq, FOX_WIDTH, seq), F32)
        heads = pl.BlockSpec((1, FOX_WIDTH, tm), lambda i: (i // per, 0, i % per))
    else:
        kv_shape = jax.ShapeDtypeStruct((n * N_FOX_HEADS, FOX_HEAD_DIM), F32)
        heads = pl.BlockSpec((tm * N_FOX_HEADS, FOX_HEAD_DIM), lambda i: (i, 0))
    ins = [x2d, w['norm_mix'], w['wqkv'], w['wf'], w['bf'], w['wqm'], w['wu'], w['wg'],
           w['qn_fox'], w['kn_fox'], w['qn_mem'], w['bd']]
    in_specs = [row(D_MODEL)] + [_full(a.shape) for a in ins[1:]]
    out_shape = (
        jax.ShapeDtypeStruct((n, FOX_WIDTH), BF16),
        jax.ShapeDtypeStruct((n, FOX_WIDTH), BF16),
        jax.ShapeDtypeStruct((n, FOX_WIDTH), BF16),
        kv_shape,
        kv_shape,
        jax.ShapeDtypeStruct((N_FOX_HEADS, n), F32),
        jax.ShapeDtypeStruct((n, MEM_WIDTH), BF16),
        jax.ShapeDtypeStruct((n, SSM_WIDTH), F32),
        jax.ShapeDtypeStruct((n, 3 * D_MODEL), BF16),
    )
    out_specs = (row(FOX_WIDTH), row(FOX_WIDTH), row(FOX_WIDTH), heads, heads,
                 pl.BlockSpec((N_FOX_HEADS, tm), lambda i: (0, i)),
                 row(MEM_WIDTH), row(SSM_WIDTH), row(3 * D_MODEL))
    return pl.pallas_call(
        functools.partial(_inproj_kernel, kv_transposed=kv_transposed),
        out_shape=out_shape, grid=(n // tm,), in_specs=in_specs, out_specs=out_specs,
        compiler_params=_params(("parallel",)), name="inproj")(*ins)


def _memkv_kernel(x_ref, g_ref, w_ref, kn_ref, k_ref, v_ref):
    x = x_ref[...]
    h = x * lax.rsqrt(jnp.mean(x * x, axis=-1, keepdims=True) + RMS_EPS) * g_ref[...]
    hb = h.astype(BF16)
    tm = x_ref.shape[0]
    zk = _dot(hb, w_ref[:, 0:MEM_WIDTH])
    zv = _dot(hb, w_ref[:, MEM_WIDTH:2 * MEM_WIDTH])
    for hd in range(N_MEM_HEADS):
        sl = slice(hd * MEM_HEAD_DIM, (hd + 1) * MEM_HEAD_DIM)
        rows = pl.ds(hd, tm, stride=N_MEM_HEADS)
        zh = zk[:, sl]
        ms = jnp.mean(zh * zh, axis=-1, keepdims=True)
        k_ref[rows, :] = zh * lax.rsqrt(ms + RMS_EPS) * kn_ref[...]
        v_ref[rows, :] = zv[:, sl]


def _memkv(mem2d, norm_mem, w_kv, kn_mem, tm):
    n = mem2d.shape[0]
    out = jax.ShapeDtypeStruct((n * N_MEM_HEADS, MEM_HEAD_DIM), F32)
    ospec = pl.BlockSpec((tm * N_MEM_HEADS, MEM_HEAD_DIM), lambda i: (i, 0))
    return pl.pallas_call(
        _memkv_kernel,
        out_shape=(out, out),
        grid=(n // tm,),
        in_specs=[pl.BlockSpec((tm, D_MODEL), lambda i: (i, 0)), _full(norm_mem.shape), _full(w_kv.shape),
                  _full(kn_mem.shape)],
        out_specs=(ospec, ospec),
        compiler_params=_params(("parallel",)), name="memkv")(mem2d, norm_mem, w_kv, kn_mem)


CUMSUM_BLOCK = 256


def _cumsum_kernel(x_ref, o_ref):
    nblk = x_ref.shape[1] // CUMSUM_BLOCK
    r = lax.broadcasted_iota(jnp.int32, (CUMSUM_BLOCK, CUMSUM_BLOCK), 0)
    c = lax.broadcasted_iota(jnp.int32, (CUMSUM_BLOCK, CUMSUM_BLOCK), 1)
    tri = (r <= c).astype(F32)
    carry = jnp.zeros((x_ref.shape[0], 1), F32)
    for j in range(nblk):
        sl = slice(j * CUMSUM_BLOCK, (j + 1) * CUMSUM_BLOCK)
        cs = _dot_exact(x_ref[:, sl], tri) + carry
        o_ref[:, sl] = cs
        carry = cs[:, CUMSUM_BLOCK - 1:CUMSUM_BLOCK]


def _cumsum_rows(x):
    rows, n = x.shape
    npad = -(-n // CUMSUM_BLOCK) * CUMSUM_BLOCK
    xp = jnp.pad(x, ((0, 0), (0, npad - n))) if npad != n else x
    out = pl.pallas_call(
        _cumsum_kernel, out_shape=jax.ShapeDtypeStruct((rows, npad), F32), grid=(1,),
        in_specs=[_full((rows, npad))], out_specs=_full((rows, npad)),
        compiler_params=_params(("arbitrary",)), name="cumsum")(xp)
    return out[:, :n] if npad != n else out


def _reduce_rows(x, op):
    rows, cols = x.shape
    if rows > 64 and rows % 64 == 0:
        x = op(x.reshape(rows // 64, 64, cols), axis=0)
        rows = 64
    if rows == 64:
        x = op(x.reshape(8, 8, cols), axis=0)
    return op(x, axis=0, keepdims=True)


def _fox_prompt_kernel(q_ref, k_ref, v_ref, cr_ref, o_ref,
                       vt_ref, ck0_ref, ck1_ref, st0_ref, st1_ref, pt0_ref, pt1_ref, acc_ref, *, tq, tk):
    st_refs = (st0_ref, st1_ref)
    pt_refs = (pt0_ref, pt1_ref)
    i = pl.program_id(2)
    s_len = k_ref.shape[1]

    @pl.when(i == 0)
    def _():
        vt_ref[...] = v_ref[0].astype(F32).T.astype(BF16)
        ck0_ref[...] = jnp.broadcast_to(cr_ref[0, 0, 0:1, :], (LANES, s_len)).T
        ck1_ref[...] = jnp.broadcast_to(cr_ref[0, 0, 1:2, :], (LANES, s_len)).T

    def q_heads_t(blk):
        qt = q_ref[0, pl.ds(pl.multiple_of(blk * tq, tq), tq), :].astype(F32).T
        row = lax.broadcasted_iota(jnp.int32, (LANES, tq), 0)
        return (jnp.where(row < FOX_HEAD_DIM, qt, 0.0).astype(BF16),
                jnp.where(row < FOX_HEAD_DIM, 0.0, qt).astype(BF16))

    qts = q_heads_t(i)
    q0 = pl.multiple_of(i * tq, tq)
    cq = cr_ref[0, 0, :, pl.ds(q0, tq)]
    ck_refs = (ck0_ref, ck1_ref)

    def stage_a(n, par, qts=qts):
        s = pl.multiple_of(n * tk, tk)
        kb = k_ref[0, pl.ds(s, tk), :]
        for hh in range(2):
            ck = ck_refs[hh][pl.ds(s, tk), :]
            st_refs[par][hh] = _dot(kb, qts[hh]) - jnp.concatenate([ck] * (tq // LANES), axis=1)

    def stage_b(n, par, stats, masked):
        if masked:
            kpos = n * tk + lax.broadcasted_iota(jnp.int32, (tk, tq), 0)
            qpos = q0 + lax.broadcasted_iota(jnp.int32, (tk, tq), 1)
            mask = kpos <= qpos
        out = []
        for hh in range(2):
            m, l = stats[2 * hh:2 * hh + 2]
            t = st_refs[par][hh]
            if masked:
                t = jnp.where(mask, t, NEG_INF)
            cqh = cq[hh:hh + 1, :]
            m_new = jnp.maximum(m, _reduce_rows(t, jnp.max) + cqh)
            alpha = jnp.exp2(m - m_new)
            p = jnp.exp2(t + (cqh - m_new))
            pt_refs[par][hh] = p.astype(BF16)
            out.extend([m_new, alpha * l + _reduce_rows(p, jnp.sum), alpha])
        return tuple(out)

    def stage_c(n, par, alphas):
        s = pl.multiple_of(jnp.maximum(n, 0) * tk, tk)
        for hh in range(2):
            vt = vt_ref[hh * FOX_HEAD_DIM:(hh + 1) * FOX_HEAD_DIM, pl.ds(s, tk)]
            acc_ref[hh] = alphas[hh] * acc_ref[hh] + _dot(vt, pt_refs[par][hh])

    def iteration(n, par, carry):
        m0, l0, al0, m1, l1, al1 = carry
        stage_c(n - 1, 1 - par, (al0, al1))
        new = stage_b(n, par, (m0, l0, m1, l1), False)
        stage_a(n + 1, 1 - par)
        return new

    def finish(par, carry):
        m0, l0, al0, m1, l1, al1 = carry
        stage_c(nfull - 1, 1 - par, (al0, al1))
        _, l0, be0, _, l1, be1 = stage_b(nfull, par, (m0, l0, m1, l1), True)
        stage_a(0, 0, q_heads_t(jnp.minimum(i + 1, pl.num_programs(2) - 1)))
        stage_c(nfull, par, (be0, be1))
        ot = jnp.concatenate([acc_ref[0] / l0, acc_ref[1] / l1], axis=0)
        o_ref[0] = ot.T.astype(o_ref.dtype)

    acc_ref[...] = jnp.zeros(acc_ref.shape, F32)
    pt1_ref[...] = jnp.zeros(pt1_ref.shape, BF16)
    neg = jnp.full((1, tq), NEG_INF, F32)
    zero = jnp.zeros((1, tq), F32)
    one = jnp.ones((1, tq), F32)
    nfull = (i * tq) // tk
    pl.when(i == 0)(lambda: stage_a(0, 0))
    carry = lax.fori_loop(0, nfull // 2, lambda k, c: iteration(2 * k + 1, 1, iteration(2 * k, 0, c)),
                          (neg, zero, one, neg, zero, one))
    odd = nfull % 2 == 1
    carry = lax.cond(odd, lambda c: iteration(nfull - 1, 0, c), lambda c: c, carry)
    pl.when(odd)(lambda: finish(1, carry))
    pl.when(jnp.logical_not(odd))(lambda: finish(0, carry))


def _fox_prompt(q, k, v, c_row, tq, tk):
    b, s, _ = q.shape
    assert s % tk == 0 and tk % tq == 0
    npair = N_FOX_HEADS // 2
    return pl.pallas_call(
        functools.partial(_fox_prompt_kernel, tq=tq, tk=tk),
        out_shape=jax.ShapeDtypeStruct((b, s, FOX_WIDTH), BF16),
        grid=(b, npair, s // tq),
        in_specs=[
            pl.BlockSpec((1, s, LANES), lambda bi, hp, i: (bi, 0, hp)),
            pl.BlockSpec((1, s, LANES), lambda bi, hp, i: (bi, 0, hp)),
            pl.BlockSpec((1, s, LANES), lambda bi, hp, i: (bi, 0, hp)),
            pl.BlockSpec((1, 1, 2, s), lambda bi, hp, i: (bi, hp, 0, 0)),
        ],
        out_specs=pl.BlockSpec((1, tq, LANES), lambda bi, hp, i: (bi, i, hp)),
        scratch_shapes=[pltpu.VMEM((LANES, s), BF16),
                        pltpu.VMEM((s, LANES), F32), pltpu.VMEM((s, LANES), F32),
                        pltpu.VMEM((2, tk, tq), F32), pltpu.VMEM((2, tk, tq), F32),
                        pltpu.VMEM((2, tk, tq), BF16), pltpu.VMEM((2, tk, tq), BF16),
                        pltpu.VMEM((2, FOX_HEAD_DIM, tq), F32)],
        compiler_params=_params(("parallel", "parallel", "arbitrary")), name="fox_prompt")(q, k, v, c_row)


def _fox_sample_kernel(q_ref, ck_ref, cv_ref, nk_ref, nv_ref, cq_ref, crc_ref, crn_ref, o_ref, *state, n):
    j = pl.program_id(1)
    nj = pl.num_programs(1)
    m_refs = state[0:N_FOX_HEADS]
    l_refs = state[N_FOX_HEADS:2 * N_FOX_HEADS]
    acc_refs = state[2 * N_FOX_HEADS:3 * N_FOX_HEADS]

    @pl.when(j == 0)
    def _():
        for hd in range(N_FOX_HEADS):
            m_refs[hd][...] = jnp.full(m_refs[hd].shape, NEG_INF, F32)
            l_refs[hd][...] = jnp.zeros(l_refs[hd].shape, F32)
            acc_refs[hd][...] = jnp.zeros(acc_refs[hd].shape, F32)

    def update(k_of, v_of, cr_ref_, mask, transposed):
        qk = _dot if transposed else _dot_nt
        pv = _dot_nt if transposed else _dot
        ts = []
        for hd in range(N_FOX_HEADS):
            hs = slice(hd * FOX_HEAD_DIM, (hd + 1) * FOX_HEAD_DIM)
            t = qk(q_ref[0, :, hs], k_of(hd)) - cr_ref_[0, hd:hd + 1, :]
            ts.append(t if mask is None else jnp.where(mask, t, NEG_INF))
        ps = []
        for hd in range(N_FOX_HEADS):
            cq = cq_ref[0, :, hd:hd + 1]
            m = m_refs[hd][...]
            m_new = jnp.maximum(m, jnp.max(ts[hd], axis=-1, keepdims=True) + cq)
            alpha = jnp.exp2(m - m_new)
            p = jnp.exp2(ts[hd] + (cq - m_new))
            m_refs[hd][...] = m_new
            l_refs[hd][...] = alpha * l_refs[hd][...] + jnp.sum(p, axis=-1, keepdims=True)
            ps.append((alpha, p.astype(BF16)))
        for hd in range(N_FOX_HEADS):
            alpha, p = ps[hd]
            acc_refs[hd][...] = alpha * acc_refs[hd][...] + pv(p, v_of(hd))

    update(lambda hd: ck_ref[0, hd].astype(BF16), lambda hd: cv_ref[0, hd].astype(BF16), crc_ref, None, True)

    @pl.when(j == nj - 1)
    def _():
        r = lax.broadcasted_iota(jnp.int32, (n, n), 0)
        c = lax.broadcasted_iota(jnp.int32, (n, n), 1)
        head = lambda ref: (lambda hd: ref[0, :, hd * FOX_HEAD_DIM:(hd + 1) * FOX_HEAD_DIM])
        update(head(nk_ref), head(nv_ref), crn_ref, c <= r, False)
        for hd in range(N_FOX_HEADS):
            hs = slice(hd * FOX_HEAD_DIM, (hd + 1) * FOX_HEAD_DIM)
            o_ref[0, :, hs] = (acc_refs[hd][...] / l_refs[hd][...]).astype(o_ref.dtype)


def _fox_sample(q, cache_k, cache_v, k_new, v_new, c_q, c_row_cache, c_row_new, tk):
    b, n, _ = q.shape
    past = cache_k.shape[3]
    assert past % tk == 0
    cache_spec = pl.BlockSpec((1, N_FOX_HEADS, FOX_HEAD_DIM, tk), lambda bi, j: (bi, 0, 0, j))
    return pl.pallas_call(
        functools.partial(_fox_sample_kernel, n=n),
        out_shape=jax.ShapeDtypeStruct((b, n, FOX_WIDTH), BF16),
        grid=(b, past // tk),
        in_specs=[
            pl.BlockSpec((1, n, FOX_WIDTH), lambda bi, j: (bi, 0, 0)),
            cache_spec,
            cache_spec,
            pl.BlockSpec((1, n, FOX_WIDTH), lambda bi, j: (bi, 0, 0)),
            pl.BlockSpec((1, n, FOX_WIDTH), lambda bi, j: (bi, 0, 0)),
            pl.BlockSpec((1, n, N_FOX_HEADS), lambda bi, j: (bi, 0, 0)),
            pl.BlockSpec((1, N_FOX_HEADS, tk), lambda bi, j: (bi, 0, j)),
            pl.BlockSpec((1, N_FOX_HEADS, n), lambda bi, j: (bi, 0, 0)),
        ],
        out_specs=pl.BlockSpec((1, n, FOX_WIDTH), lambda bi, j: (bi, 0, 0)),
        scratch_shapes=([pltpu.VMEM((n, 1), F32)] * (2 * N_FOX_HEADS)
                        + [pltpu.VMEM((n, FOX_HEAD_DIM), F32)] * N_FOX_HEADS),
        compiler_params=_params(("parallel", "arbitrary")), name="fox_sample")(
            q, cache_k, cache_v, k_new, v_new, c_q, c_row_cache, c_row_new)


def _ssm_mats(p):
    f32 = F32
    a_re, a_im = p['ssm_a_re'].astype(f32), p['ssm_a_im'].astype(f32)
    b_re, b_im = p['ssm_b_re'].astype(f32), p['ssm_b_im'].astype(f32)
    c_re, c_im = p['ssm_c_re'].astype(f32), p['ssm_c_im'].astype(f32)
    dt = jnp.exp(p['ssm_log_dt'].astype(f32))[:, None]
    mag = jnp.exp(dt * a_re)
    ab_re = mag * jnp.cos(dt * a_im)
    ab_im = mag * jnp.sin(dt * a_im)
    den = a_re * a_re + a_im * a_im
    nr, ni = ab_re - 1.0, ab_im
    coef_re = (nr * a_re + ni * a_im) / den
    coef_im = (ni * a_re - nr * a_im) / den
    bb_re = coef_re[..., None] * b_re - coef_im[..., None] * b_im
    bb_im = coef_re[..., None] * b_im + coef_im[..., None] * b_re
    pr, pi = [jnp.ones_like(ab_re)], [jnp.zeros_like(ab_im)]
    for _ in range(SSM_CHUNK):
        pr.append(pr[-1] * ab_re - pi[-1] * ab_im)
        pi.append(pr[-2] * ab_im + pi[-1] * ab_re)
    pw_re, pw_im = jnp.stack(pr), jnp.stack(pi)
    T = SSM_CHUNK
    w_re = pw_re[..., None] * bb_re[None] - pw_im[..., None] * bb_im[None]
    w_im = pw_re[..., None] * bb_im[None] + pw_im[..., None] * bb_re[None]
    kk = (jnp.einsum('gop,kgpi->kgoi', c_re, w_re[:T], precision='highest')
          - jnp.einsum('gop,kgpi->kgoi', c_im, w_im[:T], precision='highest'))
    nq = N_SSM_GROUPS // SSM_GPB

    def group_diag(m):
        rows, c = m.shape[-2:]
        m = jnp.tile(m, (1,) * (m.ndim - 1) + (SSM_GPB,))
        same = (jnp.arange(rows) // (rows // SSM_GPB))[:, None] == (jnp.arange(SSM_GPB * c) // c)[None, :]
        return jnp.where(same, m, 0.0)

    def lane_diag(m):
        lead = m.shape[:-3]
        i, c = m.shape[-2:]
        return group_diag(m.reshape(lead + (nq, SSM_GPB * i, c)))

    ktau = lane_diag(jnp.swapaxes(kk, -1, -2))
    ktau = jnp.concatenate([jnp.zeros_like(ktau[:1]), ktau], axis=0)
    units = []
    for dlag in range(T // 2 - 1, -1, -1):
        top = jnp.concatenate([ktau[2 * dlag + 1], ktau[2 * dlag + 2]], axis=-1)
        bot = jnp.concatenate([ktau[2 * dlag], ktau[2 * dlag + 1]], axis=-1)
        units.append(jnp.concatenate([top, bot], axis=-2))
    kstack = jnp.concatenate(units, axis=-2).astype(BF16)
    rev = T - 1 - jnp.arange(T)
    def local_rows(w):
        w = jnp.transpose(w[rev], (1, 0, 3, 2)).reshape(nq, SSM_GPB, T, SSM_GROUP, SSM_STATE)
        return group_diag(jnp.transpose(w, (0, 2, 1, 3, 4)).reshape(nq, T, LANES, SSM_STATE))

    m_all = jnp.concatenate([local_rows(w_re), local_rows(w_im)], axis=-1)
    m_all = m_all.reshape(nq, T * LANES, 2 * SSM_GPB * SSM_STATE)
    m_hi, m_lo = _split_bf16(m_all)
    ar, ai = pw_re[1:], pw_im[1:]
    n_re = (c_re[None] * ar[:, :, None, :] - c_im[None] * ai[:, :, None, :])
    n_im = -(c_re[None] * ai[:, :, None, :] + c_im[None] * ar[:, :, None, :])

    def state_rows(n):
        n = jnp.transpose(n, (1, 3, 0, 2)).reshape(nq, SSM_GPB * SSM_STATE, T, SSM_GROUP)
        n = jnp.tile(jnp.transpose(n, (0, 2, 1, 3)), (1, 1, 1, SSM_GPB))
        same = (jnp.arange(SSM_GPB * SSM_STATE) // SSM_STATE)[:, None] == (jnp.arange(LANES) // SSM_GROUP)[None, :]
        return jnp.where(same, n, 0.0)

    n_all = jnp.concatenate([state_rows(n_re), state_rows(n_im)], axis=2).astype(BF16)
    return dict(kstack=kstack, m_hi=m_hi, m_lo=m_lo, n_all=n_all,
                a16_re=pw_re[T].reshape(STATE_TILE), a16_im=pw_im[T].reshape(STATE_TILE),
                d=p['ssm_d'].astype(f32).reshape(1, SSM_WIDTH))


def _chunk_tokens(u_ref, rows):
    return [u_ref[pl.ds(t, rows, stride=SSM_CHUNK), :] for t in range(SSM_CHUNK)]


def _ssm_local_kernel(u_ref, mh_ref, ml_ref, hre_ref, him_ref):
    rows = hre_ref.shape[0]
    parts = [_split_bf16(ut) for ut in _chunk_tokens(u_ref, rows)]
    x_hi = jnp.concatenate([h for h, _ in parts], axis=1)
    x_lo = jnp.concatenate([l for _, l in parts], axis=1)
    h = _dot(x_hi, mh_ref[0]) + _dot(x_hi, ml_ref[0]) + _dot(x_lo, mh_ref[0])
    half = SSM_GPB * SSM_STATE
    hre_ref[...] = h[:, 0:half]
    him_ref[...] = h[:, half:2 * half]


def _ssm_local(u2d, mats, rows):
    n = u2d.shape[0]
    r = n // SSM_CHUNK
    nq = N_SSM_GROUPS // SSM_GPB
    half = SSM_GPB * SSM_STATE
    mspec = pl.BlockSpec((1, SSM_CHUNK * LANES, 2 * half), lambda q, i: (q, 0, 0))
    ospec = pl.BlockSpec((rows, half), lambda q, i: (i, q))
    return pl.pallas_call(
        _ssm_local_kernel,
        out_shape=(jax.ShapeDtypeStruct((r, N_SSM_GROUPS * SSM_STATE), F32),) * 2,
        grid=(nq, r // rows),
        in_specs=[pl.BlockSpec((rows * SSM_CHUNK, LANES), lambda q, i: (i, q)), mspec, mspec],
        out_specs=(ospec, ospec),
        compiler_params=_params(("parallel", "parallel")), name="ssm_local")(u2d, mats['m_hi'], mats['m_lo'])


def _ssm_scan_kernel(lre_ref, lim_ref, are_ref, aim_ref, h0re_ref, h0im_ref,
                     pre_ref, pim_ref, fre_ref, fim_ref):
    nchunk = lre_ref.shape[1]
    ar, ai = are_ref[...], aim_ref[...]

    def body(c, carry):
        hr, hi = carry
        pre_ref[0, c] = hr
        pim_ref[0, c] = hi
        return (ar * hr - ai * hi + lre_ref[0, c], ar * hi + ai * hr + lim_ref[0, c])

    hr, hi = lax.fori_loop(0, nchunk, body, (h0re_ref[0], h0im_ref[0]))
    fre_ref[0] = hr
    fim_ref[0] = hi


def _ssm_scan(hloc_re, hloc_im, mats, h0_re, h0_im):
    b, nchunk = hloc_re.shape[:2]
    big = pl.BlockSpec((1, nchunk) + STATE_TILE, lambda i: (i, 0, 0, 0))
    small = pl.BlockSpec((1,) + STATE_TILE, lambda i: (i, 0, 0))
    return pl.pallas_call(
        _ssm_scan_kernel,
        out_shape=(jax.ShapeDtypeStruct(hloc_re.shape, F32),) * 2 + (jax.ShapeDtypeStruct((b,) + STATE_TILE, F32),) * 2,
        grid=(b,),
        in_specs=[big, big, _full(STATE_TILE), _full(STATE_TILE), small, small],
        out_specs=(big, big, small, small),
        compiler_params=_params(("parallel",)), name="ssm_scan")(
            hloc_re, hloc_im, mats['a16_re'], mats['a16_im'], h0_re, h0_im)


def _gelu_tanh(y):
    return 0.5 * y * (1.0 + jnp.tanh(math.sqrt(2.0 / math.pi) * (y + 0.044715 * (y * y * y))))


def _ssm_out_kernel(u_ref, k_ref, pre_ref, pim_ref, n_ref, d_ref, y_ref, ysc_ref):
    rows = pre_ref.shape[0]
    us = _chunk_tokens(u_ref, rows)
    x = jnp.concatenate([ut.astype(BF16) for ut in us], axis=1)
    hp = jnp.concatenate([pre_ref[...], pim_ref[...]], axis=1).astype(BF16)
    unit = 2 * LANES
    nunit = SSM_CHUNK // 2
    for j in range(nunit):
        n_unit = jnp.concatenate([n_ref[0, 2 * j], n_ref[0, 2 * j + 1]], axis=1)
        yj = _dot(x[:, 0:unit * (j + 1)], k_ref[0, unit * (nunit - 1 - j):, :]) + _dot(hp, n_unit)
        for t2 in range(2):
            t = 2 * j + t2
            y = yj[:, t2 * LANES:(t2 + 1) * LANES] + d_ref[...] * us[t]
            ysc_ref[pl.ds(t, rows, stride=SSM_CHUNK), :] = _gelu_tanh(y)
    y_ref[...] = ysc_ref[...].astype(y_ref.dtype)


def _ssm_out(u2d, hprev_re, hprev_im, mats, rows):
    n = u2d.shape[0]
    r = n // SSM_CHUNK
    nq = N_SSM_GROUPS // SSM_GPB
    half = SSM_GPB * SSM_STATE
    uspec = pl.BlockSpec((rows * SSM_CHUNK, LANES), lambda q, i: (i, q))
    hspec = pl.BlockSpec((rows, half), lambda q, i: (i, q))
    return pl.pallas_call(
        _ssm_out_kernel,
        out_shape=jax.ShapeDtypeStruct((n, SSM_WIDTH), BF16),
        grid=(nq, r // rows),
        in_specs=[uspec, pl.BlockSpec((1, SSM_CHUNK * LANES, 2 * LANES), lambda q, i: (q, 0, 0)), hspec, hspec,
                  pl.BlockSpec((1, SSM_CHUNK, 2 * half, LANES), lambda q, i: (q, 0, 0, 0)),
                  pl.BlockSpec((1, LANES), lambda q, i: (0, q))],
        out_specs=uspec,
        scratch_shapes=[pltpu.VMEM((rows * SSM_CHUNK, LANES), F32)],
        compiler_params=_params(("parallel", "parallel")), name="ssm_out")(
            u2d, mats['kstack'], hprev_re, hprev_im, mats['n_all'], mats['d'])


def _ssm(u2d, b, h0_re, h0_im, mats):
    n = u2d.shape[0]
    nchunk = n // b // SSM_CHUNK
    r = b * nchunk
    rows = _pick_tile(r, TILE_SSM_ROWS)
    hloc_re, hloc_im = _ssm_local(u2d, mats, rows)
    shp = (b, nchunk) + STATE_TILE
    hprev_re, hprev_im, f_re, f_im = _ssm_scan(hloc_re.reshape(shp), hloc_im.reshape(shp), mats,
                                               h0_re.reshape((b,) + STATE_TILE), h0_im.reshape((b,) + STATE_TILE))
    y = _ssm_out(u2d, hprev_re.reshape(r, -1), hprev_im.reshape(r, -1), mats, rows)
    return y, f_re.reshape(b, N_SSM_GROUPS, SSM_STATE), f_im.reshape(b, N_SSM_GROUPS, SSM_STATE)


def _merge_kernel(x_ref, of_ref, ys_ref, qm_ref, gate_ref, mk_ref, mv_ref,
                  wglu_ref, wbf_ref, wbs_ref, wbm_ref, wo_ref, nf_ref, wr_ref,
                  x1_ref, h2_ref, r_ref):
    tm = x_ref.shape[0]
    nb = mk_ref.shape[0]
    rows = tm // nb
    om = []
    for hd in range(N_MEM_HEADS):
        sl = slice(hd * MEM_HEAD_DIM, (hd + 1) * MEM_HEAD_DIM)
        head_rows = pl.ds(hd, N_MEM, stride=N_MEM_HEADS)
        per_batch = []
        for bi in range(nb):
            kh = mk_ref[bi, head_rows, :].astype(BF16)
            vh = mv_ref[bi, head_rows, :].astype(BF16)
            sc = _dot_nt(qm_ref[bi * rows:(bi + 1) * rows, sl], kh)
            p = jnp.exp(sc - jnp.max(sc, axis=-1, keepdims=True))
            per_batch.append(_dot(p.astype(BF16), vh) / jnp.sum(p, axis=-1, keepdims=True))
        om.append(per_batch[0] if nb == 1 else jnp.concatenate(per_batch, axis=0))
    o_mem = jnp.concatenate(om, axis=-1).astype(BF16)
    z = _dot(ys_ref[...], wglu_ref[...])
    y_ssm = (z[:, 0:SSM_WIDTH] * jax.nn.sigmoid(z[:, SSM_WIDTH:2 * SSM_WIDTH])).astype(BF16)
    g = lambda c: gate_ref[:, c * D_MODEL:(c + 1) * D_MODEL].astype(F32)
    merged = (g(0) * _dot(of_ref[...], wbf_ref[...]) + g(1) * _dot(y_ssm, wbs_ref[...])
              + g(2) * _dot(o_mem, wbm_ref[...]))
    x1 = x_ref[...] + _dot(merged.astype(BF16), wo_ref[...])
    x1_ref[...] = x1
    h2 = x1 * lax.rsqrt(jnp.mean(x1 * x1, axis=-1, keepdims=True) + RMS_EPS) * nf_ref[...]
    h2_ref[...] = h2.astype(BF16)
    h2_hi, h2_lo = _split_bf16(h2)
    hw = _dot(h2_hi, wr_ref[...])
    logits = hw[:, 0:LANES] + hw[:, LANES:2 * LANES] + _dot(h2_lo, wr_ref[:, 0:LANES])
    lane = lax.broadcasted_iota(jnp.int32, (tm, LANES), 1)
    big = jnp.int32(LANES)
    is_grp = (lane >= N_EXPERTS) & (lane < N_EXPERTS + N_EXPERT_GROUPS)
    gl = jnp.where(is_grp, logits, NEG_INF)
    gmax = jnp.max(gl, axis=-1, keepdims=True)
    grp = jnp.min(jnp.where(is_grp & (gl == gmax), lane, big), axis=-1, keepdims=True) - N_EXPERTS
    g_w = 1.0 / jnp.sum(jnp.where(is_grp, jnp.exp(gl - gmax), 0.0), axis=-1, keepdims=True)
    in_grp = (lane >= grp * EXPERTS_PER_GROUP) & (lane < (grp + 1) * EXPERTS_PER_GROUP)
    e1 = jnp.where(in_grp, logits, NEG_INF)
    m1 = jnp.max(e1, axis=-1, keepdims=True)
    i1 = jnp.min(jnp.where(in_grp & (e1 == m1), lane, big), axis=-1, keepdims=True)
    rest = in_grp & (lane != i1)
    e2 = jnp.where(rest, logits, NEG_INF)
    m2 = jnp.max(e2, axis=-1, keepdims=True)
    i2 = jnp.min(jnp.where(rest & (e2 == m2), lane, big), axis=-1, keepdims=True)
    ex = jnp.exp(m2 - m1)
    w1 = g_w / (1.0 + ex)
    w2 = g_w * ex / (1.0 + ex)
    r_ref[...] = jnp.where(lane == i1, w1, jnp.where(lane == i2, w2, jnp.where(lane == GROUP_LANE, grp.astype(F32), 0.0)))


def _merge(x2d, o_fox, ys, q_m, gates, mem_k, mem_v, w, tm, rows_per_batch):
    n = x2d.shape[0]
    assert n % tm == 0 and (rows_per_batch % tm == 0 or tm % rows_per_batch == 0)
    row = lambda width: pl.BlockSpec((tm, width), lambda i: (i, 0))
    if rows_per_batch >= tm:
        per = rows_per_batch // tm
        memspec = pl.BlockSpec((1, N_MEM * N_MEM_HEADS, MEM_HEAD_DIM), lambda i: (i // per, 0, 0))
    else:
        memspec = pl.BlockSpec((tm // rows_per_batch, N_MEM * N_MEM_HEADS, MEM_HEAD_DIM), lambda i: (i, 0, 0))
    ws = [w['w_glu'], w['w_br_fox'], w['w_br_ssm'], w['w_br_mem'], w['w_out'], w['norm_ffn'], w['w_router']]
    return pl.pallas_call(
        _merge_kernel,
        out_shape=(jax.ShapeDtypeStruct((n, D_MODEL), F32), jax.ShapeDtypeStruct((n, D_MODEL), BF16),
                   jax.ShapeDtypeStruct((n, LANES), F32)),
        grid=(n // tm,),
        in_specs=[row(D_MODEL), row(FOX_WIDTH), row(SSM_WIDTH), row(MEM_WIDTH), row(3 * D_MODEL), memspec, memspec]
                 + [_full(a.shape) for a in ws],
        out_specs=(row(D_MODEL), row(D_MODEL), row(LANES)),
        compiler_params=_params(("parallel",)), name="merge")(
            x2d, o_fox, ys, q_m, gates, mem_k, mem_v, *ws)


def _moe_kernel(h_ref, r_ref, x1_ref, tri_ref, wg_ref, wu_ref, wd_ref, o_ref,
                xs_ref, cw_ref, og_ref, acc_ref, rank_ref, count_ref, *, main):
    step = pl.program_id(1)
    steps_per_group = EXPERTS_PER_GROUP // MOE_EPS
    g = step // steps_per_group
    tm = h_ref.shape[0]
    gf = g.astype(F32)
    bounds = [0, main] + list(range(-(-main // MOE_SUB) * MOE_SUB, tm, MOE_SUB)) + [tm]
    bounds = sorted(set(bounds))
    blocks = [(lo, hi - lo, lo > 0) for lo, hi in zip(bounds[:-1], bounds[1:])]

    def guarded(r0, fn):
        if r0 == 0:
            fn()
        else:
            pl.when(r0 < count_ref[0])(fn)

    @pl.when(step == 0)
    def _():
        acc_ref[...] = jnp.zeros(acc_ref.shape, F32)

    @pl.when(step % steps_per_group == 0)
    def _():
        rt = r_ref[...]
        rtt = rt.T
        mrow = rtt[GROUP_LANE:GROUP_LANE + 1, :] == gf
        m8 = jnp.broadcast_to(jnp.where(mrow, 1.0, 0.0), (8, tm))
        rank8 = _dot(m8.astype(BF16), tri_ref[...])
        rank_row = jnp.where(mrow, rank8[0:1, :], -1.0)
        rank_ref[...] = jnp.broadcast_to(jnp.where(mrow, rank8, -1.0).T[:, 0:1], rank_ref.shape)
        count_ref[0] = jnp.sum(jnp.where(mrow, 1, 0))
        hilo = jnp.concatenate(_split_bf16(rt), axis=1)
        for r0, nrows, _ in blocks:
            def compact(r0=r0, nrows=nrows):
                rows = slice(r0, r0 + nrows)
                slot = r0 + lax.broadcasted_iota(jnp.int32, (nrows, tm), 0)
                perm = jnp.where(rank_row == slot.astype(F32), 1.0, 0.0).astype(BF16)
                xs_ref[rows, :] = _dot(perm, h_ref[...]).astype(BF16)
                cw = _dot(perm, hilo)
                cw_ref[rows, :] = cw[:, 0:LANES] + cw[:, LANES:2 * LANES]
                og_ref[rows, :] = jnp.zeros((nrows, D_MODEL), F32)
            guarded(r0, compact)

    for k in range(MOE_EPS):
        e = step * MOE_EPS + k
        for r0, nrows, _ in blocks:
            def expert(r0=r0, nrows=nrows, k=k, e=e):
                rows = slice(r0, r0 + nrows)
                x = xs_ref[rows, :]
                a = _dot(x, wg_ref[k])
                up = _dot(x, wu_ref[k])
                lane = lax.broadcasted_iota(jnp.int32, (nrows, LANES), 1)
                ce = jnp.sum(jnp.where(lane == e, cw_ref[rows, :], 0.0), axis=-1, keepdims=True)
                act = a * jax.nn.sigmoid(a) * up * ce
                og_ref[rows, :] += _dot(act.astype(BF16), wd_ref[k])
            guarded(r0, expert)

    @pl.when(step % steps_per_group == steps_per_group - 1)
    def _():
        for r0, nrows, _ in blocks:
            def scatter_back(r0=r0, nrows=nrows):
                rows = slice(r0, r0 + nrows)
                slot = r0 + lax.broadcasted_iota(jnp.int32, (tm, nrows), 1)
                back = jnp.where(rank_ref[:, 0:1] == slot.astype(F32), 1.0, 0.0).astype(BF16)
                acc_ref[...] += _dot(back, og_ref[rows, :].astype(BF16))
            guarded(r0, scatter_back)

    @pl.when(step == pl.num_programs(1) - 1)
    def _():
        o_ref[...] = x1_ref[...] + acc_ref[...]


def _moe(h2, route, x1, wg, wu, wd, tm):
    n = h2.shape[0]
    assert n % tm == 0 and tm % MOE_SUB == 0
    main = max(MOE_SUB // 2, (5 * tm // 16) // 64 * 64)
    row = lambda width: pl.BlockSpec((tm, width), lambda i, s: (i, 0))
    r = jnp.arange(tm)
    tri = (r[:, None] < r[None, :]).astype(BF16)
    return pl.pallas_call(
        functools.partial(_moe_kernel, main=main),
        out_shape=jax.ShapeDtypeStruct((n, D_MODEL), F32),
        grid=(n // tm, N_EXPERTS // MOE_EPS),
        in_specs=[row(D_MODEL), row(LANES), row(D_MODEL), pl.BlockSpec((tm, tm), lambda i, s: (0, 0)),
                  pl.BlockSpec((MOE_EPS, D_MODEL, D_EXPERT), lambda i, s: (s, 0, 0)),
                  pl.BlockSpec((MOE_EPS, D_MODEL, D_EXPERT), lambda i, s: (s, 0, 0)),
                  pl.BlockSpec((MOE_EPS, D_EXPERT, D_MODEL), lambda i, s: (s, 0, 0))],
        out_specs=row(D_MODEL),
        scratch_shapes=[pltpu.VMEM((tm, D_MODEL), BF16), pltpu.VMEM((tm, LANES), F32), pltpu.VMEM((tm, D_MODEL), F32),
                        pltpu.VMEM((tm, D_MODEL), F32), pltpu.VMEM((tm, LANES), F32), pltpu.SMEM((1,), jnp.int32)],
        compiler_params=_params(("parallel", "arbitrary")), name="moe")(h2, route, x1, tri, wg, wu, wd)


def _prep_weights(p):
    w_in = p['w_in'].astype(BF16)
    o = 0
    wqkv = w_in[:, 0:3 * FOX_WIDTH]
    o = 3 * FOX_WIDTH
    wf = jnp.pad(w_in[:, o:o + N_FOX_HEADS], ((0, 0), (0, LANES - N_FOX_HEADS)))
    o += N_FOX_HEADS
    wqm = w_in[:, o:o + MEM_WIDTH]
    o += MEM_WIDTH
    wu = w_in[:, o:o + SSM_WIDTH]
    o += SSM_WIDTH
    wg = w_in[:, o:o + 3 * D_MODEL]
    r = jnp.arange(FOX_WIDTH) // FOX_HEAD_DIM
    bd = (r[:, None] == r[None, :]).astype(BF16)
    w_router = jnp.concatenate(
        [p['w_router_expert'], p['w_router_group'],
         jnp.zeros((D_MODEL, LANES - N_EXPERTS - N_EXPERT_GROUPS), F32)], axis=1)
    w_router = jnp.concatenate(_split_bf16(w_router), axis=1)
    return dict(
        norm_mix=p['norm_mix'].reshape(1, D_MODEL), wqkv=wqkv, wf=wf,
        bf=jnp.pad(p['b_forget'], (0, LANES - N_FOX_HEADS)).reshape(1, LANES),
        wqm=wqm, wu=wu, wg=wg,
        qn_fox=jnp.tile(p['qn_fox'], N_FOX_HEADS).reshape(1, FOX_WIDTH),
        kn_fox=jnp.tile(p['kn_fox'], N_FOX_HEADS).reshape(1, FOX_WIDTH),
        qn_mem=p['qn_mem'].reshape(1, MEM_HEAD_DIM), bd=bd,
        w_glu=p['w_glu'].astype(BF16), w_br_fox=p['w_br_fox'].astype(BF16),
        w_br_ssm=p['w_br_ssm'].astype(BF16), w_br_mem=p['w_br_mem'].astype(BF16),
        w_out=p['w_out'].astype(BF16), norm_ffn=p['norm_ffn'].reshape(1, D_MODEL), w_router=w_router,
        moe_wg=p['moe_w_gate'].astype(BF16), moe_wu=p['moe_w_up'].astype(BF16),
        moe_wd=p['moe_w_down'].astype(BF16))


def _pick_tile(n, target):
    t = min(n, target)
    while n % t:
        t //= 2
    return t


def _group(x, w, mats, mem_k, mem_v, h0_re, h0_im, cache):
    b, s, _ = x.shape
    n = b * s
    x2d = x.reshape(n, D_MODEL)
    prompt = cache is None
    q, kb, vb, k_out, v_out, lf_t, q_m, u, gates = _inproj(x2d, w, _pick_tile(s if prompt else n, TILE_INPROJ), s, prompt)
    q3 = q.reshape(b, s, FOX_WIDTH)
    k3 = kb.reshape(b, s, FOX_WIDTH)
    v3 = vb.reshape(b, s, FOX_WIDTH)
    lf_rows = lf_t.reshape(N_FOX_HEADS, b, s).transpose(1, 0, 2)
    lf3 = lf_rows.transpose(0, 2, 1)
    npair = N_FOX_HEADS // 2
    if prompt:
        c_row = LOG2E * _cumsum_rows(lf_rows.reshape(b * N_FOX_HEADS, s)).reshape(b, npair, 2, s)
        o_fox = _fox_prompt(q3, k3, v3, c_row, _pick_tile(s, TILE_FOX_Q), _pick_tile(s, TILE_FOX_K))
        unt = lambda a: a.reshape(b, N_FOX_HEADS, FOX_HEAD_DIM, s).transpose(0, 3, 1, 2)
        k4, v4 = unt(k_out), unt(v_out)
    else:
        cache_k, cache_v, cache_logf = cache
        past = cache_k.shape[1]
        lf_all = jnp.concatenate([cache_logf.astype(F32).transpose(0, 2, 1), lf_rows], axis=2)
        c_row = LOG2E * _cumsum_rows(lf_all.reshape(b * N_FOX_HEADS, past + s)).reshape(b, N_FOX_HEADS, past + s)
        o_fox = _fox_sample(q3, cache_k.transpose(0, 2, 3, 1), cache_v.transpose(0, 2, 3, 1), k3, v3,
                            c_row[:, :, past:].transpose(0, 2, 1), c_row[:, :, :past], c_row[:, :, past:],
                            _pick_tile(past, TILE_SAMPLE_K))
        k4 = k_out.reshape(b, s, N_FOX_HEADS, FOX_HEAD_DIM)
        v4 = v_out.reshape(b, s, N_FOX_HEADS, FOX_HEAD_DIM)
    ys, f_re, f_im = _ssm(u, b, h0_re, h0_im, mats)
    tm = _pick_tile(n, TILE_MERGE) if TILE_MERGE % s == 0 else _pick_tile(s, TILE_MERGE)
    x1, h2, route = _merge(x2d, o_fox.reshape(n, FOX_WIDTH), ys, q_m, gates, mem_k, mem_v, w, tm, s)
    y = _moe(h2, route, x1, w['moe_wg'], w['moe_wu'], w['moe_wd'], _pick_tile(n, TILE_MOE))
    return y.reshape(b, s, D_MODEL), k4, v4, lf3, f_re, f_im


def kernel(x_prompt, x_sample, mem_prompt, cache_fox_k, cache_fox_v, cache_fox_logf, state_ssm_re, state_ssm_im,
           cache_mem_k, cache_mem_v, norm_mix, w_in, b_forget, qn_fox, kn_fox, qn_mem, kn_mem, norm_mem, w_mem_kv,
           ssm_a_re, ssm_a_im, ssm_log_dt, ssm_b_re, ssm_b_im, ssm_c_re, ssm_c_im, ssm_d, w_glu, w_br_fox,
           w_br_ssm, w_br_mem, w_out, norm_ffn, w_router_group, w_router_expert, moe_w_gate, moe_w_up,
           moe_w_down):
    depth = norm_mix.shape[0]
    assert depth == 1
    l = 0
    p = dict(norm_mix=norm_mix[l], w_in=w_in[l], b_forget=b_forget[l], qn_fox=qn_fox[l], kn_fox=kn_fox[l],
             qn_mem=qn_mem[l], ssm_a_re=ssm_a_re[l], ssm_a_im=ssm_a_im[l], ssm_log_dt=ssm_log_dt[l],
             ssm_b_re=ssm_b_re[l], ssm_b_im=ssm_b_im[l], ssm_c_re=ssm_c_re[l], ssm_c_im=ssm_c_im[l],
             ssm_d=ssm_d[l], w_glu=w_glu[l], w_br_fox=w_br_fox[l], w_br_ssm=w_br_ssm[l], w_br_mem=w_br_mem[l],
             w_out=w_out[l], norm_ffn=norm_ffn[l], w_router_group=w_router_group[l],
             w_router_expert=w_router_expert[l], moe_w_gate=moe_w_gate[l], moe_w_up=moe_w_up[l],
             moe_w_down=moe_w_down[l])
    w = _prep_weights(p)
    mats = _ssm_mats(p)
    bp, sp, _ = x_prompt.shape
    bs, ss, _ = x_sample.shape

    mk, mv = _memkv(mem_prompt.reshape(bp * N_MEM, D_MODEL), norm_mem[l].reshape(1, D_MODEL),
                    w_mem_kv[l].astype(BF16), kn_mem[l].reshape(1, MEM_HEAD_DIM), _pick_tile(bp * N_MEM, TILE_MEMKV))
    mem_rows = lambda a, b: a.reshape(b, N_MEM * N_MEM_HEADS, MEM_HEAD_DIM)
    mk = mem_rows(mk, bp)
    mv = mem_rows(mv, bp)
    zeros = jnp.zeros((bp, N_SSM_GROUPS, SSM_STATE), F32)
    yp, pk, pv, plf, pre, pim = _group(x_prompt, w, mats, mk, mv, zeros, zeros, None)
    cache = (cache_fox_k[l], cache_fox_v[l], cache_fox_logf[l])
    ys, sk, sv, slf, sre, sim = _group(
        x_sample, w, mats, mem_rows(cache_mem_k[l], bs), mem_rows(cache_mem_v[l], bs),
        state_ssm_re[l].astype(F32), state_ssm_im[l].astype(F32), cache)
    st = lambda a: a[None]
    return (yp, ys, st(pk), st(pv), st(plf), st(pre), st(pim),
            st(mk.reshape(bp, N_MEM, N_MEM_HEADS, MEM_HEAD_DIM)), st(mv.reshape(bp, N_MEM, N_MEM_HEADS, MEM_HEAD_DIM)),
            st(sk), st(sv), st(slf), st(sre), st(sim))
```

```python
import functools
import math

import jax
import jax.numpy as jnp
from jax import lax
from jax.experimental import pallas as pl
from jax.experimental.pallas import tpu as pltpu

F32 = jnp.float32
BF16 = jnp.bfloat16

D_MODEL = 1024
N_FOX_HEADS = 8
FOX_HEAD_DIM = 64
FOX_WIDTH = N_FOX_HEADS * FOX_HEAD_DIM
N_MEM = 256
N_MEM_HEADS = 4
MEM_HEAD_DIM = 128
MEM_WIDTH = N_MEM_HEADS * MEM_HEAD_DIM
SSM_GROUP = 16
SSM_WIDTH = 512
N_SSM_GROUPS = SSM_WIDTH // SSM_GROUP
SSM_STATE = 64
N_EXPERT_GROUPS = 4
EXPERTS_PER_GROUP = 8
N_EXPERTS = N_EXPERT_GROUPS * EXPERTS_PER_GROUP
D_EXPERT = 256
RMS_EPS = 1e-6
NEG_INF = -1e30
LOG2E = 1.4426950408889634

LANES = 128
SSM_CHUNK = 16
SSM_GPB = LANES // SSM_GROUP
GROUP_LANE = N_EXPERTS
MOE_SUB = 128
MOE_EPS = 4
VMEM_LIMIT = 56 * 1024 * 1024
STATE_TILE = (8, N_SSM_GROUPS * SSM_STATE // 8)

TILE_INPROJ = 512
TILE_MEMKV = 512
TILE_FOX_Q = 512
TILE_FOX_K = 512
TILE_SAMPLE_K = 4096
TILE_SSM_ROWS = 256
SCAN_CHUNK_ROWS = 512
TILE_MERGE = 512
TILE_MOE = 1024


def _dot(a, b):
    return jnp.dot(a, b, preferred_element_type=F32)


def _dot_nt(a, b):
    return lax.dot_general(a, b, (((1,), (1,)), ((), ())), preferred_element_type=F32)


def _dot_exact(a, b):
    return jnp.dot(a, b, preferred_element_type=F32, precision=lax.Precision.HIGHEST)


def _split_bf16(x):
    hi = x.astype(BF16)
    lo = (x - hi.astype(F32)).astype(BF16)
    return hi, lo


def _params(sem):
    return pltpu.CompilerParams(dimension_semantics=sem, vmem_limit_bytes=VMEM_LIMIT)


def _full(shape):
    n = len(shape)
    return pl.BlockSpec(shape, lambda *_: (0,) * n)


def _inproj_kernel(x_ref, g_ref, wqkv_ref, wf_ref, bf_ref, wqm_ref, wu_ref, wg_ref,
                   qn_ref, kn_ref, qmn_ref, bd_ref,
                   q_ref, kb_ref, vb_ref, k_ref, v_ref, lf_ref, qm_ref, u_ref, gate_ref, *, kv_transposed):
    x = x_ref[...]
    h = x * lax.rsqrt(jnp.mean(x * x, axis=-1, keepdims=True) + RMS_EPS) * g_ref[...]
    hb = h.astype(BF16)

    def head_norm(z, gain):
        ss = _dot((z * z).astype(BF16), bd_ref[...])
        return z * lax.rsqrt(ss * (1.0 / FOX_HEAD_DIM) + RMS_EPS) * gain

    zq = _dot(hb, wqkv_ref[:, 0:FOX_WIDTH])
    q_ref[...] = (head_norm(zq, qn_ref[...]) * (LOG2E * FOX_HEAD_DIM ** -0.5)).astype(BF16)
    zk = _dot(hb, wqkv_ref[:, FOX_WIDTH:2 * FOX_WIDTH])
    kn = head_norm(zk, kn_ref[...])
    zv = _dot(hb, wqkv_ref[:, 2 * FOX_WIDTH:3 * FOX_WIDTH])
    kb_ref[...] = kn.astype(BF16)
    vb_ref[...] = zv.astype(BF16)
    tm = x_ref.shape[0]
    if kv_transposed:
        k_ref[0] = kn.T
        v_ref[0] = zv.T
    else:
        for hd in range(N_FOX_HEADS):
            hs = slice(hd * FOX_HEAD_DIM, (hd + 1) * FOX_HEAD_DIM)
            rows = pl.ds(hd, tm, stride=N_FOX_HEADS)
            k_ref[rows, :] = kn[:, hs]
            v_ref[rows, :] = zv[:, hs]

    zf = (_dot(hb, wf_ref[...]) + bf_ref[...]).T[0:N_FOX_HEADS, :]
    lf_ref[...] = jnp.minimum(zf, 0.0) - jnp.log1p(jnp.exp(-jnp.abs(zf)))

    zm = _dot(hb, wqm_ref[...])
    for hd in range(N_MEM_HEADS):
        sl = slice(hd * MEM_HEAD_DIM, (hd + 1) * MEM_HEAD_DIM)
        zh = zm[:, sl]
        ms = jnp.mean(zh * zh, axis=-1, keepdims=True)
        qm_ref[:, sl] = (zh * lax.rsqrt(ms + RMS_EPS) * qmn_ref[...] * (MEM_HEAD_DIM ** -0.5)).astype(BF16)

    u_ref[...] = _dot(hb, wu_ref[...])
    for c in range(3):
        sl = slice(c * D_MODEL, (c + 1) * D_MODEL)
        gate_ref[:, sl] = (0.5 * jnp.tanh(0.5 * _dot(hb, wg_ref[:, sl])) + 0.5).astype(BF16)


def _inproj(x2d, w, tm, seq, kv_transposed):
    n = x2d.shape[0]
    assert n % tm == 0
    row = lambda width: pl.BlockSpec((tm, width), lambda i: (i, 0))
    if kv_transposed:
        assert seq % tm == 0
        per = seq // tm
        kv_shape = jax.ShapeDtypeStruct((n // seq, FOX_WIDTH, seq), F32)
        heads = pl.BlockSpec((1, FOX_WIDTH, tm), lambda i: (i // per, 0, i % per))
    else:
        kv_shape = jax.ShapeDtypeStruct((n * N_FOX_HEADS, FOX_HEAD_DIM), F32)
        heads = pl.BlockSpec((tm * N_FOX_HEADS, FOX_HEAD_DIM), lambda i: (i, 0))
    ins = [x2d, w['norm_mix'], w['wqkv'], w['wf'], w['bf'], w['wqm'], w['wu'], w['wg'],
           w['qn_fox'], w['kn_fox'], w['qn_mem'], w['bd']]
    in_specs = [row(D_MODEL)] + [_full(a.shape) for a in ins[1:]]
    out_shape = (
        jax.ShapeDtypeStruct((n, FOX_WIDTH), BF16),
        jax.ShapeDtypeStruct((n, FOX_WIDTH), BF16),
        jax.ShapeDtypeStruct((n, FOX_WIDTH), BF16),
        kv_shape,
        kv_shape,
        jax.ShapeDtypeStruct((N_FOX_HEADS, n), F32),
        jax.ShapeDtypeStruct((n, MEM_WIDTH), BF16),
        jax.ShapeDtypeStruct((n, SSM_WIDTH), F32),
        jax.ShapeDtypeStruct((n, 3 * D_MODEL), BF16),
    )
    out_specs = (row(FOX_WIDTH), row(FOX_WIDTH), row(FOX_WIDTH), heads, heads,
                 pl.BlockSpec((N_FOX_HEADS, tm), lambda i: (0, i)),
                 row(MEM_WIDTH), row(SSM_WIDTH), row(3 * D_MODEL))
    return pl.pallas_call(
        functools.partial(_inproj_kernel, kv_transposed=kv_transposed),
        out_shape=out_shape, grid=(n // tm,), in_specs=in_specs, out_specs=out_specs,
        compiler_params=_params(("parallel",)), name="inproj")(*ins)


def _memkv_kernel(x_ref, g_ref, w_ref, kn_ref, k_ref, v_ref):
    x = x_ref[...]
    h = x * lax.rsqrt(jnp.mean(x * x, axis=-1, keepdims=True) + RMS_EPS) * g_ref[...]
    hb = h.astype(BF16)
    tm = x_ref.shape[0]
    zk = _dot(hb, w_ref[:, 0:MEM_WIDTH])
    zv = _dot(hb, w_ref[:, MEM_WIDTH:2 * MEM_WIDTH])
    for hd in range(N_MEM_HEADS):
        sl = slice(hd * MEM_HEAD_DIM, (hd + 1) * MEM_HEAD_DIM)
        rows = pl.ds(hd, tm, stride=N_MEM_HEADS)
        zh = zk[:, sl]
        ms = jnp.mean(zh * zh, axis=-1, keepdims=True)
        k_ref[rows, :] = zh * lax.rsqrt(ms + RMS_EPS) * kn_ref[...]
        v_ref[rows, :] = zv[:, sl]


def _memkv(mem2d, norm_mem, w_kv, kn_mem, tm):
    n = mem2d.shape[0]
    out = jax.ShapeDtypeStruct((n * N_MEM_HEADS, MEM_HEAD_DIM), F32)
    ospec = pl.BlockSpec((tm * N_MEM_HEADS, MEM_HEAD_DIM), lambda i: (i, 0))
    return pl.pallas_call(
        _memkv_kernel,
        out_shape=(out, out),
        grid=(n // tm,),
        in_specs=[pl.BlockSpec((tm, D_MODEL), lambda i: (i, 0)), _full(norm_mem.shape), _full(w_kv.shape),
                  _full(kn_mem.shape)],
        out_specs=(ospec, ospec),
        compiler_params=_params(("parallel",)), name="memkv")(mem2d, norm_mem, w_kv, kn_mem)


CUMSUM_BLOCK = 256


def _cumsum_kernel(x_ref, o_ref):
    nblk = x_ref.shape[1] // CUMSUM_BLOCK
    r = lax.broadcasted_iota(jnp.int32, (CUMSUM_BLOCK, CUMSUM_BLOCK), 0)
    c = lax.broadcasted_iota(jnp.int32, (CUMSUM_BLOCK, CUMSUM_BLOCK), 1)
    tri = (r <= c).astype(F32)
    carry = jnp.zeros((x_ref.shape[0], 1), F32)
    for j in range(nblk):
        sl = slice(j * CUMSUM_BLOCK, (j + 1) * CUMSUM_BLOCK)
        cs = _dot_exact(x_ref[:, sl], tri) + carry
        o_ref[:, sl] = cs
        carry = cs[:, CUMSUM_BLOCK - 1:CUMSUM_BLOCK]


def _cumsum_rows(x):
    rows, n = x.shape
    npad = -(-n // CUMSUM_BLOCK) * CUMSUM_BLOCK
    xp = jnp.pad(x, ((0, 0), (0, npad - n))) if npad != n else x
    out = pl.pallas_call(
        _cumsum_kernel, out_shape=jax.ShapeDtypeStruct((rows, npad), F32), grid=(1,),
        in_specs=[_full((rows, npad))], out_specs=_full((rows, npad)),
        compiler_params=_params(("arbitrary",)), name="cumsum")(xp)
    return out[:, :n] if npad != n else out


def _reduce_rows(x, op):
    rows, cols = x.shape
    if rows > 64 and rows % 64 == 0:
        x = op(x.reshape(rows // 64, 64, cols), axis=0)
        rows = 64
    if rows == 64:
        x = op(x.reshape(8, 8, cols), axis=0)
    return op(x, axis=0, keepdims=True)


def _fox_prompt_kernel(q_ref, k_ref, v_ref, cr_ref, o_ref,
                       vt_ref, ck0_ref, ck1_ref, st0_ref, st1_ref, pt0_ref, pt1_ref, acc_ref, *, tq, tk):
    st_refs = (st0_ref, st1_ref)
    pt_refs = (pt0_ref, pt1_ref)
    i = pl.program_id(2)
    s_len = k_ref.shape[1]

    @pl.when(i == 0)
    def _():
        vt_ref[...] = v_ref[0].astype(F32).T.astype(BF16)
        ck0_ref[...] = jnp.broadcast_to(cr_ref[0, 0, 0:1, :], (LANES, s_len)).T
        ck1_ref[...] = jnp.broadcast_to(cr_ref[0, 0, 1:2, :], (LANES, s_len)).T

    def q_heads_t(blk):
        qt = q_ref[0, pl.ds(pl.multiple_of(blk * tq, tq), tq), :].astype(F32).T
        row = lax.broadcasted_iota(jnp.int32, (LANES, tq), 0)
        return (jnp.where(row < FOX_HEAD_DIM, qt, 0.0).astype(BF16),
                jnp.where(row < FOX_HEAD_DIM, 0.0, qt).astype(BF16))

    qts = q_heads_t(i)
    q0 = pl.multiple_of(i * tq, tq)
    cq = cr_ref[0, 0, :, pl.ds(q0, tq)]
    ck_refs = (ck0_ref, ck1_ref)

    def stage_a(n, par, qts=qts):
        s = pl.multiple_of(n * tk, tk)
        kb = k_ref[0, pl.ds(s, tk), :]
        for hh in range(2):
            ck = ck_refs[hh][pl.ds(s, tk), :]
            st_refs[par][hh] = _dot(kb, qts[hh]) - jnp.concatenate([ck] * (tq // LANES), axis=1)

    def stage_b(n, par, stats, masked):
        if masked:
            kpos = n * tk + lax.broadcasted_iota(jnp.int32, (tk, tq), 0)
            qpos = q0 + lax.broadcasted_iota(jnp.int32, (tk, tq), 1)
            mask = kpos <= qpos
        out = []
        for hh in range(2):
            m, l = stats[2 * hh:2 * hh + 2]
            t = st_refs[par][hh]
            if masked:
                t = jnp.where(mask, t, NEG_INF)
            cqh = cq[hh:hh + 1, :]
            m_new = jnp.maximum(m, _reduce_rows(t, jnp.max) + cqh)
            alpha = jnp.exp2(m - m_new)
            p = jnp.exp2(t + (cqh - m_new))
            pt_refs[par][hh] = p.astype(BF16)
            out.extend([m_new, alpha * l + _reduce_rows(p, jnp.sum), alpha])
        return tuple(out)

    def stage_c(n, par, alphas):
        s = pl.multiple_of(jnp.maximum(n, 0) * tk, tk)
        for hh in range(2):
            vt = vt_ref[hh * FOX_HEAD_DIM:(hh + 1) * FOX_HEAD_DIM, pl.ds(s, tk)]
            acc_ref[hh] = alphas[hh] * acc_ref[hh] + _dot(vt, pt_refs[par][hh])

    def iteration(n, par, carry):
        m0, l0, al0, m1, l1, al1 = carry
        stage_c(n - 1, 1 - par, (al0, al1))
        new = stage_b(n, par, (m0, l0, m1, l1), False)
        stage_a(n + 1, 1 - par)
        return new

    def finish(par, carry):
        m0, l0, al0, m1, l1, al1 = carry
        stage_c(nfull - 1, 1 - par, (al0, al1))
        _, l0, be0, _, l1, be1 = stage_b(nfull, par, (m0, l0, m1, l1), True)
        stage_a(0, 0, q_heads_t(jnp.minimum(i + 1, pl.num_programs(2) - 1)))
        stage_c(nfull, par, (be0, be1))
        ot = jnp.concatenate([acc_ref[0] / l0, acc_ref[1] / l1], axis=0)
        o_ref[0] = ot.T.astype(o_ref.dtype)

    acc_ref[...] = jnp.zeros(acc_ref.shape, F32)
    pt1_ref[...] = jnp.zeros(pt1_ref.shape, BF16)
    neg = jnp.full((1, tq), NEG_INF, F32)
    zero = jnp.zeros((1, tq), F32)
    one = jnp.ones((1, tq), F32)
    nfull = (i * tq) // tk
    pl.when(i == 0)(lambda: stage_a(0, 0))
    carry = lax.fori_loop(0, nfull // 2, lambda k, c: iteration(2 * k + 1, 1, iteration(2 * k, 0, c)),
                          (neg, zero, one, neg, zero, one))
    odd = nfull % 2 == 1
    carry = lax.cond(odd, lambda c: iteration(nfull - 1, 0, c), lambda c: c, carry)
    pl.when(odd)(lambda: finish(1, carry))
    pl.when(jnp.logical_not(odd))(lambda: finish(0, carry))


def _fox_prompt(q, k, v, c_row, tq, tk):
    b, s, _ = q.shape
    assert s % tk == 0 and tk % tq == 0
    npair = N_FOX_HEADS // 2
    return pl.pallas_call(
        functools.partial(_fox_prompt_kernel, tq=tq, tk=tk),
        out_shape=jax.ShapeDtypeStruct((b, s, FOX_WIDTH), BF16),
        grid=(b, npair, s // tq),
        in_specs=[
            pl.BlockSpec((1, s, LANES), lambda bi, hp, i: (bi, 0, hp)),
            pl.BlockSpec((1, s, LANES), lambda bi, hp, i: (bi, 0, hp)),
            pl.BlockSpec((1, s, LANES), lambda bi, hp, i: (bi, 0, hp)),
            pl.BlockSpec((1, 1, 2, s), lambda bi, hp, i: (bi, hp, 0, 0)),
        ],
        out_specs=pl.BlockSpec((1, tq, LANES), lambda bi, hp, i: (bi, i, hp)),
        scratch_shapes=[pltpu.VMEM((LANES, s), BF16),
                        pltpu.VMEM((s, LANES), F32), pltpu.VMEM((s, LANES), F32),
                        pltpu.VMEM((2, tk, tq), F32), pltpu.VMEM((2, tk, tq), F32),
                        pltpu.VMEM((2, tk, tq), BF16), pltpu.VMEM((2, tk, tq), BF16),
                        pltpu.VMEM((2, FOX_HEAD_DIM, tq), F32)],
        compiler_params=_params(("parallel", "parallel", "arbitrary")), name="fox_prompt")(q, k, v, c_row)


def _fox_sample_kernel(q_ref, ck_ref, cv_ref, nk_ref, nv_ref, cq_ref, crc_ref, crn_ref, o_ref, *state, n):
    j = pl.program_id(1)
    nj = pl.num_programs(1)
    m_refs = state[0:N_FOX_HEADS]
    l_refs = state[N_FOX_HEADS:2 * N_FOX_HEADS]
    acc_refs = state[2 * N_FOX_HEADS:3 * N_FOX_HEADS]

    @pl.when(j == 0)
    def _():
        for hd in range(N_FOX_HEADS):
            m_refs[hd][...] = jnp.full(m_refs[hd].shape, NEG_INF, F32)
            l_refs[hd][...] = jnp.zeros(l_refs[hd].shape, F32)
            acc_refs[hd][...] = jnp.zeros(acc_refs[hd].shape, F32)

    def update(k_of, v_of, cr_ref_, mask, transposed):
        qk = _dot if transposed else _dot_nt
        pv = _dot_nt if transposed else _dot
        ts = []
        for hd in range(N_FOX_HEADS):
            hs = slice(hd * FOX_HEAD_DIM, (hd + 1) * FOX_HEAD_DIM)
            t = qk(q_ref[0, :, hs], k_of(hd)) - cr_ref_[0, hd:hd + 1, :]
            ts.append(t if mask is None else jnp.where(mask, t, NEG_INF))
        ps = []
        for hd in range(N_FOX_HEADS):
            cq = cq_ref[0, :, hd:hd + 1]
            m = m_refs[hd][...]
            m_new = jnp.maximum(m, jnp.max(ts[hd], axis=-1, keepdims=True) + cq)
            alpha = jnp.exp2(m - m_new)
            p = jnp.exp2(ts[hd] + (cq - m_new))
            m_refs[hd][...] = m_new
            l_refs[hd][...] = alpha * l_refs[hd][...] + jnp.sum(p, axis=-1, keepdims=True)
            ps.append((alpha, p.astype(BF16)))
        for hd in range(N_FOX_HEADS):
            alpha, p = ps[hd]
            acc_refs[hd][...] = alpha * acc_refs[hd][...] + pv(p, v_of(hd))

    update(lambda hd: ck_ref[0, hd].astype(BF16), lambda hd: cv_ref[0, hd].astype(BF16), crc_ref, None, True)

    @pl.when(j == nj - 1)
    def _():
        r = lax.broadcasted_iota(jnp.int32, (n, n), 0)
        c = lax.broadcasted_iota(jnp.int32, (n, n), 1)
        head = lambda ref: (lambda hd: ref[0, :, hd * FOX_HEAD_DIM:(hd + 1) * FOX_HEAD_DIM])
        update(head(nk_ref), head(nv_ref), crn_ref, c <= r, False)
        for hd in range(N_FOX_HEADS):
            hs = slice(hd * FOX_HEAD_DIM, (hd + 1) * FOX_HEAD_DIM)
            o_ref[0, :, hs] = (acc_refs[hd][...] / l_refs[hd][...]).astype(o_ref.dtype)


def _fox_sample(q, cache_k, cache_v, k_new, v_new, c_q, c_row_cache, c_row_new, tk):
    b, n, _ = q.shape
    past = cache_k.shape[3]
    assert past % tk == 0
    cache_spec = pl.BlockSpec((1, N_FOX_HEADS, FOX_HEAD_DIM, tk), lambda bi, j: (bi, 0, 0, j))
    return pl.pallas_call(
        functools.partial(_fox_sample_kernel, n=n),
        out_shape=jax.ShapeDtypeStruct((b, n, FOX_WIDTH), BF16),
        grid=(b, past // tk),
        in_specs=[
            pl.BlockSpec((1, n, FOX_WIDTH), lambda bi, j: (bi, 0, 0)),
            cache_spec,
            cache_spec,
            pl.BlockSpec((1, n, FOX_WIDTH), lambda bi, j: (bi, 0, 0)),
            pl.BlockSpec((1, n, FOX_WIDTH), lambda bi, j: (bi, 0, 0)),
            pl.BlockSpec((1, n, N_FOX_HEADS), lambda bi, j: (bi, 0, 0)),
            pl.BlockSpec((1, N_FOX_HEADS, tk), lambda bi, j: (bi, 0, j)),
            pl.BlockSpec((1, N_FOX_HEADS, n), lambda bi, j: (bi, 0, 0)),
        ],
        out_specs=pl.BlockSpec((1, n, FOX_WIDTH), lambda bi, j: (bi, 0, 0)),
        scratch_shapes=([pltpu.VMEM((n, 1), F32)] * (2 * N_FOX_HEADS)
                        + [pltpu.VMEM((n, FOX_HEAD_DIM), F32)] * N_FOX_HEADS),
        compiler_params=_params(("parallel", "arbitrary")), name="fox_sample")(
            q, cache_k, cache_v, k_new, v_new, c_q, c_row_cache, c_row_new)


def _ssm_mats(p):
    f32 = F32
    a_re, a_im = p['ssm_a_re'].astype(f32), p['ssm_a_im'].astype(f32)
    b_re, b_im = p['ssm_b_re'].astype(f32), p['ssm_b_im'].astype(f32)
    c_re, c_im = p['ssm_c_re'].astype(f32), p['ssm_c_im'].astype(f32)
    dt = jnp.exp(p['ssm_log_dt'].astype(f32))[:, None]
    mag = jnp.exp(dt * a_re)
    ab_re = mag * jnp.cos(dt * a_im)
    ab_im = mag * jnp.sin(dt * a_im)
    den = a_re * a_re + a_im * a_im
    nr, ni = ab_re - 1.0, ab_im
    coef_re = (nr * a_re + ni * a_im) / den
    coef_im = (ni * a_re - nr * a_im) / den
    bb_re = coef_re[..., None] * b_re - coef_im[..., None] * b_im
    bb_im = coef_re[..., None] * b_im + coef_im[..., None] * b_re
    pr, pi = [jnp.ones_like(ab_re)], [jnp.zeros_like(ab_im)]
    for _ in range(SSM_CHUNK):
        pr.append(pr[-1] * ab_re - pi[-1] * ab_im)
        pi.append(pr[-2] * ab_im + pi[-1] * ab_re)
    pw_re, pw_im = jnp.stack(pr), jnp.stack(pi)
    T = SSM_CHUNK
    w_re = pw_re[..., None] * bb_re[None] - pw_im[..., None] * bb_im[None]
    w_im = pw_re[..., None] * bb_im[None] + pw_im[..., None] * bb_re[None]
    kk = (jnp.einsum('gop,kgpi->kgoi', c_re, w_re[:T], precision='highest')
          - jnp.einsum('gop,kgpi->kgoi', c_im, w_im[:T], precision='highest'))
    nq = N_SSM_GROUPS // SSM_GPB

    def group_diag(m):
        rows, c = m.shape[-2:]
        m = jnp.tile(m, (1,) * (m.ndim - 1) + (SSM_GPB,))
        same = (jnp.arange(rows) // (rows // SSM_GPB))[:, None] == (jnp.arange(SSM_GPB * c) // c)[None, :]
        return jnp.where(same, m, 0.0)

    def lane_diag(m):
        lead = m.shape[:-3]
        i, c = m.shape[-2:]
        return group_diag(m.reshape(lead + (nq, SSM_GPB * i, c)))

    ktau = lane_diag(jnp.swapaxes(kk, -1, -2))
    ktau = jnp.concatenate([jnp.zeros_like(ktau[:1]), ktau], axis=0)
    units = []
    for dlag in range(T // 2 - 1, -1, -1):
        top = jnp.concatenate([ktau[2 * dlag + 1], ktau[2 * dlag + 2]], axis=-1)
        bot = jnp.concatenate([ktau[2 * dlag], ktau[2 * dlag + 1]], axis=-1)
        units.append(jnp.concatenate([top, bot], axis=-2))
    kstack = jnp.concatenate(units, axis=-2).astype(BF16)
    rev = T - 1 - jnp.arange(T)
    def local_rows(w):
        w = jnp.transpose(w[rev], (1, 0, 3, 2)).reshape(nq, SSM_GPB, T, SSM_GROUP, SSM_STATE)
        return group_diag(jnp.transpose(w, (0, 2, 1, 3, 4)).reshape(nq, T, LANES, SSM_STATE))

    m_all = jnp.concatenate([local_rows(w_re), local_rows(w_im)], axis=-1)
    m_all = m_all.reshape(nq, T * LANES, 2 * SSM_GPB * SSM_STATE)
    m_hi, m_lo = _split_bf16(m_all)
    ar, ai = pw_re[1:], pw_im[1:]
    n_re = (c_re[None] * ar[:, :, None, :] - c_im[None] * ai[:, :, None, :])
    n_im = -(c_re[None] * ai[:, :, None, :] + c_im[None] * ar[:, :, None, :])

    def state_rows(n):
        n = jnp.transpose(n, (1, 3, 0, 2)).reshape(nq, SSM_GPB * SSM_STATE, T, SSM_GROUP)
        n = jnp.tile(jnp.transpose(n, (0, 2, 1, 3)), (1, 1, 1, SSM_GPB))
        same = (jnp.arange(SSM_GPB * SSM_STATE) // SSM_STATE)[:, None] == (jnp.arange(LANES) // SSM_GROUP)[None, :]
        return jnp.where(same, n, 0.0)

    n_all = jnp.concatenate([state_rows(n_re), state_rows(n_im)], axis=2).astype(BF16)
    return dict(kstack=kstack, m_hi=m_hi, m_lo=m_lo, n_all=n_all,
                a16_re=pw_re[T].reshape(STATE_TILE), a16_im=pw_im[T].reshape(STATE_TILE),
                d=p['ssm_d'].astype(f32).reshape(1, SSM_WIDTH))


def _chunk_tokens(u_ref, rows):
    return [u_ref[pl.ds(t, rows, stride=SSM_CHUNK), :] for t in range(SSM_CHUNK)]


def _ssm_local_kernel(u_ref, mh_ref, ml_ref, hre_ref, him_ref):
    rows = hre_ref.shape[0]
    parts = [_split_bf16(ut) for ut in _chunk_tokens(u_ref, rows)]
    x_hi = jnp.concatenate([h for h, _ in parts], axis=1)
    x_lo = jnp.concatenate([l for _, l in parts], axis=1)
    h = _dot(x_hi, mh_ref[0]) + _dot(x_hi, ml_ref[0]) + _dot(x_lo, mh_ref[0])
    half = SSM_GPB * SSM_STATE
    hre_ref[...] = h[:, 0:half]
    him_ref[...] = h[:, half:2 * half]


def _ssm_local(u2d, mats, rows):
    n = u2d.shape[0]
    r = n // SSM_CHUNK
    nq = N_SSM_GROUPS // SSM_GPB
    half = SSM_GPB * SSM_STATE
    mspec = pl.BlockSpec((1, SSM_CHUNK * LANES, 2 * half), lambda q, i: (q, 0, 0))
    ospec = pl.BlockSpec((rows, half), lambda q, i: (i, q))
    return pl.pallas_call(
        _ssm_local_kernel,
        out_shape=(jax.ShapeDtypeStruct((r, N_SSM_GROUPS * SSM_STATE), F32),) * 2,
        grid=(nq, r // rows),
        in_specs=[pl.BlockSpec((rows * SSM_CHUNK, LANES), lambda q, i: (i, q)), mspec, mspec],
        out_specs=(ospec, ospec),
        compiler_params=_params(("parallel", "parallel")), name="ssm_local")(u2d, mats['m_hi'], mats['m_lo'])


def _ssm_scan_kernel(lre_ref, lim_ref, are_ref, aim_ref, h0re_ref, h0im_ref,
                     pre_ref, pim_ref, fre_ref, fim_ref):
    nb, nchunk = lre_ref.shape[:2]
    ar, ai = are_ref[...], aim_ref[...]

    def body(c, carry):
        out = []
        for bi in range(nb):
            hr, hi = carry[2 * bi:2 * bi + 2]
            pre_ref[bi, c] = hr
            pim_ref[bi, c] = hi
            out.extend([ar * hr - ai * hi + lre_ref[bi, c], ar * hi + ai * hr + lim_ref[bi, c]])
        return tuple(out)

    init = []
    for bi in range(nb):
        init.extend([h0re_ref[bi], h0im_ref[bi]])
    final = lax.fori_loop(0, nchunk, body, tuple(init))
    for bi in range(nb):
        fre_ref[bi] = final[2 * bi]
        fim_ref[bi] = final[2 * bi + 1]


def _ssm_scan(hloc_re, hloc_im, mats, h0_re, h0_im):
    b, nchunk = hloc_re.shape[:2]
    nb = _pick_tile(b, max(1, SCAN_CHUNK_ROWS // nchunk))
    big = pl.BlockSpec((nb, nchunk) + STATE_TILE, lambda i: (i, 0, 0, 0))
    small = pl.BlockSpec((nb,) + STATE_TILE, lambda i: (i, 0, 0))
    return pl.pallas_call(
        _ssm_scan_kernel,
        out_shape=(jax.ShapeDtypeStruct(hloc_re.shape, F32),) * 2 + (jax.ShapeDtypeStruct((b,) + STATE_TILE, F32),) * 2,
        grid=(b // nb,),
        in_specs=[big, big, _full(STATE_TILE), _full(STATE_TILE), small, small],
        out_specs=(big, big, small, small),
        compiler_params=_params(("parallel",)), name="ssm_scan")(
            hloc_re, hloc_im, mats['a16_re'], mats['a16_im'], h0_re, h0_im)


def _gelu_tanh(y):
    return 0.5 * y * (1.0 + jnp.tanh(math.sqrt(2.0 / math.pi) * (y + 0.044715 * (y * y * y))))


def _ssm_out_kernel(u_ref, k_ref, pre_ref, pim_ref, n_ref, d_ref, y_ref, ysc_ref):
    rows = pre_ref.shape[0]
    us = _chunk_tokens(u_ref, rows)
    x = jnp.concatenate([ut.astype(BF16) for ut in us], axis=1)
    hp = jnp.concatenate([pre_ref[...], pim_ref[...]], axis=1).astype(BF16)
    unit = 2 * LANES
    nunit = SSM_CHUNK // 2
    for j in range(nunit):
        n_unit = jnp.concatenate([n_ref[0, 2 * j], n_ref[0, 2 * j + 1]], axis=1)
        yj = _dot(x[:, 0:unit * (j + 1)], k_ref[0, unit * (nunit - 1 - j):, :]) + _dot(hp, n_unit)
        for t2 in range(2):
            t = 2 * j + t2
            y = yj[:, t2 * LANES:(t2 + 1) * LANES] + d_ref[...] * us[t]
            ysc_ref[pl.ds(t, rows, stride=SSM_CHUNK), :] = _gelu_tanh(y)
    y_ref[...] = ysc_ref[...].astype(y_ref.dtype)


def _ssm_out(u2d, hprev_re, hprev_im, mats, rows):
    n = u2d.shape[0]
    r = n // SSM_CHUNK
    nq = N_SSM_GROUPS // SSM_GPB
    half = SSM_GPB * SSM_STATE
    uspec = pl.BlockSpec((rows * SSM_CHUNK, LANES), lambda q, i: (i, q))
    hspec = pl.BlockSpec((rows, half), lambda q, i: (i, q))
    return pl.pallas_call(
        _ssm_out_kernel,
        out_shape=jax.ShapeDtypeStruct((n, SSM_WIDTH), BF16),
        grid=(nq, r // rows),
        in_specs=[uspec, pl.BlockSpec((1, SSM_CHUNK * LANES, 2 * LANES), lambda q, i: (q, 0, 0)), hspec, hspec,
                  pl.BlockSpec((1, SSM_CHUNK, 2 * half, LANES), lambda q, i: (q, 0, 0, 0)),
                  pl.BlockSpec((1, LANES), lambda q, i: (0, q))],
        out_specs=uspec,
        scratch_shapes=[pltpu.VMEM((rows * SSM_CHUNK, LANES), F32)],
        compiler_params=_params(("parallel", "parallel")), name="ssm_out")(
            u2d, mats['kstack'], hprev_re, hprev_im, mats['n_all'], mats['d'])


def _ssm(u2d, b, h0_re, h0_im, mats):
    n = u2d.shape[0]
    nchunk = n // b // SSM_CHUNK
    r = b * nchunk
    rows = _pick_tile(r, TILE_SSM_ROWS)
    hloc_re, hloc_im = _ssm_local(u2d, mats, rows)
    shp = (b, nchunk) + STATE_TILE
    hprev_re, hprev_im, f_re, f_im = _ssm_scan(hloc_re.reshape(shp), hloc_im.reshape(shp), mats,
                                               h0_re.reshape((b,) + STATE_TILE), h0_im.reshape((b,) + STATE_TILE))
    y = _ssm_out(u2d, hprev_re.reshape(r, -1), hprev_im.reshape(r, -1), mats, rows)
    return y, f_re.reshape(b, N_SSM_GROUPS, SSM_STATE), f_im.reshape(b, N_SSM_GROUPS, SSM_STATE)


def _merge_kernel(x_ref, of_ref, ys_ref, qm_ref, gate_ref, mk_ref, mv_ref,
                  wglu_ref, wbf_ref, wbs_ref, wbm_ref, wo_ref, nf_ref, wr_ref,
                  x1_ref, h2_ref, r_ref):
    tm = x_ref.shape[0]
    nb = mk_ref.shape[0]
    rows = tm // nb
    om = []
    for hd in range(N_MEM_HEADS):
        sl = slice(hd * MEM_HEAD_DIM, (hd + 1) * MEM_HEAD_DIM)
        head_rows = pl.ds(hd, N_MEM, stride=N_MEM_HEADS)
        per_batch = []
        for bi in range(nb):
            kh = mk_ref[bi, head_rows, :].astype(BF16)
            vh = mv_ref[bi, head_rows, :].astype(BF16)
            sc = _dot_nt(qm_ref[bi * rows:(bi + 1) * rows, sl], kh)
            p = jnp.exp(sc - jnp.max(sc, axis=-1, keepdims=True))
            per_batch.append(_dot(p.astype(BF16), vh) / jnp.sum(p, axis=-1, keepdims=True))
        om.append(per_batch[0] if nb == 1 else jnp.concatenate(per_batch, axis=0))
    o_mem = jnp.concatenate(om, axis=-1).astype(BF16)
    z = _dot(ys_ref[...], wglu_ref[...])
    y_ssm = (z[:, 0:SSM_WIDTH] * jax.nn.sigmoid(z[:, SSM_WIDTH:2 * SSM_WIDTH])).astype(BF16)
    g = lambda c: gate_ref[:, c * D_MODEL:(c + 1) * D_MODEL].astype(F32)
    merged = (g(0) * _dot(of_ref[...], wbf_ref[...]) + g(1) * _dot(y_ssm, wbs_ref[...])
              + g(2) * _dot(o_mem, wbm_ref[...]))
    x1 = x_ref[...] + _dot(merged.astype(BF16), wo_ref[...])
    x1_ref[...] = x1
    h2 = x1 * lax.rsqrt(jnp.mean(x1 * x1, axis=-1, keepdims=True) + RMS_EPS) * nf_ref[...]
    h2_ref[...] = h2.astype(BF16)
    h2_hi, h2_lo = _split_bf16(h2)
    hw = _dot(h2_hi, wr_ref[...])
    logits = hw[:, 0:LANES] + hw[:, LANES:2 * LANES] + _dot(h2_lo, wr_ref[:, 0:LANES])
    lane = lax.broadcasted_iota(jnp.int32, (tm, LANES), 1)
    big = jnp.int32(LANES)
    is_grp = (lane >= N_EXPERTS) & (lane < N_EXPERTS + N_EXPERT_GROUPS)
    gl = jnp.where(is_grp, logits, NEG_INF)
    gmax = jnp.max(gl, axis=-1, keepdims=True)
    grp = jnp.min(jnp.where(is_grp & (gl == gmax), lane, big), axis=-1, keepdims=True) - N_EXPERTS
    g_w = 1.0 / jnp.sum(jnp.where(is_grp, jnp.exp(gl - gmax), 0.0), axis=-1, keepdims=True)
    in_grp = (lane >= grp * EXPERTS_PER_GROUP) & (lane < (grp + 1) * EXPERTS_PER_GROUP)
    e1 = jnp.where(in_grp, logits, NEG_INF)
    m1 = jnp.max(e1, axis=-1, keepdims=True)
    i1 = jnp.min(jnp.where(in_grp & (e1 == m1), lane, big), axis=-1, keepdims=True)
    rest = in_grp & (lane != i1)
    e2 = jnp.where(rest, logits, NEG_INF)
    m2 = jnp.max(e2, axis=-1, keepdims=True)
    i2 = jnp.min(jnp.where(rest & (e2 == m2), lane, big), axis=-1, keepdims=True)
    ex = jnp.exp(m2 - m1)
    w1 = g_w / (1.0 + ex)
    w2 = g_w * ex / (1.0 + ex)
    r_ref[...] = jnp.where(lane == i1, w1, jnp.where(lane == i2, w2, jnp.where(lane == GROUP_LANE, grp.astype(F32), 0.0)))


def _merge(x2d, o_fox, ys, q_m, gates, mem_k, mem_v, w, tm, rows_per_batch):
    n = x2d.shape[0]
    assert n % tm == 0 and (rows_per_batch % tm == 0 or tm % rows_per_batch == 0)
    row = lambda width: pl.BlockSpec((tm, width), lambda i: (i, 0))
    if rows_per_batch >= tm:
        per = rows_per_batch // tm
        memspec = pl.BlockSpec((1, N_MEM * N_MEM_HEADS, MEM_HEAD_DIM), lambda i: (i // per, 0, 0))
    else:
        memspec = pl.BlockSpec((tm // rows_per_batch, N_MEM * N_MEM_HEADS, MEM_HEAD_DIM), lambda i: (i, 0, 0))
    ws = [w['w_glu'], w['w_br_fox'], w['w_br_ssm'], w['w_br_mem'], w['w_out'], w['norm_ffn'], w['w_router']]
    return pl.pallas_call(
        _merge_kernel,
        out_shape=(jax.ShapeDtypeStruct((n, D_MODEL), F32), jax.ShapeDtypeStruct((n, D_MODEL), BF16),
                   jax.ShapeDtypeStruct((n, LANES), F32)),
        grid=(n // tm,),
        in_specs=[row(D_MODEL), row(FOX_WIDTH), row(SSM_WIDTH), row(MEM_WIDTH), row(3 * D_MODEL), memspec, memspec]
                 + [_full(a.shape) for a in ws],
        out_specs=(row(D_MODEL), row(D_MODEL), row(LANES)),
        compiler_params=_params(("parallel",)), name="merge")(
            x2d, o_fox, ys, q_m, gates, mem_k, mem_v, *ws)


def _moe_kernel(h_ref, r_ref, x1_ref, tri_ref, wg_ref, wu_ref, wd_ref, o_ref,
                xs_ref, cw_ref, og_ref, acc_ref, rank_ref, count_ref, *, main):
    step = pl.program_id(1)
    steps_per_group = EXPERTS_PER_GROUP // MOE_EPS
    g = step // steps_per_group
    tm = h_ref.shape[0]
    gf = g.astype(F32)
    bounds = [0, main] + list(range(-(-main // MOE_SUB) * MOE_SUB, tm, MOE_SUB)) + [tm]
    bounds = sorted(set(bounds))
    blocks = [(lo, hi - lo, lo > 0) for lo, hi in zip(bounds[:-1], bounds[1:])]

    def guarded(r0, fn):
        if r0 == 0:
            fn()
        else:
            pl.when(r0 < count_ref[0])(fn)

    @pl.when(step == 0)
    def _():
        acc_ref[...] = jnp.zeros(acc_ref.shape, F32)

    @pl.when(step % steps_per_group == 0)
    def _():
        rt = r_ref[...]
        rtt = rt.T
        mrow = rtt[GROUP_LANE:GROUP_LANE + 1, :] == gf
        m8 = jnp.broadcast_to(jnp.where(mrow, 1.0, 0.0), (8, tm))
        rank8 = _dot(m8.astype(BF16), tri_ref[...])
        rank_row = jnp.where(mrow, rank8[0:1, :], -1.0)
        rank_ref[...] = jnp.broadcast_to(jnp.where(mrow, rank8, -1.0).T[:, 0:1], rank_ref.shape)
        count_ref[0] = jnp.sum(jnp.where(mrow, 1, 0))
        hilo = jnp.concatenate(_split_bf16(rt), axis=1)
        for r0, nrows, _ in blocks:
            def compact(r0=r0, nrows=nrows):
                rows = slice(r0, r0 + nrows)
                slot = r0 + lax.broadcasted_iota(jnp.int32, (nrows, tm), 0)
                perm = jnp.where(rank_row == slot.astype(F32), 1.0, 0.0).astype(BF16)
                xs_ref[rows, :] = _dot(perm, h_ref[...]).astype(BF16)
                cw = _dot(perm, hilo)
                cw_ref[rows, :] = cw[:, 0:LANES] + cw[:, LANES:2 * LANES]
                og_ref[rows, :] = jnp.zeros((nrows, D_MODEL), F32)
            guarded(r0, compact)

    for k in range(MOE_EPS):
        e = step * MOE_EPS + k
        for r0, nrows, _ in blocks:
            def expert(r0=r0, nrows=nrows, k=k, e=e):
                rows = slice(r0, r0 + nrows)
                x = xs_ref[rows, :]
                a = _dot(x, wg_ref[k])
                up = _dot(x, wu_ref[k])
                lane = lax.broadcasted_iota(jnp.int32, (nrows, LANES), 1)
                ce = jnp.sum(jnp.where(lane == e, cw_ref[rows, :], 0.0), axis=-1, keepdims=True)
                act = a * jax.nn.sigmoid(a) * up * ce
                og_ref[rows, :] += _dot(act.astype(BF16), wd_ref[k])
            guarded(r0, expert)

    @pl.when(step % steps_per_group == steps_per_group - 1)
    def _():
        for r0, nrows, _ in blocks:
            def scatter_back(r0=r0, nrows=nrows):
                rows = slice(r0, r0 + nrows)
                slot = r0 + lax.broadcasted_iota(jnp.int32, (tm, nrows), 1)
                back = jnp.where(rank_ref[:, 0:1] == slot.astype(F32), 1.0, 0.0).astype(BF16)
                acc_ref[...] += _dot(back, og_ref[rows, :].astype(BF16))
            guarded(r0, scatter_back)

    @pl.when(step == pl.num_programs(1) - 1)
    def _():
        o_ref[...] = x1_ref[...] + acc_ref[...]


def _moe(h2, route, x1, wg, wu, wd, tm):
    n = h2.shape[0]
    assert n % tm == 0 and tm % MOE_SUB == 0
    main = max(MOE_SUB // 2, (5 * tm // 16) // 64 * 64)
    row = lambda width: pl.BlockSpec((tm, width), lambda i, s: (i, 0))
    r = jnp.arange(tm)
    tri = (r[:, None] < r[None, :]).astype(BF16)
    return pl.pallas_call(
        functools.partial(_moe_kernel, main=main),
        out_shape=jax.ShapeDtypeStruct((n, D_MODEL), F32),
        grid=(n // tm, N_EXPERTS // MOE_EPS),
        in_specs=[row(D_MODEL), row(LANES), row(D_MODEL), pl.BlockSpec((tm, tm), lambda i, s: (0, 0)),
                  pl.BlockSpec((MOE_EPS, D_MODEL, D_EXPERT), lambda i, s: (s, 0, 0)),
                  pl.BlockSpec((MOE_EPS, D_MODEL, D_EXPERT), lambda i, s: (s, 0, 0)),
                  pl.BlockSpec((MOE_EPS, D_EXPERT, D_MODEL), lambda i, s: (s, 0, 0))],
        out_specs=row(D_MODEL),
        scratch_shapes=[pltpu.VMEM((tm, D_MODEL), BF16), pltpu.VMEM((tm, LANES), F32), pltpu.VMEM((tm, D_MODEL), F32),
                        pltpu.VMEM((tm, D_MODEL), F32), pltpu.VMEM((tm, LANES), F32), pltpu.SMEM((1,), jnp.int32)],
        compiler_params=_params(("parallel", "arbitrary")), name="moe")(h2, route, x1, tri, wg, wu, wd)


def _prep_weights(p):
    w_in = p['w_in'].astype(BF16)
    o = 0
    wqkv = w_in[:, 0:3 * FOX_WIDTH]
    o = 3 * FOX_WIDTH
    wf = jnp.pad(w_in[:, o:o + N_FOX_HEADS], ((0, 0), (0, LANES - N_FOX_HEADS)))
    o += N_FOX_HEADS
    wqm = w_in[:, o:o + MEM_WIDTH]
    o += MEM_WIDTH
    wu = w_in[:, o:o + SSM_WIDTH]
    o += SSM_WIDTH
    wg = w_in[:, o:o + 3 * D_MODEL]
    r = jnp.arange(FOX_WIDTH) // FOX_HEAD_DIM
    bd = (r[:, None] == r[None, :]).astype(BF16)
    w_router = jnp.concatenate(
        [p['w_router_expert'], p['w_router_group'],
         jnp.zeros((D_MODEL, LANES - N_EXPERTS - N_EXPERT_GROUPS), F32)], axis=1)
    w_router = jnp.concatenate(_split_bf16(w_router), axis=1)
    return dict(
        norm_mix=p['norm_mix'].reshape(1, D_MODEL), wqkv=wqkv, wf=wf,
        bf=jnp.pad(p['b_forget'], (0, LANES - N_FOX_HEADS)).reshape(1, LANES),
        wqm=wqm, wu=wu, wg=wg,
        qn_fox=jnp.tile(p['qn_fox'], N_FOX_HEADS).reshape(1, FOX_WIDTH),
        kn_fox=jnp.tile(p['kn_fox'], N_FOX_HEADS).reshape(1, FOX_WIDTH),
        qn_mem=p['qn_mem'].reshape(1, MEM_HEAD_DIM), bd=bd,
        w_glu=p['w_glu'].astype(BF16), w_br_fox=p['w_br_fox'].astype(BF16),
        w_br_ssm=p['w_br_ssm'].astype(BF16), w_br_mem=p['w_br_mem'].astype(BF16),
        w_out=p['w_out'].astype(BF16), norm_ffn=p['norm_ffn'].reshape(1, D_MODEL), w_router=w_router,
        moe_wg=p['moe_w_gate'].astype(BF16), moe_wu=p['moe_w_up'].astype(BF16),
        moe_wd=p['moe_w_down'].astype(BF16))


def _pick_tile(n, target):
    t = min(n, target)
    while n % t:
        t //= 2
    return t


def _group(x, w, mats, mem_k, mem_v, h0_re, h0_im, cache):
    b, s, _ = x.shape
    n = b * s
    x2d = x.reshape(n, D_MODEL)
    prompt = cache is None
    q, kb, vb, k_out, v_out, lf_t, q_m, u, gates = _inproj(x2d, w, _pick_tile(s if prompt else n, TILE_INPROJ), s, prompt)
    q3 = q.reshape(b, s, FOX_WIDTH)
    k3 = kb.reshape(b, s, FOX_WIDTH)
    v3 = vb.reshape(b, s, FOX_WIDTH)
    lf_rows = lf_t.reshape(N_FOX_HEADS, b, s).transpose(1, 0, 2)
    lf3 = lf_rows.transpose(0, 2, 1)
    npair = N_FOX_HEADS // 2
    if prompt:
        c_row = LOG2E * _cumsum_rows(lf_rows.reshape(b * N_FOX_HEADS, s)).reshape(b, npair, 2, s)
        o_fox = _fox_prompt(q3, k3, v3, c_row, _pick_tile(s, TILE_FOX_Q), _pick_tile(s, TILE_FOX_K))
        unt = lambda a: a.reshape(b, N_FOX_HEADS, FOX_HEAD_DIM, s).transpose(0, 3, 1, 2)
        k4, v4 = unt(k_out), unt(v_out)
    else:
        cache_k, cache_v, cache_logf = cache
        past = cache_k.shape[1]
        lf_all = jnp.concatenate([cache_logf.astype(F32).transpose(0, 2, 1), lf_rows], axis=2)
        c_row = LOG2E * _cumsum_rows(lf_all.reshape(b * N_FOX_HEADS, past + s)).reshape(b, N_FOX_HEADS, past + s)
        o_fox = _fox_sample(q3, cache_k.transpose(0, 2, 3, 1), cache_v.transpose(0, 2, 3, 1), k3, v3,
                            c_row[:, :, past:].transpose(0, 2, 1), c_row[:, :, :past], c_row[:, :, past:],
                            _pick_tile(past, TILE_SAMPLE_K))
        k4 = k_out.reshape(b, s, N_FOX_HEADS, FOX_HEAD_DIM)
        v4 = v_out.reshape(b, s, N_FOX_HEADS, FOX_HEAD_DIM)
    ys, f_re, f_im = _ssm(u, b, h0_re, h0_im, mats)
    tm = _pick_tile(n, TILE_MERGE) if TILE_MERGE % s == 0 else _pick_tile(s, TILE_MERGE)
    x1, h2, route = _merge(x2d, o_fox.reshape(n, FOX_WIDTH), ys, q_m, gates, mem_k, mem_v, w, tm, s)
    y = _moe(h2, route, x1, w['moe_wg'], w['moe_wu'], w['moe_wd'], _pick_tile(n, TILE_MOE))
    return y.reshape(b, s, D_MODEL), k4, v4, lf3, f_re, f_im


def kernel(x_prompt, x_sample, mem_prompt, cache_fox_k, cache_fox_v, cache_fox_logf, state_ssm_re, state_ssm_im,
           cache_mem_k, cache_mem_v, norm_mix, w_in, b_forget, qn_fox, kn_fox, qn_mem, kn_mem, norm_mem, w_mem_kv,
           ssm_a_re, ssm_a_im, ssm_log_dt, ssm_b_re, ssm_b_im, ssm_c_re, ssm_c_im, ssm_d, w_glu, w_br_fox,
           w_br_ssm, w_br_mem, w_out, norm_ffn, w_router_group, w_router_expert, moe_w_gate, moe_w_up,
           moe_w_down):
    depth = norm_mix.shape[0]
    assert depth == 1
    l = 0
    p = dict(norm_mix=norm_mix[l], w_in=w_in[l], b_forget=b_forget[l], qn_fox=qn_fox[l], kn_fox=kn_fox[l],
             qn_mem=qn_mem[l], ssm_a_re=ssm_a_re[l], ssm_a_im=ssm_a_im[l], ssm_log_dt=ssm_log_dt[l],
             ssm_b_re=ssm_b_re[l], ssm_b_im=ssm_b_im[l], ssm_c_re=ssm_c_re[l], ssm_c_im=ssm_c_im[l],
             ssm_d=ssm_d[l], w_glu=w_glu[l], w_br_fox=w_br_fox[l], w_br_ssm=w_br_ssm[l], w_br_mem=w_br_mem[l],
             w_out=w_out[l], norm_ffn=norm_ffn[l], w_router_group=w_router_group[l],
             w_router_expert=w_router_expert[l], moe_w_gate=moe_w_gate[l], moe_w_up=moe_w_up[l],
             moe_w_down=moe_w_down[l])
    w = _prep_weights(p)
    mats = _ssm_mats(p)
    bp, sp, _ = x_prompt.shape
    bs, ss, _ = x_sample.shape

    mk, mv = _memkv(mem_prompt.reshape(bp * N_MEM, D_MODEL), norm_mem[l].reshape(1, D_MODEL),
                    w_mem_kv[l].astype(BF16), kn_mem[l].reshape(1, MEM_HEAD_DIM), _pick_tile(bp * N_MEM, TILE_MEMKV))
    mem_rows = lambda a, b: a.reshape(b, N_MEM * N_MEM_HEADS, MEM_HEAD_DIM)
    mk = mem_rows(mk, bp)
    mv = mem_rows(mv, bp)
    zeros = jnp.zeros((bp, N_SSM_GROUPS, SSM_STATE), F32)
    yp, pk, pv, plf, pre, pim = _group(x_prompt, w, mats, mk, mv, zeros, zeros, None)
    cache = (cache_fox_k[l], cache_fox_v[l], cache_fox_logf[l])
    ys, sk, sv, slf, sre, sim = _group(
        x_sample, w, mats, mem_rows(cache_mem_k[l], bs), mem_rows(cache_mem_v[l], bs),
        state_ssm_re[l].astype(F32), state_ssm_im[l].astype(F32), cache)
    st = lambda a: a[None]
    return (yp, ys, st(pk), st(pv), st(plf), st(pre), st(pim),
            st(mk.reshape(bp, N_MEM, N_MEM_HEADS, MEM_HEAD_DIM)), st(mv.reshape(bp, N_MEM, N_MEM_HEADS, MEM_HEAD_DIM)),
            st(sk), st(sv), st(slf), st(sre), st(sim))
```

```python
import functools
import math

import jax
import jax.numpy as jnp
from jax import lax
from jax.experimental import pallas as pl
from jax.experimental.pallas import tpu as pltpu

F32 = jnp.float32
BF16 = jnp.bfloat16

D_MODEL = 1024
N_FOX_HEADS = 8
FOX_HEAD_DIM = 64
FOX_WIDTH = N_FOX_HEADS * FOX_HEAD_DIM
N_MEM = 256
N_MEM_HEADS = 4
MEM_HEAD_DIM = 128
MEM_WIDTH = N_MEM_HEADS * MEM_HEAD_DIM
SSM_GROUP = 16
SSM_WIDTH = 512
N_SSM_GROUPS = SSM_WIDTH // SSM_GROUP
SSM_STATE = 64
N_EXPERT_GROUPS = 4
EXPERTS_PER_GROUP = 8
N_EXPERTS = N_EXPERT_GROUPS * EXPERTS_PER_GROUP
D_EXPERT = 256
RMS_EPS = 1e-6
NEG_INF = -1e30
LOG2E = 1.4426950408889634

LANES = 128
SSM_CHUNK = 16
SSM_GPB = LANES // SSM_GROUP
GROUP_LANE = N_EXPERTS
MOE_SUB = 128
MOE_EPS = 4
VMEM_LIMIT = 56 * 1024 * 1024
STATE_TILE = (8, N_SSM_GROUPS * SSM_STATE // 8)

TILE_INPROJ = 512
TILE_MEMKV = 512
TILE_FOX_Q = 512
TILE_FOX_K = 512
TILE_SAMPLE_K = 4096
TILE_SSM_ROWS = 256
SCAN_CHUNK_ROWS = 512
TILE_MERGE = 512
TILE_MOE = 1024


def _dot(a, b):
    return jnp.dot(a, b, preferred_element_type=F32)


def _dot_nt(a, b):
    return lax.dot_general(a, b, (((1,), (1,)), ((), ())), preferred_element_type=F32)


def _dot_exact(a, b):
    return jnp.dot(a, b, preferred_element_type=F32, precision=lax.Precision.HIGHEST)


def _split_bf16(x):
    hi = x.astype(BF16)
    lo = (x - hi.astype(F32)).astype(BF16)
    return hi, lo


def _params(sem):
    return pltpu.CompilerParams(dimension_semantics=sem, vmem_limit_bytes=VMEM_LIMIT)


def _full(shape):
    n = len(shape)
    return pl.BlockSpec(shape, lambda *_: (0,) * n)


def _inproj_kernel(x_ref, g_ref, wqkv_ref, wf_ref, bf_ref, wqm_ref, wu_ref, wg_ref,
                   qn_ref, kn_ref, qmn_ref, bd_ref,
                   q_ref, kb_ref, vb_ref, k_ref, v_ref, lf_ref, qm_ref, u_ref, gate_ref, *, kv_transposed):
    x = x_ref[...]
    h = x * lax.rsqrt(jnp.mean(x * x, axis=-1, keepdims=True) + RMS_EPS) * g_ref[...]
    hb = h.astype(BF16)

    def head_norm(z, gain):
        ss = _dot((z * z).astype(BF16), bd_ref[...])
        return z * lax.rsqrt(ss * (1.0 / FOX_HEAD_DIM) + RMS_EPS) * gain

    zq = _dot(hb, wqkv_ref[:, 0:FOX_WIDTH])
    q_ref[...] = (head_norm(zq, qn_ref[...]) * (LOG2E * FOX_HEAD_DIM ** -0.5)).astype(BF16)
    zk = _dot(hb, wqkv_ref[:, FOX_WIDTH:2 * FOX_WIDTH])
    kn = head_norm(zk, kn_ref[...])
    zv = _dot(hb, wqkv_ref[:, 2 * FOX_WIDTH:3 * FOX_WIDTH])
    kb_ref[...] = kn.astype(BF16)
    vb_ref[...] = zv.astype(BF16)
    tm = x_ref.shape[0]
    if kv_transposed:
        k_ref[0] = kn.T
        v_ref[0] = zv.T
    else:
        for hd in range(N_FOX_HEADS):
            hs = slice(hd * FOX_HEAD_DIM, (hd + 1) * FOX_HEAD_DIM)
            rows = pl.ds(hd, tm, stride=N_FOX_HEADS)
            k_ref[rows, :] = kn[:, hs]
            v_ref[rows, :] = zv[:, hs]

    zf = (_dot(hb, wf_ref[...]) + bf_ref[...]).T[0:N_FOX_HEADS, :]
    lf_ref[...] = jnp.minimum(zf, 0.0) - jnp.log1p(jnp.exp(-jnp.abs(zf)))

    zm = _dot(hb, wqm_ref[...])
    for hd in range(N_MEM_HEADS):
        sl = slice(hd * MEM_HEAD_DIM, (hd + 1) * MEM_HEAD_DIM)
        zh = zm[:, sl]
        ms = jnp.mean(zh * zh, axis=-1, keepdims=True)
        qm_ref[:, sl] = (zh * lax.rsqrt(ms + RMS_EPS) * qmn_ref[...] * (MEM_HEAD_DIM ** -0.5)).astype(BF16)

    u_ref[...] = _dot(hb, wu_ref[...])
    for c in range(3):
        sl = slice(c * D_MODEL, (c + 1) * D_MODEL)
        gate_ref[:, sl] = (0.5 * jnp.tanh(0.5 * _dot(hb, wg_ref[:, sl])) + 0.5).astype(BF16)


def _inproj(x2d, w, tm, seq, kv_transposed):
    n = x2d.shape[0]
    assert n % tm == 0
    row = lambda width: pl.BlockSpec((tm, width), lambda i: (i, 0))
    if kv_transposed:
        assert seq % tm == 0
        per = seq // tm
        kv_shape = jax.ShapeDtypeStruct((n // seq, FOX_WIDTH, seq), F32)
        heads = pl.BlockSpec((1, FOX_WIDTH, tm), lambda i: (i // per, 0, i % per))
    else:
        kv_shape = jax.ShapeDtypeStruct((n * N_FOX_HEADS, FOX_HEAD_DIM), F32)
        heads = pl.BlockSpec((tm * N_FOX_HEADS, FOX_HEAD_DIM), lambda i: (i, 0))
    ins = [x2d, w['norm_mix'], w['wqkv'], w['wf'], w['bf'], w['wqm'], w['wu'], w['wg'],
           w['qn_fox'], w['kn_fox'], w['qn_mem'], w['bd']]
    in_specs = [row(D_MODEL)] + [_full(a.shape) for a in ins[1:]]
    out_shape = (
        jax.ShapeDtypeStruct((n, FOX_WIDTH), BF16),
        jax.ShapeDtypeStruct((n, FOX_WIDTH), BF16),
        jax.ShapeDtypeStruct((n, FOX_WIDTH), BF16),
        kv_shape,
        kv_shape,
        jax.ShapeDtypeStruct((N_FOX_HEADS, n), F32),
        jax.ShapeDtypeStruct((n, MEM_WIDTH), BF16),
        jax.ShapeDtypeStruct((n, SSM_WIDTH), F32),
        jax.ShapeDtypeStruct((n, 3 * D_MODEL), BF16),
    )
    out_specs = (row(FOX_WIDTH), row(FOX_WIDTH), row(FOX_WIDTH), heads, heads,
                 pl.BlockSpec((N_FOX_HEADS, tm), lambda i: (0, i)),
                 row(MEM_WIDTH), row(SSM_WIDTH), row(3 * D_MODEL))
    return pl.pallas_call(
        functools.partial(_inproj_kernel, kv_transposed=kv_transposed),
        out_shape=out_shape, grid=(n // tm,), in_specs=in_specs, out_specs=out_specs,
        compiler_params=_params(("parallel",)), name="inproj")(*ins)


def _memkv_kernel(x_ref, g_ref, w_ref, kn_ref, k_ref, v_ref):
    x = x_ref[...]
    h = x * lax.rsqrt(jnp.mean(x * x, axis=-1, keepdims=True) + RMS_EPS) * g_ref[...]
    hb = h.astype(BF16)
    tm = x_ref.shape[0]
    zk = _dot(hb, w_ref[:, 0:MEM_WIDTH])
    zv = _dot(hb, w_ref[:, MEM_WIDTH:2 * MEM_WIDTH])
    for hd in range(N_MEM_HEADS):
        sl = slice(hd * MEM_HEAD_DIM, (hd + 1) * MEM_HEAD_DIM)
        rows = pl.ds(hd, tm, stride=N_MEM_HEADS)
        zh = zk[:, sl]
        ms = jnp.mean(zh * zh, axis=-1, keepdims=True)
        k_ref[rows, :] = zh * lax.rsqrt(ms + RMS_EPS) * kn_ref[...]
        v_ref[rows, :] = zv[:, sl]


def _memkv(mem2d, norm_mem, w_kv, kn_mem, tm):
    n = mem2d.shape[0]
    out = jax.ShapeDtypeStruct((n * N_MEM_HEADS, MEM_HEAD_DIM), F32)
    ospec = pl.BlockSpec((tm * N_MEM_HEADS, MEM_HEAD_DIM), lambda i: (i, 0))
    return pl.pallas_call(
        _memkv_kernel,
        out_shape=(out, out),
        grid=(n // tm,),
        in_specs=[pl.BlockSpec((tm, D_MODEL), lambda i: (i, 0)), _full(norm_mem.shape), _full(w_kv.shape),
                  _full(kn_mem.shape)],
        out_specs=(ospec, ospec),
        compiler_params=_params(("parallel",)), name="memkv")(mem2d, norm_mem, w_kv, kn_mem)


CUMSUM_BLOCK = 256


def _cumsum_kernel(x_ref, o_ref):
    nblk = x_ref.shape[1] // CUMSUM_BLOCK
    r = lax.broadcasted_iota(jnp.int32, (CUMSUM_BLOCK, CUMSUM_BLOCK), 0)
    c = lax.broadcasted_iota(jnp.int32, (CUMSUM_BLOCK, CUMSUM_BLOCK), 1)
    tri = (r <= c).astype(F32)
    carry = jnp.zeros((x_ref.shape[0], 1), F32)
    for j in range(nblk):
        sl = slice(j * CUMSUM_BLOCK, (j + 1) * CUMSUM_BLOCK)
        cs = _dot_exact(x_ref[:, sl], tri) + carry
        o_ref[:, sl] = cs
        carry = cs[:, CUMSUM_BLOCK - 1:CUMSUM_BLOCK]


def _cumsum_rows(x):
    rows, n = x.shape
    npad = -(-n // CUMSUM_BLOCK) * CUMSUM_BLOCK
    xp = jnp.pad(x, ((0, 0), (0, npad - n))) if npad != n else x
    out = pl.pallas_call(
        _cumsum_kernel, out_shape=jax.ShapeDtypeStruct((rows, npad), F32), grid=(1,),
        in_specs=[_full((rows, npad))], out_specs=_full((rows, npad)),
        compiler_params=_params(("arbitrary",)), name="cumsum")(xp)
    return out[:, :n] if npad != n else out


def _reduce_rows(x, op):
    rows, cols = x.shape
    if rows > 64 and rows % 64 == 0:
        x = op(x.reshape(rows // 64, 64, cols), axis=0)
        rows = 64
    if rows == 64:
        x = op(x.reshape(8, 8, cols), axis=0)
    return op(x, axis=0, keepdims=True)


def _fox_prompt_kernel(q_ref, k_ref, v_ref, cr_ref, o_ref,
                       vt_ref, ck0_ref, ck1_ref, st0_ref, st1_ref, pt0_ref, pt1_ref, acc_ref, *, tq, tk):
    st_refs = (st0_ref, st1_ref)
    pt_refs = (pt0_ref, pt1_ref)
    i = pl.program_id(2)
    s_len = k_ref.shape[1]

    @pl.when(i == 0)
    def _():
        vt_ref[...] = v_ref[0].astype(F32).T.astype(BF16)
        ck0_ref[...] = jnp.broadcast_to(cr_ref[0, 0, 0:1, :], (LANES, s_len)).T
        ck1_ref[...] = jnp.broadcast_to(cr_ref[0, 0, 1:2, :], (LANES, s_len)).T

    def q_heads_t(blk):
        qt = q_ref[0, pl.ds(pl.multiple_of(blk * tq, tq), tq), :].astype(F32).T
        row = lax.broadcasted_iota(jnp.int32, (LANES, tq), 0)
        return (jnp.where(row < FOX_HEAD_DIM, qt, 0.0).astype(BF16),
                jnp.where(row < FOX_HEAD_DIM, 0.0, qt).astype(BF16))

    qts = q_heads_t(i)
    q0 = pl.multiple_of(i * tq, tq)
    cq = cr_ref[0, 0, :, pl.ds(q0, tq)]
    ck_refs = (ck0_ref, ck1_ref)

    def stage_a(n, par, qts=qts):
        s = pl.multiple_of(n * tk, tk)
        kb = k_ref[0, pl.ds(s, tk), :]
        for hh in range(2):
            ck = ck_refs[hh][pl.ds(s, tk), :]
            st_refs[par][hh] = _dot(kb, qts[hh]) - jnp.concatenate([ck] * (tq // LANES), axis=1)

    def stage_b(n, par, stats, masked):
        if masked:
            kpos = n * tk + lax.broadcasted_iota(jnp.int32, (tk, tq), 0)
            qpos = q0 + lax.broadcasted_iota(jnp.int32, (tk, tq), 1)
            mask = kpos <= qpos
        out = []
        for hh in range(2):
            m, l = stats[2 * hh:2 * hh + 2]
            t = st_refs[par][hh]
            if masked:
                t = jnp.where(mask, t, NEG_INF)
            cqh = cq[hh:hh + 1, :]
            m_new = jnp.maximum(m, _reduce_rows(t, jnp.max) + cqh)
            alpha = jnp.exp2(m - m_new)
            p = jnp.exp2(t + (cqh - m_new))
            pt_refs[par][hh] = p.astype(BF16)
            out.extend([m_new, alpha * l + _reduce_rows(p, jnp.sum), alpha])
        return tuple(out)

    def stage_c(n, par, alphas):
        s = pl.multiple_of(jnp.maximum(n, 0) * tk, tk)
        for hh in range(2):
            vt = vt_ref[hh * FOX_HEAD_DIM:(hh + 1) * FOX_HEAD_DIM, pl.ds(s, tk)]
            acc_ref[hh] = alphas[hh] * acc_ref[hh] + _dot(vt, pt_refs[par][hh])

    def iteration(n, par, carry):
        m0, l0, al0, m1, l1, al1 = carry
        stage_c(n - 1, 1 - par, (al0, al1))
        new = stage_b(n, par, (m0, l0, m1, l1), False)
        stage_a(n + 1, 1 - par)
        return new

    def finish(par, carry):
        m0, l0, al0, m1, l1, al1 = carry
        stage_c(nfull - 1, 1 - par, (al0, al1))
        _, l0, be0, _, l1, be1 = stage_b(nfull, par, (m0, l0, m1, l1), True)
        stage_a(0, 0, q_heads_t(jnp.minimum(i + 1, pl.num_programs(2) - 1)))
        stage_c(nfull, par, (be0, be1))
        ot = jnp.concatenate([acc_ref[0] / l0, acc_ref[1] / l1], axis=0)
        o_ref[0] = ot.T.astype(o_ref.dtype)

    acc_ref[...] = jnp.zeros(acc_ref.shape, F32)
    pt1_ref[...] = jnp.zeros(pt1_ref.shape, BF16)
    neg = jnp.full((1, tq), NEG_INF, F32)
    zero = jnp.zeros((1, tq), F32)
    one = jnp.ones((1, tq), F32)
    nfull = (i * tq) // tk
    pl.when(i == 0)(lambda: stage_a(0, 0))
    carry = lax.fori_loop(0, nfull // 2, lambda k, c: iteration(2 * k + 1, 1, iteration(2 * k, 0, c)),
                          (neg, zero, one, neg, zero, one))
    odd = nfull % 2 == 1
    carry = lax.cond(odd, lambda c: iteration(nfull - 1, 0, c), lambda c: c, carry)
    pl.when(odd)(lambda: finish(1, carry))
    pl.when(jnp.logical_not(odd))(lambda: finish(0, carry))


def _fox_prompt(q, k, v, c_row, tq, tk):
    b, s, _ = q.shape
    assert s % tk == 0 and tk % tq == 0
    npair = N_FOX_HEADS // 2
    return pl.pallas_call(
        functools.partial(_fox_prompt_kernel, tq=tq, tk=tk),
        out_shape=jax.ShapeDtypeStruct((b, s, FOX_WIDTH), BF16),
        grid=(b, npair, s // tq),
        in_specs=[
            pl.BlockSpec((1, s, LANES), lambda bi, hp, i: (bi, 0, hp)),
            pl.BlockSpec((1, s, LANES), lambda bi, hp, i: (bi, 0, hp)),
            pl.BlockSpec((1, s, LANES), lambda bi, hp, i: (bi, 0, hp)),
            pl.BlockSpec((1, 1, 2, s), lambda bi, hp, i: (bi, hp, 0, 0)),
        ],
        out_specs=pl.BlockSpec((1, tq, LANES), lambda bi, hp, i: (bi, i, hp)),
        scratch_shapes=[pltpu.VMEM((LANES, s), BF16),
                        pltpu.VMEM((s, LANES), F32), pltpu.VMEM((s, LANES), F32),
                        pltpu.VMEM((2, tk, tq), F32), pltpu.VMEM((2, tk, tq), F32),
                        pltpu.VMEM((2, tk, tq), BF16), pltpu.VMEM((2, tk, tq), BF16),
                        pltpu.VMEM((2, FOX_HEAD_DIM, tq), F32)],
        compiler_params=_params(("parallel", "parallel", "arbitrary")), name="fox_prompt")(q, k, v, c_row)


def _fox_sample_kernel(q_ref, ck_ref, cv_ref, nk_ref, nv_ref, cq_ref, crc_ref, crn_ref, o_ref, *state, n):
    j = pl.program_id(1)
    nj = pl.num_programs(1)
    m_refs = state[0:N_FOX_HEADS]
    l_refs = state[N_FOX_HEADS:2 * N_FOX_HEADS]
    acc_refs = state[2 * N_FOX_HEADS:3 * N_FOX_HEADS]

    @pl.when(j == 0)
    def _():
        for hd in range(N_FOX_HEADS):
            m_refs[hd][...] = jnp.full(m_refs[hd].shape, NEG_INF, F32)
            l_refs[hd][...] = jnp.zeros(l_refs[hd].shape, F32)
            acc_refs[hd][...] = jnp.zeros(acc_refs[hd].shape, F32)

    def update(k_of, v_of, cr_ref_, mask, transposed):
        qk = _dot if transposed else _dot_nt
        pv = _dot_nt if transposed else _dot
        ts = []
        for hd in range(N_FOX_HEADS):
            hs = slice(hd * FOX_HEAD_DIM, (hd + 1) * FOX_HEAD_DIM)
            t = qk(q_ref[0, :, hs], k_of(hd)) - cr_ref_[0, hd:hd + 1, :]
            ts.append(t if mask is None else jnp.where(mask, t, NEG_INF))
        ps = []
        for hd in range(N_FOX_HEADS):
            cq = cq_ref[0, :, hd:hd + 1]
            m = m_refs[hd][...]
            m_new = jnp.maximum(m, jnp.max(ts[hd], axis=-1, keepdims=True) + cq)
            alpha = jnp.exp2(m - m_new)
            p = jnp.exp2(ts[hd] + (cq - m_new))
            m_refs[hd][...] = m_new
            l_refs[hd][...] = alpha * l_refs[hd][...] + jnp.sum(p, axis=-1, keepdims=True)
            ps.append((alpha, p.astype(BF16)))
        for hd in range(N_FOX_HEADS):
            alpha, p = ps[hd]
            acc_refs[hd][...] = alpha * acc_refs[hd][...] + pv(p, v_of(hd))

    update(lambda hd: ck_ref[0, hd].astype(BF16), lambda hd: cv_ref[0, hd].astype(BF16), crc_ref, None, True)

    @pl.when(j == nj - 1)
    def _():
        r = lax.broadcasted_iota(jnp.int32, (n, n), 0)
        c = lax.broadcasted_iota(jnp.int32, (n, n), 1)
        head = lambda ref: (lambda hd: ref[0, :, hd * FOX_HEAD_DIM:(hd + 1) * FOX_HEAD_DIM])
        update(head(nk_ref), head(nv_ref), crn_ref, c <= r, False)
        for hd in range(N_FOX_HEADS):
            hs = slice(hd * FOX_HEAD_DIM, (hd + 1) * FOX_HEAD_DIM)
            o_ref[0, :, hs] = (acc_refs[hd][...] / l_refs[hd][...]).astype(o_ref.dtype)


def _fox_sample(q, cache_k, cache_v, k_new, v_new, c_q, c_row_cache, c_row_new, tk):
    b, n, _ = q.shape
    past = cache_k.shape[3]
    assert past % tk == 0
    cache_spec = pl.BlockSpec((1, N_FOX_HEADS, FOX_HEAD_DIM, tk), lambda bi, j: (bi, 0, 0, j))
    return pl.pallas_call(
        functools.partial(_fox_sample_kernel, n=n),
        out_shape=jax.ShapeDtypeStruct((b, n, FOX_WIDTH), BF16),
        grid=(b, past // tk),
        in_specs=[
            pl.BlockSpec((1, n, FOX_WIDTH), lambda bi, j: (bi, 0, 0)),
            cache_spec,
            cache_spec,
            pl.BlockSpec((1, n, FOX_WIDTH), lambda bi, j: (bi, 0, 0)),
            pl.BlockSpec((1, n, FOX_WIDTH), lambda bi, j: (bi, 0, 0)),
            pl.BlockSpec((1, n, N_FOX_HEADS), lambda bi, j: (bi, 0, 0)),
            pl.BlockSpec((1, N_FOX_HEADS, tk), lambda bi, j: (bi, 0, j)),
            pl.BlockSpec((1, N_FOX_HEADS, n), lambda bi, j: (bi, 0, 0)),
        ],
        out_specs=pl.BlockSpec((1, n, FOX_WIDTH), lambda bi, j: (bi, 0, 0)),
        scratch_shapes=([pltpu.VMEM((n, 1), F32)] * (2 * N_FOX_HEADS)
                        + [pltpu.VMEM((n, FOX_HEAD_DIM), F32)] * N_FOX_HEADS),
        compiler_params=_params(("parallel", "arbitrary")), name="fox_sample")(
            q, cache_k, cache_v, k_new, v_new, c_q, c_row_cache, c_row_new)


def _ssm_mats(p):
    f32 = F32
    a_re, a_im = p['ssm_a_re'].astype(f32), p['ssm_a_im'].astype(f32)
    b_re, b_im = p['ssm_b_re'].astype(f32), p['ssm_b_im'].astype(f32)
    c_re, c_im = p['ssm_c_re'].astype(f32), p['ssm_c_im'].astype(f32)
    dt = jnp.exp(p['ssm_log_dt'].astype(f32))[:, None]
    mag = jnp.exp(dt * a_re)
    ab_re = mag * jnp.cos(dt * a_im)
    ab_im = mag * jnp.sin(dt * a_im)
    den = a_re * a_re + a_im * a_im
    nr, ni = ab_re - 1.0, ab_im
    coef_re = (nr * a_re + ni * a_im) / den
    coef_im = (ni * a_re - nr * a_im) / den
    bb_re = coef_re[..., None] * b_re - coef_im[..., None] * b_im
    bb_im = coef_re[..., None] * b_im + coef_im[..., None] * b_re
    pr, pi = [jnp.ones_like(ab_re)], [jnp.zeros_like(ab_im)]
    for _ in range(SSM_CHUNK):
        pr.append(pr[-1] * ab_re - pi[-1] * ab_im)
        pi.append(pr[-2] * ab_im + pi[-1] * ab_re)
    pw_re, pw_im = jnp.stack(pr), jnp.stack(pi)
    T = SSM_CHUNK
    w_re = pw_re[..., None] * bb_re[None] - pw_im[..., None] * bb_im[None]
    w_im = pw_re[..., None] * bb_im[None] + pw_im[..., None] * bb_re[None]
    kk = (jnp.einsum('gop,kgpi->kgoi', c_re, w_re[:T], precision='highest')
          - jnp.einsum('gop,kgpi->kgoi', c_im, w_im[:T], precision='highest'))
    nq = N_SSM_GROUPS // SSM_GPB

    def group_diag(m):
        rows, c = m.shape[-2:]
        m = jnp.tile(m, (1,) * (m.ndim - 1) + (SSM_GPB,))
        same = (jnp.arange(rows) // (rows // SSM_GPB))[:, None] == (jnp.arange(SSM_GPB * c) // c)[None, :]
        return jnp.where(same, m, 0.0)

    def lane_diag(m):
        lead = m.shape[:-3]
        i, c = m.shape[-2:]
        return group_diag(m.reshape(lead + (nq, SSM_GPB * i, c)))

    ktau = lane_diag(jnp.swapaxes(kk, -1, -2))
    ktau = jnp.concatenate([jnp.zeros_like(ktau[:1]), ktau], axis=0)
    units = []
    for dlag in range(T // 2 - 1, -1, -1):
        top = jnp.concatenate([ktau[2 * dlag + 1], ktau[2 * dlag + 2]], axis=-1)
        bot = jnp.concatenate([ktau[2 * dlag], ktau[2 * dlag + 1]], axis=-1)
        units.append(jnp.concatenate([top, bot], axis=-2))
    kstack = jnp.concatenate(units, axis=-2).astype(BF16)
    rev = T - 1 - jnp.arange(T)
    def local_rows(w):
        w = jnp.transpose(w[rev], (1, 0, 3, 2)).reshape(nq, SSM_GPB, T, SSM_GROUP, SSM_STATE)
        return group_diag(jnp.transpose(w, (0, 2, 1, 3, 4)).reshape(nq, T, LANES, SSM_STATE))

    m_all = jnp.concatenate([local_rows(w_re), local_rows(w_im)], axis=-1)
    m_all = m_all.reshape(nq, T * LANES, 2 * SSM_GPB * SSM_STATE)
    m_hi, m_lo = _split_bf16(m_all)
    ar, ai = pw_re[1:], pw_im[1:]
    n_re = (c_re[None] * ar[:, :, None, :] - c_im[None] * ai[:, :, None, :])
    n_im = -(c_re[None] * ai[:, :, None, :] + c_im[None] * ar[:, :, None, :])

    def state_rows(n):
        n = jnp.transpose(n, (1, 3, 0, 2)).reshape(nq, SSM_GPB * SSM_STATE, T, SSM_GROUP)
        n = jnp.tile(jnp.transpose(n, (0, 2, 1, 3)), (1, 1, 1, SSM_GPB))
        same = (jnp.arange(SSM_GPB * SSM_STATE) // SSM_STATE)[:, None] == (jnp.arange(LANES) // SSM_GROUP)[None, :]
        return jnp.where(same, n, 0.0)

    n_all = jnp.concatenate([state_rows(n_re), state_rows(n_im)], axis=2).astype(BF16)
    return dict(kstack=kstack, m_hi=m_hi, m_lo=m_lo, n_all=n_all,
                a16_re=pw_re[T].reshape(STATE_TILE), a16_im=pw_im[T].reshape(STATE_TILE),
                d=p['ssm_d'].astype(f32).reshape(1, SSM_WIDTH))


def _chunk_tokens(u_ref, rows):
    return [u_ref[pl.ds(t, rows, stride=SSM_CHUNK), :] for t in range(SSM_CHUNK)]


def _ssm_local_kernel(u_ref, mh_ref, ml_ref, hre_ref, him_ref):
    rows = hre_ref.shape[0]
    parts = [_split_bf16(ut) for ut in _chunk_tokens(u_ref, rows)]
    x_hi = jnp.concatenate([h for h, _ in parts], axis=1)
    x_lo = jnp.concatenate([l for _, l in parts], axis=1)
    h = _dot(x_hi, mh_ref[0]) + _dot(x_hi, ml_ref[0]) + _dot(x_lo, mh_ref[0])
    half = SSM_GPB * SSM_STATE
    hre_ref[...] = h[:, 0:half]
    him_ref[...] = h[:, half:2 * half]


def _ssm_local(u2d, mats, rows):
    n = u2d.shape[0]
    r = n // SSM_CHUNK
    nq = N_SSM_GROUPS // SSM_GPB
    half = SSM_GPB * SSM_STATE
    mspec = pl.BlockSpec((1, SSM_CHUNK * LANES, 2 * half), lambda q, i: (q, 0, 0))
    ospec = pl.BlockSpec((rows, half), lambda q, i: (i, q))
    return pl.pallas_call(
        _ssm_local_kernel,
        out_shape=(jax.ShapeDtypeStruct((r, N_SSM_GROUPS * SSM_STATE), F32),) * 2,
        grid=(nq, r // rows),
        in_specs=[pl.BlockSpec((rows * SSM_CHUNK, LANES), lambda q, i: (i, q)), mspec, mspec],
        out_specs=(ospec, ospec),
        compiler_params=_params(("parallel", "parallel")), name="ssm_local")(u2d, mats['m_hi'], mats['m_lo'])


def _ssm_scan_kernel(lre_ref, lim_ref, are_ref, aim_ref, h0re_ref, h0im_ref,
                     pre_ref, pim_ref, fre_ref, fim_ref):
    nb, nchunk = lre_ref.shape[:2]
    ar, ai = are_ref[...], aim_ref[...]

    def body(c, carry):
        out = []
        for bi in range(nb):
            hr, hi = carry[2 * bi:2 * bi + 2]
            pre_ref[bi, c] = hr
            pim_ref[bi, c] = hi
            out.extend([ar * hr - ai * hi + lre_ref[bi, c], ar * hi + ai * hr + lim_ref[bi, c]])
        return tuple(out)

    init = []
    for bi in range(nb):
        init.extend([h0re_ref[bi], h0im_ref[bi]])
    final = lax.fori_loop(0, nchunk, body, tuple(init))
    for bi in range(nb):
        fre_ref[bi] = final[2 * bi]
        fim_ref[bi] = final[2 * bi + 1]


def _ssm_scan(hloc_re, hloc_im, mats, h0_re, h0_im):
    b, nchunk = hloc_re.shape[:2]
    nb = _pick_tile(b, max(1, SCAN_CHUNK_ROWS // nchunk))
    big = pl.BlockSpec((nb, nchunk) + STATE_TILE, lambda i: (i, 0, 0, 0))
    small = pl.BlockSpec((nb,) + STATE_TILE, lambda i: (i, 0, 0))
    return pl.pallas_call(
        _ssm_scan_kernel,
        out_shape=(jax.ShapeDtypeStruct(hloc_re.shape, F32),) * 2 + (jax.ShapeDtypeStruct((b,) + STATE_TILE, F32),) * 2,
        grid=(b // nb,),
        in_specs=[big, big, _full(STATE_TILE), _full(STATE_TILE), small, small],
        out_specs=(big, big, small, small),
        compiler_params=_params(("parallel",)), name="ssm_scan")(
            hloc_re, hloc_im, mats['a16_re'], mats['a16_im'], h0_re, h0_im)


def _gelu_tanh(y):
    return 0.5 * y * (1.0 + jnp.tanh(math.sqrt(2.0 / math.pi) * (y + 0.044715 * (y * y * y))))


def _ssm_out_kernel(u_ref, k_ref, pre_ref, pim_ref, n_ref, d_ref, y_ref, ysc_ref):
    rows = pre_ref.shape[0]
    us = _chunk_tokens(u_ref, rows)
    x = jnp.concatenate([ut.astype(BF16) for ut in us], axis=1)
    hp = jnp.concatenate([pre_ref[...], pim_ref[...]], axis=1).astype(BF16)
    unit = 2 * LANES
    nunit = SSM_CHUNK // 2
    for j in range(nunit):
        n_unit = jnp.concatenate([n_ref[0, 2 * j], n_ref[0, 2 * j + 1]], axis=1)
        yj = _dot(x[:, 0:unit * (j + 1)], k_ref[0, unit * (nunit - 1 - j):, :]) + _dot(hp, n_unit)
        for t2 in range(2):
            t = 2 * j + t2
            y = yj[:, t2 * LANES:(t2 + 1) * LANES] + d_ref[...] * us[t]
            ysc_ref[pl.ds(t, rows, stride=SSM_CHUNK), :] = _gelu_tanh(y)
    y_ref[...] = ysc_ref[...].astype(y_ref.dtype)


def _ssm_out(u2d, hprev_re, hprev_im, mats, rows):
    n = u2d.shape[0]
    r = n // SSM_CHUNK
    nq = N_SSM_GROUPS // SSM_GPB
    half = SSM_GPB * SSM_STATE
    uspec = pl.BlockSpec((rows * SSM_CHUNK, LANES), lambda q, i: (i, q))
    hspec = pl.BlockSpec((rows, half), lambda q, i: (i, q))
    return pl.pallas_call(
        _ssm_out_kernel,
        out_shape=jax.ShapeDtypeStruct((n, SSM_WIDTH), BF16),
        grid=(nq, r // rows),
        in_specs=[uspec, pl.BlockSpec((1, SSM_CHUNK * LANES, 2 * LANES), lambda q, i: (q, 0, 0)), hspec, hspec,
                  pl.BlockSpec((1, SSM_CHUNK, 2 * half, LANES), lambda q, i: (q, 0, 0, 0)),
                  pl.BlockSpec((1, LANES), lambda q, i: (0, q))],
        out_specs=uspec,
        scratch_shapes=[pltpu.VMEM((rows * SSM_CHUNK, LANES), F32)],
        compiler_params=_params(("parallel", "parallel")), name="ssm_out")(
            u2d, mats['kstack'], hprev_re, hprev_im, mats['n_all'], mats['d'])


def _ssm(u2d, b, h0_re, h0_im, mats):
    n = u2d.shape[0]
    nchunk = n // b // SSM_CHUNK
    r = b * nchunk
    rows = _pick_tile(r, TILE_SSM_ROWS)
    hloc_re, hloc_im = _ssm_local(u2d, mats, rows)
    shp = (b, nchunk) + STATE_TILE
    hprev_re, hprev_im, f_re, f_im = _ssm_scan(hloc_re.reshape(shp), hloc_im.reshape(shp), mats,
                                               h0_re.reshape((b,) + STATE_TILE), h0_im.reshape((b,) + STATE_TILE))
    y = _ssm_out(u2d, hprev_re.reshape(r, -1), hprev_im.reshape(r, -1), mats, rows)
    return y, f_re.reshape(b, N_SSM_GROUPS, SSM_STATE), f_im.reshape(b, N_SSM_GROUPS, SSM_STATE)


def _merge_kernel(x_ref, of_ref, ys_ref, qm_ref, gate_ref, mk_ref, mv_ref,
                  wglu_ref, wbf_ref, wbs_ref, wbm_ref, wo_ref, nf_ref, wr_ref,
                  x1_ref, h2_ref, r_ref):
    tm = x_ref.shape[0]
    nb = mk_ref.shape[0]
    rows = tm // nb
    om = []
    for hd in range(N_MEM_HEADS):
        sl = slice(hd * MEM_HEAD_DIM, (hd + 1) * MEM_HEAD_DIM)
        head_rows = pl.ds(hd, N_MEM, stride=N_MEM_HEADS)
        per_batch = []
        for bi in range(nb):
            kh = mk_ref[bi, head_rows, :].astype(BF16)
            vh = mv_ref[bi, head_rows, :].astype(BF16)
            sc = _dot_nt(qm_ref[bi * rows:(bi + 1) * rows, sl], kh)
            p = jnp.exp(sc - jnp.max(sc, axis=-1, keepdims=True))
            per_batch.append(_dot(p.astype(BF16), vh) / jnp.sum(p, axis=-1, keepdims=True))
        om.append(per_batch[0] if nb == 1 else jnp.concatenate(per_batch, axis=0))
    o_mem = jnp.concatenate(om, axis=-1).astype(BF16)
    z = _dot(ys_ref[...], wglu_ref[...])
    y_ssm = (z[:, 0:SSM_WIDTH] * jax.nn.sigmoid(z[:, SSM_WIDTH:2 * SSM_WIDTH])).astype(BF16)
    g = lambda c: gate_ref[:, c * D_MODEL:(c + 1) * D_MODEL].astype(F32)
    merged = (g(0) * _dot(of_ref[...], wbf_ref[...]) + g(1) * _dot(y_ssm, wbs_ref[...])
              + g(2) * _dot(o_mem, wbm_ref[...]))
    x1 = x_ref[...] + _dot(merged.astype(BF16), wo_ref[...])
    x1_ref[...] = x1
    h2 = x1 * lax.rsqrt(jnp.mean(x1 * x1, axis=-1, keepdims=True) + RMS_EPS) * nf_ref[...]
    h2_ref[...] = h2.astype(BF16)
    h2_hi, h2_lo = _split_bf16(h2)
    hw = _dot(h2_hi, wr_ref[...])
    logits = hw[:, 0:LANES] + hw[:, LANES:2 * LANES] + _dot(h2_lo, wr_ref[:, 0:LANES])
    lane = lax.broadcasted_iota(jnp.int32, (tm, LANES), 1)
    big = jnp.int32(LANES)
    is_grp = (lane >= N_EXPERTS) & (lane < N_EXPERTS + N_EXPERT_GROUPS)
    gl = jnp.where(is_grp, logits, NEG_INF)
    gmax = jnp.max(gl, axis=-1, keepdims=True)
    grp = jnp.min(jnp.where(is_grp & (gl == gmax), lane, big), axis=-1, keepdims=True) - N_EXPERTS
    g_w = 1.0 / jnp.sum(jnp.where(is_grp, jnp.exp(gl - gmax), 0.0), axis=-1, keepdims=True)
    in_grp = (lane >= grp * EXPERTS_PER_GROUP) & (lane < (grp + 1) * EXPERTS_PER_GROUP)
    e1 = jnp.where(in_grp, logits, NEG_INF)
    m1 = jnp.max(e1, axis=-1, keepdims=True)
    i1 = jnp.min(jnp.where(in_grp & (e1 == m1), lane, big), axis=-1, keepdims=True)
    rest = in_grp & (lane != i1)
    e2 = jnp.where(rest, logits, NEG_INF)
    m2 = jnp.max(e2, axis=-1, keepdims=True)
    i2 = jnp.min(jnp.where(rest & (e2 == m2), lane, big), axis=-1, keepdims=True)
    ex = jnp.exp(m2 - m1)
    w1 = g_w / (1.0 + ex)
    w2 = g_w * ex / (1.0 + ex)
    r_ref[...] = jnp.where(lane == i1, w1, jnp.where(lane == i2, w2, jnp.where(lane == GROUP_LANE, grp.astype(F32), 0.0)))


def _merge(x2d, o_fox, ys, q_m, gates, mem_k, mem_v, w, tm, rows_per_batch):
    n = x2d.shape[0]
    assert n % tm == 0 and (rows_per_batch % tm == 0 or tm % rows_per_batch == 0)
    row = lambda width: pl.BlockSpec((tm, width), lambda i: (i, 0))
    if rows_per_batch >= tm:
        per = rows_per_batch // tm
        memspec = pl.BlockSpec((1, N_MEM * N_MEM_HEADS, MEM_HEAD_DIM), lambda i: (i // per, 0, 0))
    else:
        memspec = pl.BlockSpec((tm // rows_per_batch, N_MEM * N_MEM_HEADS, MEM_HEAD_DIM), lambda i: (i, 0, 0))
    ws = [w['w_glu'], w['w_br_fox'], w['w_br_ssm'], w['w_br_mem'], w['w_out'], w['norm_ffn'], w['w_router']]
    return pl.pallas_call(
        _merge_kernel,
        out_shape=(jax.ShapeDtypeStruct((n, D_MODEL), F32), jax.ShapeDtypeStruct((n, D_MODEL), BF16),
                   jax.ShapeDtypeStruct((n, LANES), F32)),
        grid=(n // tm,),
        in_specs=[row(D_MODEL), row(FOX_WIDTH), row(SSM_WIDTH), row(MEM_WIDTH), row(3 * D_MODEL), memspec, memspec]
                 + [_full(a.shape) for a in ws],
        out_specs=(row(D_MODEL), row(D_MODEL), row(LANES)),
        compiler_params=_params(("parallel",)), name="merge")(
            x2d, o_fox, ys, q_m, gates, mem_k, mem_v, *ws)


def _moe_kernel(h_ref, r_ref, x1_ref, tri_ref, wg_ref, wu_ref, wd_ref, o_ref,
                xs_ref, cw_ref, og_ref, acc_ref, rank_ref, count_ref, *, main):
    step = pl.program_id(1)
    steps_per_group = EXPERTS_PER_GROUP // MOE_EPS
    g = step // steps_per_group
    tm = h_ref.shape[0]
    gf = g.astype(F32)
    bounds = [0, main] + list(range(-(-main // MOE_SUB) * MOE_SUB, tm, MOE_SUB)) + [tm]
    bounds = sorted(set(bounds))
    blocks = [(lo, hi - lo, lo > 0) for lo, hi in zip(bounds[:-1], bounds[1:])]

    def guarded(r0, fn):
        if r0 == 0:
            fn()
        else:
            pl.when(r0 < count_ref[0])(fn)

    @pl.when(step == 0)
    def _():
        acc_ref[...] = jnp.zeros(acc_ref.shape, F32)

    @pl.when(step % steps_per_group == 0)
    def _():
        rt = r_ref[...]
        rtt = rt.T
        mrow = rtt[GROUP_LANE:GROUP_LANE + 1, :] == gf
        m8 = jnp.broadcast_to(jnp.where(mrow, 1.0, 0.0), (8, tm))
        rank8 = _dot(m8.astype(BF16), tri_ref[...])
        rank_row = jnp.where(mrow, rank8[0:1, :], -1.0)
        rank_ref[...] = jnp.broadcast_to(jnp.where(mrow, rank8, -1.0).T[:, 0:1], rank_ref.shape)
        count_ref[0] = jnp.sum(jnp.where(mrow, 1, 0))
        hilo = jnp.concatenate(_split_bf16(rt), axis=1)
        for r0, nrows, _ in blocks:
            def compact(r0=r0, nrows=nrows):
                rows = slice(r0, r0 + nrows)
                slot = r0 + lax.broadcasted_iota(jnp.int32, (nrows, tm), 0)
                perm = jnp.where(rank_row == slot.astype(F32), 1.0, 0.0).astype(BF16)
                xs_ref[rows, :] = _dot(perm, h_ref[...]).astype(BF16)
                cw = _dot(perm, hilo)
                cw_ref[rows, :] = cw[:, 0:LANES] + cw[:, LANES:2 * LANES]
                og_ref[rows, :] = jnp.zeros((nrows, D_MODEL), F32)
            guarded(r0, compact)

    for k in range(MOE_EPS):
        e = step * MOE_EPS + k
        for r0, nrows, _ in blocks:
            def expert(r0=r0, nrows=nrows, k=k, e=e):
                rows = slice(r0, r0 + nrows)
                x = xs_ref[rows, :]
                a = _dot(x, wg_ref[k])
                up = _dot(x, wu_ref[k])
                lane = lax.broadcasted_iota(jnp.int32, (nrows, LANES), 1)
                ce = jnp.sum(jnp.where(lane == e, cw_ref[rows, :], 0.0), axis=-1, keepdims=True)
                act = a * jax.nn.sigmoid(a) * up * ce
                og_ref[rows, :] += _dot(act.astype(BF16), wd_ref[k])
            guarded(r0, expert)

    @pl.when(step % steps_per_group == steps_per_group - 1)
    def _():
        for r0, nrows, _ in blocks:
            def scatter_back(r0=r0, nrows=nrows):
                rows = slice(r0, r0 + nrows)
                slot = r0 + lax.broadcasted_iota(jnp.int32, (tm, nrows), 1)
                back = jnp.where(rank_ref[:, 0:1] == slot.astype(F32), 1.0, 0.0).astype(BF16)
                acc_ref[...] += _dot(back, og_ref[rows, :].astype(BF16))
            guarded(r0, scatter_back)

    @pl.when(step == pl.num_programs(1) - 1)
    def _():
        o_ref[...] = x1_ref[...] + acc_ref[...]


def _moe(h2, route, x1, wg, wu, wd, tm):
    n = h2.shape[0]
    assert n % tm == 0 and tm % MOE_SUB == 0
    main = max(MOE_SUB // 2, (9 * tm // 32) // 32 * 32)
    row = lambda width: pl.BlockSpec((tm, width), lambda i, s: (i, 0))
    r = jnp.arange(tm)
    tri = (r[:, None] < r[None, :]).astype(BF16)
    return pl.pallas_call(
        functools.partial(_moe_kernel, main=main),
        out_shape=jax.ShapeDtypeStruct((n, D_MODEL), F32),
        grid=(n // tm, N_EXPERTS // MOE_EPS),
        in_specs=[row(D_MODEL), row(LANES), row(D_MODEL), pl.BlockSpec((tm, tm), lambda i, s: (0, 0)),
                  pl.BlockSpec((MOE_EPS, D_MODEL, D_EXPERT), lambda i, s: (s, 0, 0)),
                  pl.BlockSpec((MOE_EPS, D_MODEL, D_EXPERT), lambda i, s: (s, 0, 0)),
                  pl.BlockSpec((MOE_EPS, D_EXPERT, D_MODEL), lambda i, s: (s, 0, 0))],
        out_specs=row(D_MODEL),
        scratch_shapes=[pltpu.VMEM((tm, D_MODEL), BF16), pltpu.VMEM((tm, LANES), F32), pltpu.VMEM((tm, D_MODEL), F32),
                        pltpu.VMEM((tm, D_MODEL), F32), pltpu.VMEM((tm, LANES), F32), pltpu.SMEM((1,), jnp.int32)],
        compiler_params=_params(("parallel", "arbitrary")), name="moe")(h2, route, x1, tri, wg, wu, wd)


def _prep_weights(p):
    w_in = p['w_in'].astype(BF16)
    o = 0
    wqkv = w_in[:, 0:3 * FOX_WIDTH]
    o = 3 * FOX_WIDTH
    wf = jnp.pad(w_in[:, o:o + N_FOX_HEADS], ((0, 0), (0, LANES - N_FOX_HEADS)))
    o += N_FOX_HEADS
    wqm = w_in[:, o:o + MEM_WIDTH]
    o += MEM_WIDTH
    wu = w_in[:, o:o + SSM_WIDTH]
    o += SSM_WIDTH
    wg = w_in[:, o:o + 3 * D_MODEL]
    r = jnp.arange(FOX_WIDTH) // FOX_HEAD_DIM
    bd = (r[:, None] == r[None, :]).astype(BF16)
    w_router = jnp.concatenate(
        [p['w_router_expert'], p['w_router_group'],
         jnp.zeros((D_MODEL, LANES - N_EXPERTS - N_EXPERT_GROUPS), F32)], axis=1)
    w_router = jnp.concatenate(_split_bf16(w_router), axis=1)
    return dict(
        norm_mix=p['norm_mix'].reshape(1, D_MODEL), wqkv=wqkv, wf=wf,
        bf=jnp.pad(p['b_forget'], (0, LANES - N_FOX_HEADS)).reshape(1, LANES),
        wqm=wqm, wu=wu, wg=wg,
        qn_fox=jnp.tile(p['qn_fox'], N_FOX_HEADS).reshape(1, FOX_WIDTH),
        kn_fox=jnp.tile(p['kn_fox'], N_FOX_HEADS).reshape(1, FOX_WIDTH),
        qn_mem=p['qn_mem'].reshape(1, MEM_HEAD_DIM), bd=bd,
        w_glu=p['w_glu'].astype(BF16), w_br_fox=p['w_br_fox'].astype(BF16),
        w_br_ssm=p['w_br_ssm'].astype(BF16), w_br_mem=p['w_br_mem'].astype(BF16),
        w_out=p['w_out'].astype(BF16), norm_ffn=p['norm_ffn'].reshape(1, D_MODEL), w_router=w_router,
        moe_wg=p['moe_w_gate'].astype(BF16), moe_wu=p['moe_w_up'].astype(BF16),
        moe_wd=p['moe_w_down'].astype(BF16))


def _pick_tile(n, target):
    t = min(n, target)
    while n % t:
        t //= 2
    return t


def _group(x, w, mats, mem_k, mem_v, h0_re, h0_im, cache):
    b, s, _ = x.shape
    n = b * s
    x2d = x.reshape(n, D_MODEL)
    prompt = cache is None
    q, kb, vb, k_out, v_out, lf_t, q_m, u, gates = _inproj(x2d, w, _pick_tile(s if prompt else n, TILE_INPROJ), s, prompt)
    q3 = q.reshape(b, s, FOX_WIDTH)
    k3 = kb.reshape(b, s, FOX_WIDTH)
    v3 = vb.reshape(b, s, FOX_WIDTH)
    lf_rows = lf_t.reshape(N_FOX_HEADS, b, s).transpose(1, 0, 2)
    lf3 = lf_rows.transpose(0, 2, 1)
    npair = N_FOX_HEADS // 2
    if prompt:
        c_row = LOG2E * _cumsum_rows(lf_rows.reshape(b * N_FOX_HEADS, s)).reshape(b, npair, 2, s)
        o_fox = _fox_prompt(q3, k3, v3, c_row, _pick_tile(s, TILE_FOX_Q), _pick_tile(s, TILE_FOX_K))
        unt = lambda a: a.reshape(b, N_FOX_HEADS, FOX_HEAD_DIM, s).transpose(0, 3, 1, 2)
        k4, v4 = unt(k_out), unt(v_out)
    else:
        cache_k, cache_v, cache_logf = cache
        past = cache_k.shape[1]
        lf_all = jnp.concatenate([cache_logf.astype(F32).transpose(0, 2, 1), lf_rows], axis=2)
        c_row = LOG2E * _cumsum_rows(lf_all.reshape(b * N_FOX_HEADS, past + s)).reshape(b, N_FOX_HEADS, past + s)
        o_fox = _fox_sample(q3, cache_k.transpose(0, 2, 3, 1), cache_v.transpose(0, 2, 3, 1), k3, v3,
                            c_row[:, :, past:].transpose(0, 2, 1), c_row[:, :, :past], c_row[:, :, past:],
                            _pick_tile(past, TILE_SAMPLE_K))
        k4 = k_out.reshape(b, s, N_FOX_HEADS, FOX_HEAD_DIM)
        v4 = v_out.reshape(b, s, N_FOX_HEADS, FOX_HEAD_DIM)
    ys, f_re, f_im = _ssm(u, b, h0_re, h0_im, mats)
    tm = _pick_tile(n, TILE_MERGE) if TILE_MERGE % s == 0 else _pick_tile(s, TILE_MERGE)
    x1, h2, route = _merge(x2d, o_fox.reshape(n, FOX_WIDTH), ys, q_m, gates, mem_k, mem_v, w, tm, s)
    y = _moe(h2, route, x1, w['moe_wg'], w['moe_wu'], w['moe_wd'], _pick_tile(n, TILE_MOE))
    return y.reshape(b, s, D_MODEL), k4, v4, lf3, f_re, f_im


def kernel(x_prompt, x_sample, mem_prompt, cache_fox_k, cache_fox_v, cache_fox_logf, state_ssm_re, state_ssm_im,
           cache_mem_k, cache_mem_v, norm_mix, w_in, b_forget, qn_fox, kn_fox, qn_mem, kn_mem, norm_mem, w_mem_kv,
           ssm_a_re, ssm_a_im, ssm_log_dt, ssm_b_re, ssm_b_im, ssm_c_re, ssm_c_im, ssm_d, w_glu, w_br_fox,
           w_br_ssm, w_br_mem, w_out, norm_ffn, w_router_group, w_router_expert, moe_w_gate, moe_w_up,
           moe_w_down):
    depth = norm_mix.shape[0]
    assert depth == 1
    l = 0
    p = dict(norm_mix=norm_mix[l], w_in=w_in[l], b_forget=b_forget[l], qn_fox=qn_fox[l], kn_fox=kn_fox[l],
             qn_mem=qn_mem[l], ssm_a_re=ssm_a_re[l], ssm_a_im=ssm_a_im[l], ssm_log_dt=ssm_log_dt[l],
             ssm_b_re=ssm_b_re[l], ssm_b_im=ssm_b_im[l], ssm_c_re=ssm_c_re[l], ssm_c_im=ssm_c_im[l],
             ssm_d=ssm_d[l], w_glu=w_glu[l], w_br_fox=w_br_fox[l], w_br_ssm=w_br_ssm[l], w_br_mem=w_br_mem[l],
             w_out=w_out[l], norm_ffn=norm_ffn[l], w_router_group=w_router_group[l],
             w_router_expert=w_router_expert[l], moe_w_gate=moe_w_gate[l], moe_w_up=moe_w_up[l],
             moe_w_down=moe_w_down[l])
    w = _prep_weights(p)
    mats = _ssm_mats(p)
    bp, sp, _ = x_prompt.shape
    bs, ss, _ = x_sample.shape

    mk, mv = _memkv(mem_prompt.reshape(bp * N_MEM, D_MODEL), norm_mem[l].reshape(1, D_MODEL),
                    w_mem_kv[l].astype(BF16), kn_mem[l].reshape(1, MEM_HEAD_DIM), _pick_tile(bp * N_MEM, TILE_MEMKV))
    mem_rows = lambda a, b: a.reshape(b, N_MEM * N_MEM_HEADS, MEM_HEAD_DIM)
    mk = mem_rows(mk, bp)
    mv = mem_rows(mv, bp)
    zeros = jnp.zeros((bp, N_SSM_GROUPS, SSM_STATE), F32)
    yp, pk, pv, plf, pre, pim = _group(x_prompt, w, mats, mk, mv, zeros, zeros, None)
    cache = (cache_fox_k[l], cache_fox_v[l], cache_fox_logf[l])
    ys, sk, sv, slf, sre, sim = _group(
        x_sample, w, mats, mem_rows(cache_mem_k[l], bs), mem_rows(cache_mem_v[l], bs),
        state_ssm_re[l].astype(F32), state_ssm_im[l].astype(F32), cache)
    st = lambda a: a[None]
    return (yp, ys, st(pk), st(pv), st(plf), st(pre), st(pim),
            st(mk.reshape(bp, N_MEM, N_MEM_HEADS, MEM_HEAD_DIM)), st(mv.reshape(bp, N_MEM, N_MEM_HEADS, MEM_HEAD_DIM)),
            st(sk), st(sv), st(slf), st(sre), st(sim))
```

```python
import functools
import math

import jax
import jax.numpy as jnp
from jax import lax
from jax.experimental import pallas as pl
from jax.experimental.pallas import tpu as pltpu

F32 = jnp.float32
BF16 = jnp.bfloat16

D_MODEL = 1024
N_FOX_HEADS = 8
FOX_HEAD_DIM = 64
FOX_WIDTH = N_FOX_HEADS * FOX_HEAD_DIM
N_MEM = 256
N_MEM_HEADS = 4
MEM_HEAD_DIM = 128
MEM_WIDTH = N_MEM_HEADS * MEM_HEAD_DIM
SSM_GROUP = 16
SSM_WIDTH = 512
N_SSM_GROUPS = SSM_WIDTH // SSM_GROUP
SSM_STATE = 64
N_EXPERT_GROUPS = 4
EXPERTS_PER_GROUP = 8
N_EXPERTS = N_EXPERT_GROUPS * EXPERTS_PER_GROUP
D_EXPERT = 256
RMS_EPS = 1e-6
NEG_INF = -1e30
LOG2E = 1.4426950408889634

LANES = 128
SSM_CHUNK = 16
SSM_GPB = LANES // SSM_GROUP
GROUP_LANE = N_EXPERTS
MOE_SUB = 128
MOE_EPS = 4
VMEM_LIMIT = 56 * 1024 * 1024
STATE_TILE = (8, N_SSM_GROUPS * SSM_STATE // 8)

TILE_INPROJ = 512
TILE_MEMKV = 512
TILE_FOX_Q = 512
TILE_FOX_K = 512
TILE_SAMPLE_K = 4096
TILE_SSM_ROWS = 256
SCAN_CHUNK_ROWS = 512
TILE_MERGE = 512
TILE_MOE = 1024


def _dot(a, b):
    return jnp.dot(a, b, preferred_element_type=F32)


def _dot_nt(a, b):
    return lax.dot_general(a, b, (((1,), (1,)), ((), ())), preferred_element_type=F32)


def _dot_exact(a, b):
    return jnp.dot(a, b, preferred_element_type=F32, precision=lax.Precision.HIGHEST)


def _split_bf16(x):
    hi = x.astype(BF16)
    lo = (x - hi.astype(F32)).astype(BF16)
    return hi, lo


def _params(sem):
    return pltpu.CompilerParams(dimension_semantics=sem, vmem_limit_bytes=VMEM_LIMIT)


def _full(shape):
    n = len(shape)
    return pl.BlockSpec(shape, lambda *_: (0,) * n)


def _inproj_kernel(x_ref, g_ref, wqkv_ref, wf_ref, bf_ref, wqm_ref, wu_ref, wg_ref,
                   qn_ref, kn_ref, qmn_ref, bd_ref,
                   q_ref, kb_ref, vb_ref, k_ref, v_ref, lf_ref, qm_ref, u_ref, gate_ref, *, kv_transposed):
    x = x_ref[...]
    h = x * lax.rsqrt(jnp.mean(x * x, axis=-1, keepdims=True) + RMS_EPS) * g_ref[...]
    hb = h.astype(BF16)

    def head_norm(z, gain):
        ss = _dot((z * z).astype(BF16), bd_ref[...])
        return z * lax.rsqrt(ss * (1.0 / FOX_HEAD_DIM) + RMS_EPS) * gain

    zq = _dot(hb, wqkv_ref[:, 0:FOX_WIDTH])
    q_ref[...] = (head_norm(zq, qn_ref[...]) * (LOG2E * FOX_HEAD_DIM ** -0.5)).astype(BF16)
    zk = _dot(hb, wqkv_ref[:, FOX_WIDTH:2 * FOX_WIDTH])
    kn = head_norm(zk, kn_ref[...])
    zv = _dot(hb, wqkv_ref[:, 2 * FOX_WIDTH:3 * FOX_WIDTH])
    kb_ref[...] = kn.astype(BF16)
    vb_ref[...] = zv.astype(BF16)
    tm = x_ref.shape[0]
    if kv_transposed:
        k_ref[0] = kn.T
        v_ref[0] = zv.T
    else:
        for hd in range(N_FOX_HEADS):
            hs = slice(hd * FOX_HEAD_DIM, (hd + 1) * FOX_HEAD_DIM)
            rows = pl.ds(hd, tm, stride=N_FOX_HEADS)
            k_ref[rows, :] = kn[:, hs]
            v_ref[rows, :] = zv[:, hs]

    zf = (_dot(hb, wf_ref[...]) + bf_ref[...]).T[0:N_FOX_HEADS, :]
    lf_ref[...] = jnp.minimum(zf, 0.0) - jnp.log1p(jnp.exp(-jnp.abs(zf)))

    zm = _dot(hb, wqm_ref[...])
    for hd in range(N_MEM_HEADS):
        sl = slice(hd * MEM_HEAD_DIM, (hd + 1) * MEM_HEAD_DIM)
        zh = zm[:, sl]
        ms = jnp.mean(zh * zh, axis=-1, keepdims=True)
        qm_ref[:, sl] = (zh * lax.rsqrt(ms + RMS_EPS) * qmn_ref[...] * (MEM_HEAD_DIM ** -0.5)).astype(BF16)

    u_ref[...] = _dot(hb, wu_ref[...])
    for c in range(3):
        sl = slice(c * D_MODEL, (c + 1) * D_MODEL)
        gate_ref[:, sl] = (0.5 * jnp.tanh(0.5 * _dot(hb, wg_ref[:, sl])) + 0.5).astype(BF16)


def _inproj(x2d, w, tm, seq, kv_transposed):
    n = x2d.shape[0]
    assert n % tm == 0
    row = lambda width: pl.BlockSpec((tm, width), lambda i: (i, 0))
    if kv_transposed:
        assert seq % tm == 0
        per = seq // tm
        kv_shape = jax.ShapeDtypeStruct((n // seq, FOX_WIDTH, seq), F32)
        heads = pl.BlockSpec((1, FOX_WIDTH, tm), lambda i: (i // per, 0, i % per))
    else:
        kv_shape = jax.ShapeDtypeStruct((n * N_FOX_HEADS, FOX_HEAD_DIM), F32)
        heads = pl.BlockSpec((tm * N_FOX_HEADS, FOX_HEAD_DIM), lambda i: (i, 0))
    ins = [x2d, w['norm_mix'], w['wqkv'], w['wf'], w['bf'], w['wqm'], w['wu'], w['wg'],
           w['qn_fox'], w['kn_fox'], w['qn_mem'], w['bd']]
    in_specs = [row(D_MODEL)] + [_full(a.shape) for a in ins[1:]]
    out_shape = (
        jax.ShapeDtypeStruct((n, FOX_WIDTH), BF16),
        jax.ShapeDtypeStruct((n, FOX_WIDTH), BF16),
        jax.ShapeDtypeStruct((n, FOX_WIDTH), BF16),
        kv_shape,
        kv_shape,
        jax.ShapeDtypeStruct((N_FOX_HEADS, n), F32),
        jax.ShapeDtypeStruct((n, MEM_WIDTH), BF16),
        jax.ShapeDtypeStruct((n, SSM_WIDTH), F32),
        jax.ShapeDtypeStruct((n, 3 * D_MODEL), BF16),
    )
    out_specs = (row(FOX_WIDTH), row(FOX_WIDTH), row(FOX_WIDTH), heads, heads,
                 pl.BlockSpec((N_FOX_HEADS, tm), lambda i: (0, i)),
                 row(MEM_WIDTH), row(SSM_WIDTH), row(3 * D_MODEL))
    return pl.pallas_call(
        functools.partial(_inproj_kernel, kv_transposed=kv_transposed),
        out_shape=out_shape, grid=(n // tm,), in_specs=in_specs, out_specs=out_specs,
        compiler_params=_params(("parallel",)), name="inproj")(*ins)


def _memkv_kernel(x_ref, g_ref, w_ref, kn_ref, k_ref, v_ref):
    x = x_ref[...]
    h = x * lax.rsqrt(jnp.mean(x * x, axis=-1, keepdims=True) + RMS_EPS) * g_ref[...]
    hb = h.astype(BF16)
    tm = x_ref.shape[0]
    zk = _dot(hb, w_ref[:, 0:MEM_WIDTH])
    zv = _dot(hb, w_ref[:, MEM_WIDTH:2 * MEM_WIDTH])
    for hd in range(N_MEM_HEADS):
        sl = slice(hd * MEM_HEAD_DIM, (hd + 1) * MEM_HEAD_DIM)
        rows = pl.ds(hd, tm, stride=N_MEM_HEADS)
        zh = zk[:, sl]
        ms = jnp.mean(zh * zh, axis=-1, keepdims=True)
        k_ref[rows, :] = zh * lax.rsqrt(ms + RMS_EPS) * kn_ref[...]
        v_ref[rows, :] = zv[:, sl]


def _memkv(mem2d, norm_mem, w_kv, kn_mem, tm):
    n = mem2d.shape[0]
    out = jax.ShapeDtypeStruct((n * N_MEM_HEADS, MEM_HEAD_DIM), F32)
    ospec = pl.BlockSpec((tm * N_MEM_HEADS, MEM_HEAD_DIM), lambda i: (i, 0))
    return pl.pallas_call(
        _memkv_kernel,
        out_shape=(out, out),
        grid=(n // tm,),
        in_specs=[pl.BlockSpec((tm, D_MODEL), lambda i: (i, 0)), _full(norm_mem.shape), _full(w_kv.shape),
                  _full(kn_mem.shape)],
        out_specs=(ospec, ospec),
        compiler_params=_params(("parallel",)), name="memkv")(mem2d, norm_mem, w_kv, kn_mem)


CUMSUM_BLOCK = 256


def _cumsum_kernel(x_ref, o_ref):
    nblk = x_ref.shape[1] // CUMSUM_BLOCK
    r = lax.broadcasted_iota(jnp.int32, (CUMSUM_BLOCK, CUMSUM_BLOCK), 0)
    c = lax.broadcasted_iota(jnp.int32, (CUMSUM_BLOCK, CUMSUM_BLOCK), 1)
    tri = (r <= c).astype(F32)
    carry = jnp.zeros((x_ref.shape[0], 1), F32)
    for j in range(nblk):
        sl = slice(j * CUMSUM_BLOCK, (j + 1) * CUMSUM_BLOCK)
        cs = _dot_exact(x_ref[:, sl], tri) + carry
        o_ref[:, sl] = cs
        carry = cs[:, CUMSUM_BLOCK - 1:CUMSUM_BLOCK]


def _cumsum_rows(x):
    rows, n = x.shape
    npad = -(-n // CUMSUM_BLOCK) * CUMSUM_BLOCK
    xp = jnp.pad(x, ((0, 0), (0, npad - n))) if npad != n else x
    out = pl.pallas_call(
        _cumsum_kernel, out_shape=jax.ShapeDtypeStruct((rows, npad), F32), grid=(1,),
        in_specs=[_full((rows, npad))], out_specs=_full((rows, npad)),
        compiler_params=_params(("arbitrary",)), name="cumsum")(xp)
    return out[:, :n] if npad != n else out


def _reduce_rows(x, op):
    rows, cols = x.shape
    if rows > 64 and rows % 64 == 0:
        x = op(x.reshape(rows // 64, 64, cols), axis=0)
        rows = 64
    if rows == 64:
        x = op(x.reshape(8, 8, cols), axis=0)
    return op(x, axis=0, keepdims=True)


def _fox_prompt_kernel(q_ref, k_ref, v_ref, cr_ref, o_ref,
                       vt_ref, ck0_ref, ck1_ref, st0_ref, st1_ref, pt0_ref, pt1_ref, acc_ref, *, tq, tk):
    st_refs = (st0_ref, st1_ref)
    pt_refs = (pt0_ref, pt1_ref)
    i = pl.program_id(2)
    s_len = k_ref.shape[1]

    @pl.when(i == 0)
    def _():
        vt_ref[...] = v_ref[0].astype(F32).T.astype(BF16)
        ck0_ref[...] = jnp.broadcast_to(cr_ref[0, 0, 0:1, :], (LANES, s_len)).T
        ck1_ref[...] = jnp.broadcast_to(cr_ref[0, 0, 1:2, :], (LANES, s_len)).T

    def q_heads_t(blk):
        qt = q_ref[0, pl.ds(pl.multiple_of(blk * tq, tq), tq), :].astype(F32).T
        row = lax.broadcasted_iota(jnp.int32, (LANES, tq), 0)
        return (jnp.where(row < FOX_HEAD_DIM, qt, 0.0).astype(BF16),
                jnp.where(row < FOX_HEAD_DIM, 0.0, qt).astype(BF16))

    qts = q_heads_t(i)
    q0 = pl.multiple_of(i * tq, tq)
    cq = cr_ref[0, 0, :, pl.ds(q0, tq)]
    ck_refs = (ck0_ref, ck1_ref)

    def stage_a(n, par, qts=qts):
        s = pl.multiple_of(n * tk, tk)
        kb = k_ref[0, pl.ds(s, tk), :]
        for hh in range(2):
            ck = ck_refs[hh][pl.ds(s, tk), :]
            st_refs[par][hh] = _dot(kb, qts[hh]) - jnp.concatenate([ck] * (tq // LANES), axis=1)

    def stage_b(n, par, stats, masked):
        if masked:
            kpos = n * tk + lax.broadcasted_iota(jnp.int32, (tk, tq), 0)
            qpos = q0 + lax.broadcasted_iota(jnp.int32, (tk, tq), 1)
            mask = kpos <= qpos
        out = []
        for hh in range(2):
            m, l = stats[2 * hh:2 * hh + 2]
            t = st_refs[par][hh]
            if masked:
                t = jnp.where(mask, t, NEG_INF)
            cqh = cq[hh:hh + 1, :]
            m_new = jnp.maximum(m, _reduce_rows(t, jnp.max) + cqh)
            alpha = jnp.exp2(m - m_new)
            p = jnp.exp2(t + (cqh - m_new))
            pt_refs[par][hh] = p.astype(BF16)
            out.extend([m_new, alpha * l + _reduce_rows(p, jnp.sum), alpha])
        return tuple(out)

    def stage_c(n, par, alphas):
        s = pl.multiple_of(jnp.maximum(n, 0) * tk, tk)
        for hh in range(2):
            vt = vt_ref[hh * FOX_HEAD_DIM:(hh + 1) * FOX_HEAD_DIM, pl.ds(s, tk)]
            acc_ref[hh] = alphas[hh] * acc_ref[hh] + _dot(vt, pt_refs[par][hh])

    def iteration(n, par, carry):
        m0, l0, al0, m1, l1, al1 = carry
        stage_c(n - 1, 1 - par, (al0, al1))
        new = stage_b(n, par, (m0, l0, m1, l1), False)
        stage_a(n + 1, 1 - par)
        return new

    def finish(par, carry):
        m0, l0, al0, m1, l1, al1 = carry
        stage_c(nfull - 1, 1 - par, (al0, al1))
        _, l0, be0, _, l1, be1 = stage_b(nfull, par, (m0, l0, m1, l1), True)
        stage_a(0, 0, q_heads_t(jnp.minimum(i + 1, pl.num_programs(2) - 1)))
        stage_c(nfull, par, (be0, be1))
        ot = jnp.concatenate([acc_ref[0] / l0, acc_ref[1] / l1], axis=0)
        o_ref[0] = ot.T.astype(o_ref.dtype)

    acc_ref[...] = jnp.zeros(acc_ref.shape, F32)
    pt1_ref[...] = jnp.zeros(pt1_ref.shape, BF16)
    neg = jnp.full((1, tq), NEG_INF, F32)
    zero = jnp.zeros((1, tq), F32)
    one = jnp.ones((1, tq), F32)
    nfull = (i * tq) // tk
    pl.when(i == 0)(lambda: stage_a(0, 0))
    carry = lax.fori_loop(0, nfull // 2, lambda k, c: iteration(2 * k + 1, 1, iteration(2 * k, 0, c)),
                          (neg, zero, one, neg, zero, one))
    odd = nfull % 2 == 1
    carry = lax.cond(odd, lambda c: iteration(nfull - 1, 0, c), lambda c: c, carry)
    pl.when(odd)(lambda: finish(1, carry))
    pl.when(jnp.logical_not(odd))(lambda: finish(0, carry))


def _fox_prompt(q, k, v, c_row, tq, tk):
    b, s, _ = q.shape
    assert s % tk == 0 and tk % tq == 0
    npair = N_FOX_HEADS // 2
    return pl.pallas_call(
        functools.partial(_fox_prompt_kernel, tq=tq, tk=tk),
        out_shape=jax.ShapeDtypeStruct((b, s, FOX_WIDTH), BF16),
        grid=(b, npair, s // tq),
        in_specs=[
            pl.BlockSpec((1, s, LANES), lambda bi, hp, i: (bi, 0, hp)),
            pl.BlockSpec((1, s, LANES), lambda bi, hp, i: (bi, 0, hp)),
            pl.BlockSpec((1, s, LANES), lambda bi, hp, i: (bi, 0, hp)),
            pl.BlockSpec((1, 1, 2, s), lambda bi, hp, i: (bi, hp, 0, 0)),
        ],
        out_specs=pl.BlockSpec((1, tq, LANES), lambda bi, hp, i: (bi, i, hp)),
        scratch_shapes=[pltpu.VMEM((LANES, s), BF16),
                        pltpu.VMEM((s, LANES), F32), pltpu.VMEM((s, LANES), F32),
                        pltpu.VMEM((2, tk, tq), F32), pltpu.VMEM((2, tk, tq), F32),
                        pltpu.VMEM((2, tk, tq), BF16), pltpu.VMEM((2, tk, tq), BF16),
                        pltpu.VMEM((2, FOX_HEAD_DIM, tq), F32)],
        compiler_params=_params(("parallel", "parallel", "arbitrary")), name="fox_prompt")(q, k, v, c_row)


def _fox_sample_kernel(q_ref, ck_ref, cv_ref, nk_ref, nv_ref, cq_ref, crc_ref, crn_ref, o_ref, *state, n):
    j = pl.program_id(1)
    nj = pl.num_programs(1)
    m_refs = state[0:N_FOX_HEADS]
    l_refs = state[N_FOX_HEADS:2 * N_FOX_HEADS]
    acc_refs = state[2 * N_FOX_HEADS:3 * N_FOX_HEADS]

    @pl.when(j == 0)
    def _():
        for hd in range(N_FOX_HEADS):
            m_refs[hd][...] = jnp.full(m_refs[hd].shape, NEG_INF, F32)
            l_refs[hd][...] = jnp.zeros(l_refs[hd].shape, F32)
            acc_refs[hd][...] = jnp.zeros(acc_refs[hd].shape, F32)

    def update(k_of, v_of, cr_ref_, mask, transposed):
        qk = _dot if transposed else _dot_nt
        pv = _dot_nt if transposed else _dot
        ts = []
        for hd in range(N_FOX_HEADS):
            hs = slice(hd * FOX_HEAD_DIM, (hd + 1) * FOX_HEAD_DIM)
            t = qk(q_ref[0, :, hs], k_of(hd)) - cr_ref_[0, hd:hd + 1, :]
            ts.append(t if mask is None else jnp.where(mask, t, NEG_INF))
        ps = []
        for hd in range(N_FOX_HEADS):
            cq = cq_ref[0, :, hd:hd + 1]
            m = m_refs[hd][...]
            m_new = jnp.maximum(m, jnp.max(ts[hd], axis=-1, keepdims=True) + cq)
            alpha = jnp.exp2(m - m_new)
            p = jnp.exp2(ts[hd] + (cq - m_new))
            m_refs[hd][...] = m_new
            l_refs[hd][...] = alpha * l_refs[hd][...] + jnp.sum(p, axis=-1, keepdims=True)
            ps.append((alpha, p.astype(BF16)))
        for hd in range(N_FOX_HEADS):
            alpha, p = ps[hd]
            acc_refs[hd][...] = alpha * acc_refs[hd][...] + pv(p, v_of(hd))

    update(lambda hd: ck_ref[0, hd].astype(BF16), lambda hd: cv_ref[0, hd].astype(BF16), crc_ref, None, True)

    @pl.when(j == nj - 1)
    def _():
        r = lax.broadcasted_iota(jnp.int32, (n, n), 0)
        c = lax.broadcasted_iota(jnp.int32, (n, n), 1)
        head = lambda ref: (lambda hd: ref[0, :, hd * FOX_HEAD_DIM:(hd + 1) * FOX_HEAD_DIM])
        update(head(nk_ref), head(nv_ref), crn_ref, c <= r, False)
        for hd in range(N_FOX_HEADS):
            hs = slice(hd * FOX_HEAD_DIM, (hd + 1) * FOX_HEAD_DIM)
            o_ref[0, :, hs] = (acc_refs[hd][...] / l_refs[hd][...]).astype(o_ref.dtype)


def _fox_sample(q, cache_k, cache_v, k_new, v_new, c_q, c_row_cache, c_row_new, tk):
    b, n, _ = q.shape
    past = cache_k.shape[3]
    assert past % tk == 0
    cache_spec = pl.BlockSpec((1, N_FOX_HEADS, FOX_HEAD_DIM, tk), lambda bi, j: (bi, 0, 0, j))
    return pl.pallas_call(
        functools.partial(_fox_sample_kernel, n=n),
        out_shape=jax.ShapeDtypeStruct((b, n, FOX_WIDTH), BF16),
        grid=(b, past // tk),
        in_specs=[
            pl.BlockSpec((1, n, FOX_WIDTH), lambda bi, j: (bi, 0, 0)),
            cache_spec,
            cache_spec,
            pl.BlockSpec((1, n, FOX_WIDTH), lambda bi, j: (bi, 0, 0)),
            pl.BlockSpec((1, n, FOX_WIDTH), lambda bi, j: (bi, 0, 0)),
            pl.BlockSpec((1, n, N_FOX_HEADS), lambda bi, j: (bi, 0, 0)),
            pl.BlockSpec((1, N_FOX_HEADS, tk), lambda bi, j: (bi, 0, j)),
            pl.BlockSpec((1, N_FOX_HEADS, n), lambda bi, j: (bi, 0, 0)),
        ],
        out_specs=pl.BlockSpec((1, n, FOX_WIDTH), lambda bi, j: (bi, 0, 0)),
        scratch_shapes=([pltpu.VMEM((n, 1), F32)] * (2 * N_FOX_HEADS)
                        + [pltpu.VMEM((n, FOX_HEAD_DIM), F32)] * N_FOX_HEADS),
        compiler_params=_params(("parallel", "arbitrary")), name="fox_sample")(
            q, cache_k, cache_v, k_new, v_new, c_q, c_row_cache, c_row_new)


def _ssm_mats(p):
    f32 = F32
    a_re, a_im = p['ssm_a_re'].astype(f32), p['ssm_a_im'].astype(f32)
    b_re, b_im = p['ssm_b_re'].astype(f32), p['ssm_b_im'].astype(f32)
    c_re, c_im = p['ssm_c_re'].astype(f32), p['ssm_c_im'].astype(f32)
    dt = jnp.exp(p['ssm_log_dt'].astype(f32))[:, None]
    mag = jnp.exp(dt * a_re)
    ab_re = mag * jnp.cos(dt * a_im)
    ab_im = mag * jnp.sin(dt * a_im)
    den = a_re * a_re + a_im * a_im
    nr, ni = ab_re - 1.0, ab_im
    coef_re = (nr * a_re + ni * a_im) / den
    coef_im = (ni * a_re - nr * a_im) / den
    bb_re = coef_re[..., None] * b_re - coef_im[..., None] * b_im
    bb_im = coef_re[..., None] * b_im + coef_im[..., None] * b_re
    pr, pi = [jnp.ones_like(ab_re)], [jnp.zeros_like(ab_im)]
    for _ in range(SSM_CHUNK):
        pr.append(pr[-1] * ab_re - pi[-1] * ab_im)
        pi.append(pr[-2] * ab_im + pi[-1] * ab_re)
    pw_re, pw_im = jnp.stack(pr), jnp.stack(pi)
    T = SSM_CHUNK
    w_re = pw_re[..., None] * bb_re[None] - pw_im[..., None] * bb_im[None]
    w_im = pw_re[..., None] * bb_im[None] + pw_im[..., None] * bb_re[None]
    kk = (jnp.einsum('gop,kgpi->kgoi', c_re, w_re[:T], precision='highest')
          - jnp.einsum('gop,kgpi->kgoi', c_im, w_im[:T], precision='highest'))
    nq = N_SSM_GROUPS // SSM_GPB

    def group_diag(m):
        rows, c = m.shape[-2:]
        m = jnp.tile(m, (1,) * (m.ndim - 1) + (SSM_GPB,))
        same = (jnp.arange(rows) // (rows // SSM_GPB))[:, None] == (jnp.arange(SSM_GPB * c) // c)[None, :]
        return jnp.where(same, m, 0.0)

    def lane_diag(m):
        lead = m.shape[:-3]
        i, c = m.shape[-2:]
        return group_diag(m.reshape(lead + (nq, SSM_GPB * i, c)))

    ktau = lane_diag(jnp.swapaxes(kk, -1, -2))
    ktau = jnp.concatenate([jnp.zeros_like(ktau[:1]), ktau], axis=0)
    units = []
    for dlag in range(T // 2 - 1, -1, -1):
        top = jnp.concatenate([ktau[2 * dlag + 1], ktau[2 * dlag + 2]], axis=-1)
        bot = jnp.concatenate([ktau[2 * dlag], ktau[2 * dlag + 1]], axis=-1)
        units.append(jnp.concatenate([top, bot], axis=-2))
    kstack = jnp.concatenate(units, axis=-2).astype(BF16)
    rev = T - 1 - jnp.arange(T)
    def local_rows(w):
        w = jnp.transpose(w[rev], (1, 0, 3, 2)).reshape(nq, SSM_GPB, T, SSM_GROUP, SSM_STATE)
        return group_diag(jnp.transpose(w, (0, 2, 1, 3, 4)).reshape(nq, T, LANES, SSM_STATE))

    m_all = jnp.concatenate([local_rows(w_re), local_rows(w_im)], axis=-1)
    m_all = m_all.reshape(nq, T * LANES, 2 * SSM_GPB * SSM_STATE)
    m_hi, m_lo = _split_bf16(m_all)
    ar, ai = pw_re[1:], pw_im[1:]
    n_re = (c_re[None] * ar[:, :, None, :] - c_im[None] * ai[:, :, None, :])
    n_im = -(c_re[None] * ai[:, :, None, :] + c_im[None] * ar[:, :, None, :])

    def state_rows(n):
        n = jnp.transpose(n, (1, 3, 0, 2)).reshape(nq, SSM_GPB * SSM_STATE, T, SSM_GROUP)
        n = jnp.tile(jnp.transpose(n, (0, 2, 1, 3)), (1, 1, 1, SSM_GPB))
        same = (jnp.arange(SSM_GPB * SSM_STATE) // SSM_STATE)[:, None] == (jnp.arange(LANES) // SSM_GROUP)[None, :]
        return jnp.where(same, n, 0.0)

    n_all = jnp.concatenate([state_rows(n_re), state_rows(n_im)], axis=2).astype(BF16)
    return dict(kstack=kstack, m_hi=m_hi, m_lo=m_lo, n_all=n_all,
                a16_re=pw_re[T].reshape(STATE_TILE), a16_im=pw_im[T].reshape(STATE_TILE),
                d=p['ssm_d'].astype(f32).reshape(1, SSM_WIDTH))


def _chunk_tokens(u_ref, rows):
    return [u_ref[pl.ds(t, rows, stride=SSM_CHUNK), :] for t in range(SSM_CHUNK)]


def _ssm_local_kernel(u_ref, mh_ref, ml_ref, hre_ref, him_ref):
    rows = hre_ref.shape[0]
    parts = [_split_bf16(ut) for ut in _chunk_tokens(u_ref, rows)]
    x_hi = jnp.concatenate([h for h, _ in parts], axis=1)
    x_lo = jnp.concatenate([l for _, l in parts], axis=1)
    h = _dot(x_hi, mh_ref[0]) + _dot(x_hi, ml_ref[0]) + _dot(x_lo, mh_ref[0])
    half = SSM_GPB * SSM_STATE
    hre_ref[...] = h[:, 0:half]
    him_ref[...] = h[:, half:2 * half]


def _ssm_local(u2d, mats, rows):
    n = u2d.shape[0]
    r = n // SSM_CHUNK
    nq = N_SSM_GROUPS // SSM_GPB
    half = SSM_GPB * SSM_STATE
    mspec = pl.BlockSpec((1, SSM_CHUNK * LANES, 2 * half), lambda q, i: (q, 0, 0))
    ospec = pl.BlockSpec((rows, half), lambda q, i: (i, q))
    return pl.pallas_call(
        _ssm_local_kernel,
        out_shape=(jax.ShapeDtypeStruct((r, N_SSM_GROUPS * SSM_STATE), F32),) * 2,
        grid=(nq, r // rows),
        in_specs=[pl.BlockSpec((rows * SSM_CHUNK, LANES), lambda q, i: (i, q)), mspec, mspec],
        out_specs=(ospec, ospec),
        compiler_params=_params(("parallel", "parallel")), name="ssm_local")(u2d, mats['m_hi'], mats['m_lo'])


def _ssm_scan_kernel(lre_ref, lim_ref, are_ref, aim_ref, h0re_ref, h0im_ref,
                     pre_ref, pim_ref, fre_ref, fim_ref):
    nb, nchunk = lre_ref.shape[:2]
    ar, ai = are_ref[...], aim_ref[...]

    def body(c, carry):
        out = []
        for bi in range(nb):
            hr, hi = carry[2 * bi:2 * bi + 2]
            pre_ref[bi, c] = hr
            pim_ref[bi, c] = hi
            out.extend([ar * hr - ai * hi + lre_ref[bi, c], ar * hi + ai * hr + lim_ref[bi, c]])
        return tuple(out)

    init = []
    for bi in range(nb):
        init.extend([h0re_ref[bi], h0im_ref[bi]])
    final = lax.fori_loop(0, nchunk, body, tuple(init))
    for bi in range(nb):
        fre_ref[bi] = final[2 * bi]
        fim_ref[bi] = final[2 * bi + 1]


def _ssm_scan(hloc_re, hloc_im, mats, h0_re, h0_im):
    b, nchunk = hloc_re.shape[:2]
    nb = _pick_tile(b, max(1, SCAN_CHUNK_ROWS // nchunk))
    big = pl.BlockSpec((nb, nchunk) + STATE_TILE, lambda i: (i, 0, 0, 0))
    small = pl.BlockSpec((nb,) + STATE_TILE, lambda i: (i, 0, 0))
    return pl.pallas_call(
        _ssm_scan_kernel,
        out_shape=(jax.ShapeDtypeStruct(hloc_re.shape, F32),) * 2 + (jax.ShapeDtypeStruct((b,) + STATE_TILE, F32),) * 2,
        grid=(b // nb,),
        in_specs=[big, big, _full(STATE_TILE), _full(STATE_TILE), small, small],
        out_specs=(big, big, small, small),
        compiler_params=_params(("parallel",)), name="ssm_scan")(
            hloc_re, hloc_im, mats['a16_re'], mats['a16_im'], h0_re, h0_im)


def _gelu_tanh(y):
    return 0.5 * y * (1.0 + jnp.tanh(math.sqrt(2.0 / math.pi) * (y + 0.044715 * (y * y * y))))


def _ssm_out_kernel(u_ref, k_ref, pre_ref, pim_ref, n_ref, d_ref, y_ref, ysc_ref):
    rows = pre_ref.shape[0]
    us = _chunk_tokens(u_ref, rows)
    x = jnp.concatenate([ut.astype(BF16) for ut in us], axis=1)
    hp = jnp.concatenate([pre_ref[...], pim_ref[...]], axis=1).astype(BF16)
    unit = 2 * LANES
    nunit = SSM_CHUNK // 2
    for j in range(nunit):
        n_unit = jnp.concatenate([n_ref[0, 2 * j], n_ref[0, 2 * j + 1]], axis=1)
        yj = _dot(x[:, 0:unit * (j + 1)], k_ref[0, unit * (nunit - 1 - j):, :]) + _dot(hp, n_unit)
        for t2 in range(2):
            t = 2 * j + t2
            y = yj[:, t2 * LANES:(t2 + 1) * LANES] + d_ref[...] * us[t]
            ysc_ref[pl.ds(t, rows, stride=SSM_CHUNK), :] = _gelu_tanh(y)
    y_ref[...] = ysc_ref[...].astype(y_ref.dtype)


def _ssm_out(u2d, hprev_re, hprev_im, mats, rows):
    n = u2d.shape[0]
    r = n // SSM_CHUNK
    nq = N_SSM_GROUPS // SSM_GPB
    half = SSM_GPB * SSM_STATE
    uspec = pl.BlockSpec((rows * SSM_CHUNK, LANES), lambda q, i: (i, q))
    hspec = pl.BlockSpec((rows, half), lambda q, i: (i, q))
    return pl.pallas_call(
        _ssm_out_kernel,
        out_shape=jax.ShapeDtypeStruct((n, SSM_WIDTH), BF16),
        grid=(nq, r // rows),
        in_specs=[uspec, pl.BlockSpec((1, SSM_CHUNK * LANES, 2 * LANES), lambda q, i: (q, 0, 0)), hspec, hspec,
                  pl.BlockSpec((1, SSM_CHUNK, 2 * half, LANES), lambda q, i: (q, 0, 0, 0)),
                  pl.BlockSpec((1, LANES), lambda q, i: (0, q))],
        out_specs=uspec,
        scratch_shapes=[pltpu.VMEM((rows * SSM_CHUNK, LANES), F32)],
        compiler_params=_params(("parallel", "parallel")), name="ssm_out")(
            u2d, mats['kstack'], hprev_re, hprev_im, mats['n_all'], mats['d'])


def _ssm(u2d, b, h0_re, h0_im, mats):
    n = u2d.shape[0]
    nchunk = n // b // SSM_CHUNK
    r = b * nchunk
    rows = _pick_tile(r, TILE_SSM_ROWS)
    hloc_re, hloc_im = _ssm_local(u2d, mats, rows)
    shp = (b, nchunk) + STATE_TILE
    hprev_re, hprev_im, f_re, f_im = _ssm_scan(hloc_re.reshape(shp), hloc_im.reshape(shp), mats,
                                               h0_re.reshape((b,) + STATE_TILE), h0_im.reshape((b,) + STATE_TILE))
    y = _ssm_out(u2d, hprev_re.reshape(r, -1), hprev_im.reshape(r, -1), mats, rows)
    return y, f_re.reshape(b, N_SSM_GROUPS, SSM_STATE), f_im.reshape(b, N_SSM_GROUPS, SSM_STATE)


def _merge_kernel(x_ref, of_ref, ys_ref, qm_ref, gate_ref, mk_ref, mv_ref,
                  wglu_ref, wbf_ref, wbs_ref, wbm_ref, wo_ref, nf_ref, wr_ref,
                  x1_ref, h2_ref, r_ref):
    tm = x_ref.shape[0]
    nb = mk_ref.shape[0]
    rows = tm // nb
    om = []
    for hd in range(N_MEM_HEADS):
        sl = slice(hd * MEM_HEAD_DIM, (hd + 1) * MEM_HEAD_DIM)
        head_rows = pl.ds(hd, N_MEM, stride=N_MEM_HEADS)
        per_batch = []
        for bi in range(nb):
            kh = mk_ref[bi, head_rows, :].astype(BF16)
            vh = mv_ref[bi, head_rows, :].astype(BF16)
            sc = _dot_nt(qm_ref[bi * rows:(bi + 1) * rows, sl], kh)
            p = jnp.exp(sc - jnp.max(sc, axis=-1, keepdims=True))
            per_batch.append(_dot(p.astype(BF16), vh) / jnp.sum(p, axis=-1, keepdims=True))
        om.append(per_batch[0] if nb == 1 else jnp.concatenate(per_batch, axis=0))
    o_mem = jnp.concatenate(om, axis=-1).astype(BF16)
    z = _dot(ys_ref[...], wglu_ref[...])
    y_ssm = (z[:, 0:SSM_WIDTH] * jax.nn.sigmoid(z[:, SSM_WIDTH:2 * SSM_WIDTH])).astype(BF16)
    g = lambda c: gate_ref[:, c * D_MODEL:(c + 1) * D_MODEL].astype(F32)
    merged = (g(0) * _dot(of_ref[...], wbf_ref[...]) + g(1) * _dot(y_ssm, wbs_ref[...])
              + g(2) * _dot(o_mem, wbm_ref[...]))
    x1 = x_ref[...] + _dot(merged.astype(BF16), wo_ref[...])
    x1_ref[...] = x1
    h2 = x1 * lax.rsqrt(jnp.mean(x1 * x1, axis=-1, keepdims=True) + RMS_EPS) * nf_ref[...]
    h2_ref[...] = h2.astype(BF16)
    h2_hi, h2_lo = _split_bf16(h2)
    hw = _dot(h2_hi, wr_ref[...])
    logits = hw[:, 0:LANES] + hw[:, LANES:2 * LANES] + _dot(h2_lo, wr_ref[:, 0:LANES])
    lane = lax.broadcasted_iota(jnp.int32, (tm, LANES), 1)
    big = jnp.int32(LANES)
    is_grp = (lane >= N_EXPERTS) & (lane < N_EXPERTS + N_EXPERT_GROUPS)
    gl = jnp.where(is_grp, logits, NEG_INF)
    gmax = jnp.max(gl, axis=-1, keepdims=True)
    grp = jnp.min(jnp.where(is_grp & (gl == gmax), lane, big), axis=-1, keepdims=True) - N_EXPERTS
    g_w = 1.0 / jnp.sum(jnp.where(is_grp, jnp.exp(gl - gmax), 0.0), axis=-1, keepdims=True)
    in_grp = (lane >= grp * EXPERTS_PER_GROUP) & (lane < (grp + 1) * EXPERTS_PER_GROUP)
    e1 = jnp.where(in_grp, logits, NEG_INF)
    m1 = jnp.max(e1, axis=-1, keepdims=True)
    i1 = jnp.min(jnp.where(in_grp & (e1 == m1), lane, big), axis=-1, keepdims=True)
    rest = in_grp & (lane != i1)
    e2 = jnp.where(rest, logits, NEG_INF)
    m2 = jnp.max(e2, axis=-1, keepdims=True)
    i2 = jnp.min(jnp.where(rest & (e2 == m2), lane, big), axis=-1, keepdims=True)
    ex = jnp.exp(m2 - m1)
    w1 = g_w / (1.0 + ex)
    w2 = g_w * ex / (1.0 + ex)
    r_ref[...] = jnp.where(lane == i1, w1, jnp.where(lane == i2, w2, jnp.where(lane == GROUP_LANE, grp.astype(F32), 0.0)))


def _merge(x2d, o_fox, ys, q_m, gates, mem_k, mem_v, w, tm, rows_per_batch):
    n = x2d.shape[0]
    assert n % tm == 0 and (rows_per_batch % tm == 0 or tm % rows_per_batch == 0)
    row = lambda width: pl.BlockSpec((tm, width), lambda i: (i, 0))
    if rows_per_batch >= tm:
        per = rows_per_batch // tm
        memspec = pl.BlockSpec((1, N_MEM * N_MEM_HEADS, MEM_HEAD_DIM), lambda i: (i // per, 0, 0))
    else:
        memspec = pl.BlockSpec((tm // rows_per_batch, N_MEM * N_MEM_HEADS, MEM_HEAD_DIM), lambda i: (i, 0, 0))
    ws = [w['w_glu'], w['w_br_fox'], w['w_br_ssm'], w['w_br_mem'], w['w_out'], w['norm_ffn'], w['w_router']]
    return pl.pallas_call(
        _merge_kernel,
        out_shape=(jax.ShapeDtypeStruct((n, D_MODEL), F32), jax.ShapeDtypeStruct((n, D_MODEL), BF16),
                   jax.ShapeDtypeStruct((n, LANES), F32)),
        grid=(n // tm,),
        in_specs=[row(D_MODEL), row(FOX_WIDTH), row(SSM_WIDTH), row(MEM_WIDTH), row(3 * D_MODEL), memspec, memspec]
                 + [_full(a.shape) for a in ws],
        out_specs=(row(D_MODEL), row(D_MODEL), row(LANES)),
        compiler_params=_params(("parallel",)), name="merge")(
            x2d, o_fox, ys, q_m, gates, mem_k, mem_v, *ws)


def _moe_kernel(h_ref, r_ref, x1_ref, tri_ref, wg_ref, wu_ref, wd_ref, o_ref,
                xs_ref, cw_ref, og_ref, acc_ref, rank_ref, count_ref, rrow_ref, hilo_ref, *, main):
    step = pl.program_id(1)
    steps_per_group = EXPERTS_PER_GROUP // MOE_EPS
    g = step // steps_per_group
    tm = h_ref.shape[0]
    gf = g.astype(F32)
    bounds = [0, main] + list(range(-(-main // MOE_SUB) * MOE_SUB, tm, MOE_SUB)) + [tm]
    bounds = sorted(set(bounds))
    blocks = [(lo, hi - lo, lo > 0) for lo, hi in zip(bounds[:-1], bounds[1:])]

    def guarded(r0, fn):
        if r0 == 0:
            fn()
        else:
            pl.when(r0 < count_ref[0])(fn)

    @pl.when(step == 0)
    def _():
        acc_ref[...] = jnp.zeros(acc_ref.shape, F32)
        rt = r_ref[...]
        grow = rt.T[GROUP_LANE:GROUP_LANE + 1, :]
        member = jnp.where(grow == lax.broadcasted_iota(jnp.int32, (8, tm), 0).astype(F32), 1.0, 0.0)
        rank8 = _dot(member.astype(BF16), tri_ref[...])
        own = jnp.sum(member * rank8, axis=0, keepdims=True)
        rrow_ref[...] = jnp.concatenate([own, grow, jnp.zeros((6, tm), F32)], axis=0)
        rank_ref[...] = jnp.broadcast_to(jnp.broadcast_to(own, (8, tm)).T[:, 0:1], rank_ref.shape)
        hilo_ref[...] = jnp.concatenate(_split_bf16(rt), axis=1)

    @pl.when(step % steps_per_group == 0)
    def _():
        mrow = rrow_ref[1:2, :] == gf
        rank_row = jnp.where(mrow, rrow_ref[0:1, :], -1.0)
        count_ref[0] = jnp.sum(jnp.where(mrow, 1, 0))
        hilo = hilo_ref[...]
        for r0, nrows, _ in blocks:
            def compact(r0=r0, nrows=nrows):
                rows = slice(r0, r0 + nrows)
                slot = r0 + lax.broadcasted_iota(jnp.int32, (nrows, tm), 0)
                perm = jnp.where(rank_row == slot.astype(F32), 1.0, 0.0).astype(BF16)
                xs_ref[rows, :] = _dot(perm, h_ref[...]).astype(BF16)
                cw = _dot(perm, hilo)
                cw_ref[rows, :] = cw[:, 0:LANES] + cw[:, LANES:2 * LANES]
                og_ref[rows, :] = jnp.zeros((nrows, D_MODEL), F32)
            guarded(r0, compact)

    for k in range(MOE_EPS):
        e = step * MOE_EPS + k
        for r0, nrows, _ in blocks:
            def expert(r0=r0, nrows=nrows, k=k, e=e):
                rows = slice(r0, r0 + nrows)
                x = xs_ref[rows, :]
                a = _dot(x, wg_ref[k])
                up = _dot(x, wu_ref[k])
                lane = lax.broadcasted_iota(jnp.int32, (nrows, LANES), 1)
                ce = jnp.sum(jnp.where(lane == e, cw_ref[rows, :], 0.0), axis=-1, keepdims=True)
                act = a * jax.nn.sigmoid(a) * up * ce
                og_ref[rows, :] += _dot(act.astype(BF16), wd_ref[k])
            guarded(r0, expert)

    @pl.when(step % steps_per_group == steps_per_group - 1)
    def _():
        for r0, nrows, _ in blocks:
            def scatter_back(r0=r0, nrows=nrows):
                rows = slice(r0, r0 + nrows)
                slot = r0 + lax.broadcasted_iota(jnp.int32, (tm, nrows), 1)
                mine = (rank_ref[:, 0:1] == slot.astype(F32)) & (r_ref[:, GROUP_LANE:GROUP_LANE + 1] == gf)
                back = jnp.where(mine, 1.0, 0.0).astype(BF16)
                acc_ref[...] += _dot(back, og_ref[rows, :].astype(BF16))
            guarded(r0, scatter_back)

    @pl.when(step == pl.num_programs(1) - 1)
    def _():
        o_ref[...] = x1_ref[...] + acc_ref[...]


def _moe(h2, route, x1, wg, wu, wd, tm):
    n = h2.shape[0]
    assert n % tm == 0 and tm % MOE_SUB == 0
    main = max(MOE_SUB // 2, (9 * tm // 32) // 32 * 32)
    row = lambda width: pl.BlockSpec((tm, width), lambda i, s: (i, 0))
    r = jnp.arange(tm)
    tri = (r[:, None] < r[None, :]).astype(BF16)
    return pl.pallas_call(
        functools.partial(_moe_kernel, main=main),
        out_shape=jax.ShapeDtypeStruct((n, D_MODEL), F32),
        grid=(n // tm, N_EXPERTS // MOE_EPS),
        in_specs=[row(D_MODEL), row(LANES), row(D_MODEL), pl.BlockSpec((tm, tm), lambda i, s: (0, 0)),
                  pl.BlockSpec((MOE_EPS, D_MODEL, D_EXPERT), lambda i, s: (s, 0, 0)),
                  pl.BlockSpec((MOE_EPS, D_MODEL, D_EXPERT), lambda i, s: (s, 0, 0)),
                  pl.BlockSpec((MOE_EPS, D_EXPERT, D_MODEL), lambda i, s: (s, 0, 0))],
        out_specs=row(D_MODEL),
        scratch_shapes=[pltpu.VMEM((tm, D_MODEL), BF16), pltpu.VMEM((tm, LANES), F32), pltpu.VMEM((tm, D_MODEL), F32),
                        pltpu.VMEM((tm, D_MODEL), F32), pltpu.VMEM((tm, LANES), F32), pltpu.SMEM((1,), jnp.int32),
                        pltpu.VMEM((8, tm), F32), pltpu.VMEM((tm, 2 * LANES), BF16)],
        compiler_params=_params(("parallel", "arbitrary")), name="moe")(h2, route, x1, tri, wg, wu, wd)


def _prep_weights(p):
    w_in = p['w_in'].astype(BF16)
    o = 0
    wqkv = w_in[:, 0:3 * FOX_WIDTH]
    o = 3 * FOX_WIDTH
    wf = jnp.pad(w_in[:, o:o + N_FOX_HEADS], ((0, 0), (0, LANES - N_FOX_HEADS)))
    o += N_FOX_HEADS
    wqm = w_in[:, o:o + MEM_WIDTH]
    o += MEM_WIDTH
    wu = w_in[:, o:o + SSM_WIDTH]
    o += SSM_WIDTH
    wg = w_in[:, o:o + 3 * D_MODEL]
    r = jnp.arange(FOX_WIDTH) // FOX_HEAD_DIM
    bd = (r[:, None] == r[None, :]).astype(BF16)
    w_router = jnp.concatenate(
        [p['w_router_expert'], p['w_router_group'],
         jnp.zeros((D_MODEL, LANES - N_EXPERTS - N_EXPERT_GROUPS), F32)], axis=1)
    w_router = jnp.concatenate(_split_bf16(w_router), axis=1)
    return dict(
        norm_mix=p['norm_mix'].reshape(1, D_MODEL), wqkv=wqkv, wf=wf,
        bf=jnp.pad(p['b_forget'], (0, LANES - N_FOX_HEADS)).reshape(1, LANES),
        wqm=wqm, wu=wu, wg=wg,
        qn_fox=jnp.tile(p['qn_fox'], N_FOX_HEADS).reshape(1, FOX_WIDTH),
        kn_fox=jnp.tile(p['kn_fox'], N_FOX_HEADS).reshape(1, FOX_WIDTH),
        qn_mem=p['qn_mem'].reshape(1, MEM_HEAD_DIM), bd=bd,
        w_glu=p['w_glu'].astype(BF16), w_br_fox=p['w_br_fox'].astype(BF16),
        w_br_ssm=p['w_br_ssm'].astype(BF16), w_br_mem=p['w_br_mem'].astype(BF16),
        w_out=p['w_out'].astype(BF16), norm_ffn=p['norm_ffn'].reshape(1, D_MODEL), w_router=w_router,
        moe_wg=p['moe_w_gate'].astype(BF16), moe_wu=p['moe_w_up'].astype(BF16),
        moe_wd=p['moe_w_down'].astype(BF16))


def _pick_tile(n, target):
    t = min(n, target)
    while n % t:
        t //= 2
    return t


def _group(x, w, mats, mem_k, mem_v, h0_re, h0_im, cache):
    b, s, _ = x.shape
    n = b * s
    x2d = x.reshape(n, D_MODEL)
    prompt = cache is None
    q, kb, vb, k_out, v_out, lf_t, q_m, u, gates = _inproj(x2d, w, _pick_tile(s if prompt else n, TILE_INPROJ), s, prompt)
    q3 = q.reshape(b, s, FOX_WIDTH)
    k3 = kb.reshape(b, s, FOX_WIDTH)
    v3 = vb.reshape(b, s, FOX_WIDTH)
    lf_rows = lf_t.reshape(N_FOX_HEADS, b, s).transpose(1, 0, 2)
    lf3 = lf_rows.transpose(0, 2, 1)
    npair = N_FOX_HEADS // 2
    if prompt:
        c_row = LOG2E * _cumsum_rows(lf_rows.reshape(b * N_FOX_HEADS, s)).reshape(b, npair, 2, s)
        o_fox = _fox_prompt(q3, k3, v3, c_row, _pick_tile(s, TILE_FOX_Q), _pick_tile(s, TILE_FOX_K))
        unt = lambda a: a.reshape(b, N_FOX_HEADS, FOX_HEAD_DIM, s).transpose(0, 3, 1, 2)
        k4, v4 = unt(k_out), unt(v_out)
    else:
        cache_k, cache_v, cache_logf = cache
        past = cache_k.shape[1]
        lf_all = jnp.concatenate([cache_logf.astype(F32).transpose(0, 2, 1), lf_rows], axis=2)
        c_row = LOG2E * _cumsum_rows(lf_all.reshape(b * N_FOX_HEADS, past + s)).reshape(b, N_FOX_HEADS, past + s)
        o_fox = _fox_sample(q3, cache_k.transpose(0, 2, 3, 1), cache_v.transpose(0, 2, 3, 1), k3, v3,
                            c_row[:, :, past:].transpose(0, 2, 1), c_row[:, :, :past], c_row[:, :, past:],
                            _pick_tile(past, TILE_SAMPLE_K))
        k4 = k_out.reshape(b, s, N_FOX_HEADS, FOX_HEAD_DIM)
        v4 = v_out.reshape(b, s, N_FOX_HEADS, FOX_HEAD_DIM)
    ys, f_re, f_im = _ssm(u, b, h0_re, h0_im, mats)
    tm = _pick_tile(n, TILE_MERGE) if TILE_MERGE % s == 0 else _pick_tile(s, TILE_MERGE)
    x1, h2, route = _merge(x2d, o_fox.reshape(n, FOX_WIDTH), ys, q_m, gates, mem_k, mem_v, w, tm, s)
    y = _moe(h2, route, x1, w['moe_wg'], w['moe_wu'], w['moe_wd'], _pick_tile(n, TILE_MOE))
    return y.reshape(b, s, D_MODEL), k4, v4, lf3, f_re, f_im


def kernel(x_prompt, x_sample, mem_prompt, cache_fox_k, cache_fox_v, cache_fox_logf, state_ssm_re, state_ssm_im,
           cache_mem_k, cache_mem_v, norm_mix, w_in, b_forget, qn_fox, kn_fox, qn_mem, kn_mem, norm_mem, w_mem_kv,
           ssm_a_re, ssm_a_im, ssm_log_dt, ssm_b_re, ssm_b_im, ssm_c_re, ssm_c_im, ssm_d, w_glu, w_br_fox,
           w_br_ssm, w_br_mem, w_out, norm_ffn, w_router_group, w_router_expert, moe_w_gate, moe_w_up,
           moe_w_down):
    depth = norm_mix.shape[0]
    assert depth == 1
    l = 0
    p = dict(norm_mix=norm_mix[l], w_in=w_in[l], b_forget=b_forget[l], qn_fox=qn_fox[l], kn_fox=kn_fox[l],
             qn_mem=qn_mem[l], ssm_a_re=ssm_a_re[l], ssm_a_im=ssm_a_im[l], ssm_log_dt=ssm_log_dt[l],
             ssm_b_re=ssm_b_re[l], ssm_b_im=ssm_b_im[l], ssm_c_re=ssm_c_re[l], ssm_c_im=ssm_c_im[l],
             ssm_d=ssm_d[l], w_glu=w_glu[l], w_br_fox=w_br_fox[l], w_br_ssm=w_br_ssm[l], w_br_mem=w_br_mem[l],
             w_out=w_out[l], norm_ffn=norm_ffn[l], w_router_group=w_router_group[l],
             w_router_expert=w_router_expert[l], moe_w_gate=moe_w_gate[l], moe_w_up=moe_w_up[l],
             moe_w_down=moe_w_down[l])
    w = _prep_weights(p)
    mats = _ssm_mats(p)
    bp, sp, _ = x_prompt.shape
    bs, ss, _ = x_sample.shape

    mk, mv = _memkv(mem_prompt.reshape(bp * N_MEM, D_MODEL), norm_mem[l].reshape(1, D_MODEL),
                    w_mem_kv[l].astype(BF16), kn_mem[l].reshape(1, MEM_HEAD_DIM), _pick_tile(bp * N_MEM, TILE_MEMKV))
    mem_rows = lambda a, b: a.reshape(b, N_MEM * N_MEM_HEADS, MEM_HEAD_DIM)
    mk = mem_rows(mk, bp)
    mv = mem_rows(mv, bp)
    zeros = jnp.zeros((bp, N_SSM_GROUPS, SSM_STATE), F32)
    yp, pk, pv, plf, pre, pim = _group(x_prompt, w, mats, mk, mv, zeros, zeros, None)
    cache = (cache_fox_k[l], cache_fox_v[l], cache_fox_logf[l])
    ys, sk, sv, slf, sre, sim = _group(
        x_sample, w, mats, mem_rows(cache_mem_k[l], bs), mem_rows(cache_mem_v[l], bs),
        state_ssm_re[l].astype(F32), state_ssm_im[l].astype(F32), cache)
    st = lambda a: a[None]
    return (yp, ys, st(pk), st(pv), st(plf), st(pre), st(pim),
            st(mk.reshape(bp, N_MEM, N_MEM_HEADS, MEM_HEAD_DIM)), st(mv.reshape(bp, N_MEM, N_MEM_HEADS, MEM_HEAD_DIM)),
            st(sk), st(sv), st(slf), st(sre), st(sim))
```
